```python
import math
import jax
import jax.numpy as jnp
from jax import lax
import numpy as np

D_MODEL = 2048
BATCH = 8
SEQ = 2048
DEPTH = 4

N_MIXERS = 2
N_A_LAYERS = (DEPTH + 1) // 2
N_B_LAYERS = DEPTH // 2
EPS = 1e-6
NEG = -1e30

A_HEAD_DIM = 128
A_HEADS_PER_GROUP = 4
A_PATTERNS = ((128, 1), (512, 4), (2048, 16))
A_N_GROUPS = len(A_PATTERNS)
A_HEADS = A_HEADS_PER_GROUP * A_N_GROUPS
A_QKV_W = A_HEADS * A_HEAD_DIM
A_OUT_W = A_HEADS_PER_GROUP * A_HEAD_DIM
QBLK = 128

CHUNK = 128
B_GROUPS = 12
B_GROUP_W = 128
B_W = B_GROUPS * B_GROUP_W

MEM_LEN = 256
MEM_HEADS = 4
MEM_HEAD_DIM = 128
MEM_W = MEM_HEADS * MEM_HEAD_DIM

A_IN = 3 * A_QKV_W + MEM_W
A_OUT_IN = A_OUT_W + MEM_W
B_IN = 2 * B_W + MEM_W
B_OUT_IN = B_W + MEM_W

FF = 5632
CONV_W = 3

kernel_name = "hybrid_dilated_sgu_memory_encoder"


def rmsnorm(x, g):
    xf = x.astype(jnp.float32)
    y = xf * lax.rsqrt(jnp.mean(xf * xf, axis=-1, keepdims=True) + EPS)
    return (y * g.astype(jnp.float32)).astype(x.dtype)


def alibi_slopes():
    return (2.0 ** (-8.0 * (np.arange(A_HEADS) + 1) / A_HEADS)).astype(np.float32)


def dilated_window_attention(q, k, v, dilation, n_side, slopes):
    B, S, H, E = q.shape
    L = S // dilation
    nblk = -(-L // QBLK)
    Lp = nblk * QBLK
    W = QBLK + 2 * n_side
    qs = q.reshape(B, L, dilation, H, E)
    ks = k.reshape(B, L, dilation, H, E)
    vs = v.reshape(B, L, dilation, H, E)
    qb = jnp.pad(qs, ((0, 0), (0, Lp - L), (0, 0), (0, 0), (0, 0)))
    qb = qb.reshape(B, nblk, QBLK, dilation, H, E)
    pad_k = ((0, 0), (n_side, n_side + Lp - L), (0, 0), (0, 0), (0, 0))
    kp = jnp.pad(ks, pad_k)
    vp = jnp.pad(vs, pad_k)
    idx = np.arange(nblk)[:, None] * QBLK + np.arange(W)[None, :]
    kb = kp[:, idx]
    vb = vp[:, idx]
    s = jnp.einsum('bnqrhe,bnkrhe->bnrhqk', qb.astype(jnp.float32),
                   kb.astype(jnp.float32)) * (E ** -0.5)
    rel = np.arange(W)[None, :] - n_side - np.arange(QBLK)[:, None]
    band = np.abs(rel) <= n_side
    jk = np.arange(nblk)[:, None] * QBLK - n_side + np.arange(W)[None, :]
    valid = (jk >= 0) & (jk < L)
    mask = band[None, :, :] & valid[:, None, :]
    dist = (np.abs(rel) * dilation).astype(np.float32)
    alibi = -slopes[:, None, None] * dist[None]
    s = s + alibi[None, None, None]
    s = jnp.where(mask[None, :, None, None], s, NEG)
    lse = jax.nn.logsumexp(s, axis=-1)
    p = jnp.exp(s - lse[..., None])
    o = jnp.einsum('bnrhqk,bnkrhe->bnqrhe', p, vb.astype(jnp.float32))
    o = o.reshape(B, Lp, dilation, H, E)[:, :L].reshape(B, S, H, E)
    lse = lse.transpose(0, 1, 4, 2, 3).reshape(B, Lp, dilation, H)[:, :L].reshape(B, S, H)
    return o, lse


def mixer_a(proj):
    B, S, _ = proj.shape
    qkv = proj.reshape(B, S, 3, A_N_GROUPS, A_HEADS_PER_GROUP, A_HEAD_DIM)
    slopes_all = jnp.asarray(alibi_slopes())
    outs, lses = [], []
    for g, (window, dilation) in enumerate(A_PATTERNS):
        n_side = (window // 2) // dilation
        sl = slopes_all[g * A_HEADS_PER_GROUP:(g + 1) * A_HEADS_PER_GROUP]
        o, l = dilated_window_attention(qkv[:, :, 0, g], qkv[:, :, 1, g],
                                        qkv[:, :, 2, g], dilation, n_side, sl)
        outs.append(o)
        lses.append(l)
    outs = jnp.stack(outs)
    wts = jax.nn.softmax(jnp.stack(lses), axis=0)
    comb = jnp.sum(wts[..., None] * outs, axis=0)
    return comb.reshape(B, S, A_OUT_W).astype(proj.dtype)


def mixer_b(proj_uv, v_norm_g, w_s, s_bias):
    B, S, _ = proj_uv.shape
    uv = jax.nn.gelu(proj_uv, approximate=False)
    u, v = uv[..., :B_W], uv[..., B_W:]
    v = rmsnorm(v, v_norm_g)
    vc = v.reshape(B, S // CHUNK, CHUNK, B_GROUPS, B_GROUP_W)
    mixed = jnp.einsum('gpq,bcqge->bcpge', w_s, vc) + s_bias.T[None, None, :, :, None]
    return u * mixed.reshape(B, S, B_W)


def memory_cross_attention(qm, mem_n, w_kv):
    B, S, _ = qm.shape
    M = mem_n.shape[1]
    kv = jnp.einsum('bmd,df->bmf', mem_n, w_kv).reshape(B, M, 2, MEM_HEADS, MEM_HEAD_DIM)
    q = qm.reshape(B, S, MEM_HEADS, MEM_HEAD_DIM).astype(jnp.float32)
    k = kv[:, :, 0].astype(jnp.float32)
    v = kv[:, :, 1].astype(jnp.float32)
    s = jnp.einsum('bshe,bmhe->bhsm', q, k) * (MEM_HEAD_DIM ** -0.5)
    p = jax.nn.softmax(s, axis=-1)
    o = jnp.einsum('bhsm,bmhe->bshe', p, v)
    return o.reshape(B, S, MEM_W).astype(qm.dtype)


def conv_ffn(h, w_up, conv_w, conv_b, w_down):
    a = jnp.einsum('bsd,df->bsf', h, w_up)
    ap = jnp.pad(a, ((0, 0), (1, 1), (0, 0)))
    a = ap[:, :-2] * conv_w[0] + ap[:, 1:-1] * conv_w[1] + ap[:, 2:] * conv_w[2] + conv_b
    gate, val = a[..., :FF], a[..., FF:]
    return jnp.einsum('bsf,fd->bsd', jax.nn.gelu(gate, approximate=False) * val, w_down)


def _fwd_setup_inputs(seed: int = 0) -> dict:
    key = jax.random.key(seed)
    ks = jax.random.split(key, 20)

    def nrm(k, shape, scale):
        return jax.random.normal(k, shape, jnp.float32) * scale

    D = D_MODEL
    return {
        "x": nrm(ks[0], (BATCH, SEQ, D), 1.0),
        "mem": nrm(ks[1], (BATCH, MEM_LEN, D), 1.0),
        "mix_norm_g": 1.0 + nrm(ks[2], (DEPTH, D), 0.02),
        "ffn_norm_g": 1.0 + nrm(ks[3], (DEPTH, D), 0.02),
        "mem_norm_g": 1.0 + nrm(ks[4], (DEPTH, D), 0.02),
        "w_mem_kv": nrm(ks[5], (DEPTH, D, 2 * MEM_W), D ** -0.5),
        "a_w_in": nrm(ks[6], (N_A_LAYERS, D, A_IN), D ** -0.5),
        "a_w_out": nrm(ks[7], (N_A_LAYERS, A_OUT_IN, D), A_OUT_IN ** -0.5),
        "b_w_in": nrm(ks[8], (N_B_LAYERS, D, B_IN), D ** -0.5),
        "b_v_norm_g": 1.0 + nrm(ks[9], (N_B_LAYERS, B_W), 0.02),
        "b_w_s": nrm(ks[10], (N_B_LAYERS, B_GROUPS, CHUNK, CHUNK), CHUNK ** -0.5),
        "b_s_bias": 1.0 + nrm(ks[11], (N_B_LAYERS, B_GROUPS, CHUNK), 0.02),
        "b_w_out": nrm(ks[12], (N_B_LAYERS, B_OUT_IN, D), B_OUT_IN ** -0.5),
        "ffn_w_up": nrm(ks[13], (DEPTH, D, 2 * FF), D ** -0.5),
        "ffn_conv_w": nrm(ks[14], (DEPTH, CONV_W, 2 * FF), CONV_W ** -0.5),
        "ffn_conv_b": nrm(ks[15], (DEPTH, 2 * FF), 0.02),
        "ffn_w_down": nrm(ks[16], (DEPTH, FF, D), FF ** -0.5),
        "final_norm_g": 1.0 + nrm(ks[17], (D,), 0.02),
    }


def _fwd_reference(x, mem, mix_norm_g, ffn_norm_g, mem_norm_g, w_mem_kv, a_w_in, a_w_out,
              b_w_in, b_v_norm_g, b_w_s, b_s_bias, b_w_out, ffn_w_up, ffn_conv_w,
              ffn_conv_b, ffn_w_down, final_norm_g):
    for i in range(DEPTH):
        h = rmsnorm(x, mix_norm_g[i])
        mem_n = rmsnorm(mem, mem_norm_g[i])
        j = i // N_MIXERS
        if i % N_MIXERS == 0:
            proj = jnp.einsum('bsd,df->bsf', h, a_w_in[j])
            tok = mixer_a(proj[..., :3 * A_QKV_W])
            mem_out = memory_cross_attention(proj[..., 3 * A_QKV_W:], mem_n, w_mem_kv[i])
            w_out = a_w_out[j]
        else:
            proj = jnp.einsum('bsd,df->bsf', h, b_w_in[j])
            tok = mixer_b(proj[..., :2 * B_W], b_v_norm_g[j], b_w_s[j], b_s_bias[j])
            mem_out = memory_cross_attention(proj[..., 2 * B_W:], mem_n, w_mem_kv[i])
            w_out = b_w_out[j]
        cat = jnp.concatenate([tok, mem_out], axis=-1)
        x = x + jnp.einsum('bsf,fd->bsd', cat, w_out)
        h = rmsnorm(x, ffn_norm_g[i])
        x = x + conv_ffn(h, ffn_w_up[i], ffn_conv_w[i], ffn_conv_b[i], ffn_w_down[i])
    return rmsnorm(x, final_norm_g)


import jax as _jax
import jax.numpy as _jnp

TWIN_FORMAT = 'train_step'
FWD_PARAMS = ['x', 'mem', 'mix_norm_g', 'ffn_norm_g', 'mem_norm_g', 'w_mem_kv', 'a_w_in', 'a_w_out', 'b_w_in', 'b_v_norm_g', 'b_w_s', 'b_s_bias', 'b_w_out', 'ffn_w_up', 'ffn_conv_w', 'ffn_conv_b', 'ffn_w_down', 'final_norm_g']
TWIN_WEIGHTS = ['mix_norm_g', 'ffn_norm_g', 'mem_norm_g', 'w_mem_kv', 'a_w_in', 'a_w_out', 'b_w_in', 'b_v_norm_g', 'b_w_s', 'b_s_bias', 'b_w_out', 'ffn_w_up', 'ffn_conv_w', 'ffn_conv_b', 'ffn_w_down', 'final_norm_g']
TWIN_DIFF_INPUT = 'x'
TWIN_INPUTS = ['x', 'mem', 'mix_norm_g', 'ffn_norm_g', 'mem_norm_g', 'w_mem_kv', 'a_w_in', 'a_w_out', 'b_w_in', 'b_v_norm_g', 'b_w_s', 'b_s_bias', 'b_w_out', 'ffn_w_up', 'ffn_conv_w', 'ffn_conv_b', 'ffn_w_down', 'final_norm_g', 'loss_target', 'm_mix_norm_g', 'm_ffn_norm_g', 'm_mem_norm_g', 'm_w_mem_kv', 'm_a_w_in', 'm_a_w_out', 'm_b_w_in', 'm_b_v_norm_g', 'm_b_w_s', 'm_b_s_bias', 'm_b_w_out', 'm_ffn_w_up', 'm_ffn_conv_w', 'm_ffn_conv_b', 'm_ffn_w_down', 'm_final_norm_g', 'v_mix_norm_g', 'v_ffn_norm_g', 'v_mem_norm_g', 'v_w_mem_kv', 'v_a_w_in', 'v_a_w_out', 'v_b_w_in', 'v_b_v_norm_g', 'v_b_w_s', 'v_b_s_bias', 'v_b_w_out', 'v_ffn_w_up', 'v_ffn_conv_w', 'v_ffn_conv_b', 'v_ffn_w_down', 'v_final_norm_g']
TWIN_OUTPUTS = ['loss', 'grad_x', 'grad_mix_norm_g', 'grad_ffn_norm_g', 'grad_mem_norm_g', 'grad_w_mem_kv', 'grad_a_w_in', 'grad_a_w_out', 'grad_b_w_in', 'grad_b_v_norm_g', 'grad_b_w_s', 'grad_b_s_bias', 'grad_b_w_out', 'grad_ffn_w_up', 'grad_ffn_conv_w', 'grad_ffn_conv_b', 'grad_ffn_w_down', 'grad_final_norm_g', 'delta_mix_norm_g', 'delta_ffn_norm_g', 'delta_mem_norm_g', 'delta_w_mem_kv', 'delta_a_w_in', 'delta_a_w_out', 'delta_b_w_in', 'delta_b_v_norm_g', 'delta_b_w_s', 'delta_b_s_bias', 'delta_b_w_out', 'delta_ffn_w_up', 'delta_ffn_conv_w', 'delta_ffn_conv_b', 'delta_ffn_w_down', 'delta_final_norm_g', 'new_m_mix_norm_g', 'new_m_ffn_norm_g', 'new_m_mem_norm_g', 'new_m_w_mem_kv', 'new_m_a_w_in', 'new_m_a_w_out', 'new_m_b_w_in', 'new_m_b_v_norm_g', 'new_m_b_w_s', 'new_m_b_s_bias', 'new_m_b_w_out', 'new_m_ffn_w_up', 'new_m_ffn_conv_w', 'new_m_ffn_conv_b', 'new_m_ffn_w_down', 'new_m_final_norm_g', 'new_v_mix_norm_g', 'new_v_ffn_norm_g', 'new_v_mem_norm_g', 'new_v_w_mem_kv', 'new_v_a_w_in', 'new_v_a_w_out', 'new_v_b_w_in', 'new_v_b_v_norm_g', 'new_v_b_w_s', 'new_v_b_s_bias', 'new_v_b_w_out', 'new_v_ffn_w_up', 'new_v_ffn_conv_w', 'new_v_ffn_conv_b', 'new_v_ffn_w_down', 'new_v_final_norm_g']
TWIN_LEAF_KINDS = {'loss': 'loss', 'grad_x': 'grad_x', 'grad_mix_norm_g': 'grad_w', 'grad_ffn_norm_g': 'grad_w', 'grad_mem_norm_g': 'grad_w', 'grad_w_mem_kv': 'grad_w', 'grad_a_w_in': 'grad_w', 'grad_a_w_out': 'grad_w', 'grad_b_w_in': 'grad_w', 'grad_b_v_norm_g': 'grad_w', 'grad_b_w_s': 'grad_w', 'grad_b_s_bias': 'grad_w', 'grad_b_w_out': 'grad_w', 'grad_ffn_w_up': 'grad_w', 'grad_ffn_conv_w': 'grad_w', 'grad_ffn_conv_b': 'grad_w', 'grad_ffn_w_down': 'grad_w', 'grad_final_norm_g': 'grad_w', 'delta_mix_norm_g': 'delta_w', 'delta_ffn_norm_g': 'delta_w', 'delta_mem_norm_g': 'delta_w', 'delta_w_mem_kv': 'delta_w', 'delta_a_w_in': 'delta_w', 'delta_a_w_out': 'delta_w', 'delta_b_w_in': 'delta_w', 'delta_b_v_norm_g': 'delta_w', 'delta_b_w_s': 'delta_w', 'delta_b_s_bias': 'delta_w', 'delta_b_w_out': 'delta_w', 'delta_ffn_w_up': 'delta_w', 'delta_ffn_conv_w': 'delta_w', 'delta_ffn_conv_b': 'delta_w', 'delta_ffn_w_down': 'delta_w', 'delta_final_norm_g': 'delta_w', 'new_m_mix_norm_g': 'new_m', 'new_m_ffn_norm_g': 'new_m', 'new_m_mem_norm_g': 'new_m', 'new_m_w_mem_kv': 'new_m', 'new_m_a_w_in': 'new_m', 'new_m_a_w_out': 'new_m', 'new_m_b_w_in': 'new_m', 'new_m_b_v_norm_g': 'new_m', 'new_m_b_w_s': 'new_m', 'new_m_b_s_bias': 'new_m', 'new_m_b_w_out': 'new_m', 'new_m_ffn_w_up': 'new_m', 'new_m_ffn_conv_w': 'new_m', 'new_m_ffn_conv_b': 'new_m', 'new_m_ffn_w_down': 'new_m', 'new_m_final_norm_g': 'new_m', 'new_v_mix_norm_g': 'new_v', 'new_v_ffn_norm_g': 'new_v', 'new_v_mem_norm_g': 'new_v', 'new_v_w_mem_kv': 'new_v', 'new_v_a_w_in': 'new_v', 'new_v_a_w_out': 'new_v', 'new_v_b_w_in': 'new_v', 'new_v_b_v_norm_g': 'new_v', 'new_v_b_w_s': 'new_v', 'new_v_b_s_bias': 'new_v', 'new_v_b_w_out': 'new_v', 'new_v_ffn_w_up': 'new_v', 'new_v_ffn_conv_w': 'new_v', 'new_v_ffn_conv_b': 'new_v', 'new_v_ffn_w_down': 'new_v', 'new_v_final_norm_g': 'new_v'}


def _forward(args):
    return _fwd_reference(*[args[k] for k in FWD_PARAMS])


def _output_shape():
    out = _jax.eval_shape(lambda: _forward(_fwd_setup_inputs(0)))
    return out.shape, out.dtype

N_MICROBATCH = 1
ADAM_LR = 0.001
ADAM_B1 = 0.9
ADAM_B2 = 0.999
ADAM_EPS = 1e-08
ADAM_WD = 0.01
ADAM_STEP = 10
PER_EXAMPLE_BATCH_AXIS = {'x': 0, 'mem': 0, 'loss_target': 0}
SHARED_INPUTS = []
_WEIGHT_DTYPES = {'mix_norm_g': _jnp.float32, 'ffn_norm_g': _jnp.float32, 'mem_norm_g': _jnp.float32, 'w_mem_kv': _jnp.float32, 'a_w_in': _jnp.float32, 'a_w_out': _jnp.float32, 'b_w_in': _jnp.float32, 'b_v_norm_g': _jnp.float32, 'b_w_s': _jnp.float32, 'b_s_bias': _jnp.float32, 'b_w_out': _jnp.float32, 'ffn_w_up': _jnp.float32, 'ffn_conv_w': _jnp.float32, 'ffn_conv_b': _jnp.float32, 'ffn_w_down': _jnp.float32, 'final_norm_g': _jnp.float32}
MOMENT_SCALE = {'mix_norm_g': 3.731425e-02, 'ffn_norm_g': 4.808294e-02, 'mem_norm_g': 6.416110e-03, 'w_mem_kv': 8.964001e-03, 'a_w_in': 1.416011e-02, 'a_w_out': 1.471830e-02, 'b_w_in': 3.595423e-02, 'b_v_norm_g': 3.112166e-02, 'b_w_s': 3.045105e-02, 'b_s_bias': 3.045764e-02, 'b_w_out': 3.777253e-02, 'ffn_w_up': 2.046791e-02, 'ffn_conv_w': 2.050190e-02, 'ffn_conv_b': 1.997073e-02, 'ffn_w_down': 3.340815e-02, 'final_norm_g': 8.013631e+00}


def _to_microbatches(a, axis):
    t = _jnp.moveaxis(a, axis, 0)
    t = t.reshape((N_MICROBATCH, t.shape[0] // N_MICROBATCH) + t.shape[1:])
    return _jnp.moveaxis(t, 1, axis + 1)


def setup_inputs(seed: int = 0) -> dict:
    inp = _fwd_setup_inputs(seed)
    key = _jax.random.fold_in(_jax.random.key(seed), 7919)
    shape, _ = _output_shape()
    out = dict(inp)
    out["loss_target"] = _jax.random.normal(_jax.random.fold_in(key, 0), shape, _jnp.float32)
    for i, name in enumerate(TWIN_WEIGHTS):
        w = inp[name].astype(_jnp.float32)
        if MOMENT_SCALE is None:
            s = _jnp.sqrt(_jnp.mean(_jnp.square(w)) + 1e-30)
        else:
            s = MOMENT_SCALE[name]
        km, kv = _jax.random.split(_jax.random.fold_in(key, i + 1))
        out[name] = w
        out["m_" + name] = s * _jax.random.normal(km, w.shape, _jnp.float32)
        out["v_" + name] = (s * s) * _jax.random.uniform(kv, w.shape, _jnp.float32, 0.5, 1.5)
    if N_MICROBATCH > 1:
        for name, axis in PER_EXAMPLE_BATCH_AXIS.items():
            out[name] = _to_microbatches(out[name], axis)
    return {'x': out['x'], 'mem': out['mem'], 'mix_norm_g': out['mix_norm_g'], 'ffn_norm_g': out['ffn_norm_g'], 'mem_norm_g': out['mem_norm_g'], 'w_mem_kv': out['w_mem_kv'], 'a_w_in': out['a_w_in'], 'a_w_out': out['a_w_out'], 'b_w_in': out['b_w_in'], 'b_v_norm_g': out['b_v_norm_g'], 'b_w_s': out['b_w_s'], 'b_s_bias': out['b_s_bias'], 'b_w_out': out['b_w_out'], 'ffn_w_up': out['ffn_w_up'], 'ffn_conv_w': out['ffn_conv_w'], 'ffn_conv_b': out['ffn_conv_b'], 'ffn_w_down': out['ffn_w_down'], 'final_norm_g': out['final_norm_g'], 'loss_target': out['loss_target'], 'm_mix_norm_g': out['m_mix_norm_g'], 'm_ffn_norm_g': out['m_ffn_norm_g'], 'm_mem_norm_g': out['m_mem_norm_g'], 'm_w_mem_kv': out['m_w_mem_kv'], 'm_a_w_in': out['m_a_w_in'], 'm_a_w_out': out['m_a_w_out'], 'm_b_w_in': out['m_b_w_in'], 'm_b_v_norm_g': out['m_b_v_norm_g'], 'm_b_w_s': out['m_b_w_s'], 'm_b_s_bias': out['m_b_s_bias'], 'm_b_w_out': out['m_b_w_out'], 'm_ffn_w_up': out['m_ffn_w_up'], 'm_ffn_conv_w': out['m_ffn_conv_w'], 'm_ffn_conv_b': out['m_ffn_conv_b'], 'm_ffn_w_down': out['m_ffn_w_down'], 'm_final_norm_g': out['m_final_norm_g'], 'v_mix_norm_g': out['v_mix_norm_g'], 'v_ffn_norm_g': out['v_ffn_norm_g'], 'v_mem_norm_g': out['v_mem_norm_g'], 'v_w_mem_kv': out['v_w_mem_kv'], 'v_a_w_in': out['v_a_w_in'], 'v_a_w_out': out['v_a_w_out'], 'v_b_w_in': out['v_b_w_in'], 'v_b_v_norm_g': out['v_b_v_norm_g'], 'v_b_w_s': out['v_b_w_s'], 'v_b_s_bias': out['v_b_s_bias'], 'v_b_w_out': out['v_b_w_out'], 'v_ffn_w_up': out['v_ffn_w_up'], 'v_ffn_conv_w': out['v_ffn_conv_w'], 'v_ffn_conv_b': out['v_ffn_conv_b'], 'v_ffn_w_down': out['v_ffn_w_down'], 'v_final_norm_g': out['v_final_norm_g']}


def _loss(weights, diff, rest, loss_target):
    with _jax.named_scope("forward"):
        args = {**rest, TWIN_DIFF_INPUT: diff, **{k: w.astype(_WEIGHT_DTYPES[k]) for k, w in weights.items()}}
        y = _forward(args)
    with _jax.named_scope("loss_head"):
        err = _jnp.square(y.astype(_jnp.float32) - loss_target)
        return 0.5 * _jnp.sum(_jnp.mean(err, axis=-1)) if err.ndim else 0.5 * err


def _adamw(w, g, m, v):
    m = ADAM_B1 * m + (1.0 - ADAM_B1) * g
    v = ADAM_B2 * v + (1.0 - ADAM_B2) * _jnp.square(g)
    m_hat = m / (1.0 - ADAM_B1 ** ADAM_STEP)
    v_hat = v / (1.0 - ADAM_B2 ** ADAM_STEP)
    delta = -ADAM_LR * (m_hat / (_jnp.sqrt(v_hat) + ADAM_EPS) + ADAM_WD * w)
    return delta, m, v


def reference(x, mem, mix_norm_g, ffn_norm_g, mem_norm_g, w_mem_kv, a_w_in, a_w_out, b_w_in, b_v_norm_g, b_w_s, b_s_bias, b_w_out, ffn_w_up, ffn_conv_w, ffn_conv_b, ffn_w_down, final_norm_g, loss_target, m_mix_norm_g, m_ffn_norm_g, m_mem_norm_g, m_w_mem_kv, m_a_w_in, m_a_w_out, m_b_w_in, m_b_v_norm_g, m_b_w_s, m_b_s_bias, m_b_w_out, m_ffn_w_up, m_ffn_conv_w, m_ffn_conv_b, m_ffn_w_down, m_final_norm_g, v_mix_norm_g, v_ffn_norm_g, v_mem_norm_g, v_w_mem_kv, v_a_w_in, v_a_w_out, v_b_w_in, v_b_v_norm_g, v_b_w_s, v_b_s_bias, v_b_w_out, v_ffn_w_up, v_ffn_conv_w, v_ffn_conv_b, v_ffn_w_down, v_final_norm_g):
    given = dict(x=x, mem=mem, mix_norm_g=mix_norm_g, ffn_norm_g=ffn_norm_g, mem_norm_g=mem_norm_g, w_mem_kv=w_mem_kv, a_w_in=a_w_in, a_w_out=a_w_out, b_w_in=b_w_in, b_v_norm_g=b_v_norm_g, b_w_s=b_w_s, b_s_bias=b_s_bias, b_w_out=b_w_out, ffn_w_up=ffn_w_up, ffn_conv_w=ffn_conv_w, ffn_conv_b=ffn_conv_b, ffn_w_down=ffn_w_down, final_norm_g=final_norm_g, loss_target=loss_target, m_mix_norm_g=m_mix_norm_g, m_ffn_norm_g=m_ffn_norm_g, m_mem_norm_g=m_mem_norm_g, m_w_mem_kv=m_w_mem_kv, m_a_w_in=m_a_w_in, m_a_w_out=m_a_w_out, m_b_w_in=m_b_w_in, m_b_v_norm_g=m_b_v_norm_g, m_b_w_s=m_b_w_s, m_b_s_bias=m_b_s_bias, m_b_w_out=m_b_w_out, m_ffn_w_up=m_ffn_w_up, m_ffn_conv_w=m_ffn_conv_w, m_ffn_conv_b=m_ffn_conv_b, m_ffn_w_down=m_ffn_w_down, m_final_norm_g=m_final_norm_g, v_mix_norm_g=v_mix_norm_g, v_ffn_norm_g=v_ffn_norm_g, v_mem_norm_g=v_mem_norm_g, v_w_mem_kv=v_w_mem_kv, v_a_w_in=v_a_w_in, v_a_w_out=v_a_w_out, v_b_w_in=v_b_w_in, v_b_v_norm_g=v_b_v_norm_g, v_b_w_s=v_b_w_s, v_b_s_bias=v_b_s_bias, v_b_w_out=v_b_w_out, v_ffn_w_up=v_ffn_w_up, v_ffn_conv_w=v_ffn_conv_w, v_ffn_conv_b=v_ffn_conv_b, v_ffn_w_down=v_ffn_w_down, v_final_norm_g=v_final_norm_g)
    weights = {n: given[n] for n in TWIN_WEIGHTS}
    shared = {n: given[n] for n in SHARED_INPUTS}
    per_example = {n: given[n] for n in ['x', 'mem']}
    grad_fn = _jax.value_and_grad(_loss, argnums=(0, 1))

    def one_microbatch(ex, loss_target):
        ex = dict(ex)
        diff = ex.pop(TWIN_DIFF_INPUT)
        return grad_fn(weights, diff, {**shared, **ex}, loss_target)

    if N_MICROBATCH == 1:
        loss, (grad_w, grad_x) = one_microbatch(per_example, given["loss_target"])
    else:
        def body(carry, xs):
            loss_sum, grad_sum = carry
            l_k, (gw_k, gx_k) = one_microbatch(xs[0], xs[1])
            with _jax.named_scope("update"):
                return (loss_sum + l_k, _jax.tree.map(_jnp.add, grad_sum, gw_k)), gx_k

        init = (_jnp.zeros((), _jnp.float32), _jax.tree.map(_jnp.zeros_like, weights))
        (loss, grad_w), grad_x = _jax.lax.scan(body, init, (per_example, given["loss_target"]))
    with _jax.named_scope("update"):
        delta_w, new_m, new_v = {}, {}, {}
        for n in TWIN_WEIGHTS:
            delta_w[n], new_m[n], new_v[n] = _adamw(weights[n], grad_w[n], given["m_" + n], given["v_" + n])
    return (loss, grad_x, *[grad_w[n] for n in TWIN_WEIGHTS], *[delta_w[n] for n in TWIN_WEIGHTS],
            *[new_m[n] for n in TWIN_WEIGHTS], *[new_v[n] for n in TWIN_WEIGHTS])
```

```python
import functools

import numpy as np
import jax
import jax.numpy as jnp
from jax import lax
from jax.experimental import pallas as pl
from jax.experimental.pallas import tpu as pltpu

F32 = jnp.float32
BF16 = jnp.bfloat16
MESH = pl.DeviceIdType.MESH
AXES = ("x", "y", "c")
N_DEV = 8

EPS = 1e-6
NEG = -1e30
HEAD = 128
HPG = 4
GW = HPG * HEAD
A_PATTERNS = ((128, 1), (512, 4), (2048, 16))
A_HEADS = HPG * len(A_PATTERNS)
QBLK = 128
B_GROUPS = 12
B_W = B_GROUPS * HEAD
SLOPES = (2.0 ** (-8.0 * (np.arange(A_HEADS) + 1) / A_HEADS)).astype(np.float32)
SCALE = HEAD ** -0.5

ADAM_LR = 0.001
ADAM_B1 = 0.9
ADAM_B2 = 0.999
ADAM_EPS = 1e-08
ADAM_WD = 0.01
ADAM_STEP = 10

V7X_VMEM_LIMIT = 50 * 1024 * 1024

NN = (((1,), (0,)), ((), ()))
NT = (((1,), (1,)), ((), ()))
TN = (((0,), (0,)), ((), ()))


def _cp(*sem):
    return pltpu.CompilerParams(dimension_semantics=sem, vmem_limit_bytes=V7X_VMEM_LIMIT)


def _dot(a, b, dims=NN):
    return lax.dot_general(a, b, dims, preferred_element_type=F32)


def _tile(n, pref):
    t = min(n, pref)
    assert n % t == 0, (n, pref)
    return t


def _row_tile(rows, cols):
    best = None
    for t in range(16, rows + 1, 16):
        if rows % t == 0 and t * cols * 4 <= (1 << 20):
            best = t
    if best is None:
        best = rows
    return best


def _matmul(name, dims, grid, a, a_spec, b, b_spec, out_shape, o_spec, tile, res=None, res_spec=None):
    nk = grid[2]
    has_res = res is not None

    def body(*refs):
        a_ref, b_ref = refs[0], refs[1]
        r_ref = refs[2] if has_res else None
        o_ref, acc_ref = refs[-2], refs[-1]
        part = _dot(a_ref[...].astype(BF16), b_ref[...].astype(BF16), dims)

        def finish(val):
            if has_res:
                val = val + r_ref[...]
            o_ref[...] = val.astype(o_ref.dtype)

        if nk == 1:
            finish(part)
        else:
            k = pl.program_id(2)

            @pl.when(k == 0)
            def _():
                acc_ref[...] = part

            @pl.when(k > 0)
            def _():
                acc_ref[...] += part

            @pl.when(k == nk - 1)
            def _():
                finish(acc_ref[...])

    ins = [a, b] + ([res] if has_res else [])
    specs = [a_spec, b_spec] + ([res_spec] if has_res else [])
    return pl.pallas_call(
        body, name=name, grid=grid, in_specs=specs, out_specs=o_spec, out_shape=out_shape,
        scratch_shapes=[pltpu.VMEM(tile if nk > 1 else (8, 128), F32)],
        compiler_params=_cp("parallel", "parallel", "arbitrary"))(*ins)


def _mm_full(name, a, w, res=None, tm=1024, tn=512, tk=2048):
    M, K = a.shape
    N = w.shape[1]
    tm, tn, tk = _tile(M, tm), _tile(N, tn), _tile(K, tk)
    return _matmul(
        name, NN, (N // tn, M // tm, K // tk),
        a, pl.BlockSpec((tm, tk), lambda j, i, k: (i, k)),
        w, pl.BlockSpec((tk, tn), lambda j, i, k: (k, j)),
        jax.ShapeDtypeStruct((M, N), F32), pl.BlockSpec((tm, tn), lambda j, i, k: (i, j)), (tm, tn),
        res, pl.BlockSpec((tm, tn), lambda j, i, k: (i, j)))


def _mm_gcols(name, a, wg, res=None, split_out=False, tm=1024):
    M, K = a.shape
    G, _, Nl = wg.shape
    tm = _tile(M, tm)
    hg = G // 2
    if split_out:
        shape = jax.ShapeDtypeStruct((2, M, hg * Nl), F32)
        o_spec = pl.BlockSpec((None, tm, Nl), lambda g, i, k: (g // hg, i, g % hg))
    else:
        shape = jax.ShapeDtypeStruct((M, G * Nl), F32)
        o_spec = pl.BlockSpec((tm, Nl), lambda g, i, k: (i, g))
    return _matmul(
        name, NN, (G, M // tm, 1),
        a, pl.BlockSpec((tm, K), lambda g, i, k: (i, 0)),
        wg, pl.BlockSpec((None, K, Nl), lambda g, i, k: (g, 0, 0)),
        shape, o_spec, (tm, Nl),
        res, pl.BlockSpec((tm, Nl), lambda g, i, k: (i, g)))


def _mm_dx_full(name, dy, w, tm=512, tko=512, tc=2048):
    M, N = dy.shape
    K = w.shape[0]
    tm, tko, tc = _tile(M, tm), _tile(K, tko), _tile(N, tc)
    return _matmul(
        name, NT, (K // tko, M // tm, N // tc),
        dy, pl.BlockSpec((tm, tc), lambda j, i, k: (i, k)),
        w, pl.BlockSpec((tko, tc), lambda j, i, k: (j, k)),
        jax.ShapeDtypeStruct((M, K), F32), pl.BlockSpec((tm, tko), lambda j, i, k: (i, j)), (tm, tko))


def _mm_dx_gcols(name, dy, wg, split_in=False, tm=1024, tko=1024):
    G, K, Nl = wg.shape
    M = dy.shape[-2]
    tm, tko = _tile(M, tm), _tile(K, tko)
    hg = G // 2
    if split_in:
        dy_spec = pl.BlockSpec((None, tm, Nl), lambda j, i, g: (g // hg, i, g % hg))
    else:
        dy_spec = pl.BlockSpec((tm, Nl), lambda j, i, g: (i, g))
    return _matmul(
        name, NT, (K // tko, M // tm, G),
        dy, dy_spec,
        wg, pl.BlockSpec((None, tko, Nl), lambda j, i, g: (g, j, 0)),
        jax.ShapeDtypeStruct((M, K), F32), pl.BlockSpec((tm, tko), lambda j, i, g: (i, j)), (tm, tko))


def _mm_dw(name, a, dy, tko=512, tn=1024, ts=2048):
    S, K1 = a.shape
    N = dy.shape[1]
    tko, tn, ts = _tile(K1, tko), _tile(N, tn), _tile(S, ts)
    return _matmul(
        name, TN, (N // tn, K1 // tko, S // ts),
        a, pl.BlockSpec((ts, tko), lambda i, j, k: (k, j)),
        dy, pl.BlockSpec((ts, tn), lambda i, j, k: (k, i)),
        jax.ShapeDtypeStruct((K1, N), F32), pl.BlockSpec((tko, tn), lambda i, j, k: (j, i)), (tko, tn))


def _mm_dw_gcols(name, a, dy, G, split_in=False, tko=1024, ts=2048):
    S, K1 = a.shape
    Nl = (dy.shape[-1] * (2 if split_in else 1)) // G
    tko, ts = _tile(K1, tko), _tile(S, ts)
    hg = G // 2
    if split_in:
        dy_spec = pl.BlockSpec((None, ts, Nl), lambda g, j, k: (g // hg, k, g % hg))
    else:
        dy_spec = pl.BlockSpec((ts, Nl), lambda g, j, k: (k, g))
    return _matmul(
        name, TN, (G, K1 // tko, S // ts),
        a, pl.BlockSpec((ts, tko), lambda g, j, k: (k, j)),
        dy, dy_spec,
        jax.ShapeDtypeStruct((G, K1, Nl), F32), pl.BlockSpec((None, tko, Nl), lambda g, j, k: (g, j, 0)),
        (tko, Nl))


def _rms_fwd(name, x, g, tr=256):
    S, D = x.shape
    tr = _tile(S, tr)

    def body(x_ref, g_ref, o_ref):
        xf = x_ref[...]
        r = lax.rsqrt(jnp.mean(xf * xf, axis=-1, keepdims=True) + EPS)
        o_ref[...] = (xf * r * g_ref[...]).astype(o_ref.dtype)

    return pl.pallas_call(
        body, name=name, grid=(S // tr,),
        in_specs=[pl.BlockSpec((tr, D), lambda i: (i, 0)), pl.BlockSpec((1, D), lambda i: (0, 0))],
        out_specs=pl.BlockSpec((tr, D), lambda i: (i, 0)),
        out_shape=jax.ShapeDtypeStruct((S, D), BF16), compiler_params=_cp("parallel"))(x, g.reshape(1, D))


def _rms_bwd(name, dh, x, g, dres=None, tr=256):
    S, D = x.shape
    tr = _tile(S, tr)
    has_res = dres is not None

    def body(*refs):
        dh_ref, x_ref, g_ref = refs[:3]
        dres_ref = refs[3] if has_res else None
        dx_ref, dg_ref = refs[-2], refs[-1]
        xf = x_ref[...]
        r = lax.rsqrt(jnp.mean(xf * xf, axis=-1, keepdims=True) + EPS)
        xh = xf * r
        dhv = dh_ref[...]
        dxh = dhv * g_ref[...]
        dx = r * (dxh - xh * jnp.mean(dxh * xh, axis=-1, keepdims=True))
        if has_res:
            dx = dx + dres_ref[...]
        dx_ref[...] = dx
        part = jnp.sum(dhv * xh, axis=0, keepdims=True)
        i = pl.program_id(0)

        @pl.when(i == 0)
        def _():
            dg_ref[...] = part

        @pl.when(i > 0)
        def _():
            dg_ref[...] += part

    row = pl.BlockSpec((tr, D), lambda i: (i, 0))
    vec = pl.BlockSpec((1, D), lambda i: (0, 0))
    ins = [dh, x, g.reshape(1, D)] + ([dres] if has_res else [])
    return pl.pallas_call(
        body, name=name, grid=(S // tr,),
        in_specs=[row, row, vec] + ([row] if has_res else []),
        out_specs=[row, vec],
        out_shape=[jax.ShapeDtypeStruct((S, D), F32), jax.ShapeDtypeStruct((1, D), F32)],
        compiler_params=_cp("arbitrary"))(*ins)


def _final(name, x, tgt, g, tr=256):
    S, D = x.shape
    tr = _tile(S, tr)

    def body(x_ref, t_ref, g_ref, dx_ref, dg_ref, loss_ref):
        xf = x_ref[...]
        gv = g_ref[...]
        r = lax.rsqrt(jnp.mean(xf * xf, axis=-1, keepdims=True) + EPS)
        xh = xf * r
        err = xh * gv - t_ref[...]
        sq = jnp.sum(jnp.sum(err * err, axis=1, keepdims=True), axis=0, keepdims=True)
        dy = err * (1.0 / D)
        dxh = dy * gv
        dx_ref[...] = r * (dxh - xh * jnp.mean(dxh * xh, axis=-1, keepdims=True))
        part = jnp.sum(dy * xh, axis=0, keepdims=True)
        lpart = jnp.broadcast_to(sq, (8, 128))
        i = pl.program_id(0)

        @pl.when(i == 0)
        def _():
            dg_ref[...] = part
            loss_ref[...] = lpart

        @pl.when(i > 0)
        def _():
            dg_ref[...] += part
            loss_ref[...] += lpart

    row = pl.BlockSpec((tr, D), lambda i: (i, 0))
    vec = pl.BlockSpec((1, D), lambda i: (0, 0))
    return pl.pallas_call(
        body, name=name, grid=(S // tr,), in_specs=[row, row, vec],
        out_specs=[row, vec, pl.BlockSpec((8, 128), lambda i: (0, 0))],
        out_shape=[jax.ShapeDtypeStruct((S, D), F32), jax.ShapeDtypeStruct((1, D), F32),
                   jax.ShapeDtypeStruct((8, 128), F32)],
        compiler_params=_cp("arbitrary"))(x, tgt, g.reshape(1, D))


def _band_specs(nb, col_of):
    prev = pl.BlockSpec((QBLK, GW), lambda r, b: (jnp.maximum(b - 1, 0), col_of(r)))
    cur = pl.BlockSpec((QBLK, GW), lambda r, b: (b, col_of(r)))
    nxt = pl.BlockSpec((QBLK, GW), lambda r, b: (jnp.minimum(b + 1, nb - 1), col_of(r)))
    return [prev, cur, nxt]


def _cat3(refs, sl):
    return jnp.concatenate([ref[:, sl] for ref in refs], axis=0)


def _attn_fwd(name, proj, g):
    window, dil = A_PATTERNS[g]
    n_side = (window // 2) // dil
    S, C = proj.shape
    L = S // dil
    nb = L // QBLK
    cb = C // GW
    pv = proj.reshape(L, dil * C)
    ng = len(A_PATTERNS)

    def body(q_ref, kp, kc, kn, vp, vc, vn, o_ref, lse_ref):
        b = pl.program_id(1)
        jq = b * QBLK + lax.broadcasted_iota(jnp.int32, (QBLK, 3 * QBLK), 0)
        jk = (b - 1) * QBLK + lax.broadcasted_iota(jnp.int32, (QBLK, 3 * QBLK), 1)
        rel = jnp.abs(jk - jq)
        mask = (rel <= n_side) & (jk >= 0) & (jk < L)
        dist = rel.astype(F32) * float(dil)
        for hh in range(HPG):
            sl = slice(hh * HEAD, (hh + 1) * HEAD)
            k = _cat3((kp, kc, kn), sl)
            v = _cat3((vp, vc, vn), sl)
            s = _dot(q_ref[:, sl], k, NT) * SCALE - float(SLOPES[g * HPG + hh]) * dist
            s = jnp.where(mask, s, NEG)
            m = jnp.max(s, axis=1, keepdims=True)
            p = jnp.exp(s - m)
            l = jnp.sum(p, axis=1, keepdims=True)
            o_ref[:, sl] = _dot(p, v) / l
            lse_ref[:, sl] = jnp.broadcast_to(m + jnp.log(l), (QBLK, HEAD))

    q_spec = pl.BlockSpec((QBLK, GW), lambda r, b: (b, r * cb + g))
    k_specs = _band_specs(nb, lambda r: r * cb + ng + g)
    v_specs = _band_specs(nb, lambda r: r * cb + 2 * ng + g)
    o_spec = pl.BlockSpec((QBLK, GW), lambda r, b: (b, r))
    shape = jax.ShapeDtypeStruct((L, dil * GW), F32)
    o, lse = pl.pallas_call(
        body, name=name, grid=(dil, nb), in_specs=[q_spec] + k_specs + v_specs,
        out_specs=[o_spec, o_spec], out_shape=[shape, shape],
        compiler_params=_cp("parallel", "parallel"))(pv, pv, pv, pv, pv, pv, pv)
    return o.reshape(S, GW), lse.reshape(S, GW)


def _attn_merge(name, outs, lses, tr=256):
    S = outs[0].shape[0]
    tr = _tile(S, tr)
    ng = len(outs)

    def body(*refs):
        o_refs, l_refs = refs[:ng], refs[ng:2 * ng]
        tok_ref, lse_ref = refs[-2], refs[-1]
        ls = [r[...] for r in l_refs]
        m = functools.reduce(jnp.maximum, ls)
        es = [jnp.exp(l - m) for l in ls]
        tot = functools.reduce(lambda a, b: a + b, es)
        acc = None
        for e, o_ref in zip(es, o_refs):
            term = (e / tot) * o_ref[...]
            acc = term if acc is None else acc + term
        tok_ref[...] = acc
        lse_ref[...] = m + jnp.log(tot)

    row = pl.BlockSpec((tr, GW), lambda i: (i, 0))
    shape = jax.ShapeDtypeStruct((S, GW), F32)
    return pl.pallas_call(
        body, name=name, grid=(S // tr,), in_specs=[row] * (2 * ng), out_specs=[row, row],
        out_shape=[shape, shape], compiler_params=_cp("parallel"))(*outs, *lses)


def _attn_bwd(name, proj, g, dtok_src, dtok_blk, tok, lse):
    window, dil = A_PATTERNS[g]
    n_side = (window // 2) // dil
    S, C = proj.shape
    L = S // dil
    nb = L // QBLK
    cb = C // GW
    ng = len(A_PATTERNS)
    pv = proj.reshape(L, dil * C)
    dcb = dtok_src.shape[1] // GW
    dv_ = dtok_src.reshape(L, dil * dtok_src.shape[1])
    ov = tok.reshape(L, dil * GW)
    lv = lse.reshape(L, dil * GW)

    def body(qp, qc, qn, kp, kc, kn, vp, vc, vn, dop, doc, don, op, oc, on, lp, lc, ln,
             dq_ref, dk_ref, dv_ref):
        b = pl.program_id(1)
        jq = b * QBLK + lax.broadcasted_iota(jnp.int32, (QBLK, 3 * QBLK), 0)
        jk = (b - 1) * QBLK + lax.broadcasted_iota(jnp.int32, (QBLK, 3 * QBLK), 1)
        rel = jnp.abs(jk - jq)
        mask = (rel <= n_side) & (jk >= 0) & (jk < L)
        dist = rel.astype(F32) * float(dil)
        jq3 = (b - 1) * QBLK + lax.broadcasted_iota(jnp.int32, (3 * QBLK, QBLK), 0)
        jk1 = b * QBLK + lax.broadcasted_iota(jnp.int32, (3 * QBLK, QBLK), 1)
        rel3 = jnp.abs(jk1 - jq3)
        mask3 = (rel3 <= n_side) & (jq3 >= 0) & (jq3 < L)
        dist3 = rel3.astype(F32) * float(dil)
        for hh in range(HPG):
            sl = slice(hh * HEAD, (hh + 1) * HEAD)
            one = slice(hh * HEAD, hh * HEAD + 1)
            slope = float(SLOPES[g * HPG + hh])
            q = qc[:, sl]
            do = doc[:, sl]
            k3 = _cat3((kp, kc, kn), sl)
            v3 = _cat3((vp, vc, vn), sl)
            delta = jnp.sum(do * oc[:, sl], axis=1, keepdims=True)
            s = _dot(q, k3, NT) * SCALE - slope * dist
            p = jnp.where(mask, jnp.exp(s - lc[:, one]), 0.0)
            ds = p * (_dot(do, v3, NT) - delta)
            dq_ref[:, sl] = _dot(ds, k3) * SCALE

            q3 = _cat3((qp, qc, qn), sl)
            do3 = _cat3((dop, doc, don), sl)
            o3 = _cat3((op, oc, on), sl)
            lse3 = _cat3((lp, lc, ln), sl)[:, :1]
            delta3 = jnp.sum(do3 * o3, axis=1, keepdims=True)
            k = kc[:, sl]
            v = vc[:, sl]
            s3 = _dot(q3, k, NT) * SCALE - slope * dist3
            p3 = jnp.where(mask3, jnp.exp(s3 - lse3), 0.0)
            ds3 = p3 * (_dot(do3, v, NT) - delta3)
            dv_ref[:, sl] = _dot(p3, do3, TN)
            dk_ref[:, sl] = _dot(ds3, q3, TN) * SCALE

    specs = (_band_specs(nb, lambda r: r * cb + g) + _band_specs(nb, lambda r: r * cb + ng + g)
             + _band_specs(nb, lambda r: r * cb + 2 * ng + g)
             + _band_specs(nb, lambda r: r * dcb + dtok_blk)
             + _band_specs(nb, lambda r: r) + _band_specs(nb, lambda r: r))
    o_spec = pl.BlockSpec((QBLK, GW), lambda r, b: (b, r))
    shape = jax.ShapeDtypeStruct((L, dil * GW), F32)
    outs = pl.pallas_call(
        body, name=name, grid=(dil, nb), in_specs=specs, out_specs=[o_spec] * 3, out_shape=[shape] * 3,
        compiler_params=_cp("parallel", "parallel"))(*([pv] * 9 + [dv_] * 3 + [ov] * 3 + [lv] * 3))
    return [o.reshape(S, GW) for o in outs]


def _mem_fwd(name, proj, q_blk, kv, tq=256):
    S = proj.shape[0]
    M = kv.shape[0]
    tq = _tile(S, tq)

    def body(q_ref, kv_ref, o_ref):
        for hh in range(HPG):
            sl = slice(hh * HEAD, (hh + 1) * HEAD)
            k = kv_ref[:, sl]
            v = kv_ref[:, GW + hh * HEAD:GW + (hh + 1) * HEAD]
            s = _dot(q_ref[:, sl], k, NT) * SCALE
            m = jnp.max(s, axis=1, keepdims=True)
            p = jnp.exp(s - m)
            p = p / jnp.sum(p, axis=1, keepdims=True)
            o_ref[:, sl] = _dot(p, v)

    return pl.pallas_call(
        body, name=name, grid=(S // tq,),
        in_specs=[pl.BlockSpec((tq, GW), lambda i: (i, q_blk)), pl.BlockSpec((M, 2 * GW), lambda i: (0, 0))],
        out_specs=pl.BlockSpec((tq, GW), lambda i: (i, 0)),
        out_shape=jax.ShapeDtypeStruct((S, GW), F32), compiler_params=_cp("parallel"))(proj, kv)


def _mem_bwd(name, proj, q_blk, kv, dcat, do_blk, tq=256):
    S = proj.shape[0]
    M = kv.shape[0]
    tq = _tile(S, tq)

    def body(q_ref, kv_ref, do_ref, dq_ref, dkv_ref):
        i = pl.program_id(0)
        for hh in range(HPG):
            sl = slice(hh * HEAD, (hh + 1) * HEAD)
            vsl = slice(GW + hh * HEAD, GW + (hh + 1) * HEAD)
            q = q_ref[:, sl]
            do = do_ref[:, sl]
            k = kv_ref[:, sl]
            v = kv_ref[:, vsl]
            s = _dot(q, k, NT) * SCALE
            m = jnp.max(s, axis=1, keepdims=True)
            p = jnp.exp(s - m)
            p = p / jnp.sum(p, axis=1, keepdims=True)
            dp = _dot(do, v, NT)
            ds = p * (dp - jnp.sum(dp * p, axis=1, keepdims=True))
            dq_ref[:, sl] = _dot(ds, k) * SCALE
            dk = _dot(ds, q, TN) * SCALE
            dvv = _dot(p, do, TN)

            @pl.when(i == 0)
            def _():
                dkv_ref[:, sl] = dk
                dkv_ref[:, vsl] = dvv

            @pl.when(i > 0)
            def _():
                dkv_ref[:, sl] += dk
                dkv_ref[:, vsl] += dvv

    return pl.pallas_call(
        body, name=name, grid=(S // tq,),
        in_specs=[pl.BlockSpec((tq, GW), lambda i: (i, q_blk)), pl.BlockSpec((M, 2 * GW), lambda i: (0, 0)),
                  pl.BlockSpec((tq, GW), lambda i: (i, do_blk))],
        out_specs=[pl.BlockSpec((tq, GW), lambda i: (i, 0)), pl.BlockSpec((M, 2 * GW), lambda i: (0, 0))],
        out_shape=[jax.ShapeDtypeStruct((S, GW), F32), jax.ShapeDtypeStruct((M, 2 * GW), F32)],
        compiler_params=_cp("arbitrary"))(proj, kv, dcat)


_RSQRT2 = float(1.0 / np.sqrt(2.0))
_RSQRT2PI = float(1.0 / np.sqrt(2.0 * np.pi))


def _gelu(x):
    return 0.5 * x * (1.0 + lax.erf(x * _RSQRT2))


def _gelu_grad(x):
    return 0.5 * (1.0 + lax.erf(x * _RSQRT2)) + x * jnp.exp(-0.5 * x * x) * _RSQRT2PI


def _sgu_fwd(name, proj, gv, w_s, bias_t):
    S = proj.shape[0]
    nch = S // HEAD

    def body(u_ref, v_ref, gv_ref, ws_ref, b_ref, o_ref):
        v = _gelu(v_ref[...])
        r = lax.rsqrt(jnp.mean(v * v, axis=-1, keepdims=True) + EPS)
        vn = v * r * gv_ref[...]
        for gg in range(B_GROUPS):
            sl = slice(gg * HEAD, (gg + 1) * HEAD)
            mixed = _dot(ws_ref[gg], vn[:, sl]) + b_ref[:, gg:gg + 1]
            o_ref[:, sl] = _gelu(u_ref[:, sl]) * mixed

    return pl.pallas_call(
        body, name=name, grid=(nch,),
        in_specs=[pl.BlockSpec((HEAD, B_W), lambda c: (c, 0)), pl.BlockSpec((HEAD, B_W), lambda c: (c, 1)),
                  pl.BlockSpec((1, B_W), lambda c: (0, 0)),
                  pl.BlockSpec((B_GROUPS, HEAD, HEAD), lambda c: (0, 0, 0)),
                  pl.BlockSpec((HEAD, B_GROUPS), lambda c: (0, 0))],
        out_specs=pl.BlockSpec((HEAD, B_W), lambda c: (c, 0)),
        out_shape=jax.ShapeDtypeStruct((S, B_W), F32),
        compiler_params=_cp("parallel"))(proj, proj, gv.reshape(1, B_W), w_s, bias_t)


def _sgu_bwd(name, proj, gv, w_s, bias_t, dcat):
    S = proj.shape[0]
    nch = S // HEAD

    def body(u_ref, v_ref, gv_ref, ws_ref, b_ref, dt_ref, du_ref, dvp_ref, dgv_ref, dws_ref, db_ref, dvn_ref):
        c = pl.program_id(0)
        vpre = v_ref[...]
        v = _gelu(vpre)
        r = lax.rsqrt(jnp.mean(v * v, axis=-1, keepdims=True) + EPS)
        vh = v * r
        gvv = gv_ref[...]
        vn = vh * gvv
        for gg in range(B_GROUPS):
            sl = slice(gg * HEAD, (gg + 1) * HEAD)
            upre = u_ref[:, sl]
            dt = dt_ref[:, sl]
            vng = vn[:, sl]
            mixed = _dot(ws_ref[gg], vng) + b_ref[:, gg:gg + 1]
            du_ref[:, sl] = dt * mixed * _gelu_grad(upre)
            dmix = dt * _gelu(upre)
            dvn_ref[:, sl] = _dot(ws_ref[gg], dmix, TN)
            dws = _dot(dmix, vng, NT)
            dbs = jnp.sum(dmix, axis=1, keepdims=True)

            @pl.when(c == 0)
            def _():
                dws_ref[gg] = dws
                db_ref[:, gg:gg + 1] = dbs

            @pl.when(c > 0)
            def _():
                dws_ref[gg] += dws
                db_ref[:, gg:gg + 1] += dbs

        dvn = dvn_ref[...]
        dgp = jnp.sum(dvn * vh, axis=0, keepdims=True)
        dvh = dvn * gvv
        dv = r * (dvh - vh * jnp.mean(dvh * vh, axis=-1, keepdims=True))
        dvp_ref[...] = dv * _gelu_grad(vpre)

        @pl.when(c == 0)
        def _():
            dgv_ref[...] = dgp

        @pl.when(c > 0)
        def _():
            dgv_ref[...] += dgp

    blk = lambda j: pl.BlockSpec((HEAD, B_W), lambda c: (c, j))
    vec = pl.BlockSpec((1, B_W), lambda c: (0, 0))
    ws_spec = pl.BlockSpec((B_GROUPS, HEAD, HEAD), lambda c: (0, 0, 0))
    b_spec = pl.BlockSpec((HEAD, B_GROUPS), lambda c: (0, 0))
    du, dvp, dgv, dws, db = pl.pallas_call(
        body, name=name, grid=(nch,),
        in_specs=[blk(0), blk(1), vec, ws_spec, b_spec, blk(0)],
        out_specs=[blk(0), blk(0), vec, ws_spec, b_spec],
        out_shape=[jax.ShapeDtypeStruct((S, B_W), F32), jax.ShapeDtypeStruct((S, B_W), F32),
                   jax.ShapeDtypeStruct((1, B_W), F32), jax.ShapeDtypeStruct((B_GROUPS, HEAD, HEAD), F32),
                   jax.ShapeDtypeStruct((HEAD, B_GROUPS), F32)],
        scratch_shapes=[pltpu.VMEM((HEAD, B_W), F32)],
        compiler_params=_cp("arbitrary"))(proj, proj, gv.reshape(1, B_W), w_s, bias_t, dcat)
    return du, dvp, dgv, dws, db


def _shift_down(a, row):
    return jnp.where(row == 0, 0.0, pltpu.roll(a, 1, 0))


def _shift_up(a, row):
    n = a.shape[0]
    return jnp.where(row == n - 1, 0.0, pltpu.roll(a, n - 1, 0))


def _conv(a, w, b, row):
    return _shift_down(a, row) * w[0:1] + a * w[1:2] + _shift_up(a, row) * w[2:3] + b


def _conv_fwd(name, a3, cw, cb, tc=256):
    _, S, FF = a3.shape
    tc = _tile(FF, tc)

    def body(a_ref, w_ref, b_ref, o_ref):
        row = lax.broadcasted_iota(jnp.int32, (S, tc), 0)
        cg = _conv(a_ref[0], w_ref[0], b_ref[0], row)
        cv = _conv(a_ref[1], w_ref[1], b_ref[1], row)
        o_ref[...] = (_gelu(cg) * cv).astype(o_ref.dtype)

    return pl.pallas_call(
        body, name=name, grid=(FF // tc,),
        in_specs=[pl.BlockSpec((2, S, tc), lambda j: (0, 0, j)), pl.BlockSpec((2, 3, tc), lambda j: (0, 0, j)),
                  pl.BlockSpec((2, 1, tc), lambda j: (0, 0, j))],
        out_specs=pl.BlockSpec((S, tc), lambda j: (0, j)),
        out_shape=jax.ShapeDtypeStruct((S, FF), BF16), compiler_params=_cp("parallel"))(a3, cw, cb)


def _conv_bwd(name, a3, cw, cb, dact, tc=128):
    _, S, FF = a3.shape
    tc = _tile(FF, tc)

    def body(a_ref, w_ref, b_ref, d_ref, da_ref, dw_ref, db_ref):
        row = lax.broadcasted_iota(jnp.int32, (S, tc), 0)
        ag, av = a_ref[0], a_ref[1]
        wg, wv = w_ref[0], w_ref[1]
        cg = _conv(ag, wg, b_ref[0], row)
        cv = _conv(av, wv, b_ref[1], row)
        d = d_ref[...]
        dcs = (d * cv * _gelu_grad(cg), d * _gelu(cg))
        for h, (dc, a, w) in enumerate(zip(dcs, (ag, av), (wg, wv))):
            da_ref[h] = _shift_up(dc, row) * w[0:1] + dc * w[1:2] + _shift_down(dc, row) * w[2:3]
            dw_ref[h, 0:1, :] = jnp.sum(dc * _shift_down(a, row), axis=0, keepdims=True)
            dw_ref[h, 1:2, :] = jnp.sum(dc * a, axis=0, keepdims=True)
            dw_ref[h, 2:3, :] = jnp.sum(dc * _shift_up(a, row), axis=0, keepdims=True)
            db_ref[h] = jnp.sum(dc, axis=0, keepdims=True)

    a_spec = pl.BlockSpec((2, S, tc), lambda j: (0, 0, j))
    w_spec = pl.BlockSpec((2, 3, tc), lambda j: (0, 0, j))
    b_spec = pl.BlockSpec((2, 1, tc), lambda j: (0, 0, j))
    return pl.pallas_call(
        body, name=name, grid=(FF // tc,),
        in_specs=[a_spec, w_spec, b_spec, pl.BlockSpec((S, tc), lambda j: (0, j))],
        out_specs=[a_spec, w_spec, b_spec],
        out_shape=[jax.ShapeDtypeStruct((2, S, FF), F32), jax.ShapeDtypeStruct((2, 3, FF), F32),
                   jax.ShapeDtypeStruct((2, 1, FF), F32)],
        compiler_params=_cp("parallel"))(a3, cw, cb, dact)


_HBM = pl.BlockSpec(memory_space=pltpu.HBM)


def _position():
    return lax.axis_index("x"), lax.axis_index("y"), lax.axis_index("c")


def _all_gather(name, x, layer=None):
    block = x.shape if layer is None else x.shape[1:]

    def body(x_ref, out_ref, send_sems, recv_sems, local_sem):
        px, py, pc = _position()
        me, sibling = (px, py, pc), (px, py, 1 - pc)
        chips = [(1 - px, py), (px, 1 - py), (1 - px, 1 - py)]
        src = x_ref if layer is None else x_ref.at[layer]

        def slot(qx, qy, qc):
            return out_ref.at[4 * qx + 2 * qy + qc]

        def copy(k, blockpos, to, from_src=False):
            return pltpu.make_async_remote_copy(
                src_ref=src if from_src else slot(*blockpos), dst_ref=slot(*blockpos),
                send_sem=send_sems.at[k], recv_sem=recv_sems.at[k], device_id=to, device_id_type=MESH)

        mine = pltpu.make_async_copy(src, slot(*me), local_sem)
        mine.start()
        first = [copy(0, me, sibling, True)]
        first += [copy(1 + j, me, (*chip, pc), True) for j, chip in enumerate(chips)]
        for cp in first:
            cp.start()
        passed = [copy(4 + j, (*chip, pc), sibling) for j, chip in enumerate(chips)]
        for j, chip in enumerate(chips):
            copy(1 + j, (*chip, pc), me).wait_recv()
            passed[j].start()
        copy(0, sibling, me).wait_recv()
        for j, chip in enumerate(chips):
            copy(4 + j, (*chip, 1 - pc), me).wait_recv()
        for cp in first + passed:
            cp.wait_send()
        mine.wait()

    return pl.pallas_call(
        body, name=name, in_specs=[_HBM], out_specs=_HBM,
        out_shape=jax.ShapeDtypeStruct((N_DEV,) + tuple(block), x.dtype),
        scratch_shapes=[pltpu.SemaphoreType.DMA((7,)), pltpu.SemaphoreType.DMA((7,)), pltpu.SemaphoreType.DMA(())],
    )(x)


def _pair_exchange(name, dw):
    _, R, C = dw.shape

    def body(dw_ref, recv_ref, send_sems, recv_sems):
        px, py, pc = _position()
        copies = [pltpu.make_async_remote_copy(
            src_ref=dw_ref.at[2 * k + (1 - pc)], dst_ref=recv_ref.at[k], send_sem=send_sems.at[k],
            recv_sem=recv_sems.at[k], device_id=(px, py, 1 - pc), device_id_type=MESH) for k in range(4)]
        for cp in copies:
            cp.start()
        for cp in copies:
            cp.wait()

    return pl.pallas_call(
        body, name=name, in_specs=[_HBM], out_specs=_HBM,
        out_shape=jax.ShapeDtypeStruct((4, R, C), dw.dtype),
        scratch_shapes=[pltpu.SemaphoreType.DMA((4,)), pltpu.SemaphoreType.DMA((4,))])(dw)


_FLIPS = ((1, 0), (0, 1), (1, 1))


def _chip_exchange(name, p):
    _, R, C = p.shape

    def body(p_ref, recv_ref, send_sems, recv_sems):
        px, py, pc = _position()
        copies = []
        for j, (fx, fy) in enumerate(_FLIPS):
            qx = 1 - px if fx else px
            qy = 1 - py if fy else py
            copies.append(pltpu.make_async_remote_copy(
                src_ref=p_ref.at[2 * qx + qy], dst_ref=recv_ref.at[j], send_sem=send_sems.at[j],
                recv_sem=recv_sems.at[j], device_id=(qx, qy, pc), device_id_type=MESH))
        for cp in copies:
            cp.start()
        for cp in copies:
            cp.wait()

    return pl.pallas_call(
        body, name=name, in_specs=[_HBM], out_specs=_HBM,
        out_shape=jax.ShapeDtypeStruct((3, R, C), p.dtype),
        scratch_shapes=[pltpu.SemaphoreType.DMA((3,)), pltpu.SemaphoreType.DMA((3,))])(p)


def _pair_sum(name, core, dw, recv):
    _, R, C = dw.shape
    tr = _row_tile(R, C)
    dw4 = dw.reshape(4, 2, R, C)

    def body(core_ref, a_ref, b_ref, o_ref):
        o_ref[...] = (a_ref[...] + b_ref[...]).astype(o_ref.dtype)

    return pl.pallas_call(
        body, name=name,
        grid_spec=pltpu.PrefetchScalarGridSpec(
            num_scalar_prefetch=1, grid=(4, R // tr),
            in_specs=[pl.BlockSpec((None, None, tr, C), lambda k, i, c_ref: (k, c_ref[0], i, 0)),
                      pl.BlockSpec((None, tr, C), lambda k, i, c_ref: (k, i, 0))],
            out_specs=pl.BlockSpec((None, tr, C), lambda k, i, c_ref: (k, i, 0))),
        out_shape=jax.ShapeDtypeStruct((4, R, C), BF16),
        compiler_params=_cp("parallel", "parallel"))(core, dw4, recv)


def _adamw_math(w, g, m, v):
    m = ADAM_B1 * m + (1.0 - ADAM_B1) * g
    v = ADAM_B2 * v + (1.0 - ADAM_B2) * (g * g)
    m_hat = m / (1.0 - ADAM_B1 ** ADAM_STEP)
    v_hat = v / (1.0 - ADAM_B2 ** ADAM_STEP)
    delta = -ADAM_LR * (m_hat / (jnp.sqrt(v_hat) + ADAM_EPS) + ADAM_WD * w)
    return delta, m, v


def _adamw_shard(name, chip, layer, w, m, v, p, recv, prev):
    nl, R, C = w.shape
    tr = _row_tile(R, C)
    n_prev = 0 if prev is None else 4

    def body(chip_ref, w_ref, m_ref, v_ref, p_ref, r_ref, *rest):
        g_ref, d_ref, nm_ref, nv_ref = rest[n_prev:]
        g = p_ref[...].astype(F32)
        for j in range(3):
            g = g + r_ref[j].astype(F32)
        delta, nm, nv = _adamw_math(w_ref[...], g, m_ref[...], v_ref[...])
        g_ref[...] = g
        d_ref[...] = delta
        nm_ref[...] = nm
        nv_ref[...] = nv

    lay = pl.BlockSpec((None, tr, C), lambda i, c_ref: (layer, i, 0))
    in_specs = [lay, lay, lay,
                pl.BlockSpec((None, tr, C), lambda i, c_ref: (c_ref[0], i, 0)),
                pl.BlockSpec((3, tr, C), lambda i, c_ref: (0, i, 0))]
    in_specs += [pl.BlockSpec(memory_space=pl.ANY)] * n_prev
    shape = jax.ShapeDtypeStruct((nl, R, C), F32)
    ins = [chip, w, m, v, p, recv] + ([] if prev is None else list(prev))
    return pl.pallas_call(
        body, name=name,
        grid_spec=pltpu.PrefetchScalarGridSpec(
            num_scalar_prefetch=1, grid=(R // tr,), in_specs=in_specs, out_specs=[lay] * 4),
        out_shape=[shape] * 4,
        input_output_aliases={6 + j: j for j in range(n_prev)},
        compiler_params=_cp("parallel"))(*ins)


def _sum_slots(name, parts, tr=512):
    n, R, C = parts.shape
    tr = _tile(R, tr)

    def body(p_ref, o_ref):
        acc = p_ref[0]
        for j in range(1, n):
            acc = acc + p_ref[j]
        o_ref[...] = acc

    return pl.pallas_call(
        body, name=name, grid=(R // tr,),
        in_specs=[pl.BlockSpec((n, tr, C), lambda i: (0, i, 0))],
        out_specs=pl.BlockSpec((tr, C), lambda i: (i, 0)),
        out_shape=jax.ShapeDtypeStruct((R, C), F32), compiler_params=_cp("parallel"))(parts)


def _adamw_flat(name, w, g, m, v, tr=512):
    R, C = w.shape
    tr = _tile(R, tr)

    def body(w_ref, g_ref, m_ref, v_ref, d_ref, nm_ref, nv_ref):
        delta, nm, nv = _adamw_math(w_ref[...], g_ref[...], m_ref[...], v_ref[...])
        d_ref[...] = delta
        nm_ref[...] = nm
        nv_ref[...] = nv

    row = pl.BlockSpec((tr, C), lambda i: (i, 0))
    shape = jax.ShapeDtypeStruct((R, C), F32)
    return pl.pallas_call(
        body, name=name, grid=(R // tr,), in_specs=[row] * 4, out_specs=[row] * 3, out_shape=[shape] * 3,
        compiler_params=_cp("parallel"))(w, g, m, v)


_PACK_ROWS = 512


def _pack(arrs):
    flat = jnp.concatenate([a.reshape(-1) for a in arrs])
    unit = _PACK_ROWS * 128
    pad = (-flat.shape[0]) % unit
    return jnp.pad(flat, (0, pad)).reshape(-1, 128)


def _unpack(packed, shapes):
    flat = packed.reshape(-1)
    outs, off = [], 0
    for s in shapes:
        n = int(np.prod(s))
        outs.append(flat[off:off + n].reshape(s))
        off += n
    return outs


def kernel(x, mem, mix_norm_g, ffn_norm_g, mem_norm_g, w_mem_kv, a_w_in, a_w_out, b_w_in, b_v_norm_g, b_w_s, b_s_bias, b_w_out, ffn_w_up, ffn_conv_w, ffn_conv_b, ffn_w_down, final_norm_g, loss_target, m_mix_norm_g, m_ffn_norm_g, m_mem_norm_g, m_w_mem_kv, m_a_w_in, m_a_w_out, m_b_w_in, m_b_v_norm_g, m_b_w_s, m_b_s_bias, m_b_w_out, m_ffn_w_up, m_ffn_conv_w, m_ffn_conv_b, m_ffn_w_down, m_final_norm_g, v_mix_norm_g, v_ffn_norm_g, v_mem_norm_g, v_w_mem_kv, v_a_w_in, v_a_w_out, v_b_w_in, v_b_v_norm_g, v_b_w_s, v_b_s_bias, v_b_w_out, v_ffn_w_up, v_ffn_conv_w, v_ffn_conv_b, v_ffn_w_down, v_final_norm_g):
    px, py, pc = _position()
    dev = 4 * px + 2 * py + pc
    core = jnp.reshape(pc, (1,)).astype(jnp.int32)
    chip = jnp.reshape(2 * px + py, (1,)).astype(jnp.int32)

    x0 = x[0]
    mem0 = mem[0]
    tgt = loss_target[0]
    S, D = x0.shape
    depth = mix_norm_g.shape[0]
    FF = ffn_w_down.shape[1] * N_DEV
    a_in = a_w_in.shape[2] * N_DEV
    b_in = b_w_in.shape[2] * N_DEV
    a_q_blk = (a_in - GW) // GW
    b_q_blk = (b_in - GW) // GW

    kv16, ain16, aout16, bin16 = (w.astype(BF16) for w in (w_mem_kv, a_w_in, a_w_out, b_w_in))
    bout16, up16, down16 = (w.astype(BF16) for w in (b_w_out, ffn_w_up, ffn_w_down))
    W = []
    for i in range(depth):
        j = i // 2
        lw = {}
        lw["kv"] = _all_gather(f"ag_kv{i}", kv16, i).reshape(D, 2 * GW)
        if i % 2 == 0:
            lw["in"] = _all_gather(f"ag_ain{j}", ain16, j)
            lw["out"] = _all_gather(f"ag_aout{j}", aout16, j)
        else:
            g_in = _all_gather(f"ag_bin{j}", bin16, j)
            lw["in"] = jnp.transpose(g_in, (1, 0, 2)).reshape(D, b_in)
            lw["out"] = _all_gather(f"ag_bout{j}", bout16, j).reshape(B_W + GW, D)
        lw["up"] = _all_gather(f"ag_up{i}", up16, i)
        lw["down"] = _all_gather(f"ag_down{i}", down16, i).reshape(FF, D)
        W.append(lw)
    small_local = _pack([ffn_conv_w, b_v_norm_g])
    small_all = _all_gather("ag_small", small_local)
    cw_parts, gv_parts = [], []
    for d in range(N_DEV):
        cw_d, gv_d = _unpack(small_all[d], [ffn_conv_w.shape, b_v_norm_g.shape])
        cw_parts.append(cw_d)
        gv_parts.append(gv_d)
    conv_w_full = jnp.concatenate(cw_parts, axis=-1)
    gv_full = jnp.concatenate(gv_parts, axis=-1)

    def conv_params(i):
        cw = conv_w_full[i].reshape(3, 2, FF).transpose(1, 0, 2)
        cb = ffn_conv_b[i].reshape(2, 1, FF)
        return cw, cb

    saved = []
    xc = x0
    for i in range(depth):
        j = i // 2
        lw = W[i]
        sv = {"x0": xc}
        h1 = _rms_fwd(f"mixnorm{i}", xc, mix_norm_g[i])
        memn = _rms_fwd(f"memnorm{i}", mem0, mem_norm_g[i])
        kv = _mm_full(f"kvproj{i}", memn, lw["kv"])
        if i % 2 == 0:
            proj = _mm_gcols(f"ain{i}", h1, lw["in"])
            outs, lses = [], []
            for g in range(len(A_PATTERNS)):
                o, l = _attn_fwd(f"attn{i}_{g}", proj, g)
                outs.append(o)
                lses.append(l)
            tok, lse = _attn_merge(f"merge{i}", outs, lses)
            mo = _mem_fwd(f"memattn{i}", proj, a_q_blk, kv)
            cat = jnp.concatenate([tok.astype(BF16), mo.astype(BF16)], axis=1)
            x1 = _mm_gcols(f"aout{i}", cat, lw["out"], res=xc)
            sv.update(tok=tok, lse=lse)
        else:
            proj = _mm_full(f"bin{i}", h1, lw["in"])
            bias_t = b_s_bias[j].T
            tok = _sgu_fwd(f"sgu{i}", proj, gv_full[j], b_w_s[j], bias_t)
            mo = _mem_fwd(f"memattn{i}", proj, b_q_blk, kv)
            cat = jnp.concatenate([tok.astype(BF16), mo.astype(BF16)], axis=1)
            x1 = _mm_full(f"bout{i}", cat, lw["out"], res=xc)
        h2 = _rms_fwd(f"ffnnorm{i}", x1, ffn_norm_g[i])
        cw, cb = conv_params(i)
        a3 = _mm_gcols(f"up{i}", h2, lw["up"], split_out=True, tm=512)
        act = _conv_fwd(f"conv{i}", a3, cw, cb)
        x2 = _mm_full(f"down{i}", act, lw["down"], res=x1, tm=512, tn=1024, tk=FF // 4)
        sv.update(h1=h1, memn=memn, kv=kv, proj=proj, cat=cat, x1=x1, h2=h2, a3=a3, act=act)
        saved.append(sv)
        xc = x2

    dx, dg_final, sq = _final("final", xc, tgt, final_norm_g)
    loss = lax.psum(sq[0, 0] * (0.5 / D), AXES)

    big = {k: [None] * n for k, n in (("kv", depth), ("ain", depth // 2 + depth % 2), ("aout", depth // 2 + depth % 2),
                                      ("bin", depth // 2), ("bout", depth // 2), ("up", depth), ("down", depth))}
    dg_mix, dg_ffn, dg_mem = [None] * depth, [None] * depth, [None] * depth
    d_conv_w, d_conv_b = [None] * depth, [None] * depth
    d_gv, d_ws, d_sb = [None] * (depth // 2), [None] * (depth // 2), [None] * (depth // 2)
    for i in reversed(range(depth)):
        j = i // 2
        lw, sv = W[i], saved[i]
        cw, cb = conv_params(i)
        dact = _mm_dx_full(f"ddown{i}", dx, lw["down"], tm=512, tko=FF // 4, tc=D)
        big["down"][i] = _mm_dw(f"wdown{i}", sv["act"], dx).reshape(N_DEV, FF // N_DEV, D)
        da3, dcw, dcb = _conv_bwd(f"dconv{i}", sv["a3"], cw, cb, dact)
        d_conv_w[i] = dcw.transpose(1, 0, 2).reshape(3, 2 * FF)
        d_conv_b[i] = dcb.reshape(2 * FF)
        dh2 = _mm_dx_gcols(f"dup{i}", da3, lw["up"], split_in=True, tm=512)
        big["up"][i] = _mm_dw_gcols(f"wup{i}", sv["h2"], da3, N_DEV, split_in=True, tko=512, ts=1024)
        dx1, dg_ffn[i] = _rms_bwd(f"dffnnorm{i}", dh2, sv["x1"], ffn_norm_g[i], dx)
        if i % 2 == 0:
            dcat = _mm_dx_gcols(f"daout{i}", dx1, lw["out"])
            big["aout"][j] = _mm_dw_gcols(f"waout{i}", sv["cat"], dx1, N_DEV)
            parts = [None] * 9
            for g in range(len(A_PATTERNS)):
                dq, dk, dv = _attn_bwd(f"dattn{i}_{g}", sv["proj"], g, dcat, 0, sv["tok"], sv["lse"])
                parts[g], parts[3 + g], parts[6 + g] = dq, dk, dv
            dqm, dkv = _mem_bwd(f"dmemattn{i}", sv["proj"], a_q_blk, sv["kv"], dcat, 1)
            dproj = jnp.concatenate(parts + [dqm], axis=1)
            dh1 = _mm_dx_gcols(f"dain{i}", dproj, lw["in"])
            big["ain"][j] = _mm_dw_gcols(f"wain{i}", sv["h1"], dproj, N_DEV)
        else:
            dcat = _mm_dx_full(f"dbout{i}", dx1, lw["out"], tm=1024)
            big["bout"][j] = _mm_dw(f"wbout{i}", sv["cat"], dx1).reshape(N_DEV, (B_W + GW) // N_DEV, D)
            bias_t = b_s_bias[j].T
            du, dvp, dgv, dws, dbt = _sgu_bwd(f"dsgu{i}", sv["proj"], gv_full[j], b_w_s[j], bias_t, dcat)
            d_gv[j], d_ws[j], d_sb[j] = dgv.reshape(B_W), dws, dbt.T
            dqm, dkv = _mem_bwd(f"dmemattn{i}", sv["proj"], b_q_blk, sv["kv"], dcat, B_W // GW)
            dproj = jnp.concatenate([du, dvp, dqm], axis=1)
            dh1 = _mm_dx_full(f"dbin{i}", dproj, lw["in"], tc=b_in // 2)
            dwin = _mm_dw(f"wbin{i}", sv["h1"], dproj, tko=1024, tn=512)
            big["bin"][j] = dwin.reshape(D, N_DEV, b_in // N_DEV).transpose(1, 0, 2)
        big["kv"][i] = _mm_dw(f"wkv{i}", sv["memn"], dkv, tko=1024).reshape(N_DEV, D // N_DEV, 2 * GW)
        dmemn = _mm_dx_full(f"dkvproj{i}", dkv, lw["kv"], tko=1024)
        _, dg_mem[i] = _rms_bwd(f"dmemnorm{i}", dmemn, mem0, mem_norm_g[i])
        dx, dg_mix[i] = _rms_bwd(f"dmixnorm{i}", dh1, sv["x0"], mix_norm_g[i], dx1)
    grad_x = dx[None]

    def update(tag, w, m, v, dws):
        nl = w.shape[0]
        C = w.shape[-1]
        w2, m2, v2 = (t.reshape(nl, -1, C) for t in (w, m, v))
        prev = None
        for l in range(nl):
            dw = dws[l]
            got = _pair_exchange(f"rs_pair_{tag}{l}", dw)
            p = _pair_sum(f"rs_sum_{tag}{l}", core, dw, got)
            recv = _chip_exchange(f"rs_chip_{tag}{l}", p)
            prev = _adamw_shard(f"adamw_{tag}{l}", chip, l, w2, m2, v2, p, recv, prev)
        return [t.reshape(w.shape) for t in prev]

    res = {}
    res["w_mem_kv"] = update("kv", w_mem_kv, m_w_mem_kv, v_w_mem_kv, big["kv"])
    res["a_w_in"] = update("ain", a_w_in, m_a_w_in, v_a_w_in, big["ain"])
    res["a_w_out"] = update("aout", a_w_out, m_a_w_out, v_a_w_out, big["aout"])
    res["b_w_in"] = update("bin", b_w_in, m_b_w_in, v_b_w_in, big["bin"])
    res["b_w_out"] = update("bout", b_w_out, m_b_w_out, v_b_w_out, big["bout"])
    res["ffn_w_up"] = update("up", ffn_w_up, m_ffn_w_up, v_ffn_w_up, big["up"])
    res["ffn_w_down"] = update("down", ffn_w_down, m_ffn_w_down, v_ffn_w_down, big["down"])

    small_grads = [jnp.concatenate(dg_mix), jnp.concatenate(dg_ffn), jnp.concatenate(dg_mem),
                   jnp.stack(d_ws), jnp.stack(d_sb), jnp.stack(d_conv_b), dg_final.reshape(D),
                   jnp.stack(d_gv), jnp.stack(d_conv_w)]
    g_small = _sum_slots("small_sum", _all_gather("ag_smallgrad", _pack(small_grads)))
    rep_shapes = [mix_norm_g.shape, ffn_norm_g.shape, mem_norm_g.shape, b_w_s.shape, b_s_bias.shape,
                  ffn_conv_b.shape, final_norm_g.shape]
    full_gv_shape = (b_v_norm_g.shape[0], B_W)
    full_cw_shape = (depth, 3, 2 * FF)
    g_list = _unpack(g_small, rep_shapes + [full_gv_shape, full_cw_shape])
    g_gv = lax.dynamic_slice_in_dim(g_list[7], dev * b_v_norm_g.shape[1], b_v_norm_g.shape[1], axis=1)
    g_cw = lax.dynamic_slice_in_dim(g_list[8], dev * ffn_conv_w.shape[2], ffn_conv_w.shape[2], axis=2)
    g_all = g_list[:7] + [g_gv, g_cw]
    names = ["mix_norm_g", "ffn_norm_g", "mem_norm_g", "b_w_s", "b_s_bias", "ffn_conv_b", "final_norm_g",
             "b_v_norm_g", "ffn_conv_w"]
    ws = [mix_norm_g, ffn_norm_g, mem_norm_g, b_w_s, b_s_bias, ffn_conv_b, final_norm_g, b_v_norm_g, ffn_conv_w]
    ms = [m_mix_norm_g, m_ffn_norm_g, m_mem_norm_g, m_b_w_s, m_b_s_bias, m_ffn_conv_b, m_final_norm_g,
          m_b_v_norm_g, m_ffn_conv_w]
    vs = [v_mix_norm_g, v_ffn_norm_g, v_mem_norm_g, v_b_w_s, v_b_s_bias, v_ffn_conv_b, v_final_norm_g,
          v_b_v_norm_g, v_ffn_conv_w]
    shapes = [w.shape for w in ws]
    d_p, m_p, v_p = _adamw_flat("adamw_small", _pack(ws), _pack(g_all), _pack(ms), _pack(vs))
    for n, g, d, nm, nv in zip(names, g_all, _unpack(d_p, shapes), _unpack(m_p, shapes), _unpack(v_p, shapes)):
        res[n] = [g, d, nm, nv]

    order = ["mix_norm_g", "ffn_norm_g", "mem_norm_g", "w_mem_kv", "a_w_in", "a_w_out", "b_w_in", "b_v_norm_g",
             "b_w_s", "b_s_bias", "b_w_out", "ffn_w_up", "ffn_conv_w", "ffn_conv_b", "ffn_w_down", "final_norm_g"]
    return (loss, grad_x, *[res[n][0] for n in order], *[res[n][1] for n in order],
            *[res[n][2] for n in order], *[res[n][3] for n in order])
```

```python
import functools

import numpy as np
import jax
import jax.numpy as jnp
from jax import lax
from jax.experimental import pallas as pl
from jax.experimental.pallas import tpu as pltpu

F32 = jnp.float32
BF16 = jnp.bfloat16
MESH = pl.DeviceIdType.MESH
AXES = ("x", "y", "c")
N_DEV = 8

EPS = 1e-6
NEG = -1e30
HEAD = 128
HPG = 4
GW = HPG * HEAD
A_PATTERNS = ((128, 1), (512, 4), (2048, 16))
A_HEADS = HPG * len(A_PATTERNS)
QBLK = 128
B_GROUPS = 12
B_W = B_GROUPS * HEAD
SLOPES = (2.0 ** (-8.0 * (np.arange(A_HEADS) + 1) / A_HEADS)).astype(np.float32)
SCALE = HEAD ** -0.5

ADAM_LR = 0.001
ADAM_B1 = 0.9
ADAM_B2 = 0.999
ADAM_EPS = 1e-08
ADAM_WD = 0.01
ADAM_STEP = 10

V7X_VMEM_LIMIT = 50 * 1024 * 1024

NN = (((1,), (0,)), ((), ()))
NT = (((1,), (1,)), ((), ()))
TN = (((0,), (0,)), ((), ()))


def _cp(*sem):
    return pltpu.CompilerParams(dimension_semantics=sem, vmem_limit_bytes=V7X_VMEM_LIMIT)


def _dot(a, b, dims=NN):
    return lax.dot_general(a, b, dims, preferred_element_type=F32)


def _tile(n, pref):
    t = min(n, pref)
    assert n % t == 0, (n, pref)
    return t


def _row_tile(rows, cols):
    best = None
    for t in range(16, rows + 1, 16):
        if rows % t == 0 and t * cols * 4 <= (1 << 20):
            best = t
    if best is None:
        best = rows
    return best


_DEP = pl.BlockSpec((8, 128), lambda *_: (0, 0))


def _matmul(name, dims, grid, a, a_spec, b, b_spec, out_shape, o_spec, tile, res=None, res_spec=None, deps=()):
    nk = grid[2]
    has_res = res is not None

    def body(*refs):
        a_ref, b_ref = refs[0], refs[1]
        r_ref = refs[2] if has_res else None
        o_ref, acc_ref = refs[-2], refs[-1]
        part = _dot(a_ref[...].astype(BF16), b_ref[...].astype(BF16), dims)

        def finish(val):
            if has_res:
                val = val + r_ref[...]
            o_ref[...] = val.astype(o_ref.dtype)

        if nk == 1:
            finish(part)
        else:
            k = pl.program_id(2)

            @pl.when(k == 0)
            def _():
                acc_ref[...] = part

            @pl.when(k > 0)
            def _():
                acc_ref[...] += part

            @pl.when(k == nk - 1)
            def _():
                finish(acc_ref[...])

    ins = [a, b] + ([res] if has_res else []) + list(deps)
    specs = [a_spec, b_spec] + ([res_spec] if has_res else []) + [_DEP] * len(deps)
    return pl.pallas_call(
        body, name=name, grid=grid, in_specs=specs, out_specs=o_spec, out_shape=out_shape,
        scratch_shapes=[pltpu.VMEM(tile if nk > 1 else (8, 128), F32)],
        compiler_params=_cp("parallel", "parallel", "arbitrary"))(*ins)


def _mm_full(name, a, w, res=None, tm=1024, tn=512, tk=2048):
    M, K = a.shape
    N = w.shape[1]
    tm, tn, tk = _tile(M, tm), _tile(N, tn), _tile(K, tk)
    return _matmul(
        name, NN, (N // tn, M // tm, K // tk),
        a, pl.BlockSpec((tm, tk), lambda j, i, k: (i, k)),
        w, pl.BlockSpec((tk, tn), lambda j, i, k: (k, j)),
        jax.ShapeDtypeStruct((M, N), F32), pl.BlockSpec((tm, tn), lambda j, i, k: (i, j)), (tm, tn),
        res, pl.BlockSpec((tm, tn), lambda j, i, k: (i, j)))


def _mm_gcols(name, a, wg, res=None, split_out=False, tm=1024):
    M, K = a.shape
    G, _, Nl = wg.shape
    tm = _tile(M, tm)
    hg = G // 2
    if split_out:
        shape = jax.ShapeDtypeStruct((2, M, hg * Nl), F32)
        o_spec = pl.BlockSpec((None, tm, Nl), lambda g, i, k: (g // hg, i, g % hg))
    else:
        shape = jax.ShapeDtypeStruct((M, G * Nl), F32)
        o_spec = pl.BlockSpec((tm, Nl), lambda g, i, k: (i, g))
    return _matmul(
        name, NN, (G, M // tm, 1),
        a, pl.BlockSpec((tm, K), lambda g, i, k: (i, 0)),
        wg, pl.BlockSpec((None, K, Nl), lambda g, i, k: (g, 0, 0)),
        shape, o_spec, (tm, Nl),
        res, pl.BlockSpec((tm, Nl), lambda g, i, k: (i, g)))


def _mm_dx_full(name, dy, w, tm=512, tko=512, tc=2048, deps=()):
    M, N = dy.shape
    K = w.shape[0]
    tm, tko, tc = _tile(M, tm), _tile(K, tko), _tile(N, tc)
    return _matmul(
        name, NT, (K // tko, M // tm, N // tc),
        dy, pl.BlockSpec((tm, tc), lambda j, i, k: (i, k)),
        w, pl.BlockSpec((tko, tc), lambda j, i, k: (j, k)),
        jax.ShapeDtypeStruct((M, K), F32), pl.BlockSpec((tm, tko), lambda j, i, k: (i, j)), (tm, tko), deps=deps)


def _mm_dx_gcols(name, dy, wg, split_in=False, tm=1024, tko=1024, deps=()):
    G, K, Nl = wg.shape
    M = dy.shape[-2]
    tm, tko = _tile(M, tm), _tile(K, tko)
    hg = G // 2
    if split_in:
        dy_spec = pl.BlockSpec((None, tm, Nl), lambda j, i, g: (g // hg, i, g % hg))
    else:
        dy_spec = pl.BlockSpec((tm, Nl), lambda j, i, g: (i, g))
    return _matmul(
        name, NT, (K // tko, M // tm, G),
        dy, dy_spec,
        wg, pl.BlockSpec((None, tko, Nl), lambda j, i, g: (g, j, 0)),
        jax.ShapeDtypeStruct((M, K), F32), pl.BlockSpec((tm, tko), lambda j, i, g: (i, j)), (tm, tko), deps=deps)


def _mm_dw(name, a, dy, tko=512, tn=1024, ts=2048, deps=()):
    S, K1 = a.shape
    N = dy.shape[1]
    tko, tn, ts = _tile(K1, tko), _tile(N, tn), _tile(S, ts)
    return _matmul(
        name, TN, (N // tn, K1 // tko, S // ts),
        a, pl.BlockSpec((ts, tko), lambda i, j, k: (k, j)),
        dy, pl.BlockSpec((ts, tn), lambda i, j, k: (k, i)),
        jax.ShapeDtypeStruct((K1, N), F32), pl.BlockSpec((tko, tn), lambda i, j, k: (j, i)), (tko, tn), deps=deps)


def _mm_dw_gcols(name, a, dy, G, split_in=False, tko=1024, ts=2048, deps=()):
    S, K1 = a.shape
    Nl = (dy.shape[-1] * (2 if split_in else 1)) // G
    tko, ts = _tile(K1, tko), _tile(S, ts)
    hg = G // 2
    if split_in:
        dy_spec = pl.BlockSpec((None, ts, Nl), lambda g, j, k: (g // hg, k, g % hg))
    else:
        dy_spec = pl.BlockSpec((ts, Nl), lambda g, j, k: (k, g))
    return _matmul(
        name, TN, (G, K1 // tko, S // ts),
        a, pl.BlockSpec((ts, tko), lambda g, j, k: (k, j)),
        dy, dy_spec,
        jax.ShapeDtypeStruct((G, K1, Nl), F32), pl.BlockSpec((None, tko, Nl), lambda g, j, k: (g, j, 0)),
        (tko, Nl), deps=deps)


def _rms_fwd(name, x, g, tr=256, deps=()):
    S, D = x.shape
    tr = _tile(S, tr)

    def body(x_ref, g_ref, *rest):
        o_ref = rest[-1]
        xf = x_ref[...]
        r = lax.rsqrt(jnp.mean(xf * xf, axis=-1, keepdims=True) + EPS)
        o_ref[...] = (xf * r * g_ref[...]).astype(o_ref.dtype)

    return pl.pallas_call(
        body, name=name, grid=(S // tr,),
        in_specs=[pl.BlockSpec((tr, D), lambda i: (i, 0)), pl.BlockSpec((1, D), lambda i: (0, 0))] + [_DEP] * len(deps),
        out_specs=pl.BlockSpec((tr, D), lambda i: (i, 0)),
        out_shape=jax.ShapeDtypeStruct((S, D), BF16), compiler_params=_cp("parallel"))(x, g.reshape(1, D), *deps)


def _rms_bwd(name, dh, x, g, dres=None, tr=256):
    S, D = x.shape
    tr = _tile(S, tr)
    has_res = dres is not None

    def body(*refs):
        dh_ref, x_ref, g_ref = refs[:3]
        dres_ref = refs[3] if has_res else None
        dx_ref, dg_ref = refs[-2], refs[-1]
        xf = x_ref[...]
        r = lax.rsqrt(jnp.mean(xf * xf, axis=-1, keepdims=True) + EPS)
        xh = xf * r
        dhv = dh_ref[...]
        dxh = dhv * g_ref[...]
        dx = r * (dxh - xh * jnp.mean(dxh * xh, axis=-1, keepdims=True))
        if has_res:
            dx = dx + dres_ref[...]
        dx_ref[...] = dx
        part = jnp.sum(dhv * xh, axis=0, keepdims=True)
        i = pl.program_id(0)

        @pl.when(i == 0)
        def _():
            dg_ref[...] = part

        @pl.when(i > 0)
        def _():
            dg_ref[...] += part

    row = pl.BlockSpec((tr, D), lambda i: (i, 0))
    vec = pl.BlockSpec((1, D), lambda i: (0, 0))
    ins = [dh, x, g.reshape(1, D)] + ([dres] if has_res else [])
    return pl.pallas_call(
        body, name=name, grid=(S // tr,),
        in_specs=[row, row, vec] + ([row] if has_res else []),
        out_specs=[row, vec],
        out_shape=[jax.ShapeDtypeStruct((S, D), F32), jax.ShapeDtypeStruct((1, D), F32)],
        compiler_params=_cp("arbitrary"))(*ins)


def _final(name, x, tgt, g, tr=256):
    S, D = x.shape
    tr = _tile(S, tr)

    def body(x_ref, t_ref, g_ref, dx_ref, dg_ref, loss_ref):
        xf = x_ref[...]
        gv = g_ref[...]
        r = lax.rsqrt(jnp.mean(xf * xf, axis=-1, keepdims=True) + EPS)
        xh = xf * r
        err = xh * gv - t_ref[...]
        sq = jnp.sum(jnp.sum(err * err, axis=1, keepdims=True), axis=0, keepdims=True)
        dy = err * (1.0 / D)
        dxh = dy * gv
        dx_ref[...] = r * (dxh - xh * jnp.mean(dxh * xh, axis=-1, keepdims=True))
        part = jnp.sum(dy * xh, axis=0, keepdims=True)
        lpart = jnp.broadcast_to(sq, (8, 128))
        i = pl.program_id(0)

        @pl.when(i == 0)
        def _():
            dg_ref[...] = part
            loss_ref[...] = lpart

        @pl.when(i > 0)
        def _():
            dg_ref[...] += part
            loss_ref[...] += lpart

    row = pl.BlockSpec((tr, D), lambda i: (i, 0))
    vec = pl.BlockSpec((1, D), lambda i: (0, 0))
    return pl.pallas_call(
        body, name=name, grid=(S // tr,), in_specs=[row, row, vec],
        out_specs=[row, vec, pl.BlockSpec((8, 128), lambda i: (0, 0))],
        out_shape=[jax.ShapeDtypeStruct((S, D), F32), jax.ShapeDtypeStruct((1, D), F32),
                   jax.ShapeDtypeStruct((8, 128), F32)],
        compiler_params=_cp("arbitrary"))(x, tgt, g.reshape(1, D))


def _band_specs(nb, col_of):
    prev = pl.BlockSpec((QBLK, GW), lambda r, b: (jnp.maximum(b - 1, 0), col_of(r)))
    cur = pl.BlockSpec((QBLK, GW), lambda r, b: (b, col_of(r)))
    nxt = pl.BlockSpec((QBLK, GW), lambda r, b: (jnp.minimum(b + 1, nb - 1), col_of(r)))
    return [prev, cur, nxt]


def _cat3(refs, sl):
    return jnp.concatenate([ref[:, sl] for ref in refs], axis=0)


def _attn_fwd(name, proj, g):
    window, dil = A_PATTERNS[g]
    n_side = (window // 2) // dil
    S, C = proj.shape
    L = S // dil
    nb = L // QBLK
    cb = C // GW
    pv = proj.reshape(L, dil * C)
    ng = len(A_PATTERNS)

    def body(q_ref, kp, kc, kn, vp, vc, vn, o_ref, lse_ref):
        b = pl.program_id(1)
        jq = b * QBLK + lax.broadcasted_iota(jnp.int32, (QBLK, 3 * QBLK), 0)
        jk = (b - 1) * QBLK + lax.broadcasted_iota(jnp.int32, (QBLK, 3 * QBLK), 1)
        rel = jnp.abs(jk - jq)
        mask = (rel <= n_side) & (jk >= 0) & (jk < L)
        dist = rel.astype(F32) * float(dil)
        for hh in range(HPG):
            sl = slice(hh * HEAD, (hh + 1) * HEAD)
            k = _cat3((kp, kc, kn), sl)
            v = _cat3((vp, vc, vn), sl)
            s = _dot(q_ref[:, sl], k, NT) * SCALE - float(SLOPES[g * HPG + hh]) * dist
            s = jnp.where(mask, s, NEG)
            m = jnp.max(s, axis=1, keepdims=True)
            p = jnp.exp(s - m)
            l = jnp.sum(p, axis=1, keepdims=True)
            o_ref[:, sl] = _dot(p, v) / l
            lse_ref[:, sl] = jnp.broadcast_to(m + jnp.log(l), (QBLK, HEAD))

    q_spec = pl.BlockSpec((QBLK, GW), lambda r, b: (b, r * cb + g))
    k_specs = _band_specs(nb, lambda r: r * cb + ng + g)
    v_specs = _band_specs(nb, lambda r: r * cb + 2 * ng + g)
    o_spec = pl.BlockSpec((QBLK, GW), lambda r, b: (b, r))
    shape = jax.ShapeDtypeStruct((L, dil * GW), F32)
    o, lse = pl.pallas_call(
        body, name=name, grid=(dil, nb), in_specs=[q_spec] + k_specs + v_specs,
        out_specs=[o_spec, o_spec], out_shape=[shape, shape],
        compiler_params=_cp("parallel", "parallel"))(pv, pv, pv, pv, pv, pv, pv)
    return o.reshape(S, GW), lse.reshape(S, GW)


def _attn_merge(name, outs, lses, tr=256):
    S = outs[0].shape[0]
    tr = _tile(S, tr)
    ng = len(outs)

    def body(*refs):
        o_refs, l_refs = refs[:ng], refs[ng:2 * ng]
        tok_ref, lse_ref = refs[-2], refs[-1]
        ls = [r[...] for r in l_refs]
        m = functools.reduce(jnp.maximum, ls)
        es = [jnp.exp(l - m) for l in ls]
        tot = functools.reduce(lambda a, b: a + b, es)
        acc = None
        for e, o_ref in zip(es, o_refs):
            term = (e / tot) * o_ref[...]
            acc = term if acc is None else acc + term
        tok_ref[...] = acc
        lse_ref[...] = m + jnp.log(tot)

    row = pl.BlockSpec((tr, GW), lambda i: (i, 0))
    shape = jax.ShapeDtypeStruct((S, GW), F32)
    return pl.pallas_call(
        body, name=name, grid=(S // tr,), in_specs=[row] * (2 * ng), out_specs=[row, row],
        out_shape=[shape, shape], compiler_params=_cp("parallel"))(*outs, *lses)


def _attn_bwd(name, proj, g, dtok_src, dtok_blk, tok, lse):
    window, dil = A_PATTERNS[g]
    n_side = (window // 2) // dil
    S, C = proj.shape
    L = S // dil
    nb = L // QBLK
    cb = C // GW
    ng = len(A_PATTERNS)
    pv = proj.reshape(L, dil * C)
    dcb = dtok_src.shape[1] // GW
    dv_ = dtok_src.reshape(L, dil * dtok_src.shape[1])
    ov = tok.reshape(L, dil * GW)
    lv = lse.reshape(L, dil * GW)

    def body(qp, qc, qn, kp, kc, kn, vp, vc, vn, dop, doc, don, op, oc, on, lp, lc, ln,
             dq_ref, dk_ref, dv_ref):
        b = pl.program_id(1)
        jq = b * QBLK + lax.broadcasted_iota(jnp.int32, (QBLK, 3 * QBLK), 0)
        jk = (b - 1) * QBLK + lax.broadcasted_iota(jnp.int32, (QBLK, 3 * QBLK), 1)
        rel = jnp.abs(jk - jq)
        mask = (rel <= n_side) & (jk >= 0) & (jk < L)
        dist = rel.astype(F32) * float(dil)
        jq3 = (b - 1) * QBLK + lax.broadcasted_iota(jnp.int32, (3 * QBLK, QBLK), 0)
        jk1 = b * QBLK + lax.broadcasted_iota(jnp.int32, (3 * QBLK, QBLK), 1)
        rel3 = jnp.abs(jk1 - jq3)
        mask3 = (rel3 <= n_side) & (jq3 >= 0) & (jq3 < L)
        dist3 = rel3.astype(F32) * float(dil)
        for hh in range(HPG):
            sl = slice(hh * HEAD, (hh + 1) * HEAD)
            one = slice(hh * HEAD, hh * HEAD + 1)
            slope = float(SLOPES[g * HPG + hh])
            q = qc[:, sl]
            do = doc[:, sl]
            k3 = _cat3((kp, kc, kn), sl)
            v3 = _cat3((vp, vc, vn), sl)
            delta = jnp.sum(do * oc[:, sl], axis=1, keepdims=True)
            s = _dot(q, k3, NT) * SCALE - slope * dist
            p = jnp.where(mask, jnp.exp(s - lc[:, one]), 0.0)
            ds = p * (_dot(do, v3, NT) - delta)
            dq_ref[:, sl] = _dot(ds, k3) * SCALE

            q3 = _cat3((qp, qc, qn), sl)
            do3 = _cat3((dop, doc, don), sl)
            o3 = _cat3((op, oc, on), sl)
            lse3 = _cat3((lp, lc, ln), sl)[:, :1]
            delta3 = jnp.sum(do3 * o3, axis=1, keepdims=True)
            k = kc[:, sl]
            v = vc[:, sl]
            s3 = _dot(q3, k, NT) * SCALE - slope * dist3
            p3 = jnp.where(mask3, jnp.exp(s3 - lse3), 0.0)
            ds3 = p3 * (_dot(do3, v, NT) - delta3)
            dv_ref[:, sl] = _dot(p3, do3, TN)
            dk_ref[:, sl] = _dot(ds3, q3, TN) * SCALE

    specs = (_band_specs(nb, lambda r: r * cb + g) + _band_specs(nb, lambda r: r * cb + ng + g)
             + _band_specs(nb, lambda r: r * cb + 2 * ng + g)
             + _band_specs(nb, lambda r: r * dcb + dtok_blk)
             + _band_specs(nb, lambda r: r) + _band_specs(nb, lambda r: r))
    o_spec = pl.BlockSpec((QBLK, GW), lambda r, b: (b, r))
    shape = jax.ShapeDtypeStruct((L, dil * GW), F32)
    outs = pl.pallas_call(
        body, name=name, grid=(dil, nb), in_specs=specs, out_specs=[o_spec] * 3, out_shape=[shape] * 3,
        compiler_params=_cp("parallel", "parallel"))(*([pv] * 9 + [dv_] * 3 + [ov] * 3 + [lv] * 3))
    return [o.reshape(S, GW) for o in outs]


def _mem_fwd(name, proj, q_blk, kv, tq=256):
    S = proj.shape[0]
    M = kv.shape[0]
    tq = _tile(S, tq)

    def body(q_ref, kv_ref, o_ref):
        for hh in range(HPG):
            sl = slice(hh * HEAD, (hh + 1) * HEAD)
            k = kv_ref[:, sl]
            v = kv_ref[:, GW + hh * HEAD:GW + (hh + 1) * HEAD]
            s = _dot(q_ref[:, sl], k, NT) * SCALE
            m = jnp.max(s, axis=1, keepdims=True)
            p = jnp.exp(s - m)
            p = p / jnp.sum(p, axis=1, keepdims=True)
            o_ref[:, sl] = _dot(p, v)

    return pl.pallas_call(
        body, name=name, grid=(S // tq,),
        in_specs=[pl.BlockSpec((tq, GW), lambda i: (i, q_blk)), pl.BlockSpec((M, 2 * GW), lambda i: (0, 0))],
        out_specs=pl.BlockSpec((tq, GW), lambda i: (i, 0)),
        out_shape=jax.ShapeDtypeStruct((S, GW), F32), compiler_params=_cp("parallel"))(proj, kv)


def _mem_bwd(name, proj, q_blk, kv, dcat, do_blk, tq=256):
    S = proj.shape[0]
    M = kv.shape[0]
    tq = _tile(S, tq)

    def body(q_ref, kv_ref, do_ref, dq_ref, dkv_ref):
        i = pl.program_id(0)
        for hh in range(HPG):
            sl = slice(hh * HEAD, (hh + 1) * HEAD)
            vsl = slice(GW + hh * HEAD, GW + (hh + 1) * HEAD)
            q = q_ref[:, sl]
            do = do_ref[:, sl]
            k = kv_ref[:, sl]
            v = kv_ref[:, vsl]
            s = _dot(q, k, NT) * SCALE
            m = jnp.max(s, axis=1, keepdims=True)
            p = jnp.exp(s - m)
            p = p / jnp.sum(p, axis=1, keepdims=True)
            dp = _dot(do, v, NT)
            ds = p * (dp - jnp.sum(dp * p, axis=1, keepdims=True))
            dq_ref[:, sl] = _dot(ds, k) * SCALE
            dk = _dot(ds, q, TN) * SCALE
            dvv = _dot(p, do, TN)

            @pl.when(i == 0)
            def _():
                dkv_ref[:, sl] = dk
                dkv_ref[:, vsl] = dvv

            @pl.when(i > 0)
            def _():
                dkv_ref[:, sl] += dk
                dkv_ref[:, vsl] += dvv

    return pl.pallas_call(
        body, name=name, grid=(S // tq,),
        in_specs=[pl.BlockSpec((tq, GW), lambda i: (i, q_blk)), pl.BlockSpec((M, 2 * GW), lambda i: (0, 0)),
                  pl.BlockSpec((tq, GW), lambda i: (i, do_blk))],
        out_specs=[pl.BlockSpec((tq, GW), lambda i: (i, 0)), pl.BlockSpec((M, 2 * GW), lambda i: (0, 0))],
        out_shape=[jax.ShapeDtypeStruct((S, GW), F32), jax.ShapeDtypeStruct((M, 2 * GW), F32)],
        compiler_params=_cp("arbitrary"))(proj, kv, dcat)


_RSQRT2 = float(1.0 / np.sqrt(2.0))
_RSQRT2PI = float(1.0 / np.sqrt(2.0 * np.pi))


def _gelu(x):
    return 0.5 * x * (1.0 + lax.erf(x * _RSQRT2))


def _gelu_grad(x):
    return 0.5 * (1.0 + lax.erf(x * _RSQRT2)) + x * jnp.exp(-0.5 * x * x) * _RSQRT2PI


def _sgu_fwd(name, proj, gv, w_s, bias_t):
    S = proj.shape[0]
    nch = S // HEAD

    def body(u_ref, v_ref, gv_ref, ws_ref, b_ref, o_ref):
        v = _gelu(v_ref[...])
        r = lax.rsqrt(jnp.mean(v * v, axis=-1, keepdims=True) + EPS)
        vn = v * r * gv_ref[...]
        for gg in range(B_GROUPS):
            sl = slice(gg * HEAD, (gg + 1) * HEAD)
            mixed = _dot(ws_ref[gg], vn[:, sl]) + b_ref[:, gg:gg + 1]
            o_ref[:, sl] = _gelu(u_ref[:, sl]) * mixed

    return pl.pallas_call(
        body, name=name, grid=(nch,),
        in_specs=[pl.BlockSpec((HEAD, B_W), lambda c: (c, 0)), pl.BlockSpec((HEAD, B_W), lambda c: (c, 1)),
                  pl.BlockSpec((1, B_W), lambda c: (0, 0)),
                  pl.BlockSpec((B_GROUPS, HEAD, HEAD), lambda c: (0, 0, 0)),
                  pl.BlockSpec((HEAD, B_GROUPS), lambda c: (0, 0))],
        out_specs=pl.BlockSpec((HEAD, B_W), lambda c: (c, 0)),
        out_shape=jax.ShapeDtypeStruct((S, B_W), F32),
        compiler_params=_cp("parallel"))(proj, proj, gv.reshape(1, B_W), w_s, bias_t)


def _sgu_bwd(name, proj, gv, w_s, bias_t, dcat):
    S = proj.shape[0]
    nch = S // HEAD

    def body(u_ref, v_ref, gv_ref, ws_ref, b_ref, dt_ref, du_ref, dvp_ref, dgv_ref, dws_ref, db_ref, dvn_ref):
        c = pl.program_id(0)
        vpre = v_ref[...]
        v = _gelu(vpre)
        r = lax.rsqrt(jnp.mean(v * v, axis=-1, keepdims=True) + EPS)
        vh = v * r
        gvv = gv_ref[...]
        vn = vh * gvv
        for gg in range(B_GROUPS):
            sl = slice(gg * HEAD, (gg + 1) * HEAD)
            upre = u_ref[:, sl]
            dt = dt_ref[:, sl]
            vng = vn[:, sl]
            mixed = _dot(ws_ref[gg], vng) + b_ref[:, gg:gg + 1]
            du_ref[:, sl] = dt * mixed * _gelu_grad(upre)
            dmix = dt * _gelu(upre)
            dvn_ref[:, sl] = _dot(ws_ref[gg], dmix, TN)
            dws = _dot(dmix, vng, NT)
            dbs = jnp.sum(dmix, axis=1, keepdims=True)

            @pl.when(c == 0)
            def _():
                dws_ref[gg] = dws
                db_ref[:, gg:gg + 1] = dbs

            @pl.when(c > 0)
            def _():
                dws_ref[gg] += dws
                db_ref[:, gg:gg + 1] += dbs

        dvn = dvn_ref[...]
        dgp = jnp.sum(dvn * vh, axis=0, keepdims=True)
        dvh = dvn * gvv
        dv = r * (dvh - vh * jnp.mean(dvh * vh, axis=-1, keepdims=True))
        dvp_ref[...] = dv * _gelu_grad(vpre)

        @pl.when(c == 0)
        def _():
            dgv_ref[...] = dgp

        @pl.when(c > 0)
        def _():
            dgv_ref[...] += dgp

    blk = lambda j: pl.BlockSpec((HEAD, B_W), lambda c: (c, j))
    vec = pl.BlockSpec((1, B_W), lambda c: (0, 0))
    ws_spec = pl.BlockSpec((B_GROUPS, HEAD, HEAD), lambda c: (0, 0, 0))
    b_spec = pl.BlockSpec((HEAD, B_GROUPS), lambda c: (0, 0))
    du, dvp, dgv, dws, db = pl.pallas_call(
        body, name=name, grid=(nch,),
        in_specs=[blk(0), blk(1), vec, ws_spec, b_spec, blk(0)],
        out_specs=[blk(0), blk(0), vec, ws_spec, b_spec],
        out_shape=[jax.ShapeDtypeStruct((S, B_W), F32), jax.ShapeDtypeStruct((S, B_W), F32),
                   jax.ShapeDtypeStruct((1, B_W), F32), jax.ShapeDtypeStruct((B_GROUPS, HEAD, HEAD), F32),
                   jax.ShapeDtypeStruct((HEAD, B_GROUPS), F32)],
        scratch_shapes=[pltpu.VMEM((HEAD, B_W), F32)],
        compiler_params=_cp("arbitrary"))(proj, proj, gv.reshape(1, B_W), w_s, bias_t, dcat)
    return du, dvp, dgv, dws, db


def _shift_down(a, row):
    return jnp.where(row == 0, 0.0, pltpu.roll(a, 1, 0))


def _shift_up(a, row):
    n = a.shape[0]
    return jnp.where(row == n - 1, 0.0, pltpu.roll(a, n - 1, 0))


def _conv(a, w, b, row):
    return _shift_down(a, row) * w[0:1] + a * w[1:2] + _shift_up(a, row) * w[2:3] + b


def _conv_fwd(name, a3, cw, cb, tc=256):
    _, S, FF = a3.shape
    tc = _tile(FF, tc)

    def body(a_ref, w_ref, b_ref, o_ref):
        row = lax.broadcasted_iota(jnp.int32, (S, tc), 0)
        cg = _conv(a_ref[0], w_ref[0], b_ref[0], row)
        cv = _conv(a_ref[1], w_ref[1], b_ref[1], row)
        o_ref[...] = (_gelu(cg) * cv).astype(o_ref.dtype)

    return pl.pallas_call(
        body, name=name, grid=(FF // tc,),
        in_specs=[pl.BlockSpec((2, S, tc), lambda j: (0, 0, j)), pl.BlockSpec((2, 3, tc), lambda j: (0, 0, j)),
                  pl.BlockSpec((2, 1, tc), lambda j: (0, 0, j))],
        out_specs=pl.BlockSpec((S, tc), lambda j: (0, j)),
        out_shape=jax.ShapeDtypeStruct((S, FF), BF16), compiler_params=_cp("parallel"))(a3, cw, cb)


def _conv_bwd(name, a3, cw, cb, dact, tc=128):
    _, S, FF = a3.shape
    tc = _tile(FF, tc)

    def body(a_ref, w_ref, b_ref, d_ref, da_ref, dw_ref, db_ref):
        row = lax.broadcasted_iota(jnp.int32, (S, tc), 0)
        ag, av = a_ref[0], a_ref[1]
        wg, wv = w_ref[0], w_ref[1]
        cg = _conv(ag, wg, b_ref[0], row)
        cv = _conv(av, wv, b_ref[1], row)
        d = d_ref[...]
        dcs = (d * cv * _gelu_grad(cg), d * _gelu(cg))
        for h, (dc, a, w) in enumerate(zip(dcs, (ag, av), (wg, wv))):
            da_ref[h] = _shift_up(dc, row) * w[0:1] + dc * w[1:2] + _shift_down(dc, row) * w[2:3]
            dw_ref[h, 0:1, :] = jnp.sum(dc * _shift_down(a, row), axis=0, keepdims=True)
            dw_ref[h, 1:2, :] = jnp.sum(dc * a, axis=0, keepdims=True)
            dw_ref[h, 2:3, :] = jnp.sum(dc * _shift_up(a, row), axis=0, keepdims=True)
            db_ref[h] = jnp.sum(dc, axis=0, keepdims=True)

    a_spec = pl.BlockSpec((2, S, tc), lambda j: (0, 0, j))
    w_spec = pl.BlockSpec((2, 3, tc), lambda j: (0, 0, j))
    b_spec = pl.BlockSpec((2, 1, tc), lambda j: (0, 0, j))
    return pl.pallas_call(
        body, name=name, grid=(FF // tc,),
        in_specs=[a_spec, w_spec, b_spec, pl.BlockSpec((S, tc), lambda j: (0, j))],
        out_specs=[a_spec, w_spec, b_spec],
        out_shape=[jax.ShapeDtypeStruct((2, S, FF), F32), jax.ShapeDtypeStruct((2, 3, FF), F32),
                   jax.ShapeDtypeStruct((2, 1, FF), F32)],
        compiler_params=_cp("parallel"))(a3, cw, cb, dact)


_HBM = pl.BlockSpec(memory_space=pltpu.HBM)


def _position():
    return lax.axis_index("x"), lax.axis_index("y"), lax.axis_index("c")


def _all_gather(name, x, layer=None):
    block = x.shape if layer is None else x.shape[1:]

    def body(x_ref, out_ref, send_sems, recv_sems, local_sem):
        px, py, pc = _position()
        me, sibling = (px, py, pc), (px, py, 1 - pc)
        chips = [(1 - px, py), (px, 1 - py), (1 - px, 1 - py)]
        src = x_ref if layer is None else x_ref.at[layer]

        def slot(qx, qy, qc):
            return out_ref.at[4 * qx + 2 * qy + qc]

        def copy(k, blockpos, to, from_src=False):
            return pltpu.make_async_remote_copy(
                src_ref=src if from_src else slot(*blockpos), dst_ref=slot(*blockpos),
                send_sem=send_sems.at[k], recv_sem=recv_sems.at[k], device_id=to, device_id_type=MESH)

        mine = pltpu.make_async_copy(src, slot(*me), local_sem)
        mine.start()
        first = [copy(0, me, sibling, True)]
        first += [copy(1 + j, me, (*chip, pc), True) for j, chip in enumerate(chips)]
        for cp in first:
            cp.start()
        passed = [copy(4 + j, (*chip, pc), sibling) for j, chip in enumerate(chips)]
        for j, chip in enumerate(chips):
            copy(1 + j, (*chip, pc), me).wait_recv()
            passed[j].start()
        copy(0, sibling, me).wait_recv()
        for j, chip in enumerate(chips):
            copy(4 + j, (*chip, 1 - pc), me).wait_recv()
        for cp in first + passed:
            cp.wait_send()
        mine.wait()

    return pl.pallas_call(
        body, name=name, in_specs=[_HBM], out_specs=_HBM,
        out_shape=jax.ShapeDtypeStruct((N_DEV,) + tuple(block), x.dtype),
        scratch_shapes=[pltpu.SemaphoreType.DMA((7,)), pltpu.SemaphoreType.DMA((7,)), pltpu.SemaphoreType.DMA(())],
    )(x)


_SEM = pl.BlockSpec(memory_space=pltpu.SEMAPHORE)
_EFFECT = pltpu.SideEffectType.DATAFLOW_SIDE_EFFECTING
_FLIPS = ((1, 0), (0, 1), (1, 1))


def _split_start(name, bufs, ncopy, plan):
    n = len(bufs)

    def body(*refs):
        ins = refs[:n]
        send_sems, recv_sems, token = refs[n], refs[n + 1], refs[-1]
        for i, (src, dst, to) in enumerate(plan(ins)):
            pltpu.make_async_remote_copy(src_ref=src, dst_ref=dst, send_sem=send_sems.at[i], recv_sem=recv_sems.at[i],
                                         device_id=to, device_id_type=MESH).start()
        token[...] = jnp.zeros_like(token)

    outs = pl.pallas_call(
        body, name=name,
        out_shape=(pltpu.SemaphoreType.DMA((ncopy,)), pltpu.SemaphoreType.DMA((ncopy,)),
                   *[pltpu.HBM(b.shape, b.dtype) for b in bufs], jax.ShapeDtypeStruct((8, 128), F32)),
        in_specs=[_HBM] * n,
        out_specs=(_SEM, _SEM, *([_HBM] * n), pl.BlockSpec(memory_space=pltpu.VMEM)),
        input_output_aliases={i: 2 + i for i in range(n)},
        compiler_params=pltpu.CompilerParams(has_side_effects=_EFFECT),
    )(*[pltpu.with_memory_space_constraint(b, pltpu.HBM) for b in bufs])
    return outs[0], outs[1], list(outs[2:2 + n]), outs[-1]


def _split_wait(name, bufs, send_sems, recv_sems, plan, after):
    n = len(bufs)

    def body(*refs):
        ins = refs[:n]
        ssem, rsem = refs[n], refs[n + 1]
        for i, (src, dst, to) in enumerate(plan(ins)):
            cp = pltpu.make_async_remote_copy(src_ref=src, dst_ref=dst, send_sem=ssem.at[i], recv_sem=rsem.at[i],
                                              device_id=to, device_id_type=MESH)
            cp.wait_send()
            cp.wait_recv()

    outs = pl.pallas_call(
        body, name=name, out_shape=tuple(pltpu.HBM(b.shape, b.dtype) for b in bufs),
        in_specs=[_HBM] * n + [_SEM, _SEM, pl.BlockSpec(memory_space=pl.ANY)], out_specs=tuple([_HBM] * n),
        input_output_aliases={i: i for i in range(n)},
        compiler_params=pltpu.CompilerParams(has_side_effects=_EFFECT),
    )(*bufs, send_sems, recv_sems, after)
    return list(outs)


def _gather_plan(refs):
    px, py, pc = _position()
    me = 4 * px + 2 * py + pc
    targets = [(px, py, 1 - pc), (1 - px, py, pc), (px, 1 - py, pc), (1 - px, 1 - py, pc)]
    return [(r.at[me], r.at[me], to) for r in refs for to in targets]


def _gather_forward(name, lands):
    n = len(lands)

    def body(*refs):
        outs, send_sems, recv_sems = refs[n:2 * n], refs[2 * n], refs[2 * n + 1]
        px, py, pc = _position()
        copies = []
        for w, ref in enumerate(outs):
            for j, (fx, fy) in enumerate(_FLIPS):
                qx = 1 - px if fx else px
                qy = 1 - py if fy else py
                mine, theirs = 4 * qx + 2 * qy + pc, 4 * qx + 2 * qy + (1 - pc)
                send = pltpu.make_async_remote_copy(
                    src_ref=ref.at[mine], dst_ref=ref.at[mine], send_sem=send_sems.at[3 * w + j],
                    recv_sem=recv_sems.at[3 * w + j], device_id=(px, py, 1 - pc), device_id_type=MESH)
                recv = pltpu.make_async_remote_copy(
                    src_ref=ref.at[theirs], dst_ref=ref.at[theirs], send_sem=send_sems.at[3 * w + j],
                    recv_sem=recv_sems.at[3 * w + j], device_id=(px, py, 1 - pc), device_id_type=MESH)
                copies.append((send, recv))
        for send, _ in copies:
            send.start()
        for send, recv in copies:
            send.wait_send()
            recv.wait_recv()

    outs = pl.pallas_call(
        body, name=name, in_specs=[_HBM] * n, out_specs=tuple([_HBM] * n),
        out_shape=tuple(jax.ShapeDtypeStruct(l.shape, l.dtype) for l in lands),
        input_output_aliases={i: i for i in range(n)},
        scratch_shapes=[pltpu.SemaphoreType.DMA((3 * n,)), pltpu.SemaphoreType.DMA((3 * n,))])(*lands)
    return list(outs)


def _pair_plan(n):
    def plan(refs):
        px, py, pc = _position()
        return [(refs[w].at[2 * k + (1 - pc)], refs[n + w].at[k], (px, py, 1 - pc)) for w in range(n) for k in range(4)]
    return plan


def _chip_plan(n):
    def plan(refs):
        px, py, pc = _position()
        out = []
        for w in range(n):
            for j, (fx, fy) in enumerate(_FLIPS):
                qx = 1 - px if fx else px
                qy = 1 - py if fy else py
                out.append((refs[w].at[2 * qx + qy], refs[n + w].at[j], (qx, qy, pc)))
        return out
    return plan


def _cast_place(name, dev, w, layer):
    nl, R, C = w.shape
    tr = _row_tile(R, C)

    def body(dev_ref, w_ref, o_ref):
        o_ref[...] = w_ref[...].astype(o_ref.dtype)

    return pl.pallas_call(
        body, name=name,
        grid_spec=pltpu.PrefetchScalarGridSpec(
            num_scalar_prefetch=1, grid=(R // tr,),
            in_specs=[pl.BlockSpec((None, tr, C), lambda i, d: (layer, i, 0))],
            out_specs=pl.BlockSpec((None, tr, C), lambda i, d: (d[0], i, 0))),
        out_shape=jax.ShapeDtypeStruct((N_DEV, R, C), BF16), compiler_params=_cp("parallel"))(dev, w)


def _pair_sum(name, core, dw, recv):
    _, R, C = dw.shape
    tr = _row_tile(R, C)
    dw4 = dw.reshape(4, 2, R, C)

    def body(core_ref, a_ref, b_ref, o_ref):
        o_ref[...] = (a_ref[...] + b_ref[...]).astype(o_ref.dtype)

    return pl.pallas_call(
        body, name=name,
        grid_spec=pltpu.PrefetchScalarGridSpec(
            num_scalar_prefetch=1, grid=(4, R // tr),
            in_specs=[pl.BlockSpec((None, None, tr, C), lambda k, i, c_ref: (k, c_ref[0], i, 0)),
                      pl.BlockSpec((None, tr, C), lambda k, i, c_ref: (k, i, 0))],
            out_specs=pl.BlockSpec((None, tr, C), lambda k, i, c_ref: (k, i, 0))),
        out_shape=jax.ShapeDtypeStruct((4, R, C), BF16),
        compiler_params=_cp("parallel", "parallel"))(core, dw4, recv)


def _adamw_math(w, g, m, v):
    m = ADAM_B1 * m + (1.0 - ADAM_B1) * g
    v = ADAM_B2 * v + (1.0 - ADAM_B2) * (g * g)
    m_hat = m / (1.0 - ADAM_B1 ** ADAM_STEP)
    v_hat = v / (1.0 - ADAM_B2 ** ADAM_STEP)
    delta = -ADAM_LR * (m_hat / (jnp.sqrt(v_hat) + ADAM_EPS) + ADAM_WD * w)
    return delta, m, v


def _adamw_shard(name, chip, layer, w, m, v, p, recv, prev):
    nl, R, C = w.shape
    tr = _row_tile(R, C)
    n_prev = 0 if prev is None else 4

    def body(chip_ref, w_ref, m_ref, v_ref, p_ref, r_ref, *rest):
        g_ref, d_ref, nm_ref, nv_ref = rest[n_prev:]
        g = p_ref[...].astype(F32)
        for j in range(3):
            g = g + r_ref[j].astype(F32)
        delta, nm, nv = _adamw_math(w_ref[...], g, m_ref[...], v_ref[...])
        g_ref[...] = g
        d_ref[...] = delta
        nm_ref[...] = nm
        nv_ref[...] = nv

    lay = pl.BlockSpec((None, tr, C), lambda i, c_ref: (layer, i, 0))
    in_specs = [lay, lay, lay,
                pl.BlockSpec((None, tr, C), lambda i, c_ref: (c_ref[0], i, 0)),
                pl.BlockSpec((3, tr, C), lambda i, c_ref: (0, i, 0))]
    in_specs += [pl.BlockSpec(memory_space=pl.ANY)] * n_prev
    shape = jax.ShapeDtypeStruct((nl, R, C), F32)
    ins = [chip, w, m, v, p, recv] + ([] if prev is None else list(prev))
    return pl.pallas_call(
        body, name=name,
        grid_spec=pltpu.PrefetchScalarGridSpec(
            num_scalar_prefetch=1, grid=(R // tr,), in_specs=in_specs, out_specs=[lay] * 4),
        out_shape=[shape] * 4,
        input_output_aliases={6 + j: j for j in range(n_prev)},
        compiler_params=_cp("parallel"))(*ins)


def _sum_slots(name, parts, tr=512):
    n, R, C = parts.shape
    tr = _tile(R, tr)

    def body(p_ref, o_ref):
        acc = p_ref[0]
        for j in range(1, n):
            acc = acc + p_ref[j]
        o_ref[...] = acc

    return pl.pallas_call(
        body, name=name, grid=(R // tr,),
        in_specs=[pl.BlockSpec((n, tr, C), lambda i: (0, i, 0))],
        out_specs=pl.BlockSpec((tr, C), lambda i: (i, 0)),
        out_shape=jax.ShapeDtypeStruct((R, C), F32), compiler_params=_cp("parallel"))(parts)


def _adamw_flat(name, w, g, m, v, tr=512):
    R, C = w.shape
    tr = _tile(R, tr)

    def body(w_ref, g_ref, m_ref, v_ref, d_ref, nm_ref, nv_ref):
        delta, nm, nv = _adamw_math(w_ref[...], g_ref[...], m_ref[...], v_ref[...])
        d_ref[...] = delta
        nm_ref[...] = nm
        nv_ref[...] = nv

    row = pl.BlockSpec((tr, C), lambda i: (i, 0))
    shape = jax.ShapeDtypeStruct((R, C), F32)
    return pl.pallas_call(
        body, name=name, grid=(R // tr,), in_specs=[row] * 4, out_specs=[row] * 3, out_shape=[shape] * 3,
        compiler_params=_cp("parallel"))(w, g, m, v)


_PACK_ROWS = 512


def _pack(arrs):
    flat = jnp.concatenate([a.reshape(-1) for a in arrs])
    unit = _PACK_ROWS * 128
    pad = (-flat.shape[0]) % unit
    return jnp.pad(flat, (0, pad)).reshape(-1, 128)


def _unpack(packed, shapes):
    flat = packed.reshape(-1)
    outs, off = [], 0
    for s in shapes:
        n = int(np.prod(s))
        outs.append(flat[off:off + n].reshape(s))
        off += n
    return outs


def kernel(x, mem, mix_norm_g, ffn_norm_g, mem_norm_g, w_mem_kv, a_w_in, a_w_out, b_w_in, b_v_norm_g, b_w_s, b_s_bias, b_w_out, ffn_w_up, ffn_conv_w, ffn_conv_b, ffn_w_down, final_norm_g, loss_target, m_mix_norm_g, m_ffn_norm_g, m_mem_norm_g, m_w_mem_kv, m_a_w_in, m_a_w_out, m_b_w_in, m_b_v_norm_g, m_b_w_s, m_b_s_bias, m_b_w_out, m_ffn_w_up, m_ffn_conv_w, m_ffn_conv_b, m_ffn_w_down, m_final_norm_g, v_mix_norm_g, v_ffn_norm_g, v_mem_norm_g, v_w_mem_kv, v_a_w_in, v_a_w_out, v_b_w_in, v_b_v_norm_g, v_b_w_s, v_b_s_bias, v_b_w_out, v_ffn_w_up, v_ffn_conv_w, v_ffn_conv_b, v_ffn_w_down, v_final_norm_g):
    px, py, pc = _position()
    dev = 4 * px + 2 * py + pc
    core = jnp.reshape(pc, (1,)).astype(jnp.int32)
    chip = jnp.reshape(2 * px + py, (1,)).astype(jnp.int32)

    x0 = x[0]
    mem0 = mem[0]
    tgt = loss_target[0]
    S, D = x0.shape
    depth = mix_norm_g.shape[0]
    FF = ffn_w_down.shape[1] * N_DEV
    a_in = a_w_in.shape[2] * N_DEV
    b_in = b_w_in.shape[2] * N_DEV
    a_q_blk = (a_in - GW) // GW
    b_q_blk = (b_in - GW) // GW

    stacks = {"kv": (w_mem_kv, m_w_mem_kv, v_w_mem_kv), "ain": (a_w_in, m_a_w_in, v_a_w_in),
              "aout": (a_w_out, m_a_w_out, v_a_w_out), "bin": (b_w_in, m_b_w_in, v_b_w_in),
              "bout": (b_w_out, m_b_w_out, v_b_w_out), "up": (ffn_w_up, m_ffn_w_up, v_ffn_w_up),
              "down": (ffn_w_down, m_ffn_w_down, v_ffn_w_down)}
    dev1 = jnp.reshape(dev, (1,)).astype(jnp.int32)

    def groups_of(i):
        j = i // 2
        mix = [("kv", i), ("ain", j), ("aout", j)] if i % 2 == 0 else [("kv", i), ("bin", j), ("bout", j)]
        return mix, [("up", i), ("down", i)]

    gather = []
    start_tokens = []
    for i in range(depth):
        for half, members in zip("mf", groups_of(i)):
            lands = [_cast_place(f"place_{t}{l}", dev1, stacks[t][0], l) for t, l in members]
            ssem, rsem, lands, tok = _split_start(f"ag_start_{half}{i}", lands, 4 * len(lands), _gather_plan)
            gather.append((f"{half}{i}", members, lands, ssem, rsem))
            start_tokens.append(tok)

    def gathered(k, after):
        gname, members, lands, ssem, rsem = gather[k]
        lands = _split_wait(f"ag_wait_{gname}", lands, ssem, rsem, _gather_plan, after)
        lands = _gather_forward(f"ag_pass_{gname}", lands)
        out = {}
        for (t, l), land in zip(members, lands):
            if t == "kv":
                out["kv"] = land.reshape(D, 2 * GW)
            elif t in ("ain", "aout", "up"):
                out[{"ain": "in", "aout": "out", "up": "up"}[t]] = land
            elif t == "bin":
                out["in"] = jnp.transpose(land, (1, 0, 2)).reshape(D, b_in)
            elif t == "bout":
                out["out"] = land.reshape(B_W + GW, D)
            else:
                out["down"] = land.reshape(FF, D)
        return out

    small_local = _pack([ffn_conv_w, b_v_norm_g])
    small_all = _all_gather("ag_small", small_local)
    cw_parts, gv_parts = [], []
    for d in range(N_DEV):
        cw_d, gv_d = _unpack(small_all[d], [ffn_conv_w.shape, b_v_norm_g.shape])
        cw_parts.append(cw_d)
        gv_parts.append(gv_d)
    conv_w_full = jnp.concatenate(cw_parts, axis=-1)
    gv_full = jnp.concatenate(gv_parts, axis=-1)

    def conv_params(i):
        cw = conv_w_full[i].reshape(3, 2, FF).transpose(1, 0, 2)
        cb = ffn_conv_b[i].reshape(2, 1, FF)
        return cw, cb

    saved = []
    W = []
    xc = x0
    for i in range(depth):
        j = i // 2
        lw = gathered(2 * i, xc)
        sv = {"x0": xc}
        h1 = _rms_fwd(f"mixnorm{i}", xc, mix_norm_g[i], deps=start_tokens if i == 0 else ())
        memn = _rms_fwd(f"memnorm{i}", mem0, mem_norm_g[i])
        kv = _mm_full(f"kvproj{i}", memn, lw["kv"])
        if i % 2 == 0:
            proj = _mm_gcols(f"ain{i}", h1, lw["in"])
            outs, lses = [], []
            for g in range(len(A_PATTERNS)):
                o, l = _attn_fwd(f"attn{i}_{g}", proj, g)
                outs.append(o)
                lses.append(l)
            tok, lse = _attn_merge(f"merge{i}", outs, lses)
            mo = _mem_fwd(f"memattn{i}", proj, a_q_blk, kv)
            cat = jnp.concatenate([tok.astype(BF16), mo.astype(BF16)], axis=1)
            x1 = _mm_gcols(f"aout{i}", cat, lw["out"], res=xc)
            sv.update(tok=tok, lse=lse)
        else:
            proj = _mm_full(f"bin{i}", h1, lw["in"])
            bias_t = b_s_bias[j].T
            tok = _sgu_fwd(f"sgu{i}", proj, gv_full[j], b_w_s[j], bias_t)
            mo = _mem_fwd(f"memattn{i}", proj, b_q_blk, kv)
            cat = jnp.concatenate([tok.astype(BF16), mo.astype(BF16)], axis=1)
            x1 = _mm_full(f"bout{i}", cat, lw["out"], res=xc)
        lw.update(gathered(2 * i + 1, x1))
        W.append(lw)
        h2 = _rms_fwd(f"ffnnorm{i}", x1, ffn_norm_g[i])
        cw, cb = conv_params(i)
        a3 = _mm_gcols(f"up{i}", h2, lw["up"], split_out=True, tm=512)
        act = _conv_fwd(f"conv{i}", a3, cw, cb)
        x2 = _mm_full(f"down{i}", act, lw["down"], res=x1, tm=512, tn=1024, tk=FF // 4)
        sv.update(h1=h1, memn=memn, kv=kv, proj=proj, cat=cat, x1=x1, h2=h2, a3=a3, act=act)
        saved.append(sv)
        xc = x2

    dx, dg_final, sq = _final("final", xc, tgt, final_norm_g)
    loss = lax.psum(sq[0, 0] * (0.5 / D), AXES)

    chain = {}

    def pair_begin(gname, members, dws):
        n = len(dws)
        recvs = [lax.empty((4,) + dw.shape[1:], F32) for dw in dws]
        ssem, rsem, bufs, tok = _split_start(f"rs_pair_start_{gname}", dws + recvs, 4 * n, _pair_plan(n))
        return dict(name=gname, members=members, n=n, bufs=bufs, sems=(ssem, rsem)), tok

    def pair_end_chip_begin(st, after):
        n, gname = st["n"], st["name"]
        bufs = _split_wait(f"rs_pair_wait_{gname}", st["bufs"], *st["sems"], _pair_plan(n), after)
        ps = [_pair_sum(f"rs_sum_{t}{l}", core, bufs[w], bufs[n + w]) for w, (t, l) in enumerate(st["members"])]
        recvs = [lax.empty((3,) + p.shape[1:], BF16) for p in ps]
        ssem, rsem, bufs, tok = _split_start(f"rs_chip_start_{gname}", ps + recvs, 3 * n, _chip_plan(n))
        return dict(name=gname, members=st["members"], n=n, bufs=bufs, sems=(ssem, rsem)), tok

    def chip_end_update(st, after):
        n = st["n"]
        bufs = _split_wait(f"rs_chip_wait_{st['name']}", st["bufs"], *st["sems"], _chip_plan(n), after)
        for w, (t, l) in enumerate(st["members"]):
            wst, mst, vst = stacks[t]
            chain[t] = _adamw_shard(f"adamw_{t}{l}", chip, l, wst, mst, vst, bufs[w], bufs[n + w], chain.get(t))

    pipe = {"pair": None, "chip": None, "deps": []}

    def half_done(gname, members, dws, after):
        new_pair, tok = pair_begin(gname, members, dws)
        toks, new_chip = [tok], None
        if pipe["pair"] is not None:
            new_chip, tok = pair_end_chip_begin(pipe["pair"], after)
            toks.append(tok)
        if pipe["chip"] is not None:
            chip_end_update(pipe["chip"], after)
        pipe.update(pair=new_pair, chip=new_chip, deps=toks)

    big = {k: [None] * n for k, n in (("kv", depth), ("ain", depth // 2 + depth % 2), ("aout", depth // 2 + depth % 2),
                                      ("bin", depth // 2), ("bout", depth // 2), ("up", depth), ("down", depth))}
    dg_mix, dg_ffn, dg_mem = [None] * depth, [None] * depth, [None] * depth
    d_conv_w, d_conv_b = [None] * depth, [None] * depth
    d_gv, d_ws, d_sb = [None] * (depth // 2), [None] * (depth // 2), [None] * (depth // 2)
    for i in reversed(range(depth)):
        j = i // 2
        lw, sv = W[i], saved[i]
        cw, cb = conv_params(i)
        mix_members, ffn_members = groups_of(i)
        dact = _mm_dx_full(f"ddown{i}", dx, lw["down"], tm=512, tko=FF // 4, tc=D, deps=pipe["deps"])
        big["down"][i] = _mm_dw(f"wdown{i}", sv["act"], dx, deps=pipe["deps"]).reshape(N_DEV, FF // N_DEV, D)
        da3, dcw, dcb = _conv_bwd(f"dconv{i}", sv["a3"], cw, cb, dact)
        d_conv_w[i] = dcw.transpose(1, 0, 2).reshape(3, 2 * FF)
        d_conv_b[i] = dcb.reshape(2 * FF)
        dh2 = _mm_dx_gcols(f"dup{i}", da3, lw["up"], split_in=True, tm=512)
        big["up"][i] = _mm_dw_gcols(f"wup{i}", sv["h2"], da3, N_DEV, split_in=True, tko=512, ts=1024)
        dx1, dg_ffn[i] = _rms_bwd(f"dffnnorm{i}", dh2, sv["x1"], ffn_norm_g[i], dx)
        half_done(f"f{i}", ffn_members, [big["up"][i], big["down"][i]], dx1)
        if i % 2 == 0:
            dcat = _mm_dx_gcols(f"daout{i}", dx1, lw["out"], deps=pipe["deps"])
            big["aout"][j] = _mm_dw_gcols(f"waout{i}", sv["cat"], dx1, N_DEV, deps=pipe["deps"])
            parts = [None] * 9
            for g in range(len(A_PATTERNS)):
                dq, dk, dv = _attn_bwd(f"dattn{i}_{g}", sv["proj"], g, dcat, 0, sv["tok"], sv["lse"])
                parts[g], parts[3 + g], parts[6 + g] = dq, dk, dv
            dqm, dkv = _mem_bwd(f"dmemattn{i}", sv["proj"], a_q_blk, sv["kv"], dcat, 1)
            dproj = jnp.concatenate(parts + [dqm], axis=1)
            dh1 = _mm_dx_gcols(f"dain{i}", dproj, lw["in"])
            big["ain"][j] = _mm_dw_gcols(f"wain{i}", sv["h1"], dproj, N_DEV)
        else:
            dcat = _mm_dx_full(f"dbout{i}", dx1, lw["out"], tm=1024, deps=pipe["deps"])
            big["bout"][j] = _mm_dw(f"wbout{i}", sv["cat"], dx1, deps=pipe["deps"]).reshape(
                N_DEV, (B_W + GW) // N_DEV, D)
            bias_t = b_s_bias[j].T
            du, dvp, dgv, dws, dbt = _sgu_bwd(f"dsgu{i}", sv["proj"], gv_full[j], b_w_s[j], bias_t, dcat)
            d_gv[j], d_ws[j], d_sb[j] = dgv.reshape(B_W), dws, dbt.T
            dqm, dkv = _mem_bwd(f"dmemattn{i}", sv["proj"], b_q_blk, sv["kv"], dcat, B_W // GW)
            dproj = jnp.concatenate([du, dvp, dqm], axis=1)
            dh1 = _mm_dx_full(f"dbin{i}", dproj, lw["in"], tc=b_in // 2)
            dwin = _mm_dw(f"wbin{i}", sv["h1"], dproj, tko=1024, tn=512)
            big["bin"][j] = dwin.reshape(D, N_DEV, b_in // N_DEV).transpose(1, 0, 2)
        big["kv"][i] = _mm_dw(f"wkv{i}", sv["memn"], dkv, tko=1024).reshape(N_DEV, D // N_DEV, 2 * GW)
        dmemn = _mm_dx_full(f"dkvproj{i}", dkv, lw["kv"], tko=1024)
        _, dg_mem[i] = _rms_bwd(f"dmemnorm{i}", dmemn, mem0, mem_norm_g[i])
        dx, dg_mix[i] = _rms_bwd(f"dmixnorm{i}", dh1, sv["x0"], mix_norm_g[i], dx1)
        half_done(f"m{i}", mix_members, [big[t][l] for t, l in mix_members], dx)
    grad_x = dx[None]

    last_chip, _ = pair_end_chip_begin(pipe["pair"], dx)
    chip_end_update(pipe["chip"], dx)
    chip_end_update(last_chip, dx)

    res = {}
    for tag, name in (("kv", "w_mem_kv"), ("ain", "a_w_in"), ("aout", "a_w_out"), ("bin", "b_w_in"),
                      ("bout", "b_w_out"), ("up", "ffn_w_up"), ("down", "ffn_w_down")):
        res[name] = list(chain[tag])

    small_grads = [jnp.concatenate(dg_mix), jnp.concatenate(dg_ffn), jnp.concatenate(dg_mem),
                   jnp.stack(d_ws), jnp.stack(d_sb), jnp.stack(d_conv_b), dg_final.reshape(D),
                   jnp.stack(d_gv), jnp.stack(d_conv_w)]
    g_small = _sum_slots("small_sum", _all_gather("ag_smallgrad", _pack(small_grads)))
    rep_shapes = [mix_norm_g.shape, ffn_norm_g.shape, mem_norm_g.shape, b_w_s.shape, b_s_bias.shape,
                  ffn_conv_b.shape, final_norm_g.shape]
    full_gv_shape = (b_v_norm_g.shape[0], B_W)
    full_cw_shape = (depth, 3, 2 * FF)
    g_list = _unpack(g_small, rep_shapes + [full_gv_shape, full_cw_shape])
    g_gv = lax.dynamic_slice_in_dim(g_list[7], dev * b_v_norm_g.shape[1], b_v_norm_g.shape[1], axis=1)
    g_cw = lax.dynamic_slice_in_dim(g_list[8], dev * ffn_conv_w.shape[2], ffn_conv_w.shape[2], axis=2)
    g_all = g_list[:7] + [g_gv, g_cw]
    names = ["mix_norm_g", "ffn_norm_g", "mem_norm_g", "b_w_s", "b_s_bias", "ffn_conv_b", "final_norm_g",
             "b_v_norm_g", "ffn_conv_w"]
    ws = [mix_norm_g, ffn_norm_g, mem_norm_g, b_w_s, b_s_bias, ffn_conv_b, final_norm_g, b_v_norm_g, ffn_conv_w]
    ms = [m_mix_norm_g, m_ffn_norm_g, m_mem_norm_g, m_b_w_s, m_b_s_bias, m_ffn_conv_b, m_final_norm_g,
          m_b_v_norm_g, m_ffn_conv_w]
    vs = [v_mix_norm_g, v_ffn_norm_g, v_mem_norm_g, v_b_w_s, v_b_s_bias, v_ffn_conv_b, v_final_norm_g,
          v_b_v_norm_g, v_ffn_conv_w]
    shapes = [w.shape for w in ws]
    d_p, m_p, v_p = _adamw_flat("adamw_small", _pack(ws), _pack(g_all), _pack(ms), _pack(vs))
    for n, g, d, nm, nv in zip(names, g_all, _unpack(d_p, shapes), _unpack(m_p, shapes), _unpack(v_p, shapes)):
        res[n] = [g, d, nm, nv]

    order = ["mix_norm_g", "ffn_norm_g", "mem_norm_g", "w_mem_kv", "a_w_in", "a_w_out", "b_w_in", "b_v_norm_g",
             "b_w_s", "b_s_bias", "b_w_out", "ffn_w_up", "ffn_conv_w", "ffn_conv_b", "ffn_w_down", "final_norm_g"]
    return (loss, grad_x, *[res[n][0] for n in order], *[res[n][1] for n in order],
            *[res[n][2] for n in order], *[res[n][3] for n in order])
```

```python
import functools

import numpy as np
import jax
import jax.numpy as jnp
from jax import lax
from jax.experimental import pallas as pl
from jax.experimental.pallas import tpu as pltpu

F32 = jnp.float32
BF16 = jnp.bfloat16
MESH = pl.DeviceIdType.MESH
AXES = ("x", "y", "c")
N_DEV = 8

EPS = 1e-6
NEG = -1e30
HEAD = 128
HPG = 4
GW = HPG * HEAD
A_PATTERNS = ((128, 1), (512, 4), (2048, 16))
A_HEADS = HPG * len(A_PATTERNS)
QBLK = 128
B_GROUPS = 12
B_W = B_GROUPS * HEAD
SLOPES = (2.0 ** (-8.0 * (np.arange(A_HEADS) + 1) / A_HEADS)).astype(np.float32)
SCALE = HEAD ** -0.5

ADAM_LR = 0.001
ADAM_B1 = 0.9
ADAM_B2 = 0.999
ADAM_EPS = 1e-08
ADAM_WD = 0.01
ADAM_STEP = 10

V7X_VMEM_LIMIT = 50 * 1024 * 1024

NN = (((1,), (0,)), ((), ()))
NT = (((1,), (1,)), ((), ()))
TN = (((0,), (0,)), ((), ()))


def _cp(*sem):
    return pltpu.CompilerParams(dimension_semantics=sem, vmem_limit_bytes=V7X_VMEM_LIMIT)


def _dot(a, b, dims=NN):
    return lax.dot_general(a, b, dims, preferred_element_type=F32)


def _tile(n, pref):
    t = min(n, pref)
    assert n % t == 0, (n, pref)
    return t


def _row_tile(rows, cols):
    best = None
    for t in range(16, rows + 1, 16):
        if rows % t == 0 and t * cols * 4 <= (1 << 20):
            best = t
    if best is None:
        best = rows
    return best


_DEP = pl.BlockSpec((8, 128), lambda *_: (0, 0))


def _matmul(name, dims, grid, a, a_spec, b, b_spec, out_shape, o_spec, tile, res=None, res_spec=None, deps=()):
    nk = grid[2]
    has_res = res is not None

    def body(*refs):
        a_ref, b_ref = refs[0], refs[1]
        r_ref = refs[2] if has_res else None
        o_ref, acc_ref = refs[-2], refs[-1]
        part = _dot(a_ref[...].astype(BF16), b_ref[...].astype(BF16), dims)

        def finish(val):
            if has_res:
                val = val + r_ref[...]
            o_ref[...] = val.astype(o_ref.dtype)

        if nk == 1:
            finish(part)
        else:
            k = pl.program_id(2)

            @pl.when(k == 0)
            def _():
                acc_ref[...] = part

            @pl.when(k > 0)
            def _():
                acc_ref[...] += part

            @pl.when(k == nk - 1)
            def _():
                finish(acc_ref[...])

    ins = [a, b] + ([res] if has_res else []) + list(deps)
    specs = [a_spec, b_spec] + ([res_spec] if has_res else []) + [_DEP] * len(deps)
    return pl.pallas_call(
        body, name=name, grid=grid, in_specs=specs, out_specs=o_spec, out_shape=out_shape,
        scratch_shapes=[pltpu.VMEM(tile if nk > 1 else (8, 128), F32)],
        compiler_params=_cp("parallel", "parallel", "arbitrary"))(*ins)


def _mm_full(name, a, w, res=None, tm=1024, tn=512, tk=2048):
    M, K = a.shape
    N = w.shape[1]
    tm, tn, tk = _tile(M, tm), _tile(N, tn), _tile(K, tk)
    return _matmul(
        name, NN, (N // tn, M // tm, K // tk),
        a, pl.BlockSpec((tm, tk), lambda j, i, k: (i, k)),
        w, pl.BlockSpec((tk, tn), lambda j, i, k: (k, j)),
        jax.ShapeDtypeStruct((M, N), F32), pl.BlockSpec((tm, tn), lambda j, i, k: (i, j)), (tm, tn),
        res, pl.BlockSpec((tm, tn), lambda j, i, k: (i, j)))


def _mm_gcols(name, a, wg, res=None, split_out=False, tm=1024):
    M, K = a.shape
    G, _, Nl = wg.shape
    tm = _tile(M, tm)
    hg = G // 2
    if split_out:
        shape = jax.ShapeDtypeStruct((2, M, hg * Nl), F32)
        o_spec = pl.BlockSpec((None, tm, Nl), lambda g, i, k: (g // hg, i, g % hg))
    else:
        shape = jax.ShapeDtypeStruct((M, G * Nl), F32)
        o_spec = pl.BlockSpec((tm, Nl), lambda g, i, k: (i, g))
    return _matmul(
        name, NN, (G, M // tm, 1),
        a, pl.BlockSpec((tm, K), lambda g, i, k: (i, 0)),
        wg, pl.BlockSpec((None, K, Nl), lambda g, i, k: (g, 0, 0)),
        shape, o_spec, (tm, Nl),
        res, pl.BlockSpec((tm, Nl), lambda g, i, k: (i, g)))


def _mm_dx_full(name, dy, w, tm=512, tko=512, tc=2048, deps=()):
    M, N = dy.shape
    K = w.shape[0]
    tm, tko, tc = _tile(M, tm), _tile(K, tko), _tile(N, tc)
    return _matmul(
        name, NT, (K // tko, M // tm, N // tc),
        dy, pl.BlockSpec((tm, tc), lambda j, i, k: (i, k)),
        w, pl.BlockSpec((tko, tc), lambda j, i, k: (j, k)),
        jax.ShapeDtypeStruct((M, K), F32), pl.BlockSpec((tm, tko), lambda j, i, k: (i, j)), (tm, tko), deps=deps)


def _mm_dx_gcols(name, dy, wg, split_in=False, tm=1024, tko=1024, deps=()):
    G, K, Nl = wg.shape
    M = dy.shape[-2]
    tm, tko = _tile(M, tm), _tile(K, tko)
    hg = G // 2
    if split_in:
        dy_spec = pl.BlockSpec((None, tm, Nl), lambda j, i, g: (g // hg, i, g % hg))
    else:
        dy_spec = pl.BlockSpec((tm, Nl), lambda j, i, g: (i, g))
    return _matmul(
        name, NT, (K // tko, M // tm, G),
        dy, dy_spec,
        wg, pl.BlockSpec((None, tko, Nl), lambda j, i, g: (g, j, 0)),
        jax.ShapeDtypeStruct((M, K), F32), pl.BlockSpec((tm, tko), lambda j, i, g: (i, j)), (tm, tko), deps=deps)


def _lhs_of_dw(a, a_t, ts, tko, index):
    if a_t:
        return NN, pl.BlockSpec((tko, ts), lambda *ids: index(*ids))
    return TN, pl.BlockSpec((ts, tko), lambda *ids: index(*ids)[::-1])


def _mm_dw(name, a, dy, tko=512, tn=1024, ts=2048, deps=(), a_t=False):
    K1, S = a.shape if a_t else a.shape[::-1]
    N = dy.shape[1]
    tko, tn, ts = _tile(K1, tko), _tile(N, tn), _tile(S, ts)
    dims, a_spec = _lhs_of_dw(a, a_t, ts, tko, lambda i, j, k: (j, k))
    return _matmul(
        name, dims, (N // tn, K1 // tko, S // ts),
        a, a_spec,
        dy, pl.BlockSpec((ts, tn), lambda i, j, k: (k, i)),
        jax.ShapeDtypeStruct((K1, N), F32), pl.BlockSpec((tko, tn), lambda i, j, k: (j, i)), (tko, tn), deps=deps)


def _mm_dw_gcols(name, a, dy, G, split_in=False, tko=1024, ts=2048, deps=(), a_t=False):
    K1, S = a.shape if a_t else a.shape[::-1]
    Nl = (dy.shape[-1] * (2 if split_in else 1)) // G
    tko, ts = _tile(K1, tko), _tile(S, ts)
    hg = G // 2
    dims, a_spec = _lhs_of_dw(a, a_t, ts, tko, lambda g, j, k: (j, k))
    if split_in:
        dy_spec = pl.BlockSpec((None, ts, Nl), lambda g, j, k: (g // hg, k, g % hg))
    else:
        dy_spec = pl.BlockSpec((ts, Nl), lambda g, j, k: (k, g))
    return _matmul(
        name, dims, (G, K1 // tko, S // ts),
        a, a_spec,
        dy, dy_spec,
        jax.ShapeDtypeStruct((G, K1, Nl), F32), pl.BlockSpec((None, tko, Nl), lambda g, j, k: (g, j, 0)),
        (tko, Nl), deps=deps)


def _rms_fwd(name, x, g, tr=256, deps=(), with_t=False):
    S, D = x.shape
    tr = _tile(S, tr)

    def body(x_ref, g_ref, *rest):
        xf = x_ref[...]
        r = lax.rsqrt(jnp.mean(xf * xf, axis=-1, keepdims=True) + EPS)
        y = xf * r * g_ref[...]
        if with_t:
            rest[-2][...] = y.astype(BF16)
            rest[-1][...] = y.T.astype(BF16)
        else:
            rest[-1][...] = y.astype(BF16)

    row = pl.BlockSpec((tr, D), lambda i: (i, 0))
    out_specs, out_shape = row, jax.ShapeDtypeStruct((S, D), BF16)
    if with_t:
        out_specs = [row, pl.BlockSpec((D, tr), lambda i: (0, i))]
        out_shape = [out_shape, jax.ShapeDtypeStruct((D, S), BF16)]
    return pl.pallas_call(
        body, name=name, grid=(S // tr,),
        in_specs=[row, pl.BlockSpec((1, D), lambda i: (0, 0))] + [_DEP] * len(deps),
        out_specs=out_specs, out_shape=out_shape, compiler_params=_cp("parallel"))(x, g.reshape(1, D), *deps)


def _rms_bwd(name, dh, x, g, dres=None, tr=256):
    S, D = x.shape
    tr = _tile(S, tr)
    has_res = dres is not None

    def body(*refs):
        dh_ref, x_ref, g_ref = refs[:3]
        dres_ref = refs[3] if has_res else None
        dx_ref, dg_ref = refs[-2], refs[-1]
        xf = x_ref[...]
        r = lax.rsqrt(jnp.mean(xf * xf, axis=-1, keepdims=True) + EPS)
        xh = xf * r
        dhv = dh_ref[...]
        dxh = dhv * g_ref[...]
        dx = r * (dxh - xh * jnp.mean(dxh * xh, axis=-1, keepdims=True))
        if has_res:
            dx = dx + dres_ref[...]
        dx_ref[...] = dx
        part = jnp.sum(dhv * xh, axis=0, keepdims=True)
        i = pl.program_id(0)

        @pl.when(i == 0)
        def _():
            dg_ref[...] = part

        @pl.when(i > 0)
        def _():
            dg_ref[...] += part

    row = pl.BlockSpec((tr, D), lambda i: (i, 0))
    vec = pl.BlockSpec((1, D), lambda i: (0, 0))
    ins = [dh, x, g.reshape(1, D)] + ([dres] if has_res else [])
    return pl.pallas_call(
        body, name=name, grid=(S // tr,),
        in_specs=[row, row, vec] + ([row] if has_res else []),
        out_specs=[row, vec],
        out_shape=[jax.ShapeDtypeStruct((S, D), F32), jax.ShapeDtypeStruct((1, D), F32)],
        compiler_params=_cp("arbitrary"))(*ins)


def _final(name, x, tgt, g, tr=256):
    S, D = x.shape
    tr = _tile(S, tr)

    def body(x_ref, t_ref, g_ref, dx_ref, dg_ref, loss_ref):
        xf = x_ref[...]
        gv = g_ref[...]
        r = lax.rsqrt(jnp.mean(xf * xf, axis=-1, keepdims=True) + EPS)
        xh = xf * r
        err = xh * gv - t_ref[...]
        sq = jnp.sum(jnp.sum(err * err, axis=1, keepdims=True), axis=0, keepdims=True)
        dy = err * (1.0 / D)
        dxh = dy * gv
        dx_ref[...] = r * (dxh - xh * jnp.mean(dxh * xh, axis=-1, keepdims=True))
        part = jnp.sum(dy * xh, axis=0, keepdims=True)
        lpart = jnp.broadcast_to(sq, (8, 128))
        i = pl.program_id(0)

        @pl.when(i == 0)
        def _():
            dg_ref[...] = part
            loss_ref[...] = lpart

        @pl.when(i > 0)
        def _():
            dg_ref[...] += part
            loss_ref[...] += lpart

    row = pl.BlockSpec((tr, D), lambda i: (i, 0))
    vec = pl.BlockSpec((1, D), lambda i: (0, 0))
    return pl.pallas_call(
        body, name=name, grid=(S // tr,), in_specs=[row, row, vec],
        out_specs=[row, vec, pl.BlockSpec((8, 128), lambda i: (0, 0))],
        out_shape=[jax.ShapeDtypeStruct((S, D), F32), jax.ShapeDtypeStruct((1, D), F32),
                   jax.ShapeDtypeStruct((8, 128), F32)],
        compiler_params=_cp("arbitrary"))(x, tgt, g.reshape(1, D))


def _band_specs(nb, col_of):
    prev = pl.BlockSpec((QBLK, GW), lambda r, b: (jnp.maximum(b - 1, 0), col_of(r)))
    cur = pl.BlockSpec((QBLK, GW), lambda r, b: (b, col_of(r)))
    nxt = pl.BlockSpec((QBLK, GW), lambda r, b: (jnp.minimum(b + 1, nb - 1), col_of(r)))
    return [prev, cur, nxt]


def _cat3(refs, sl):
    return jnp.concatenate([ref[:, sl] for ref in refs], axis=0)


def _attn_fwd(name, proj, g):
    window, dil = A_PATTERNS[g]
    n_side = (window // 2) // dil
    S, C = proj.shape
    L = S // dil
    nb = L // QBLK
    cb = C // GW
    pv = proj.reshape(L, dil * C)
    ng = len(A_PATTERNS)

    def body(q_ref, kp, kc, kn, vp, vc, vn, o_ref, lse_ref):
        b = pl.program_id(1)
        jq = b * QBLK + lax.broadcasted_iota(jnp.int32, (QBLK, 3 * QBLK), 0)
        jk = (b - 1) * QBLK + lax.broadcasted_iota(jnp.int32, (QBLK, 3 * QBLK), 1)
        rel = jnp.abs(jk - jq)
        mask = (rel <= n_side) & (jk >= 0) & (jk < L)
        dist = rel.astype(F32) * float(dil)
        for hh in range(HPG):
            sl = slice(hh * HEAD, (hh + 1) * HEAD)
            k = _cat3((kp, kc, kn), sl)
            v = _cat3((vp, vc, vn), sl)
            s = _dot(q_ref[:, sl], k, NT) * SCALE - float(SLOPES[g * HPG + hh]) * dist
            s = jnp.where(mask, s, NEG)
            m = jnp.max(s, axis=1, keepdims=True)
            p = jnp.exp(s - m)
            l = jnp.sum(p, axis=1, keepdims=True)
            o_ref[:, sl] = _dot(p, v) / l
            lse_ref[:, sl] = jnp.broadcast_to(m + jnp.log(l), (QBLK, HEAD))

    q_spec = pl.BlockSpec((QBLK, GW), lambda r, b: (b, r * cb + g))
    k_specs = _band_specs(nb, lambda r: r * cb + ng + g)
    v_specs = _band_specs(nb, lambda r: r * cb + 2 * ng + g)
    o_spec = pl.BlockSpec((QBLK, GW), lambda r, b: (b, r))
    shape = jax.ShapeDtypeStruct((L, dil * GW), F32)
    o, lse = pl.pallas_call(
        body, name=name, grid=(dil, nb), in_specs=[q_spec] + k_specs + v_specs,
        out_specs=[o_spec, o_spec], out_shape=[shape, shape],
        compiler_params=_cp("parallel", "parallel"))(pv, pv, pv, pv, pv, pv, pv)
    return o.reshape(S, GW), lse.reshape(S, GW)


def _attn_merge(name, outs, lses, tr=256):
    S = outs[0].shape[0]
    tr = _tile(S, tr)
    ng = len(outs)

    def body(*refs):
        o_refs, l_refs = refs[:ng], refs[ng:2 * ng]
        tok_ref, lse_ref = refs[-2], refs[-1]
        ls = [r[...] for r in l_refs]
        m = functools.reduce(jnp.maximum, ls)
        es = [jnp.exp(l - m) for l in ls]
        tot = functools.reduce(lambda a, b: a + b, es)
        acc = None
        for e, o_ref in zip(es, o_refs):
            term = (e / tot) * o_ref[...]
            acc = term if acc is None else acc + term
        tok_ref[...] = acc
        lse_ref[...] = m + jnp.log(tot)

    row = pl.BlockSpec((tr, GW), lambda i: (i, 0))
    shape = jax.ShapeDtypeStruct((S, GW), F32)
    return pl.pallas_call(
        body, name=name, grid=(S // tr,), in_specs=[row] * (2 * ng), out_specs=[row, row],
        out_shape=[shape, shape], compiler_params=_cp("parallel"))(*outs, *lses)


def _attn_bwd(name, proj, g, dtok_src, dtok_blk, tok, lse):
    window, dil = A_PATTERNS[g]
    n_side = (window // 2) // dil
    S, C = proj.shape
    L = S // dil
    nb = L // QBLK
    cb = C // GW
    ng = len(A_PATTERNS)
    pv = proj.reshape(L, dil * C)
    dcb = dtok_src.shape[1] // GW
    dv_ = dtok_src.reshape(L, dil * dtok_src.shape[1])
    ov = tok.reshape(L, dil * GW)
    lv = lse.reshape(L, dil * GW)

    def body(qp, qc, qn, kp, kc, kn, vp, vc, vn, dop, doc, don, op, oc, on, lp, lc, ln,
             dq_ref, dk_ref, dv_ref):
        b = pl.program_id(1)
        jq = b * QBLK + lax.broadcasted_iota(jnp.int32, (QBLK, 3 * QBLK), 0)
        jk = (b - 1) * QBLK + lax.broadcasted_iota(jnp.int32, (QBLK, 3 * QBLK), 1)
        rel = jnp.abs(jk - jq)
        mask = (rel <= n_side) & (jk >= 0) & (jk < L)
        dist = rel.astype(F32) * float(dil)
        jq3 = (b - 1) * QBLK + lax.broadcasted_iota(jnp.int32, (3 * QBLK, QBLK), 0)
        jk1 = b * QBLK + lax.broadcasted_iota(jnp.int32, (3 * QBLK, QBLK), 1)
        rel3 = jnp.abs(jk1 - jq3)
        mask3 = (rel3 <= n_side) & (jq3 >= 0) & (jq3 < L)
        dist3 = rel3.astype(F32) * float(dil)
        for hh in range(HPG):
            sl = slice(hh * HEAD, (hh + 1) * HEAD)
            one = slice(hh * HEAD, hh * HEAD + 1)
            slope = float(SLOPES[g * HPG + hh])
            q = qc[:, sl]
            do = doc[:, sl]
            k3 = _cat3((kp, kc, kn), sl)
            v3 = _cat3((vp, vc, vn), sl)
            delta = jnp.sum(do * oc[:, sl], axis=1, keepdims=True)
            s = _dot(q, k3, NT) * SCALE - slope * dist
            p = jnp.where(mask, jnp.exp(s - lc[:, one]), 0.0)
            ds = p * (_dot(do, v3, NT) - delta)
            dq_ref[:, sl] = _dot(ds, k3) * SCALE

            q3 = _cat3((qp, qc, qn), sl)
            do3 = _cat3((dop, doc, don), sl)
            o3 = _cat3((op, oc, on), sl)
            lse3 = _cat3((lp, lc, ln), sl)[:, :1]
            delta3 = jnp.sum(do3 * o3, axis=1, keepdims=True)
            k = kc[:, sl]
            v = vc[:, sl]
            s3 = _dot(q3, k, NT) * SCALE - slope * dist3
            p3 = jnp.where(mask3, jnp.exp(s3 - lse3), 0.0)
            ds3 = p3 * (_dot(do3, v, NT) - delta3)
            dv_ref[:, sl] = _dot(p3, do3, TN)
            dk_ref[:, sl] = _dot(ds3, q3, TN) * SCALE

    specs = (_band_specs(nb, lambda r: r * cb + g) + _band_specs(nb, lambda r: r * cb + ng + g)
             + _band_specs(nb, lambda r: r * cb + 2 * ng + g)
             + _band_specs(nb, lambda r: r * dcb + dtok_blk)
             + _band_specs(nb, lambda r: r) + _band_specs(nb, lambda r: r))
    o_spec = pl.BlockSpec((QBLK, GW), lambda r, b: (b, r))
    shape = jax.ShapeDtypeStruct((L, dil * GW), F32)
    outs = pl.pallas_call(
        body, name=name, grid=(dil, nb), in_specs=specs, out_specs=[o_spec] * 3, out_shape=[shape] * 3,
        compiler_params=_cp("parallel", "parallel"))(*([pv] * 9 + [dv_] * 3 + [ov] * 3 + [lv] * 3))
    return [o.reshape(S, GW) for o in outs]


def _mem_fwd(name, proj, q_blk, kv, tq=256):
    S = proj.shape[0]
    M = kv.shape[0]
    tq = _tile(S, tq)

    def body(q_ref, kv_ref, o_ref):
        for hh in range(HPG):
            sl = slice(hh * HEAD, (hh + 1) * HEAD)
            k = kv_ref[:, sl]
            v = kv_ref[:, GW + hh * HEAD:GW + (hh + 1) * HEAD]
            s = _dot(q_ref[:, sl], k, NT) * SCALE
            m = jnp.max(s, axis=1, keepdims=True)
            p = jnp.exp(s - m)
            p = p / jnp.sum(p, axis=1, keepdims=True)
            o_ref[:, sl] = _dot(p, v)

    return pl.pallas_call(
        body, name=name, grid=(S // tq,),
        in_specs=[pl.BlockSpec((tq, GW), lambda i: (i, q_blk)), pl.BlockSpec((M, 2 * GW), lambda i: (0, 0))],
        out_specs=pl.BlockSpec((tq, GW), lambda i: (i, 0)),
        out_shape=jax.ShapeDtypeStruct((S, GW), F32), compiler_params=_cp("parallel"))(proj, kv)


def _mem_bwd(name, proj, q_blk, kv, dcat, do_blk, tq=256):
    S = proj.shape[0]
    M = kv.shape[0]
    tq = _tile(S, tq)

    def body(q_ref, kv_ref, do_ref, dq_ref, dkv_ref):
        i = pl.program_id(0)
        for hh in range(HPG):
            sl = slice(hh * HEAD, (hh + 1) * HEAD)
            vsl = slice(GW + hh * HEAD, GW + (hh + 1) * HEAD)
            q = q_ref[:, sl]
            do = do_ref[:, sl]
            k = kv_ref[:, sl]
            v = kv_ref[:, vsl]
            s = _dot(q, k, NT) * SCALE
            m = jnp.max(s, axis=1, keepdims=True)
            p = jnp.exp(s - m)
            p = p / jnp.sum(p, axis=1, keepdims=True)
            dp = _dot(do, v, NT)
            ds = p * (dp - jnp.sum(dp * p, axis=1, keepdims=True))
            dq_ref[:, sl] = _dot(ds, k) * SCALE
            dk = _dot(ds, q, TN) * SCALE
            dvv = _dot(p, do, TN)

            @pl.when(i == 0)
            def _():
                dkv_ref[:, sl] = dk
                dkv_ref[:, vsl] = dvv

            @pl.when(i > 0)
            def _():
                dkv_ref[:, sl] += dk
                dkv_ref[:, vsl] += dvv

    return pl.pallas_call(
        body, name=name, grid=(S // tq,),
        in_specs=[pl.BlockSpec((tq, GW), lambda i: (i, q_blk)), pl.BlockSpec((M, 2 * GW), lambda i: (0, 0)),
                  pl.BlockSpec((tq, GW), lambda i: (i, do_blk))],
        out_specs=[pl.BlockSpec((tq, GW), lambda i: (i, 0)), pl.BlockSpec((M, 2 * GW), lambda i: (0, 0))],
        out_shape=[jax.ShapeDtypeStruct((S, GW), F32), jax.ShapeDtypeStruct((M, 2 * GW), F32)],
        compiler_params=_cp("arbitrary"))(proj, kv, dcat)


_RSQRT2 = float(1.0 / np.sqrt(2.0))
_RSQRT2PI = float(1.0 / np.sqrt(2.0 * np.pi))


def _gelu(x):
    return 0.5 * x * (1.0 + lax.erf(x * _RSQRT2))


def _gelu_grad(x):
    return 0.5 * (1.0 + lax.erf(x * _RSQRT2)) + x * jnp.exp(-0.5 * x * x) * _RSQRT2PI


def _sgu_fwd(name, proj, gv, w_s, bias_t):
    S = proj.shape[0]
    nch = S // HEAD

    def body(u_ref, v_ref, gv_ref, ws_ref, b_ref, o_ref):
        v = _gelu(v_ref[...])
        r = lax.rsqrt(jnp.mean(v * v, axis=-1, keepdims=True) + EPS)
        vn = v * r * gv_ref[...]
        for gg in range(B_GROUPS):
            sl = slice(gg * HEAD, (gg + 1) * HEAD)
            mixed = _dot(ws_ref[gg], vn[:, sl]) + b_ref[:, gg:gg + 1]
            o_ref[:, sl] = _gelu(u_ref[:, sl]) * mixed

    return pl.pallas_call(
        body, name=name, grid=(nch,),
        in_specs=[pl.BlockSpec((HEAD, B_W), lambda c: (c, 0)), pl.BlockSpec((HEAD, B_W), lambda c: (c, 1)),
                  pl.BlockSpec((1, B_W), lambda c: (0, 0)),
                  pl.BlockSpec((B_GROUPS, HEAD, HEAD), lambda c: (0, 0, 0)),
                  pl.BlockSpec((HEAD, B_GROUPS), lambda c: (0, 0))],
        out_specs=pl.BlockSpec((HEAD, B_W), lambda c: (c, 0)),
        out_shape=jax.ShapeDtypeStruct((S, B_W), F32),
        compiler_params=_cp("parallel"))(proj, proj, gv.reshape(1, B_W), w_s, bias_t)


def _sgu_bwd(name, proj, gv, w_s, bias_t, dcat):
    S = proj.shape[0]
    nch = S // HEAD

    def body(u_ref, v_ref, gv_ref, ws_ref, b_ref, dt_ref, du_ref, dvp_ref, dgv_ref, dws_ref, db_ref, dvn_ref):
        c = pl.program_id(0)
        vpre = v_ref[...]
        v = _gelu(vpre)
        r = lax.rsqrt(jnp.mean(v * v, axis=-1, keepdims=True) + EPS)
        vh = v * r
        gvv = gv_ref[...]
        vn = vh * gvv
        for gg in range(B_GROUPS):
            sl = slice(gg * HEAD, (gg + 1) * HEAD)
            upre = u_ref[:, sl]
            dt = dt_ref[:, sl]
            vng = vn[:, sl]
            mixed = _dot(ws_ref[gg], vng) + b_ref[:, gg:gg + 1]
            du_ref[:, sl] = dt * mixed * _gelu_grad(upre)
            dmix = dt * _gelu(upre)
            dvn_ref[:, sl] = _dot(ws_ref[gg], dmix, TN)
            dws = _dot(dmix, vng, NT)
            dbs = jnp.sum(dmix, axis=1, keepdims=True)

            @pl.when(c == 0)
            def _():
                dws_ref[gg] = dws
                db_ref[:, gg:gg + 1] = dbs

            @pl.when(c > 0)
            def _():
                dws_ref[gg] += dws
                db_ref[:, gg:gg + 1] += dbs

        dvn = dvn_ref[...]
        dgp = jnp.sum(dvn * vh, axis=0, keepdims=True)
        dvh = dvn * gvv
        dv = r * (dvh - vh * jnp.mean(dvh * vh, axis=-1, keepdims=True))
        dvp_ref[...] = dv * _gelu_grad(vpre)

        @pl.when(c == 0)
        def _():
            dgv_ref[...] = dgp

        @pl.when(c > 0)
        def _():
            dgv_ref[...] += dgp

    blk = lambda j: pl.BlockSpec((HEAD, B_W), lambda c: (c, j))
    vec = pl.BlockSpec((1, B_W), lambda c: (0, 0))
    ws_spec = pl.BlockSpec((B_GROUPS, HEAD, HEAD), lambda c: (0, 0, 0))
    b_spec = pl.BlockSpec((HEAD, B_GROUPS), lambda c: (0, 0))
    du, dvp, dgv, dws, db = pl.pallas_call(
        body, name=name, grid=(nch,),
        in_specs=[blk(0), blk(1), vec, ws_spec, b_spec, blk(0)],
        out_specs=[blk(0), blk(0), vec, ws_spec, b_spec],
        out_shape=[jax.ShapeDtypeStruct((S, B_W), F32), jax.ShapeDtypeStruct((S, B_W), F32),
                   jax.ShapeDtypeStruct((1, B_W), F32), jax.ShapeDtypeStruct((B_GROUPS, HEAD, HEAD), F32),
                   jax.ShapeDtypeStruct((HEAD, B_GROUPS), F32)],
        scratch_shapes=[pltpu.VMEM((HEAD, B_W), F32)],
        compiler_params=_cp("arbitrary"))(proj, proj, gv.reshape(1, B_W), w_s, bias_t, dcat)
    return du, dvp, dgv, dws, db


def _shift_down(a, row):
    return jnp.where(row == 0, 0.0, pltpu.roll(a, 1, 0))


def _shift_up(a, row):
    n = a.shape[0]
    return jnp.where(row == n - 1, 0.0, pltpu.roll(a, n - 1, 0))


def _conv(a, w, b, row):
    return _shift_down(a, row) * w[0:1] + a * w[1:2] + _shift_up(a, row) * w[2:3] + b


def _conv_fwd(name, a3, cw, cb, tc=256):
    _, S, FF = a3.shape
    tc = _tile(FF, tc)

    def body(a_ref, w_ref, b_ref, o_ref):
        row = lax.broadcasted_iota(jnp.int32, (S, tc), 0)
        cg = _conv(a_ref[0], w_ref[0], b_ref[0], row)
        cv = _conv(a_ref[1], w_ref[1], b_ref[1], row)
        o_ref[...] = (_gelu(cg) * cv).astype(o_ref.dtype)

    return pl.pallas_call(
        body, name=name, grid=(FF // tc,),
        in_specs=[pl.BlockSpec((2, S, tc), lambda j: (0, 0, j)), pl.BlockSpec((2, 3, tc), lambda j: (0, 0, j)),
                  pl.BlockSpec((2, 1, tc), lambda j: (0, 0, j))],
        out_specs=pl.BlockSpec((S, tc), lambda j: (0, j)),
        out_shape=jax.ShapeDtypeStruct((S, FF), BF16), compiler_params=_cp("parallel"))(a3, cw, cb)


def _conv_bwd(name, a3, cw, cb, dact, tc=128):
    _, S, FF = a3.shape
    tc = _tile(FF, tc)

    def body(a_ref, w_ref, b_ref, d_ref, da_ref, dw_ref, db_ref):
        row = lax.broadcasted_iota(jnp.int32, (S, tc), 0)
        ag, av = a_ref[0], a_ref[1]
        wg, wv = w_ref[0], w_ref[1]
        cg = _conv(ag, wg, b_ref[0], row)
        cv = _conv(av, wv, b_ref[1], row)
        d = d_ref[...]
        dcs = (d * cv * _gelu_grad(cg), d * _gelu(cg))
        for h, (dc, a, w) in enumerate(zip(dcs, (ag, av), (wg, wv))):
            da_ref[h] = _shift_up(dc, row) * w[0:1] + dc * w[1:2] + _shift_down(dc, row) * w[2:3]
            dw_ref[h, 0:1, :] = jnp.sum(dc * _shift_down(a, row), axis=0, keepdims=True)
            dw_ref[h, 1:2, :] = jnp.sum(dc * a, axis=0, keepdims=True)
            dw_ref[h, 2:3, :] = jnp.sum(dc * _shift_up(a, row), axis=0, keepdims=True)
            db_ref[h] = jnp.sum(dc, axis=0, keepdims=True)

    a_spec = pl.BlockSpec((2, S, tc), lambda j: (0, 0, j))
    w_spec = pl.BlockSpec((2, 3, tc), lambda j: (0, 0, j))
    b_spec = pl.BlockSpec((2, 1, tc), lambda j: (0, 0, j))
    return pl.pallas_call(
        body, name=name, grid=(FF // tc,),
        in_specs=[a_spec, w_spec, b_spec, pl.BlockSpec((S, tc), lambda j: (0, j))],
        out_specs=[a_spec, w_spec, b_spec],
        out_shape=[jax.ShapeDtypeStruct((2, S, FF), F32), jax.ShapeDtypeStruct((2, 3, FF), F32),
                   jax.ShapeDtypeStruct((2, 1, FF), F32)],
        compiler_params=_cp("parallel"))(a3, cw, cb, dact)


_HBM = pl.BlockSpec(memory_space=pltpu.HBM)


def _position():
    return lax.axis_index("x"), lax.axis_index("y"), lax.axis_index("c")


def _all_gather(name, x, layer=None):
    block = x.shape if layer is None else x.shape[1:]

    def body(x_ref, out_ref, send_sems, recv_sems, local_sem):
        px, py, pc = _position()
        me, sibling = (px, py, pc), (px, py, 1 - pc)
        chips = [(1 - px, py), (px, 1 - py), (1 - px, 1 - py)]
        src = x_ref if layer is None else x_ref.at[layer]

        def slot(qx, qy, qc):
            return out_ref.at[4 * qx + 2 * qy + qc]

        def copy(k, blockpos, to, from_src=False):
            return pltpu.make_async_remote_copy(
                src_ref=src if from_src else slot(*blockpos), dst_ref=slot(*blockpos),
                send_sem=send_sems.at[k], recv_sem=recv_sems.at[k], device_id=to, device_id_type=MESH)

        mine = pltpu.make_async_copy(src, slot(*me), local_sem)
        mine.start()
        first = [copy(0, me, sibling, True)]
        first += [copy(1 + j, me, (*chip, pc), True) for j, chip in enumerate(chips)]
        for cp in first:
            cp.start()
        passed = [copy(4 + j, (*chip, pc), sibling) for j, chip in enumerate(chips)]
        for j, chip in enumerate(chips):
            copy(1 + j, (*chip, pc), me).wait_recv()
            passed[j].start()
        copy(0, sibling, me).wait_recv()
        for j, chip in enumerate(chips):
            copy(4 + j, (*chip, 1 - pc), me).wait_recv()
        for cp in first + passed:
            cp.wait_send()
        mine.wait()

    return pl.pallas_call(
        body, name=name, in_specs=[_HBM], out_specs=_HBM,
        out_shape=jax.ShapeDtypeStruct((N_DEV,) + tuple(block), x.dtype),
        scratch_shapes=[pltpu.SemaphoreType.DMA((7,)), pltpu.SemaphoreType.DMA((7,)), pltpu.SemaphoreType.DMA(())],
    )(x)


_SEM = pl.BlockSpec(memory_space=pltpu.SEMAPHORE)
_EFFECT = pltpu.SideEffectType.DATAFLOW_SIDE_EFFECTING
_FLIPS = ((1, 0), (0, 1), (1, 1))


def _split_start(name, bufs, ncopy, plan, after=()):
    n = len(bufs)
    after = list(after)

    def body(*refs):
        ins = refs[:n]
        send_sems, recv_sems, token = refs[n + len(after)], refs[n + len(after) + 1], refs[-1]
        for i, (src, dst, to) in enumerate(plan(ins)):
            pltpu.make_async_remote_copy(src_ref=src, dst_ref=dst, send_sem=send_sems.at[i], recv_sem=recv_sems.at[i],
                                         device_id=to, device_id_type=MESH).start()
        token[...] = jnp.zeros_like(token)

    outs = pl.pallas_call(
        body, name=name,
        out_shape=(pltpu.SemaphoreType.DMA((ncopy,)), pltpu.SemaphoreType.DMA((ncopy,)),
                   *[pltpu.HBM(b.shape, b.dtype) for b in bufs], jax.ShapeDtypeStruct((8, 128), F32)),
        in_specs=[_HBM] * n + [pl.BlockSpec(memory_space=pl.ANY)] * len(after),
        out_specs=(_SEM, _SEM, *([_HBM] * n), pl.BlockSpec(memory_space=pltpu.VMEM)),
        input_output_aliases={i: 2 + i for i in range(n)},
        compiler_params=pltpu.CompilerParams(has_side_effects=_EFFECT),
    )(*[pltpu.with_memory_space_constraint(b, pltpu.HBM) for b in bufs], *after)
    return outs[0], outs[1], list(outs[2:2 + n]), outs[-1]


def _split_wait(name, bufs, send_sems, recv_sems, plan, after):
    n = len(bufs)
    after = list(after)

    def body(*refs):
        ins = refs[:n]
        ssem, rsem = refs[n], refs[n + 1]
        for i, (src, dst, to) in enumerate(plan(ins)):
            cp = pltpu.make_async_remote_copy(src_ref=src, dst_ref=dst, send_sem=ssem.at[i], recv_sem=rsem.at[i],
                                              device_id=to, device_id_type=MESH)
            cp.wait_send()
            cp.wait_recv()

    outs = pl.pallas_call(
        body, name=name, out_shape=tuple(pltpu.HBM(b.shape, b.dtype) for b in bufs),
        in_specs=[_HBM] * n + [_SEM, _SEM] + [pl.BlockSpec(memory_space=pl.ANY)] * len(after),
        out_specs=tuple([_HBM] * n), input_output_aliases={i: i for i in range(n)},
        compiler_params=pltpu.CompilerParams(has_side_effects=_EFFECT),
    )(*bufs, send_sems, recv_sems, *after)
    return list(outs)


def _gather_plan(refs):
    px, py, pc = _position()
    me = 4 * px + 2 * py + pc
    targets = [(px, py, 1 - pc), (1 - px, py, pc), (px, 1 - py, pc), (1 - px, 1 - py, pc)]
    return [(r.at[me], r.at[me], to) for r in refs for to in targets]


def _gather_forward(name, lands):
    n = len(lands)

    def body(*refs):
        outs, send_sems, recv_sems = refs[n:2 * n], refs[2 * n], refs[2 * n + 1]
        px, py, pc = _position()
        copies = []
        for w, ref in enumerate(outs):
            for j, (fx, fy) in enumerate(_FLIPS):
                qx = 1 - px if fx else px
                qy = 1 - py if fy else py
                mine, theirs = 4 * qx + 2 * qy + pc, 4 * qx + 2 * qy + (1 - pc)
                send = pltpu.make_async_remote_copy(
                    src_ref=ref.at[mine], dst_ref=ref.at[mine], send_sem=send_sems.at[3 * w + j],
                    recv_sem=recv_sems.at[3 * w + j], device_id=(px, py, 1 - pc), device_id_type=MESH)
                recv = pltpu.make_async_remote_copy(
                    src_ref=ref.at[theirs], dst_ref=ref.at[theirs], send_sem=send_sems.at[3 * w + j],
                    recv_sem=recv_sems.at[3 * w + j], device_id=(px, py, 1 - pc), device_id_type=MESH)
                copies.append((send, recv))
        for send, _ in copies:
            send.start()
        for send, recv in copies:
            send.wait_send()
            recv.wait_recv()

    outs = pl.pallas_call(
        body, name=name, in_specs=[_HBM] * n, out_specs=tuple([_HBM] * n),
        out_shape=tuple(jax.ShapeDtypeStruct(l.shape, l.dtype) for l in lands),
        input_output_aliases={i: i for i in range(n)},
        scratch_shapes=[pltpu.SemaphoreType.DMA((3 * n,)), pltpu.SemaphoreType.DMA((3 * n,))])(*lands)
    return list(outs)


def _pair_plan(n):
    def plan(refs):
        px, py, pc = _position()
        return [(refs[w].at[2 * k + (1 - pc)], refs[n + w].at[k], (px, py, 1 - pc)) for w in range(n) for k in range(4)]
    return plan


def _chip_plan(n):
    def plan(refs):
        px, py, pc = _position()
        out = []
        for w in range(n):
            for j, (fx, fy) in enumerate(_FLIPS):
                qx = 1 - px if fx else px
                qy = 1 - py if fy else py
                out.append((refs[w].at[2 * qx + qy], refs[n + w].at[j], (qx, qy, pc)))
        return out
    return plan


def _cast_place(name, dev, w, layer):
    nl, R, C = w.shape
    tr = _row_tile(R, C)

    def body(dev_ref, w_ref, o_ref):
        o_ref[...] = w_ref[...].astype(o_ref.dtype)

    return pl.pallas_call(
        body, name=name,
        grid_spec=pltpu.PrefetchScalarGridSpec(
            num_scalar_prefetch=1, grid=(R // tr,),
            in_specs=[pl.BlockSpec((None, tr, C), lambda i, d: (layer, i, 0))],
            out_specs=pl.BlockSpec((None, tr, C), lambda i, d: (d[0], i, 0))),
        out_shape=jax.ShapeDtypeStruct((N_DEV, R, C), BF16), compiler_params=_cp("parallel"))(dev, w)


def _pair_sum(name, core, dw, recv):
    _, R, C = dw.shape
    tr = _row_tile(R, C)
    dw4 = dw.reshape(4, 2, R, C)

    def body(core_ref, a_ref, b_ref, o_ref):
        o_ref[...] = (a_ref[...] + b_ref[...]).astype(o_ref.dtype)

    return pl.pallas_call(
        body, name=name,
        grid_spec=pltpu.PrefetchScalarGridSpec(
            num_scalar_prefetch=1, grid=(4, R // tr),
            in_specs=[pl.BlockSpec((None, None, tr, C), lambda k, i, c_ref: (k, c_ref[0], i, 0)),
                      pl.BlockSpec((None, tr, C), lambda k, i, c_ref: (k, i, 0))],
            out_specs=pl.BlockSpec((None, tr, C), lambda k, i, c_ref: (k, i, 0))),
        out_shape=jax.ShapeDtypeStruct((4, R, C), BF16),
        compiler_params=_cp("parallel", "parallel"))(core, dw4, recv)


def _adamw_math(w, g, m, v):
    m = ADAM_B1 * m + (1.0 - ADAM_B1) * g
    v = ADAM_B2 * v + (1.0 - ADAM_B2) * (g * g)
    m_hat = m / (1.0 - ADAM_B1 ** ADAM_STEP)
    v_hat = v / (1.0 - ADAM_B2 ** ADAM_STEP)
    delta = -ADAM_LR * (m_hat / (jnp.sqrt(v_hat) + ADAM_EPS) + ADAM_WD * w)
    return delta, m, v


def _adamw_shard(name, chip, layer, w, m, v, p, recv, prev, deps=()):
    nl, R, C = w.shape
    tr = _row_tile(R, C)
    n_prev = 0 if prev is None else 4

    def body(chip_ref, w_ref, m_ref, v_ref, p_ref, r_ref, *rest):
        g_ref, d_ref, nm_ref, nv_ref = rest[-4:]
        g = p_ref[...].astype(F32)
        for j in range(3):
            g = g + r_ref[j].astype(F32)
        delta, nm, nv = _adamw_math(w_ref[...], g, m_ref[...], v_ref[...])
        g_ref[...] = g
        d_ref[...] = delta
        nm_ref[...] = nm
        nv_ref[...] = nv

    lay = pl.BlockSpec((None, tr, C), lambda i, c_ref: (layer, i, 0))
    in_specs = [lay, lay, lay,
                pl.BlockSpec((None, tr, C), lambda i, c_ref: (c_ref[0], i, 0)),
                pl.BlockSpec((3, tr, C), lambda i, c_ref: (0, i, 0))]
    in_specs += [pl.BlockSpec(memory_space=pl.ANY)] * n_prev + [_DEP] * len(deps)
    shape = jax.ShapeDtypeStruct((nl, R, C), F32)
    ins = [chip, w, m, v, p, recv] + ([] if prev is None else list(prev)) + list(deps)
    return pl.pallas_call(
        body, name=name,
        grid_spec=pltpu.PrefetchScalarGridSpec(
            num_scalar_prefetch=1, grid=(R // tr,), in_specs=in_specs, out_specs=[lay] * 4),
        out_shape=[shape] * 4,
        input_output_aliases={6 + j: j for j in range(n_prev)},
        compiler_params=_cp("parallel"))(*ins)


def _sum_slots(name, parts, tr=512):
    n, R, C = parts.shape
    tr = _tile(R, tr)

    def body(p_ref, o_ref):
        acc = p_ref[0]
        for j in range(1, n):
            acc = acc + p_ref[j]
        o_ref[...] = acc

    return pl.pallas_call(
        body, name=name, grid=(R // tr,),
        in_specs=[pl.BlockSpec((n, tr, C), lambda i: (0, i, 0))],
        out_specs=pl.BlockSpec((tr, C), lambda i: (i, 0)),
        out_shape=jax.ShapeDtypeStruct((R, C), F32), compiler_params=_cp("parallel"))(parts)


def _adamw_flat(name, w, g, m, v, tr=512):
    R, C = w.shape
    tr = _tile(R, tr)

    def body(w_ref, g_ref, m_ref, v_ref, d_ref, nm_ref, nv_ref):
        delta, nm, nv = _adamw_math(w_ref[...], g_ref[...], m_ref[...], v_ref[...])
        d_ref[...] = delta
        nm_ref[...] = nm
        nv_ref[...] = nv

    row = pl.BlockSpec((tr, C), lambda i: (i, 0))
    shape = jax.ShapeDtypeStruct((R, C), F32)
    return pl.pallas_call(
        body, name=name, grid=(R // tr,), in_specs=[row] * 4, out_specs=[row] * 3, out_shape=[shape] * 3,
        compiler_params=_cp("parallel"))(w, g, m, v)


_PACK_ROWS = 512


def _pack(arrs):
    flat = jnp.concatenate([a.reshape(-1) for a in arrs])
    unit = _PACK_ROWS * 128
    pad = (-flat.shape[0]) % unit
    return jnp.pad(flat, (0, pad)).reshape(-1, 128)


def _unpack(packed, shapes):
    flat = packed.reshape(-1)
    outs, off = [], 0
    for s in shapes:
        n = int(np.prod(s))
        outs.append(flat[off:off + n].reshape(s))
        off += n
    return outs


def kernel(x, mem, mix_norm_g, ffn_norm_g, mem_norm_g, w_mem_kv, a_w_in, a_w_out, b_w_in, b_v_norm_g, b_w_s, b_s_bias, b_w_out, ffn_w_up, ffn_conv_w, ffn_conv_b, ffn_w_down, final_norm_g, loss_target, m_mix_norm_g, m_ffn_norm_g, m_mem_norm_g, m_w_mem_kv, m_a_w_in, m_a_w_out, m_b_w_in, m_b_v_norm_g, m_b_w_s, m_b_s_bias, m_b_w_out, m_ffn_w_up, m_ffn_conv_w, m_ffn_conv_b, m_ffn_w_down, m_final_norm_g, v_mix_norm_g, v_ffn_norm_g, v_mem_norm_g, v_w_mem_kv, v_a_w_in, v_a_w_out, v_b_w_in, v_b_v_norm_g, v_b_w_s, v_b_s_bias, v_b_w_out, v_ffn_w_up, v_ffn_conv_w, v_ffn_conv_b, v_ffn_w_down, v_final_norm_g):
    px, py, pc = _position()
    dev = 4 * px + 2 * py + pc
    core = jnp.reshape(pc, (1,)).astype(jnp.int32)
    chip = jnp.reshape(2 * px + py, (1,)).astype(jnp.int32)

    x0 = x[0]
    mem0 = mem[0]
    tgt = loss_target[0]
    S, D = x0.shape
    depth = mix_norm_g.shape[0]
    FF = ffn_w_down.shape[1] * N_DEV
    a_in = a_w_in.shape[2] * N_DEV
    b_in = b_w_in.shape[2] * N_DEV
    a_q_blk = (a_in - GW) // GW
    b_q_blk = (b_in - GW) // GW

    stacks = {"kv": (w_mem_kv, m_w_mem_kv, v_w_mem_kv), "ain": (a_w_in, m_a_w_in, v_a_w_in),
              "aout": (a_w_out, m_a_w_out, v_a_w_out), "bin": (b_w_in, m_b_w_in, v_b_w_in),
              "bout": (b_w_out, m_b_w_out, v_b_w_out), "up": (ffn_w_up, m_ffn_w_up, v_ffn_w_up),
              "down": (ffn_w_down, m_ffn_w_down, v_ffn_w_down)}
    dev1 = jnp.reshape(dev, (1,)).astype(jnp.int32)

    def groups_of(i):
        j = i // 2
        mix = [("kv", i), ("ain", j), ("aout", j)] if i % 2 == 0 else [("kv", i), ("bin", j), ("bout", j)]
        return mix, [("up", i), ("down", i)]

    gather_groups = [(f"{half}{i}", members) for i in range(depth) for half, members in zip("mf", groups_of(i))]
    gather_ahead = 2
    in_flight = {}

    def gather_start(k, after):
        gname, members = gather_groups[k]
        lands = [_cast_place(f"place_{t}{l}", dev1, stacks[t][0], l) for t, l in members]
        ssem, rsem, lands, tok = _split_start(f"ag_start_{gname}", lands, 4 * len(lands), _gather_plan, after)
        in_flight[k] = (lands, ssem, rsem)
        return tok

    start_tokens = [gather_start(k, ()) for k in range(gather_ahead)]

    def gathered(k, after):
        gname, members = gather_groups[k]
        lands, ssem, rsem = in_flight.pop(k)
        lands = _split_wait(f"ag_wait_{gname}", lands, ssem, rsem, _gather_plan, [after])
        toks = []
        if k + gather_ahead < len(gather_groups):
            toks.append(gather_start(k + gather_ahead, [lands[0]]))
        lands = _gather_forward(f"ag_pass_{gname}", lands)
        out = {}
        for (t, l), land in zip(members, lands):
            if t == "kv":
                out["kv"] = land.reshape(D, 2 * GW)
            elif t in ("ain", "aout", "up"):
                out[{"ain": "in", "aout": "out", "up": "up"}[t]] = land
            elif t == "bin":
                out["in"] = jnp.transpose(land, (1, 0, 2)).reshape(D, b_in)
            elif t == "bout":
                out["out"] = land.reshape(B_W + GW, D)
            else:
                out["down"] = land.reshape(FF, D)
        return out, toks

    small_local = _pack([ffn_conv_w, b_v_norm_g])
    small_all = _all_gather("ag_small", small_local)
    cw_parts, gv_parts = [], []
    for d in range(N_DEV):
        cw_d, gv_d = _unpack(small_all[d], [ffn_conv_w.shape, b_v_norm_g.shape])
        cw_parts.append(cw_d)
        gv_parts.append(gv_d)
    conv_w_full = jnp.concatenate(cw_parts, axis=-1)
    gv_full = jnp.concatenate(gv_parts, axis=-1)

    def conv_params(i):
        cw = conv_w_full[i].reshape(3, 2, FF).transpose(1, 0, 2)
        cb = ffn_conv_b[i].reshape(2, 1, FF)
        return cw, cb

    saved = []
    W = []
    xc = x0
    for i in range(depth):
        j = i // 2
        lw, toks = gathered(2 * i, xc)
        sv = {"x0": xc}
        h1, h1t = _rms_fwd(f"mixnorm{i}", xc, mix_norm_g[i], deps=toks + (start_tokens if i == 0 else []),
                           with_t=True)
        memn = _rms_fwd(f"memnorm{i}", mem0, mem_norm_g[i])
        kv = _mm_full(f"kvproj{i}", memn, lw["kv"])
        if i % 2 == 0:
            proj = _mm_gcols(f"ain{i}", h1, lw["in"])
            outs, lses = [], []
            for g in range(len(A_PATTERNS)):
                o, l = _attn_fwd(f"attn{i}_{g}", proj, g)
                outs.append(o)
                lses.append(l)
            tok, lse = _attn_merge(f"merge{i}", outs, lses)
            mo = _mem_fwd(f"memattn{i}", proj, a_q_blk, kv)
            cat = jnp.concatenate([tok.astype(BF16), mo.astype(BF16)], axis=1)
            x1 = _mm_gcols(f"aout{i}", cat, lw["out"], res=xc)
            sv.update(tok=tok, lse=lse)
        else:
            proj = _mm_full(f"bin{i}", h1, lw["in"])
            bias_t = b_s_bias[j].T
            tok = _sgu_fwd(f"sgu{i}", proj, gv_full[j], b_w_s[j], bias_t)
            mo = _mem_fwd(f"memattn{i}", proj, b_q_blk, kv)
            cat = jnp.concatenate([tok.astype(BF16), mo.astype(BF16)], axis=1)
            x1 = _mm_full(f"bout{i}", cat, lw["out"], res=xc)
        lw_ffn, toks = gathered(2 * i + 1, x1)
        lw.update(lw_ffn)
        W.append(lw)
        h2, h2t = _rms_fwd(f"ffnnorm{i}", x1, ffn_norm_g[i], deps=toks, with_t=True)
        cw, cb = conv_params(i)
        a3 = _mm_gcols(f"up{i}", h2, lw["up"], split_out=True, tm=512)
        act = _conv_fwd(f"conv{i}", a3, cw, cb)
        x2 = _mm_full(f"down{i}", act, lw["down"], res=x1, tm=512, tn=1024, tk=FF // 4)
        sv.update(h1t=h1t, memn=memn, kv=kv, proj=proj, cat=cat, x1=x1, h2t=h2t, a3=a3, act=act)
        saved.append(sv)
        xc = x2

    dx, dg_final, sq = _final("final", xc, tgt, final_norm_g)
    loss = lax.psum(sq[0, 0] * (0.5 / D), AXES)

    chain = {}

    def pair_begin(gname, members, dws):
        n = len(dws)
        recvs = [lax.empty((4,) + dw.shape[1:], F32) for dw in dws]
        ssem, rsem, bufs, tok = _split_start(f"rs_pair_start_{gname}", dws + recvs, 4 * n, _pair_plan(n))
        return dict(name=gname, members=members, n=n, bufs=bufs, sems=(ssem, rsem)), tok

    def pair_end_chip_begin(st, after):
        n, gname = st["n"], st["name"]
        bufs = _split_wait(f"rs_pair_wait_{gname}", st["bufs"], *st["sems"], _pair_plan(n), after)
        ps = [_pair_sum(f"rs_sum_{t}{l}", core, bufs[w], bufs[n + w]) for w, (t, l) in enumerate(st["members"])]
        recvs = [lax.empty((3,) + p.shape[1:], BF16) for p in ps]
        ssem, rsem, bufs, tok = _split_start(f"rs_chip_start_{gname}", ps + recvs, 3 * n, _chip_plan(n))
        return dict(name=gname, members=st["members"], n=n, bufs=bufs, sems=(ssem, rsem)), tok

    def chip_end_update(st, after, deps=()):
        n = st["n"]
        bufs = _split_wait(f"rs_chip_wait_{st['name']}", st["bufs"], *st["sems"], _chip_plan(n), after)
        for w, (t, l) in enumerate(st["members"]):
            wst, mst, vst = stacks[t]
            chain[t] = _adamw_shard(f"adamw_{t}{l}", chip, l, wst, mst, vst, bufs[w], bufs[n + w], chain.get(t), deps)

    pipe = {"pair": [], "chip": [], "deps": []}

    def take_deps():
        deps, pipe["deps"] = pipe["deps"], []
        return deps

    def submit(gname, members, dws):
        st, tok = pair_begin(gname, members, dws)
        pipe["pair"].append(st)
        pipe["deps"].append(tok)

    def advance(after):
        arrived, pipe["chip"] = pipe["chip"], []
        toks = []
        for st in pipe["pair"]:
            new, tok = pair_end_chip_begin(st, [after])
            pipe["chip"].append(new)
            toks.append(tok)
        pipe["pair"] = []
        for st in arrived:
            chip_end_update(st, [after], toks)
        pipe["deps"] += toks

    big = {k: [None] * n for k, n in (("kv", depth), ("ain", depth // 2 + depth % 2), ("aout", depth // 2 + depth % 2),
                                      ("bin", depth // 2), ("bout", depth // 2), ("up", depth), ("down", depth))}
    dg_mix, dg_ffn, dg_mem = [None] * depth, [None] * depth, [None] * depth
    d_conv_w, d_conv_b = [None] * depth, [None] * depth
    d_gv, d_ws, d_sb = [None] * (depth // 2), [None] * (depth // 2), [None] * (depth // 2)
    for i in reversed(range(depth)):
        j = i // 2
        lw, sv = W[i], saved[i]
        cw, cb = conv_params(i)
        mix_members, ffn_members = groups_of(i)
        deps = take_deps()
        dact = _mm_dx_full(f"ddown{i}", dx, lw["down"], tm=512, tko=FF // 4, tc=D, deps=deps)
        big["down"][i] = _mm_dw(f"wdown{i}", sv["act"], dx, deps=deps).reshape(N_DEV, FF // N_DEV, D)
        da3, dcw, dcb = _conv_bwd(f"dconv{i}", sv["a3"], cw, cb, dact)
        d_conv_w[i] = dcw.transpose(1, 0, 2).reshape(3, 2 * FF)
        d_conv_b[i] = dcb.reshape(2 * FF)
        advance(da3)
        deps = take_deps()
        dh2 = _mm_dx_gcols(f"dup{i}", da3, lw["up"], split_in=True, tm=512, deps=deps)
        big["up"][i] = _mm_dw_gcols(f"wup{i}", sv["h2t"], da3, N_DEV, split_in=True, tko=512, deps=deps, a_t=True)
        dx1, dg_ffn[i] = _rms_bwd(f"dffnnorm{i}", dh2, sv["x1"], ffn_norm_g[i], dx)
        submit(f"f{i}", ffn_members, [big["up"][i], big["down"][i]])
        deps = take_deps()
        if i % 2 == 0:
            dcat = _mm_dx_gcols(f"daout{i}", dx1, lw["out"], deps=deps)
            big["aout"][j] = _mm_dw_gcols(f"waout{i}", sv["cat"], dx1, N_DEV, deps=deps)
            parts = [None] * 9
            for g in range(len(A_PATTERNS)):
                dq, dk, dv = _attn_bwd(f"dattn{i}_{g}", sv["proj"], g, dcat, 0, sv["tok"], sv["lse"])
                parts[g], parts[3 + g], parts[6 + g] = dq, dk, dv
            dqm, dkv = _mem_bwd(f"dmemattn{i}", sv["proj"], a_q_blk, sv["kv"], dcat, 1)
            dproj = jnp.concatenate(parts + [dqm], axis=1)
            advance(dkv)
            deps = take_deps()
            dh1 = _mm_dx_gcols(f"dain{i}", dproj, lw["in"], deps=deps)
            big["ain"][j] = _mm_dw_gcols(f"wain{i}", sv["h1t"], dproj, N_DEV, deps=deps, a_t=True)
        else:
            dcat = _mm_dx_full(f"dbout{i}", dx1, lw["out"], tm=1024, deps=deps)
            big["bout"][j] = _mm_dw(f"wbout{i}", sv["cat"], dx1, deps=deps).reshape(N_DEV, (B_W + GW) // N_DEV, D)
            bias_t = b_s_bias[j].T
            du, dvp, dgv, dws, dbt = _sgu_bwd(f"dsgu{i}", sv["proj"], gv_full[j], b_w_s[j], bias_t, dcat)
            d_gv[j], d_ws[j], d_sb[j] = dgv.reshape(B_W), dws, dbt.T
            dqm, dkv = _mem_bwd(f"dmemattn{i}", sv["proj"], b_q_blk, sv["kv"], dcat, B_W // GW)
            dproj = jnp.concatenate([du, dvp, dqm], axis=1)
            advance(dkv)
            deps = take_deps()
            dh1 = _mm_dx_full(f"dbin{i}", dproj, lw["in"], tc=b_in // 2, deps=deps)
            dwin = _mm_dw(f"wbin{i}", sv["h1t"], dproj, tko=1024, tn=512, deps=deps, a_t=True)
            big["bin"][j] = dwin.reshape(D, N_DEV, b_in // N_DEV).transpose(1, 0, 2)
        big["kv"][i] = _mm_dw(f"wkv{i}", sv["memn"], dkv, tko=1024).reshape(N_DEV, D // N_DEV, 2 * GW)
        dmemn = _mm_dx_full(f"dkvproj{i}", dkv, lw["kv"], tko=1024)
        _, dg_mem[i] = _rms_bwd(f"dmemnorm{i}", dmemn, mem0, mem_norm_g[i])
        dx, dg_mix[i] = _rms_bwd(f"dmixnorm{i}", dh1, sv["x0"], mix_norm_g[i], dx1)
        submit(f"m{i}", mix_members, [big[t][l] for t, l in mix_members])
    grad_x = dx[None]

    advance(dx)
    tail = take_deps()

    hold = sum(t[0, 0] for t in tail)
    small_grads = [jnp.concatenate(dg_mix), jnp.concatenate(dg_ffn), jnp.concatenate(dg_mem),
                   jnp.stack(d_ws), jnp.stack(d_sb), jnp.stack(d_conv_b), dg_final.reshape(D) + hold,
                   jnp.stack(d_gv), jnp.stack(d_conv_w)]
    g_small = _sum_slots("small_sum", _all_gather("ag_smallgrad", _pack(small_grads)))
    for st in pipe["chip"]:
        chip_end_update(st, [g_small] + [chain[t][0] for t in chain])
    res = {}
    for tag, name in (("kv", "w_mem_kv"), ("ain", "a_w_in"), ("aout", "a_w_out"), ("bin", "b_w_in"),
                      ("bout", "b_w_out"), ("up", "ffn_w_up"), ("down", "ffn_w_down")):
        res[name] = list(chain[tag])

    rep_shapes = [mix_norm_g.shape, ffn_norm_g.shape, mem_norm_g.shape, b_w_s.shape, b_s_bias.shape,
                  ffn_conv_b.shape, final_norm_g.shape]
    full_gv_shape = (b_v_norm_g.shape[0], B_W)
    full_cw_shape = (depth, 3, 2 * FF)
    g_list = _unpack(g_small, rep_shapes + [full_gv_shape, full_cw_shape])
    g_gv = lax.dynamic_slice_in_dim(g_list[7], dev * b_v_norm_g.shape[1], b_v_norm_g.shape[1], axis=1)
    g_cw = lax.dynamic_slice_in_dim(g_list[8], dev * ffn_conv_w.shape[2], ffn_conv_w.shape[2], axis=2)
    g_all = g_list[:7] + [g_gv, g_cw]
    names = ["mix_norm_g", "ffn_norm_g", "mem_norm_g", "b_w_s", "b_s_bias", "ffn_conv_b", "final_norm_g",
             "b_v_norm_g", "ffn_conv_w"]
    ws = [mix_norm_g, ffn_norm_g, mem_norm_g, b_w_s, b_s_bias, ffn_conv_b, final_norm_g, b_v_norm_g, ffn_conv_w]
    ms = [m_mix_norm_g, m_ffn_norm_g, m_mem_norm_g, m_b_w_s, m_b_s_bias, m_ffn_conv_b, m_final_norm_g,
          m_b_v_norm_g, m_ffn_conv_w]
    vs = [v_mix_norm_g, v_ffn_norm_g, v_mem_norm_g, v_b_w_s, v_b_s_bias, v_ffn_conv_b, v_final_norm_g,
          v_b_v_norm_g, v_ffn_conv_w]
    shapes = [w.shape for w in ws]
    d_p, m_p, v_p = _adamw_flat("adamw_small", _pack(ws), _pack(g_all), _pack(ms), _pack(vs))
    for n, g, d, nm, nv in zip(names, g_all, _unpack(d_p, shapes), _unpack(m_p, shapes), _unpack(v_p, shapes)):
        res[n] = [g, d, nm, nv]

    order = ["mix_norm_g", "ffn_norm_g", "mem_norm_g", "w_mem_kv", "a_w_in", "a_w_out", "b_w_in", "b_v_norm_g",
             "b_w_s", "b_s_bias", "b_w_out", "ffn_w_up", "ffn_conv_w", "ffn_conv_b", "ffn_w_down", "final_norm_g"]
    return (loss, grad_x, *[res[n][0] for n in order], *[res[n][1] for n in order],
            *[res[n][2] for n in order], *[res[n][3] for n in order])
```

```python
import functools

import numpy as np
import jax
import jax.numpy as jnp
from jax import lax
from jax.experimental import pallas as pl
from jax.experimental.pallas import tpu as pltpu

F32 = jnp.float32
BF16 = jnp.bfloat16
MESH = pl.DeviceIdType.MESH
AXES = ("x", "y", "c")
N_DEV = 8

EPS = 1e-6
NEG = -1e30
HEAD = 128
HPG = 4
GW = HPG * HEAD
A_PATTERNS = ((128, 1), (512, 4), (2048, 16))
A_HEADS = HPG * len(A_PATTERNS)
QBLK = 128
B_GROUPS = 12
B_W = B_GROUPS * HEAD
SLOPES = (2.0 ** (-8.0 * (np.arange(A_HEADS) + 1) / A_HEADS)).astype(np.float32)
SCALE = HEAD ** -0.5

ADAM_LR = 0.001
ADAM_B1 = 0.9
ADAM_B2 = 0.999
ADAM_EPS = 1e-08
ADAM_WD = 0.01
ADAM_STEP = 10

V7X_VMEM_LIMIT = 50 * 1024 * 1024

NN = (((1,), (0,)), ((), ()))
NT = (((1,), (1,)), ((), ()))
TN = (((0,), (0,)), ((), ()))


def _cp(*sem):
    return pltpu.CompilerParams(dimension_semantics=sem, vmem_limit_bytes=V7X_VMEM_LIMIT)


def _dot(a, b, dims=NN):
    return lax.dot_general(a, b, dims, preferred_element_type=F32)


def _tile(n, pref):
    t = min(n, pref)
    assert n % t == 0, (n, pref)
    return t


def _row_tile(rows, cols):
    best = None
    for t in range(16, rows + 1, 16):
        if rows % t == 0 and t * cols * 4 <= (1 << 20):
            best = t
    if best is None:
        best = rows
    return best


_DEP = pl.BlockSpec((8, 128), lambda *_: (0, 0))


def _matmul(name, dims, grid, a, a_spec, b, b_spec, out_shape, o_spec, tile, res=None, res_spec=None, deps=()):
    nk = grid[2]
    has_res = res is not None

    def body(*refs):
        a_ref, b_ref = refs[0], refs[1]
        r_ref = refs[2] if has_res else None
        o_ref, acc_ref = refs[-2], refs[-1]
        part = _dot(a_ref[...].astype(BF16), b_ref[...].astype(BF16), dims)

        def finish(val):
            if has_res:
                val = val + r_ref[...]
            o_ref[...] = val.astype(o_ref.dtype)

        if nk == 1:
            finish(part)
        else:
            k = pl.program_id(2)

            @pl.when(k == 0)
            def _():
                acc_ref[...] = part

            @pl.when(k > 0)
            def _():
                acc_ref[...] += part

            @pl.when(k == nk - 1)
            def _():
                finish(acc_ref[...])

    ins = [a, b] + ([res] if has_res else []) + list(deps)
    specs = [a_spec, b_spec] + ([res_spec] if has_res else []) + [_DEP] * len(deps)
    return pl.pallas_call(
        body, name=name, grid=grid, in_specs=specs, out_specs=o_spec, out_shape=out_shape,
        scratch_shapes=[pltpu.VMEM(tile if nk > 1 else (8, 128), F32)],
        compiler_params=_cp("parallel", "parallel", "arbitrary"))(*ins)


def _mm_full(name, a, w, res=None, tm=1024, tn=512, tk=2048):
    M, K = a.shape
    N = w.shape[1]
    tm, tn, tk = _tile(M, tm), _tile(N, tn), _tile(K, tk)
    return _matmul(
        name, NN, (N // tn, M // tm, K // tk),
        a, pl.BlockSpec((tm, tk), lambda j, i, k: (i, k)),
        w, pl.BlockSpec((tk, tn), lambda j, i, k: (k, j)),
        jax.ShapeDtypeStruct((M, N), F32), pl.BlockSpec((tm, tn), lambda j, i, k: (i, j)), (tm, tn),
        res, pl.BlockSpec((tm, tn), lambda j, i, k: (i, j)))


def _mm_gcols(name, a, wg, res=None, split_out=False, tm=1024):
    M, K = a.shape
    G, _, Nl = wg.shape
    tm = _tile(M, tm)
    hg = G // 2
    if split_out:
        shape = jax.ShapeDtypeStruct((2, M, hg * Nl), F32)
        o_spec = pl.BlockSpec((None, tm, Nl), lambda g, i, k: (g // hg, i, g % hg))
    else:
        shape = jax.ShapeDtypeStruct((M, G * Nl), F32)
        o_spec = pl.BlockSpec((tm, Nl), lambda g, i, k: (i, g))
    return _matmul(
        name, NN, (G, M // tm, 1),
        a, pl.BlockSpec((tm, K), lambda g, i, k: (i, 0)),
        wg, pl.BlockSpec((None, K, Nl), lambda g, i, k: (g, 0, 0)),
        shape, o_spec, (tm, Nl),
        res, pl.BlockSpec((tm, Nl), lambda g, i, k: (i, g)))


def _mm_dx_full(name, dy, w, tm=512, tko=512, tc=2048, deps=()):
    M, N = dy.shape
    K = w.shape[0]
    tm, tko, tc = _tile(M, tm), _tile(K, tko), _tile(N, tc)
    return _matmul(
        name, NT, (K // tko, M // tm, N // tc),
        dy, pl.BlockSpec((tm, tc), lambda j, i, k: (i, k)),
        w, pl.BlockSpec((tko, tc), lambda j, i, k: (j, k)),
        jax.ShapeDtypeStruct((M, K), F32), pl.BlockSpec((tm, tko), lambda j, i, k: (i, j)), (tm, tko), deps=deps)


def _mm_dx_gcols(name, dy, wg, split_in=False, tm=1024, tko=1024, deps=()):
    G, K, Nl = wg.shape
    M = dy.shape[-2]
    tm, tko = _tile(M, tm), _tile(K, tko)
    hg = G // 2
    if split_in:
        dy_spec = pl.BlockSpec((None, tm, Nl), lambda j, i, g: (g // hg, i, g % hg))
    else:
        dy_spec = pl.BlockSpec((tm, Nl), lambda j, i, g: (i, g))
    return _matmul(
        name, NT, (K // tko, M // tm, G),
        dy, dy_spec,
        wg, pl.BlockSpec((None, tko, Nl), lambda j, i, g: (g, j, 0)),
        jax.ShapeDtypeStruct((M, K), F32), pl.BlockSpec((tm, tko), lambda j, i, g: (i, j)), (tm, tko), deps=deps)


def _lhs_of_dw(a, a_t, ts, tko, index):
    if a_t:
        return NN, pl.BlockSpec((tko, ts), lambda *ids: index(*ids))
    return TN, pl.BlockSpec((ts, tko), lambda *ids: index(*ids)[::-1])


def _mm_dw(name, a, dy, tko=512, tn=1024, ts=2048, deps=(), a_t=False):
    K1, S = a.shape if a_t else a.shape[::-1]
    N = dy.shape[1]
    tko, tn, ts = _tile(K1, tko), _tile(N, tn), _tile(S, ts)
    dims, a_spec = _lhs_of_dw(a, a_t, ts, tko, lambda i, j, k: (j, k))
    return _matmul(
        name, dims, (N // tn, K1 // tko, S // ts),
        a, a_spec,
        dy, pl.BlockSpec((ts, tn), lambda i, j, k: (k, i)),
        jax.ShapeDtypeStruct((K1, N), F32), pl.BlockSpec((tko, tn), lambda i, j, k: (j, i)), (tko, tn), deps=deps)


def _mm_dw_gcols(name, a, dy, G, split_in=False, tko=1024, ts=2048, deps=(), a_t=False):
    K1, S = a.shape if a_t else a.shape[::-1]
    Nl = (dy.shape[-1] * (2 if split_in else 1)) // G
    tko, ts = _tile(K1, tko), _tile(S, ts)
    hg = G // 2
    dims, a_spec = _lhs_of_dw(a, a_t, ts, tko, lambda g, j, k: (j, k))
    if split_in:
        dy_spec = pl.BlockSpec((None, ts, Nl), lambda g, j, k: (g // hg, k, g % hg))
    else:
        dy_spec = pl.BlockSpec((ts, Nl), lambda g, j, k: (k, g))
    return _matmul(
        name, dims, (G, K1 // tko, S // ts),
        a, a_spec,
        dy, dy_spec,
        jax.ShapeDtypeStruct((G, K1, Nl), F32), pl.BlockSpec((None, tko, Nl), lambda g, j, k: (g, j, 0)),
        (tko, Nl), deps=deps)


def _rms_fwd(name, x, g, tr=256, deps=(), with_t=False):
    S, D = x.shape
    tr = _tile(S, tr)

    def body(x_ref, g_ref, *rest):
        xf = x_ref[...]
        r = lax.rsqrt(jnp.mean(xf * xf, axis=-1, keepdims=True) + EPS)
        y = xf * r * g_ref[...]
        if with_t:
            rest[-2][...] = y.astype(BF16)
            rest[-1][...] = y.T.astype(BF16)
        else:
            rest[-1][...] = y.astype(BF16)

    row = pl.BlockSpec((tr, D), lambda i: (i, 0))
    out_specs, out_shape = row, jax.ShapeDtypeStruct((S, D), BF16)
    if with_t:
        out_specs = [row, pl.BlockSpec((D, tr), lambda i: (0, i))]
        out_shape = [out_shape, jax.ShapeDtypeStruct((D, S), BF16)]
    return pl.pallas_call(
        body, name=name, grid=(S // tr,),
        in_specs=[row, pl.BlockSpec((1, D), lambda i: (0, 0))] + [_DEP] * len(deps),
        out_specs=out_specs, out_shape=out_shape, compiler_params=_cp("parallel"))(x, g.reshape(1, D), *deps)


def _rms_bwd(name, dh, x, g, dres=None, tr=256):
    S, D = x.shape
    tr = _tile(S, tr)
    has_res = dres is not None

    def body(*refs):
        dh_ref, x_ref, g_ref = refs[:3]
        dres_ref = refs[3] if has_res else None
        dx_ref, dg_ref = refs[-2], refs[-1]
        xf = x_ref[...]
        r = lax.rsqrt(jnp.mean(xf * xf, axis=-1, keepdims=True) + EPS)
        xh = xf * r
        dhv = dh_ref[...]
        dxh = dhv * g_ref[...]
        dx = r * (dxh - xh * jnp.mean(dxh * xh, axis=-1, keepdims=True))
        if has_res:
            dx = dx + dres_ref[...]
        dx_ref[...] = dx
        part = jnp.sum(dhv * xh, axis=0, keepdims=True)
        i = pl.program_id(0)

        @pl.when(i == 0)
        def _():
            dg_ref[...] = part

        @pl.when(i > 0)
        def _():
            dg_ref[...] += part

    row = pl.BlockSpec((tr, D), lambda i: (i, 0))
    vec = pl.BlockSpec((1, D), lambda i: (0, 0))
    ins = [dh, x, g.reshape(1, D)] + ([dres] if has_res else [])
    return pl.pallas_call(
        body, name=name, grid=(S // tr,),
        in_specs=[row, row, vec] + ([row] if has_res else []),
        out_specs=[row, vec],
        out_shape=[jax.ShapeDtypeStruct((S, D), F32), jax.ShapeDtypeStruct((1, D), F32)],
        compiler_params=_cp("arbitrary"))(*ins)


def _final(name, x, tgt, g, tr=256):
    S, D = x.shape
    tr = _tile(S, tr)

    def body(x_ref, t_ref, g_ref, dx_ref, dg_ref, loss_ref):
        xf = x_ref[...]
        gv = g_ref[...]
        r = lax.rsqrt(jnp.mean(xf * xf, axis=-1, keepdims=True) + EPS)
        xh = xf * r
        err = xh * gv - t_ref[...]
        sq = jnp.sum(jnp.sum(err * err, axis=1, keepdims=True), axis=0, keepdims=True)
        dy = err * (1.0 / D)
        dxh = dy * gv
        dx_ref[...] = r * (dxh - xh * jnp.mean(dxh * xh, axis=-1, keepdims=True))
        part = jnp.sum(dy * xh, axis=0, keepdims=True)
        lpart = jnp.broadcast_to(sq, (8, 128))
        i = pl.program_id(0)

        @pl.when(i == 0)
        def _():
            dg_ref[...] = part
            loss_ref[...] = lpart

        @pl.when(i > 0)
        def _():
            dg_ref[...] += part
            loss_ref[...] += lpart

    row = pl.BlockSpec((tr, D), lambda i: (i, 0))
    vec = pl.BlockSpec((1, D), lambda i: (0, 0))
    return pl.pallas_call(
        body, name=name, grid=(S // tr,), in_specs=[row, row, vec],
        out_specs=[row, vec, pl.BlockSpec((8, 128), lambda i: (0, 0))],
        out_shape=[jax.ShapeDtypeStruct((S, D), F32), jax.ShapeDtypeStruct((1, D), F32),
                   jax.ShapeDtypeStruct((8, 128), F32)],
        compiler_params=_cp("arbitrary"))(x, tgt, g.reshape(1, D))


def _band_specs(nb, col_of):
    prev = pl.BlockSpec((QBLK, GW), lambda r, b: (jnp.maximum(b - 1, 0), col_of(r)))
    cur = pl.BlockSpec((QBLK, GW), lambda r, b: (b, col_of(r)))
    nxt = pl.BlockSpec((QBLK, GW), lambda r, b: (jnp.minimum(b + 1, nb - 1), col_of(r)))
    return [prev, cur, nxt]


def _cat3(refs, sl):
    return jnp.concatenate([ref[:, sl] for ref in refs], axis=0)


def _attn_fwd(name, proj, g):
    window, dil = A_PATTERNS[g]
    n_side = (window // 2) // dil
    S, C = proj.shape
    L = S // dil
    nb = L // QBLK
    cb = C // GW
    pv = proj.reshape(L, dil * C)
    ng = len(A_PATTERNS)

    def body(q_ref, kp, kc, kn, vp, vc, vn, o_ref, lse_ref):
        b = pl.program_id(1)
        jq = b * QBLK + lax.broadcasted_iota(jnp.int32, (QBLK, 3 * QBLK), 0)
        jk = (b - 1) * QBLK + lax.broadcasted_iota(jnp.int32, (QBLK, 3 * QBLK), 1)
        rel = jnp.abs(jk - jq)
        mask = (rel <= n_side) & (jk >= 0) & (jk < L)
        dist = rel.astype(F32) * float(dil)
        for hh in range(HPG):
            sl = slice(hh * HEAD, (hh + 1) * HEAD)
            k = _cat3((kp, kc, kn), sl)
            v = _cat3((vp, vc, vn), sl)
            s = _dot(q_ref[:, sl], k, NT) * SCALE - float(SLOPES[g * HPG + hh]) * dist
            s = jnp.where(mask, s, NEG)
            m = jnp.max(s, axis=1, keepdims=True)
            p = jnp.exp(s - m)
            l = jnp.sum(p, axis=1, keepdims=True)
            o_ref[:, sl] = _dot(p, v) / l
            lse_ref[:, sl] = jnp.broadcast_to(m + jnp.log(l), (QBLK, HEAD))

    q_spec = pl.BlockSpec((QBLK, GW), lambda r, b: (b, r * cb + g))
    k_specs = _band_specs(nb, lambda r: r * cb + ng + g)
    v_specs = _band_specs(nb, lambda r: r * cb + 2 * ng + g)
    o_spec = pl.BlockSpec((QBLK, GW), lambda r, b: (b, r))
    shape = jax.ShapeDtypeStruct((L, dil * GW), F32)
    o, lse = pl.pallas_call(
        body, name=name, grid=(dil, nb), in_specs=[q_spec] + k_specs + v_specs,
        out_specs=[o_spec, o_spec], out_shape=[shape, shape],
        compiler_params=_cp("parallel", "parallel"))(pv, pv, pv, pv, pv, pv, pv)
    return o.reshape(S, GW), lse.reshape(S, GW)


def _attn_merge(name, outs, lses, tr=256):
    S = outs[0].shape[0]
    tr = _tile(S, tr)
    ng = len(outs)

    def body(*refs):
        o_refs, l_refs = refs[:ng], refs[ng:2 * ng]
        tok_ref, lse_ref = refs[-2], refs[-1]
        ls = [r[...] for r in l_refs]
        m = functools.reduce(jnp.maximum, ls)
        es = [jnp.exp(l - m) for l in ls]
        tot = functools.reduce(lambda a, b: a + b, es)
        acc = None
        for e, o_ref in zip(es, o_refs):
            term = (e / tot) * o_ref[...]
            acc = term if acc is None else acc + term
        tok_ref[...] = acc
        lse_ref[...] = m + jnp.log(tot)

    row = pl.BlockSpec((tr, GW), lambda i: (i, 0))
    shape = jax.ShapeDtypeStruct((S, GW), F32)
    return pl.pallas_call(
        body, name=name, grid=(S // tr,), in_specs=[row] * (2 * ng), out_specs=[row, row],
        out_shape=[shape, shape], compiler_params=_cp("parallel"))(*outs, *lses)


def _attn_bwd(name, proj, g, dtok_src, dtok_blk, tok, lse):
    window, dil = A_PATTERNS[g]
    n_side = (window // 2) // dil
    S, C = proj.shape
    L = S // dil
    nb = L // QBLK
    cb = C // GW
    ng = len(A_PATTERNS)
    pv = proj.reshape(L, dil * C)
    dcb = dtok_src.shape[1] // GW
    dv_ = dtok_src.reshape(L, dil * dtok_src.shape[1])
    ov = tok.reshape(L, dil * GW)
    lv = lse.reshape(L, dil * GW)

    def body(qp, qc, qn, kp, kc, kn, vp, vc, vn, dop, doc, don, op, oc, on, lp, lc, ln,
             dq_ref, dk_ref, dv_ref):
        b = pl.program_id(1)
        jq = b * QBLK + lax.broadcasted_iota(jnp.int32, (QBLK, 3 * QBLK), 0)
        jk = (b - 1) * QBLK + lax.broadcasted_iota(jnp.int32, (QBLK, 3 * QBLK), 1)
        rel = jnp.abs(jk - jq)
        mask = (rel <= n_side) & (jk >= 0) & (jk < L)
        dist = rel.astype(F32) * float(dil)
        jq3 = (b - 1) * QBLK + lax.broadcasted_iota(jnp.int32, (3 * QBLK, QBLK), 0)
        jk1 = b * QBLK + lax.broadcasted_iota(jnp.int32, (3 * QBLK, QBLK), 1)
        rel3 = jnp.abs(jk1 - jq3)
        mask3 = (rel3 <= n_side) & (jq3 >= 0) & (jq3 < L)
        dist3 = rel3.astype(F32) * float(dil)
        for hh in range(HPG):
            sl = slice(hh * HEAD, (hh + 1) * HEAD)
            one = slice(hh * HEAD, hh * HEAD + 1)
            slope = float(SLOPES[g * HPG + hh])
            q = qc[:, sl]
            do = doc[:, sl]
            k3 = _cat3((kp, kc, kn), sl)
            v3 = _cat3((vp, vc, vn), sl)
            delta = jnp.sum(do * oc[:, sl], axis=1, keepdims=True)
            s = _dot(q, k3, NT) * SCALE - slope * dist
            p = jnp.where(mask, jnp.exp(s - lc[:, one]), 0.0)
            ds = p * (_dot(do, v3, NT) - delta)
            dq_ref[:, sl] = _dot(ds, k3) * SCALE

            q3 = _cat3((qp, qc, qn), sl)
            do3 = _cat3((dop, doc, don), sl)
            o3 = _cat3((op, oc, on), sl)
            lse3 = _cat3((lp, lc, ln), sl)[:, :1]
            delta3 = jnp.sum(do3 * o3, axis=1, keepdims=True)
            k = kc[:, sl]
            v = vc[:, sl]
            s3 = _dot(q3, k, NT) * SCALE - slope * dist3
            p3 = jnp.where(mask3, jnp.exp(s3 - lse3), 0.0)
            ds3 = p3 * (_dot(do3, v, NT) - delta3)
            dv_ref[:, sl] = _dot(p3, do3, TN)
            dk_ref[:, sl] = _dot(ds3, q3, TN) * SCALE

    specs = (_band_specs(nb, lambda r: r * cb + g) + _band_specs(nb, lambda r: r * cb + ng + g)
             + _band_specs(nb, lambda r: r * cb + 2 * ng + g)
             + _band_specs(nb, lambda r: r * dcb + dtok_blk)
             + _band_specs(nb, lambda r: r) + _band_specs(nb, lambda r: r))
    o_spec = pl.BlockSpec((QBLK, GW), lambda r, b: (b, r))
    shape = jax.ShapeDtypeStruct((L, dil * GW), F32)
    outs = pl.pallas_call(
        body, name=name, grid=(dil, nb), in_specs=specs, out_specs=[o_spec] * 3, out_shape=[shape] * 3,
        compiler_params=_cp("parallel", "parallel"))(*([pv] * 9 + [dv_] * 3 + [ov] * 3 + [lv] * 3))
    return [o.reshape(S, GW) for o in outs]


def _mem_fwd(name, proj, q_blk, kv, tq=256):
    S = proj.shape[0]
    M = kv.shape[0]
    tq = _tile(S, tq)

    def body(q_ref, kv_ref, o_ref):
        for hh in range(HPG):
            sl = slice(hh * HEAD, (hh + 1) * HEAD)
            k = kv_ref[:, sl]
            v = kv_ref[:, GW + hh * HEAD:GW + (hh + 1) * HEAD]
            s = _dot(q_ref[:, sl], k, NT) * SCALE
            m = jnp.max(s, axis=1, keepdims=True)
            p = jnp.exp(s - m)
            p = p / jnp.sum(p, axis=1, keepdims=True)
            o_ref[:, sl] = _dot(p, v)

    return pl.pallas_call(
        body, name=name, grid=(S // tq,),
        in_specs=[pl.BlockSpec((tq, GW), lambda i: (i, q_blk)), pl.BlockSpec((M, 2 * GW), lambda i: (0, 0))],
        out_specs=pl.BlockSpec((tq, GW), lambda i: (i, 0)),
        out_shape=jax.ShapeDtypeStruct((S, GW), F32), compiler_params=_cp("parallel"))(proj, kv)


def _mem_bwd(name, proj, q_blk, kv, dcat, do_blk, tq=256):
    S = proj.shape[0]
    M = kv.shape[0]
    tq = _tile(S, tq)

    def body(q_ref, kv_ref, do_ref, dq_ref, dkv_ref):
        i = pl.program_id(0)
        for hh in range(HPG):
            sl = slice(hh * HEAD, (hh + 1) * HEAD)
            vsl = slice(GW + hh * HEAD, GW + (hh + 1) * HEAD)
            q = q_ref[:, sl]
            do = do_ref[:, sl]
            k = kv_ref[:, sl]
            v = kv_ref[:, vsl]
            s = _dot(q, k, NT) * SCALE
            m = jnp.max(s, axis=1, keepdims=True)
            p = jnp.exp(s - m)
            p = p / jnp.sum(p, axis=1, keepdims=True)
            dp = _dot(do, v, NT)
            ds = p * (dp - jnp.sum(dp * p, axis=1, keepdims=True))
            dq_ref[:, sl] = _dot(ds, k) * SCALE
            dk = _dot(ds, q, TN) * SCALE
            dvv = _dot(p, do, TN)

            @pl.when(i == 0)
            def _():
                dkv_ref[:, sl] = dk
                dkv_ref[:, vsl] = dvv

            @pl.when(i > 0)
            def _():
                dkv_ref[:, sl] += dk
                dkv_ref[:, vsl] += dvv

    return pl.pallas_call(
        body, name=name, grid=(S // tq,),
        in_specs=[pl.BlockSpec((tq, GW), lambda i: (i, q_blk)), pl.BlockSpec((M, 2 * GW), lambda i: (0, 0)),
                  pl.BlockSpec((tq, GW), lambda i: (i, do_blk))],
        out_specs=[pl.BlockSpec((tq, GW), lambda i: (i, 0)), pl.BlockSpec((M, 2 * GW), lambda i: (0, 0))],
        out_shape=[jax.ShapeDtypeStruct((S, GW), F32), jax.ShapeDtypeStruct((M, 2 * GW), F32)],
        compiler_params=_cp("arbitrary"))(proj, kv, dcat)


_RSQRT2 = float(1.0 / np.sqrt(2.0))
_RSQRT2PI = float(1.0 / np.sqrt(2.0 * np.pi))


def _gelu(x):
    return 0.5 * x * (1.0 + lax.erf(x * _RSQRT2))


def _gelu_grad(x):
    return 0.5 * (1.0 + lax.erf(x * _RSQRT2)) + x * jnp.exp(-0.5 * x * x) * _RSQRT2PI


def _sgu_fwd(name, proj, gv, w_s, bias_t):
    S = proj.shape[0]
    nch = S // HEAD

    def body(u_ref, v_ref, gv_ref, ws_ref, b_ref, o_ref):
        v = _gelu(v_ref[...])
        r = lax.rsqrt(jnp.mean(v * v, axis=-1, keepdims=True) + EPS)
        vn = v * r * gv_ref[...]
        for gg in range(B_GROUPS):
            sl = slice(gg * HEAD, (gg + 1) * HEAD)
            mixed = _dot(ws_ref[gg], vn[:, sl]) + b_ref[:, gg:gg + 1]
            o_ref[:, sl] = _gelu(u_ref[:, sl]) * mixed

    return pl.pallas_call(
        body, name=name, grid=(nch,),
        in_specs=[pl.BlockSpec((HEAD, B_W), lambda c: (c, 0)), pl.BlockSpec((HEAD, B_W), lambda c: (c, 1)),
                  pl.BlockSpec((1, B_W), lambda c: (0, 0)),
                  pl.BlockSpec((B_GROUPS, HEAD, HEAD), lambda c: (0, 0, 0)),
                  pl.BlockSpec((HEAD, B_GROUPS), lambda c: (0, 0))],
        out_specs=pl.BlockSpec((HEAD, B_W), lambda c: (c, 0)),
        out_shape=jax.ShapeDtypeStruct((S, B_W), F32),
        compiler_params=_cp("parallel"))(proj, proj, gv.reshape(1, B_W), w_s, bias_t)


def _sgu_bwd(name, proj, gv, w_s, bias_t, dcat):
    S = proj.shape[0]
    nch = S // HEAD

    def body(u_ref, v_ref, gv_ref, ws_ref, b_ref, dt_ref, du_ref, dvp_ref, dgv_ref, dws_ref, db_ref, dvn_ref):
        c = pl.program_id(0)
        vpre = v_ref[...]
        v = _gelu(vpre)
        r = lax.rsqrt(jnp.mean(v * v, axis=-1, keepdims=True) + EPS)
        vh = v * r
        gvv = gv_ref[...]
        vn = vh * gvv
        for gg in range(B_GROUPS):
            sl = slice(gg * HEAD, (gg + 1) * HEAD)
            upre = u_ref[:, sl]
            dt = dt_ref[:, sl]
            vng = vn[:, sl]
            mixed = _dot(ws_ref[gg], vng) + b_ref[:, gg:gg + 1]
            du_ref[:, sl] = dt * mixed * _gelu_grad(upre)
            dmix = dt * _gelu(upre)
            dvn_ref[:, sl] = _dot(ws_ref[gg], dmix, TN)
            dws = _dot(dmix, vng, NT)
            dbs = jnp.sum(dmix, axis=1, keepdims=True)

            @pl.when(c == 0)
            def _():
                dws_ref[gg] = dws
                db_ref[:, gg:gg + 1] = dbs

            @pl.when(c > 0)
            def _():
                dws_ref[gg] += dws
                db_ref[:, gg:gg + 1] += dbs

        dvn = dvn_ref[...]
        dgp = jnp.sum(dvn * vh, axis=0, keepdims=True)
        dvh = dvn * gvv
        dv = r * (dvh - vh * jnp.mean(dvh * vh, axis=-1, keepdims=True))
        dvp_ref[...] = dv * _gelu_grad(vpre)

        @pl.when(c == 0)
        def _():
            dgv_ref[...] = dgp

        @pl.when(c > 0)
        def _():
            dgv_ref[...] += dgp

    blk = lambda j: pl.BlockSpec((HEAD, B_W), lambda c: (c, j))
    vec = pl.BlockSpec((1, B_W), lambda c: (0, 0))
    ws_spec = pl.BlockSpec((B_GROUPS, HEAD, HEAD), lambda c: (0, 0, 0))
    b_spec = pl.BlockSpec((HEAD, B_GROUPS), lambda c: (0, 0))
    du, dvp, dgv, dws, db = pl.pallas_call(
        body, name=name, grid=(nch,),
        in_specs=[blk(0), blk(1), vec, ws_spec, b_spec, blk(0)],
        out_specs=[blk(0), blk(0), vec, ws_spec, b_spec],
        out_shape=[jax.ShapeDtypeStruct((S, B_W), F32), jax.ShapeDtypeStruct((S, B_W), F32),
                   jax.ShapeDtypeStruct((1, B_W), F32), jax.ShapeDtypeStruct((B_GROUPS, HEAD, HEAD), F32),
                   jax.ShapeDtypeStruct((HEAD, B_GROUPS), F32)],
        scratch_shapes=[pltpu.VMEM((HEAD, B_W), F32)],
        compiler_params=_cp("arbitrary"))(proj, proj, gv.reshape(1, B_W), w_s, bias_t, dcat)
    return du, dvp, dgv, dws, db


def _shift_down(a, row):
    return jnp.where(row == 0, 0.0, pltpu.roll(a, 1, 0))


def _shift_up(a, row):
    n = a.shape[0]
    return jnp.where(row == n - 1, 0.0, pltpu.roll(a, n - 1, 0))


def _conv(a, w, b, row):
    return _shift_down(a, row) * w[0:1] + a * w[1:2] + _shift_up(a, row) * w[2:3] + b


def _conv_fwd(name, a3, cw, cb, tc=256):
    _, S, FF = a3.shape
    tc = _tile(FF, tc)

    def body(a_ref, w_ref, b_ref, o_ref):
        row = lax.broadcasted_iota(jnp.int32, (S, tc), 0)
        cg = _conv(a_ref[0], w_ref[0], b_ref[0], row)
        cv = _conv(a_ref[1], w_ref[1], b_ref[1], row)
        o_ref[...] = (_gelu(cg) * cv).astype(o_ref.dtype)

    return pl.pallas_call(
        body, name=name, grid=(FF // tc,),
        in_specs=[pl.BlockSpec((2, S, tc), lambda j: (0, 0, j)), pl.BlockSpec((2, 3, tc), lambda j: (0, 0, j)),
                  pl.BlockSpec((2, 1, tc), lambda j: (0, 0, j))],
        out_specs=pl.BlockSpec((S, tc), lambda j: (0, j)),
        out_shape=jax.ShapeDtypeStruct((S, FF), BF16), compiler_params=_cp("parallel"))(a3, cw, cb)


def _conv_bwd(name, a3, cw, cb, dact, tc=128):
    _, S, FF = a3.shape
    tc = _tile(FF, tc)

    def body(a_ref, w_ref, b_ref, d_ref, da_ref, dw_ref, db_ref):
        row = lax.broadcasted_iota(jnp.int32, (S, tc), 0)
        ag, av = a_ref[0], a_ref[1]
        wg, wv = w_ref[0], w_ref[1]
        cg = _conv(ag, wg, b_ref[0], row)
        cv = _conv(av, wv, b_ref[1], row)
        d = d_ref[...]
        dcs = (d * cv * _gelu_grad(cg), d * _gelu(cg))
        for h, (dc, a, w) in enumerate(zip(dcs, (ag, av), (wg, wv))):
            da_ref[h] = _shift_up(dc, row) * w[0:1] + dc * w[1:2] + _shift_down(dc, row) * w[2:3]
            dw_ref[h, 0:1, :] = jnp.sum(dc * _shift_down(a, row), axis=0, keepdims=True)
            dw_ref[h, 1:2, :] = jnp.sum(dc * a, axis=0, keepdims=True)
            dw_ref[h, 2:3, :] = jnp.sum(dc * _shift_up(a, row), axis=0, keepdims=True)
            db_ref[h] = jnp.sum(dc, axis=0, keepdims=True)

    a_spec = pl.BlockSpec((2, S, tc), lambda j: (0, 0, j))
    w_spec = pl.BlockSpec((2, 3, tc), lambda j: (0, 0, j))
    b_spec = pl.BlockSpec((2, 1, tc), lambda j: (0, 0, j))
    return pl.pallas_call(
        body, name=name, grid=(FF // tc,),
        in_specs=[a_spec, w_spec, b_spec, pl.BlockSpec((S, tc), lambda j: (0, j))],
        out_specs=[a_spec, w_spec, b_spec],
        out_shape=[jax.ShapeDtypeStruct((2, S, FF), F32), jax.ShapeDtypeStruct((2, 3, FF), F32),
                   jax.ShapeDtypeStruct((2, 1, FF), F32)],
        compiler_params=_cp("parallel"))(a3, cw, cb, dact)


_HBM = pl.BlockSpec(memory_space=pltpu.HBM)


def _position():
    return lax.axis_index("x"), lax.axis_index("y"), lax.axis_index("c")


def _all_gather(name, x, layer=None):
    block = x.shape if layer is None else x.shape[1:]

    def body(x_ref, out_ref, send_sems, recv_sems, local_sem):
        px, py, pc = _position()
        me, sibling = (px, py, pc), (px, py, 1 - pc)
        chips = [(1 - px, py), (px, 1 - py), (1 - px, 1 - py)]
        src = x_ref if layer is None else x_ref.at[layer]

        def slot(qx, qy, qc):
            return out_ref.at[4 * qx + 2 * qy + qc]

        def copy(k, blockpos, to, from_src=False):
            return pltpu.make_async_remote_copy(
                src_ref=src if from_src else slot(*blockpos), dst_ref=slot(*blockpos),
                send_sem=send_sems.at[k], recv_sem=recv_sems.at[k], device_id=to, device_id_type=MESH)

        mine = pltpu.make_async_copy(src, slot(*me), local_sem)
        mine.start()
        first = [copy(0, me, sibling, True)]
        first += [copy(1 + j, me, (*chip, pc), True) for j, chip in enumerate(chips)]
        for cp in first:
            cp.start()
        passed = [copy(4 + j, (*chip, pc), sibling) for j, chip in enumerate(chips)]
        for j, chip in enumerate(chips):
            copy(1 + j, (*chip, pc), me).wait_recv()
            passed[j].start()
        copy(0, sibling, me).wait_recv()
        for j, chip in enumerate(chips):
            copy(4 + j, (*chip, 1 - pc), me).wait_recv()
        for cp in first + passed:
            cp.wait_send()
        mine.wait()

    return pl.pallas_call(
        body, name=name, in_specs=[_HBM], out_specs=_HBM,
        out_shape=jax.ShapeDtypeStruct((N_DEV,) + tuple(block), x.dtype),
        scratch_shapes=[pltpu.SemaphoreType.DMA((7,)), pltpu.SemaphoreType.DMA((7,)), pltpu.SemaphoreType.DMA(())],
    )(x)


_SEM = pl.BlockSpec(memory_space=pltpu.SEMAPHORE)
_EFFECT = pltpu.SideEffectType.DATAFLOW_SIDE_EFFECTING
_FLIPS = ((1, 0), (0, 1), (1, 1))


def _split_start(name, bufs, ncopy, plan, after=()):
    n = len(bufs)
    after = list(after)

    def body(*refs):
        ins = refs[:n]
        send_sems, recv_sems, token = refs[n + len(after)], refs[n + len(after) + 1], refs[-1]
        for i, (src, dst, to) in enumerate(plan(ins)):
            pltpu.make_async_remote_copy(src_ref=src, dst_ref=dst, send_sem=send_sems.at[i], recv_sem=recv_sems.at[i],
                                         device_id=to, device_id_type=MESH).start()
        token[...] = jnp.zeros_like(token)

    outs = pl.pallas_call(
        body, name=name,
        out_shape=(pltpu.SemaphoreType.DMA((ncopy,)), pltpu.SemaphoreType.DMA((ncopy,)),
                   *[pltpu.HBM(b.shape, b.dtype) for b in bufs], jax.ShapeDtypeStruct((8, 128), F32)),
        in_specs=[_HBM] * n + [pl.BlockSpec(memory_space=pl.ANY)] * len(after),
        out_specs=(_SEM, _SEM, *([_HBM] * n), pl.BlockSpec(memory_space=pltpu.VMEM)),
        input_output_aliases={i: 2 + i for i in range(n)},
        compiler_params=pltpu.CompilerParams(has_side_effects=_EFFECT),
    )(*[pltpu.with_memory_space_constraint(b, pltpu.HBM) for b in bufs], *after)
    return outs[0], outs[1], list(outs[2:2 + n]), outs[-1]


def _split_wait(name, bufs, send_sems, recv_sems, plan, after):
    n = len(bufs)
    after = list(after)

    def body(*refs):
        ins = refs[:n]
        ssem, rsem = refs[n], refs[n + 1]
        for i, (src, dst, to) in enumerate(plan(ins)):
            cp = pltpu.make_async_remote_copy(src_ref=src, dst_ref=dst, send_sem=ssem.at[i], recv_sem=rsem.at[i],
                                              device_id=to, device_id_type=MESH)
            cp.wait_send()
            cp.wait_recv()

    outs = pl.pallas_call(
        body, name=name, out_shape=tuple(pltpu.HBM(b.shape, b.dtype) for b in bufs),
        in_specs=[_HBM] * n + [_SEM, _SEM] + [pl.BlockSpec(memory_space=pl.ANY)] * len(after),
        out_specs=tuple([_HBM] * n), input_output_aliases={i: i for i in range(n)},
        compiler_params=pltpu.CompilerParams(has_side_effects=_EFFECT),
    )(*bufs, send_sems, recv_sems, *after)
    return list(outs)


def _gather_plan(refs):
    px, py, pc = _position()
    me = 4 * px + 2 * py + pc
    targets = [(px, py, 1 - pc), (1 - px, py, pc), (px, 1 - py, pc), (1 - px, 1 - py, pc)]
    return [(r.at[me], r.at[me], to) for r in refs for to in targets]


def _gather_forward(name, lands):
    n = len(lands)

    def body(*refs):
        outs, send_sems, recv_sems = refs[n:2 * n], refs[2 * n], refs[2 * n + 1]
        px, py, pc = _position()
        copies = []
        for w, ref in enumerate(outs):
            for j, (fx, fy) in enumerate(_FLIPS):
                qx = 1 - px if fx else px
                qy = 1 - py if fy else py
                mine, theirs = 4 * qx + 2 * qy + pc, 4 * qx + 2 * qy + (1 - pc)
                send = pltpu.make_async_remote_copy(
                    src_ref=ref.at[mine], dst_ref=ref.at[mine], send_sem=send_sems.at[3 * w + j],
                    recv_sem=recv_sems.at[3 * w + j], device_id=(px, py, 1 - pc), device_id_type=MESH)
                recv = pltpu.make_async_remote_copy(
                    src_ref=ref.at[theirs], dst_ref=ref.at[theirs], send_sem=send_sems.at[3 * w + j],
                    recv_sem=recv_sems.at[3 * w + j], device_id=(px, py, 1 - pc), device_id_type=MESH)
                copies.append((send, recv))
        for send, _ in copies:
            send.start()
        for send, recv in copies:
            send.wait_send()
            recv.wait_recv()

    outs = pl.pallas_call(
        body, name=name, in_specs=[_HBM] * n, out_specs=tuple([_HBM] * n),
        out_shape=tuple(jax.ShapeDtypeStruct(l.shape, l.dtype) for l in lands),
        input_output_aliases={i: i for i in range(n)},
        scratch_shapes=[pltpu.SemaphoreType.DMA((3 * n,)), pltpu.SemaphoreType.DMA((3 * n,))])(*lands)
    return list(outs)


def _pair_plan(n):
    def plan(refs):
        px, py, pc = _position()
        return [(refs[w].at[2 * k + (1 - pc)], refs[n + w].at[k], (px, py, 1 - pc)) for w in range(n) for k in range(4)]
    return plan


def _chip_plan(n):
    def plan(refs):
        px, py, pc = _position()
        out = []
        for w in range(n):
            for j, (fx, fy) in enumerate(_FLIPS):
                qx = 1 - px if fx else px
                qy = 1 - py if fy else py
                out.append((refs[w].at[2 * qx + qy], refs[n + w].at[j], (qx, qy, pc)))
        return out
    return plan


def _broadcast_plan(refs):
    px, py, pc = _position()
    me = 4 * px + 2 * py + pc
    flips = [(fx, fy, fc) for fx in (0, 1) for fy in (0, 1) for fc in (0, 1)][1:]
    targets = [(1 - px if fx else px, 1 - py if fy else py, 1 - pc if fc else pc) for fx, fy, fc in flips]
    return [(r.at[me], r.at[me], to) for r in refs for to in targets]


def _cast_place(name, dev, w, layer, dtype=BF16):
    nl, R, C = w.shape
    tr = _row_tile(R, C)

    def body(dev_ref, w_ref, o_ref):
        o_ref[...] = w_ref[...].astype(o_ref.dtype)

    return pl.pallas_call(
        body, name=name,
        grid_spec=pltpu.PrefetchScalarGridSpec(
            num_scalar_prefetch=1, grid=(R // tr,),
            in_specs=[pl.BlockSpec((None, tr, C), lambda i, d: (layer, i, 0))],
            out_specs=pl.BlockSpec((None, tr, C), lambda i, d: (d[0], i, 0))),
        out_shape=jax.ShapeDtypeStruct((N_DEV, R, C), dtype), compiler_params=_cp("parallel"))(dev, w)


def _pair_sum(name, core, dw, recv):
    _, R, C = dw.shape
    tr = _row_tile(R, C)
    dw4 = dw.reshape(4, 2, R, C)

    def body(core_ref, a_ref, b_ref, o_ref):
        o_ref[...] = (a_ref[...] + b_ref[...]).astype(o_ref.dtype)

    return pl.pallas_call(
        body, name=name,
        grid_spec=pltpu.PrefetchScalarGridSpec(
            num_scalar_prefetch=1, grid=(4, R // tr),
            in_specs=[pl.BlockSpec((None, None, tr, C), lambda k, i, c_ref: (k, c_ref[0], i, 0)),
                      pl.BlockSpec((None, tr, C), lambda k, i, c_ref: (k, i, 0))],
            out_specs=pl.BlockSpec((None, tr, C), lambda k, i, c_ref: (k, i, 0))),
        out_shape=jax.ShapeDtypeStruct((4, R, C), BF16),
        compiler_params=_cp("parallel", "parallel"))(core, dw4, recv)


def _adamw_math(w, g, m, v):
    m = ADAM_B1 * m + (1.0 - ADAM_B1) * g
    v = ADAM_B2 * v + (1.0 - ADAM_B2) * (g * g)
    m_hat = m / (1.0 - ADAM_B1 ** ADAM_STEP)
    v_hat = v / (1.0 - ADAM_B2 ** ADAM_STEP)
    delta = -ADAM_LR * (m_hat / (jnp.sqrt(v_hat) + ADAM_EPS) + ADAM_WD * w)
    return delta, m, v


def _adamw_shard(name, chip, layer, w, m, v, p, recv, prev, deps=()):
    nl, R, C = w.shape
    tr = _row_tile(R, C)
    n_prev = 0 if prev is None else 4

    def body(chip_ref, w_ref, m_ref, v_ref, p_ref, r_ref, *rest):
        g_ref, d_ref, nm_ref, nv_ref = rest[-4:]
        g = p_ref[...].astype(F32)
        for j in range(3):
            g = g + r_ref[j].astype(F32)
        delta, nm, nv = _adamw_math(w_ref[...], g, m_ref[...], v_ref[...])
        g_ref[...] = g
        d_ref[...] = delta
        nm_ref[...] = nm
        nv_ref[...] = nv

    lay = pl.BlockSpec((None, tr, C), lambda i, c_ref: (layer, i, 0))
    in_specs = [lay, lay, lay,
                pl.BlockSpec((None, tr, C), lambda i, c_ref: (c_ref[0], i, 0)),
                pl.BlockSpec((3, tr, C), lambda i, c_ref: (0, i, 0))]
    in_specs += [pl.BlockSpec(memory_space=pl.ANY)] * n_prev + [_DEP] * len(deps)
    shape = jax.ShapeDtypeStruct((nl, R, C), F32)
    ins = [chip, w, m, v, p, recv] + ([] if prev is None else list(prev)) + list(deps)
    return pl.pallas_call(
        body, name=name,
        grid_spec=pltpu.PrefetchScalarGridSpec(
            num_scalar_prefetch=1, grid=(R // tr,), in_specs=in_specs, out_specs=[lay] * 4),
        out_shape=[shape] * 4,
        input_output_aliases={6 + j: j for j in range(n_prev)},
        compiler_params=_cp("parallel"))(*ins)


def _sum_slots(name, parts, tr=512):
    n, R, C = parts.shape
    tr = _tile(R, tr)

    def body(p_ref, o_ref):
        acc = p_ref[0]
        for j in range(1, n):
            acc = acc + p_ref[j]
        o_ref[...] = acc

    return pl.pallas_call(
        body, name=name, grid=(R // tr,),
        in_specs=[pl.BlockSpec((n, tr, C), lambda i: (0, i, 0))],
        out_specs=pl.BlockSpec((tr, C), lambda i: (i, 0)),
        out_shape=jax.ShapeDtypeStruct((R, C), F32), compiler_params=_cp("parallel"))(parts)


def _adamw_flat(name, w, g, m, v, tr=512):
    R, C = w.shape
    tr = _tile(R, tr)

    def body(w_ref, g_ref, m_ref, v_ref, d_ref, nm_ref, nv_ref):
        delta, nm, nv = _adamw_math(w_ref[...], g_ref[...], m_ref[...], v_ref[...])
        d_ref[...] = delta
        nm_ref[...] = nm
        nv_ref[...] = nv

    row = pl.BlockSpec((tr, C), lambda i: (i, 0))
    shape = jax.ShapeDtypeStruct((R, C), F32)
    return pl.pallas_call(
        body, name=name, grid=(R // tr,), in_specs=[row] * 4, out_specs=[row] * 3, out_shape=[shape] * 3,
        compiler_params=_cp("parallel"))(w, g, m, v)


_PACK_ROWS = 512


def _pack(arrs):
    flat = jnp.concatenate([a.reshape(-1) for a in arrs])
    unit = _PACK_ROWS * 128
    pad = (-flat.shape[0]) % unit
    return jnp.pad(flat, (0, pad)).reshape(-1, 128)


def _unpack(packed, shapes):
    flat = packed.reshape(-1)
    outs, off = [], 0
    for s in shapes:
        n = int(np.prod(s))
        outs.append(flat[off:off + n].reshape(s))
        off += n
    return outs


def kernel(x, mem, mix_norm_g, ffn_norm_g, mem_norm_g, w_mem_kv, a_w_in, a_w_out, b_w_in, b_v_norm_g, b_w_s, b_s_bias, b_w_out, ffn_w_up, ffn_conv_w, ffn_conv_b, ffn_w_down, final_norm_g, loss_target, m_mix_norm_g, m_ffn_norm_g, m_mem_norm_g, m_w_mem_kv, m_a_w_in, m_a_w_out, m_b_w_in, m_b_v_norm_g, m_b_w_s, m_b_s_bias, m_b_w_out, m_ffn_w_up, m_ffn_conv_w, m_ffn_conv_b, m_ffn_w_down, m_final_norm_g, v_mix_norm_g, v_ffn_norm_g, v_mem_norm_g, v_w_mem_kv, v_a_w_in, v_a_w_out, v_b_w_in, v_b_v_norm_g, v_b_w_s, v_b_s_bias, v_b_w_out, v_ffn_w_up, v_ffn_conv_w, v_ffn_conv_b, v_ffn_w_down, v_final_norm_g):
    px, py, pc = _position()
    dev = 4 * px + 2 * py + pc
    core = jnp.reshape(pc, (1,)).astype(jnp.int32)
    chip = jnp.reshape(2 * px + py, (1,)).astype(jnp.int32)

    x0 = x[0]
    mem0 = mem[0]
    tgt = loss_target[0]
    S, D = x0.shape
    depth = mix_norm_g.shape[0]
    FF = ffn_w_down.shape[1] * N_DEV
    a_in = a_w_in.shape[2] * N_DEV
    b_in = b_w_in.shape[2] * N_DEV
    a_q_blk = (a_in - GW) // GW
    b_q_blk = (b_in - GW) // GW

    stacks = {"kv": (w_mem_kv, m_w_mem_kv, v_w_mem_kv), "ain": (a_w_in, m_a_w_in, v_a_w_in),
              "aout": (a_w_out, m_a_w_out, v_a_w_out), "bin": (b_w_in, m_b_w_in, v_b_w_in),
              "bout": (b_w_out, m_b_w_out, v_b_w_out), "up": (ffn_w_up, m_ffn_w_up, v_ffn_w_up),
              "down": (ffn_w_down, m_ffn_w_down, v_ffn_w_down)}
    dev1 = jnp.reshape(dev, (1,)).astype(jnp.int32)

    def groups_of(i):
        j = i // 2
        mix = [("kv", i), ("ain", j), ("aout", j)] if i % 2 == 0 else [("kv", i), ("bin", j), ("bout", j)]
        return mix, [("up", i), ("down", i)]

    gather_groups = [(f"{half}{i}", members) for i in range(depth) for half, members in zip("mf", groups_of(i))]
    gather_ahead = 2
    in_flight = {}

    def gather_start(k, after):
        gname, members = gather_groups[k]
        lands = [_cast_place(f"place_{t}{l}", dev1, stacks[t][0], l) for t, l in members]
        ssem, rsem, lands, tok = _split_start(f"ag_start_{gname}", lands, 4 * len(lands), _gather_plan, after)
        in_flight[k] = (lands, ssem, rsem)
        return tok

    start_tokens = [gather_start(0, ())]

    def gathered(k, after):
        gname, members = gather_groups[k]
        lands, ssem, rsem = in_flight.pop(k)
        lands = _split_wait(f"ag_wait_{gname}", lands, ssem, rsem, _gather_plan, [after])
        toks = []
        for q in (range(1, 1 + gather_ahead) if k == 0 else [k + gather_ahead]):
            if q < len(gather_groups):
                toks.append(gather_start(q, [lands[0]] + toks))
        lands = _gather_forward(f"ag_pass_{gname}", lands)
        out = {}
        for (t, l), land in zip(members, lands):
            if t == "kv":
                out["kv"] = land.reshape(D, 2 * GW)
            elif t in ("ain", "aout", "up"):
                out[{"ain": "in", "aout": "out", "up": "up"}[t]] = land
            elif t == "bin":
                out["in"] = jnp.transpose(land, (1, 0, 2)).reshape(D, b_in)
            elif t == "bout":
                out["out"] = land.reshape(B_W + GW, D)
            else:
                out["down"] = land.reshape(FF, D)
        return out, toks

    small_local = _pack([ffn_conv_w, b_v_norm_g])
    small_all = _all_gather("ag_small", small_local)
    cw_parts, gv_parts = [], []
    for d in range(N_DEV):
        cw_d, gv_d = _unpack(small_all[d], [ffn_conv_w.shape, b_v_norm_g.shape])
        cw_parts.append(cw_d)
        gv_parts.append(gv_d)
    conv_w_full = jnp.concatenate(cw_parts, axis=-1)
    gv_full = jnp.concatenate(gv_parts, axis=-1)

    def conv_params(i):
        cw = conv_w_full[i].reshape(3, 2, FF).transpose(1, 0, 2)
        cb = ffn_conv_b[i].reshape(2, 1, FF)
        return cw, cb

    saved = []
    W = []
    xc = x0
    for i in range(depth):
        j = i // 2
        lw, toks = gathered(2 * i, xc)
        sv = {"x0": xc}
        h1, h1t = _rms_fwd(f"mixnorm{i}", xc, mix_norm_g[i], deps=toks + (start_tokens if i == 0 else []),
                           with_t=True)
        memn = _rms_fwd(f"memnorm{i}", mem0, mem_norm_g[i])
        kv = _mm_full(f"kvproj{i}", memn, lw["kv"])
        if i % 2 == 0:
            proj = _mm_gcols(f"ain{i}", h1, lw["in"])
            outs, lses = [], []
            for g in range(len(A_PATTERNS)):
                o, l = _attn_fwd(f"attn{i}_{g}", proj, g)
                outs.append(o)
                lses.append(l)
            tok, lse = _attn_merge(f"merge{i}", outs, lses)
            mo = _mem_fwd(f"memattn{i}", proj, a_q_blk, kv)
            cat = jnp.concatenate([tok.astype(BF16), mo.astype(BF16)], axis=1)
            x1 = _mm_gcols(f"aout{i}", cat, lw["out"], res=xc)
            sv.update(tok=tok, lse=lse)
        else:
            proj = _mm_full(f"bin{i}", h1, lw["in"])
            bias_t = b_s_bias[j].T
            tok = _sgu_fwd(f"sgu{i}", proj, gv_full[j], b_w_s[j], bias_t)
            mo = _mem_fwd(f"memattn{i}", proj, b_q_blk, kv)
            cat = jnp.concatenate([tok.astype(BF16), mo.astype(BF16)], axis=1)
            x1 = _mm_full(f"bout{i}", cat, lw["out"], res=xc)
        lw_ffn, toks = gathered(2 * i + 1, x1)
        lw.update(lw_ffn)
        W.append(lw)
        h2, h2t = _rms_fwd(f"ffnnorm{i}", x1, ffn_norm_g[i], deps=toks, with_t=True)
        cw, cb = conv_params(i)
        a3 = _mm_gcols(f"up{i}", h2, lw["up"], split_out=True, tm=512)
        act = _conv_fwd(f"conv{i}", a3, cw, cb)
        x2 = _mm_full(f"down{i}", act, lw["down"], res=x1, tm=512, tn=1024, tk=FF // 4)
        sv.update(h1t=h1t, memn=memn, kv=kv, proj=proj, cat=cat, x1=x1, h2t=h2t, a3=a3, act=act)
        saved.append(sv)
        xc = x2

    dx, dg_final, sq = _final("final", xc, tgt, final_norm_g)
    loss_local = sq[0, 0] * (0.5 / D)

    chain = {}

    def pair_begin(gname, members, dws):
        n = len(dws)
        recvs = [lax.empty((4,) + dw.shape[1:], F32) for dw in dws]
        ssem, rsem, bufs, tok = _split_start(f"rs_pair_start_{gname}", dws + recvs, 4 * n, _pair_plan(n))
        return dict(name=gname, members=members, n=n, bufs=bufs, sems=(ssem, rsem)), tok

    def pair_end_chip_begin(st, after):
        n, gname = st["n"], st["name"]
        bufs = _split_wait(f"rs_pair_wait_{gname}", st["bufs"], *st["sems"], _pair_plan(n), after)
        ps = [_pair_sum(f"rs_sum_{t}{l}", core, bufs[w], bufs[n + w]) for w, (t, l) in enumerate(st["members"])]
        recvs = [lax.empty((3,) + p.shape[1:], BF16) for p in ps]
        ssem, rsem, bufs, tok = _split_start(f"rs_chip_start_{gname}", ps + recvs, 3 * n, _chip_plan(n))
        return dict(name=gname, members=st["members"], n=n, bufs=bufs, sems=(ssem, rsem)), tok

    def chip_end_update(st, after, deps=()):
        n = st["n"]
        bufs = _split_wait(f"rs_chip_wait_{st['name']}", st["bufs"], *st["sems"], _chip_plan(n), after)
        for w, (t, l) in enumerate(st["members"]):
            wst, mst, vst = stacks[t]
            chain[t] = _adamw_shard(f"adamw_{t}{l}", chip, l, wst, mst, vst, bufs[w], bufs[n + w], chain.get(t), deps)

    pipe = {"pair": [], "chip": [], "deps": []}

    def take_deps():
        deps, pipe["deps"] = pipe["deps"], []
        return deps

    def submit(gname, members, dws):
        st, tok = pair_begin(gname, members, dws)
        pipe["pair"].append(st)
        pipe["deps"].append(tok)

    def advance(after):
        arrived, pipe["chip"] = pipe["chip"], []
        toks = []
        for st in pipe["pair"]:
            new, tok = pair_end_chip_begin(st, [after])
            pipe["chip"].append(new)
            toks.append(tok)
        pipe["pair"] = []
        for st in arrived:
            chip_end_update(st, [after], toks)
        pipe["deps"] += toks

    def small_start(tag, arrs, after):
        land = _cast_place(f"place_small_{tag}", dev1, _pack(arrs)[None], 0, F32)
        ssem, rsem, lands, tok = _split_start(f"small_start_{tag}", [land], N_DEV - 1, _broadcast_plan, after)
        return (lands, ssem, rsem), tok

    def small_end(tag, state, shapes, after):
        lands, ssem, rsem = state
        lands = _split_wait(f"small_wait_{tag}", lands, ssem, rsem, _broadcast_plan, after)
        return _unpack(_sum_slots(f"small_sum_{tag}", lands[0]), shapes)

    def late_small():
        return [dg_mix[0], dg_ffn[0], dg_mem[0], d_conv_b[0][None], d_conv_w[0][None]]

    assert depth >= 2
    big = {k: [None] * n for k, n in (("kv", depth), ("ain", depth // 2 + depth % 2), ("aout", depth // 2 + depth % 2),
                                      ("bin", depth // 2), ("bout", depth // 2), ("up", depth), ("down", depth))}
    dg_mix, dg_ffn, dg_mem = [None] * depth, [None] * depth, [None] * depth
    d_conv_w, d_conv_b = [None] * depth, [None] * depth
    d_gv, d_ws, d_sb = [None] * (depth // 2), [None] * (depth // 2), [None] * (depth // 2)
    for i in reversed(range(depth)):
        j = i // 2
        lw, sv = W[i], saved[i]
        cw, cb = conv_params(i)
        mix_members, ffn_members = groups_of(i)
        if i == 0:
            early_arrays = [loss_local.reshape(1), dg_final.reshape(D), jnp.concatenate(dg_mix[1:]),
                            jnp.concatenate(dg_ffn[1:]), jnp.concatenate(dg_mem[1:]), jnp.stack(d_ws), jnp.stack(d_sb),
                            jnp.stack(d_gv), jnp.stack(d_conv_b[1:]), jnp.stack(d_conv_w[1:])]
            early_state, tok = small_start("early", early_arrays, [dx])
            pipe["deps"].append(tok)
        deps = take_deps()
        dact = _mm_dx_full(f"ddown{i}", dx, lw["down"], tm=512, tko=FF // 4, tc=D, deps=deps)
        big["down"][i] = _mm_dw(f"wdown{i}", sv["act"], dx, deps=deps).reshape(N_DEV, FF // N_DEV, D)
        da3, dcw, dcb = _conv_bwd(f"dconv{i}", sv["a3"], cw, cb, dact)
        d_conv_w[i] = dcw.transpose(1, 0, 2).reshape(3, 2 * FF)
        d_conv_b[i] = dcb.reshape(2 * FF)
        advance(da3)
        deps = take_deps()
        dh2 = _mm_dx_gcols(f"dup{i}", da3, lw["up"], split_in=True, tm=512, deps=deps)
        big["up"][i] = _mm_dw_gcols(f"wup{i}", sv["h2t"], da3, N_DEV, split_in=True, tko=512, deps=deps, a_t=True)
        dx1, dg_ffn[i] = _rms_bwd(f"dffnnorm{i}", dh2, sv["x1"], ffn_norm_g[i], dx)
        submit(f"f{i}", ffn_members, [big["up"][i], big["down"][i]])
        deps = take_deps()
        if i % 2 == 0:
            dcat = _mm_dx_gcols(f"daout{i}", dx1, lw["out"], deps=deps)
            big["aout"][j] = _mm_dw_gcols(f"waout{i}", sv["cat"], dx1, N_DEV, deps=deps)
            parts = [None] * 9
            for g in range(len(A_PATTERNS)):
                dq, dk, dv = _attn_bwd(f"dattn{i}_{g}", sv["proj"], g, dcat, 0, sv["tok"], sv["lse"])
                parts[g], parts[3 + g], parts[6 + g] = dq, dk, dv
            dqm, dkv = _mem_bwd(f"dmemattn{i}", sv["proj"], a_q_blk, sv["kv"], dcat, 1)
            dproj = jnp.concatenate(parts + [dqm], axis=1)
            advance(dkv)
            deps = take_deps()
            dh1 = _mm_dx_gcols(f"dain{i}", dproj, lw["in"], deps=deps)
            big["ain"][j] = _mm_dw_gcols(f"wain{i}", sv["h1t"], dproj, N_DEV, deps=deps, a_t=True)
        else:
            dcat = _mm_dx_full(f"dbout{i}", dx1, lw["out"], tm=1024, deps=deps)
            big["bout"][j] = _mm_dw(f"wbout{i}", sv["cat"], dx1, deps=deps).reshape(N_DEV, (B_W + GW) // N_DEV, D)
            bias_t = b_s_bias[j].T
            du, dvp, dgv, dws, dbt = _sgu_bwd(f"dsgu{i}", sv["proj"], gv_full[j], b_w_s[j], bias_t, dcat)
            d_gv[j], d_ws[j], d_sb[j] = dgv.reshape(B_W), dws, dbt.T
            dqm, dkv = _mem_bwd(f"dmemattn{i}", sv["proj"], b_q_blk, sv["kv"], dcat, B_W // GW)
            dproj = jnp.concatenate([du, dvp, dqm], axis=1)
            advance(dkv)
            deps = take_deps()
            dh1 = _mm_dx_full(f"dbin{i}", dproj, lw["in"], tc=b_in // 2, deps=deps)
            dwin = _mm_dw(f"wbin{i}", sv["h1t"], dproj, tko=1024, tn=512, deps=deps, a_t=True)
            big["bin"][j] = dwin.reshape(D, N_DEV, b_in // N_DEV).transpose(1, 0, 2)
        big["kv"][i] = _mm_dw(f"wkv{i}", sv["memn"], dkv, tko=1024).reshape(N_DEV, D // N_DEV, 2 * GW)
        dmemn = _mm_dx_full(f"dkvproj{i}", dkv, lw["kv"], tko=1024)
        _, dg_mem[i] = _rms_bwd(f"dmemnorm{i}", dmemn, mem0, mem_norm_g[i])
        dx, dg_mix[i] = _rms_bwd(f"dmixnorm{i}", dh1, sv["x0"], mix_norm_g[i], dx1)
        submit(f"m{i}", mix_members, [big[t][l] for t, l in mix_members])
    grad_x = dx[None]

    last_chips, toks = [], []
    for st in pipe["pair"]:
        new, tok = pair_end_chip_begin(st, [dx])
        last_chips.append(new)
        toks.append(tok)
    late_state, late_tok = small_start("late", late_small(), toks)
    g_early = small_end("early", early_state, [a.shape for a in early_arrays], [dx])
    for st in pipe["chip"]:
        chip_end_update(st, [g_early[1]], [late_tok])
    g_late = small_end("late", late_state, [a.shape for a in late_small()], [chain[t][0] for t in chain])

    loss = g_early[0][0]
    layer0 = dict(zip(("mix", "ffn", "mem", "conv_b", "conv_w"), g_late))
    rest = dict(zip(("final", "mix", "ffn", "mem", "w_s", "s_bias", "gv", "conv_b", "conv_w"), g_early[1:]))
    g_cw_full = jnp.concatenate([layer0["conv_w"], rest["conv_w"]])
    g_gv = lax.dynamic_slice_in_dim(rest["gv"], dev * b_v_norm_g.shape[1], b_v_norm_g.shape[1], axis=1)
    g_cw = lax.dynamic_slice_in_dim(g_cw_full, dev * ffn_conv_w.shape[2], ffn_conv_w.shape[2], axis=2)
    g_all = [jnp.concatenate([layer0["mix"], rest["mix"]]), jnp.concatenate([layer0["ffn"], rest["ffn"]]),
             jnp.concatenate([layer0["mem"], rest["mem"]]), rest["w_s"], rest["s_bias"],
             jnp.concatenate([layer0["conv_b"], rest["conv_b"]]), rest["final"], g_gv, g_cw]
    names = ["mix_norm_g", "ffn_norm_g", "mem_norm_g", "b_w_s", "b_s_bias", "ffn_conv_b", "final_norm_g",
             "b_v_norm_g", "ffn_conv_w"]
    ws = [mix_norm_g, ffn_norm_g, mem_norm_g, b_w_s, b_s_bias, ffn_conv_b, final_norm_g, b_v_norm_g, ffn_conv_w]
    ms = [m_mix_norm_g, m_ffn_norm_g, m_mem_norm_g, m_b_w_s, m_b_s_bias, m_ffn_conv_b, m_final_norm_g,
          m_b_v_norm_g, m_ffn_conv_w]
    vs = [v_mix_norm_g, v_ffn_norm_g, v_mem_norm_g, v_b_w_s, v_b_s_bias, v_ffn_conv_b, v_final_norm_g,
          v_b_v_norm_g, v_ffn_conv_w]
    shapes = [w.shape for w in ws]
    d_p, m_p, v_p = _adamw_flat("adamw_small", _pack(ws), _pack(g_all), _pack(ms), _pack(vs))
    res = {}
    for n, g, d, nm, nv in zip(names, g_all, _unpack(d_p, shapes), _unpack(m_p, shapes), _unpack(v_p, shapes)):
        res[n] = [g, d, nm, nv]
    for st in last_chips:
        chip_end_update(st, [d_p] + [chain[t][0] for t in chain])
    for tag, name in (("kv", "w_mem_kv"), ("ain", "a_w_in"), ("aout", "a_w_out"), ("bin", "b_w_in"),
                      ("bout", "b_w_out"), ("up", "ffn_w_up"), ("down", "ffn_w_down")):
        res[name] = list(chain[tag])

    order = ["mix_norm_g", "ffn_norm_g", "mem_norm_g", "w_mem_kv", "a_w_in", "a_w_out", "b_w_in", "b_v_norm_g",
             "b_w_s", "b_s_bias", "b_w_out", "ffn_w_up", "ffn_conv_w", "ffn_conv_b", "ffn_w_down", "final_norm_g"]
    return (loss, grad_x, *[res[n][0] for n in order], *[res[n][1] for n in order],
            *[res[n][2] for n in order], *[res[n][3] for n in order])
```

```python
import functools

import numpy as np
import jax
import jax.numpy as jnp
from jax import lax
from jax.experimental import pallas as pl
from jax.experimental.pallas import tpu as pltpu

F32 = jnp.float32
BF16 = jnp.bfloat16
MESH = pl.DeviceIdType.MESH
AXES = ("x", "y", "c")
N_DEV = 8

EPS = 1e-6
NEG = -1e30
HEAD = 128
HPG = 4
GW = HPG * HEAD
A_PATTERNS = ((128, 1), (512, 4), (2048, 16))
A_HEADS = HPG * len(A_PATTERNS)
QBLK = 128
B_GROUPS = 12
B_W = B_GROUPS * HEAD
SLOPES = (2.0 ** (-8.0 * (np.arange(A_HEADS) + 1) / A_HEADS)).astype(np.float32)
SCALE = HEAD ** -0.5

ADAM_LR = 0.001
ADAM_B1 = 0.9
ADAM_B2 = 0.999
ADAM_EPS = 1e-08
ADAM_WD = 0.01
ADAM_STEP = 10

V7X_VMEM_LIMIT = 50 * 1024 * 1024

NN = (((1,), (0,)), ((), ()))
NT = (((1,), (1,)), ((), ()))
TN = (((0,), (0,)), ((), ()))


def _cp(*sem):
    return pltpu.CompilerParams(dimension_semantics=sem, vmem_limit_bytes=V7X_VMEM_LIMIT)


def _dot(a, b, dims=NN):
    return lax.dot_general(a.astype(BF16), b.astype(BF16), dims, preferred_element_type=F32)


def _tile(n, pref):
    t = min(n, pref)
    assert n % t == 0, (n, pref)
    return t


def _row_tile(rows, cols):
    best = None
    for t in range(16, rows + 1, 16):
        if rows % t == 0 and t * cols * 4 <= (1 << 20):
            best = t
    if best is None:
        best = rows
    return best


_DEP = pl.BlockSpec((8, 128), lambda *_: (0, 0))


def _matmul(name, dims, grid, a, a_spec, b, b_spec, out_shape, o_spec, tile, res=None, res_spec=None, deps=()):
    nk = grid[2]
    has_res = res is not None

    def body(*refs):
        a_ref, b_ref = refs[0], refs[1]
        r_ref = refs[2] if has_res else None
        o_ref, acc_ref = refs[-2], refs[-1]
        part = _dot(a_ref[...], b_ref[...], dims)

        def finish(val):
            if has_res:
                val = val + r_ref[...]
            o_ref[...] = val.astype(o_ref.dtype)

        if nk == 1:
            finish(part)
        else:
            k = pl.program_id(2)

            @pl.when(k == 0)
            def _():
                acc_ref[...] = part

            @pl.when(k > 0)
            def _():
                acc_ref[...] += part

            @pl.when(k == nk - 1)
            def _():
                finish(acc_ref[...])

    ins = [a, b] + ([res] if has_res else []) + list(deps)
    specs = [a_spec, b_spec] + ([res_spec] if has_res else []) + [_DEP] * len(deps)
    return pl.pallas_call(
        body, name=name, grid=grid, in_specs=specs, out_specs=o_spec, out_shape=out_shape,
        scratch_shapes=[pltpu.VMEM(tile if nk > 1 else (8, 128), F32)],
        compiler_params=_cp("parallel", "parallel", "arbitrary"))(*ins)


def _mm_full(name, a, w, res=None, tm=1024, tn=512, tk=2048, deps=()):
    M, K = a.shape
    N = w.shape[1]
    tm, tn, tk = _tile(M, tm), _tile(N, tn), _tile(K, tk)
    return _matmul(
        name, NN, (N // tn, M // tm, K // tk),
        a, pl.BlockSpec((tm, tk), lambda j, i, k: (i, k)),
        w, pl.BlockSpec((tk, tn), lambda j, i, k: (k, j)),
        jax.ShapeDtypeStruct((M, N), F32), pl.BlockSpec((tm, tn), lambda j, i, k: (i, j)), (tm, tn),
        res, pl.BlockSpec((tm, tn), lambda j, i, k: (i, j)), deps=deps)


def _mm_gcols(name, a, wg, res=None, split_out=False, tm=1024, deps=()):
    M, K = a.shape
    G, _, Nl = wg.shape
    tm = _tile(M, tm)
    hg = G // 2
    if split_out:
        shape = jax.ShapeDtypeStruct((2, M, hg * Nl), F32)
        o_spec = pl.BlockSpec((None, tm, Nl), lambda g, i, k: (g // hg, i, g % hg))
    else:
        shape = jax.ShapeDtypeStruct((M, G * Nl), F32)
        o_spec = pl.BlockSpec((tm, Nl), lambda g, i, k: (i, g))
    return _matmul(
        name, NN, (G, M // tm, 1),
        a, pl.BlockSpec((tm, K), lambda g, i, k: (i, 0)),
        wg, pl.BlockSpec((None, K, Nl), lambda g, i, k: (g, 0, 0)),
        shape, o_spec, (tm, Nl),
        res, pl.BlockSpec((tm, Nl), lambda g, i, k: (i, g)), deps=deps)


def _mm_dx_full(name, dy, w, tm=512, tko=512, tc=2048, deps=()):
    M, N = dy.shape
    K = w.shape[0]
    tm, tko, tc = _tile(M, tm), _tile(K, tko), _tile(N, tc)
    return _matmul(
        name, NT, (K // tko, M // tm, N // tc),
        dy, pl.BlockSpec((tm, tc), lambda j, i, k: (i, k)),
        w, pl.BlockSpec((tko, tc), lambda j, i, k: (j, k)),
        jax.ShapeDtypeStruct((M, K), F32), pl.BlockSpec((tm, tko), lambda j, i, k: (i, j)), (tm, tko), deps=deps)


def _mm_dx_gcols(name, dy, wg, split_in=False, tm=1024, tko=1024, deps=()):
    G, K, Nl = wg.shape
    M = dy.shape[-2]
    tm, tko = _tile(M, tm), _tile(K, tko)
    hg = G // 2
    if split_in:
        dy_spec = pl.BlockSpec((None, tm, Nl), lambda j, i, g: (g // hg, i, g % hg))
    else:
        dy_spec = pl.BlockSpec((tm, Nl), lambda j, i, g: (i, g))
    return _matmul(
        name, NT, (K // tko, M // tm, G),
        dy, dy_spec,
        wg, pl.BlockSpec((None, tko, Nl), lambda j, i, g: (g, j, 0)),
        jax.ShapeDtypeStruct((M, K), F32), pl.BlockSpec((tm, tko), lambda j, i, g: (i, j)), (tm, tko), deps=deps)


def _lhs_of_dw(a, a_t, ts, tko, index):
    if a_t:
        return NN, pl.BlockSpec((tko, ts), lambda *ids: index(*ids))
    return TN, pl.BlockSpec((ts, tko), lambda *ids: index(*ids)[::-1])


def _mm_dw(name, a, dy, tko=512, tn=1024, ts=2048, deps=(), a_t=False):
    K1, S = a.shape if a_t else a.shape[::-1]
    N = dy.shape[1]
    tko, tn, ts = _tile(K1, tko), _tile(N, tn), _tile(S, ts)
    dims, a_spec = _lhs_of_dw(a, a_t, ts, tko, lambda i, j, k: (j, k))
    return _matmul(
        name, dims, (N // tn, K1 // tko, S // ts),
        a, a_spec,
        dy, pl.BlockSpec((ts, tn), lambda i, j, k: (k, i)),
        jax.ShapeDtypeStruct((K1, N), F32), pl.BlockSpec((tko, tn), lambda i, j, k: (j, i)), (tko, tn), deps=deps)


def _mm_dw_gcols(name, a, dy, G, split_in=False, tko=1024, ts=2048, deps=(), a_t=False):
    K1, S = a.shape if a_t else a.shape[::-1]
    Nl = (dy.shape[-1] * (2 if split_in else 1)) // G
    tko, ts = _tile(K1, tko), _tile(S, ts)
    hg = G // 2
    dims, a_spec = _lhs_of_dw(a, a_t, ts, tko, lambda g, j, k: (j, k))
    if split_in:
        dy_spec = pl.BlockSpec((None, ts, Nl), lambda g, j, k: (g // hg, k, g % hg))
    else:
        dy_spec = pl.BlockSpec((ts, Nl), lambda g, j, k: (k, g))
    return _matmul(
        name, dims, (G, K1 // tko, S // ts),
        a, a_spec,
        dy, dy_spec,
        jax.ShapeDtypeStruct((G, K1, Nl), F32), pl.BlockSpec((None, tko, Nl), lambda g, j, k: (g, j, 0)),
        (tko, Nl), deps=deps)


def _rms_fwd(name, x, g, tr=256, deps=(), with_t=False):
    S, D = x.shape
    tr = _tile(S, tr)

    def body(x_ref, g_ref, *rest):
        xf = x_ref[...]
        r = lax.rsqrt(jnp.mean(xf * xf, axis=-1, keepdims=True) + EPS)
        y = xf * r * g_ref[...]
        if with_t:
            rest[-2][...] = y.astype(BF16)
            rest[-1][...] = y.T.astype(BF16)
        else:
            rest[-1][...] = y.astype(BF16)

    row = pl.BlockSpec((tr, D), lambda i: (i, 0))
    out_specs, out_shape = row, jax.ShapeDtypeStruct((S, D), BF16)
    if with_t:
        out_specs = [row, pl.BlockSpec((D, tr), lambda i: (0, i))]
        out_shape = [out_shape, jax.ShapeDtypeStruct((D, S), BF16)]
    return pl.pallas_call(
        body, name=name, grid=(S // tr,),
        in_specs=[row, pl.BlockSpec((1, D), lambda i: (0, 0))] + [_DEP] * len(deps),
        out_specs=out_specs, out_shape=out_shape, compiler_params=_cp("parallel"))(x, g.reshape(1, D), *deps)


def _rms_bwd(name, dh, x, g, dres=None, tr=256):
    S, D = x.shape
    tr = _tile(S, tr)
    has_res = dres is not None

    def body(*refs):
        dh_ref, x_ref, g_ref = refs[:3]
        dres_ref = refs[3] if has_res else None
        dx_ref, dg_ref = refs[-2], refs[-1]
        xf = x_ref[...]
        r = lax.rsqrt(jnp.mean(xf * xf, axis=-1, keepdims=True) + EPS)
        xh = xf * r
        dhv = dh_ref[...]
        dxh = dhv * g_ref[...]
        dx = r * (dxh - xh * jnp.mean(dxh * xh, axis=-1, keepdims=True))
        if has_res:
            dx = dx + dres_ref[...]
        dx_ref[...] = dx
        part = jnp.sum(dhv * xh, axis=0, keepdims=True)
        i = pl.program_id(0)

        @pl.when(i == 0)
        def _():
            dg_ref[...] = part

        @pl.when(i > 0)
        def _():
            dg_ref[...] += part

    row = pl.BlockSpec((tr, D), lambda i: (i, 0))
    vec = pl.BlockSpec((1, D), lambda i: (0, 0))
    ins = [dh, x, g.reshape(1, D)] + ([dres] if has_res else [])
    return pl.pallas_call(
        body, name=name, grid=(S // tr,),
        in_specs=[row, row, vec] + ([row] if has_res else []),
        out_specs=[row, vec],
        out_shape=[jax.ShapeDtypeStruct((S, D), F32), jax.ShapeDtypeStruct((1, D), F32)],
        compiler_params=_cp("arbitrary"))(*ins)


def _final(name, x, tgt, g, tr=256):
    S, D = x.shape
    tr = _tile(S, tr)

    def body(x_ref, t_ref, g_ref, dx_ref, dg_ref, loss_ref):
        xf = x_ref[...]
        gv = g_ref[...]
        r = lax.rsqrt(jnp.mean(xf * xf, axis=-1, keepdims=True) + EPS)
        xh = xf * r
        err = xh * gv - t_ref[...]
        sq = jnp.sum(jnp.sum(err * err, axis=1, keepdims=True), axis=0, keepdims=True)
        dy = err * (1.0 / D)
        dxh = dy * gv
        dx_ref[...] = r * (dxh - xh * jnp.mean(dxh * xh, axis=-1, keepdims=True))
        part = jnp.sum(dy * xh, axis=0, keepdims=True)
        lpart = jnp.broadcast_to(sq, (8, 128))
        i = pl.program_id(0)

        @pl.when(i == 0)
        def _():
            dg_ref[...] = part
            loss_ref[...] = lpart

        @pl.when(i > 0)
        def _():
            dg_ref[...] += part
            loss_ref[...] += lpart

    row = pl.BlockSpec((tr, D), lambda i: (i, 0))
    vec = pl.BlockSpec((1, D), lambda i: (0, 0))
    return pl.pallas_call(
        body, name=name, grid=(S // tr,), in_specs=[row, row, vec],
        out_specs=[row, vec, pl.BlockSpec((8, 128), lambda i: (0, 0))],
        out_shape=[jax.ShapeDtypeStruct((S, D), F32), jax.ShapeDtypeStruct((1, D), F32),
                   jax.ShapeDtypeStruct((8, 128), F32)],
        compiler_params=_cp("arbitrary"))(x, tgt, g.reshape(1, D))


def _band_specs(nb, col_of):
    prev = pl.BlockSpec((QBLK, GW), lambda r, b: (jnp.maximum(b - 1, 0), col_of(r)))
    cur = pl.BlockSpec((QBLK, GW), lambda r, b: (b, col_of(r)))
    nxt = pl.BlockSpec((QBLK, GW), lambda r, b: (jnp.minimum(b + 1, nb - 1), col_of(r)))
    return [prev, cur, nxt]


def _cat3(refs, sl):
    return jnp.concatenate([ref[:, sl] for ref in refs], axis=0)


def _attn_fwd(name, proj, g):
    window, dil = A_PATTERNS[g]
    n_side = (window // 2) // dil
    S, C = proj.shape
    L = S // dil
    nb = L // QBLK
    cb = C // GW
    pv = proj.reshape(L, dil * C)
    ng = len(A_PATTERNS)

    def body(q_ref, kp, kc, kn, vp, vc, vn, o_ref, lse_ref):
        b = pl.program_id(1)
        jq = b * QBLK + lax.broadcasted_iota(jnp.int32, (QBLK, 3 * QBLK), 0)
        jk = (b - 1) * QBLK + lax.broadcasted_iota(jnp.int32, (QBLK, 3 * QBLK), 1)
        rel = jnp.abs(jk - jq)
        mask = (rel <= n_side) & (jk >= 0) & (jk < L)
        dist = rel.astype(F32) * float(dil)
        for hh in range(HPG):
            sl = slice(hh * HEAD, (hh + 1) * HEAD)
            k = _cat3((kp, kc, kn), sl)
            v = _cat3((vp, vc, vn), sl)
            s = _dot(q_ref[:, sl], k, NT) * SCALE - float(SLOPES[g * HPG + hh]) * dist
            s = jnp.where(mask, s, NEG)
            m = jnp.max(s, axis=1, keepdims=True)
            p = jnp.exp(s - m)
            l = jnp.sum(p, axis=1, keepdims=True)
            o_ref[:, sl] = _dot(p, v) / l
            lse_ref[:, sl] = jnp.broadcast_to(m + jnp.log(l), (QBLK, HEAD))

    q_spec = pl.BlockSpec((QBLK, GW), lambda r, b: (b, r * cb + g))
    k_specs = _band_specs(nb, lambda r: r * cb + ng + g)
    v_specs = _band_specs(nb, lambda r: r * cb + 2 * ng + g)
    o_spec = pl.BlockSpec((QBLK, GW), lambda r, b: (b, r))
    shape = jax.ShapeDtypeStruct((L, dil * GW), F32)
    o, lse = pl.pallas_call(
        body, name=name, grid=(dil, nb), in_specs=[q_spec] + k_specs + v_specs,
        out_specs=[o_spec, o_spec], out_shape=[shape, shape],
        compiler_params=_cp("parallel", "parallel"))(pv, pv, pv, pv, pv, pv, pv)
    return o.reshape(S, GW), lse.reshape(S, GW)


def _attn_merge(name, outs, lses, tr=256):
    S = outs[0].shape[0]
    tr = _tile(S, tr)
    ng = len(outs)

    def body(*refs):
        o_refs, l_refs = refs[:ng], refs[ng:2 * ng]
        tok_ref, lse_ref = refs[-2], refs[-1]
        ls = [r[...] for r in l_refs]
        m = functools.reduce(jnp.maximum, ls)
        es = [jnp.exp(l - m) for l in ls]
        tot = functools.reduce(lambda a, b: a + b, es)
        acc = None
        for e, o_ref in zip(es, o_refs):
            term = (e / tot) * o_ref[...]
            acc = term if acc is None else acc + term
        tok_ref[...] = acc
        lse_ref[...] = m + jnp.log(tot)

    row = pl.BlockSpec((tr, GW), lambda i: (i, 0))
    shape = jax.ShapeDtypeStruct((S, GW), F32)
    return pl.pallas_call(
        body, name=name, grid=(S // tr,), in_specs=[row] * (2 * ng), out_specs=[row, row],
        out_shape=[shape, shape], compiler_params=_cp("parallel"))(*outs, *lses)


def _attn_bwd(name, proj, g, dtok_src, dtok_blk, tok, lse):
    window, dil = A_PATTERNS[g]
    n_side = (window // 2) // dil
    S, C = proj.shape
    L = S // dil
    nb = L // QBLK
    cb = C // GW
    ng = len(A_PATTERNS)
    pv = proj.reshape(L, dil * C)
    dcb = dtok_src.shape[1] // GW
    dv_ = dtok_src.reshape(L, dil * dtok_src.shape[1])
    ov = tok.reshape(L, dil * GW)
    lv = lse.reshape(L, dil * GW)

    def body(qp, qc, qn, kp, kc, kn, vp, vc, vn, dop, doc, don, op, oc, on, lp, lc, ln,
             dq_ref, dk_ref, dv_ref):
        b = pl.program_id(1)
        jq = b * QBLK + lax.broadcasted_iota(jnp.int32, (QBLK, 3 * QBLK), 0)
        jk = (b - 1) * QBLK + lax.broadcasted_iota(jnp.int32, (QBLK, 3 * QBLK), 1)
        rel = jnp.abs(jk - jq)
        mask = (rel <= n_side) & (jk >= 0) & (jk < L)
        dist = rel.astype(F32) * float(dil)
        jq3 = (b - 1) * QBLK + lax.broadcasted_iota(jnp.int32, (3 * QBLK, QBLK), 0)
        jk1 = b * QBLK + lax.broadcasted_iota(jnp.int32, (3 * QBLK, QBLK), 1)
        rel3 = jnp.abs(jk1 - jq3)
        mask3 = (rel3 <= n_side) & (jq3 >= 0) & (jq3 < L)
        dist3 = rel3.astype(F32) * float(dil)
        for hh in range(HPG):
            sl = slice(hh * HEAD, (hh + 1) * HEAD)
            one = slice(hh * HEAD, hh * HEAD + 1)
            slope = float(SLOPES[g * HPG + hh])
            q = qc[:, sl]
            do = doc[:, sl]
            k3 = _cat3((kp, kc, kn), sl)
            v3 = _cat3((vp, vc, vn), sl)
            delta = jnp.sum(do * oc[:, sl], axis=1, keepdims=True)
            s = _dot(q, k3, NT) * SCALE - slope * dist
            p = jnp.where(mask, jnp.exp(s - lc[:, one]), 0.0)
            ds = p * (_dot(do, v3, NT) - delta)
            dq_ref[:, sl] = _dot(ds, k3) * SCALE

            q3 = _cat3((qp, qc, qn), sl)
            do3 = _cat3((dop, doc, don), sl)
            o3 = _cat3((op, oc, on), sl)
            lse3 = _cat3((lp, lc, ln), sl)[:, :1]
            delta3 = jnp.sum(do3 * o3, axis=1, keepdims=True)
            k = kc[:, sl]
            v = vc[:, sl]
            s3 = _dot(q3, k, NT) * SCALE - slope * dist3
            p3 = jnp.where(mask3, jnp.exp(s3 - lse3), 0.0)
            ds3 = p3 * (_dot(do3, v, NT) - delta3)
            dv_ref[:, sl] = _dot(p3, do3, TN)
            dk_ref[:, sl] = _dot(ds3, q3, TN) * SCALE

    specs = (_band_specs(nb, lambda r: r * cb + g) + _band_specs(nb, lambda r: r * cb + ng + g)
             + _band_specs(nb, lambda r: r * cb + 2 * ng + g)
             + _band_specs(nb, lambda r: r * dcb + dtok_blk)
             + _band_specs(nb, lambda r: r) + _band_specs(nb, lambda r: r))
    o_spec = pl.BlockSpec((QBLK, GW), lambda r, b: (b, r))
    shape = jax.ShapeDtypeStruct((L, dil * GW), F32)
    outs = pl.pallas_call(
        body, name=name, grid=(dil, nb), in_specs=specs, out_specs=[o_spec] * 3, out_shape=[shape] * 3,
        compiler_params=_cp("parallel", "parallel"))(*([pv] * 9 + [dv_] * 3 + [ov] * 3 + [lv] * 3))
    return [o.reshape(S, GW) for o in outs]


def _mem_fwd(name, proj, q_blk, kv, tq=256):
    S = proj.shape[0]
    M = kv.shape[0]
    tq = _tile(S, tq)

    def body(q_ref, kv_ref, o_ref):
        for hh in range(HPG):
            sl = slice(hh * HEAD, (hh + 1) * HEAD)
            k = kv_ref[:, sl]
            v = kv_ref[:, GW + hh * HEAD:GW + (hh + 1) * HEAD]
            s = _dot(q_ref[:, sl], k, NT) * SCALE
            m = jnp.max(s, axis=1, keepdims=True)
            p = jnp.exp(s - m)
            p = p / jnp.sum(p, axis=1, keepdims=True)
            o_ref[:, sl] = _dot(p, v)

    return pl.pallas_call(
        body, name=name, grid=(S // tq,),
        in_specs=[pl.BlockSpec((tq, GW), lambda i: (i, q_blk)), pl.BlockSpec((M, 2 * GW), lambda i: (0, 0))],
        out_specs=pl.BlockSpec((tq, GW), lambda i: (i, 0)),
        out_shape=jax.ShapeDtypeStruct((S, GW), F32), compiler_params=_cp("parallel"))(proj, kv)


def _mem_bwd(name, proj, q_blk, kv, dcat, do_blk, tq=256):
    S = proj.shape[0]
    M = kv.shape[0]
    tq = _tile(S, tq)

    def body(q_ref, kv_ref, do_ref, dq_ref, dkv_ref):
        i = pl.program_id(0)
        for hh in range(HPG):
            sl = slice(hh * HEAD, (hh + 1) * HEAD)
            vsl = slice(GW + hh * HEAD, GW + (hh + 1) * HEAD)
            q = q_ref[:, sl]
            do = do_ref[:, sl]
            k = kv_ref[:, sl]
            v = kv_ref[:, vsl]
            s = _dot(q, k, NT) * SCALE
            m = jnp.max(s, axis=1, keepdims=True)
            p = jnp.exp(s - m)
            p = p / jnp.sum(p, axis=1, keepdims=True)
            dp = _dot(do, v, NT)
            ds = p * (dp - jnp.sum(dp * p, axis=1, keepdims=True))
            dq_ref[:, sl] = _dot(ds, k) * SCALE
            dk = _dot(ds, q, TN) * SCALE
            dvv = _dot(p, do, TN)

            @pl.when(i == 0)
            def _():
                dkv_ref[:, sl] = dk
                dkv_ref[:, vsl] = dvv

            @pl.when(i > 0)
            def _():
                dkv_ref[:, sl] += dk
                dkv_ref[:, vsl] += dvv

    return pl.pallas_call(
        body, name=name, grid=(S // tq,),
        in_specs=[pl.BlockSpec((tq, GW), lambda i: (i, q_blk)), pl.BlockSpec((M, 2 * GW), lambda i: (0, 0)),
                  pl.BlockSpec((tq, GW), lambda i: (i, do_blk))],
        out_specs=[pl.BlockSpec((tq, GW), lambda i: (i, 0)), pl.BlockSpec((M, 2 * GW), lambda i: (0, 0))],
        out_shape=[jax.ShapeDtypeStruct((S, GW), F32), jax.ShapeDtypeStruct((M, 2 * GW), F32)],
        compiler_params=_cp("arbitrary"))(proj, kv, dcat)


_RSQRT2 = float(1.0 / np.sqrt(2.0))
_RSQRT2PI = float(1.0 / np.sqrt(2.0 * np.pi))


def _gelu(x):
    return 0.5 * x * (1.0 + lax.erf(x * _RSQRT2))


def _gelu_grad(x):
    return 0.5 * (1.0 + lax.erf(x * _RSQRT2)) + x * jnp.exp(-0.5 * x * x) * _RSQRT2PI


def _sgu_fwd(name, proj, gv, w_s, bias_t):
    S = proj.shape[0]
    nch = S // HEAD

    def body(u_ref, v_ref, gv_ref, ws_ref, b_ref, o_ref):
        v = _gelu(v_ref[...])
        r = lax.rsqrt(jnp.mean(v * v, axis=-1, keepdims=True) + EPS)
        vn = v * r * gv_ref[...]
        for gg in range(B_GROUPS):
            sl = slice(gg * HEAD, (gg + 1) * HEAD)
            mixed = _dot(ws_ref[gg], vn[:, sl]) + b_ref[:, gg:gg + 1]
            o_ref[:, sl] = _gelu(u_ref[:, sl]) * mixed

    return pl.pallas_call(
        body, name=name, grid=(nch,),
        in_specs=[pl.BlockSpec((HEAD, B_W), lambda c: (c, 0)), pl.BlockSpec((HEAD, B_W), lambda c: (c, 1)),
                  pl.BlockSpec((1, B_W), lambda c: (0, 0)),
                  pl.BlockSpec((B_GROUPS, HEAD, HEAD), lambda c: (0, 0, 0)),
                  pl.BlockSpec((HEAD, B_GROUPS), lambda c: (0, 0))],
        out_specs=pl.BlockSpec((HEAD, B_W), lambda c: (c, 0)),
        out_shape=jax.ShapeDtypeStruct((S, B_W), F32),
        compiler_params=_cp("parallel"))(proj, proj, gv.reshape(1, B_W), w_s, bias_t)


def _sgu_bwd(name, proj, gv, w_s, bias_t, dcat):
    S = proj.shape[0]
    nch = S // HEAD

    def body(u_ref, v_ref, gv_ref, ws_ref, b_ref, dt_ref, du_ref, dvp_ref, dgv_ref, dws_ref, db_ref, dvn_ref):
        c = pl.program_id(0)
        vpre = v_ref[...]
        v = _gelu(vpre)
        r = lax.rsqrt(jnp.mean(v * v, axis=-1, keepdims=True) + EPS)
        vh = v * r
        gvv = gv_ref[...]
        vn = vh * gvv
        for gg in range(B_GROUPS):
            sl = slice(gg * HEAD, (gg + 1) * HEAD)
            upre = u_ref[:, sl]
            dt = dt_ref[:, sl]
            vng = vn[:, sl]
            mixed = _dot(ws_ref[gg], vng) + b_ref[:, gg:gg + 1]
            du_ref[:, sl] = dt * mixed * _gelu_grad(upre)
            dmix = dt * _gelu(upre)
            dvn_ref[:, sl] = _dot(ws_ref[gg], dmix, TN)
            dws = _dot(dmix, vng, NT)
            dbs = jnp.sum(dmix, axis=1, keepdims=True)

            @pl.when(c == 0)
            def _():
                dws_ref[gg] = dws
                db_ref[:, gg:gg + 1] = dbs

            @pl.when(c > 0)
            def _():
                dws_ref[gg] += dws
                db_ref[:, gg:gg + 1] += dbs

        dvn = dvn_ref[...]
        dgp = jnp.sum(dvn * vh, axis=0, keepdims=True)
        dvh = dvn * gvv
        dv = r * (dvh - vh * jnp.mean(dvh * vh, axis=-1, keepdims=True))
        dvp_ref[...] = dv * _gelu_grad(vpre)

        @pl.when(c == 0)
        def _():
            dgv_ref[...] = dgp

        @pl.when(c > 0)
        def _():
            dgv_ref[...] += dgp

    blk = lambda j: pl.BlockSpec((HEAD, B_W), lambda c: (c, j))
    vec = pl.BlockSpec((1, B_W), lambda c: (0, 0))
    ws_spec = pl.BlockSpec((B_GROUPS, HEAD, HEAD), lambda c: (0, 0, 0))
    b_spec = pl.BlockSpec((HEAD, B_GROUPS), lambda c: (0, 0))
    du, dvp, dgv, dws, db = pl.pallas_call(
        body, name=name, grid=(nch,),
        in_specs=[blk(0), blk(1), vec, ws_spec, b_spec, blk(0)],
        out_specs=[blk(0), blk(0), vec, ws_spec, b_spec],
        out_shape=[jax.ShapeDtypeStruct((S, B_W), F32), jax.ShapeDtypeStruct((S, B_W), F32),
                   jax.ShapeDtypeStruct((1, B_W), F32), jax.ShapeDtypeStruct((B_GROUPS, HEAD, HEAD), F32),
                   jax.ShapeDtypeStruct((HEAD, B_GROUPS), F32)],
        scratch_shapes=[pltpu.VMEM((HEAD, B_W), F32)],
        compiler_params=_cp("arbitrary"))(proj, proj, gv.reshape(1, B_W), w_s, bias_t, dcat)
    return du, dvp, dgv, dws, db


def _shift_down(a, row):
    return jnp.where(row == 0, 0.0, pltpu.roll(a, 1, 0))


def _shift_up(a, row):
    n = a.shape[0]
    return jnp.where(row == n - 1, 0.0, pltpu.roll(a, n - 1, 0))


def _conv(a, w, b, row):
    return _shift_down(a, row) * w[0:1] + a * w[1:2] + _shift_up(a, row) * w[2:3] + b


def _conv_fwd(name, a3, cw, cb, tc=256):
    _, S, FF = a3.shape
    tc = _tile(FF, tc)

    def body(a_ref, w_ref, b_ref, o_ref):
        row = lax.broadcasted_iota(jnp.int32, (S, tc), 0)
        cg = _conv(a_ref[0], w_ref[0], b_ref[0], row)
        cv = _conv(a_ref[1], w_ref[1], b_ref[1], row)
        o_ref[...] = (_gelu(cg) * cv).astype(o_ref.dtype)

    return pl.pallas_call(
        body, name=name, grid=(FF // tc,),
        in_specs=[pl.BlockSpec((2, S, tc), lambda j: (0, 0, j)), pl.BlockSpec((2, 3, tc), lambda j: (0, 0, j)),
                  pl.BlockSpec((2, 1, tc), lambda j: (0, 0, j))],
        out_specs=pl.BlockSpec((S, tc), lambda j: (0, j)),
        out_shape=jax.ShapeDtypeStruct((S, FF), BF16), compiler_params=_cp("parallel"))(a3, cw, cb)


def _conv_bwd(name, a3, cw, cb, dact, tc=128):
    _, S, FF = a3.shape
    tc = _tile(FF, tc)

    def body(a_ref, w_ref, b_ref, d_ref, da_ref, dw_ref, db_ref):
        row = lax.broadcasted_iota(jnp.int32, (S, tc), 0)
        ag, av = a_ref[0], a_ref[1]
        wg, wv = w_ref[0], w_ref[1]
        cg = _conv(ag, wg, b_ref[0], row)
        cv = _conv(av, wv, b_ref[1], row)
        d = d_ref[...]
        dcs = (d * cv * _gelu_grad(cg), d * _gelu(cg))
        for h, (dc, a, w) in enumerate(zip(dcs, (ag, av), (wg, wv))):
            da_ref[h] = _shift_up(dc, row) * w[0:1] + dc * w[1:2] + _shift_down(dc, row) * w[2:3]
            dw_ref[h, 0:1, :] = jnp.sum(dc * _shift_down(a, row), axis=0, keepdims=True)
            dw_ref[h, 1:2, :] = jnp.sum(dc * a, axis=0, keepdims=True)
            dw_ref[h, 2:3, :] = jnp.sum(dc * _shift_up(a, row), axis=0, keepdims=True)
            db_ref[h] = jnp.sum(dc, axis=0, keepdims=True)

    a_spec = pl.BlockSpec((2, S, tc), lambda j: (0, 0, j))
    w_spec = pl.BlockSpec((2, 3, tc), lambda j: (0, 0, j))
    b_spec = pl.BlockSpec((2, 1, tc), lambda j: (0, 0, j))
    return pl.pallas_call(
        body, name=name, grid=(FF // tc,),
        in_specs=[a_spec, w_spec, b_spec, pl.BlockSpec((S, tc), lambda j: (0, j))],
        out_specs=[a_spec, w_spec, b_spec],
        out_shape=[jax.ShapeDtypeStruct((2, S, FF), F32), jax.ShapeDtypeStruct((2, 3, FF), F32),
                   jax.ShapeDtypeStruct((2, 1, FF), F32)],
        compiler_params=_cp("parallel"))(a3, cw, cb, dact)


_HBM = pl.BlockSpec(memory_space=pltpu.HBM)


def _position():
    return lax.axis_index("x"), lax.axis_index("y"), lax.axis_index("c")


_SEM =pl.BlockSpec(memory_space=pltpu.SEMAPHORE)
_EFFECT = pltpu.SideEffectType.DATAFLOW_SIDE_EFFECTING
_FLIPS = ((1, 0), (0, 1), (1, 1))


def _split_start(name, bufs, ncopy, plan, after=()):
    n = len(bufs)
    after = list(after)

    def body(*refs):
        ins = refs[:n]
        send_sems, recv_sems, token = refs[n + len(after)], refs[n + len(after) + 1], refs[-1]
        for i, (src, dst, to) in enumerate(plan(ins)):
            pltpu.make_async_remote_copy(src_ref=src, dst_ref=dst, send_sem=send_sems.at[i], recv_sem=recv_sems.at[i],
                                         device_id=to, device_id_type=MESH).start()
        token[...] = jnp.zeros_like(token)

    outs = pl.pallas_call(
        body, name=name,
        out_shape=(pltpu.SemaphoreType.DMA((ncopy,)), pltpu.SemaphoreType.DMA((ncopy,)),
                   *[pltpu.HBM(b.shape, b.dtype) for b in bufs], jax.ShapeDtypeStruct((8, 128), F32)),
        in_specs=[_HBM] * n + [pl.BlockSpec(memory_space=pl.ANY)] * len(after),
        out_specs=(_SEM, _SEM, *([_HBM] * n), pl.BlockSpec(memory_space=pltpu.VMEM)),
        input_output_aliases={i: 2 + i for i in range(n)},
        compiler_params=pltpu.CompilerParams(has_side_effects=_EFFECT),
    )(*[pltpu.with_memory_space_constraint(b, pltpu.HBM) for b in bufs], *after)
    return outs[0], outs[1], list(outs[2:2 + n]), outs[-1]


def _split_wait(name, bufs, send_sems, recv_sems, plan, after):
    n = len(bufs)
    after = list(after)

    def body(*refs):
        ins = refs[:n]
        ssem, rsem = refs[n], refs[n + 1]
        for i, (src, dst, to) in enumerate(plan(ins)):
            cp = pltpu.make_async_remote_copy(src_ref=src, dst_ref=dst, send_sem=ssem.at[i], recv_sem=rsem.at[i],
                                              device_id=to, device_id_type=MESH)
            cp.wait_send()
            cp.wait_recv()

    outs = pl.pallas_call(
        body, name=name, out_shape=tuple(pltpu.HBM(b.shape, b.dtype) for b in bufs),
        in_specs=[_HBM] * n + [_SEM, _SEM] + [pl.BlockSpec(memory_space=pl.ANY)] * len(after),
        out_specs=tuple([_HBM] * n), input_output_aliases={i: i for i in range(n)},
        compiler_params=pltpu.CompilerParams(has_side_effects=_EFFECT),
    )(*bufs, send_sems, recv_sems, *after)
    return list(outs)


def _gather_plan(refs):
    px, py, pc = _position()
    me = 4 * px + 2 * py + pc
    targets = [(px, py, 1 - pc), (1 - px, py, pc), (px, 1 - py, pc), (1 - px, 1 - py, pc)]
    return [(r.at[me], r.at[me], to) for r in refs for to in targets]


def _forward_plan(refs):
    px, py, pc = _position()
    out = []
    for r in refs:
        for fx, fy in _FLIPS:
            slot = 4 * (1 - px if fx else px) + 2 * (1 - py if fy else py) + pc
            out.append((r.at[slot], r.at[slot], (px, py, 1 - pc)))
    return out


def _pair_plan(n):
    def plan(refs):
        px, py, pc = _position()
        return [(refs[w].at[2 * k + (1 - pc)], refs[n + w].at[k], (px, py, 1 - pc)) for w in range(n) for k in range(4)]
    return plan


def _chip_plan(n):
    def plan(refs):
        px, py, pc = _position()
        out = []
        for w in range(n):
            for j, (fx, fy) in enumerate(_FLIPS):
                qx = 1 - px if fx else px
                qy = 1 - py if fy else py
                out.append((refs[w].at[2 * qx + qy], refs[n + w].at[j], (qx, qy, pc)))
        return out
    return plan


def _broadcast_plan(refs):
    px, py, pc = _position()
    me = 4 * px + 2 * py + pc
    flips = [(fx, fy, fc) for fx in (0, 1) for fy in (0, 1) for fc in (0, 1)][1:]
    targets = [(1 - px if fx else px, 1 - py if fy else py, 1 - pc if fc else pc) for fx, fy, fc in flips]
    return [(r.at[me], r.at[me], to) for r in refs for to in targets]


def _cast_place(name, dev, w, layer, dtype=BF16):
    nl, R, C = w.shape
    tr = _row_tile(R, C)

    def body(dev_ref, w_ref, o_ref):
        o_ref[...] = w_ref[...].astype(o_ref.dtype)

    return pl.pallas_call(
        body, name=name,
        grid_spec=pltpu.PrefetchScalarGridSpec(
            num_scalar_prefetch=1, grid=(R // tr,),
            in_specs=[pl.BlockSpec((None, tr, C), lambda i, d: (layer, i, 0))],
            out_specs=pl.BlockSpec((None, tr, C), lambda i, d: (d[0], i, 0))),
        out_shape=jax.ShapeDtypeStruct((N_DEV, R, C), dtype), compiler_params=_cp("parallel"))(dev, w)


def _pair_sum(name, core, dw, recv):
    _, R, C = dw.shape
    tr = _row_tile(R, C)
    dw4 = dw.reshape(4, 2, R, C)

    def body(core_ref, a_ref, b_ref, o_ref):
        o_ref[...] = (a_ref[...] + b_ref[...]).astype(o_ref.dtype)

    return pl.pallas_call(
        body, name=name,
        grid_spec=pltpu.PrefetchScalarGridSpec(
            num_scalar_prefetch=1, grid=(4, R // tr),
            in_specs=[pl.BlockSpec((None, None, tr, C), lambda k, i, c_ref: (k, c_ref[0], i, 0)),
                      pl.BlockSpec((None, tr, C), lambda k, i, c_ref: (k, i, 0))],
            out_specs=pl.BlockSpec((None, tr, C), lambda k, i, c_ref: (k, i, 0))),
        out_shape=jax.ShapeDtypeStruct((4, R, C), BF16),
        compiler_params=_cp("parallel", "parallel"))(core, dw4, recv)


def _adamw_math(w, g, m, v):
    m = ADAM_B1 * m + (1.0 - ADAM_B1) * g
    v = ADAM_B2 * v + (1.0 - ADAM_B2) * (g * g)
    m_hat = m / (1.0 - ADAM_B1 ** ADAM_STEP)
    v_hat = v / (1.0 - ADAM_B2 ** ADAM_STEP)
    delta = -ADAM_LR * (m_hat / (jnp.sqrt(v_hat) + ADAM_EPS) + ADAM_WD * w)
    return delta, m, v


def _adamw_shard(name, chip, layer, w, m, v, p, recv, prev, deps=()):
    nl, R, C = w.shape
    tr = _row_tile(R, C)
    n_prev = 0 if prev is None else 4

    def body(chip_ref, w_ref, m_ref, v_ref, p_ref, r_ref, *rest):
        g_ref, d_ref, nm_ref, nv_ref = rest[-4:]
        g = p_ref[...].astype(F32)
        for j in range(3):
            g = g + r_ref[j].astype(F32)
        delta, nm, nv = _adamw_math(w_ref[...], g, m_ref[...], v_ref[...])
        g_ref[...] = g
        d_ref[...] = delta
        nm_ref[...] = nm
        nv_ref[...] = nv

    lay = pl.BlockSpec((None, tr, C), lambda i, c_ref: (layer, i, 0))
    in_specs = [lay, lay, lay,
                pl.BlockSpec((None, tr, C), lambda i, c_ref: (c_ref[0], i, 0)),
                pl.BlockSpec((3, tr, C), lambda i, c_ref: (0, i, 0))]
    in_specs += [pl.BlockSpec(memory_space=pl.ANY)] * n_prev + [_DEP] * len(deps)
    shape = jax.ShapeDtypeStruct((nl, R, C), F32)
    ins = [chip, w, m, v, p, recv] + ([] if prev is None else list(prev)) + list(deps)
    return pl.pallas_call(
        body, name=name,
        grid_spec=pltpu.PrefetchScalarGridSpec(
            num_scalar_prefetch=1, grid=(R // tr,), in_specs=in_specs, out_specs=[lay] * 4),
        out_shape=[shape] * 4,
        input_output_aliases={6 + j: j for j in range(n_prev)},
        compiler_params=_cp("parallel"))(*ins)


def _sum_slots(name, parts, tr=512):
    n, R, C = parts.shape
    tr = _tile(R, tr)

    def body(p_ref, o_ref):
        acc = p_ref[0]
        for j in range(1, n):
            acc = acc + p_ref[j]
        o_ref[...] = acc

    return pl.pallas_call(
        body, name=name, grid=(R // tr,),
        in_specs=[pl.BlockSpec((n, tr, C), lambda i: (0, i, 0))],
        out_specs=pl.BlockSpec((tr, C), lambda i: (i, 0)),
        out_shape=jax.ShapeDtypeStruct((R, C), F32), compiler_params=_cp("parallel"))(parts)


def _adamw_flat(name, w, g, m, v, tr=512):
    R, C = w.shape
    tr = _tile(R, tr)

    def body(w_ref, g_ref, m_ref, v_ref, d_ref, nm_ref, nv_ref):
        delta, nm, nv = _adamw_math(w_ref[...], g_ref[...], m_ref[...], v_ref[...])
        d_ref[...] = delta
        nm_ref[...] = nm
        nv_ref[...] = nv

    row = pl.BlockSpec((tr, C), lambda i: (i, 0))
    shape = jax.ShapeDtypeStruct((R, C), F32)
    return pl.pallas_call(
        body, name=name, grid=(R // tr,), in_specs=[row] * 4, out_specs=[row] * 3, out_shape=[shape] * 3,
        compiler_params=_cp("parallel"))(w, g, m, v)


_PACK_ROWS = 512


def _pack(arrs):
    flat = jnp.concatenate([a.reshape(-1) for a in arrs])
    unit = _PACK_ROWS * 128
    pad = (-flat.shape[0]) % unit
    return jnp.pad(flat, (0, pad)).reshape(-1, 128)


def _unpack(packed, shapes):
    flat = packed.reshape(-1)
    outs, off = [], 0
    for s in shapes:
        n = int(np.prod(s))
        outs.append(flat[off:off + n].reshape(s))
        off += n
    return outs


def kernel(x, mem, mix_norm_g, ffn_norm_g, mem_norm_g, w_mem_kv, a_w_in, a_w_out, b_w_in, b_v_norm_g, b_w_s, b_s_bias, b_w_out, ffn_w_up, ffn_conv_w, ffn_conv_b, ffn_w_down, final_norm_g, loss_target, m_mix_norm_g, m_ffn_norm_g, m_mem_norm_g, m_w_mem_kv, m_a_w_in, m_a_w_out, m_b_w_in, m_b_v_norm_g, m_b_w_s, m_b_s_bias, m_b_w_out, m_ffn_w_up, m_ffn_conv_w, m_ffn_conv_b, m_ffn_w_down, m_final_norm_g, v_mix_norm_g, v_ffn_norm_g, v_mem_norm_g, v_w_mem_kv, v_a_w_in, v_a_w_out, v_b_w_in, v_b_v_norm_g, v_b_w_s, v_b_s_bias, v_b_w_out, v_ffn_w_up, v_ffn_conv_w, v_ffn_conv_b, v_ffn_w_down, v_final_norm_g):
    px, py, pc = _position()
    dev = 4 * px + 2 * py + pc
    core = jnp.reshape(pc, (1,)).astype(jnp.int32)
    chip = jnp.reshape(2 * px + py, (1,)).astype(jnp.int32)

    x0 = x[0]
    mem0 = mem[0]
    tgt = loss_target[0]
    S, D = x0.shape
    depth = mix_norm_g.shape[0]
    FF = ffn_w_down.shape[1] * N_DEV
    a_in = a_w_in.shape[2] * N_DEV
    b_in = b_w_in.shape[2] * N_DEV
    a_q_blk = (a_in - GW) // GW
    b_q_blk = (b_in - GW) // GW

    stacks = {"kv": (w_mem_kv, m_w_mem_kv, v_w_mem_kv), "ain": (a_w_in, m_a_w_in, v_a_w_in),
              "aout": (a_w_out, m_a_w_out, v_a_w_out), "bin": (b_w_in, m_b_w_in, v_b_w_in),
              "bout": (b_w_out, m_b_w_out, v_b_w_out), "up": (ffn_w_up, m_ffn_w_up, v_ffn_w_up),
              "down": (ffn_w_down, m_ffn_w_down, v_ffn_w_down)}
    dev1 = jnp.reshape(dev, (1,)).astype(jnp.int32)

    def groups_of(i):
        j = i // 2
        mix = [("kv", i), ("ain", j), ("aout", j)] if i % 2 == 0 else [("kv", i), ("bin", j), ("bout", j)]
        return mix, [("up", i), ("down", i)]

    gather_groups = [(f"{half}{i}", members) for i in range(depth) for half, members in zip("mf", groups_of(i))]
    gather_ahead = 2
    in_flight = {}

    def gather_start(k, after):
        gname, members = gather_groups[k]
        lands = [_cast_place(f"place_{t}{l}", dev1, stacks[t][0], l) for t, l in members]
        ssem, rsem, lands, tok = _split_start(f"ag_start_{gname}", lands, 4 * len(lands), _gather_plan, after)
        in_flight[k] = (lands, ssem, rsem)
        return tok

    small_land = _cast_place("place_small_w", dev1, _pack([ffn_conv_w, b_v_norm_g])[None], 0, F32)
    small_ssem, small_rsem, small_lands, small_tok = _split_start("smallw_start", [small_land], N_DEV - 1,
                                                                  _broadcast_plan)
    start_tokens = [small_tok, gather_start(0, [small_tok])]
    passing = {}

    def gather_arrive(k, after):
        if k >= len(gather_groups):
            return []
        gname, members = gather_groups[k]
        lands, ssem, rsem = in_flight.pop(k)
        lands = _split_wait(f"ag_wait_{gname}", lands, ssem, rsem, _gather_plan, after)
        toks = []
        for q in (range(1, 1 + gather_ahead) if k == 0 else [k + gather_ahead]):
            if q < len(gather_groups):
                toks.append(gather_start(q, [lands[0]] + toks))
        ssem, rsem, lands, tok = _split_start(f"ag_pass_{gname}", lands, 3 * len(lands), _forward_plan, toks)
        passing[k] = (lands, ssem, rsem)
        return toks + [tok]

    def gather_ready(k, after):
        gname, members = gather_groups[k]
        lands, ssem, rsem = passing.pop(k)
        lands = _split_wait(f"ag_ready_{gname}", lands, ssem, rsem, _forward_plan, after)
        out = {}
        for (t, l), land in zip(members, lands):
            if t == "kv":
                out["kv"] = land.reshape(D, 2 * GW)
            elif t in ("ain", "aout", "up"):
                out[{"ain": "in", "aout": "out", "up": "up"}[t]] = land
            elif t == "bin":
                out["in"] = jnp.transpose(land, (1, 0, 2)).reshape(D, b_in)
            elif t == "bout":
                out["out"] = land.reshape(B_W + GW, D)
            else:
                out["down"] = land.reshape(FF, D)
        return out

    def small_weights(after):
        (small_all,) = _split_wait("smallw_wait", small_lands, small_ssem, small_rsem, _broadcast_plan, after)
        cw_parts, gv_parts = [], []
        for d in range(N_DEV):
            cw_d, gv_d = _unpack(small_all[d], [ffn_conv_w.shape, b_v_norm_g.shape])
            cw_parts.append(cw_d)
            gv_parts.append(gv_d)
        return jnp.concatenate(cw_parts, axis=-1), jnp.concatenate(gv_parts, axis=-1)

    def conv_params(i):
        cw = conv_w_full[i].reshape(3, 2, FF).transpose(1, 0, 2)
        cb = ffn_conv_b[i].reshape(2, 1, FF)
        return cw, cb

    saved = []
    W = []
    xc = x0
    toks = start_tokens + gather_arrive(0, [x0])
    for i in range(depth):
        j = i // 2
        lw = gather_ready(2 * i, [xc])
        sv = {"x0": xc}
        h1, h1t = _rms_fwd(f"mixnorm{i}", xc, mix_norm_g[i], deps=toks, with_t=True)
        memn = _rms_fwd(f"memnorm{i}", mem0, mem_norm_g[i])
        if i % 2 == 0:
            proj = _mm_gcols(f"ain{i}", h1, lw["in"])
        else:
            proj = _mm_full(f"bin{i}", h1, lw["in"])
        toks = gather_arrive(2 * i + 1, [proj]) if i > 0 else []
        kv = _mm_full(f"kvproj{i}", memn, lw["kv"], deps=toks)
        if i % 2 == 0:
            outs, lses = [], []
            for g in range(len(A_PATTERNS)):
                o, l = _attn_fwd(f"attn{i}_{g}", proj, g)
                outs.append(o)
                lses.append(l)
            tok, lse = _attn_merge(f"merge{i}", outs, lses)
            mo = _mem_fwd(f"memattn{i}", proj, a_q_blk, kv)
            cat = jnp.concatenate([tok.astype(BF16), mo.astype(BF16)], axis=1)
            x1 = _mm_gcols(f"aout{i}", cat, lw["out"], res=xc)
            sv.update(tok=tok, lse=lse)
        else:
            bias_t = b_s_bias[j].T
            tok = _sgu_fwd(f"sgu{i}", proj, gv_full[j], b_w_s[j], bias_t)
            mo = _mem_fwd(f"memattn{i}", proj, b_q_blk, kv)
            cat = jnp.concatenate([tok.astype(BF16), mo.astype(BF16)], axis=1)
            x1 = _mm_full(f"bout{i}", cat, lw["out"], res=xc)
        toks = gather_arrive(1, [x1]) if i == 0 else []
        lw.update(gather_ready(2 * i + 1, [x1]))
        W.append(lw)
        if i == 0:
            conv_w_full, gv_full = small_weights([x1])
        h2, h2t = _rms_fwd(f"ffnnorm{i}", x1, ffn_norm_g[i], deps=toks, with_t=True)
        cw, cb = conv_params(i)
        a3 = _mm_gcols(f"up{i}", h2, lw["up"], split_out=True, tm=512)
        toks = gather_arrive(2 * i + 2, [a3])
        act = _conv_fwd(f"conv{i}", a3, cw, cb)
        x2 = _mm_full(f"down{i}", act, lw["down"], res=x1, tm=512, tn=1024, tk=FF // 4, deps=toks)
        toks = []
        sv.update(h1t=h1t, memn=memn, kv=kv, proj=proj, cat=cat, x1=x1, h2t=h2t, a3=a3, act=act)
        saved.append(sv)
        xc = x2

    dx, dg_final, sq = _final("final", xc, tgt, final_norm_g)
    loss_local = sq[0, 0] * (0.5 / D)

    chain = {}

    def pair_begin(gname, members, dws):
        n = len(dws)
        recvs = [lax.empty((4,) + dw.shape[1:], F32) for dw in dws]
        ssem, rsem, bufs, tok = _split_start(f"rs_pair_start_{gname}", dws + recvs, 4 * n, _pair_plan(n))
        return dict(name=gname, members=members, n=n, bufs=bufs, sems=(ssem, rsem)), tok

    def pair_end_chip_begin(st, after):
        n, gname = st["n"], st["name"]
        bufs = _split_wait(f"rs_pair_wait_{gname}", st["bufs"], *st["sems"], _pair_plan(n), after)
        ps = [_pair_sum(f"rs_sum_{t}{l}", core, bufs[w], bufs[n + w]) for w, (t, l) in enumerate(st["members"])]
        recvs = [lax.empty((3,) + p.shape[1:], BF16) for p in ps]
        ssem, rsem, bufs, tok = _split_start(f"rs_chip_start_{gname}", ps + recvs, 3 * n, _chip_plan(n))
        return dict(name=gname, members=st["members"], n=n, bufs=bufs, sems=(ssem, rsem)), tok

    def chip_end_update(st, after, deps=()):
        n = st["n"]
        bufs = _split_wait(f"rs_chip_wait_{st['name']}", st["bufs"], *st["sems"], _chip_plan(n), after)
        for w, (t, l) in enumerate(st["members"]):
            wst, mst, vst = stacks[t]
            chain[t] = _adamw_shard(f"adamw_{t}{l}", chip, l, wst, mst, vst, bufs[w], bufs[n + w], chain.get(t), deps)

    pipe = {"pair": [], "chip": [], "deps": []}

    def take_deps():
        deps, pipe["deps"] = pipe["deps"], []
        return deps

    def submit(gname, members, dws):
        st, tok = pair_begin(gname, members, dws)
        pipe["pair"].append(st)
        pipe["deps"].append(tok)

    def advance(after):
        arrived, pipe["chip"] = pipe["chip"], []
        toks = []
        for st in pipe["pair"]:
            new, tok = pair_end_chip_begin(st, [after])
            pipe["chip"].append(new)
            toks.append(tok)
        pipe["pair"] = []
        for st in arrived:
            chip_end_update(st, [after], toks)
        pipe["deps"] += toks

    def small_start(tag, arrs, after):
        land = _cast_place(f"place_small_{tag}", dev1, _pack(arrs)[None], 0, F32)
        ssem, rsem, lands, tok = _split_start(f"small_start_{tag}", [land], N_DEV - 1, _broadcast_plan, after)
        return (lands, ssem, rsem), tok

    def small_end(tag, state, shapes, after):
        lands, ssem, rsem = state
        lands = _split_wait(f"small_wait_{tag}", lands, ssem, rsem, _broadcast_plan, after)
        return _unpack(_sum_slots(f"small_sum_{tag}", lands[0]), shapes)

    def late_small():
        return [dg_mix[0], dg_ffn[0], dg_mem[0], d_conv_b[0][None], d_conv_w[0][None]]

    assert depth >= 2
    big = {k: [None] * n for k, n in (("kv", depth), ("ain", depth // 2 + depth % 2), ("aout", depth // 2 + depth % 2),
                                      ("bin", depth // 2), ("bout", depth // 2), ("up", depth), ("down", depth))}
    dg_mix, dg_ffn, dg_mem = [None] * depth, [None] * depth, [None] * depth
    d_conv_w, d_conv_b = [None] * depth, [None] * depth
    d_gv, d_ws, d_sb = [None] * (depth // 2), [None] * (depth // 2), [None] * (depth // 2)
    for i in reversed(range(depth)):
        j = i // 2
        lw, sv = W[i], saved[i]
        cw, cb = conv_params(i)
        mix_members, ffn_members = groups_of(i)
        if i == 0:
            early_arrays = [loss_local.reshape(1), dg_final.reshape(D), jnp.concatenate(dg_mix[1:]),
                            jnp.concatenate(dg_ffn[1:]), jnp.concatenate(dg_mem[1:]), jnp.stack(d_ws), jnp.stack(d_sb),
                            jnp.stack(d_gv), jnp.stack(d_conv_b[1:]), jnp.stack(d_conv_w[1:])]
            early_state, tok = small_start("early", early_arrays, [dx])
            pipe["deps"].append(tok)
        deps = take_deps()
        dact = _mm_dx_full(f"ddown{i}", dx, lw["down"], tm=512, tko=FF // 4, tc=D, deps=deps)
        big["down"][i] = _mm_dw(f"wdown{i}", sv["act"], dx, deps=deps).reshape(N_DEV, FF // N_DEV, D)
        da3, dcw, dcb = _conv_bwd(f"dconv{i}", sv["a3"], cw, cb, dact)
        d_conv_w[i] = dcw.transpose(1, 0, 2).reshape(3, 2 * FF)
        d_conv_b[i] = dcb.reshape(2 * FF)
        advance(da3)
        deps = take_deps()
        dh2 = _mm_dx_gcols(f"dup{i}", da3, lw["up"], split_in=True, tm=512, deps=deps)
        big["up"][i] = _mm_dw_gcols(f"wup{i}", sv["h2t"], da3, N_DEV, split_in=True, tko=512, deps=deps, a_t=True)
        dx1, dg_ffn[i] = _rms_bwd(f"dffnnorm{i}", dh2, sv["x1"], ffn_norm_g[i], dx)
        submit(f"f{i}", ffn_members, [big["up"][i], big["down"][i]])
        deps = take_deps()
        if i % 2 == 0:
            dcat = _mm_dx_gcols(f"daout{i}", dx1, lw["out"], deps=deps)
            big["aout"][j] = _mm_dw_gcols(f"waout{i}", sv["cat"], dx1, N_DEV, deps=deps)
            parts = [None] * 9
            for g in range(len(A_PATTERNS)):
                dq, dk, dv = _attn_bwd(f"dattn{i}_{g}", sv["proj"], g, dcat, 0, sv["tok"], sv["lse"])
                parts[g], parts[3 + g], parts[6 + g] = dq, dk, dv
            dqm, dkv = _mem_bwd(f"dmemattn{i}", sv["proj"], a_q_blk, sv["kv"], dcat, 1)
            dproj = jnp.concatenate(parts + [dqm], axis=1)
            advance(dkv)
            deps = take_deps()
            dh1 = _mm_dx_gcols(f"dain{i}", dproj, lw["in"], deps=deps)
            big["ain"][j] = _mm_dw_gcols(f"wain{i}", sv["h1t"], dproj, N_DEV, deps=deps, a_t=True)
        else:
            dcat = _mm_dx_full(f"dbout{i}", dx1, lw["out"], tm=1024, deps=deps)
            big["bout"][j] = _mm_dw(f"wbout{i}", sv["cat"], dx1, deps=deps).reshape(N_DEV, (B_W + GW) // N_DEV, D)
            bias_t = b_s_bias[j].T
            du, dvp, dgv, dws, dbt = _sgu_bwd(f"dsgu{i}", sv["proj"], gv_full[j], b_w_s[j], bias_t, dcat)
            d_gv[j], d_ws[j], d_sb[j] = dgv.reshape(B_W), dws, dbt.T
            dqm, dkv = _mem_bwd(f"dmemattn{i}", sv["proj"], b_q_blk, sv["kv"], dcat, B_W // GW)
            dproj = jnp.concatenate([du, dvp, dqm], axis=1)
            advance(dkv)
            deps = take_deps()
            dh1 = _mm_dx_full(f"dbin{i}", dproj, lw["in"], tc=b_in // 2, deps=deps)
            dwin = _mm_dw(f"wbin{i}", sv["h1t"], dproj, tko=1024, tn=512, deps=deps, a_t=True)
            big["bin"][j] = dwin.reshape(D, N_DEV, b_in // N_DEV).transpose(1, 0, 2)
        big["kv"][i] = _mm_dw(f"wkv{i}", sv["memn"], dkv, tko=1024).reshape(N_DEV, D // N_DEV, 2 * GW)
        dmemn = _mm_dx_full(f"dkvproj{i}", dkv, lw["kv"], tko=1024)
        _, dg_mem[i] = _rms_bwd(f"dmemnorm{i}", dmemn, mem0, mem_norm_g[i])
        dx, dg_mix[i] = _rms_bwd(f"dmixnorm{i}", dh1, sv["x0"], mix_norm_g[i], dx1)
        submit(f"m{i}", mix_members, [big[t][l] for t, l in mix_members])
    grad_x = dx[None]

    last_chips, toks = [], []
    for st in pipe["pair"]:
        new, tok = pair_end_chip_begin(st, [dx])
        last_chips.append(new)
        toks.append(tok)
    late_state, late_tok = small_start("late", late_small(), toks)
    g_early = small_end("early", early_state, [a.shape for a in early_arrays], [dx])
    for st in pipe["chip"]:
        chip_end_update(st, [g_early[1]], [late_tok])
    g_late = small_end("late", late_state, [a.shape for a in late_small()], [chain[t][0] for t in chain])

    loss = g_early[0][0]
    layer0 = dict(zip(("mix", "ffn", "mem", "conv_b", "conv_w"), g_late))
    rest = dict(zip(("final", "mix", "ffn", "mem", "w_s", "s_bias", "gv", "conv_b", "conv_w"), g_early[1:]))
    g_cw_full = jnp.concatenate([layer0["conv_w"], rest["conv_w"]])
    g_gv = lax.dynamic_slice_in_dim(rest["gv"], dev * b_v_norm_g.shape[1], b_v_norm_g.shape[1], axis=1)
    g_cw = lax.dynamic_slice_in_dim(g_cw_full, dev * ffn_conv_w.shape[2], ffn_conv_w.shape[2], axis=2)
    g_all = [jnp.concatenate([layer0["mix"], rest["mix"]]), jnp.concatenate([layer0["ffn"], rest["ffn"]]),
             jnp.concatenate([layer0["mem"], rest["mem"]]), rest["w_s"], rest["s_bias"],
             jnp.concatenate([layer0["conv_b"], rest["conv_b"]]), rest["final"], g_gv, g_cw]
    names = ["mix_norm_g", "ffn_norm_g", "mem_norm_g", "b_w_s", "b_s_bias", "ffn_conv_b", "final_norm_g",
             "b_v_norm_g", "ffn_conv_w"]
    ws = [mix_norm_g, ffn_norm_g, mem_norm_g, b_w_s, b_s_bias, ffn_conv_b, final_norm_g, b_v_norm_g, ffn_conv_w]
    ms = [m_mix_norm_g, m_ffn_norm_g, m_mem_norm_g, m_b_w_s, m_b_s_bias, m_ffn_conv_b, m_final_norm_g,
          m_b_v_norm_g, m_ffn_conv_w]
    vs = [v_mix_norm_g, v_ffn_norm_g, v_mem_norm_g, v_b_w_s, v_b_s_bias, v_ffn_conv_b, v_final_norm_g,
          v_b_v_norm_g, v_ffn_conv_w]
    shapes = [w.shape for w in ws]
    d_p, m_p, v_p = _adamw_flat("adamw_small", _pack(ws), _pack(g_all), _pack(ms), _pack(vs))
    res = {}
    for n, g, d, nm, nv in zip(names, g_all, _unpack(d_p, shapes), _unpack(m_p, shapes), _unpack(v_p, shapes)):
        res[n] = [g, d, nm, nv]
    for st in last_chips:
        chip_end_update(st, [d_p] + [chain[t][0] for t in chain])
    for tag, name in (("kv", "w_mem_kv"), ("ain", "a_w_in"), ("aout", "a_w_out"), ("bin", "b_w_in"),
                      ("bout", "b_w_out"), ("up", "ffn_w_up"), ("down", "ffn_w_down")):
        res[name] = list(chain[tag])

    order = ["mix_norm_g", "ffn_norm_g", "mem_norm_g", "w_mem_kv", "a_w_in", "a_w_out", "b_w_in", "b_v_norm_g",
             "b_w_s", "b_s_bias", "b_w_out", "ffn_w_up", "ffn_conv_w", "ffn_conv_b", "ffn_w_down", "final_norm_g"]
    return (loss, grad_x, *[res[n][0] for n in order], *[res[n][1] for n in order],
            *[res[n][2] for n in order], *[res[n][3] for n in order])
```

```python
import functools

import numpy as np
import jax
import jax.numpy as jnp
from jax import lax
from jax.experimental import pallas as pl
from jax.experimental.pallas import tpu as pltpu

F32 = jnp.float32
BF16 = jnp.bfloat16
MESH = pl.DeviceIdType.MESH
AXES = ("x", "y", "c")
N_DEV = 8

EPS = 1e-6
NEG = -1e30
HEAD = 128
HPG = 4
GW = HPG * HEAD
A_PATTERNS = ((128, 1), (512, 4), (2048, 16))
A_HEADS = HPG * len(A_PATTERNS)
QBLK = 128
B_GROUPS = 12
B_W = B_GROUPS * HEAD
SLOPES = (2.0 ** (-8.0 * (np.arange(A_HEADS) + 1) / A_HEADS)).astype(np.float32)
SCALE = HEAD ** -0.5

ADAM_LR = 0.001
ADAM_B1 = 0.9
ADAM_B2 = 0.999
ADAM_EPS = 1e-08
ADAM_WD = 0.01
ADAM_STEP = 10

V7X_VMEM_LIMIT = 50 * 1024 * 1024

NN = (((1,), (0,)), ((), ()))
NT = (((1,), (1,)), ((), ()))
TN = (((0,), (0,)), ((), ()))


def _cp(*sem):
    return pltpu.CompilerParams(dimension_semantics=sem, vmem_limit_bytes=V7X_VMEM_LIMIT)


def _dot(a, b, dims=NN):
    return lax.dot_general(a.astype(BF16), b.astype(BF16), dims, preferred_element_type=F32)


def _tile(n, pref):
    t = min(n, pref)
    assert n % t == 0, (n, pref)
    return t


def _row_tile(rows, cols):
    best = None
    for t in range(16, rows + 1, 16):
        if rows % t == 0 and t * cols * 4 <= (1 << 20):
            best = t
    if best is None:
        best = rows
    return best


_DEP = pl.BlockSpec((8, 128), lambda *_: (0, 0))


def _matmul(name, dims, grid, a, a_spec, b, b_spec, out_shape, o_spec, tile, res=None, res_spec=None, deps=()):
    nk = grid[2]
    has_res = res is not None

    def body(*refs):
        a_ref, b_ref = refs[0], refs[1]
        r_ref = refs[2] if has_res else None
        o_ref, acc_ref = refs[-2], refs[-1]
        part = _dot(a_ref[...], b_ref[...], dims)

        def finish(val):
            if has_res:
                val = val + r_ref[...]
            o_ref[...] = val.astype(o_ref.dtype)

        if nk == 1:
            finish(part)
        else:
            k = pl.program_id(2)

            @pl.when(k == 0)
            def _():
                acc_ref[...] = part

            @pl.when(k > 0)
            def _():
                acc_ref[...] += part

            @pl.when(k == nk - 1)
            def _():
                finish(acc_ref[...])

    ins = [a, b] + ([res] if has_res else []) + list(deps)
    specs = [a_spec, b_spec] + ([res_spec] if has_res else []) + [_DEP] * len(deps)
    return pl.pallas_call(
        body, name=name, grid=grid, in_specs=specs, out_specs=o_spec, out_shape=out_shape,
        scratch_shapes=[pltpu.VMEM(tile if nk > 1 else (8, 128), F32)],
        compiler_params=_cp("parallel", "parallel", "arbitrary"))(*ins)


def _mm_full(name, a, w, res=None, tm=1024, tn=512, tk=2048, deps=()):
    M, K = a.shape
    N = w.shape[1]
    tm, tn, tk = _tile(M, tm), _tile(N, tn), _tile(K, tk)
    return _matmul(
        name, NN, (N // tn, M // tm, K // tk),
        a, pl.BlockSpec((tm, tk), lambda j, i, k: (i, k)),
        w, pl.BlockSpec((tk, tn), lambda j, i, k: (k, j)),
        jax.ShapeDtypeStruct((M, N), F32), pl.BlockSpec((tm, tn), lambda j, i, k: (i, j)), (tm, tn),
        res, pl.BlockSpec((tm, tn), lambda j, i, k: (i, j)), deps=deps)


def _mm_gcols(name, a, wg, res=None, split_out=False, tm=1024, deps=(), out_dtype=F32):
    M, K = a.shape
    G, _, Nl = wg.shape
    tm = _tile(M, tm)
    hg = G // 2
    if split_out:
        shape = jax.ShapeDtypeStruct((2, M, hg * Nl), out_dtype)
        o_spec = pl.BlockSpec((None, tm, Nl), lambda g, i, k: (g // hg, i, g % hg))
    else:
        shape = jax.ShapeDtypeStruct((M, G * Nl), out_dtype)
        o_spec = pl.BlockSpec((tm, Nl), lambda g, i, k: (i, g))
    return _matmul(
        name, NN, (G, M // tm, 1),
        a, pl.BlockSpec((tm, K), lambda g, i, k: (i, 0)),
        wg, pl.BlockSpec((None, K, Nl), lambda g, i, k: (g, 0, 0)),
        shape, o_spec, (tm, Nl),
        res, pl.BlockSpec((tm, Nl), lambda g, i, k: (i, g)), deps=deps)


def _mm_dx_full(name, dy, w, tm=512, tko=512, tc=2048, deps=()):
    M, N = dy.shape
    K = w.shape[0]
    tm, tko, tc = _tile(M, tm), _tile(K, tko), _tile(N, tc)
    return _matmul(
        name, NT, (K // tko, M // tm, N // tc),
        dy, pl.BlockSpec((tm, tc), lambda j, i, k: (i, k)),
        w, pl.BlockSpec((tko, tc), lambda j, i, k: (j, k)),
        jax.ShapeDtypeStruct((M, K), F32), pl.BlockSpec((tm, tko), lambda j, i, k: (i, j)), (tm, tko), deps=deps)


def _mm_dx_gcols(name, dy, wg, split_in=False, tm=1024, tko=1024, deps=()):
    G, K, Nl = wg.shape
    M = dy.shape[-2]
    tm, tko = _tile(M, tm), _tile(K, tko)
    hg = G // 2
    if split_in:
        dy_spec = pl.BlockSpec((None, tm, Nl), lambda j, i, g: (g // hg, i, g % hg))
    else:
        dy_spec = pl.BlockSpec((tm, Nl), lambda j, i, g: (i, g))
    return _matmul(
        name, NT, (K // tko, M // tm, G),
        dy, dy_spec,
        wg, pl.BlockSpec((None, tko, Nl), lambda j, i, g: (g, j, 0)),
        jax.ShapeDtypeStruct((M, K), F32), pl.BlockSpec((tm, tko), lambda j, i, g: (i, j)), (tm, tko), deps=deps)


def _lhs_of_dw(a, a_t, ts, tko, index):
    if a_t:
        return NN, pl.BlockSpec((tko, ts), lambda *ids: index(*ids))
    return TN, pl.BlockSpec((ts, tko), lambda *ids: index(*ids)[::-1])


def _mm_dw(name, a, dy, tko=512, tn=1024, ts=2048, deps=(), a_t=False):
    K1, S = a.shape if a_t else a.shape[::-1]
    N = dy.shape[1]
    tko, tn, ts = _tile(K1, tko), _tile(N, tn), _tile(S, ts)
    dims, a_spec = _lhs_of_dw(a, a_t, ts, tko, lambda i, j, k: (j, k))
    return _matmul(
        name, dims, (N // tn, K1 // tko, S // ts),
        a, a_spec,
        dy, pl.BlockSpec((ts, tn), lambda i, j, k: (k, i)),
        jax.ShapeDtypeStruct((K1, N), F32), pl.BlockSpec((tko, tn), lambda i, j, k: (j, i)), (tko, tn), deps=deps)


def _mm_dw_gcols(name, a, dy, G, split_in=False, tko=1024, ts=2048, deps=(), a_t=False):
    K1, S = a.shape if a_t else a.shape[::-1]
    Nl = (dy.shape[-1] * (2 if split_in else 1)) // G
    tko, ts = _tile(K1, tko), _tile(S, ts)
    hg = G // 2
    dims, a_spec = _lhs_of_dw(a, a_t, ts, tko, lambda g, j, k: (j, k))
    if split_in:
        dy_spec = pl.BlockSpec((None, ts, Nl), lambda g, j, k: (g // hg, k, g % hg))
    else:
        dy_spec = pl.BlockSpec((ts, Nl), lambda g, j, k: (k, g))
    return _matmul(
        name, dims, (G, K1 // tko, S // ts),
        a, a_spec,
        dy, dy_spec,
        jax.ShapeDtypeStruct((G, K1, Nl), F32), pl.BlockSpec((None, tko, Nl), lambda g, j, k: (g, j, 0)),
        (tko, Nl), deps=deps)


def _rms_fwd(name, x, g, tr=256, deps=(), with_t=False):
    S, D = x.shape
    tr = _tile(S, tr)

    def body(x_ref, g_ref, *rest):
        xf = x_ref[...]
        r = lax.rsqrt(jnp.mean(xf * xf, axis=-1, keepdims=True) + EPS)
        y = xf * r * g_ref[...]
        if with_t:
            rest[-2][...] = y.astype(BF16)
            rest[-1][...] = y.T.astype(BF16)
        else:
            rest[-1][...] = y.astype(BF16)

    row = pl.BlockSpec((tr, D), lambda i: (i, 0))
    out_specs, out_shape = row, jax.ShapeDtypeStruct((S, D), BF16)
    if with_t:
        out_specs = [row, pl.BlockSpec((D, tr), lambda i: (0, i))]
        out_shape = [out_shape, jax.ShapeDtypeStruct((D, S), BF16)]
    return pl.pallas_call(
        body, name=name, grid=(S // tr,),
        in_specs=[row, pl.BlockSpec((1, D), lambda i: (0, 0))] + [_DEP] * len(deps),
        out_specs=out_specs, out_shape=out_shape, compiler_params=_cp("parallel"))(x, g.reshape(1, D), *deps)


def _rms_bwd(name, dh, x, g, dres=None, tr=256):
    S, D = x.shape
    tr = _tile(S, tr)
    has_res = dres is not None

    def body(*refs):
        dh_ref, x_ref, g_ref = refs[:3]
        dres_ref = refs[3] if has_res else None
        dx_ref, dg_ref = refs[-2], refs[-1]
        xf = x_ref[...]
        r = lax.rsqrt(jnp.mean(xf * xf, axis=-1, keepdims=True) + EPS)
        xh = xf * r
        dhv = dh_ref[...]
        dxh = dhv * g_ref[...]
        dx = r * (dxh - xh * jnp.mean(dxh * xh, axis=-1, keepdims=True))
        if has_res:
            dx = dx + dres_ref[...]
        dx_ref[...] = dx
        part = jnp.sum(dhv * xh, axis=0, keepdims=True)
        i = pl.program_id(0)

        @pl.when(i == 0)
        def _():
            dg_ref[...] = part

        @pl.when(i > 0)
        def _():
            dg_ref[...] += part

    row = pl.BlockSpec((tr, D), lambda i: (i, 0))
    vec = pl.BlockSpec((1, D), lambda i: (0, 0))
    ins = [dh, x, g.reshape(1, D)] + ([dres] if has_res else [])
    return pl.pallas_call(
        body, name=name, grid=(S // tr,),
        in_specs=[row, row, vec] + ([row] if has_res else []),
        out_specs=[row, vec],
        out_shape=[jax.ShapeDtypeStruct((S, D), F32), jax.ShapeDtypeStruct((1, D), F32)],
        compiler_params=_cp("arbitrary"))(*ins)


def _final(name, x, tgt, g, tr=256):
    S, D = x.shape
    tr = _tile(S, tr)

    def body(x_ref, t_ref, g_ref, dx_ref, dg_ref, loss_ref):
        xf = x_ref[...]
        gv = g_ref[...]
        r = lax.rsqrt(jnp.mean(xf * xf, axis=-1, keepdims=True) + EPS)
        xh = xf * r
        err = xh * gv - t_ref[...]
        sq = jnp.sum(jnp.sum(err * err, axis=1, keepdims=True), axis=0, keepdims=True)
        dy = err * (1.0 / D)
        dxh = dy * gv
        dx_ref[...] = r * (dxh - xh * jnp.mean(dxh * xh, axis=-1, keepdims=True))
        part = jnp.sum(dy * xh, axis=0, keepdims=True)
        lpart = jnp.broadcast_to(sq, (8, 128))
        i = pl.program_id(0)

        @pl.when(i == 0)
        def _():
            dg_ref[...] = part
            loss_ref[...] = lpart

        @pl.when(i > 0)
        def _():
            dg_ref[...] += part
            loss_ref[...] += lpart

    row = pl.BlockSpec((tr, D), lambda i: (i, 0))
    vec = pl.BlockSpec((1, D), lambda i: (0, 0))
    return pl.pallas_call(
        body, name=name, grid=(S // tr,), in_specs=[row, row, vec],
        out_specs=[row, vec, pl.BlockSpec((8, 128), lambda i: (0, 0))],
        out_shape=[jax.ShapeDtypeStruct((S, D), F32), jax.ShapeDtypeStruct((1, D), F32),
                   jax.ShapeDtypeStruct((8, 128), F32)],
        compiler_params=_cp("arbitrary"))(x, tgt, g.reshape(1, D))


def _band_specs(nb, col_of):
    prev = pl.BlockSpec((QBLK, GW), lambda r, b: (jnp.maximum(b - 1, 0), col_of(r)))
    cur = pl.BlockSpec((QBLK, GW), lambda r, b: (b, col_of(r)))
    nxt = pl.BlockSpec((QBLK, GW), lambda r, b: (jnp.minimum(b + 1, nb - 1), col_of(r)))
    return [prev, cur, nxt]


def _cat3(refs, sl):
    return jnp.concatenate([ref[:, sl] for ref in refs], axis=0)


def _attn_fwd(name, proj, g):
    window, dil = A_PATTERNS[g]
    n_side = (window // 2) // dil
    S, C = proj.shape
    L = S // dil
    nb = L // QBLK
    cb = C // GW
    pv = proj.reshape(L, dil * C)
    ng = len(A_PATTERNS)

    def body(q_ref, kp, kc, kn, vp, vc, vn, o_ref, lse_ref):
        b = pl.program_id(1)
        jq = b * QBLK + lax.broadcasted_iota(jnp.int32, (QBLK, 3 * QBLK), 0)
        jk = (b - 1) * QBLK + lax.broadcasted_iota(jnp.int32, (QBLK, 3 * QBLK), 1)
        rel = jnp.abs(jk - jq)
        mask = (rel <= n_side) & (jk >= 0) & (jk < L)
        dist = rel.astype(F32) * float(dil)
        for hh in range(HPG):
            sl = slice(hh * HEAD, (hh + 1) * HEAD)
            k = _cat3((kp, kc, kn), sl)
            v = _cat3((vp, vc, vn), sl)
            s = _dot(q_ref[:, sl], k, NT) * SCALE - float(SLOPES[g * HPG + hh]) * dist
            s = jnp.where(mask, s, NEG)
            m = jnp.max(s, axis=1, keepdims=True)
            p = jnp.exp(s - m)
            l = jnp.sum(p, axis=1, keepdims=True)
            o_ref[:, sl] = _dot(p, v) / l
            lse_ref[:, sl] = jnp.broadcast_to(m + jnp.log(l), (QBLK, HEAD))

    q_spec = pl.BlockSpec((QBLK, GW), lambda r, b: (b, r * cb + g))
    k_specs = _band_specs(nb, lambda r: r * cb + ng + g)
    v_specs = _band_specs(nb, lambda r: r * cb + 2 * ng + g)
    o_spec = pl.BlockSpec((QBLK, GW), lambda r, b: (b, r))
    shape = jax.ShapeDtypeStruct((L, dil * GW), F32)
    o, lse = pl.pallas_call(
        body, name=name, grid=(dil, nb), in_specs=[q_spec] + k_specs + v_specs,
        out_specs=[o_spec, o_spec], out_shape=[shape, shape],
        compiler_params=_cp("parallel", "parallel"))(pv, pv, pv, pv, pv, pv, pv)
    return o.reshape(S, GW), lse.reshape(S, GW)


def _attn_merge(name, outs, lses, tr=256):
    S = outs[0].shape[0]
    tr = _tile(S, tr)
    ng = len(outs)

    def body(*refs):
        o_refs, l_refs = refs[:ng], refs[ng:2 * ng]
        tok_ref, lse_ref = refs[-2], refs[-1]
        ls = [r[...] for r in l_refs]
        m = functools.reduce(jnp.maximum, ls)
        es = [jnp.exp(l - m) for l in ls]
        tot = functools.reduce(lambda a, b: a + b, es)
        acc = None
        for e, o_ref in zip(es, o_refs):
            term = (e / tot) * o_ref[...]
            acc = term if acc is None else acc + term
        tok_ref[...] = acc
        lse_ref[...] = m + jnp.log(tot)

    row = pl.BlockSpec((tr, GW), lambda i: (i, 0))
    shape = jax.ShapeDtypeStruct((S, GW), F32)
    return pl.pallas_call(
        body, name=name, grid=(S // tr,), in_specs=[row] * (2 * ng), out_specs=[row, row],
        out_shape=[shape, shape], compiler_params=_cp("parallel"))(*outs, *lses)


def _attn_bwd(name, proj, g, dtok_src, dtok_blk, tok, lse):
    window, dil = A_PATTERNS[g]
    n_side = (window // 2) // dil
    S, C = proj.shape
    L = S // dil
    nb = L // QBLK
    cb = C // GW
    ng = len(A_PATTERNS)
    pv = proj.reshape(L, dil * C)
    dcb = dtok_src.shape[1] // GW
    dv_ = dtok_src.reshape(L, dil * dtok_src.shape[1])
    ov = tok.reshape(L, dil * GW)
    lv = lse.reshape(L, dil * GW)

    def body(qp, qc, qn, kp, kc, kn, vp, vc, vn, dop, doc, don, op, oc, on, lp, lc, ln,
             dq_ref, dk_ref, dv_ref):
        b = pl.program_id(1)
        jq = b * QBLK + lax.broadcasted_iota(jnp.int32, (QBLK, 3 * QBLK), 0)
        jk = (b - 1) * QBLK + lax.broadcasted_iota(jnp.int32, (QBLK, 3 * QBLK), 1)
        rel = jnp.abs(jk - jq)
        mask = (rel <= n_side) & (jk >= 0) & (jk < L)
        dist = rel.astype(F32) * float(dil)
        jq3 = (b - 1) * QBLK + lax.broadcasted_iota(jnp.int32, (3 * QBLK, QBLK), 0)
        jk1 = b * QBLK + lax.broadcasted_iota(jnp.int32, (3 * QBLK, QBLK), 1)
        rel3 = jnp.abs(jk1 - jq3)
        mask3 = (rel3 <= n_side) & (jq3 >= 0) & (jq3 < L)
        dist3 = rel3.astype(F32) * float(dil)
        for hh in range(HPG):
            sl = slice(hh * HEAD, (hh + 1) * HEAD)
            one = slice(hh * HEAD, hh * HEAD + 1)
            slope = float(SLOPES[g * HPG + hh])
            q = qc[:, sl]
            do = doc[:, sl]
            k3 = _cat3((kp, kc, kn), sl)
            v3 = _cat3((vp, vc, vn), sl)
            delta = jnp.sum(do * oc[:, sl], axis=1, keepdims=True)
            s = _dot(q, k3, NT) * SCALE - slope * dist
            p = jnp.where(mask, jnp.exp(s - lc[:, one]), 0.0)
            ds = p * (_dot(do, v3, NT) - delta)
            dq_ref[:, sl] = (_dot(ds, k3) * SCALE).astype(dq_ref.dtype)

            q3 = _cat3((qp, qc, qn), sl)
            do3 = _cat3((dop, doc, don), sl)
            o3 = _cat3((op, oc, on), sl)
            lse3 = _cat3((lp, lc, ln), sl)[:, :1]
            delta3 = jnp.sum(do3 * o3, axis=1, keepdims=True)
            k = kc[:, sl]
            v = vc[:, sl]
            s3 = _dot(q3, k, NT) * SCALE - slope * dist3
            p3 = jnp.where(mask3, jnp.exp(s3 - lse3), 0.0)
            ds3 = p3 * (_dot(do3, v, NT) - delta3)
            dv_ref[:, sl] = _dot(p3, do3, TN).astype(dv_ref.dtype)
            dk_ref[:, sl] = (_dot(ds3, q3, TN) * SCALE).astype(dk_ref.dtype)

    specs = (_band_specs(nb, lambda r: r * cb + g) + _band_specs(nb, lambda r: r * cb + ng + g)
             + _band_specs(nb, lambda r: r * cb + 2 * ng + g)
             + _band_specs(nb, lambda r: r * dcb + dtok_blk)
             + _band_specs(nb, lambda r: r) + _band_specs(nb, lambda r: r))
    o_spec = pl.BlockSpec((QBLK, GW), lambda r, b: (b, r))
    shape = jax.ShapeDtypeStruct((L, dil * GW), BF16)
    outs = pl.pallas_call(
        body, name=name, grid=(dil, nb), in_specs=specs, out_specs=[o_spec] * 3, out_shape=[shape] * 3,
        compiler_params=_cp("parallel", "parallel"))(*([pv] * 9 + [dv_] * 3 + [ov] * 3 + [lv] * 3))
    return [o.reshape(S, GW) for o in outs]


def _mem_fwd(name, proj, q_blk, kv, tq=256):
    S = proj.shape[0]
    M = kv.shape[0]
    tq = _tile(S, tq)

    def body(q_ref, kv_ref, o_ref):
        for hh in range(HPG):
            sl = slice(hh * HEAD, (hh + 1) * HEAD)
            k = kv_ref[:, sl]
            v = kv_ref[:, GW + hh * HEAD:GW + (hh + 1) * HEAD]
            s = _dot(q_ref[:, sl], k, NT) * SCALE
            m = jnp.max(s, axis=1, keepdims=True)
            p = jnp.exp(s - m)
            p = p / jnp.sum(p, axis=1, keepdims=True)
            o_ref[:, sl] = _dot(p, v)

    return pl.pallas_call(
        body, name=name, grid=(S // tq,),
        in_specs=[pl.BlockSpec((tq, GW), lambda i: (i, q_blk)), pl.BlockSpec((M, 2 * GW), lambda i: (0, 0))],
        out_specs=pl.BlockSpec((tq, GW), lambda i: (i, 0)),
        out_shape=jax.ShapeDtypeStruct((S, GW), F32), compiler_params=_cp("parallel"))(proj, kv)


def _mem_bwd(name, proj, q_blk, kv, dcat, do_blk, tq=256):
    S = proj.shape[0]
    M = kv.shape[0]
    tq = _tile(S, tq)

    def body(q_ref, kv_ref, do_ref, dq_ref, dkv_ref):
        i = pl.program_id(0)
        for hh in range(HPG):
            sl = slice(hh * HEAD, (hh + 1) * HEAD)
            vsl = slice(GW + hh * HEAD, GW + (hh + 1) * HEAD)
            q = q_ref[:, sl]
            do = do_ref[:, sl]
            k = kv_ref[:, sl]
            v = kv_ref[:, vsl]
            s = _dot(q, k, NT) * SCALE
            m = jnp.max(s, axis=1, keepdims=True)
            p = jnp.exp(s - m)
            p = p / jnp.sum(p, axis=1, keepdims=True)
            dp = _dot(do, v, NT)
            ds = p * (dp - jnp.sum(dp * p, axis=1, keepdims=True))
            dq_ref[:, sl] = (_dot(ds, k) * SCALE).astype(dq_ref.dtype)
            dk = _dot(ds, q, TN) * SCALE
            dvv = _dot(p, do, TN)

            @pl.when(i == 0)
            def _():
                dkv_ref[:, sl] = dk
                dkv_ref[:, vsl] = dvv

            @pl.when(i > 0)
            def _():
                dkv_ref[:, sl] += dk
                dkv_ref[:, vsl] += dvv

    return pl.pallas_call(
        body, name=name, grid=(S // tq,),
        in_specs=[pl.BlockSpec((tq, GW), lambda i: (i, q_blk)), pl.BlockSpec((M, 2 * GW), lambda i: (0, 0)),
                  pl.BlockSpec((tq, GW), lambda i: (i, do_blk))],
        out_specs=[pl.BlockSpec((tq, GW), lambda i: (i, 0)), pl.BlockSpec((M, 2 * GW), lambda i: (0, 0))],
        out_shape=[jax.ShapeDtypeStruct((S, GW), BF16), jax.ShapeDtypeStruct((M, 2 * GW), F32)],
        compiler_params=_cp("arbitrary"))(proj, kv, dcat)


_RSQRT2 = float(1.0 / np.sqrt(2.0))
_RSQRT2PI = float(1.0 / np.sqrt(2.0 * np.pi))


def _gelu(x):
    return 0.5 * x * (1.0 + lax.erf(x * _RSQRT2))


def _gelu_grad(x):
    return 0.5 * (1.0 + lax.erf(x * _RSQRT2)) + x * jnp.exp(-0.5 * x * x) * _RSQRT2PI


def _sgu_fwd(name, proj, gv, w_s, bias_t):
    S = proj.shape[0]
    nch = S // HEAD

    def body(u_ref, v_ref, gv_ref, ws_ref, b_ref, o_ref):
        v = _gelu(v_ref[...])
        r = lax.rsqrt(jnp.mean(v * v, axis=-1, keepdims=True) + EPS)
        vn = v * r * gv_ref[...]
        for gg in range(B_GROUPS):
            sl = slice(gg * HEAD, (gg + 1) * HEAD)
            mixed = _dot(ws_ref[gg], vn[:, sl]) + b_ref[:, gg:gg + 1]
            o_ref[:, sl] = _gelu(u_ref[:, sl]) * mixed

    return pl.pallas_call(
        body, name=name, grid=(nch,),
        in_specs=[pl.BlockSpec((HEAD, B_W), lambda c: (c, 0)), pl.BlockSpec((HEAD, B_W), lambda c: (c, 1)),
                  pl.BlockSpec((1, B_W), lambda c: (0, 0)),
                  pl.BlockSpec((B_GROUPS, HEAD, HEAD), lambda c: (0, 0, 0)),
                  pl.BlockSpec((HEAD, B_GROUPS), lambda c: (0, 0))],
        out_specs=pl.BlockSpec((HEAD, B_W), lambda c: (c, 0)),
        out_shape=jax.ShapeDtypeStruct((S, B_W), F32),
        compiler_params=_cp("parallel"))(proj, proj, gv.reshape(1, B_W), w_s, bias_t)


def _sgu_bwd(name, proj, gv, w_s, bias_t, dcat):
    S = proj.shape[0]
    nch = S // HEAD

    def body(u_ref, v_ref, gv_ref, ws_ref, b_ref, dt_ref, du_ref, dvp_ref, dgv_ref, dws_ref, db_ref, dvn_ref):
        c = pl.program_id(0)
        vpre = v_ref[...]
        v = _gelu(vpre)
        r = lax.rsqrt(jnp.mean(v * v, axis=-1, keepdims=True) + EPS)
        vh = v * r
        gvv = gv_ref[...]
        vn = vh * gvv
        for gg in range(B_GROUPS):
            sl = slice(gg * HEAD, (gg + 1) * HEAD)
            upre = u_ref[:, sl]
            dt = dt_ref[:, sl]
            vng = vn[:, sl]
            mixed = _dot(ws_ref[gg], vng) + b_ref[:, gg:gg + 1]
            du_ref[:, sl] = (dt * mixed * _gelu_grad(upre)).astype(du_ref.dtype)
            dmix = dt * _gelu(upre)
            dvn_ref[:, sl] = _dot(ws_ref[gg], dmix, TN)
            dws = _dot(dmix, vng, NT)
            dbs = jnp.sum(dmix, axis=1, keepdims=True)

            @pl.when(c == 0)
            def _():
                dws_ref[gg] = dws
                db_ref[:, gg:gg + 1] = dbs

            @pl.when(c > 0)
            def _():
                dws_ref[gg] += dws
                db_ref[:, gg:gg + 1] += dbs

        dvn = dvn_ref[...]
        dgp = jnp.sum(dvn * vh, axis=0, keepdims=True)
        dvh = dvn * gvv
        dv = r * (dvh - vh * jnp.mean(dvh * vh, axis=-1, keepdims=True))
        dvp_ref[...] = (dv * _gelu_grad(vpre)).astype(dvp_ref.dtype)

        @pl.when(c == 0)
        def _():
            dgv_ref[...] = dgp

        @pl.when(c > 0)
        def _():
            dgv_ref[...] += dgp

    blk = lambda j: pl.BlockSpec((HEAD, B_W), lambda c: (c, j))
    vec = pl.BlockSpec((1, B_W), lambda c: (0, 0))
    ws_spec = pl.BlockSpec((B_GROUPS, HEAD, HEAD), lambda c: (0, 0, 0))
    b_spec = pl.BlockSpec((HEAD, B_GROUPS), lambda c: (0, 0))
    du, dvp, dgv, dws, db = pl.pallas_call(
        body, name=name, grid=(nch,),
        in_specs=[blk(0), blk(1), vec, ws_spec, b_spec, blk(0)],
        out_specs=[blk(0), blk(0), vec, ws_spec, b_spec],
        out_shape=[jax.ShapeDtypeStruct((S, B_W), BF16), jax.ShapeDtypeStruct((S, B_W), BF16),
                   jax.ShapeDtypeStruct((1, B_W), F32), jax.ShapeDtypeStruct((B_GROUPS, HEAD, HEAD), F32),
                   jax.ShapeDtypeStruct((HEAD, B_GROUPS), F32)],
        scratch_shapes=[pltpu.VMEM((HEAD, B_W), F32)],
        compiler_params=_cp("arbitrary"))(proj, proj, gv.reshape(1, B_W), w_s, bias_t, dcat)
    return du, dvp, dgv, dws, db


def _shift_down(a, row):
    return jnp.where(row == 0, 0.0, pltpu.roll(a, 1, 0))


def _shift_up(a, row):
    n = a.shape[0]
    return jnp.where(row == n - 1, 0.0, pltpu.roll(a, n - 1, 0))


def _conv(a, w, b, row):
    return _shift_down(a, row) * w[0:1] + a * w[1:2] + _shift_up(a, row) * w[2:3] + b


def _conv_fwd(name, a3, cw, cb, tc=256):
    _, S, FF = a3.shape
    tc = _tile(FF, tc)

    def body(a_ref, w_ref, b_ref, o_ref):
        row = lax.broadcasted_iota(jnp.int32, (S, tc), 0)
        cg = _conv(a_ref[0], w_ref[0], b_ref[0], row)
        cv = _conv(a_ref[1], w_ref[1], b_ref[1], row)
        o_ref[...] = (_gelu(cg) * cv).astype(o_ref.dtype)

    return pl.pallas_call(
        body, name=name, grid=(FF // tc,),
        in_specs=[pl.BlockSpec((2, S, tc), lambda j: (0, 0, j)), pl.BlockSpec((2, 3, tc), lambda j: (0, 0, j)),
                  pl.BlockSpec((2, 1, tc), lambda j: (0, 0, j))],
        out_specs=pl.BlockSpec((S, tc), lambda j: (0, j)),
        out_shape=jax.ShapeDtypeStruct((S, FF), BF16), compiler_params=_cp("parallel"))(a3, cw, cb)


def _conv_bwd(name, a3, cw, cb, dact, tc=128):
    _, S, FF = a3.shape
    tc = _tile(FF, tc)

    def body(a_ref, w_ref, b_ref, d_ref, da_ref, dw_ref, db_ref):
        row = lax.broadcasted_iota(jnp.int32, (S, tc), 0)
        ag, av = a_ref[0], a_ref[1]
        wg, wv = w_ref[0], w_ref[1]
        cg = _conv(ag, wg, b_ref[0], row)
        cv = _conv(av, wv, b_ref[1], row)
        d = d_ref[...]
        dcs = (d * cv * _gelu_grad(cg), d * _gelu(cg))
        for h, (dc, a, w) in enumerate(zip(dcs, (ag, av), (wg, wv))):
            da = _shift_up(dc, row) * w[0:1] + dc * w[1:2] + _shift_down(dc, row) * w[2:3]
            da_ref[h] = da.astype(da_ref.dtype)
            dw_ref[h, 0:1, :] = jnp.sum(dc * _shift_down(a, row), axis=0, keepdims=True)
            dw_ref[h, 1:2, :] = jnp.sum(dc * a, axis=0, keepdims=True)
            dw_ref[h, 2:3, :] = jnp.sum(dc * _shift_up(a, row), axis=0, keepdims=True)
            db_ref[h] = jnp.sum(dc, axis=0, keepdims=True)

    a_spec = pl.BlockSpec((2, S, tc), lambda j: (0, 0, j))
    w_spec = pl.BlockSpec((2, 3, tc), lambda j: (0, 0, j))
    b_spec = pl.BlockSpec((2, 1, tc), lambda j: (0, 0, j))
    return pl.pallas_call(
        body, name=name, grid=(FF // tc,),
        in_specs=[a_spec, w_spec, b_spec, pl.BlockSpec((S, tc), lambda j: (0, j))],
        out_specs=[a_spec, w_spec, b_spec],
        out_shape=[jax.ShapeDtypeStruct((2, S, FF), BF16), jax.ShapeDtypeStruct((2, 3, FF), F32),
                   jax.ShapeDtypeStruct((2, 1, FF), F32)],
        compiler_params=_cp("parallel"))(a3, cw, cb, dact)


_HBM = pl.BlockSpec(memory_space=pltpu.HBM)


def _position():
    return lax.axis_index("x"), lax.axis_index("y"), lax.axis_index("c")


_SEM =pl.BlockSpec(memory_space=pltpu.SEMAPHORE)
_EFFECT = pltpu.SideEffectType.DATAFLOW_SIDE_EFFECTING
_FLIPS = ((1, 0), (0, 1), (1, 1))


def _split_start(name, bufs, ncopy, plan, after=()):
    n = len(bufs)
    after = list(after)

    def body(*refs):
        ins = refs[:n]
        send_sems, recv_sems, token = refs[n + len(after)], refs[n + len(after) + 1], refs[-1]
        for i, (src, dst, to) in enumerate(plan(ins)):
            pltpu.make_async_remote_copy(src_ref=src, dst_ref=dst, send_sem=send_sems.at[i], recv_sem=recv_sems.at[i],
                                         device_id=to, device_id_type=MESH).start()
        token[...] = jnp.zeros_like(token)

    outs = pl.pallas_call(
        body, name=name,
        out_shape=(pltpu.SemaphoreType.DMA((ncopy,)), pltpu.SemaphoreType.DMA((ncopy,)),
                   *[pltpu.HBM(b.shape, b.dtype) for b in bufs], jax.ShapeDtypeStruct((8, 128), F32)),
        in_specs=[_HBM] * n + [pl.BlockSpec(memory_space=pl.ANY)] * len(after),
        out_specs=(_SEM, _SEM, *([_HBM] * n), pl.BlockSpec(memory_space=pltpu.VMEM)),
        input_output_aliases={i: 2 + i for i in range(n)},
        compiler_params=pltpu.CompilerParams(has_side_effects=_EFFECT),
    )(*[pltpu.with_memory_space_constraint(b, pltpu.HBM) for b in bufs], *after)
    return outs[0], outs[1], list(outs[2:2 + n]), outs[-1]


def _split_wait(name, bufs, send_sems, recv_sems, plan, after):
    n = len(bufs)
    after = list(after)

    def body(*refs):
        ins = refs[:n]
        ssem, rsem = refs[n], refs[n + 1]
        for i, (src, dst, to) in enumerate(plan(ins)):
            cp = pltpu.make_async_remote_copy(src_ref=src, dst_ref=dst, send_sem=ssem.at[i], recv_sem=rsem.at[i],
                                              device_id=to, device_id_type=MESH)
            cp.wait_send()
            cp.wait_recv()

    outs = pl.pallas_call(
        body, name=name, out_shape=tuple(pltpu.HBM(b.shape, b.dtype) for b in bufs),
        in_specs=[_HBM] * n + [_SEM, _SEM] + [pl.BlockSpec(memory_space=pl.ANY)] * len(after),
        out_specs=tuple([_HBM] * n), input_output_aliases={i: i for i in range(n)},
        compiler_params=pltpu.CompilerParams(has_side_effects=_EFFECT),
    )(*bufs, send_sems, recv_sems, *after)
    return list(outs)


def _gather_plan(refs):
    px, py, pc = _position()
    me = 4 * px + 2 * py + pc
    targets = [(px, py, 1 - pc), (1 - px, py, pc), (px, 1 - py, pc), (1 - px, 1 - py, pc)]
    return [(r.at[me], r.at[me], to) for r in refs for to in targets]


def _forward_plan(refs):
    px, py, pc = _position()
    out = []
    for r in refs:
        for fx, fy in _FLIPS:
            slot = 4 * (1 - px if fx else px) + 2 * (1 - py if fy else py) + pc
            out.append((r.at[slot], r.at[slot], (px, py, 1 - pc)))
    return out


def _pair_plan(n):
    def plan(refs):
        px, py, pc = _position()
        return [(refs[w].at[2 * k + (1 - pc)], refs[n + w].at[k], (px, py, 1 - pc)) for w in range(n) for k in range(4)]
    return plan


def _chip_plan(n):
    def plan(refs):
        px, py, pc = _position()
        out = []
        for w in range(n):
            for j, (fx, fy) in enumerate(_FLIPS):
                qx = 1 - px if fx else px
                qy = 1 - py if fy else py
                out.append((refs[w].at[2 * qx + qy], refs[n + w].at[j], (qx, qy, pc)))
        return out
    return plan


def _broadcast_plan(refs):
    px, py, pc = _position()
    me = 4 * px + 2 * py + pc
    flips = [(fx, fy, fc) for fx in (0, 1) for fy in (0, 1) for fc in (0, 1)][1:]
    targets = [(1 - px if fx else px, 1 - py if fy else py, 1 - pc if fc else pc) for fx, fy, fc in flips]
    return [(r.at[me], r.at[me], to) for r in refs for to in targets]


def _cast_place(name, dev, w, layer, dtype=BF16):
    nl, R, C = w.shape
    tr = _row_tile(R, C)

    def body(dev_ref, w_ref, o_ref):
        o_ref[...] = w_ref[...].astype(o_ref.dtype)

    return pl.pallas_call(
        body, name=name,
        grid_spec=pltpu.PrefetchScalarGridSpec(
            num_scalar_prefetch=1, grid=(R // tr,),
            in_specs=[pl.BlockSpec((None, tr, C), lambda i, d: (layer, i, 0))],
            out_specs=pl.BlockSpec((None, tr, C), lambda i, d: (d[0], i, 0))),
        out_shape=jax.ShapeDtypeStruct((N_DEV, R, C), dtype), compiler_params=_cp("parallel"))(dev, w)


def _pair_sum(name, core, dw, recv):
    _, R, C = dw.shape
    tr = _row_tile(R, C)
    dw4 = dw.reshape(4, 2, R, C)

    def body(core_ref, a_ref, b_ref, o_ref):
        o_ref[...] = (a_ref[...] + b_ref[...]).astype(o_ref.dtype)

    return pl.pallas_call(
        body, name=name,
        grid_spec=pltpu.PrefetchScalarGridSpec(
            num_scalar_prefetch=1, grid=(4, R // tr),
            in_specs=[pl.BlockSpec((None, None, tr, C), lambda k, i, c_ref: (k, c_ref[0], i, 0)),
                      pl.BlockSpec((None, tr, C), lambda k, i, c_ref: (k, i, 0))],
            out_specs=pl.BlockSpec((None, tr, C), lambda k, i, c_ref: (k, i, 0))),
        out_shape=jax.ShapeDtypeStruct((4, R, C), BF16),
        compiler_params=_cp("parallel", "parallel"))(core, dw4, recv)


def _adamw_math(w, g, m, v):
    m = ADAM_B1 * m + (1.0 - ADAM_B1) * g
    v = ADAM_B2 * v + (1.0 - ADAM_B2) * (g * g)
    m_hat = m / (1.0 - ADAM_B1 ** ADAM_STEP)
    v_hat = v / (1.0 - ADAM_B2 ** ADAM_STEP)
    delta = -ADAM_LR * (m_hat / (jnp.sqrt(v_hat) + ADAM_EPS) + ADAM_WD * w)
    return delta, m, v


def _adamw_shard(name, chip, layer, w, m, v, p, recv, prev, deps=()):
    nl, R, C = w.shape
    tr = _row_tile(R, C)
    n_prev = 0 if prev is None else 4

    def body(chip_ref, w_ref, m_ref, v_ref, p_ref, r_ref, *rest):
        g_ref, d_ref, nm_ref, nv_ref = rest[-4:]
        g = p_ref[...].astype(F32)
        for j in range(3):
            g = g + r_ref[j].astype(F32)
        delta, nm, nv = _adamw_math(w_ref[...], g, m_ref[...], v_ref[...])
        g_ref[...] = g
        d_ref[...] = delta
        nm_ref[...] = nm
        nv_ref[...] = nv

    lay = pl.BlockSpec((None, tr, C), lambda i, c_ref: (layer, i, 0))
    in_specs = [lay, lay, lay,
                pl.BlockSpec((None, tr, C), lambda i, c_ref: (c_ref[0], i, 0)),
                pl.BlockSpec((3, tr, C), lambda i, c_ref: (0, i, 0))]
    in_specs += [pl.BlockSpec(memory_space=pl.ANY)] * n_prev + [_DEP] * len(deps)
    shape = jax.ShapeDtypeStruct((nl, R, C), F32)
    ins = [chip, w, m, v, p, recv] + ([] if prev is None else list(prev)) + list(deps)
    return pl.pallas_call(
        body, name=name,
        grid_spec=pltpu.PrefetchScalarGridSpec(
            num_scalar_prefetch=1, grid=(R // tr,), in_specs=in_specs, out_specs=[lay] * 4),
        out_shape=[shape] * 4,
        input_output_aliases={6 + j: j for j in range(n_prev)},
        compiler_params=_cp("parallel"))(*ins)


def _sum_slots(name, parts, tr=512):
    n, R, C = parts.shape
    tr = _tile(R, tr)

    def body(p_ref, o_ref):
        acc = p_ref[0]
        for j in range(1, n):
            acc = acc + p_ref[j]
        o_ref[...] = acc

    return pl.pallas_call(
        body, name=name, grid=(R // tr,),
        in_specs=[pl.BlockSpec((n, tr, C), lambda i: (0, i, 0))],
        out_specs=pl.BlockSpec((tr, C), lambda i: (i, 0)),
        out_shape=jax.ShapeDtypeStruct((R, C), F32), compiler_params=_cp("parallel"))(parts)


def _adamw_flat(name, w, g, m, v, tr=512):
    R, C = w.shape
    tr = _tile(R, tr)

    def body(w_ref, g_ref, m_ref, v_ref, d_ref, nm_ref, nv_ref):
        delta, nm, nv = _adamw_math(w_ref[...], g_ref[...], m_ref[...], v_ref[...])
        d_ref[...] = delta
        nm_ref[...] = nm
        nv_ref[...] = nv

    row = pl.BlockSpec((tr, C), lambda i: (i, 0))
    shape = jax.ShapeDtypeStruct((R, C), F32)
    return pl.pallas_call(
        body, name=name, grid=(R // tr,), in_specs=[row] * 4, out_specs=[row] * 3, out_shape=[shape] * 3,
        compiler_params=_cp("parallel"))(w, g, m, v)


_PACK_ROWS = 512


def _pack(arrs):
    flat = jnp.concatenate([a.reshape(-1) for a in arrs])
    unit = _PACK_ROWS * 128
    pad = (-flat.shape[0]) % unit
    return jnp.pad(flat, (0, pad)).reshape(-1, 128)


def _unpack(packed, shapes):
    flat = packed.reshape(-1)
    outs, off = [], 0
    for s in shapes:
        n = int(np.prod(s))
        outs.append(flat[off:off + n].reshape(s))
        off += n
    return outs


def kernel(x, mem, mix_norm_g, ffn_norm_g, mem_norm_g, w_mem_kv, a_w_in, a_w_out, b_w_in, b_v_norm_g, b_w_s, b_s_bias, b_w_out, ffn_w_up, ffn_conv_w, ffn_conv_b, ffn_w_down, final_norm_g, loss_target, m_mix_norm_g, m_ffn_norm_g, m_mem_norm_g, m_w_mem_kv, m_a_w_in, m_a_w_out, m_b_w_in, m_b_v_norm_g, m_b_w_s, m_b_s_bias, m_b_w_out, m_ffn_w_up, m_ffn_conv_w, m_ffn_conv_b, m_ffn_w_down, m_final_norm_g, v_mix_norm_g, v_ffn_norm_g, v_mem_norm_g, v_w_mem_kv, v_a_w_in, v_a_w_out, v_b_w_in, v_b_v_norm_g, v_b_w_s, v_b_s_bias, v_b_w_out, v_ffn_w_up, v_ffn_conv_w, v_ffn_conv_b, v_ffn_w_down, v_final_norm_g):
    px, py, pc = _position()
    dev = 4 * px + 2 * py + pc
    core = jnp.reshape(pc, (1,)).astype(jnp.int32)
    chip = jnp.reshape(2 * px + py, (1,)).astype(jnp.int32)

    x0 = x[0]
    mem0 = mem[0]
    tgt = loss_target[0]
    S, D = x0.shape
    depth = mix_norm_g.shape[0]
    FF = ffn_w_down.shape[1] * N_DEV
    a_in = a_w_in.shape[2] * N_DEV
    b_in = b_w_in.shape[2] * N_DEV
    a_q_blk = (a_in - GW) // GW
    b_q_blk = (b_in - GW) // GW

    stacks = {"kv": (w_mem_kv, m_w_mem_kv, v_w_mem_kv), "ain": (a_w_in, m_a_w_in, v_a_w_in),
              "aout": (a_w_out, m_a_w_out, v_a_w_out), "bin": (b_w_in, m_b_w_in, v_b_w_in),
              "bout": (b_w_out, m_b_w_out, v_b_w_out), "up": (ffn_w_up, m_ffn_w_up, v_ffn_w_up),
              "down": (ffn_w_down, m_ffn_w_down, v_ffn_w_down)}
    dev1 = jnp.reshape(dev, (1,)).astype(jnp.int32)

    def groups_of(i):
        j = i // 2
        mix = [("kv", i), ("ain", j), ("aout", j)] if i % 2 == 0 else [("kv", i), ("bin", j), ("bout", j)]
        return mix, [("up", i), ("down", i)]

    gather_groups = [(f"{half}{i}", members) for i in range(depth) for half, members in zip("mf", groups_of(i))]
    gather_ahead = 2
    in_flight = {}

    def gather_start(k, after):
        gname, members = gather_groups[k]
        lands = [_cast_place(f"place_{t}{l}", dev1, stacks[t][0], l) for t, l in members]
        ssem, rsem, lands, tok = _split_start(f"ag_start_{gname}", lands, 4 * len(lands), _gather_plan, after)
        in_flight[k] = (lands, ssem, rsem)
        return tok

    small_land = _cast_place("place_small_w", dev1, _pack([ffn_conv_w, b_v_norm_g])[None], 0, F32)
    small_ssem, small_rsem, small_lands, small_tok = _split_start("smallw_start", [small_land], N_DEV - 1,
                                                                  _broadcast_plan)
    start_tokens = [small_tok, gather_start(0, [small_tok])]
    passing = {}

    def gather_arrive(k, after):
        if k >= len(gather_groups):
            return []
        gname, members = gather_groups[k]
        lands, ssem, rsem = in_flight.pop(k)
        lands = _split_wait(f"ag_wait_{gname}", lands, ssem, rsem, _gather_plan, after)
        toks = []
        for q in (range(1, 1 + gather_ahead) if k == 0 else [k + gather_ahead]):
            if q < len(gather_groups):
                toks.append(gather_start(q, [lands[0]] + toks))
        ssem, rsem, lands, tok = _split_start(f"ag_pass_{gname}", lands, 3 * len(lands), _forward_plan, toks)
        passing[k] = (lands, ssem, rsem)
        return toks + [tok]

    def gather_ready(k, after):
        gname, members = gather_groups[k]
        lands, ssem, rsem = passing.pop(k)
        lands = _split_wait(f"ag_ready_{gname}", lands, ssem, rsem, _forward_plan, after)
        out = {}
        for (t, l), land in zip(members, lands):
            if t == "kv":
                out["kv"] = land.reshape(D, 2 * GW)
            elif t in ("ain", "aout", "up"):
                out[{"ain": "in", "aout": "out", "up": "up"}[t]] = land
            elif t == "bin":
                out["in"] = jnp.transpose(land, (1, 0, 2)).reshape(D, b_in)
            elif t == "bout":
                out["out"] = land.reshape(B_W + GW, D)
            else:
                out["down"] = land.reshape(FF, D)
        return out

    def small_weights(after):
        (small_all,) = _split_wait("smallw_wait", small_lands, small_ssem, small_rsem, _broadcast_plan, after)
        cw_parts, gv_parts = [], []
        for d in range(N_DEV):
            cw_d, gv_d = _unpack(small_all[d], [ffn_conv_w.shape, b_v_norm_g.shape])
            cw_parts.append(cw_d)
            gv_parts.append(gv_d)
        return jnp.concatenate(cw_parts, axis=-1), jnp.concatenate(gv_parts, axis=-1)

    def conv_params(i):
        cw = conv_w_full[i].reshape(3, 2, FF).transpose(1, 0, 2)
        cb = ffn_conv_b[i].reshape(2, 1, FF)
        return cw, cb

    saved = []
    W = []
    xc = x0
    toks = start_tokens + gather_arrive(0, [x0])
    for i in range(depth):
        j = i // 2
        lw = gather_ready(2 * i, [xc])
        sv = {"x0": xc}
        h1, h1t = _rms_fwd(f"mixnorm{i}", xc, mix_norm_g[i], deps=toks, with_t=True)
        memn = _rms_fwd(f"memnorm{i}", mem0, mem_norm_g[i])
        if i % 2 == 0:
            proj = _mm_gcols(f"ain{i}", h1, lw["in"], out_dtype=BF16)
        else:
            proj = _mm_full(f"bin{i}", h1, lw["in"])
        toks = gather_arrive(2 * i + 1, [proj]) if i > 0 else []
        kv = _mm_full(f"kvproj{i}", memn, lw["kv"], deps=toks)
        if i % 2 == 0:
            outs, lses = [], []
            for g in range(len(A_PATTERNS)):
                o, l = _attn_fwd(f"attn{i}_{g}", proj, g)
                outs.append(o)
                lses.append(l)
            tok, lse = _attn_merge(f"merge{i}", outs, lses)
            mo = _mem_fwd(f"memattn{i}", proj, a_q_blk, kv)
            cat = jnp.concatenate([tok.astype(BF16), mo.astype(BF16)], axis=1)
            x1 = _mm_gcols(f"aout{i}", cat, lw["out"], res=xc)
            sv.update(tok=tok, lse=lse)
        else:
            bias_t = b_s_bias[j].T
            tok = _sgu_fwd(f"sgu{i}", proj, gv_full[j], b_w_s[j], bias_t)
            mo = _mem_fwd(f"memattn{i}", proj, b_q_blk, kv)
            cat = jnp.concatenate([tok.astype(BF16), mo.astype(BF16)], axis=1)
            x1 = _mm_full(f"bout{i}", cat, lw["out"], res=xc)
        toks = gather_arrive(1, [x1]) if i == 0 else []
        lw.update(gather_ready(2 * i + 1, [x1]))
        W.append(lw)
        if i == 0:
            conv_w_full, gv_full = small_weights([x1])
        h2, h2t = _rms_fwd(f"ffnnorm{i}", x1, ffn_norm_g[i], deps=toks, with_t=True)
        cw, cb = conv_params(i)
        a3 = _mm_gcols(f"up{i}", h2, lw["up"], split_out=True)
        toks = gather_arrive(2 * i + 2, [a3])
        act = _conv_fwd(f"conv{i}", a3, cw, cb)
        x2 = _mm_full(f"down{i}", act, lw["down"], res=x1, tn=1024, tk=FF // 4, deps=toks)
        toks = []
        sv.update(h1t=h1t, memn=memn, kv=kv, proj=proj, cat=cat, x1=x1, h2t=h2t, a3=a3, act=act)
        saved.append(sv)
        xc = x2

    dx, dg_final, sq = _final("final", xc, tgt, final_norm_g)
    loss_local = sq[0, 0] * (0.5 / D)

    chain = {}

    def pair_begin(gname, members, dws):
        n = len(dws)
        recvs = [lax.empty((4,) + dw.shape[1:], F32) for dw in dws]
        ssem, rsem, bufs, tok = _split_start(f"rs_pair_start_{gname}", dws + recvs, 4 * n, _pair_plan(n))
        return dict(name=gname, members=members, n=n, bufs=bufs, sems=(ssem, rsem)), tok

    def pair_end_chip_begin(st, after):
        n, gname = st["n"], st["name"]
        bufs = _split_wait(f"rs_pair_wait_{gname}", st["bufs"], *st["sems"], _pair_plan(n), after)
        ps = [_pair_sum(f"rs_sum_{t}{l}", core, bufs[w], bufs[n + w]) for w, (t, l) in enumerate(st["members"])]
        recvs = [lax.empty((3,) + p.shape[1:], BF16) for p in ps]
        ssem, rsem, bufs, tok = _split_start(f"rs_chip_start_{gname}", ps + recvs, 3 * n, _chip_plan(n))
        return dict(name=gname, members=st["members"], n=n, bufs=bufs, sems=(ssem, rsem)), tok

    def chip_end_update(st, after, deps=()):
        n = st["n"]
        bufs = _split_wait(f"rs_chip_wait_{st['name']}", st["bufs"], *st["sems"], _chip_plan(n), after)
        for w, (t, l) in enumerate(st["members"]):
            wst, mst, vst = stacks[t]
            chain[t] = _adamw_shard(f"adamw_{t}{l}", chip, l, wst, mst, vst, bufs[w], bufs[n + w], chain.get(t), deps)

    pipe = {"pair": [], "chip": [], "deps": []}

    def take_deps():
        deps, pipe["deps"] = pipe["deps"], []
        return deps

    def submit(gname, members, dws):
        st, tok = pair_begin(gname, members, dws)
        pipe["pair"].append(st)
        pipe["deps"].append(tok)

    def advance(after):
        arrived, pipe["chip"] = pipe["chip"], []
        toks = []
        for st in pipe["pair"]:
            new, tok = pair_end_chip_begin(st, [after])
            pipe["chip"].append(new)
            toks.append(tok)
        pipe["pair"] = []
        for st in arrived:
            chip_end_update(st, [after], toks)
        pipe["deps"] += toks

    def small_start(tag, arrs, after):
        land = _cast_place(f"place_small_{tag}", dev1, _pack(arrs)[None], 0, F32)
        ssem, rsem, lands, tok = _split_start(f"small_start_{tag}", [land], N_DEV - 1, _broadcast_plan, after)
        return (lands, ssem, rsem), tok

    def small_end(tag, state, shapes, after):
        lands, ssem, rsem = state
        lands = _split_wait(f"small_wait_{tag}", lands, ssem, rsem, _broadcast_plan, after)
        return _unpack(_sum_slots(f"small_sum_{tag}", lands[0]), shapes)

    def late_small():
        return [dg_mix[0], dg_ffn[0], dg_mem[0], d_conv_b[0][None], d_conv_w[0][None]]

    assert depth >= 2
    big = {k: [None] * n for k, n in (("kv", depth), ("ain", depth // 2 + depth % 2), ("aout", depth // 2 + depth % 2),
                                      ("bin", depth // 2), ("bout", depth // 2), ("up", depth), ("down", depth))}
    dg_mix, dg_ffn, dg_mem = [None] * depth, [None] * depth, [None] * depth
    d_conv_w, d_conv_b = [None] * depth, [None] * depth
    d_gv, d_ws, d_sb = [None] * (depth // 2), [None] * (depth // 2), [None] * (depth // 2)
    for i in reversed(range(depth)):
        j = i // 2
        lw, sv = W[i], saved[i]
        cw, cb = conv_params(i)
        mix_members, ffn_members = groups_of(i)
        if i == 0:
            early_arrays = [loss_local.reshape(1), dg_final.reshape(D), jnp.concatenate(dg_mix[1:]),
                            jnp.concatenate(dg_ffn[1:]), jnp.concatenate(dg_mem[1:]), jnp.stack(d_ws), jnp.stack(d_sb),
                            jnp.stack(d_gv), jnp.stack(d_conv_b[1:]), jnp.stack(d_conv_w[1:])]
            early_state, tok = small_start("early", early_arrays, [dx])
            pipe["deps"].append(tok)
        deps = take_deps()
        dact = _mm_dx_full(f"ddown{i}", dx, lw["down"], tm=1024, tko=FF // 4, tc=D, deps=deps)
        big["down"][i] = _mm_dw(f"wdown{i}", sv["act"], dx, deps=deps).reshape(N_DEV, FF // N_DEV, D)
        da3, dcw, dcb = _conv_bwd(f"dconv{i}", sv["a3"], cw, cb, dact)
        d_conv_w[i] = dcw.transpose(1, 0, 2).reshape(3, 2 * FF)
        d_conv_b[i] = dcb.reshape(2 * FF)
        advance(da3)
        deps = take_deps()
        dh2 = _mm_dx_gcols(f"dup{i}", da3, lw["up"], split_in=True, deps=deps)
        big["up"][i] = _mm_dw_gcols(f"wup{i}", sv["h2t"], da3, N_DEV, split_in=True, deps=deps, a_t=True)
        dx1, dg_ffn[i] = _rms_bwd(f"dffnnorm{i}", dh2, sv["x1"], ffn_norm_g[i], dx)
        submit(f"f{i}", ffn_members, [big["up"][i], big["down"][i]])
        deps = take_deps()
        if i % 2 == 0:
            dcat = _mm_dx_gcols(f"daout{i}", dx1, lw["out"], deps=deps)
            big["aout"][j] = _mm_dw_gcols(f"waout{i}", sv["cat"], dx1, N_DEV, deps=deps)
            parts = [None] * 9
            for g in range(len(A_PATTERNS)):
                dq, dk, dv = _attn_bwd(f"dattn{i}_{g}", sv["proj"], g, dcat, 0, sv["tok"], sv["lse"])
                parts[g], parts[3 + g], parts[6 + g] = dq, dk, dv
            dqm, dkv = _mem_bwd(f"dmemattn{i}", sv["proj"], a_q_blk, sv["kv"], dcat, 1)
            dproj = jnp.concatenate(parts + [dqm], axis=1)
            advance(dkv)
            deps = take_deps()
            dh1 = _mm_dx_gcols(f"dain{i}", dproj, lw["in"], deps=deps)
            big["ain"][j] = _mm_dw_gcols(f"wain{i}", sv["h1t"], dproj, N_DEV, deps=deps, a_t=True)
        else:
            dcat = _mm_dx_full(f"dbout{i}", dx1, lw["out"], tm=1024, deps=deps)
            big["bout"][j] = _mm_dw(f"wbout{i}", sv["cat"], dx1, deps=deps).reshape(N_DEV, (B_W + GW) // N_DEV, D)
            bias_t = b_s_bias[j].T
            du, dvp, dgv, dws, dbt = _sgu_bwd(f"dsgu{i}", sv["proj"], gv_full[j], b_w_s[j], bias_t, dcat)
            d_gv[j], d_ws[j], d_sb[j] = dgv.reshape(B_W), dws, dbt.T
            dqm, dkv = _mem_bwd(f"dmemattn{i}", sv["proj"], b_q_blk, sv["kv"], dcat, B_W // GW)
            dproj = jnp.concatenate([du, dvp, dqm], axis=1)
            advance(dkv)
            deps = take_deps()
            dh1 = _mm_dx_full(f"dbin{i}", dproj, lw["in"], tc=b_in // 2, deps=deps)
            dwin = _mm_dw(f"wbin{i}", sv["h1t"], dproj, tko=1024, tn=512, deps=deps, a_t=True)
            big["bin"][j] = dwin.reshape(D, N_DEV, b_in // N_DEV).transpose(1, 0, 2)
        big["kv"][i] = _mm_dw(f"wkv{i}", sv["memn"], dkv, tko=1024).reshape(N_DEV, D // N_DEV, 2 * GW)
        dmemn = _mm_dx_full(f"dkvproj{i}", dkv, lw["kv"], tko=1024)
        _, dg_mem[i] = _rms_bwd(f"dmemnorm{i}", dmemn, mem0, mem_norm_g[i])
        dx, dg_mix[i] = _rms_bwd(f"dmixnorm{i}", dh1, sv["x0"], mix_norm_g[i], dx1)
        submit(f"m{i}", mix_members, [big[t][l] for t, l in mix_members])
    grad_x = dx[None]

    last_chips, toks = [], []
    for st in pipe["pair"]:
        new, tok = pair_end_chip_begin(st, [dx])
        last_chips.append(new)
        toks.append(tok)
    late_state, late_tok = small_start("late", late_small(), toks)
    g_early = small_end("early", early_state, [a.shape for a in early_arrays], [dx])
    for st in pipe["chip"]:
        chip_end_update(st, [g_early[1]], [late_tok])
    g_late = small_end("late", late_state, [a.shape for a in late_small()], [chain[t][0] for t in chain])

    loss = g_early[0][0]
    layer0 = dict(zip(("mix", "ffn", "mem", "conv_b", "conv_w"), g_late))
    rest = dict(zip(("final", "mix", "ffn", "mem", "w_s", "s_bias", "gv", "conv_b", "conv_w"), g_early[1:]))
    g_cw_full = jnp.concatenate([layer0["conv_w"], rest["conv_w"]])
    g_gv = lax.dynamic_slice_in_dim(rest["gv"], dev * b_v_norm_g.shape[1], b_v_norm_g.shape[1], axis=1)
    g_cw = lax.dynamic_slice_in_dim(g_cw_full, dev * ffn_conv_w.shape[2], ffn_conv_w.shape[2], axis=2)
    g_all = [jnp.concatenate([layer0["mix"], rest["mix"]]), jnp.concatenate([layer0["ffn"], rest["ffn"]]),
             jnp.concatenate([layer0["mem"], rest["mem"]]), rest["w_s"], rest["s_bias"],
             jnp.concatenate([layer0["conv_b"], rest["conv_b"]]), rest["final"], g_gv, g_cw]
    names = ["mix_norm_g", "ffn_norm_g", "mem_norm_g", "b_w_s", "b_s_bias", "ffn_conv_b", "final_norm_g",
             "b_v_norm_g", "ffn_conv_w"]
    ws = [mix_norm_g, ffn_norm_g, mem_norm_g, b_w_s, b_s_bias, ffn_conv_b, final_norm_g, b_v_norm_g, ffn_conv_w]
    ms = [m_mix_norm_g, m_ffn_norm_g, m_mem_norm_g, m_b_w_s, m_b_s_bias, m_ffn_conv_b, m_final_norm_g,
          m_b_v_norm_g, m_ffn_conv_w]
    vs = [v_mix_norm_g, v_ffn_norm_g, v_mem_norm_g, v_b_w_s, v_b_s_bias, v_ffn_conv_b, v_final_norm_g,
          v_b_v_norm_g, v_ffn_conv_w]
    shapes = [w.shape for w in ws]
    d_p, m_p, v_p = _adamw_flat("adamw_small", _pack(ws), _pack(g_all), _pack(ms), _pack(vs))
    res = {}
    for n, g, d, nm, nv in zip(names, g_all, _unpack(d_p, shapes), _unpack(m_p, shapes), _unpack(v_p, shapes)):
        res[n] = [g, d, nm, nv]
    for st in last_chips:
        chip_end_update(st, [d_p] + [chain[t][0] for t in chain])
    for tag, name in (("kv", "w_mem_kv"), ("ain", "a_w_in"), ("aout", "a_w_out"), ("bin", "b_w_in"),
                      ("bout", "b_w_out"), ("up", "ffn_w_up"), ("down", "ffn_w_down")):
        res[name] = list(chain[tag])

    order = ["mix_norm_g", "ffn_norm_g", "mem_norm_g", "w_mem_kv", "a_w_in", "a_w_out", "b_w_in", "b_v_norm_g",
             "b_w_s", "b_s_bias", "b_w_out", "ffn_w_up", "ffn_conv_w", "ffn_conv_b", "ffn_w_down", "final_norm_g"]
    return (loss, grad_x, *[res[n][0] for n in order], *[res[n][1] for n in order],
            *[res[n][2] for n in order], *[res[n][3] for n in order])
```

```python
import functools

import numpy as np
import jax
import jax.numpy as jnp
from jax import lax
from jax.experimental import pallas as pl
from jax.experimental.pallas import tpu as pltpu

F32 = jnp.float32
BF16 = jnp.bfloat16
MESH = pl.DeviceIdType.MESH
AXES = ("x", "y", "c")
N_DEV = 8

EPS = 1e-6
NEG = -1e30
HEAD = 128
HPG = 4
GW = HPG * HEAD
A_PATTERNS = ((128, 1), (512, 4), (2048, 16))
A_HEADS = HPG * len(A_PATTERNS)
QBLK = 128
B_GROUPS = 12
B_W = B_GROUPS * HEAD
SLOPES = (2.0 ** (-8.0 * (np.arange(A_HEADS) + 1) / A_HEADS)).astype(np.float32)
SCALE = HEAD ** -0.5

ADAM_LR = 0.001
ADAM_B1 = 0.9
ADAM_B2 = 0.999
ADAM_EPS = 1e-08
ADAM_WD = 0.01
ADAM_STEP = 10

V7X_VMEM_LIMIT = 50 * 1024 * 1024

NN = (((1,), (0,)), ((), ()))
NT = (((1,), (1,)), ((), ()))
TN = (((0,), (0,)), ((), ()))


def _cp(*sem):
    return pltpu.CompilerParams(dimension_semantics=sem, vmem_limit_bytes=V7X_VMEM_LIMIT)


def _dot(a, b, dims=NN):
    return lax.dot_general(a.astype(BF16), b.astype(BF16), dims, preferred_element_type=F32)


def _tile(n, pref):
    t = min(n, pref)
    assert n % t == 0, (n, pref)
    return t


def _row_tile(rows, cols):
    best = None
    for t in range(16, rows + 1, 16):
        if rows % t == 0 and t * cols * 4 <= (1 << 20):
            best = t
    if best is None:
        best = rows
    return best


_DEP = pl.BlockSpec((8, 128), lambda *_: (0, 0))


def _matmul(name, dims, grid, a, a_spec, b, b_spec, out_shape, o_spec, tile, res=None, res_spec=None, deps=()):
    nk = grid[2]
    has_res = res is not None

    def body(*refs):
        a_ref, b_ref = refs[0], refs[1]
        r_ref = refs[2] if has_res else None
        o_ref, acc_ref = refs[-2], refs[-1]
        part = _dot(a_ref[...], b_ref[...], dims)

        def finish(val):
            if has_res:
                val = val + r_ref[...]
            o_ref[...] = val.astype(o_ref.dtype)

        if nk == 1:
            finish(part)
        else:
            k = pl.program_id(2)

            @pl.when(k == 0)
            def _():
                acc_ref[...] = part

            @pl.when(k > 0)
            def _():
                acc_ref[...] += part

            @pl.when(k == nk - 1)
            def _():
                finish(acc_ref[...])

    ins = [a, b] + ([res] if has_res else []) + list(deps)
    specs = [a_spec, b_spec] + ([res_spec] if has_res else []) + [_DEP] * len(deps)
    return pl.pallas_call(
        body, name=name, grid=grid, in_specs=specs, out_specs=o_spec, out_shape=out_shape,
        scratch_shapes=[pltpu.VMEM(tile if nk > 1 else (8, 128), F32)],
        compiler_params=_cp("parallel", "parallel", "arbitrary"))(*ins)


def _mm_full(name, a, w, res=None, tm=1024, tn=512, tk=2048, deps=()):
    M, K = a.shape
    N = w.shape[1]
    tm, tn, tk = _tile(M, tm), _tile(N, tn), _tile(K, tk)
    return _matmul(
        name, NN, (N // tn, M // tm, K // tk),
        a, pl.BlockSpec((tm, tk), lambda j, i, k: (i, k)),
        w, pl.BlockSpec((tk, tn), lambda j, i, k: (k, j)),
        jax.ShapeDtypeStruct((M, N), F32), pl.BlockSpec((tm, tn), lambda j, i, k: (i, j)), (tm, tn),
        res, pl.BlockSpec((tm, tn), lambda j, i, k: (i, j)), deps=deps)


def _mm_gcols(name, a, wg, res=None, split_out=False, tm=1024, deps=(), out_dtype=F32):
    M, K = a.shape
    G, _, Nl = wg.shape
    tm = _tile(M, tm)
    hg = G // 2
    if split_out:
        shape = jax.ShapeDtypeStruct((2, M, hg * Nl), out_dtype)
        o_spec = pl.BlockSpec((None, tm, Nl), lambda g, i, k: (g // hg, i, g % hg))
    else:
        shape = jax.ShapeDtypeStruct((M, G * Nl), out_dtype)
        o_spec = pl.BlockSpec((tm, Nl), lambda g, i, k: (i, g))
    return _matmul(
        name, NN, (G, M // tm, 1),
        a, pl.BlockSpec((tm, K), lambda g, i, k: (i, 0)),
        wg, pl.BlockSpec((None, K, Nl), lambda g, i, k: (g, 0, 0)),
        shape, o_spec, (tm, Nl),
        res, pl.BlockSpec((tm, Nl), lambda g, i, k: (i, g)), deps=deps)


def _mm_dx_full(name, dy, w, tm=512, tko=512, tc=2048, deps=()):
    M, N = dy.shape
    K = w.shape[0]
    tm, tko, tc = _tile(M, tm), _tile(K, tko), _tile(N, tc)
    return _matmul(
        name, NT, (K // tko, M // tm, N // tc),
        dy, pl.BlockSpec((tm, tc), lambda j, i, k: (i, k)),
        w, pl.BlockSpec((tko, tc), lambda j, i, k: (j, k)),
        jax.ShapeDtypeStruct((M, K), F32), pl.BlockSpec((tm, tko), lambda j, i, k: (i, j)), (tm, tko), deps=deps)


def _mm_dx_gcols(name, dy, wg, split_in=False, tm=1024, tko=1024, deps=()):
    G, K, Nl = wg.shape
    M = dy.shape[-2]
    tm, tko = _tile(M, tm), _tile(K, tko)
    hg = G // 2
    if split_in:
        dy_spec = pl.BlockSpec((None, tm, Nl), lambda j, i, g: (g // hg, i, g % hg))
    else:
        dy_spec = pl.BlockSpec((tm, Nl), lambda j, i, g: (i, g))
    return _matmul(
        name, NT, (K // tko, M // tm, G),
        dy, dy_spec,
        wg, pl.BlockSpec((None, tko, Nl), lambda j, i, g: (g, j, 0)),
        jax.ShapeDtypeStruct((M, K), F32), pl.BlockSpec((tm, tko), lambda j, i, g: (i, j)), (tm, tko), deps=deps)


def _lhs_of_dw(a, a_t, ts, tko, index):
    if a_t:
        return NN, pl.BlockSpec((tko, ts), lambda *ids: index(*ids))
    return TN, pl.BlockSpec((ts, tko), lambda *ids: index(*ids)[::-1])


def _mm_dw(name, a, dy, tko=512, tn=1024, ts=2048, deps=(), a_t=False):
    K1, S = a.shape if a_t else a.shape[::-1]
    N = dy.shape[1]
    tko, tn, ts = _tile(K1, tko), _tile(N, tn), _tile(S, ts)
    dims, a_spec = _lhs_of_dw(a, a_t, ts, tko, lambda i, j, k: (j, k))
    return _matmul(
        name, dims, (N // tn, K1 // tko, S // ts),
        a, a_spec,
        dy, pl.BlockSpec((ts, tn), lambda i, j, k: (k, i)),
        jax.ShapeDtypeStruct((K1, N), BF16), pl.BlockSpec((tko, tn), lambda i, j, k: (j, i)), (tko, tn), deps=deps)


def _mm_dw_gcols(name, a, dy, G, split_in=False, tko=1024, ts=2048, deps=(), a_t=False):
    K1, S = a.shape if a_t else a.shape[::-1]
    Nl = (dy.shape[-1] * (2 if split_in else 1)) // G
    tko, ts = _tile(K1, tko), _tile(S, ts)
    hg = G // 2
    dims, a_spec = _lhs_of_dw(a, a_t, ts, tko, lambda g, j, k: (j, k))
    if split_in:
        dy_spec = pl.BlockSpec((None, ts, Nl), lambda g, j, k: (g // hg, k, g % hg))
    else:
        dy_spec = pl.BlockSpec((ts, Nl), lambda g, j, k: (k, g))
    return _matmul(
        name, dims, (G, K1 // tko, S // ts),
        a, a_spec,
        dy, dy_spec,
        jax.ShapeDtypeStruct((G, K1, Nl), BF16), pl.BlockSpec((None, tko, Nl), lambda g, j, k: (g, j, 0)),
        (tko, Nl), deps=deps)


def _rms_fwd(name, x, g, tr=256, deps=(), with_t=False):
    S, D = x.shape
    tr = _tile(S, tr)

    def body(x_ref, g_ref, *rest):
        xf = x_ref[...]
        r = lax.rsqrt(jnp.mean(xf * xf, axis=-1, keepdims=True) + EPS)
        y = xf * r * g_ref[...]
        if with_t:
            rest[-2][...] = y.astype(BF16)
            rest[-1][...] = y.T.astype(BF16)
        else:
            rest[-1][...] = y.astype(BF16)

    row = pl.BlockSpec((tr, D), lambda i: (i, 0))
    out_specs, out_shape = row, jax.ShapeDtypeStruct((S, D), BF16)
    if with_t:
        out_specs = [row, pl.BlockSpec((D, tr), lambda i: (0, i))]
        out_shape = [out_shape, jax.ShapeDtypeStruct((D, S), BF16)]
    return pl.pallas_call(
        body, name=name, grid=(S // tr,),
        in_specs=[row, pl.BlockSpec((1, D), lambda i: (0, 0))] + [_DEP] * len(deps),
        out_specs=out_specs, out_shape=out_shape, compiler_params=_cp("parallel"))(x, g.reshape(1, D), *deps)


def _rms_bwd(name, dh, x, g, dres=None, tr=256):
    S, D = x.shape
    tr = _tile(S, tr)
    has_res = dres is not None

    def body(*refs):
        dh_ref, x_ref, g_ref = refs[:3]
        dres_ref = refs[3] if has_res else None
        dx_ref, dg_ref = refs[-2], refs[-1]
        xf = x_ref[...]
        r = lax.rsqrt(jnp.mean(xf * xf, axis=-1, keepdims=True) + EPS)
        xh = xf * r
        dhv = dh_ref[...]
        dxh = dhv * g_ref[...]
        dx = r * (dxh - xh * jnp.mean(dxh * xh, axis=-1, keepdims=True))
        if has_res:
            dx = dx + dres_ref[...]
        dx_ref[...] = dx
        part = jnp.sum(dhv * xh, axis=0, keepdims=True)
        i = pl.program_id(0)

        @pl.when(i == 0)
        def _():
            dg_ref[...] = part

        @pl.when(i > 0)
        def _():
            dg_ref[...] += part

    row = pl.BlockSpec((tr, D), lambda i: (i, 0))
    vec = pl.BlockSpec((1, D), lambda i: (0, 0))
    ins = [dh, x, g.reshape(1, D)] + ([dres] if has_res else [])
    return pl.pallas_call(
        body, name=name, grid=(S // tr,),
        in_specs=[row, row, vec] + ([row] if has_res else []),
        out_specs=[row, vec],
        out_shape=[jax.ShapeDtypeStruct((S, D), F32), jax.ShapeDtypeStruct((1, D), F32)],
        compiler_params=_cp("arbitrary"))(*ins)


def _final(name, x, tgt, g, tr=256):
    S, D = x.shape
    tr = _tile(S, tr)

    def body(x_ref, t_ref, g_ref, dx_ref, dg_ref, loss_ref):
        xf = x_ref[...]
        gv = g_ref[...]
        r = lax.rsqrt(jnp.mean(xf * xf, axis=-1, keepdims=True) + EPS)
        xh = xf * r
        err = xh * gv - t_ref[...]
        sq = jnp.sum(jnp.sum(err * err, axis=1, keepdims=True), axis=0, keepdims=True)
        dy = err * (1.0 / D)
        dxh = dy * gv
        dx_ref[...] = r * (dxh - xh * jnp.mean(dxh * xh, axis=-1, keepdims=True))
        part = jnp.sum(dy * xh, axis=0, keepdims=True)
        lpart = jnp.broadcast_to(sq, (8, 128))
        i = pl.program_id(0)

        @pl.when(i == 0)
        def _():
            dg_ref[...] = part
            loss_ref[...] = lpart

        @pl.when(i > 0)
        def _():
            dg_ref[...] += part
            loss_ref[...] += lpart

    row = pl.BlockSpec((tr, D), lambda i: (i, 0))
    vec = pl.BlockSpec((1, D), lambda i: (0, 0))
    return pl.pallas_call(
        body, name=name, grid=(S // tr,), in_specs=[row, row, vec],
        out_specs=[row, vec, pl.BlockSpec((8, 128), lambda i: (0, 0))],
        out_shape=[jax.ShapeDtypeStruct((S, D), F32), jax.ShapeDtypeStruct((1, D), F32),
                   jax.ShapeDtypeStruct((8, 128), F32)],
        compiler_params=_cp("arbitrary"))(x, tgt, g.reshape(1, D))


def _band_specs(nb, col_of):
    prev = pl.BlockSpec((QBLK, GW), lambda r, b: (jnp.maximum(b - 1, 0), col_of(r)))
    cur = pl.BlockSpec((QBLK, GW), lambda r, b: (b, col_of(r)))
    nxt = pl.BlockSpec((QBLK, GW), lambda r, b: (jnp.minimum(b + 1, nb - 1), col_of(r)))
    return [prev, cur, nxt]


def _cat3(refs, sl):
    return jnp.concatenate([ref[:, sl] for ref in refs], axis=0)


def _attn_fwd(name, proj, g):
    window, dil = A_PATTERNS[g]
    n_side = (window // 2) // dil
    S, C = proj.shape
    L = S // dil
    nb = L // QBLK
    cb = C // GW
    pv = proj.reshape(L, dil * C)
    ng = len(A_PATTERNS)

    def body(q_ref, kp, kc, kn, vp, vc, vn, o_ref, lse_ref):
        b = pl.program_id(1)
        jq = b * QBLK + lax.broadcasted_iota(jnp.int32, (QBLK, 3 * QBLK), 0)
        jk = (b - 1) * QBLK + lax.broadcasted_iota(jnp.int32, (QBLK, 3 * QBLK), 1)
        rel = jnp.abs(jk - jq)
        mask = (rel <= n_side) & (jk >= 0) & (jk < L)
        dist = rel.astype(F32) * float(dil)
        for hh in range(HPG):
            sl = slice(hh * HEAD, (hh + 1) * HEAD)
            k = _cat3((kp, kc, kn), sl)
            v = _cat3((vp, vc, vn), sl)
            s = _dot(q_ref[:, sl], k, NT) * SCALE - float(SLOPES[g * HPG + hh]) * dist
            s = jnp.where(mask, s, NEG)
            m = jnp.max(s, axis=1, keepdims=True)
            p = jnp.exp(s - m)
            l = jnp.sum(p, axis=1, keepdims=True)
            o_ref[:, sl] = _dot(p, v) / l
            lse_ref[:, sl] = jnp.broadcast_to(m + jnp.log(l), (QBLK, HEAD))

    q_spec = pl.BlockSpec((QBLK, GW), lambda r, b: (b, r * cb + g))
    k_specs = _band_specs(nb, lambda r: r * cb + ng + g)
    v_specs = _band_specs(nb, lambda r: r * cb + 2 * ng + g)
    o_spec = pl.BlockSpec((QBLK, GW), lambda r, b: (b, r))
    shape = jax.ShapeDtypeStruct((L, dil * GW), F32)
    o, lse = pl.pallas_call(
        body, name=name, grid=(dil, nb), in_specs=[q_spec] + k_specs + v_specs,
        out_specs=[o_spec, o_spec], out_shape=[shape, shape],
        compiler_params=_cp("parallel", "parallel"))(pv, pv, pv, pv, pv, pv, pv)
    return o.reshape(S, GW), lse.reshape(S, GW)


def _attn_merge(name, outs, lses, tr=256):
    S = outs[0].shape[0]
    tr = _tile(S, tr)
    ng = len(outs)

    def body(*refs):
        o_refs, l_refs = refs[:ng], refs[ng:2 * ng]
        tok_ref, lse_ref = refs[-2], refs[-1]
        ls = [r[...] for r in l_refs]
        m = functools.reduce(jnp.maximum, ls)
        es = [jnp.exp(l - m) for l in ls]
        tot = functools.reduce(lambda a, b: a + b, es)
        acc = None
        for e, o_ref in zip(es, o_refs):
            term = (e / tot) * o_ref[...]
            acc = term if acc is None else acc + term
        tok_ref[...] = acc
        lse_ref[...] = m + jnp.log(tot)

    row = pl.BlockSpec((tr, GW), lambda i: (i, 0))
    shape = jax.ShapeDtypeStruct((S, GW), F32)
    return pl.pallas_call(
        body, name=name, grid=(S // tr,), in_specs=[row] * (2 * ng), out_specs=[row, row],
        out_shape=[shape, shape], compiler_params=_cp("parallel"))(*outs, *lses)


def _attn_bwd(name, proj, g, dtok_src, dtok_blk, tok, lse):
    window, dil = A_PATTERNS[g]
    n_side = (window // 2) // dil
    S, C = proj.shape
    L = S // dil
    nb = L // QBLK
    cb = C // GW
    ng = len(A_PATTERNS)
    pv = proj.reshape(L, dil * C)
    dcb = dtok_src.shape[1] // GW
    dv_ = dtok_src.reshape(L, dil * dtok_src.shape[1])
    ov = tok.reshape(L, dil * GW)
    lv = lse.reshape(L, dil * GW)

    def body(qp, qc, qn, kp, kc, kn, vp, vc, vn, dop, doc, don, op, oc, on, lp, lc, ln,
             dq_ref, dk_ref, dv_ref):
        b = pl.program_id(1)
        jq = b * QBLK + lax.broadcasted_iota(jnp.int32, (QBLK, 3 * QBLK), 0)
        jk = (b - 1) * QBLK + lax.broadcasted_iota(jnp.int32, (QBLK, 3 * QBLK), 1)
        rel = jnp.abs(jk - jq)
        mask = (rel <= n_side) & (jk >= 0) & (jk < L)
        dist = rel.astype(F32) * float(dil)
        jq3 = (b - 1) * QBLK + lax.broadcasted_iota(jnp.int32, (3 * QBLK, QBLK), 0)
        jk1 = b * QBLK + lax.broadcasted_iota(jnp.int32, (3 * QBLK, QBLK), 1)
        rel3 = jnp.abs(jk1 - jq3)
        mask3 = (rel3 <= n_side) & (jq3 >= 0) & (jq3 < L)
        dist3 = rel3.astype(F32) * float(dil)
        for hh in range(HPG):
            sl = slice(hh * HEAD, (hh + 1) * HEAD)
            one = slice(hh * HEAD, hh * HEAD + 1)
            slope = float(SLOPES[g * HPG + hh])
            q = qc[:, sl]
            do = doc[:, sl]
            k3 = _cat3((kp, kc, kn), sl)
            v3 = _cat3((vp, vc, vn), sl)
            delta = jnp.sum(do * oc[:, sl], axis=1, keepdims=True)
            s = _dot(q, k3, NT) * SCALE - slope * dist
            p = jnp.where(mask, jnp.exp(s - lc[:, one]), 0.0)
            ds = p * (_dot(do, v3, NT) - delta)
            dq_ref[:, sl] = (_dot(ds, k3) * SCALE).astype(dq_ref.dtype)

            q3 = _cat3((qp, qc, qn), sl)
            do3 = _cat3((dop, doc, don), sl)
            o3 = _cat3((op, oc, on), sl)
            lse3 = _cat3((lp, lc, ln), sl)[:, :1]
            delta3 = jnp.sum(do3 * o3, axis=1, keepdims=True)
            k = kc[:, sl]
            v = vc[:, sl]
            s3 = _dot(q3, k, NT) * SCALE - slope * dist3
            p3 = jnp.where(mask3, jnp.exp(s3 - lse3), 0.0)
            ds3 = p3 * (_dot(do3, v, NT) - delta3)
            dv_ref[:, sl] = _dot(p3, do3, TN).astype(dv_ref.dtype)
            dk_ref[:, sl] = (_dot(ds3, q3, TN) * SCALE).astype(dk_ref.dtype)

    specs = (_band_specs(nb, lambda r: r * cb + g) + _band_specs(nb, lambda r: r * cb + ng + g)
             + _band_specs(nb, lambda r: r * cb + 2 * ng + g)
             + _band_specs(nb, lambda r: r * dcb + dtok_blk)
             + _band_specs(nb, lambda r: r) + _band_specs(nb, lambda r: r))
    o_spec = pl.BlockSpec((QBLK, GW), lambda r, b: (b, r))
    shape = jax.ShapeDtypeStruct((L, dil * GW), BF16)
    outs = pl.pallas_call(
        body, name=name, grid=(dil, nb), in_specs=specs, out_specs=[o_spec] * 3, out_shape=[shape] * 3,
        compiler_params=_cp("parallel", "parallel"))(*([pv] * 9 + [dv_] * 3 + [ov] * 3 + [lv] * 3))
    return [o.reshape(S, GW) for o in outs]


def _mem_fwd(name, proj, q_blk, kv, tq=256):
    S = proj.shape[0]
    M = kv.shape[0]
    tq = _tile(S, tq)

    def body(q_ref, kv_ref, o_ref):
        for hh in range(HPG):
            sl = slice(hh * HEAD, (hh + 1) * HEAD)
            k = kv_ref[:, sl]
            v = kv_ref[:, GW + hh * HEAD:GW + (hh + 1) * HEAD]
            s = _dot(q_ref[:, sl], k, NT) * SCALE
            m = jnp.max(s, axis=1, keepdims=True)
            p = jnp.exp(s - m)
            p = p / jnp.sum(p, axis=1, keepdims=True)
            o_ref[:, sl] = _dot(p, v)

    return pl.pallas_call(
        body, name=name, grid=(S // tq,),
        in_specs=[pl.BlockSpec((tq, GW), lambda i: (i, q_blk)), pl.BlockSpec((M, 2 * GW), lambda i: (0, 0))],
        out_specs=pl.BlockSpec((tq, GW), lambda i: (i, 0)),
        out_shape=jax.ShapeDtypeStruct((S, GW), F32), compiler_params=_cp("parallel"))(proj, kv)


def _mem_bwd(name, proj, q_blk, kv, dcat, do_blk, tq=256):
    S = proj.shape[0]
    M = kv.shape[0]
    tq = _tile(S, tq)

    def body(q_ref, kv_ref, do_ref, dq_ref, dkv_ref):
        i = pl.program_id(0)
        for hh in range(HPG):
            sl = slice(hh * HEAD, (hh + 1) * HEAD)
            vsl = slice(GW + hh * HEAD, GW + (hh + 1) * HEAD)
            q = q_ref[:, sl]
            do = do_ref[:, sl]
            k = kv_ref[:, sl]
            v = kv_ref[:, vsl]
            s = _dot(q, k, NT) * SCALE
            m = jnp.max(s, axis=1, keepdims=True)
            p = jnp.exp(s - m)
            p = p / jnp.sum(p, axis=1, keepdims=True)
            dp = _dot(do, v, NT)
            ds = p * (dp - jnp.sum(dp * p, axis=1, keepdims=True))
            dq_ref[:, sl] = (_dot(ds, k) * SCALE).astype(dq_ref.dtype)
            dk = _dot(ds, q, TN) * SCALE
            dvv = _dot(p, do, TN)

            @pl.when(i == 0)
            def _():
                dkv_ref[:, sl] = dk
                dkv_ref[:, vsl] = dvv

            @pl.when(i > 0)
            def _():
                dkv_ref[:, sl] += dk
                dkv_ref[:, vsl] += dvv

    return pl.pallas_call(
        body, name=name, grid=(S // tq,),
        in_specs=[pl.BlockSpec((tq, GW), lambda i: (i, q_blk)), pl.BlockSpec((M, 2 * GW), lambda i: (0, 0)),
                  pl.BlockSpec((tq, GW), lambda i: (i, do_blk))],
        out_specs=[pl.BlockSpec((tq, GW), lambda i: (i, 0)), pl.BlockSpec((M, 2 * GW), lambda i: (0, 0))],
        out_shape=[jax.ShapeDtypeStruct((S, GW), BF16), jax.ShapeDtypeStruct((M, 2 * GW), F32)],
        compiler_params=_cp("arbitrary"))(proj, kv, dcat)


_RSQRT2 = float(1.0 / np.sqrt(2.0))
_RSQRT2PI = float(1.0 / np.sqrt(2.0 * np.pi))


def _gelu(x):
    return 0.5 * x * (1.0 + lax.erf(x * _RSQRT2))


def _gelu_and_grad(x):
    cdf = 0.5 * (1.0 + lax.erf(x * _RSQRT2))
    return x * cdf, cdf + x * jnp.exp(-0.5 * x * x) * _RSQRT2PI


def _sgu_fwd(name, proj, gv, w_s, bias_t):
    S = proj.shape[0]
    nch = S // HEAD

    def body(u_ref, v_ref, gv_ref, ws_ref, b_ref, o_ref):
        v = _gelu(v_ref[...])
        r = lax.rsqrt(jnp.mean(v * v, axis=-1, keepdims=True) + EPS)
        vn = v * r * gv_ref[...]
        for gg in range(B_GROUPS):
            sl = slice(gg * HEAD, (gg + 1) * HEAD)
            mixed = _dot(ws_ref[gg], vn[:, sl]) + b_ref[:, gg:gg + 1]
            o_ref[:, sl] = _gelu(u_ref[:, sl]) * mixed

    return pl.pallas_call(
        body, name=name, grid=(nch,),
        in_specs=[pl.BlockSpec((HEAD, B_W), lambda c: (c, 0)), pl.BlockSpec((HEAD, B_W), lambda c: (c, 1)),
                  pl.BlockSpec((1, B_W), lambda c: (0, 0)),
                  pl.BlockSpec((B_GROUPS, HEAD, HEAD), lambda c: (0, 0, 0)),
                  pl.BlockSpec((HEAD, B_GROUPS), lambda c: (0, 0))],
        out_specs=pl.BlockSpec((HEAD, B_W), lambda c: (c, 0)),
        out_shape=jax.ShapeDtypeStruct((S, B_W), F32),
        compiler_params=_cp("parallel"))(proj, proj, gv.reshape(1, B_W), w_s, bias_t)


def _sgu_bwd(name, proj, gv, w_s, bias_t, dcat):
    S = proj.shape[0]
    nch = S // HEAD

    def body(u_ref, v_ref, gv_ref, ws_ref, b_ref, dt_ref, du_ref, dvp_ref, dgv_ref, dws_ref, db_ref, dvn_ref):
        c = pl.program_id(0)
        vpre = v_ref[...]
        v, v_slope = _gelu_and_grad(vpre)
        r = lax.rsqrt(jnp.mean(v * v, axis=-1, keepdims=True) + EPS)
        vh = v * r
        gvv = gv_ref[...]
        vn = vh * gvv
        for gg in range(B_GROUPS):
            sl = slice(gg * HEAD, (gg + 1) * HEAD)
            upre = u_ref[:, sl]
            dt = dt_ref[:, sl]
            vng = vn[:, sl]
            mixed = _dot(ws_ref[gg], vng) + b_ref[:, gg:gg + 1]
            u, u_slope = _gelu_and_grad(upre)
            du_ref[:, sl] = (dt * mixed * u_slope).astype(du_ref.dtype)
            dmix = dt * u
            dvn_ref[:, sl] = _dot(ws_ref[gg], dmix, TN)
            dws = _dot(dmix, vng, NT)
            dbs = jnp.sum(dmix, axis=1, keepdims=True)

            @pl.when(c == 0)
            def _():
                dws_ref[gg] = dws
                db_ref[:, gg:gg + 1] = dbs

            @pl.when(c > 0)
            def _():
                dws_ref[gg] += dws
                db_ref[:, gg:gg + 1] += dbs

        dvn = dvn_ref[...]
        dgp = jnp.sum(dvn * vh, axis=0, keepdims=True)
        dvh = dvn * gvv
        dv = r * (dvh - vh * jnp.mean(dvh * vh, axis=-1, keepdims=True))
        dvp_ref[...] = (dv * v_slope).astype(dvp_ref.dtype)

        @pl.when(c == 0)
        def _():
            dgv_ref[...] = dgp

        @pl.when(c > 0)
        def _():
            dgv_ref[...] += dgp

    blk = lambda j: pl.BlockSpec((HEAD, B_W), lambda c: (c, j))
    vec = pl.BlockSpec((1, B_W), lambda c: (0, 0))
    ws_spec = pl.BlockSpec((B_GROUPS, HEAD, HEAD), lambda c: (0, 0, 0))
    b_spec = pl.BlockSpec((HEAD, B_GROUPS), lambda c: (0, 0))
    du, dvp, dgv, dws, db = pl.pallas_call(
        body, name=name, grid=(nch,),
        in_specs=[blk(0), blk(1), vec, ws_spec, b_spec, blk(0)],
        out_specs=[blk(0), blk(0), vec, ws_spec, b_spec],
        out_shape=[jax.ShapeDtypeStruct((S, B_W), BF16), jax.ShapeDtypeStruct((S, B_W), BF16),
                   jax.ShapeDtypeStruct((1, B_W), F32), jax.ShapeDtypeStruct((B_GROUPS, HEAD, HEAD), F32),
                   jax.ShapeDtypeStruct((HEAD, B_GROUPS), F32)],
        scratch_shapes=[pltpu.VMEM((HEAD, B_W), F32)],
        compiler_params=_cp("arbitrary"))(proj, proj, gv.reshape(1, B_W), w_s, bias_t, dcat)
    return du, dvp, dgv, dws, db


def _shift_down(a, row):
    return jnp.where(row == 0, 0.0, pltpu.roll(a, 1, 0))


def _shift_up(a, row):
    n = a.shape[0]
    return jnp.where(row == n - 1, 0.0, pltpu.roll(a, n - 1, 0))


def _conv(a, w, b, row):
    return _shift_down(a, row) * w[0:1] + a * w[1:2] + _shift_up(a, row) * w[2:3] + b


def _conv_fwd(name, a3, cw, cb, tc=256):
    _, S, FF = a3.shape
    tc = _tile(FF, tc)

    def body(a_ref, w_ref, b_ref, o_ref):
        row = lax.broadcasted_iota(jnp.int32, (S, tc), 0)
        cg = _conv(a_ref[0], w_ref[0], b_ref[0], row)
        cv = _conv(a_ref[1], w_ref[1], b_ref[1], row)
        o_ref[...] = (_gelu(cg) * cv).astype(o_ref.dtype)

    return pl.pallas_call(
        body, name=name, grid=(FF // tc,),
        in_specs=[pl.BlockSpec((2, S, tc), lambda j: (0, 0, j)), pl.BlockSpec((2, 3, tc), lambda j: (0, 0, j)),
                  pl.BlockSpec((2, 1, tc), lambda j: (0, 0, j))],
        out_specs=pl.BlockSpec((S, tc), lambda j: (0, j)),
        out_shape=jax.ShapeDtypeStruct((S, FF), BF16), compiler_params=_cp("parallel"))(a3, cw, cb)


def _conv_bwd(name, a3, cw, cb, dact, tc=128):
    _, S, FF = a3.shape
    tc = _tile(FF, tc)

    def body(a_ref, w_ref, b_ref, d_ref, da_ref, dw_ref, db_ref):
        row = lax.broadcasted_iota(jnp.int32, (S, tc), 0)
        ag, av = a_ref[0], a_ref[1]
        wg, wv = w_ref[0], w_ref[1]
        cg = _conv(ag, wg, b_ref[0], row)
        cv = _conv(av, wv, b_ref[1], row)
        d = d_ref[...]
        gate, gate_slope = _gelu_and_grad(cg)
        dcs = (d * cv * gate_slope, d * gate)
        for h, (dc, a, w) in enumerate(zip(dcs, (ag, av), (wg, wv))):
            da = _shift_up(dc, row) * w[0:1] + dc * w[1:2] + _shift_down(dc, row) * w[2:3]
            da_ref[h] = da.astype(da_ref.dtype)
            dw_ref[h, 0:1, :] = jnp.sum(dc * _shift_down(a, row), axis=0, keepdims=True)
            dw_ref[h, 1:2, :] = jnp.sum(dc * a, axis=0, keepdims=True)
            dw_ref[h, 2:3, :] = jnp.sum(dc * _shift_up(a, row), axis=0, keepdims=True)
            db_ref[h] = jnp.sum(dc, axis=0, keepdims=True)

    a_spec = pl.BlockSpec((2, S, tc), lambda j: (0, 0, j))
    w_spec = pl.BlockSpec((2, 3, tc), lambda j: (0, 0, j))
    b_spec = pl.BlockSpec((2, 1, tc), lambda j: (0, 0, j))
    return pl.pallas_call(
        body, name=name, grid=(FF // tc,),
        in_specs=[a_spec, w_spec, b_spec, pl.BlockSpec((S, tc), lambda j: (0, j))],
        out_specs=[a_spec, w_spec, b_spec],
        out_shape=[jax.ShapeDtypeStruct((2, S, FF), BF16), jax.ShapeDtypeStruct((2, 3, FF), F32),
                   jax.ShapeDtypeStruct((2, 1, FF), F32)],
        compiler_params=_cp("parallel"))(a3, cw, cb, dact)


_HBM = pl.BlockSpec(memory_space=pltpu.HBM)


def _position():
    return lax.axis_index("x"), lax.axis_index("y"), lax.axis_index("c")


_SEM =pl.BlockSpec(memory_space=pltpu.SEMAPHORE)
_EFFECT = pltpu.SideEffectType.DATAFLOW_SIDE_EFFECTING
_FLIPS = ((1, 0), (0, 1), (1, 1))


def _split_start(name, bufs, ncopy, plan, after=()):
    n = len(bufs)
    after = list(after)

    def body(*refs):
        ins = refs[:n]
        send_sems, recv_sems, token = refs[n + len(after)], refs[n + len(after) + 1], refs[-1]
        for i, (src, dst, to) in enumerate(plan(ins)):
            pltpu.make_async_remote_copy(src_ref=src, dst_ref=dst, send_sem=send_sems.at[i], recv_sem=recv_sems.at[i],
                                         device_id=to, device_id_type=MESH).start()
        token[...] = jnp.zeros_like(token)

    outs = pl.pallas_call(
        body, name=name,
        out_shape=(pltpu.SemaphoreType.DMA((ncopy,)), pltpu.SemaphoreType.DMA((ncopy,)),
                   *[pltpu.HBM(b.shape, b.dtype) for b in bufs], jax.ShapeDtypeStruct((8, 128), F32)),
        in_specs=[_HBM] * n + [pl.BlockSpec(memory_space=pl.ANY)] * len(after),
        out_specs=(_SEM, _SEM, *([_HBM] * n), pl.BlockSpec(memory_space=pltpu.VMEM)),
        input_output_aliases={i: 2 + i for i in range(n)},
        compiler_params=pltpu.CompilerParams(has_side_effects=_EFFECT),
    )(*[pltpu.with_memory_space_constraint(b, pltpu.HBM) for b in bufs], *after)
    return outs[0], outs[1], list(outs[2:2 + n]), outs[-1]


def _split_wait(name, bufs, send_sems, recv_sems, plan, after):
    n = len(bufs)
    after = list(after)

    def body(*refs):
        ins = refs[:n]
        ssem, rsem = refs[n], refs[n + 1]
        for i, (src, dst, to) in enumerate(plan(ins)):
            cp = pltpu.make_async_remote_copy(src_ref=src, dst_ref=dst, send_sem=ssem.at[i], recv_sem=rsem.at[i],
                                              device_id=to, device_id_type=MESH)
            cp.wait_send()
            cp.wait_recv()

    outs = pl.pallas_call(
        body, name=name, out_shape=tuple(pltpu.HBM(b.shape, b.dtype) for b in bufs),
        in_specs=[_HBM] * n + [_SEM, _SEM] + [pl.BlockSpec(memory_space=pl.ANY)] * len(after),
        out_specs=tuple([_HBM] * n), input_output_aliases={i: i for i in range(n)},
        compiler_params=pltpu.CompilerParams(has_side_effects=_EFFECT),
    )(*bufs, send_sems, recv_sems, *after)
    return list(outs)


def _gather_plan(refs):
    px, py, pc = _position()
    me = 4 * px + 2 * py + pc
    targets = [(px, py, 1 - pc), (1 - px, py, pc), (px, 1 - py, pc), (1 - px, 1 - py, pc)]
    return [(r.at[me], r.at[me], to) for r in refs for to in targets]


def _forward_plan(refs):
    px, py, pc = _position()
    out = []
    for r in refs:
        for fx, fy in _FLIPS:
            slot = 4 * (1 - px if fx else px) + 2 * (1 - py if fy else py) + pc
            out.append((r.at[slot], r.at[slot], (px, py, 1 - pc)))
    return out


def _pair_plan(n):
    def plan(refs):
        px, py, pc = _position()
        return [(refs[w].at[2 * k + (1 - pc)], refs[n + w].at[k], (px, py, 1 - pc)) for w in range(n) for k in range(4)]
    return plan


def _chip_plan(n):
    def plan(refs):
        px, py, pc = _position()
        out = []
        for w in range(n):
            for j, (fx, fy) in enumerate(_FLIPS):
                qx = 1 - px if fx else px
                qy = 1 - py if fy else py
                out.append((refs[w].at[2 * qx + qy], refs[n + w].at[j], (qx, qy, pc)))
        return out
    return plan


def _broadcast_plan(refs):
    px, py, pc = _position()
    me = 4 * px + 2 * py + pc
    flips = [(fx, fy, fc) for fx in (0, 1) for fy in (0, 1) for fc in (0, 1)][1:]
    targets = [(1 - px if fx else px, 1 - py if fy else py, 1 - pc if fc else pc) for fx, fy, fc in flips]
    return [(r.at[me], r.at[me], to) for r in refs for to in targets]


def _cast_place(name, dev, w, layer, dtype=BF16):
    nl, R, C = w.shape
    tr = _row_tile(R, C)

    def body(dev_ref, w_ref, o_ref):
        o_ref[...] = w_ref[...].astype(o_ref.dtype)

    return pl.pallas_call(
        body, name=name,
        grid_spec=pltpu.PrefetchScalarGridSpec(
            num_scalar_prefetch=1, grid=(R // tr,),
            in_specs=[pl.BlockSpec((None, tr, C), lambda i, d: (layer, i, 0))],
            out_specs=pl.BlockSpec((None, tr, C), lambda i, d: (d[0], i, 0))),
        out_shape=jax.ShapeDtypeStruct((N_DEV, R, C), dtype), compiler_params=_cp("parallel"))(dev, w)


def _pair_sum(name, core, dw, recv):
    _, R, C = dw.shape
    tr = _row_tile(R, C)
    dw4 = dw.reshape(4, 2, R, C)

    def body(core_ref, a_ref, b_ref, o_ref):
        o_ref[...] = (a_ref[...].astype(F32) + b_ref[...].astype(F32)).astype(o_ref.dtype)

    return pl.pallas_call(
        body, name=name,
        grid_spec=pltpu.PrefetchScalarGridSpec(
            num_scalar_prefetch=1, grid=(4, R // tr),
            in_specs=[pl.BlockSpec((None, None, tr, C), lambda k, i, c_ref: (k, c_ref[0], i, 0)),
                      pl.BlockSpec((None, tr, C), lambda k, i, c_ref: (k, i, 0))],
            out_specs=pl.BlockSpec((None, tr, C), lambda k, i, c_ref: (k, i, 0))),
        out_shape=jax.ShapeDtypeStruct((4, R, C), BF16),
        compiler_params=_cp("parallel", "parallel"))(core, dw4, recv)


def _adamw_math(w, g, m, v):
    m = ADAM_B1 * m + (1.0 - ADAM_B1) * g
    v = ADAM_B2 * v + (1.0 - ADAM_B2) * (g * g)
    m_hat = m / (1.0 - ADAM_B1 ** ADAM_STEP)
    v_hat = v / (1.0 - ADAM_B2 ** ADAM_STEP)
    delta = -ADAM_LR * (m_hat / (jnp.sqrt(v_hat) + ADAM_EPS) + ADAM_WD * w)
    return delta, m, v


def _adamw_shard(name, chip, layer, w, m, v, p, recv, prev, deps=()):
    nl, R, C = w.shape
    tr = _row_tile(R, C)
    n_prev = 0 if prev is None else 4

    def body(chip_ref, w_ref, m_ref, v_ref, p_ref, r_ref, *rest):
        g_ref, d_ref, nm_ref, nv_ref, tok_ref = rest[-5:]
        tok_ref[...] = jnp.zeros_like(tok_ref)
        g = p_ref[...].astype(F32)
        for j in range(3):
            g = g + r_ref[j].astype(F32)
        delta, nm, nv = _adamw_math(w_ref[...], g, m_ref[...], v_ref[...])
        g_ref[...] = g
        d_ref[...] = delta
        nm_ref[...] = nm
        nv_ref[...] = nv

    lay = pl.BlockSpec((None, tr, C), lambda i, c_ref: (layer, i, 0))
    in_specs = [lay, lay, lay,
                pl.BlockSpec((None, tr, C), lambda i, c_ref: (c_ref[0], i, 0)),
                pl.BlockSpec((3, tr, C), lambda i, c_ref: (0, i, 0))]
    in_specs += [pl.BlockSpec(memory_space=pl.ANY)] * n_prev + [_DEP] * len(deps)
    shape = jax.ShapeDtypeStruct((nl, R, C), F32)
    ins = [chip, w, m, v, p, recv] + ([] if prev is None else list(prev)) + list(deps)
    return pl.pallas_call(
        body, name=name,
        grid_spec=pltpu.PrefetchScalarGridSpec(
            num_scalar_prefetch=1, grid=(R // tr,), in_specs=in_specs, out_specs=[lay] * 4 + [_DEP]),
        out_shape=[shape] * 4 + [jax.ShapeDtypeStruct((8, 128), F32)],
        input_output_aliases={6 + j: j for j in range(n_prev)},
        compiler_params=_cp("arbitrary"))(*ins)


def _sum_slots(name, parts, tr=512):
    n, R, C = parts.shape
    tr = _tile(R, tr)

    def body(p_ref, o_ref):
        acc = p_ref[0]
        for j in range(1, n):
            acc = acc + p_ref[j]
        o_ref[...] = acc

    return pl.pallas_call(
        body, name=name, grid=(R // tr,),
        in_specs=[pl.BlockSpec((n, tr, C), lambda i: (0, i, 0))],
        out_specs=pl.BlockSpec((tr, C), lambda i: (i, 0)),
        out_shape=jax.ShapeDtypeStruct((R, C), F32), compiler_params=_cp("parallel"))(parts)


def _adamw_flat(name, w, g, m, v, tr=512):
    R, C = w.shape
    tr = _tile(R, tr)

    def body(w_ref, g_ref, m_ref, v_ref, d_ref, nm_ref, nv_ref):
        delta, nm, nv = _adamw_math(w_ref[...], g_ref[...], m_ref[...], v_ref[...])
        d_ref[...] = delta
        nm_ref[...] = nm
        nv_ref[...] = nv

    row = pl.BlockSpec((tr, C), lambda i: (i, 0))
    shape = jax.ShapeDtypeStruct((R, C), F32)
    return pl.pallas_call(
        body, name=name, grid=(R // tr,), in_specs=[row] * 4, out_specs=[row] * 3, out_shape=[shape] * 3,
        compiler_params=_cp("parallel"))(w, g, m, v)


_PACK_ROWS = 512


def _pack(arrs):
    flat = jnp.concatenate([a.reshape(-1) for a in arrs])
    unit = _PACK_ROWS * 128
    pad = (-flat.shape[0]) % unit
    return jnp.pad(flat, (0, pad)).reshape(-1, 128)


def _unpack(packed, shapes):
    flat = packed.reshape(-1)
    outs, off = [], 0
    for s in shapes:
        n = int(np.prod(s))
        outs.append(flat[off:off + n].reshape(s))
        off += n
    return outs


def kernel(x, mem, mix_norm_g, ffn_norm_g, mem_norm_g, w_mem_kv, a_w_in, a_w_out, b_w_in, b_v_norm_g, b_w_s, b_s_bias, b_w_out, ffn_w_up, ffn_conv_w, ffn_conv_b, ffn_w_down, final_norm_g, loss_target, m_mix_norm_g, m_ffn_norm_g, m_mem_norm_g, m_w_mem_kv, m_a_w_in, m_a_w_out, m_b_w_in, m_b_v_norm_g, m_b_w_s, m_b_s_bias, m_b_w_out, m_ffn_w_up, m_ffn_conv_w, m_ffn_conv_b, m_ffn_w_down, m_final_norm_g, v_mix_norm_g, v_ffn_norm_g, v_mem_norm_g, v_w_mem_kv, v_a_w_in, v_a_w_out, v_b_w_in, v_b_v_norm_g, v_b_w_s, v_b_s_bias, v_b_w_out, v_ffn_w_up, v_ffn_conv_w, v_ffn_conv_b, v_ffn_w_down, v_final_norm_g):
    px, py, pc = _position()
    dev = 4 * px + 2 * py + pc
    core = jnp.reshape(pc, (1,)).astype(jnp.int32)
    chip = jnp.reshape(2 * px + py, (1,)).astype(jnp.int32)

    x0 = x[0]
    mem0 = mem[0]
    tgt = loss_target[0]
    S, D = x0.shape
    depth = mix_norm_g.shape[0]
    FF = ffn_w_down.shape[1] * N_DEV
    a_in = a_w_in.shape[2] * N_DEV
    b_in = b_w_in.shape[2] * N_DEV
    a_q_blk = (a_in - GW) // GW
    b_q_blk = (b_in - GW) // GW

    stacks = {"kv": (w_mem_kv, m_w_mem_kv, v_w_mem_kv), "ain": (a_w_in, m_a_w_in, v_a_w_in),
              "aout": (a_w_out, m_a_w_out, v_a_w_out), "bin": (b_w_in, m_b_w_in, v_b_w_in),
              "bout": (b_w_out, m_b_w_out, v_b_w_out), "up": (ffn_w_up, m_ffn_w_up, v_ffn_w_up),
              "down": (ffn_w_down, m_ffn_w_down, v_ffn_w_down)}
    dev1 = jnp.reshape(dev, (1,)).astype(jnp.int32)

    def groups_of(i):
        j = i // 2
        mix = [("kv", i), ("ain", j), ("aout", j)] if i % 2 == 0 else [("kv", i), ("bin", j), ("bout", j)]
        return mix, [("up", i), ("down", i)]

    gather_groups = [(f"{half}{i}", members) for i in range(depth) for half, members in zip("mf", groups_of(i))]
    gather_ahead = 2
    in_flight = {}

    def gather_start(k, after):
        gname, members = gather_groups[k]
        lands = [_cast_place(f"place_{t}{l}", dev1, stacks[t][0], l) for t, l in members]
        ssem, rsem, lands, tok = _split_start(f"ag_start_{gname}", lands, 4 * len(lands), _gather_plan, after)
        in_flight[k] = (lands, ssem, rsem)
        return tok

    small_land = _cast_place("place_small_w", dev1, _pack([ffn_conv_w, b_v_norm_g])[None], 0, F32)
    small_ssem, small_rsem, small_lands, small_tok = _split_start("smallw_start", [small_land], N_DEV - 1,
                                                                  _broadcast_plan)
    start_tokens = [small_tok, gather_start(0, [small_tok])]
    passing = {}

    def gather_arrive(k, after):
        if k >= len(gather_groups):
            return []
        gname, members = gather_groups[k]
        lands, ssem, rsem = in_flight.pop(k)
        lands = _split_wait(f"ag_wait_{gname}", lands, ssem, rsem, _gather_plan, after)
        toks = []
        for q in (range(1, 1 + gather_ahead) if k == 0 else [k + gather_ahead]):
            if q < len(gather_groups):
                toks.append(gather_start(q, [lands[0]] + toks))
        ssem, rsem, lands, tok = _split_start(f"ag_pass_{gname}", lands, 3 * len(lands), _forward_plan, toks)
        passing[k] = (lands, ssem, rsem)
        return toks + [tok]

    def gather_ready(k, after):
        gname, members = gather_groups[k]
        lands, ssem, rsem = passing.pop(k)
        lands = _split_wait(f"ag_ready_{gname}", lands, ssem, rsem, _forward_plan, after)
        out = {}
        for (t, l), land in zip(members, lands):
            if t == "kv":
                out["kv"] = land.reshape(D, 2 * GW)
            elif t in ("ain", "aout", "up"):
                out[{"ain": "in", "aout": "out", "up": "up"}[t]] = land
            elif t == "bin":
                out["in"] = jnp.transpose(land, (1, 0, 2)).reshape(D, b_in)
            elif t == "bout":
                out["out"] = land.reshape(B_W + GW, D)
            else:
                out["down"] = land.reshape(FF, D)
        return out

    def small_weights(after):
        (small_all,) = _split_wait("smallw_wait", small_lands, small_ssem, small_rsem, _broadcast_plan, after)
        cw_parts, gv_parts = [], []
        for d in range(N_DEV):
            cw_d, gv_d = _unpack(small_all[d], [ffn_conv_w.shape, b_v_norm_g.shape])
            cw_parts.append(cw_d)
            gv_parts.append(gv_d)
        return jnp.concatenate(cw_parts, axis=-1), jnp.concatenate(gv_parts, axis=-1)

    def conv_params(i):
        cw = conv_w_full[i].reshape(3, 2, FF).transpose(1, 0, 2)
        cb = ffn_conv_b[i].reshape(2, 1, FF)
        return cw, cb

    saved = []
    W = []
    xc = x0
    toks = start_tokens + gather_arrive(0, [x0])
    for i in range(depth):
        j = i // 2
        lw = gather_ready(2 * i, [xc])
        sv = {"x0": xc}
        h1, h1t = _rms_fwd(f"mixnorm{i}", xc, mix_norm_g[i], deps=toks, with_t=True)
        memn = _rms_fwd(f"memnorm{i}", mem0, mem_norm_g[i])
        if i % 2 == 0:
            proj = _mm_gcols(f"ain{i}", h1, lw["in"], out_dtype=BF16)
        else:
            proj = _mm_full(f"bin{i}", h1, lw["in"])
        toks = gather_arrive(2 * i + 1, [proj]) if i > 0 else []
        kv = _mm_full(f"kvproj{i}", memn, lw["kv"], deps=toks)
        if i % 2 == 0:
            outs, lses = [], []
            for g in range(len(A_PATTERNS)):
                o, l = _attn_fwd(f"attn{i}_{g}", proj, g)
                outs.append(o)
                lses.append(l)
            tok, lse = _attn_merge(f"merge{i}", outs, lses)
            mo = _mem_fwd(f"memattn{i}", proj, a_q_blk, kv)
            cat = jnp.concatenate([tok.astype(BF16), mo.astype(BF16)], axis=1)
            x1 = _mm_gcols(f"aout{i}", cat, lw["out"], res=xc)
            sv.update(tok=tok, lse=lse)
        else:
            bias_t = b_s_bias[j].T
            tok = _sgu_fwd(f"sgu{i}", proj, gv_full[j], b_w_s[j], bias_t)
            mo = _mem_fwd(f"memattn{i}", proj, b_q_blk, kv)
            cat = jnp.concatenate([tok.astype(BF16), mo.astype(BF16)], axis=1)
            x1 = _mm_full(f"bout{i}", cat, lw["out"], res=xc)
        toks = gather_arrive(1, [x1]) if i == 0 else []
        lw.update(gather_ready(2 * i + 1, [x1]))
        W.append(lw)
        if i == 0:
            conv_w_full, gv_full = small_weights([x1])
        h2, h2t = _rms_fwd(f"ffnnorm{i}", x1, ffn_norm_g[i], deps=toks, with_t=True)
        cw, cb = conv_params(i)
        a3 = _mm_gcols(f"up{i}", h2, lw["up"], split_out=True)
        toks = gather_arrive(2 * i + 2, [a3])
        act = _conv_fwd(f"conv{i}", a3, cw, cb)
        x2 = _mm_full(f"down{i}", act, lw["down"], res=x1, tn=1024, tk=FF // 4, deps=toks)
        toks = []
        sv.update(h1t=h1t, memn=memn, kv=kv, proj=proj, cat=cat, x1=x1, h2t=h2t, a3=a3, act=act)
        saved.append(sv)
        xc = x2

    dx, dg_final, sq = _final("final", xc, tgt, final_norm_g)
    loss_local = sq[0, 0] * (0.5 / D)

    chain = {}
    adamw_tokens = []

    def pair_begin(gname, members, dws):
        n = len(dws)
        recvs = [lax.empty((4,) + dw.shape[1:], dw.dtype) for dw in dws]
        ssem, rsem, bufs, tok = _split_start(f"rs_pair_start_{gname}", dws + recvs, 4 * n, _pair_plan(n))
        return dict(name=gname, members=members, n=n, bufs=bufs, sems=(ssem, rsem)), tok

    def pair_end_chip_begin(st, after):
        n, gname = st["n"], st["name"]
        bufs = _split_wait(f"rs_pair_wait_{gname}", st["bufs"], *st["sems"], _pair_plan(n), after)
        ps = [_pair_sum(f"rs_sum_{t}{l}", core, bufs[w], bufs[n + w]) for w, (t, l) in enumerate(st["members"])]
        recvs = [lax.empty((3,) + p.shape[1:], BF16) for p in ps]
        ssem, rsem, bufs, tok = _split_start(f"rs_chip_start_{gname}", ps + recvs, 3 * n, _chip_plan(n))
        return dict(name=gname, members=st["members"], n=n, bufs=bufs, sems=(ssem, rsem)), tok

    def chip_end_update(st, after, deps=()):
        n = st["n"]
        bufs = _split_wait(f"rs_chip_wait_{st['name']}", st["bufs"], *st["sems"], _chip_plan(n), after)
        for w, (t, l) in enumerate(st["members"]):
            wst, mst, vst = stacks[t]
            *chain[t], tok = _adamw_shard(f"adamw_{t}{l}", chip, l, wst, mst, vst, bufs[w], bufs[n + w], chain.get(t),
                                          deps)
            adamw_tokens.append(tok)

    pipe = {"pair": [], "chip": [], "deps": []}

    def take_deps():
        deps, pipe["deps"] = pipe["deps"], []
        return deps

    def submit(gname, members, dws):
        st, tok = pair_begin(gname, members, dws)
        pipe["pair"].append(st)
        pipe["deps"].append(tok)

    def advance(after):
        arrived, pipe["chip"] = pipe["chip"], []
        toks = []
        for st in pipe["pair"]:
            new, tok = pair_end_chip_begin(st, [after])
            pipe["chip"].append(new)
            toks.append(tok)
        pipe["pair"] = []
        for st in arrived:
            chip_end_update(st, [after], toks)
        pipe["deps"] += toks

    def small_start(tag, arrs, after):
        land = _cast_place(f"place_small_{tag}", dev1, _pack(arrs)[None], 0, F32)
        ssem, rsem, lands, tok = _split_start(f"small_start_{tag}", [land], N_DEV - 1, _broadcast_plan, after)
        return (lands, ssem, rsem), tok

    def small_end(tag, state, shapes, after):
        lands, ssem, rsem = state
        lands = _split_wait(f"small_wait_{tag}", lands, ssem, rsem, _broadcast_plan, after)
        return _unpack(_sum_slots(f"small_sum_{tag}", lands[0]), shapes)

    def late_small():
        return [dg_mix[0], dg_ffn[0], dg_mem[0], d_conv_b[0][None], d_conv_w[0][None]]

    assert depth >= 2
    big = {k: [None] * n for k, n in (("kv", depth), ("ain", depth // 2 + depth % 2), ("aout", depth // 2 + depth % 2),
                                      ("bin", depth // 2), ("bout", depth // 2), ("up", depth), ("down", depth))}
    dg_mix, dg_ffn, dg_mem = [None] * depth, [None] * depth, [None] * depth
    d_conv_w, d_conv_b = [None] * depth, [None] * depth
    d_gv, d_ws, d_sb = [None] * (depth // 2), [None] * (depth // 2), [None] * (depth // 2)
    for i in reversed(range(depth)):
        j = i // 2
        lw, sv = W[i], saved[i]
        cw, cb = conv_params(i)
        mix_members, ffn_members = groups_of(i)
        if i == 0:
            early_arrays = [loss_local.reshape(1), dg_final.reshape(D), jnp.concatenate(dg_mix[1:]),
                            jnp.concatenate(dg_ffn[1:]), jnp.concatenate(dg_mem[1:]), jnp.stack(d_ws), jnp.stack(d_sb),
                            jnp.stack(d_gv), jnp.stack(d_conv_b[1:]), jnp.stack(d_conv_w[1:])]
            early_state, tok = small_start("early", early_arrays, [dx])
            pipe["deps"].append(tok)
        deps = take_deps()
        dact = _mm_dx_full(f"ddown{i}", dx, lw["down"], tm=1024, tko=FF // 4, tc=D, deps=deps)
        big["down"][i] = _mm_dw(f"wdown{i}", sv["act"], dx, deps=deps).reshape(N_DEV, FF // N_DEV, D)
        da3, dcw, dcb = _conv_bwd(f"dconv{i}", sv["a3"], cw, cb, dact)
        d_conv_w[i] = dcw.transpose(1, 0, 2).reshape(3, 2 * FF)
        d_conv_b[i] = dcb.reshape(2 * FF)
        advance(da3)
        deps = take_deps()
        dh2 = _mm_dx_gcols(f"dup{i}", da3, lw["up"], split_in=True, deps=deps)
        big["up"][i] = _mm_dw_gcols(f"wup{i}", sv["h2t"], da3, N_DEV, split_in=True, deps=deps, a_t=True)
        dx1, dg_ffn[i] = _rms_bwd(f"dffnnorm{i}", dh2, sv["x1"], ffn_norm_g[i], dx)
        submit(f"f{i}", ffn_members, [big["up"][i], big["down"][i]])
        deps = take_deps()
        if i % 2 == 0:
            dcat = _mm_dx_gcols(f"daout{i}", dx1, lw["out"], deps=deps)
            big["aout"][j] = _mm_dw_gcols(f"waout{i}", sv["cat"], dx1, N_DEV, deps=deps)
            parts = [None] * 9
            for g in range(len(A_PATTERNS)):
                dq, dk, dv = _attn_bwd(f"dattn{i}_{g}", sv["proj"], g, dcat, 0, sv["tok"], sv["lse"])
                parts[g], parts[3 + g], parts[6 + g] = dq, dk, dv
            dqm, dkv = _mem_bwd(f"dmemattn{i}", sv["proj"], a_q_blk, sv["kv"], dcat, 1)
            dproj = jnp.concatenate(parts + [dqm], axis=1)
            advance(dkv)
            deps = take_deps()
            dh1 = _mm_dx_gcols(f"dain{i}", dproj, lw["in"], deps=deps)
        else:
            dcat = _mm_dx_full(f"dbout{i}", dx1, lw["out"], tm=1024, deps=deps)
            big["bout"][j] = _mm_dw(f"wbout{i}", sv["cat"], dx1, deps=deps).reshape(N_DEV, (B_W + GW) // N_DEV, D)
            bias_t = b_s_bias[j].T
            du, dvp, dgv, dws, dbt = _sgu_bwd(f"dsgu{i}", sv["proj"], gv_full[j], b_w_s[j], bias_t, dcat)
            d_gv[j], d_ws[j], d_sb[j] = dgv.reshape(B_W), dws, dbt.T
            dqm, dkv = _mem_bwd(f"dmemattn{i}", sv["proj"], b_q_blk, sv["kv"], dcat, B_W // GW)
            dproj = jnp.concatenate([du, dvp, dqm], axis=1)
            advance(dkv)
            deps = take_deps()
            dh1 = _mm_dx_full(f"dbin{i}", dproj, lw["in"], tc=b_in // 2, deps=deps)
        dmemn = _mm_dx_full(f"dkvproj{i}", dkv, lw["kv"], tko=1024)
        _, dg_mem[i] = _rms_bwd(f"dmemnorm{i}", dmemn, mem0, mem_norm_g[i])
        dx, dg_mix[i] = _rms_bwd(f"dmixnorm{i}", dh1, sv["x0"], mix_norm_g[i], dx1)
        if i == 0:
            late_state, late_tok = small_start("late", late_small(), [dx])
            deps = deps + [late_tok]
        if i % 2 == 0:
            big["ain"][j] = _mm_dw_gcols(f"wain{i}", sv["h1t"], dproj, N_DEV, deps=deps, a_t=True)
        else:
            dwin = _mm_dw(f"wbin{i}", sv["h1t"], dproj, tko=1024, tn=512, deps=deps, a_t=True)
            big["bin"][j] = dwin.reshape(D, N_DEV, b_in // N_DEV).transpose(1, 0, 2)
        big["kv"][i] = _mm_dw(f"wkv{i}", sv["memn"], dkv, tko=1024, deps=deps).reshape(N_DEV, D // N_DEV, 2 * GW)
        submit(f"m{i}", mix_members, [big[t][l] for t, l in mix_members])
    grad_x = dx[None]

    last_chips, toks = [], []
    for st in pipe["pair"]:
        new, tok = pair_end_chip_begin(st, [dx])
        last_chips.append(new)
        toks.append(tok)
    g_early = small_end("early", early_state, [a.shape for a in early_arrays], [dx])
    for st in pipe["chip"]:
        chip_end_update(st, [g_early[1]], toks)
    g_late = small_end("late", late_state, [a.shape for a in late_small()], list(adamw_tokens))

    loss = g_early[0][0]
    layer0 = dict(zip(("mix", "ffn", "mem", "conv_b", "conv_w"), g_late))
    rest = dict(zip(("final", "mix", "ffn", "mem", "w_s", "s_bias", "gv", "conv_b", "conv_w"), g_early[1:]))
    g_cw_full = jnp.concatenate([layer0["conv_w"], rest["conv_w"]])
    g_gv = lax.dynamic_slice_in_dim(rest["gv"], dev * b_v_norm_g.shape[1], b_v_norm_g.shape[1], axis=1)
    g_cw = lax.dynamic_slice_in_dim(g_cw_full, dev * ffn_conv_w.shape[2], ffn_conv_w.shape[2], axis=2)
    g_all = [jnp.concatenate([layer0["mix"], rest["mix"]]), jnp.concatenate([layer0["ffn"], rest["ffn"]]),
             jnp.concatenate([layer0["mem"], rest["mem"]]), rest["w_s"], rest["s_bias"],
             jnp.concatenate([layer0["conv_b"], rest["conv_b"]]), rest["final"], g_gv, g_cw]
    names = ["mix_norm_g", "ffn_norm_g", "mem_norm_g", "b_w_s", "b_s_bias", "ffn_conv_b", "final_norm_g",
             "b_v_norm_g", "ffn_conv_w"]
    ws = [mix_norm_g, ffn_norm_g, mem_norm_g, b_w_s, b_s_bias, ffn_conv_b, final_norm_g, b_v_norm_g, ffn_conv_w]
    ms = [m_mix_norm_g, m_ffn_norm_g, m_mem_norm_g, m_b_w_s, m_b_s_bias, m_ffn_conv_b, m_final_norm_g,
          m_b_v_norm_g, m_ffn_conv_w]
    vs = [v_mix_norm_g, v_ffn_norm_g, v_mem_norm_g, v_b_w_s, v_b_s_bias, v_ffn_conv_b, v_final_norm_g,
          v_b_v_norm_g, v_ffn_conv_w]
    shapes = [w.shape for w in ws]
    d_p, m_p, v_p = _adamw_flat("adamw_small", _pack(ws), _pack(g_all), _pack(ms), _pack(vs))
    res = {}
    for n, g, d, nm, nv in zip(names, g_all, _unpack(d_p, shapes), _unpack(m_p, shapes), _unpack(v_p, shapes)):
        res[n] = [g, d, nm, nv]
    for st in last_chips:
        chip_end_update(st, [d_p] + list(adamw_tokens))
    for tag, name in (("kv", "w_mem_kv"), ("ain", "a_w_in"), ("aout", "a_w_out"), ("bin", "b_w_in"),
                      ("bout", "b_w_out"), ("up", "ffn_w_up"), ("down", "ffn_w_down")):
        res[name] = list(chain[tag])

    order = ["mix_norm_g", "ffn_norm_g", "mem_norm_g", "w_mem_kv", "a_w_in", "a_w_out", "b_w_in", "b_v_norm_g",
             "b_w_s", "b_s_bias", "b_w_out", "ffn_w_up", "ffn_conv_w", "ffn_conv_b", "ffn_w_down", "final_norm_g"]
    return (loss, grad_x, *[res[n][0] for n in order], *[res[n][1] for n in order],
            *[res[n][2] for n in order], *[res[n][3] for n in order])
```

```python
import functools

import numpy as np
import jax
import jax.numpy as jnp
from jax import lax
from jax.experimental import pallas as pl
from jax.experimental.pallas import tpu as pltpu

F32 = jnp.float32
BF16 = jnp.bfloat16
MESH = pl.DeviceIdType.MESH
AXES = ("x", "y", "c")
N_DEV = 8

EPS = 1e-6
NEG = -1e30
HEAD = 128
HPG = 4
GW = HPG * HEAD
A_PATTERNS = ((128, 1), (512, 4), (2048, 16))
A_HEADS = HPG * len(A_PATTERNS)
QBLK = 128
B_GROUPS = 12
B_W = B_GROUPS * HEAD
SLOPES = (2.0 ** (-8.0 * (np.arange(A_HEADS) + 1) / A_HEADS)).astype(np.float32)
SCALE = HEAD ** -0.5

ADAM_LR = 0.001
ADAM_B1 = 0.9
ADAM_B2 = 0.999
ADAM_EPS = 1e-08
ADAM_WD = 0.01
ADAM_STEP = 10

V7X_VMEM_LIMIT = 50 * 1024 * 1024

NN = (((1,), (0,)), ((), ()))
NT = (((1,), (1,)), ((), ()))
TN = (((0,), (0,)), ((), ()))


def _cp(*sem):
    return pltpu.CompilerParams(dimension_semantics=sem, vmem_limit_bytes=V7X_VMEM_LIMIT)


def _dot(a, b, dims=NN):
    return lax.dot_general(a.astype(BF16), b.astype(BF16), dims, preferred_element_type=F32)


def _tile(n, pref):
    t = min(n, pref)
    assert n % t == 0, (n, pref)
    return t


def _row_tile(rows, cols, mib=1):
    best = None
    for t in range(16, rows + 1, 16):
        if rows % t == 0 and t * cols * 4 <= (mib << 20):
            best = t
    if best is None:
        best = rows
    return best


_DEP = pl.BlockSpec((8, 128), lambda *_: (0, 0))


def _matmul(name, dims, grid, a, a_spec, b, b_spec, out_shape, o_spec, tile, res=None, res_spec=None, deps=()):
    nk = grid[2]
    has_res = res is not None

    def body(*refs):
        a_ref, b_ref = refs[0], refs[1]
        r_ref = refs[2] if has_res else None
        o_ref, acc_ref = refs[-2], refs[-1]
        part = _dot(a_ref[...], b_ref[...], dims)

        def finish(val):
            if has_res:
                val = val + r_ref[...]
            o_ref[...] = val.astype(o_ref.dtype)

        if nk == 1:
            finish(part)
        else:
            k = pl.program_id(2)

            @pl.when(k == 0)
            def _():
                acc_ref[...] = part

            @pl.when(k > 0)
            def _():
                acc_ref[...] += part

            @pl.when(k == nk - 1)
            def _():
                finish(acc_ref[...])

    ins = [a, b] + ([res] if has_res else []) + list(deps)
    specs = [a_spec, b_spec] + ([res_spec] if has_res else []) + [_DEP] * len(deps)
    return pl.pallas_call(
        body, name=name, grid=grid, in_specs=specs, out_specs=o_spec, out_shape=out_shape,
        scratch_shapes=[pltpu.VMEM(tile if nk > 1 else (8, 128), F32)],
        compiler_params=_cp("parallel", "parallel", "arbitrary"))(*ins)


def _mm_full(name, a, w, res=None, tm=1024, tn=512, tk=2048, deps=()):
    M, K = a.shape
    N = w.shape[1]
    tm, tn, tk = _tile(M, tm), _tile(N, tn), _tile(K, tk)
    return _matmul(
        name, NN, (N // tn, M // tm, K // tk),
        a, pl.BlockSpec((tm, tk), lambda j, i, k: (i, k)),
        w, pl.BlockSpec((tk, tn), lambda j, i, k: (k, j)),
        jax.ShapeDtypeStruct((M, N), F32), pl.BlockSpec((tm, tn), lambda j, i, k: (i, j)), (tm, tn),
        res, pl.BlockSpec((tm, tn), lambda j, i, k: (i, j)), deps=deps)


def _mm_gcols(name, a, wg, res=None, split_out=False, tm=1024, deps=(), out_dtype=F32):
    M, K = a.shape
    G, _, Nl = wg.shape
    tm = _tile(M, tm)
    hg = G // 2
    if split_out:
        shape = jax.ShapeDtypeStruct((2, M, hg * Nl), out_dtype)
        o_spec = pl.BlockSpec((None, tm, Nl), lambda g, i, k: (g // hg, i, g % hg))
    else:
        shape = jax.ShapeDtypeStruct((M, G * Nl), out_dtype)
        o_spec = pl.BlockSpec((tm, Nl), lambda g, i, k: (i, g))
    return _matmul(
        name, NN, (G, M // tm, 1),
        a, pl.BlockSpec((tm, K), lambda g, i, k: (i, 0)),
        wg, pl.BlockSpec((None, K, Nl), lambda g, i, k: (g, 0, 0)),
        shape, o_spec, (tm, Nl),
        res, pl.BlockSpec((tm, Nl), lambda g, i, k: (i, g)), deps=deps)


def _mm_dx_full(name, dy, w, tm=512, tko=512, tc=2048, deps=()):
    M, N = dy.shape
    K = w.shape[0]
    tm, tko, tc = _tile(M, tm), _tile(K, tko), _tile(N, tc)
    return _matmul(
        name, NT, (K // tko, M // tm, N // tc),
        dy, pl.BlockSpec((tm, tc), lambda j, i, k: (i, k)),
        w, pl.BlockSpec((tko, tc), lambda j, i, k: (j, k)),
        jax.ShapeDtypeStruct((M, K), F32), pl.BlockSpec((tm, tko), lambda j, i, k: (i, j)), (tm, tko), deps=deps)


def _mm_dx_gcols(name, dy, wg, split_in=False, tm=1024, tko=1024, deps=()):
    G, K, Nl = wg.shape
    M = dy.shape[-2]
    tm, tko = _tile(M, tm), _tile(K, tko)
    hg = G // 2
    if split_in:
        dy_spec = pl.BlockSpec((None, tm, Nl), lambda j, i, g: (g // hg, i, g % hg))
    else:
        dy_spec = pl.BlockSpec((tm, Nl), lambda j, i, g: (i, g))
    return _matmul(
        name, NT, (K // tko, M // tm, G),
        dy, dy_spec,
        wg, pl.BlockSpec((None, tko, Nl), lambda j, i, g: (g, j, 0)),
        jax.ShapeDtypeStruct((M, K), F32), pl.BlockSpec((tm, tko), lambda j, i, g: (i, j)), (tm, tko), deps=deps)


def _lhs_of_dw(a, a_t, ts, tko, index):
    if a_t:
        return NN, pl.BlockSpec((tko, ts), lambda *ids: index(*ids))
    return TN, pl.BlockSpec((ts, tko), lambda *ids: index(*ids)[::-1])


def _mm_dw(name, a, dy, tko=512, tn=1024, ts=2048, deps=(), a_t=False):
    K1, S = a.shape if a_t else a.shape[::-1]
    N = dy.shape[1]
    tko, tn, ts = _tile(K1, tko), _tile(N, tn), _tile(S, ts)
    dims, a_spec = _lhs_of_dw(a, a_t, ts, tko, lambda i, j, k: (j, k))
    return _matmul(
        name, dims, (N // tn, K1 // tko, S // ts),
        a, a_spec,
        dy, pl.BlockSpec((ts, tn), lambda i, j, k: (k, i)),
        jax.ShapeDtypeStruct((K1, N), BF16), pl.BlockSpec((tko, tn), lambda i, j, k: (j, i)), (tko, tn), deps=deps)


def _mm_dw_gcols(name, a, dy, G, split_in=False, tko=1024, ts=2048, deps=(), a_t=False):
    K1, S = a.shape if a_t else a.shape[::-1]
    Nl = (dy.shape[-1] * (2 if split_in else 1)) // G
    tko, ts = _tile(K1, tko), _tile(S, ts)
    hg = G // 2
    dims, a_spec = _lhs_of_dw(a, a_t, ts, tko, lambda g, j, k: (j, k))
    if split_in:
        dy_spec = pl.BlockSpec((None, ts, Nl), lambda g, j, k: (g // hg, k, g % hg))
    else:
        dy_spec = pl.BlockSpec((ts, Nl), lambda g, j, k: (k, g))
    return _matmul(
        name, dims, (G, K1 // tko, S // ts),
        a, a_spec,
        dy, dy_spec,
        jax.ShapeDtypeStruct((G, K1, Nl), BF16), pl.BlockSpec((None, tko, Nl), lambda g, j, k: (g, j, 0)),
        (tko, Nl), deps=deps)


def _rms_fwd(name, x, g, tr=256, deps=(), with_t=False):
    S, D = x.shape
    tr = _tile(S, tr)

    def body(x_ref, g_ref, *rest):
        xf = x_ref[...]
        r = lax.rsqrt(jnp.mean(xf * xf, axis=-1, keepdims=True) + EPS)
        y = xf * r * g_ref[...]
        if with_t:
            rest[-2][...] = y.astype(BF16)
            rest[-1][...] = y.T.astype(BF16)
        else:
            rest[-1][...] = y.astype(BF16)

    row = pl.BlockSpec((tr, D), lambda i: (i, 0))
    out_specs, out_shape = row, jax.ShapeDtypeStruct((S, D), BF16)
    if with_t:
        out_specs = [row, pl.BlockSpec((D, tr), lambda i: (0, i))]
        out_shape = [out_shape, jax.ShapeDtypeStruct((D, S), BF16)]
    return pl.pallas_call(
        body, name=name, grid=(S // tr,),
        in_specs=[row, pl.BlockSpec((1, D), lambda i: (0, 0))] + [_DEP] * len(deps),
        out_specs=out_specs, out_shape=out_shape, compiler_params=_cp("parallel"))(x, g.reshape(1, D), *deps)


def _rms_bwd(name, dh, x, g, dres=None, tr=256):
    S, D = x.shape
    tr = _tile(S, tr)
    has_res = dres is not None

    def body(*refs):
        dh_ref, x_ref, g_ref = refs[:3]
        dres_ref = refs[3] if has_res else None
        dx_ref, dg_ref, dx16_ref = refs[-3:]
        xf = x_ref[...]
        r = lax.rsqrt(jnp.mean(xf * xf, axis=-1, keepdims=True) + EPS)
        xh = xf * r
        dhv = dh_ref[...]
        dxh = dhv * g_ref[...]
        dx = r * (dxh - xh * jnp.mean(dxh * xh, axis=-1, keepdims=True))
        if has_res:
            dx = dx + dres_ref[...]
        dx_ref[...] = dx
        dx16_ref[...] = dx.astype(BF16)
        part = jnp.sum(dhv * xh, axis=0, keepdims=True)
        i = pl.program_id(0)

        @pl.when(i == 0)
        def _():
            dg_ref[...] = part

        @pl.when(i > 0)
        def _():
            dg_ref[...] += part

    row = pl.BlockSpec((tr, D), lambda i: (i, 0))
    vec = pl.BlockSpec((1, D), lambda i: (0, 0))
    ins = [dh, x, g.reshape(1, D)] + ([dres] if has_res else [])
    return pl.pallas_call(
        body, name=name, grid=(S // tr,),
        in_specs=[row, row, vec] + ([row] if has_res else []),
        out_specs=[row, vec, row],
        out_shape=[jax.ShapeDtypeStruct((S, D), F32), jax.ShapeDtypeStruct((1, D), F32),
                   jax.ShapeDtypeStruct((S, D), BF16)],
        compiler_params=_cp("arbitrary"))(*ins)


def _final(name, x, tgt, g, tr=256):
    S, D = x.shape
    tr = _tile(S, tr)

    def body(x_ref, t_ref, g_ref, dx_ref, dg_ref, loss_ref, dx16_ref):
        xf = x_ref[...]
        gv = g_ref[...]
        r = lax.rsqrt(jnp.mean(xf * xf, axis=-1, keepdims=True) + EPS)
        xh = xf * r
        err = xh * gv - t_ref[...]
        sq = jnp.sum(jnp.sum(err * err, axis=1, keepdims=True), axis=0, keepdims=True)
        dy = err * (1.0 / D)
        dxh = dy * gv
        dx = r * (dxh - xh * jnp.mean(dxh * xh, axis=-1, keepdims=True))
        dx_ref[...] = dx
        dx16_ref[...] = dx.astype(BF16)
        part = jnp.sum(dy * xh, axis=0, keepdims=True)
        lpart = jnp.broadcast_to(sq, (8, 128))
        i = pl.program_id(0)

        @pl.when(i == 0)
        def _():
            dg_ref[...] = part
            loss_ref[...] = lpart

        @pl.when(i > 0)
        def _():
            dg_ref[...] += part
            loss_ref[...] += lpart

    row = pl.BlockSpec((tr, D), lambda i: (i, 0))
    vec = pl.BlockSpec((1, D), lambda i: (0, 0))
    return pl.pallas_call(
        body, name=name, grid=(S // tr,), in_specs=[row, row, vec],
        out_specs=[row, vec, pl.BlockSpec((8, 128), lambda i: (0, 0)), row],
        out_shape=[jax.ShapeDtypeStruct((S, D), F32), jax.ShapeDtypeStruct((1, D), F32),
                   jax.ShapeDtypeStruct((8, 128), F32), jax.ShapeDtypeStruct((S, D), BF16)],
        compiler_params=_cp("arbitrary"))(x, tgt, g.reshape(1, D))


def _band_specs(nb, col_of):
    prev = pl.BlockSpec((QBLK, GW), lambda r, b: (jnp.maximum(b - 1, 0), col_of(r)))
    cur = pl.BlockSpec((QBLK, GW), lambda r, b: (b, col_of(r)))
    nxt = pl.BlockSpec((QBLK, GW), lambda r, b: (jnp.minimum(b + 1, nb - 1), col_of(r)))
    return [prev, cur, nxt]


def _cat3(refs, sl):
    return jnp.concatenate([ref[:, sl] for ref in refs], axis=0)


def _group_view(proj, g):
    _, dil = A_PATTERNS[g]
    S, C = proj.shape
    ng = len(A_PATTERNS)
    if dil == 1:
        return proj, lambda which, r: which * ng + g
    cols = [proj[:, (which * ng + g) * GW:(which * ng + g + 1) * GW] for which in range(3)]
    return jnp.concatenate(cols, axis=1).reshape(S // dil, dil * 3 * GW), lambda which, r: r * 3 + which


def _attn_fwd(name, proj, g):
    window, dil = A_PATTERNS[g]
    n_side = (window // 2) // dil
    S, C = proj.shape
    L = S // dil
    nb = L // QBLK
    pv, col = _group_view(proj, g)

    def body(q_ref, kp, kc, kn, vp, vc, vn, o_ref, lse_ref):
        b = pl.program_id(1)
        jq = b * QBLK + lax.broadcasted_iota(jnp.int32, (QBLK, 3 * QBLK), 0)
        jk = (b - 1) * QBLK + lax.broadcasted_iota(jnp.int32, (QBLK, 3 * QBLK), 1)
        rel = jnp.abs(jk - jq)
        mask = (rel <= n_side) & (jk >= 0) & (jk < L)
        dist = rel.astype(F32) * float(dil)
        for hh in range(HPG):
            sl = slice(hh * HEAD, (hh + 1) * HEAD)
            k = _cat3((kp, kc, kn), sl)
            v = _cat3((vp, vc, vn), sl)
            s = _dot(q_ref[:, sl], k, NT) * SCALE - float(SLOPES[g * HPG + hh]) * dist
            s = jnp.where(mask, s, NEG)
            m = jnp.max(s, axis=1, keepdims=True)
            p = jnp.exp(s - m)
            l = jnp.sum(p, axis=1, keepdims=True)
            o_ref[:, sl] = _dot(p, v) / l
            lse_ref[:, sl] = jnp.broadcast_to(m + jnp.log(l), (QBLK, HEAD))

    q_spec = pl.BlockSpec((QBLK, GW), lambda r, b: (b, col(0, r)))
    k_specs = _band_specs(nb, lambda r: col(1, r))
    v_specs = _band_specs(nb, lambda r: col(2, r))
    o_spec = pl.BlockSpec((QBLK, GW), lambda r, b: (b, r))
    shape = jax.ShapeDtypeStruct((L, dil * GW), F32)
    o, lse = pl.pallas_call(
        body, name=name, grid=(dil, nb), in_specs=[q_spec] + k_specs + v_specs,
        out_specs=[o_spec, o_spec], out_shape=[shape, shape],
        compiler_params=_cp("parallel", "parallel"))(pv, pv, pv, pv, pv, pv, pv)
    return o.reshape(S, GW), lse.reshape(S, GW)


def _attn_merge(name, outs, lses, tr=256):
    S = outs[0].shape[0]
    tr = _tile(S, tr)
    ng = len(outs)

    def body(*refs):
        o_refs, l_refs = refs[:ng], refs[ng:2 * ng]
        tok_ref, lse_ref = refs[-2], refs[-1]
        ls = [r[...] for r in l_refs]
        m = functools.reduce(jnp.maximum, ls)
        es = [jnp.exp(l - m) for l in ls]
        tot = functools.reduce(lambda a, b: a + b, es)
        acc = None
        for e, o_ref in zip(es, o_refs):
            term = (e / tot) * o_ref[...]
            acc = term if acc is None else acc + term
        tok_ref[...] = acc
        lse_ref[...] = m + jnp.log(tot)

    row = pl.BlockSpec((tr, GW), lambda i: (i, 0))
    shape = jax.ShapeDtypeStruct((S, GW), F32)
    return pl.pallas_call(
        body, name=name, grid=(S // tr,), in_specs=[row] * (2 * ng), out_specs=[row, row],
        out_shape=[shape, shape], compiler_params=_cp("parallel"))(*outs, *lses)


def _attn_bwd(name, proj, g, dtok_src, dtok_blk, tok, lse):
    window, dil = A_PATTERNS[g]
    n_side = (window // 2) // dil
    S, C = proj.shape
    L = S // dil
    nb = L // QBLK
    pv, col = _group_view(proj, g)
    if dil == 1:
        dcb, dv_ = dtok_src.shape[1] // GW, dtok_src
    else:
        dcb, dv_ = 1, dtok_src[:, dtok_blk * GW:(dtok_blk + 1) * GW].reshape(L, dil * GW)
        dtok_blk = 0
    ov = tok.reshape(L, dil * GW)
    lv = lse.reshape(L, dil * GW)

    def body(qp, qc, qn, kp, kc, kn, vp, vc, vn, dop, doc, don, op, oc, on, lp, lc, ln,
             dq_ref, dk_ref, dv_ref):
        b = pl.program_id(1)
        jq = b * QBLK + lax.broadcasted_iota(jnp.int32, (QBLK, 3 * QBLK), 0)
        jk = (b - 1) * QBLK + lax.broadcasted_iota(jnp.int32, (QBLK, 3 * QBLK), 1)
        rel = jnp.abs(jk - jq)
        mask = (rel <= n_side) & (jk >= 0) & (jk < L)
        dist = rel.astype(F32) * float(dil)
        jq3 = (b - 1) * QBLK + lax.broadcasted_iota(jnp.int32, (3 * QBLK, QBLK), 0)
        jk1 = b * QBLK + lax.broadcasted_iota(jnp.int32, (3 * QBLK, QBLK), 1)
        rel3 = jnp.abs(jk1 - jq3)
        mask3 = (rel3 <= n_side) & (jq3 >= 0) & (jq3 < L)
        dist3 = rel3.astype(F32) * float(dil)
        for hh in range(HPG):
            sl = slice(hh * HEAD, (hh + 1) * HEAD)
            one = slice(hh * HEAD, hh * HEAD + 1)
            slope = float(SLOPES[g * HPG + hh])
            q = qc[:, sl]
            do = doc[:, sl]
            k3 = _cat3((kp, kc, kn), sl)
            v3 = _cat3((vp, vc, vn), sl)
            delta = jnp.sum(do * oc[:, sl], axis=1, keepdims=True)
            s = _dot(q, k3, NT) * SCALE - slope * dist
            p = jnp.where(mask, jnp.exp(s - lc[:, one]), 0.0)
            ds = p * (_dot(do, v3, NT) - delta)
            dq_ref[:, sl] = (_dot(ds, k3) * SCALE).astype(dq_ref.dtype)

            q3 = _cat3((qp, qc, qn), sl)
            do3 = _cat3((dop, doc, don), sl)
            o3 = _cat3((op, oc, on), sl)
            lse3 = _cat3((lp, lc, ln), sl)[:, :1]
            delta3 = jnp.sum(do3 * o3, axis=1, keepdims=True)
            k = kc[:, sl]
            v = vc[:, sl]
            s3 = _dot(q3, k, NT) * SCALE - slope * dist3
            p3 = jnp.where(mask3, jnp.exp(s3 - lse3), 0.0)
            ds3 = p3 * (_dot(do3, v, NT) - delta3)
            dv_ref[:, sl] = _dot(p3, do3, TN).astype(dv_ref.dtype)
            dk_ref[:, sl] = (_dot(ds3, q3, TN) * SCALE).astype(dk_ref.dtype)

    specs = (_band_specs(nb, lambda r: col(0, r)) + _band_specs(nb, lambda r: col(1, r))
             + _band_specs(nb, lambda r: col(2, r))
             + _band_specs(nb, lambda r: r * dcb + dtok_blk)
             + _band_specs(nb, lambda r: r) + _band_specs(nb, lambda r: r))
    o_spec = pl.BlockSpec((QBLK, GW), lambda r, b: (b, r))
    shape = jax.ShapeDtypeStruct((L, dil * GW), BF16)
    outs = pl.pallas_call(
        body, name=name, grid=(dil, nb), in_specs=specs, out_specs=[o_spec] * 3, out_shape=[shape] * 3,
        compiler_params=_cp("parallel", "parallel"))(*([pv] * 9 + [dv_] * 3 + [ov] * 3 + [lv] * 3))
    return [o.reshape(S, GW) for o in outs]


def _mem_fwd(name, proj, q_blk, kv, tq=256):
    S = proj.shape[0]
    M = kv.shape[0]
    tq = _tile(S, tq)

    def body(q_ref, kv_ref, o_ref):
        for hh in range(HPG):
            sl = slice(hh * HEAD, (hh + 1) * HEAD)
            k = kv_ref[:, sl]
            v = kv_ref[:, GW + hh * HEAD:GW + (hh + 1) * HEAD]
            s = _dot(q_ref[:, sl], k, NT) * SCALE
            m = jnp.max(s, axis=1, keepdims=True)
            p = jnp.exp(s - m)
            p = p / jnp.sum(p, axis=1, keepdims=True)
            o_ref[:, sl] = _dot(p, v)

    return pl.pallas_call(
        body, name=name, grid=(S // tq,),
        in_specs=[pl.BlockSpec((tq, GW), lambda i: (i, q_blk)), pl.BlockSpec((M, 2 * GW), lambda i: (0, 0))],
        out_specs=pl.BlockSpec((tq, GW), lambda i: (i, 0)),
        out_shape=jax.ShapeDtypeStruct((S, GW), F32), compiler_params=_cp("parallel"))(proj, kv)


def _mem_bwd(name, proj, q_blk, kv, dcat, do_blk, tq=256):
    S = proj.shape[0]
    M = kv.shape[0]
    tq = _tile(S, tq)

    def body(q_ref, kv_ref, do_ref, dq_ref, dkv_ref):
        i = pl.program_id(0)
        for hh in range(HPG):
            sl = slice(hh * HEAD, (hh + 1) * HEAD)
            vsl = slice(GW + hh * HEAD, GW + (hh + 1) * HEAD)
            q = q_ref[:, sl]
            do = do_ref[:, sl]
            k = kv_ref[:, sl]
            v = kv_ref[:, vsl]
            s = _dot(q, k, NT) * SCALE
            m = jnp.max(s, axis=1, keepdims=True)
            p = jnp.exp(s - m)
            p = p / jnp.sum(p, axis=1, keepdims=True)
            dp = _dot(do, v, NT)
            ds = p * (dp - jnp.sum(dp * p, axis=1, keepdims=True))
            dq_ref[:, sl] = (_dot(ds, k) * SCALE).astype(dq_ref.dtype)
            dk = _dot(ds, q, TN) * SCALE
            dvv = _dot(p, do, TN)

            @pl.when(i == 0)
            def _():
                dkv_ref[:, sl] = dk
                dkv_ref[:, vsl] = dvv

            @pl.when(i > 0)
            def _():
                dkv_ref[:, sl] += dk
                dkv_ref[:, vsl] += dvv

    return pl.pallas_call(
        body, name=name, grid=(S // tq,),
        in_specs=[pl.BlockSpec((tq, GW), lambda i: (i, q_blk)), pl.BlockSpec((M, 2 * GW), lambda i: (0, 0)),
                  pl.BlockSpec((tq, GW), lambda i: (i, do_blk))],
        out_specs=[pl.BlockSpec((tq, GW), lambda i: (i, 0)), pl.BlockSpec((M, 2 * GW), lambda i: (0, 0))],
        out_shape=[jax.ShapeDtypeStruct((S, GW), BF16), jax.ShapeDtypeStruct((M, 2 * GW), F32)],
        compiler_params=_cp("arbitrary"))(proj, kv, dcat)


_RSQRT2 = float(1.0 / np.sqrt(2.0))
_RSQRT2PI = float(1.0 / np.sqrt(2.0 * np.pi))


def _gelu(x):
    return 0.5 * x * (1.0 + lax.erf(x * _RSQRT2))


def _gelu_and_grad(x):
    cdf = 0.5 * (1.0 + lax.erf(x * _RSQRT2))
    return x * cdf, cdf + x * jnp.exp(-0.5 * x * x) * _RSQRT2PI


def _sgu_fwd(name, proj, gv, w_s, bias_t):
    S = proj.shape[0]
    nch = S // HEAD

    def body(u_ref, v_ref, gv_ref, ws_ref, b_ref, o_ref):
        v = _gelu(v_ref[...])
        r = lax.rsqrt(jnp.mean(v * v, axis=-1, keepdims=True) + EPS)
        vn = v * r * gv_ref[...]
        for gg in range(B_GROUPS):
            sl = slice(gg * HEAD, (gg + 1) * HEAD)
            mixed = _dot(ws_ref[gg], vn[:, sl]) + b_ref[:, gg:gg + 1]
            o_ref[:, sl] = _gelu(u_ref[:, sl]) * mixed

    return pl.pallas_call(
        body, name=name, grid=(nch,),
        in_specs=[pl.BlockSpec((HEAD, B_W), lambda c: (c, 0)), pl.BlockSpec((HEAD, B_W), lambda c: (c, 1)),
                  pl.BlockSpec((1, B_W), lambda c: (0, 0)),
                  pl.BlockSpec((B_GROUPS, HEAD, HEAD), lambda c: (0, 0, 0)),
                  pl.BlockSpec((HEAD, B_GROUPS), lambda c: (0, 0))],
        out_specs=pl.BlockSpec((HEAD, B_W), lambda c: (c, 0)),
        out_shape=jax.ShapeDtypeStruct((S, B_W), F32),
        compiler_params=_cp("parallel"))(proj, proj, gv.reshape(1, B_W), w_s, bias_t)


def _sgu_bwd(name, proj, gv, w_s, bias_t, dcat):
    S = proj.shape[0]
    nch = S // HEAD

    def body(u_ref, v_ref, gv_ref, ws_ref, b_ref, dt_ref, du_ref, dvp_ref, dgv_ref, dws_ref, db_ref, dvn_ref):
        c = pl.program_id(0)
        vpre = v_ref[...]
        v, v_slope = _gelu_and_grad(vpre)
        r = lax.rsqrt(jnp.mean(v * v, axis=-1, keepdims=True) + EPS)
        vh = v * r
        gvv = gv_ref[...]
        vn = vh * gvv
        for gg in range(B_GROUPS):
            sl = slice(gg * HEAD, (gg + 1) * HEAD)
            upre = u_ref[:, sl]
            dt = dt_ref[:, sl]
            vng = vn[:, sl]
            mixed = _dot(ws_ref[gg], vng) + b_ref[:, gg:gg + 1]
            u, u_slope = _gelu_and_grad(upre)
            du_ref[:, sl] = (dt * mixed * u_slope).astype(du_ref.dtype)
            dmix = dt * u
            dvn_ref[:, sl] = _dot(ws_ref[gg], dmix, TN)
            dws = _dot(dmix, vng, NT)
            dbs = jnp.sum(dmix, axis=1, keepdims=True)

            @pl.when(c == 0)
            def _():
                dws_ref[gg] = dws
                db_ref[:, gg:gg + 1] = dbs

            @pl.when(c > 0)
            def _():
                dws_ref[gg] += dws
                db_ref[:, gg:gg + 1] += dbs

        dvn = dvn_ref[...]
        dgp = jnp.sum(dvn * vh, axis=0, keepdims=True)
        dvh = dvn * gvv
        dv = r * (dvh - vh * jnp.mean(dvh * vh, axis=-1, keepdims=True))
        dvp_ref[...] = (dv * v_slope).astype(dvp_ref.dtype)

        @pl.when(c == 0)
        def _():
            dgv_ref[...] = dgp

        @pl.when(c > 0)
        def _():
            dgv_ref[...] += dgp

    blk = lambda j: pl.BlockSpec((HEAD, B_W), lambda c: (c, j))
    vec = pl.BlockSpec((1, B_W), lambda c: (0, 0))
    ws_spec = pl.BlockSpec((B_GROUPS, HEAD, HEAD), lambda c: (0, 0, 0))
    b_spec = pl.BlockSpec((HEAD, B_GROUPS), lambda c: (0, 0))
    du, dvp, dgv, dws, db = pl.pallas_call(
        body, name=name, grid=(nch,),
        in_specs=[blk(0), blk(1), vec, ws_spec, b_spec, blk(0)],
        out_specs=[blk(0), blk(0), vec, ws_spec, b_spec],
        out_shape=[jax.ShapeDtypeStruct((S, B_W), BF16), jax.ShapeDtypeStruct((S, B_W), BF16),
                   jax.ShapeDtypeStruct((1, B_W), F32), jax.ShapeDtypeStruct((B_GROUPS, HEAD, HEAD), F32),
                   jax.ShapeDtypeStruct((HEAD, B_GROUPS), F32)],
        scratch_shapes=[pltpu.VMEM((HEAD, B_W), F32)],
        compiler_params=_cp("arbitrary"))(proj, proj, gv.reshape(1, B_W), w_s, bias_t, dcat)
    return du, dvp, dgv, dws, db


def _shift_down(a, row):
    return jnp.where(row == 0, 0.0, pltpu.roll(a, 1, 0))


def _shift_up(a, row):
    n = a.shape[0]
    return jnp.where(row == n - 1, 0.0, pltpu.roll(a, n - 1, 0))


def _conv(a, w, b, row):
    return _shift_down(a, row) * w[0:1] + a * w[1:2] + _shift_up(a, row) * w[2:3] + b


def _conv_fwd(name, a3, cw, cb, tc=256):
    _, S, FF = a3.shape
    tc = _tile(FF, tc)

    def body(a_ref, w_ref, b_ref, o_ref):
        row = lax.broadcasted_iota(jnp.int32, (S, tc), 0)
        cg = _conv(a_ref[0], w_ref[0], b_ref[0], row)
        cv = _conv(a_ref[1], w_ref[1], b_ref[1], row)
        o_ref[...] = (_gelu(cg) * cv).astype(o_ref.dtype)

    return pl.pallas_call(
        body, name=name, grid=(FF // tc,),
        in_specs=[pl.BlockSpec((2, S, tc), lambda j: (0, 0, j)), pl.BlockSpec((2, 3, tc), lambda j: (0, 0, j)),
                  pl.BlockSpec((2, 1, tc), lambda j: (0, 0, j))],
        out_specs=pl.BlockSpec((S, tc), lambda j: (0, j)),
        out_shape=jax.ShapeDtypeStruct((S, FF), BF16), compiler_params=_cp("parallel"))(a3, cw, cb)


def _conv_bwd(name, a3, cw, cb, dact, tc=128):
    _, S, FF = a3.shape
    tc = _tile(FF, tc)

    def body(a_ref, w_ref, b_ref, d_ref, da_ref, dw_ref, db_ref):
        row = lax.broadcasted_iota(jnp.int32, (S, tc), 0)
        ag, av = a_ref[0], a_ref[1]
        wg, wv = w_ref[0], w_ref[1]
        cg = _conv(ag, wg, b_ref[0], row)
        cv = _conv(av, wv, b_ref[1], row)
        d = d_ref[...]
        gate, gate_slope = _gelu_and_grad(cg)
        dcs = (d * cv * gate_slope, d * gate)
        for h, (dc, a, w) in enumerate(zip(dcs, (ag, av), (wg, wv))):
            da = _shift_up(dc, row) * w[0:1] + dc * w[1:2] + _shift_down(dc, row) * w[2:3]
            da_ref[h] = da.astype(da_ref.dtype)
            dw_ref[h, 0:1, :] = jnp.sum(dc * _shift_down(a, row), axis=0, keepdims=True)
            dw_ref[h, 1:2, :] = jnp.sum(dc * a, axis=0, keepdims=True)
            dw_ref[h, 2:3, :] = jnp.sum(dc * _shift_up(a, row), axis=0, keepdims=True)
            db_ref[h] = jnp.sum(dc, axis=0, keepdims=True)

    a_spec = pl.BlockSpec((2, S, tc), lambda j: (0, 0, j))
    w_spec = pl.BlockSpec((2, 3, tc), lambda j: (0, 0, j))
    b_spec = pl.BlockSpec((2, 1, tc), lambda j: (0, 0, j))
    return pl.pallas_call(
        body, name=name, grid=(FF // tc,),
        in_specs=[a_spec, w_spec, b_spec, pl.BlockSpec((S, tc), lambda j: (0, j))],
        out_specs=[a_spec, w_spec, b_spec],
        out_shape=[jax.ShapeDtypeStruct((2, S, FF), BF16), jax.ShapeDtypeStruct((2, 3, FF), F32),
                   jax.ShapeDtypeStruct((2, 1, FF), F32)],
        compiler_params=_cp("parallel"))(a3, cw, cb, dact)


_HBM = pl.BlockSpec(memory_space=pltpu.HBM)


def _position():
    return lax.axis_index("x"), lax.axis_index("y"), lax.axis_index("c")


_SEM =pl.BlockSpec(memory_space=pltpu.SEMAPHORE)
_EFFECT = pltpu.SideEffectType.DATAFLOW_SIDE_EFFECTING
_FLIPS = ((1, 0), (0, 1), (1, 1))


def _split_start(name, bufs, ncopy, plan, after=()):
    n = len(bufs)
    after = list(after)

    def body(*refs):
        ins = refs[:n]
        send_sems, recv_sems, token = refs[n + len(after)], refs[n + len(after) + 1], refs[-1]
        for i, (src, dst, to) in enumerate(plan(ins)):
            pltpu.make_async_remote_copy(src_ref=src, dst_ref=dst, send_sem=send_sems.at[i], recv_sem=recv_sems.at[i],
                                         device_id=to, device_id_type=MESH).start()
        token[...] = jnp.zeros_like(token)

    outs = pl.pallas_call(
        body, name=name,
        out_shape=(pltpu.SemaphoreType.DMA((ncopy,)), pltpu.SemaphoreType.DMA((ncopy,)),
                   *[pltpu.HBM(b.shape, b.dtype) for b in bufs], jax.ShapeDtypeStruct((8, 128), F32)),
        in_specs=[_HBM] * n + [pl.BlockSpec(memory_space=pl.ANY)] * len(after),
        out_specs=(_SEM, _SEM, *([_HBM] * n), pl.BlockSpec(memory_space=pltpu.VMEM)),
        input_output_aliases={i: 2 + i for i in range(n)},
        compiler_params=pltpu.CompilerParams(has_side_effects=_EFFECT),
    )(*[pltpu.with_memory_space_constraint(b, pltpu.HBM) for b in bufs], *after)
    return outs[0], outs[1], list(outs[2:2 + n]), outs[-1]


def _split_wait(name, bufs, send_sems, recv_sems, plan, after):
    n = len(bufs)
    after = list(after)

    def body(*refs):
        ins = refs[:n]
        ssem, rsem = refs[n], refs[n + 1]
        for i, (src, dst, to) in enumerate(plan(ins)):
            cp = pltpu.make_async_remote_copy(src_ref=src, dst_ref=dst, send_sem=ssem.at[i], recv_sem=rsem.at[i],
                                              device_id=to, device_id_type=MESH)
            cp.wait_send()
            cp.wait_recv()

    outs = pl.pallas_call(
        body, name=name, out_shape=tuple(pltpu.HBM(b.shape, b.dtype) for b in bufs),
        in_specs=[_HBM] * n + [_SEM, _SEM] + [pl.BlockSpec(memory_space=pl.ANY)] * len(after),
        out_specs=tuple([_HBM] * n), input_output_aliases={i: i for i in range(n)},
        compiler_params=pltpu.CompilerParams(has_side_effects=_EFFECT),
    )(*bufs, send_sems, recv_sems, *after)
    return list(outs)


def _gather_plan(refs):
    px, py, pc = _position()
    me = 4 * px + 2 * py + pc
    targets = [(px, py, 1 - pc), (1 - px, py, pc), (px, 1 - py, pc), (1 - px, 1 - py, pc)]
    return [(r.at[me], r.at[me], to) for r in refs for to in targets]


def _forward_plan(refs):
    px, py, pc = _position()
    out = []
    for r in refs:
        for fx, fy in _FLIPS:
            slot = 4 * (1 - px if fx else px) + 2 * (1 - py if fy else py) + pc
            out.append((r.at[slot], r.at[slot], (px, py, 1 - pc)))
    return out


def _pair_plan(n):
    def plan(refs):
        px, py, pc = _position()
        return [(refs[w].at[2 * k + (1 - pc)], refs[n + w].at[k], (px, py, 1 - pc)) for w in range(n) for k in range(4)]
    return plan


def _chip_plan(n):
    def plan(refs):
        px, py, pc = _position()
        out = []
        for w in range(n):
            for j, (fx, fy) in enumerate(_FLIPS):
                qx = 1 - px if fx else px
                qy = 1 - py if fy else py
                out.append((refs[w].at[2 * qx + qy], refs[n + w].at[j], (qx, qy, pc)))
        return out
    return plan


def _broadcast_plan(refs):
    px, py, pc = _position()
    me = 4 * px + 2 * py + pc
    flips = [(fx, fy, fc) for fx in (0, 1) for fy in (0, 1) for fc in (0, 1)][1:]
    targets = [(1 - px if fx else px, 1 - py if fy else py, 1 - pc if fc else pc) for fx, fy, fc in flips]
    return [(r.at[me], r.at[me], to) for r in refs for to in targets]


def _cast_place(name, dev, w, layer, dtype=BF16):
    nl, R, C = w.shape
    tr = _row_tile(R, C)

    def body(dev_ref, w_ref, o_ref):
        o_ref[...] = w_ref[...].astype(o_ref.dtype)

    return pl.pallas_call(
        body, name=name,
        grid_spec=pltpu.PrefetchScalarGridSpec(
            num_scalar_prefetch=1, grid=(R // tr,),
            in_specs=[pl.BlockSpec((None, tr, C), lambda i, d: (layer, i, 0))],
            out_specs=pl.BlockSpec((None, tr, C), lambda i, d: (d[0], i, 0))),
        out_shape=jax.ShapeDtypeStruct((N_DEV, R, C), dtype), compiler_params=_cp("parallel"))(dev, w)


def _pair_sum(name, core, dw, recv):
    _, R, C = dw.shape
    tr = _row_tile(R, C, 4)
    dw4 = dw.reshape(4, 2, R, C)

    def body(core_ref, a_ref, b_ref, o_ref):
        o_ref[...] = (a_ref[...].astype(F32) + b_ref[...].astype(F32)).astype(o_ref.dtype)

    return pl.pallas_call(
        body, name=name,
        grid_spec=pltpu.PrefetchScalarGridSpec(
            num_scalar_prefetch=1, grid=(4, R // tr),
            in_specs=[pl.BlockSpec((None, None, tr, C), lambda k, i, c_ref: (k, c_ref[0], i, 0)),
                      pl.BlockSpec((None, tr, C), lambda k, i, c_ref: (k, i, 0))],
            out_specs=pl.BlockSpec((None, tr, C), lambda k, i, c_ref: (k, i, 0))),
        out_shape=jax.ShapeDtypeStruct((4, R, C), BF16),
        compiler_params=_cp("parallel", "parallel"))(core, dw4, recv)


def _adamw_math(w, g, m, v):
    m = ADAM_B1 * m + (1.0 - ADAM_B1) * g
    v = ADAM_B2 * v + (1.0 - ADAM_B2) * (g * g)
    m_hat = m / (1.0 - ADAM_B1 ** ADAM_STEP)
    v_hat = v / (1.0 - ADAM_B2 ** ADAM_STEP)
    delta = -ADAM_LR * (m_hat / (jnp.sqrt(v_hat) + ADAM_EPS) + ADAM_WD * w)
    return delta, m, v


def _adamw_shard(name, chip, layer, w, m, v, p, recv, prev, deps=()):
    nl, R, C = w.shape
    tr = _row_tile(R, C, 2)
    n_prev = 0 if prev is None else 4

    def body(chip_ref, w_ref, m_ref, v_ref, p_ref, r_ref, *rest):
        g_ref, d_ref, nm_ref, nv_ref, tok_ref = rest[-5:]
        tok_ref[...] = jnp.zeros_like(tok_ref)
        g = p_ref[...].astype(F32)
        for j in range(3):
            g = g + r_ref[j].astype(F32)
        delta, nm, nv = _adamw_math(w_ref[...], g, m_ref[...], v_ref[...])
        g_ref[...] = g
        d_ref[...] = delta
        nm_ref[...] = nm
        nv_ref[...] = nv

    lay = pl.BlockSpec((None, tr, C), lambda i, c_ref: (layer, i, 0))
    in_specs = [lay, lay, lay,
                pl.BlockSpec((None, tr, C), lambda i, c_ref: (c_ref[0], i, 0)),
                pl.BlockSpec((3, tr, C), lambda i, c_ref: (0, i, 0))]
    in_specs += [pl.BlockSpec(memory_space=pl.ANY)] * n_prev + [_DEP] * len(deps)
    shape = jax.ShapeDtypeStruct((nl, R, C), F32)
    ins = [chip, w, m, v, p, recv] + ([] if prev is None else list(prev)) + list(deps)
    return pl.pallas_call(
        body, name=name,
        grid_spec=pltpu.PrefetchScalarGridSpec(
            num_scalar_prefetch=1, grid=(R // tr,), in_specs=in_specs, out_specs=[lay] * 4 + [_DEP]),
        out_shape=[shape] * 4 + [jax.ShapeDtypeStruct((8, 128), F32)],
        input_output_aliases={6 + j: j for j in range(n_prev)},
        compiler_params=_cp("arbitrary"))(*ins)


def _sum_slots(name, parts, tr=512):
    n, R, C = parts.shape
    tr = _tile(R, tr)

    def body(p_ref, o_ref):
        acc = p_ref[0]
        for j in range(1, n):
            acc = acc + p_ref[j]
        o_ref[...] = acc

    return pl.pallas_call(
        body, name=name, grid=(R // tr,),
        in_specs=[pl.BlockSpec((n, tr, C), lambda i: (0, i, 0))],
        out_specs=pl.BlockSpec((tr, C), lambda i: (i, 0)),
        out_shape=jax.ShapeDtypeStruct((R, C), F32), compiler_params=_cp("parallel"))(parts)


def _adamw_flat(name, w, g, m, v, tr=512):
    R, C = w.shape
    tr = _tile(R, tr)

    def body(w_ref, g_ref, m_ref, v_ref, d_ref, nm_ref, nv_ref):
        delta, nm, nv = _adamw_math(w_ref[...], g_ref[...], m_ref[...], v_ref[...])
        d_ref[...] = delta
        nm_ref[...] = nm
        nv_ref[...] = nv

    row = pl.BlockSpec((tr, C), lambda i: (i, 0))
    shape = jax.ShapeDtypeStruct((R, C), F32)
    return pl.pallas_call(
        body, name=name, grid=(R // tr,), in_specs=[row] * 4, out_specs=[row] * 3, out_shape=[shape] * 3,
        compiler_params=_cp("parallel"))(w, g, m, v)


_PACK_ROWS = 512


def _pack(arrs):
    flat = jnp.concatenate([a.reshape(-1) for a in arrs])
    unit = _PACK_ROWS * 128
    pad = (-flat.shape[0]) % unit
    return jnp.pad(flat, (0, pad)).reshape(-1, 128)


def _unpack(packed, shapes):
    flat = packed.reshape(-1)
    outs, off = [], 0
    for s in shapes:
        n = int(np.prod(s))
        outs.append(flat[off:off + n].reshape(s))
        off += n
    return outs


def kernel(x, mem, mix_norm_g, ffn_norm_g, mem_norm_g, w_mem_kv, a_w_in, a_w_out, b_w_in, b_v_norm_g, b_w_s, b_s_bias, b_w_out, ffn_w_up, ffn_conv_w, ffn_conv_b, ffn_w_down, final_norm_g, loss_target, m_mix_norm_g, m_ffn_norm_g, m_mem_norm_g, m_w_mem_kv, m_a_w_in, m_a_w_out, m_b_w_in, m_b_v_norm_g, m_b_w_s, m_b_s_bias, m_b_w_out, m_ffn_w_up, m_ffn_conv_w, m_ffn_conv_b, m_ffn_w_down, m_final_norm_g, v_mix_norm_g, v_ffn_norm_g, v_mem_norm_g, v_w_mem_kv, v_a_w_in, v_a_w_out, v_b_w_in, v_b_v_norm_g, v_b_w_s, v_b_s_bias, v_b_w_out, v_ffn_w_up, v_ffn_conv_w, v_ffn_conv_b, v_ffn_w_down, v_final_norm_g):
    px, py, pc = _position()
    dev = 4 * px + 2 * py + pc
    core = jnp.reshape(pc, (1,)).astype(jnp.int32)
    chip = jnp.reshape(2 * px + py, (1,)).astype(jnp.int32)

    x0 = x[0]
    mem0 = mem[0]
    tgt = loss_target[0]
    S, D = x0.shape
    depth = mix_norm_g.shape[0]
    FF = ffn_w_down.shape[1] * N_DEV
    a_in = a_w_in.shape[2] * N_DEV
    b_in = b_w_in.shape[2] * N_DEV
    a_q_blk = (a_in - GW) // GW
    b_q_blk = (b_in - GW) // GW

    stacks = {"kv": (w_mem_kv, m_w_mem_kv, v_w_mem_kv), "ain": (a_w_in, m_a_w_in, v_a_w_in),
              "aout": (a_w_out, m_a_w_out, v_a_w_out), "bin": (b_w_in, m_b_w_in, v_b_w_in),
              "bout": (b_w_out, m_b_w_out, v_b_w_out), "up": (ffn_w_up, m_ffn_w_up, v_ffn_w_up),
              "down": (ffn_w_down, m_ffn_w_down, v_ffn_w_down)}
    dev1 = jnp.reshape(dev, (1,)).astype(jnp.int32)

    def groups_of(i):
        j = i // 2
        mix = [("kv", i), ("ain", j), ("aout", j)] if i % 2 == 0 else [("kv", i), ("bin", j), ("bout", j)]
        return mix, [("up", i), ("down", i)]

    gather_groups = [(f"{half}{i}", members) for i in range(depth) for half, members in zip("mf", groups_of(i))]
    gather_ahead = 2
    in_flight = {}

    def gather_start(k, after):
        gname, members = gather_groups[k]
        lands = [_cast_place(f"place_{t}{l}", dev1, stacks[t][0], l) for t, l in members]
        ssem, rsem, lands, tok = _split_start(f"ag_start_{gname}", lands, 4 * len(lands), _gather_plan, after)
        in_flight[k] = (lands, ssem, rsem)
        return tok

    small_land = _cast_place("place_small_w", dev1, _pack([ffn_conv_w, b_v_norm_g])[None], 0, F32)
    small_ssem, small_rsem, small_lands, small_tok = _split_start("smallw_start", [small_land], N_DEV - 1,
                                                                  _broadcast_plan)
    start_tokens = [small_tok, gather_start(0, [small_tok])]
    passing = {}

    def gather_arrive(k, after):
        if k >= len(gather_groups):
            return []
        gname, members = gather_groups[k]
        lands, ssem, rsem = in_flight.pop(k)
        lands = _split_wait(f"ag_wait_{gname}", lands, ssem, rsem, _gather_plan, after)
        toks = []
        for q in (range(1, 1 + gather_ahead) if k == 0 else [k + gather_ahead]):
            if q < len(gather_groups):
                toks.append(gather_start(q, [lands[0]] + toks))
        ssem, rsem, lands, tok = _split_start(f"ag_pass_{gname}", lands, 3 * len(lands), _forward_plan, toks)
        passing[k] = (lands, ssem, rsem)
        return toks + [tok]

    def gather_ready(k, after):
        gname, members = gather_groups[k]
        lands, ssem, rsem = passing.pop(k)
        lands = _split_wait(f"ag_ready_{gname}", lands, ssem, rsem, _forward_plan, after)
        out = {}
        for (t, l), land in zip(members, lands):
            if t == "kv":
                out["kv"] = land.reshape(D, 2 * GW)
            elif t in ("ain", "aout", "up"):
                out[{"ain": "in", "aout": "out", "up": "up"}[t]] = land
            elif t == "bin":
                out["in"] = jnp.transpose(land, (1, 0, 2)).reshape(D, b_in)
            elif t == "bout":
                out["out"] = land.reshape(B_W + GW, D)
            else:
                out["down"] = land.reshape(FF, D)
        return out

    def small_weights(after):
        (small_all,) = _split_wait("smallw_wait", small_lands, small_ssem, small_rsem, _broadcast_plan, after)
        cw_parts, gv_parts = [], []
        for d in range(N_DEV):
            cw_d, gv_d = _unpack(small_all[d], [ffn_conv_w.shape, b_v_norm_g.shape])
            cw_parts.append(cw_d)
            gv_parts.append(gv_d)
        return jnp.concatenate(cw_parts, axis=-1), jnp.concatenate(gv_parts, axis=-1)

    def conv_params(i):
        cw = conv_w_full[i].reshape(3, 2, FF).transpose(1, 0, 2)
        cb = ffn_conv_b[i].reshape(2, 1, FF)
        return cw, cb

    saved = []
    W = []
    xc = x0
    toks = start_tokens + gather_arrive(0, [x0])
    for i in range(depth):
        j = i // 2
        lw = gather_ready(2 * i, [xc])
        sv = {"x0": xc}
        h1, h1t = _rms_fwd(f"mixnorm{i}", xc, mix_norm_g[i], deps=toks, with_t=True)
        memn = _rms_fwd(f"memnorm{i}", mem0, mem_norm_g[i])
        if i % 2 == 0:
            proj = _mm_gcols(f"ain{i}", h1, lw["in"], out_dtype=BF16)
        else:
            proj = _mm_full(f"bin{i}", h1, lw["in"])
        toks = gather_arrive(2 * i + 1, [proj]) if i > 0 else []
        kv = _mm_full(f"kvproj{i}", memn, lw["kv"], deps=toks)
        if i % 2 == 0:
            outs, lses = [], []
            for g in range(len(A_PATTERNS)):
                o, l = _attn_fwd(f"attn{i}_{g}", proj, g)
                outs.append(o)
                lses.append(l)
            tok, lse = _attn_merge(f"merge{i}", outs, lses)
            mo = _mem_fwd(f"memattn{i}", proj, a_q_blk, kv)
            cat = jnp.concatenate([tok.astype(BF16), mo.astype(BF16)], axis=1)
            x1 = _mm_gcols(f"aout{i}", cat, lw["out"], res=xc)
            sv.update(tok=tok, lse=lse)
        else:
            bias_t = b_s_bias[j].T
            tok = _sgu_fwd(f"sgu{i}", proj, gv_full[j], b_w_s[j], bias_t)
            mo = _mem_fwd(f"memattn{i}", proj, b_q_blk, kv)
            cat = jnp.concatenate([tok.astype(BF16), mo.astype(BF16)], axis=1)
            x1 = _mm_full(f"bout{i}", cat, lw["out"], res=xc)
        toks = gather_arrive(1, [x1]) if i == 0 else []
        lw.update(gather_ready(2 * i + 1, [x1]))
        W.append(lw)
        if i == 0:
            conv_w_full, gv_full = small_weights([x1])
        h2, h2t = _rms_fwd(f"ffnnorm{i}", x1, ffn_norm_g[i], deps=toks, with_t=True)
        cw, cb = conv_params(i)
        a3 = _mm_gcols(f"up{i}", h2, lw["up"], split_out=True)
        toks = gather_arrive(2 * i + 2, [a3])
        act = _conv_fwd(f"conv{i}", a3, cw, cb)
        x2 = _mm_full(f"down{i}", act, lw["down"], res=x1, tn=1024, tk=FF // 4, deps=toks)
        toks = []
        sv.update(h1t=h1t, memn=memn, kv=kv, proj=proj, cat=cat, x1=x1, h2t=h2t, a3=a3, act=act)
        saved.append(sv)
        xc = x2

    dx, dg_final, sq, dx16 = _final("final", xc, tgt, final_norm_g)
    loss_local = sq[0, 0] * (0.5 / D)

    chain = {}
    adamw_tokens = []

    def pair_begin(gname, members, dws):
        n = len(dws)
        recvs = [lax.empty((4,) + dw.shape[1:], dw.dtype) for dw in dws]
        ssem, rsem, bufs, tok = _split_start(f"rs_pair_start_{gname}", dws + recvs, 4 * n, _pair_plan(n))
        return dict(name=gname, members=members, n=n, bufs=bufs, sems=(ssem, rsem)), tok

    def pair_end_chip_begin(st, after):
        n, gname = st["n"], st["name"]
        bufs = _split_wait(f"rs_pair_wait_{gname}", st["bufs"], *st["sems"], _pair_plan(n), after)
        ps = [_pair_sum(f"rs_sum_{t}{l}", core, bufs[w], bufs[n + w]) for w, (t, l) in enumerate(st["members"])]
        recvs = [lax.empty((3,) + p.shape[1:], BF16) for p in ps]
        ssem, rsem, bufs, tok = _split_start(f"rs_chip_start_{gname}", ps + recvs, 3 * n, _chip_plan(n))
        return dict(name=gname, members=st["members"], n=n, bufs=bufs, sems=(ssem, rsem)), tok

    def chip_end_update(st, after, deps=()):
        n = st["n"]
        bufs = _split_wait(f"rs_chip_wait_{st['name']}", st["bufs"], *st["sems"], _chip_plan(n), after)
        for w, (t, l) in enumerate(st["members"]):
            wst, mst, vst = stacks[t]
            *chain[t], tok = _adamw_shard(f"adamw_{t}{l}", chip, l, wst, mst, vst, bufs[w], bufs[n + w], chain.get(t),
                                          deps)
            adamw_tokens.append(tok)

    pipe = {"pair": [], "chip": [], "deps": []}

    def take_deps():
        deps, pipe["deps"] = pipe["deps"], []
        return deps

    def submit(gname, members, dws):
        st, tok = pair_begin(gname, members, dws)
        pipe["pair"].append(st)
        pipe["deps"].append(tok)

    def advance(after):
        arrived, pipe["chip"] = pipe["chip"], []
        toks = []
        for st in pipe["pair"]:
            new, tok = pair_end_chip_begin(st, [after])
            pipe["chip"].append(new)
            toks.append(tok)
        pipe["pair"] = []
        done = len(adamw_tokens)
        for st in arrived:
            chip_end_update(st, [after], toks)
        pipe["deps"] += toks + adamw_tokens[done:]

    def small_start(tag, arrs, after):
        land = _cast_place(f"place_small_{tag}", dev1, _pack(arrs)[None], 0, F32)
        ssem, rsem, lands, tok = _split_start(f"small_start_{tag}", [land], N_DEV - 1, _broadcast_plan, after)
        return (lands, ssem, rsem), tok

    def small_end(tag, state, shapes, after):
        lands, ssem, rsem = state
        lands = _split_wait(f"small_wait_{tag}", lands, ssem, rsem, _broadcast_plan, after)
        return _unpack(_sum_slots(f"small_sum_{tag}", lands[0]), shapes)

    def late_small():
        return [dg_mix[0], dg_ffn[0], dg_mem[0], d_conv_b[0][None], d_conv_w[0][None]]

    assert depth >= 2
    big = {k: [None] * n for k, n in (("kv", depth), ("ain", depth // 2 + depth % 2), ("aout", depth // 2 + depth % 2),
                                      ("bin", depth // 2), ("bout", depth // 2), ("up", depth), ("down", depth))}
    dg_mix, dg_ffn, dg_mem = [None] * depth, [None] * depth, [None] * depth
    d_conv_w, d_conv_b = [None] * depth, [None] * depth
    d_gv, d_ws, d_sb = [None] * (depth // 2), [None] * (depth // 2), [None] * (depth // 2)
    for i in reversed(range(depth)):
        j = i // 2
        lw, sv = W[i], saved[i]
        cw, cb = conv_params(i)
        mix_members, ffn_members = groups_of(i)
        if i == 0:
            early_arrays = [loss_local.reshape(1), dg_final.reshape(D), jnp.concatenate(dg_mix[1:]),
                            jnp.concatenate(dg_ffn[1:]), jnp.concatenate(dg_mem[1:]), jnp.stack(d_ws), jnp.stack(d_sb),
                            jnp.stack(d_gv), jnp.stack(d_conv_b[1:]), jnp.stack(d_conv_w[1:])]
            early_state, tok = small_start("early", early_arrays, [dx])
            pipe["deps"].append(tok)
        deps = take_deps()
        dact = _mm_dx_full(f"ddown{i}", dx16, lw["down"], tm=1024, tko=FF // 4, tc=D, deps=deps)
        big["down"][i] = _mm_dw(f"wdown{i}", sv["act"], dx16, deps=deps).reshape(N_DEV, FF // N_DEV, D)
        da3, dcw, dcb = _conv_bwd(f"dconv{i}", sv["a3"], cw, cb, dact)
        d_conv_w[i] = dcw.transpose(1, 0, 2).reshape(3, 2 * FF)
        d_conv_b[i] = dcb.reshape(2 * FF)
        advance(da3)
        deps = take_deps()
        dh2 = _mm_dx_gcols(f"dup{i}", da3, lw["up"], split_in=True, deps=deps)
        big["up"][i] = _mm_dw_gcols(f"wup{i}", sv["h2t"], da3, N_DEV, split_in=True, deps=deps, a_t=True)
        dx1, dg_ffn[i], dx1_16 = _rms_bwd(f"dffnnorm{i}", dh2, sv["x1"], ffn_norm_g[i], dx)
        submit(f"f{i}", ffn_members, [big["up"][i], big["down"][i]])
        deps = take_deps()
        if i % 2 == 0:
            dcat = _mm_dx_gcols(f"daout{i}", dx1_16, lw["out"], deps=deps)
            big["aout"][j] = _mm_dw_gcols(f"waout{i}", sv["cat"], dx1_16, N_DEV, deps=deps)
            parts = [None] * 9
            for g in range(len(A_PATTERNS)):
                dq, dk, dv = _attn_bwd(f"dattn{i}_{g}", sv["proj"], g, dcat, 0, sv["tok"], sv["lse"])
                parts[g], parts[3 + g], parts[6 + g] = dq, dk, dv
            dqm, dkv = _mem_bwd(f"dmemattn{i}", sv["proj"], a_q_blk, sv["kv"], dcat, 1)
            dproj = jnp.concatenate(parts + [dqm], axis=1)
            advance(dkv)
            deps = take_deps()
            dh1 = _mm_dx_gcols(f"dain{i}", dproj, lw["in"], deps=deps)
        else:
            dcat = _mm_dx_full(f"dbout{i}", dx1_16, lw["out"], tm=1024, deps=deps)
            big["bout"][j] = _mm_dw(f"wbout{i}", sv["cat"], dx1_16, deps=deps).reshape(
                N_DEV, (B_W + GW) // N_DEV, D)
            bias_t = b_s_bias[j].T
            du, dvp, dgv, dws, dbt = _sgu_bwd(f"dsgu{i}", sv["proj"], gv_full[j], b_w_s[j], bias_t, dcat)
            d_gv[j], d_ws[j], d_sb[j] = dgv.reshape(B_W), dws, dbt.T
            dqm, dkv = _mem_bwd(f"dmemattn{i}", sv["proj"], b_q_blk, sv["kv"], dcat, B_W // GW)
            dproj = jnp.concatenate([du, dvp, dqm], axis=1)
            advance(dkv)
            deps = take_deps()
            dh1 = _mm_dx_full(f"dbin{i}", dproj, lw["in"], tc=b_in // 2, deps=deps)
        dmemn = _mm_dx_full(f"dkvproj{i}", dkv, lw["kv"], tko=1024)
        _, dg_mem[i], _ = _rms_bwd(f"dmemnorm{i}", dmemn, mem0, mem_norm_g[i])
        dx, dg_mix[i], dx16 = _rms_bwd(f"dmixnorm{i}", dh1, sv["x0"], mix_norm_g[i], dx1)
        if i == 0:
            late_state, late_tok = small_start("late", late_small(), [dx])
            deps = deps + [late_tok]
        if i % 2 == 0:
            big["ain"][j] = _mm_dw_gcols(f"wain{i}", sv["h1t"], dproj, N_DEV, deps=deps, a_t=True)
        else:
            dwin = _mm_dw(f"wbin{i}", sv["h1t"], dproj, tko=1024, tn=512, deps=deps, a_t=True)
            big["bin"][j] = dwin.reshape(D, N_DEV, b_in // N_DEV).transpose(1, 0, 2)
        big["kv"][i] = _mm_dw(f"wkv{i}", sv["memn"], dkv, tko=1024, deps=deps).reshape(N_DEV, D // N_DEV, 2 * GW)
        submit(f"m{i}", mix_members, [big[t][l] for t, l in mix_members])
    grad_x = dx[None]

    last_chips, toks = [], []
    for st in pipe["pair"]:
        new, tok = pair_end_chip_begin(st, [dx])
        last_chips.append(new)
        toks.append(tok)
    g_early = small_end("early", early_state, [a.shape for a in early_arrays], [dx])
    for st in pipe["chip"]:
        chip_end_update(st, [g_early[1]], toks)
    g_late = small_end("late", late_state, [a.shape for a in late_small()], list(adamw_tokens))

    loss = g_early[0][0]
    layer0 = dict(zip(("mix", "ffn", "mem", "conv_b", "conv_w"), g_late))
    rest = dict(zip(("final", "mix", "ffn", "mem", "w_s", "s_bias", "gv", "conv_b", "conv_w"), g_early[1:]))
    g_cw_full = jnp.concatenate([layer0["conv_w"], rest["conv_w"]])
    g_gv = lax.dynamic_slice_in_dim(rest["gv"], dev * b_v_norm_g.shape[1], b_v_norm_g.shape[1], axis=1)
    g_cw = lax.dynamic_slice_in_dim(g_cw_full, dev * ffn_conv_w.shape[2], ffn_conv_w.shape[2], axis=2)
    g_all = [jnp.concatenate([layer0["mix"], rest["mix"]]), jnp.concatenate([layer0["ffn"], rest["ffn"]]),
             jnp.concatenate([layer0["mem"], rest["mem"]]), rest["w_s"], rest["s_bias"],
             jnp.concatenate([layer0["conv_b"], rest["conv_b"]]), rest["final"], g_gv, g_cw]
    names = ["mix_norm_g", "ffn_norm_g", "mem_norm_g", "b_w_s", "b_s_bias", "ffn_conv_b", "final_norm_g",
             "b_v_norm_g", "ffn_conv_w"]
    ws = [mix_norm_g, ffn_norm_g, mem_norm_g, b_w_s, b_s_bias, ffn_conv_b, final_norm_g, b_v_norm_g, ffn_conv_w]
    ms = [m_mix_norm_g, m_ffn_norm_g, m_mem_norm_g, m_b_w_s, m_b_s_bias, m_ffn_conv_b, m_final_norm_g,
          m_b_v_norm_g, m_ffn_conv_w]
    vs = [v_mix_norm_g, v_ffn_norm_g, v_mem_norm_g, v_b_w_s, v_b_s_bias, v_ffn_conv_b, v_final_norm_g,
          v_b_v_norm_g, v_ffn_conv_w]
    shapes = [w.shape for w in ws]
    d_p, m_p, v_p = _adamw_flat("adamw_small", _pack(ws), _pack(g_all), _pack(ms), _pack(vs))
    res = {}
    for n, g, d, nm, nv in zip(names, g_all, _unpack(d_p, shapes), _unpack(m_p, shapes), _unpack(v_p, shapes)):
        res[n] = [g, d, nm, nv]
    for st in last_chips:
        chip_end_update(st, [d_p] + list(adamw_tokens))
    for tag, name in (("kv", "w_mem_kv"), ("ain", "a_w_in"), ("aout", "a_w_out"), ("bin", "b_w_in"),
                      ("bout", "b_w_out"), ("up", "ffn_w_up"), ("down", "ffn_w_down")):
        res[name] = list(chain[tag])

    order = ["mix_norm_g", "ffn_norm_g", "mem_norm_g", "w_mem_kv", "a_w_in", "a_w_out", "b_w_in", "b_v_norm_g",
             "b_w_s", "b_s_bias", "b_w_out", "ffn_w_up", "ffn_conv_w", "ffn_conv_b", "ffn_w_down", "final_norm_g"]
    return (loss, grad_x, *[res[n][0] for n in order], *[res[n][1] for n in order],
            *[res[n][2] for n in order], *[res[n][3] for n in order])
```

```python
import functools

import numpy as np
import jax
import jax.numpy as jnp
from jax import lax
from jax.experimental import pallas as pl
from jax.experimental.pallas import tpu as pltpu

F32 = jnp.float32
BF16 = jnp.bfloat16
MESH = pl.DeviceIdType.MESH
N_DEV = 8

EPS = 1e-6
NEG = -1e30
HEAD = 128
HPG = 4
GW = HPG * HEAD
A_PATTERNS = ((128, 1), (512, 4), (2048, 16))
A_HEADS = HPG * len(A_PATTERNS)
QBLK = 128
B_GROUPS = 12
B_W = B_GROUPS * HEAD
SLOPES = (2.0 ** (-8.0 * (np.arange(A_HEADS) + 1) / A_HEADS)).astype(np.float32)
SCALE = HEAD ** -0.5

ADAM_LR = 0.001
ADAM_B1 = 0.9
ADAM_B2 = 0.999
ADAM_EPS = 1e-08
ADAM_WD = 0.01
ADAM_STEP = 10

V7X_VMEM_LIMIT = 50 * 1024 * 1024

NN = (((1,), (0,)), ((), ()))
NT = (((1,), (1,)), ((), ()))
TN = (((0,), (0,)), ((), ()))


def _cp(*sem):
    return pltpu.CompilerParams(dimension_semantics=sem, vmem_limit_bytes=V7X_VMEM_LIMIT)


def _dot(a, b, dims=NN):
    return lax.dot_general(a.astype(BF16), b.astype(BF16), dims, preferred_element_type=F32)


def _tile(n, pref):
    t = min(n, pref)
    assert n % t == 0, (n, pref)
    return t


def _row_tile(rows, cols, mib=1):
    best = None
    for t in range(16, rows + 1, 16):
        if rows % t == 0 and t * cols * 4 <= (mib << 20):
            best = t
    if best is None:
        best = rows
    return best


_DEP = pl.BlockSpec((8, 128), lambda *_: (0, 0))


def _matmul(name, dims, grid, a, a_spec, b, b_spec, out_shape, o_spec, tile, res=None, res_spec=None, deps=()):
    nk = grid[2]
    has_res = res is not None

    def body(*refs):
        a_ref, b_ref = refs[0], refs[1]
        r_ref = refs[2] if has_res else None
        o_ref, acc_ref = refs[-2], refs[-1]
        part = _dot(a_ref[...], b_ref[...], dims)

        def finish(val):
            if has_res:
                val = val + r_ref[...]
            o_ref[...] = val.astype(o_ref.dtype)

        if nk == 1:
            finish(part)
        else:
            k = pl.program_id(2)

            @pl.when(k == 0)
            def _():
                acc_ref[...] = part

            @pl.when(k > 0)
            def _():
                acc_ref[...] += part

            @pl.when(k == nk - 1)
            def _():
                finish(acc_ref[...])

    ins = [a, b] + ([res] if has_res else []) + list(deps)
    specs = [a_spec, b_spec] + ([res_spec] if has_res else []) + [_DEP] * len(deps)
    return pl.pallas_call(
        body, name=name, grid=grid, in_specs=specs, out_specs=o_spec, out_shape=out_shape,
        scratch_shapes=[pltpu.VMEM(tile if nk > 1 else (8, 128), F32)],
        compiler_params=_cp("parallel", "parallel", "arbitrary"))(*ins)


def _mm_full(name, a, w, res=None, tm=1024, tn=512, tk=2048, deps=()):
    M, K = a.shape
    N = w.shape[1]
    tm, tn, tk = _tile(M, tm), _tile(N, tn), _tile(K, tk)
    return _matmul(
        name, NN, (N // tn, M // tm, K // tk),
        a, pl.BlockSpec((tm, tk), lambda j, i, k: (i, k)),
        w, pl.BlockSpec((tk, tn), lambda j, i, k: (k, j)),
        jax.ShapeDtypeStruct((M, N), F32), pl.BlockSpec((tm, tn), lambda j, i, k: (i, j)), (tm, tn),
        res, pl.BlockSpec((tm, tn), lambda j, i, k: (i, j)), deps=deps)


def _mm_gcols(name, a, wg, res=None, split_out=False, tm=1024, deps=(), out_dtype=F32):
    M, K = a.shape
    G, _, Nl = wg.shape
    tm = _tile(M, tm)
    hg = G // 2
    if split_out:
        shape = jax.ShapeDtypeStruct((2, M, hg * Nl), out_dtype)
        o_spec = pl.BlockSpec((None, tm, Nl), lambda g, i, k: (g // hg, i, g % hg))
    else:
        shape = jax.ShapeDtypeStruct((M, G * Nl), out_dtype)
        o_spec = pl.BlockSpec((tm, Nl), lambda g, i, k: (i, g))
    return _matmul(
        name, NN, (G, M // tm, 1),
        a, pl.BlockSpec((tm, K), lambda g, i, k: (i, 0)),
        wg, pl.BlockSpec((None, K, Nl), lambda g, i, k: (g, 0, 0)),
        shape, o_spec, (tm, Nl),
        res, pl.BlockSpec((tm, Nl), lambda g, i, k: (i, g)), deps=deps)


def _mm_dx_full(name, dy, w, tm=512, tko=512, tc=2048, deps=()):
    M, N = dy.shape
    K = w.shape[0]
    tm, tko, tc = _tile(M, tm), _tile(K, tko), _tile(N, tc)
    return _matmul(
        name, NT, (K // tko, M // tm, N // tc),
        dy, pl.BlockSpec((tm, tc), lambda j, i, k: (i, k)),
        w, pl.BlockSpec((tko, tc), lambda j, i, k: (j, k)),
        jax.ShapeDtypeStruct((M, K), F32), pl.BlockSpec((tm, tko), lambda j, i, k: (i, j)), (tm, tko), deps=deps)


def _mm_dx_gcols(name, dy, wg, split_in=False, tm=1024, tko=1024, deps=()):
    G, K, Nl = wg.shape
    M = dy.shape[-2]
    tm, tko = _tile(M, tm), _tile(K, tko)
    hg = G // 2
    if split_in:
        dy_spec = pl.BlockSpec((None, tm, Nl), lambda j, i, g: (g // hg, i, g % hg))
    else:
        dy_spec = pl.BlockSpec((tm, Nl), lambda j, i, g: (i, g))
    return _matmul(
        name, NT, (K // tko, M // tm, G),
        dy, dy_spec,
        wg, pl.BlockSpec((None, tko, Nl), lambda j, i, g: (g, j, 0)),
        jax.ShapeDtypeStruct((M, K), F32), pl.BlockSpec((tm, tko), lambda j, i, g: (i, j)), (tm, tko), deps=deps)


def _lhs_of_dw(a_t, ts, tko, index):
    if a_t:
        return NN, pl.BlockSpec((tko, ts), lambda *ids: index(*ids))
    return TN, pl.BlockSpec((ts, tko), lambda *ids: index(*ids)[::-1])


def _mm_dw(name, a, dy, tko=512, tn=1024, ts=2048, deps=(), a_t=False):
    K1, S = a.shape if a_t else a.shape[::-1]
    N = dy.shape[1]
    tko, tn, ts = _tile(K1, tko), _tile(N, tn), _tile(S, ts)
    dims, a_spec = _lhs_of_dw(a_t, ts, tko, lambda i, j, k: (j, k))
    return _matmul(
        name, dims, (N // tn, K1 // tko, S // ts),
        a, a_spec,
        dy, pl.BlockSpec((ts, tn), lambda i, j, k: (k, i)),
        jax.ShapeDtypeStruct((K1, N), BF16), pl.BlockSpec((tko, tn), lambda i, j, k: (j, i)), (tko, tn), deps=deps)


def _mm_dw_gcols(name, a, dy, G, split_in=False, tko=1024, ts=2048, deps=(), a_t=False):
    K1, S = a.shape if a_t else a.shape[::-1]
    Nl = (dy.shape[-1] * (2 if split_in else 1)) // G
    tko, ts = _tile(K1, tko), _tile(S, ts)
    hg = G // 2
    dims, a_spec = _lhs_of_dw(a_t, ts, tko, lambda g, j, k: (j, k))
    if split_in:
        dy_spec = pl.BlockSpec((None, ts, Nl), lambda g, j, k: (g // hg, k, g % hg))
    else:
        dy_spec = pl.BlockSpec((ts, Nl), lambda g, j, k: (k, g))
    return _matmul(
        name, dims, (G, K1 // tko, S // ts),
        a, a_spec,
        dy, dy_spec,
        jax.ShapeDtypeStruct((G, K1, Nl), BF16), pl.BlockSpec((None, tko, Nl), lambda g, j, k: (g, j, 0)),
        (tko, Nl), deps=deps)


def _rms_fwd(name, x, g, tr=256, deps=(), with_t=False):
    S, D = x.shape
    tr = _tile(S, tr)

    def body(x_ref, g_ref, *rest):
        xf = x_ref[...]
        r = lax.rsqrt(jnp.mean(xf * xf, axis=-1, keepdims=True) + EPS)
        y = xf * r * g_ref[...]
        if with_t:
            rest[-2][...] = y.astype(BF16)
            rest[-1][...] = y.T.astype(BF16)
        else:
            rest[-1][...] = y.astype(BF16)

    row = pl.BlockSpec((tr, D), lambda i: (i, 0))
    out_specs, out_shape = row, jax.ShapeDtypeStruct((S, D), BF16)
    if with_t:
        out_specs = [row, pl.BlockSpec((D, tr), lambda i: (0, i))]
        out_shape = [out_shape, jax.ShapeDtypeStruct((D, S), BF16)]
    return pl.pallas_call(
        body, name=name, grid=(S // tr,),
        in_specs=[row, pl.BlockSpec((1, D), lambda i: (0, 0))] + [_DEP] * len(deps),
        out_specs=out_specs, out_shape=out_shape, compiler_params=_cp("parallel"))(x, g.reshape(1, D), *deps)


def _rms_bwd(name, dh, x, g, dres=None, tr=256):
    S, D = x.shape
    tr = _tile(S, tr)
    has_res = dres is not None

    def body(*refs):
        dh_ref, x_ref, g_ref = refs[:3]
        dres_ref = refs[3] if has_res else None
        dx_ref, dg_ref, dx16_ref = refs[-3:]
        xf = x_ref[...]
        r = lax.rsqrt(jnp.mean(xf * xf, axis=-1, keepdims=True) + EPS)
        xh = xf * r
        dhv = dh_ref[...]
        dxh = dhv * g_ref[...]
        dx = r * (dxh - xh * jnp.mean(dxh * xh, axis=-1, keepdims=True))
        if has_res:
            dx = dx + dres_ref[...]
        dx_ref[...] = dx
        dx16_ref[...] = dx.astype(BF16)
        part = jnp.sum(dhv * xh, axis=0, keepdims=True)
        i = pl.program_id(0)

        @pl.when(i == 0)
        def _():
            dg_ref[...] = part

        @pl.when(i > 0)
        def _():
            dg_ref[...] += part

    row = pl.BlockSpec((tr, D), lambda i: (i, 0))
    vec = pl.BlockSpec((1, D), lambda i: (0, 0))
    ins = [dh, x, g.reshape(1, D)] + ([dres] if has_res else [])
    return pl.pallas_call(
        body, name=name, grid=(S // tr,),
        in_specs=[row, row, vec] + ([row] if has_res else []),
        out_specs=[row, vec, row],
        out_shape=[jax.ShapeDtypeStruct((S, D), F32), jax.ShapeDtypeStruct((1, D), F32),
                   jax.ShapeDtypeStruct((S, D), BF16)],
        compiler_params=_cp("arbitrary"))(*ins)


def _final(name, x, tgt, g, tr=256):
    S, D = x.shape
    tr = _tile(S, tr)

    def body(x_ref, t_ref, g_ref, dx_ref, dg_ref, loss_ref, dx16_ref):
        xf = x_ref[...]
        gv = g_ref[...]
        r = lax.rsqrt(jnp.mean(xf * xf, axis=-1, keepdims=True) + EPS)
        xh = xf * r
        err = xh * gv - t_ref[...]
        sq = jnp.sum(jnp.sum(err * err, axis=1, keepdims=True), axis=0, keepdims=True)
        dy = err * (1.0 / D)
        dxh = dy * gv
        dx = r * (dxh - xh * jnp.mean(dxh * xh, axis=-1, keepdims=True))
        dx_ref[...] = dx
        dx16_ref[...] = dx.astype(BF16)
        part = jnp.sum(dy * xh, axis=0, keepdims=True)
        lpart = jnp.broadcast_to(sq, (8, 128))
        i = pl.program_id(0)

        @pl.when(i == 0)
        def _():
            dg_ref[...] = part
            loss_ref[...] = lpart

        @pl.when(i > 0)
        def _():
            dg_ref[...] += part
            loss_ref[...] += lpart

    row = pl.BlockSpec((tr, D), lambda i: (i, 0))
    vec = pl.BlockSpec((1, D), lambda i: (0, 0))
    return pl.pallas_call(
        body, name=name, grid=(S // tr,), in_specs=[row, row, vec],
        out_specs=[row, vec, pl.BlockSpec((8, 128), lambda i: (0, 0)), row],
        out_shape=[jax.ShapeDtypeStruct((S, D), F32), jax.ShapeDtypeStruct((1, D), F32),
                   jax.ShapeDtypeStruct((8, 128), F32), jax.ShapeDtypeStruct((S, D), BF16)],
        compiler_params=_cp("arbitrary"))(x, tgt, g.reshape(1, D))


def _band_specs(nb, col_of):
    prev = pl.BlockSpec((QBLK, GW), lambda r, b: (jnp.maximum(b - 1, 0), col_of(r)))
    cur = pl.BlockSpec((QBLK, GW), lambda r, b: (b, col_of(r)))
    nxt = pl.BlockSpec((QBLK, GW), lambda r, b: (jnp.minimum(b + 1, nb - 1), col_of(r)))
    return [prev, cur, nxt]


HALF = QBLK // 2
WIN = 2 * QBLK


def _cat3(refs, sl):
    prev, cur, nxt = refs
    return jnp.concatenate([prev[HALF:, sl], cur[:, sl], nxt[:HALF, sl]], axis=0)


def _group_view(proj, g):
    _, dil = A_PATTERNS[g]
    S, C = proj.shape
    ng = len(A_PATTERNS)
    if dil == 1:
        return proj, lambda which, r: which * ng + g
    cols = [proj[:, (which * ng + g) * GW:(which * ng + g + 1) * GW] for which in range(3)]
    return jnp.concatenate(cols, axis=1).reshape(S // dil, dil * 3 * GW), lambda which, r: r * 3 + which


def _attn_fwd(name, proj, g):
    window, dil = A_PATTERNS[g]
    n_side = (window // 2) // dil
    S, C = proj.shape
    L = S // dil
    nb = L // QBLK
    pv, col = _group_view(proj, g)

    assert n_side == HALF

    def body(q_ref, kp, kc, kn, vp, vc, vn, o_ref, lse_ref):
        b = pl.program_id(1)
        jq = b * QBLK + lax.broadcasted_iota(jnp.int32, (QBLK, WIN), 0)
        jk = b * QBLK - HALF + lax.broadcasted_iota(jnp.int32, (QBLK, WIN), 1)
        rel = jnp.abs(jk - jq)
        mask = (rel <= n_side) & (jk >= 0) & (jk < L)
        dist = rel.astype(F32) * float(dil)
        for hh in range(HPG):
            sl = slice(hh * HEAD, (hh + 1) * HEAD)
            k = _cat3((kp, kc, kn), sl)
            v = _cat3((vp, vc, vn), sl)
            s = _dot(q_ref[:, sl], k, NT) * SCALE - float(SLOPES[g * HPG + hh]) * dist
            s = jnp.where(mask, s, NEG)
            m = jnp.max(s, axis=1, keepdims=True)
            p = jnp.exp(s - m)
            l = jnp.sum(p, axis=1, keepdims=True)
            o_ref[:, sl] = _dot(p, v) / l
            lse_ref[:, sl] = jnp.broadcast_to(m + jnp.log(l), (QBLK, HEAD))

    q_spec = pl.BlockSpec((QBLK, GW), lambda r, b: (b, col(0, r)))
    k_specs = _band_specs(nb, lambda r: col(1, r))
    v_specs = _band_specs(nb, lambda r: col(2, r))
    o_spec = pl.BlockSpec((QBLK, GW), lambda r, b: (b, r))
    shape = jax.ShapeDtypeStruct((L, dil * GW), F32)
    o, lse = pl.pallas_call(
        body, name=name, grid=(dil, nb), in_specs=[q_spec] + k_specs + v_specs,
        out_specs=[o_spec, o_spec], out_shape=[shape, shape],
        compiler_params=_cp("parallel", "parallel"))(pv, pv, pv, pv, pv, pv, pv)
    return o.reshape(S, GW), lse.reshape(S, GW)


def _attn_merge(name, outs, lses, tr=256):
    S = outs[0].shape[0]
    tr = _tile(S, tr)
    ng = len(outs)

    def body(*refs):
        o_refs, l_refs = refs[:ng], refs[ng:2 * ng]
        tok_ref, lse_ref = refs[-2], refs[-1]
        ls = [r[...] for r in l_refs]
        m = functools.reduce(jnp.maximum, ls)
        es = [jnp.exp(l - m) for l in ls]
        tot = functools.reduce(lambda a, b: a + b, es)
        acc = None
        for e, o_ref in zip(es, o_refs):
            term = (e / tot) * o_ref[...]
            acc = term if acc is None else acc + term
        tok_ref[...] = acc
        lse_ref[...] = m + jnp.log(tot)

    row = pl.BlockSpec((tr, GW), lambda i: (i, 0))
    shape = jax.ShapeDtypeStruct((S, GW), F32)
    return pl.pallas_call(
        body, name=name, grid=(S // tr,), in_specs=[row] * (2 * ng), out_specs=[row, row],
        out_shape=[shape, shape], compiler_params=_cp("parallel"))(*outs, *lses)


def _attn_bwd(name, proj, g, dtok_src, dtok_blk, tok, lse):
    window, dil = A_PATTERNS[g]
    n_side = (window // 2) // dil
    S, C = proj.shape
    L = S // dil
    nb = L // QBLK
    pv, col = _group_view(proj, g)
    assert n_side == HALF
    if dil == 1:
        dcb, dv_ = dtok_src.shape[1] // GW, dtok_src
    else:
        dcb, dv_ = 1, dtok_src[:, dtok_blk * GW:(dtok_blk + 1) * GW].reshape(L, dil * GW)
        dtok_blk = 0
    ov = tok.reshape(L, dil * GW)
    lv = lse.reshape(L, dil * GW)

    def body(qp, qc, qn, kp, kc, kn, vp, vc, vn, dop, doc, don, op, oc, on, lp, lc, ln,
             dq_ref, dk_ref, dv_ref):
        b = pl.program_id(1)
        jq = b * QBLK + lax.broadcasted_iota(jnp.int32, (QBLK, WIN), 0)
        jk = b * QBLK - HALF + lax.broadcasted_iota(jnp.int32, (QBLK, WIN), 1)
        rel = jnp.abs(jk - jq)
        mask = (rel <= n_side) & (jk >= 0) & (jk < L)
        dist = rel.astype(F32) * float(dil)
        jq3 = b * QBLK - HALF + lax.broadcasted_iota(jnp.int32, (WIN, QBLK), 0)
        jk1 = b * QBLK + lax.broadcasted_iota(jnp.int32, (WIN, QBLK), 1)
        rel3 = jnp.abs(jk1 - jq3)
        mask3 = (rel3 <= n_side) & (jq3 >= 0) & (jq3 < L)
        dist3 = rel3.astype(F32) * float(dil)
        for hh in range(HPG):
            sl = slice(hh * HEAD, (hh + 1) * HEAD)
            one = slice(hh * HEAD, hh * HEAD + 1)
            slope = float(SLOPES[g * HPG + hh])
            q = qc[:, sl]
            do = doc[:, sl]
            k3 = _cat3((kp, kc, kn), sl)
            v3 = _cat3((vp, vc, vn), sl)
            delta = jnp.sum(do * oc[:, sl], axis=1, keepdims=True)
            s = _dot(q, k3, NT) * SCALE - slope * dist
            p = jnp.where(mask, jnp.exp(s - lc[:, one]), 0.0)
            ds = p * (_dot(do, v3, NT) - delta)
            dq_ref[:, sl] = (_dot(ds, k3) * SCALE).astype(dq_ref.dtype)

            q3 = _cat3((qp, qc, qn), sl)
            do3 = _cat3((dop, doc, don), sl)
            o3 = _cat3((op, oc, on), sl)
            lse3 = _cat3((lp, lc, ln), sl)[:, :1]
            delta3 = jnp.sum(do3 * o3, axis=1, keepdims=True)
            k = kc[:, sl]
            v = vc[:, sl]
            s3 = _dot(q3, k, NT) * SCALE - slope * dist3
            p3 = jnp.where(mask3, jnp.exp(s3 - lse3), 0.0)
            ds3 = p3 * (_dot(do3, v, NT) - delta3)
            dv_ref[:, sl] = _dot(p3, do3, TN).astype(dv_ref.dtype)
            dk_ref[:, sl] = (_dot(ds3, q3, TN) * SCALE).astype(dk_ref.dtype)

    specs = (_band_specs(nb, lambda r: col(0, r)) + _band_specs(nb, lambda r: col(1, r))
             + _band_specs(nb, lambda r: col(2, r))
             + _band_specs(nb, lambda r: r * dcb + dtok_blk)
             + _band_specs(nb, lambda r: r) + _band_specs(nb, lambda r: r))
    o_spec = pl.BlockSpec((QBLK, GW), lambda r, b: (b, r))
    shape = jax.ShapeDtypeStruct((L, dil * GW), BF16)
    outs = pl.pallas_call(
        body, name=name, grid=(dil, nb), in_specs=specs, out_specs=[o_spec] * 3, out_shape=[shape] * 3,
        compiler_params=_cp("parallel", "parallel"))(*([pv] * 9 + [dv_] * 3 + [ov] * 3 + [lv] * 3))
    return [o.reshape(S, GW) for o in outs]


def _mem_fwd(name, proj, q_blk, kv, tq=256):
    S = proj.shape[0]
    M = kv.shape[0]
    tq = _tile(S, tq)

    def body(q_ref, kv_ref, o_ref):
        for hh in range(HPG):
            sl = slice(hh * HEAD, (hh + 1) * HEAD)
            k = kv_ref[:, sl]
            v = kv_ref[:, GW + hh * HEAD:GW + (hh + 1) * HEAD]
            s = _dot(q_ref[:, sl], k, NT) * SCALE
            m = jnp.max(s, axis=1, keepdims=True)
            p = jnp.exp(s - m)
            p = p / jnp.sum(p, axis=1, keepdims=True)
            o_ref[:, sl] = _dot(p, v)

    return pl.pallas_call(
        body, name=name, grid=(S // tq,),
        in_specs=[pl.BlockSpec((tq, GW), lambda i: (i, q_blk)), pl.BlockSpec((M, 2 * GW), lambda i: (0, 0))],
        out_specs=pl.BlockSpec((tq, GW), lambda i: (i, 0)),
        out_shape=jax.ShapeDtypeStruct((S, GW), F32), compiler_params=_cp("parallel"))(proj, kv)


def _mem_bwd(name, proj, q_blk, kv, dcat, do_blk, tq=256, deps=()):
    S = proj.shape[0]
    M = kv.shape[0]
    tq = _tile(S, tq)

    def body(q_ref, kv_ref, do_ref, *rest):
        dq_ref, dkv_ref = rest[-2:]
        i = pl.program_id(0)
        for hh in range(HPG):
            sl = slice(hh * HEAD, (hh + 1) * HEAD)
            vsl = slice(GW + hh * HEAD, GW + (hh + 1) * HEAD)
            q = q_ref[:, sl]
            do = do_ref[:, sl]
            k = kv_ref[:, sl]
            v = kv_ref[:, vsl]
            s = _dot(q, k, NT) * SCALE
            m = jnp.max(s, axis=1, keepdims=True)
            p = jnp.exp(s - m)
            p = p / jnp.sum(p, axis=1, keepdims=True)
            dp = _dot(do, v, NT)
            ds = p * (dp - jnp.sum(dp * p, axis=1, keepdims=True))
            dq_ref[:, sl] = (_dot(ds, k) * SCALE).astype(dq_ref.dtype)
            dk = _dot(ds, q, TN) * SCALE
            dvv = _dot(p, do, TN)

            @pl.when(i == 0)
            def _():
                dkv_ref[:, sl] = dk
                dkv_ref[:, vsl] = dvv

            @pl.when(i > 0)
            def _():
                dkv_ref[:, sl] += dk
                dkv_ref[:, vsl] += dvv

    return pl.pallas_call(
        body, name=name, grid=(S // tq,),
        in_specs=[pl.BlockSpec((tq, GW), lambda i: (i, q_blk)), pl.BlockSpec((M, 2 * GW), lambda i: (0, 0)),
                  pl.BlockSpec((tq, GW), lambda i: (i, do_blk))] + [_DEP] * len(deps),
        out_specs=[pl.BlockSpec((tq, GW), lambda i: (i, 0)), pl.BlockSpec((M, 2 * GW), lambda i: (0, 0))],
        out_shape=[jax.ShapeDtypeStruct((S, GW), BF16), jax.ShapeDtypeStruct((M, 2 * GW), F32)],
        compiler_params=_cp("arbitrary"))(proj, kv, dcat, *deps)


_RSQRT2 = float(1.0 / np.sqrt(2.0))
_RSQRT2PI = float(1.0 / np.sqrt(2.0 * np.pi))


def _gelu(x):
    return 0.5 * x * (1.0 + lax.erf(x * _RSQRT2))


def _gelu_and_grad(x):
    cdf = 0.5 * (1.0 + lax.erf(x * _RSQRT2))
    return x * cdf, cdf + x * jnp.exp(-0.5 * x * x) * _RSQRT2PI


def _sgu_fwd(name, proj, gv, w_s, bias_t):
    S = proj.shape[0]
    nch = S // HEAD

    def body(u_ref, v_ref, gv_ref, ws_ref, b_ref, o_ref):
        v = _gelu(v_ref[...])
        r = lax.rsqrt(jnp.mean(v * v, axis=-1, keepdims=True) + EPS)
        vn = v * r * gv_ref[...]
        for gg in range(B_GROUPS):
            sl = slice(gg * HEAD, (gg + 1) * HEAD)
            mixed = _dot(ws_ref[gg], vn[:, sl]) + b_ref[:, gg:gg + 1]
            o_ref[:, sl] = _gelu(u_ref[:, sl]) * mixed

    return pl.pallas_call(
        body, name=name, grid=(nch,),
        in_specs=[pl.BlockSpec((HEAD, B_W), lambda c: (c, 0)), pl.BlockSpec((HEAD, B_W), lambda c: (c, 1)),
                  pl.BlockSpec((1, B_W), lambda c: (0, 0)),
                  pl.BlockSpec((B_GROUPS, HEAD, HEAD), lambda c: (0, 0, 0)),
                  pl.BlockSpec((HEAD, B_GROUPS), lambda c: (0, 0))],
        out_specs=pl.BlockSpec((HEAD, B_W), lambda c: (c, 0)),
        out_shape=jax.ShapeDtypeStruct((S, B_W), F32),
        compiler_params=_cp("parallel"))(proj, proj, gv.reshape(1, B_W), w_s, bias_t)


def _sgu_bwd(name, proj, gv, w_s, bias_t, dcat):
    S = proj.shape[0]
    nch = S // HEAD

    def body(u_ref, v_ref, gv_ref, ws_ref, b_ref, dt_ref, du_ref, dvp_ref, dgv_ref, dws_ref, db_ref, dvn_ref):
        c = pl.program_id(0)
        vpre = v_ref[...]
        v, v_slope = _gelu_and_grad(vpre)
        r = lax.rsqrt(jnp.mean(v * v, axis=-1, keepdims=True) + EPS)
        vh = v * r
        gvv = gv_ref[...]
        vn = vh * gvv
        for gg in range(B_GROUPS):
            sl = slice(gg * HEAD, (gg + 1) * HEAD)
            upre = u_ref[:, sl]
            dt = dt_ref[:, sl]
            vng = vn[:, sl]
            mixed = _dot(ws_ref[gg], vng) + b_ref[:, gg:gg + 1]
            u, u_slope = _gelu_and_grad(upre)
            du_ref[:, sl] = (dt * mixed * u_slope).astype(du_ref.dtype)
            dmix = dt * u
            dvn_ref[:, sl] = _dot(ws_ref[gg], dmix, TN)
            dws = _dot(dmix, vng, NT)
            dbs = jnp.sum(dmix, axis=1, keepdims=True)

            @pl.when(c == 0)
            def _():
                dws_ref[gg] = dws
                db_ref[:, gg:gg + 1] = dbs

            @pl.when(c > 0)
            def _():
                dws_ref[gg] += dws
                db_ref[:, gg:gg + 1] += dbs

        dvn = dvn_ref[...]
        dgp = jnp.sum(dvn * vh, axis=0, keepdims=True)
        dvh = dvn * gvv
        dv = r * (dvh - vh * jnp.mean(dvh * vh, axis=-1, keepdims=True))
        dvp_ref[...] = (dv * v_slope).astype(dvp_ref.dtype)

        @pl.when(c == 0)
        def _():
            dgv_ref[...] = dgp

        @pl.when(c > 0)
        def _():
            dgv_ref[...] += dgp

    blk = lambda j: pl.BlockSpec((HEAD, B_W), lambda c: (c, j))
    vec = pl.BlockSpec((1, B_W), lambda c: (0, 0))
    ws_spec = pl.BlockSpec((B_GROUPS, HEAD, HEAD), lambda c: (0, 0, 0))
    b_spec = pl.BlockSpec((HEAD, B_GROUPS), lambda c: (0, 0))
    du, dvp, dgv, dws, db = pl.pallas_call(
        body, name=name, grid=(nch,),
        in_specs=[blk(0), blk(1), vec, ws_spec, b_spec, blk(0)],
        out_specs=[blk(0), blk(0), vec, ws_spec, b_spec],
        out_shape=[jax.ShapeDtypeStruct((S, B_W), BF16), jax.ShapeDtypeStruct((S, B_W), BF16),
                   jax.ShapeDtypeStruct((1, B_W), F32), jax.ShapeDtypeStruct((B_GROUPS, HEAD, HEAD), F32),
                   jax.ShapeDtypeStruct((HEAD, B_GROUPS), F32)],
        scratch_shapes=[pltpu.VMEM((HEAD, B_W), F32)],
        compiler_params=_cp("arbitrary"))(proj, proj, gv.reshape(1, B_W), w_s, bias_t, dcat)
    return du, dvp, dgv, dws, db


def _shift_down(a, row):
    return jnp.where(row == 0, 0.0, pltpu.roll(a, 1, 0))


def _shift_up(a, row):
    n = a.shape[0]
    return jnp.where(row == n - 1, 0.0, pltpu.roll(a, n - 1, 0))


def _conv(a, w, b, row):
    return _shift_down(a, row) * w[0:1] + a * w[1:2] + _shift_up(a, row) * w[2:3] + b


def _conv_fwd(name, a3, cw, cb, tc=256):
    _, S, FF = a3.shape
    tc = _tile(FF, tc)

    def body(a_ref, w_ref, b_ref, o_ref):
        row = lax.broadcasted_iota(jnp.int32, (S, tc), 0)
        cg = _conv(a_ref[0], w_ref[0], b_ref[0], row)
        cv = _conv(a_ref[1], w_ref[1], b_ref[1], row)
        o_ref[...] = (_gelu(cg) * cv).astype(o_ref.dtype)

    return pl.pallas_call(
        body, name=name, grid=(FF // tc,),
        in_specs=[pl.BlockSpec((2, S, tc), lambda j: (0, 0, j)), pl.BlockSpec((2, 3, tc), lambda j: (0, 0, j)),
                  pl.BlockSpec((2, 1, tc), lambda j: (0, 0, j))],
        out_specs=pl.BlockSpec((S, tc), lambda j: (0, j)),
        out_shape=jax.ShapeDtypeStruct((S, FF), BF16), compiler_params=_cp("parallel"))(a3, cw, cb)


def _conv_bwd(name, a3, cw, cb, dact, tc=128):
    _, S, FF = a3.shape
    tc = _tile(FF, tc)

    def body(a_ref, w_ref, b_ref, d_ref, da_ref, dw_ref, db_ref):
        row = lax.broadcasted_iota(jnp.int32, (S, tc), 0)
        ag, av = a_ref[0], a_ref[1]
        wg, wv = w_ref[0], w_ref[1]
        cg = _conv(ag, wg, b_ref[0], row)
        cv = _conv(av, wv, b_ref[1], row)
        d = d_ref[...]
        gate, gate_slope = _gelu_and_grad(cg)
        dcs = (d * cv * gate_slope, d * gate)
        for h, (dc, a, w) in enumerate(zip(dcs, (ag, av), (wg, wv))):
            da = _shift_up(dc, row) * w[0:1] + dc * w[1:2] + _shift_down(dc, row) * w[2:3]
            da_ref[h] = da.astype(da_ref.dtype)
            dw_ref[h, 0:1, :] = jnp.sum(dc * _shift_down(a, row), axis=0, keepdims=True)
            dw_ref[h, 1:2, :] = jnp.sum(dc * a, axis=0, keepdims=True)
            dw_ref[h, 2:3, :] = jnp.sum(dc * _shift_up(a, row), axis=0, keepdims=True)
            db_ref[h] = jnp.sum(dc, axis=0, keepdims=True)

    a_spec = pl.BlockSpec((2, S, tc), lambda j: (0, 0, j))
    w_spec = pl.BlockSpec((2, 3, tc), lambda j: (0, 0, j))
    b_spec = pl.BlockSpec((2, 1, tc), lambda j: (0, 0, j))
    return pl.pallas_call(
        body, name=name, grid=(FF // tc,),
        in_specs=[a_spec, w_spec, b_spec, pl.BlockSpec((S, tc), lambda j: (0, j))],
        out_specs=[a_spec, w_spec, b_spec],
        out_shape=[jax.ShapeDtypeStruct((2, S, FF), BF16), jax.ShapeDtypeStruct((2, 3, FF), F32),
                   jax.ShapeDtypeStruct((2, 1, FF), F32)],
        compiler_params=_cp("parallel"))(a3, cw, cb, dact)


_HBM = pl.BlockSpec(memory_space=pltpu.HBM)


def _position():
    return lax.axis_index("x"), lax.axis_index("y"), lax.axis_index("c")


_SEM =pl.BlockSpec(memory_space=pltpu.SEMAPHORE)
_EFFECT = pltpu.SideEffectType.DATAFLOW_SIDE_EFFECTING
_FLIPS = ((1, 0), (0, 1), (1, 1))


def _split_start(name, bufs, ncopy, plan, after=()):
    n = len(bufs)
    after = list(after)

    def body(*refs):
        ins = refs[:n]
        send_sems, recv_sems, token = refs[n + len(after)], refs[n + len(after) + 1], refs[-1]
        for i, (src, dst, to) in enumerate(plan(ins)):
            pltpu.make_async_remote_copy(src_ref=src, dst_ref=dst, send_sem=send_sems.at[i], recv_sem=recv_sems.at[i],
                                         device_id=to, device_id_type=MESH).start()
        token[...] = jnp.zeros_like(token)

    outs = pl.pallas_call(
        body, name=name,
        out_shape=(pltpu.SemaphoreType.DMA((ncopy,)), pltpu.SemaphoreType.DMA((ncopy,)),
                   *[pltpu.HBM(b.shape, b.dtype) for b in bufs], jax.ShapeDtypeStruct((8, 128), F32)),
        in_specs=[_HBM] * n + [pl.BlockSpec(memory_space=pl.ANY)] * len(after),
        out_specs=(_SEM, _SEM, *([_HBM] * n), pl.BlockSpec(memory_space=pltpu.VMEM)),
        input_output_aliases={i: 2 + i for i in range(n)},
        compiler_params=pltpu.CompilerParams(has_side_effects=_EFFECT),
    )(*[pltpu.with_memory_space_constraint(b, pltpu.HBM) for b in bufs], *after)
    return outs[0], outs[1], list(outs[2:2 + n]), outs[-1]


def _split_wait(name, bufs, send_sems, recv_sems, plan, after):
    n = len(bufs)
    after = list(after)

    def body(*refs):
        ins = refs[:n]
        ssem, rsem = refs[n], refs[n + 1]
        for i, (src, dst, to) in enumerate(plan(ins)):
            cp = pltpu.make_async_remote_copy(src_ref=src, dst_ref=dst, send_sem=ssem.at[i], recv_sem=rsem.at[i],
                                              device_id=to, device_id_type=MESH)
            cp.wait_send()
            cp.wait_recv()

    outs = pl.pallas_call(
        body, name=name, out_shape=tuple(pltpu.HBM(b.shape, b.dtype) for b in bufs),
        in_specs=[_HBM] * n + [_SEM, _SEM] + [pl.BlockSpec(memory_space=pl.ANY)] * len(after),
        out_specs=tuple([_HBM] * n), input_output_aliases={i: i for i in range(n)},
        compiler_params=pltpu.CompilerParams(has_side_effects=_EFFECT),
    )(*bufs, send_sems, recv_sems, *after)
    return list(outs)


def _gather_plan(refs):
    px, py, pc = _position()
    me = 4 * px + 2 * py + pc
    targets = [(px, py, 1 - pc), (1 - px, py, pc), (px, 1 - py, pc), (1 - px, 1 - py, pc)]
    return [(r.at[me], r.at[me], to) for r in refs for to in targets]


def _forward_plan(refs):
    px, py, pc = _position()
    out = []
    for r in refs:
        for fx, fy in _FLIPS:
            slot = 4 * (1 - px if fx else px) + 2 * (1 - py if fy else py) + pc
            out.append((r.at[slot], r.at[slot], (px, py, 1 - pc)))
    return out


def _pair_plan(n):
    def plan(refs):
        px, py, pc = _position()
        return [(refs[w].at[2 * k + (1 - pc)], refs[n + w].at[k], (px, py, 1 - pc)) for w in range(n) for k in range(4)]
    return plan


def _chip_plan(n):
    def plan(refs):
        px, py, pc = _position()
        out = []
        for w in range(n):
            for j, (fx, fy) in enumerate(_FLIPS):
                qx = 1 - px if fx else px
                qy = 1 - py if fy else py
                out.append((refs[w].at[2 * qx + qy], refs[n + w].at[j], (qx, qy, pc)))
        return out
    return plan


def _broadcast_plan(refs):
    px, py, pc = _position()
    me = 4 * px + 2 * py + pc
    flips = [(fx, fy, fc) for fx in (0, 1) for fy in (0, 1) for fc in (0, 1)][1:]
    targets = [(1 - px if fx else px, 1 - py if fy else py, 1 - pc if fc else pc) for fx, fy, fc in flips]
    return [(r.at[me], r.at[me], to) for r in refs for to in targets]


def _cast_place(name, dev, w, layer, dtype=BF16):
    nl, R, C = w.shape
    tr = _row_tile(R, C, 4)

    def body(dev_ref, w_ref, o_ref):
        o_ref[...] = w_ref[...].astype(o_ref.dtype)

    return pl.pallas_call(
        body, name=name,
        grid_spec=pltpu.PrefetchScalarGridSpec(
            num_scalar_prefetch=1, grid=(R // tr,),
            in_specs=[pl.BlockSpec((None, tr, C), lambda i, d: (layer, i, 0))],
            out_specs=pl.BlockSpec((None, tr, C), lambda i, d: (d[0], i, 0))),
        out_shape=jax.ShapeDtypeStruct((N_DEV, R, C), dtype), compiler_params=_cp("parallel"))(dev, w)


def _pair_sum(name, core, dw, recv):
    _, R, C = dw.shape
    tr = _row_tile(R, C, 4)
    dw4 = dw.reshape(4, 2, R, C)

    def body(core_ref, a_ref, b_ref, o_ref):
        o_ref[...] = (a_ref[...].astype(F32) + b_ref[...].astype(F32)).astype(o_ref.dtype)

    return pl.pallas_call(
        body, name=name,
        grid_spec=pltpu.PrefetchScalarGridSpec(
            num_scalar_prefetch=1, grid=(4, R // tr),
            in_specs=[pl.BlockSpec((None, None, tr, C), lambda k, i, c_ref: (k, c_ref[0], i, 0)),
                      pl.BlockSpec((None, tr, C), lambda k, i, c_ref: (k, i, 0))],
            out_specs=pl.BlockSpec((None, tr, C), lambda k, i, c_ref: (k, i, 0))),
        out_shape=jax.ShapeDtypeStruct((4, R, C), BF16),
        compiler_params=_cp("parallel", "parallel"))(core, dw4, recv)


def _adamw_math(w, g, m, v):
    m = ADAM_B1 * m + (1.0 - ADAM_B1) * g
    v = ADAM_B2 * v + (1.0 - ADAM_B2) * (g * g)
    m_hat = m / (1.0 - ADAM_B1 ** ADAM_STEP)
    v_hat = v / (1.0 - ADAM_B2 ** ADAM_STEP)
    delta = -ADAM_LR * (m_hat / (jnp.sqrt(v_hat) + ADAM_EPS) + ADAM_WD * w)
    return delta, m, v


def _adamw_shard(name, chip, layer, w, m, v, p, recv, prev, deps=()):
    nl, R, C = w.shape
    tr = _row_tile(R, C, 2)
    n_prev = 0 if prev is None else 4

    def body(chip_ref, w_ref, m_ref, v_ref, p_ref, r_ref, *rest):
        g_ref, d_ref, nm_ref, nv_ref, tok_ref = rest[-5:]
        tok_ref[...] = jnp.zeros_like(tok_ref)
        g = p_ref[...].astype(F32)
        for j in range(3):
            g = g + r_ref[j].astype(F32)
        delta, nm, nv = _adamw_math(w_ref[...], g, m_ref[...], v_ref[...])
        g_ref[...] = g
        d_ref[...] = delta
        nm_ref[...] = nm
        nv_ref[...] = nv

    lay = pl.BlockSpec((None, tr, C), lambda i, c_ref: (layer, i, 0))
    in_specs = [lay, lay, lay,
                pl.BlockSpec((None, tr, C), lambda i, c_ref: (c_ref[0], i, 0)),
                pl.BlockSpec((3, tr, C), lambda i, c_ref: (0, i, 0))]
    in_specs += [pl.BlockSpec(memory_space=pl.ANY)] * n_prev + [_DEP] * len(deps)
    shape = jax.ShapeDtypeStruct((nl, R, C), F32)
    ins = [chip, w, m, v, p, recv] + ([] if prev is None else list(prev)) + list(deps)
    return pl.pallas_call(
        body, name=name,
        grid_spec=pltpu.PrefetchScalarGridSpec(
            num_scalar_prefetch=1, grid=(R // tr,), in_specs=in_specs, out_specs=[lay] * 4 + [_DEP]),
        out_shape=[shape] * 4 + [jax.ShapeDtypeStruct((8, 128), F32)],
        input_output_aliases={6 + j: j for j in range(n_prev)},
        compiler_params=_cp("arbitrary"))(*ins)


def _sum_slots(name, parts, tr=512):
    n, R, C = parts.shape
    tr = _tile(R, tr)

    def body(p_ref, o_ref):
        acc = p_ref[0]
        for j in range(1, n):
            acc = acc + p_ref[j]
        o_ref[...] = acc

    return pl.pallas_call(
        body, name=name, grid=(R // tr,),
        in_specs=[pl.BlockSpec((n, tr, C), lambda i: (0, i, 0))],
        out_specs=pl.BlockSpec((tr, C), lambda i: (i, 0)),
        out_shape=jax.ShapeDtypeStruct((R, C), F32), compiler_params=_cp("parallel"))(parts)


def _adamw_flat(name, w, g, m, v, tr=512):
    R, C = w.shape
    tr = _tile(R, tr)

    def body(w_ref, g_ref, m_ref, v_ref, d_ref, nm_ref, nv_ref):
        delta, nm, nv = _adamw_math(w_ref[...], g_ref[...], m_ref[...], v_ref[...])
        d_ref[...] = delta
        nm_ref[...] = nm
        nv_ref[...] = nv

    row = pl.BlockSpec((tr, C), lambda i: (i, 0))
    shape = jax.ShapeDtypeStruct((R, C), F32)
    return pl.pallas_call(
        body, name=name, grid=(R // tr,), in_specs=[row] * 4, out_specs=[row] * 3, out_shape=[shape] * 3,
        compiler_params=_cp("parallel"))(w, g, m, v)


_PACK_ROWS = 512


def _pack(arrs):
    flat = jnp.concatenate([a.reshape(-1) for a in arrs])
    unit = _PACK_ROWS * 128
    pad = (-flat.shape[0]) % unit
    return jnp.pad(flat, (0, pad)).reshape(-1, 128)


def _unpack(packed, shapes):
    flat = packed.reshape(-1)
    outs, off = [], 0
    for s in shapes:
        n = int(np.prod(s))
        outs.append(flat[off:off + n].reshape(s))
        off += n
    return outs


def kernel(x, mem, mix_norm_g, ffn_norm_g, mem_norm_g, w_mem_kv, a_w_in, a_w_out, b_w_in, b_v_norm_g, b_w_s, b_s_bias, b_w_out, ffn_w_up, ffn_conv_w, ffn_conv_b, ffn_w_down, final_norm_g, loss_target, m_mix_norm_g, m_ffn_norm_g, m_mem_norm_g, m_w_mem_kv, m_a_w_in, m_a_w_out, m_b_w_in, m_b_v_norm_g, m_b_w_s, m_b_s_bias, m_b_w_out, m_ffn_w_up, m_ffn_conv_w, m_ffn_conv_b, m_ffn_w_down, m_final_norm_g, v_mix_norm_g, v_ffn_norm_g, v_mem_norm_g, v_w_mem_kv, v_a_w_in, v_a_w_out, v_b_w_in, v_b_v_norm_g, v_b_w_s, v_b_s_bias, v_b_w_out, v_ffn_w_up, v_ffn_conv_w, v_ffn_conv_b, v_ffn_w_down, v_final_norm_g):
    px, py, pc = _position()
    dev = 4 * px + 2 * py + pc
    core = jnp.reshape(pc, (1,)).astype(jnp.int32)
    chip = jnp.reshape(2 * px + py, (1,)).astype(jnp.int32)

    x0 = x[0]
    mem0 = mem[0]
    tgt = loss_target[0]
    S, D = x0.shape
    depth = mix_norm_g.shape[0]
    FF = ffn_w_down.shape[1] * N_DEV
    a_in = a_w_in.shape[2] * N_DEV
    b_in = b_w_in.shape[2] * N_DEV
    a_q_blk = (a_in - GW) // GW
    b_q_blk = (b_in - GW) // GW

    stacks = {"kv": (w_mem_kv, m_w_mem_kv, v_w_mem_kv), "ain": (a_w_in, m_a_w_in, v_a_w_in),
              "aout": (a_w_out, m_a_w_out, v_a_w_out), "bin": (b_w_in, m_b_w_in, v_b_w_in),
              "bout": (b_w_out, m_b_w_out, v_b_w_out), "up": (ffn_w_up, m_ffn_w_up, v_ffn_w_up),
              "down": (ffn_w_down, m_ffn_w_down, v_ffn_w_down)}
    dev1 = jnp.reshape(dev, (1,)).astype(jnp.int32)

    def groups_of(i):
        j = i // 2
        mix = [("kv", i), ("ain", j), ("aout", j)] if i % 2 == 0 else [("kv", i), ("bin", j), ("bout", j)]
        return mix, [("up", i), ("down", i)]

    gather_groups = [(f"{half}{i}", members) for i in range(depth) for half, members in zip("mf", groups_of(i))]
    gather_ahead = 2
    in_flight = {}

    def gather_start(k, after):
        gname, members = gather_groups[k]
        lands = [_cast_place(f"place_{t}{l}", dev1, stacks[t][0], l) for t, l in members]
        ssem, rsem, lands, tok = _split_start(f"ag_start_{gname}", lands, 4 * len(lands), _gather_plan, after)
        in_flight[k] = (lands, ssem, rsem)
        return tok

    small_land = _cast_place("place_small_w", dev1, _pack([ffn_conv_w, b_v_norm_g])[None], 0, F32)
    small_ssem, small_rsem, small_lands, small_tok = _split_start("smallw_start", [small_land], N_DEV - 1,
                                                                  _broadcast_plan)
    start_tokens = [small_tok, gather_start(0, [small_tok])]
    passing = {}

    def gather_arrive(k, after):
        if k >= len(gather_groups):
            return []
        gname, members = gather_groups[k]
        lands, ssem, rsem = in_flight.pop(k)
        lands = _split_wait(f"ag_wait_{gname}", lands, ssem, rsem, _gather_plan, after)
        toks = []
        for q in (range(1, 1 + gather_ahead) if k == 0 else [k + gather_ahead]):
            if q < len(gather_groups):
                toks.append(gather_start(q, [lands[0]] + toks))
        ssem, rsem, lands, tok = _split_start(f"ag_pass_{gname}", lands, 3 * len(lands), _forward_plan, toks)
        passing[k] = (lands, ssem, rsem)
        return toks + [tok]

    def gather_ready(k, after):
        gname, members = gather_groups[k]
        lands, ssem, rsem = passing.pop(k)
        lands = _split_wait(f"ag_ready_{gname}", lands, ssem, rsem, _forward_plan, after)
        out = {}
        for (t, l), land in zip(members, lands):
            if t == "kv":
                out["kv"] = land.reshape(D, 2 * GW)
            elif t in ("ain", "aout", "up"):
                out[{"ain": "in", "aout": "out", "up": "up"}[t]] = land
            elif t == "bin":
                out["in"] = jnp.transpose(land, (1, 0, 2)).reshape(D, b_in)
            elif t == "bout":
                out["out"] = land.reshape(B_W + GW, D)
            else:
                out["down"] = land.reshape(FF, D)
        return out

    def small_weights(after):
        (small_all,) = _split_wait("smallw_wait", small_lands, small_ssem, small_rsem, _broadcast_plan, after)
        cw_parts, gv_parts = [], []
        for d in range(N_DEV):
            cw_d, gv_d = _unpack(small_all[d], [ffn_conv_w.shape, b_v_norm_g.shape])
            cw_parts.append(cw_d)
            gv_parts.append(gv_d)
        return jnp.concatenate(cw_parts, axis=-1), jnp.concatenate(gv_parts, axis=-1)

    def conv_params(i):
        cw = conv_w_full[i].reshape(3, 2, FF).transpose(1, 0, 2)
        cb = ffn_conv_b[i].reshape(2, 1, FF)
        return cw, cb

    saved = []
    W = []
    xc = x0
    toks = start_tokens + gather_arrive(0, [x0])
    for i in range(depth):
        j = i // 2
        lw = gather_ready(2 * i, [xc])
        sv = {"x0": xc}
        h1, h1t = _rms_fwd(f"mixnorm{i}", xc, mix_norm_g[i], deps=toks, with_t=True)
        memn = _rms_fwd(f"memnorm{i}", mem0, mem_norm_g[i])
        if i % 2 == 0:
            proj = _mm_gcols(f"ain{i}", h1, lw["in"], out_dtype=BF16)
        else:
            proj = _mm_full(f"bin{i}", h1, lw["in"])
        toks = gather_arrive(2 * i + 1, [proj]) if i > 0 else []
        kv = _mm_full(f"kvproj{i}", memn, lw["kv"], deps=toks)
        if i % 2 == 0:
            outs, lses = [], []
            for g in range(len(A_PATTERNS)):
                o, l = _attn_fwd(f"attn{i}_{g}", proj, g)
                outs.append(o)
                lses.append(l)
            tok, lse = _attn_merge(f"merge{i}", outs, lses)
            mo = _mem_fwd(f"memattn{i}", proj, a_q_blk, kv)
            cat = jnp.concatenate([tok.astype(BF16), mo.astype(BF16)], axis=1)
            x1 = _mm_gcols(f"aout{i}", cat, lw["out"], res=xc)
            sv.update(tok=tok, lse=lse)
        else:
            bias_t = b_s_bias[j].T
            tok = _sgu_fwd(f"sgu{i}", proj, gv_full[j], b_w_s[j], bias_t)
            mo = _mem_fwd(f"memattn{i}", proj, b_q_blk, kv)
            cat = jnp.concatenate([tok.astype(BF16), mo.astype(BF16)], axis=1)
            x1 = _mm_full(f"bout{i}", cat, lw["out"], res=xc)
        toks = gather_arrive(1, [x1]) if i == 0 else []
        lw.update(gather_ready(2 * i + 1, [x1]))
        W.append(lw)
        if i == 0:
            conv_w_full, gv_full = small_weights([x1])
        h2, h2t = _rms_fwd(f"ffnnorm{i}", x1, ffn_norm_g[i], deps=toks, with_t=True)
        cw, cb = conv_params(i)
        a3 = _mm_gcols(f"up{i}", h2, lw["up"], split_out=True)
        toks = gather_arrive(2 * i + 2, [a3])
        act = _conv_fwd(f"conv{i}", a3, cw, cb)
        x2 = _mm_full(f"down{i}", act, lw["down"], res=x1, tn=1024, tk=FF // 4, deps=toks)
        toks = []
        sv.update(h1t=h1t, memn=memn, kv=kv, proj=proj, cat=cat, x1=x1, h2t=h2t, a3=a3, act=act)
        saved.append(sv)
        xc = x2

    dx, dg_final, sq, dx16 = _final("final", xc, tgt, final_norm_g)
    loss_local = sq[0, 0] * (0.5 / D)

    chain = {}
    adamw_tokens = []

    def pair_begin(gname, members, dws):
        n = len(dws)
        recvs = [lax.empty((4,) + dw.shape[1:], dw.dtype) for dw in dws]
        ssem, rsem, bufs, tok = _split_start(f"rs_pair_start_{gname}", dws + recvs, 4 * n, _pair_plan(n))
        return dict(name=gname, members=members, n=n, bufs=bufs, sems=(ssem, rsem)), tok

    def pair_end_chip_begin(st, after):
        n, gname = st["n"], st["name"]
        bufs = _split_wait(f"rs_pair_wait_{gname}", st["bufs"], *st["sems"], _pair_plan(n), after)
        ps = [_pair_sum(f"rs_sum_{t}{l}", core, bufs[w], bufs[n + w]) for w, (t, l) in enumerate(st["members"])]
        recvs = [lax.empty((3,) + p.shape[1:], BF16) for p in ps]
        ssem, rsem, bufs, tok = _split_start(f"rs_chip_start_{gname}", ps + recvs, 3 * n, _chip_plan(n))
        return dict(name=gname, members=st["members"], n=n, bufs=bufs, sems=(ssem, rsem)), tok

    def chip_end_update(st, after, deps=()):
        n = st["n"]
        bufs = _split_wait(f"rs_chip_wait_{st['name']}", st["bufs"], *st["sems"], _chip_plan(n), after)
        for w, (t, l) in enumerate(st["members"]):
            wst, mst, vst = stacks[t]
            *chain[t], tok = _adamw_shard(f"adamw_{t}{l}", chip, l, wst, mst, vst, bufs[w], bufs[n + w], chain.get(t),
                                          deps)
            adamw_tokens.append(tok)

    pipe = {"pair": [], "chip": [], "deps": []}

    def take_deps():
        deps, pipe["deps"] = pipe["deps"], []
        return deps

    def submit(gname, members, dws):
        st, tok = pair_begin(gname, members, dws)
        pipe["pair"].append(st)
        pipe["deps"].append(tok)

    def advance(after):
        arrived, pipe["chip"] = pipe["chip"], []
        toks = []
        for st in pipe["pair"]:
            new, tok = pair_end_chip_begin(st, [after])
            pipe["chip"].append(new)
            toks.append(tok)
        pipe["pair"] = []
        done = len(adamw_tokens)
        for st in arrived:
            chip_end_update(st, [after], toks)
        pipe["deps"] += toks + adamw_tokens[done:]

    def small_start(tag, arrs, after):
        land = _cast_place(f"place_small_{tag}", dev1, _pack(arrs)[None], 0, F32)
        ssem, rsem, lands, tok = _split_start(f"small_start_{tag}", [land], N_DEV - 1, _broadcast_plan, after)
        return (lands, ssem, rsem), tok

    def small_end(tag, state, shapes, after):
        lands, ssem, rsem = state
        lands = _split_wait(f"small_wait_{tag}", lands, ssem, rsem, _broadcast_plan, after)
        return _unpack(_sum_slots(f"small_sum_{tag}", lands[0]), shapes)

    def late_small():
        return [dg_mix[0], dg_ffn[0], dg_mem[0], d_conv_b[0][None], d_conv_w[0][None]]

    assert depth >= 2
    big = {k: [None] * n for k, n in (("kv", depth), ("ain", depth // 2 + depth % 2), ("aout", depth // 2 + depth % 2),
                                      ("bin", depth // 2), ("bout", depth // 2), ("up", depth), ("down", depth))}
    dg_mix, dg_ffn, dg_mem = [None] * depth, [None] * depth, [None] * depth
    d_conv_w, d_conv_b = [None] * depth, [None] * depth
    d_gv, d_ws, d_sb = [None] * (depth // 2), [None] * (depth // 2), [None] * (depth // 2)
    for i in reversed(range(depth)):
        j = i // 2
        lw, sv = W[i], saved[i]
        cw, cb = conv_params(i)
        mix_members, ffn_members = groups_of(i)
        if i == 0:
            early_arrays = [loss_local.reshape(1), dg_final.reshape(D), jnp.concatenate(dg_mix[1:]),
                            jnp.concatenate(dg_ffn[1:]), jnp.concatenate(dg_mem[1:]), jnp.stack(d_ws), jnp.stack(d_sb),
                            jnp.stack(d_gv), jnp.stack(d_conv_b[1:]), jnp.stack(d_conv_w[1:])]
            early_state, tok = small_start("early", early_arrays, [dx])
            pipe["deps"].append(tok)
        deps = take_deps()
        dact = _mm_dx_full(f"ddown{i}", dx16, lw["down"], tm=1024, tko=FF // 4, tc=D, deps=deps)
        big["down"][i] = _mm_dw(f"wdown{i}", sv["act"], dx16, deps=deps).reshape(N_DEV, FF // N_DEV, D)
        da3, dcw, dcb = _conv_bwd(f"dconv{i}", sv["a3"], cw, cb, dact)
        d_conv_w[i] = dcw.transpose(1, 0, 2).reshape(3, 2 * FF)
        d_conv_b[i] = dcb.reshape(2 * FF)
        advance(da3)
        deps = take_deps()
        dh2 = _mm_dx_gcols(f"dup{i}", da3, lw["up"], split_in=True, deps=deps)
        big["up"][i] = _mm_dw_gcols(f"wup{i}", sv["h2t"], da3, N_DEV, split_in=True, deps=deps, a_t=True)
        dx1, dg_ffn[i], dx1_16 = _rms_bwd(f"dffnnorm{i}", dh2, sv["x1"], ffn_norm_g[i], dx)
        submit(f"f{i}", ffn_members, [big["up"][i], big["down"][i]])
        deps = take_deps()
        if i % 2 == 0:
            dcat = _mm_dx_gcols(f"daout{i}", dx1_16, lw["out"], deps=deps)
            big["aout"][j] = _mm_dw_gcols(f"waout{i}", sv["cat"], dx1_16, N_DEV, deps=deps)
            advance(dcat)
            dqm, dkv = _mem_bwd(f"dmemattn{i}", sv["proj"], a_q_blk, sv["kv"], dcat, 1, deps=take_deps())
            parts = [None] * 9
            for g in range(len(A_PATTERNS)):
                dq, dk, dv = _attn_bwd(f"dattn{i}_{g}", sv["proj"], g, dcat, 0, sv["tok"], sv["lse"])
                parts[g], parts[3 + g], parts[6 + g] = dq, dk, dv
            dproj = jnp.concatenate(parts + [dqm], axis=1)
            deps = []
            dh1 = _mm_dx_gcols(f"dain{i}", dproj, lw["in"])
        else:
            dcat = _mm_dx_full(f"dbout{i}", dx1_16, lw["out"], tm=1024, deps=deps)
            big["bout"][j] = _mm_dw(f"wbout{i}", sv["cat"], dx1_16, deps=deps).reshape(
                N_DEV, (B_W + GW) // N_DEV, D)
            advance(dcat)
            dqm, dkv = _mem_bwd(f"dmemattn{i}", sv["proj"], b_q_blk, sv["kv"], dcat, B_W // GW, deps=take_deps())
            bias_t = b_s_bias[j].T
            du, dvp, dgv, dws, dbt = _sgu_bwd(f"dsgu{i}", sv["proj"], gv_full[j], b_w_s[j], bias_t, dcat)
            d_gv[j], d_ws[j], d_sb[j] = dgv.reshape(B_W), dws, dbt.T
            dproj = jnp.concatenate([du, dvp, dqm], axis=1)
            deps = []
            dh1 = _mm_dx_full(f"dbin{i}", dproj, lw["in"], tc=b_in // 2)
        dmemn = _mm_dx_full(f"dkvproj{i}", dkv, lw["kv"], tko=1024)
        _, dg_mem[i], _ = _rms_bwd(f"dmemnorm{i}", dmemn, mem0, mem_norm_g[i])
        dx, dg_mix[i], dx16 = _rms_bwd(f"dmixnorm{i}", dh1, sv["x0"], mix_norm_g[i], dx1)
        if i == 0:
            late_state, late_tok = small_start("late", late_small(), [dx])
            deps = deps + [late_tok]
        if i % 2 == 0:
            big["ain"][j] = _mm_dw_gcols(f"wain{i}", sv["h1t"], dproj, N_DEV, deps=deps, a_t=True)
        else:
            dwin = _mm_dw(f"wbin{i}", sv["h1t"], dproj, tko=1024, tn=512, deps=deps, a_t=True)
            big["bin"][j] = dwin.reshape(D, N_DEV, b_in // N_DEV).transpose(1, 0, 2)
        big["kv"][i] = _mm_dw(f"wkv{i}", sv["memn"], dkv, tko=1024, deps=deps).reshape(N_DEV, D // N_DEV, 2 * GW)
        submit(f"m{i}", mix_members, [big[t][l] for t, l in mix_members])
    grad_x = dx[None]

    last_chips, toks = [], []
    for st in pipe["pair"]:
        new, tok = pair_end_chip_begin(st, [dx])
        last_chips.append(new)
        toks.append(tok)
    g_early = small_end("early", early_state, [a.shape for a in early_arrays], [dx])
    for st in pipe["chip"]:
        chip_end_update(st, [g_early[1]], toks)
    g_late = small_end("late", late_state, [a.shape for a in late_small()], list(adamw_tokens))

    loss = g_early[0][0]
    layer0 = dict(zip(("mix", "ffn", "mem", "conv_b", "conv_w"), g_late))
    rest = dict(zip(("final", "mix", "ffn", "mem", "w_s", "s_bias", "gv", "conv_b", "conv_w"), g_early[1:]))
    g_cw_full = jnp.concatenate([layer0["conv_w"], rest["conv_w"]])
    g_gv = lax.dynamic_slice_in_dim(rest["gv"], dev * b_v_norm_g.shape[1], b_v_norm_g.shape[1], axis=1)
    g_cw = lax.dynamic_slice_in_dim(g_cw_full, dev * ffn_conv_w.shape[2], ffn_conv_w.shape[2], axis=2)
    g_all = [jnp.concatenate([layer0["mix"], rest["mix"]]), jnp.concatenate([layer0["ffn"], rest["ffn"]]),
             jnp.concatenate([layer0["mem"], rest["mem"]]), rest["w_s"], rest["s_bias"],
             jnp.concatenate([layer0["conv_b"], rest["conv_b"]]), rest["final"], g_gv, g_cw]
    names = ["mix_norm_g", "ffn_norm_g", "mem_norm_g", "b_w_s", "b_s_bias", "ffn_conv_b", "final_norm_g",
             "b_v_norm_g", "ffn_conv_w"]
    ws = [mix_norm_g, ffn_norm_g, mem_norm_g, b_w_s, b_s_bias, ffn_conv_b, final_norm_g, b_v_norm_g, ffn_conv_w]
    ms = [m_mix_norm_g, m_ffn_norm_g, m_mem_norm_g, m_b_w_s, m_b_s_bias, m_ffn_conv_b, m_final_norm_g,
          m_b_v_norm_g, m_ffn_conv_w]
    vs = [v_mix_norm_g, v_ffn_norm_g, v_mem_norm_g, v_b_w_s, v_b_s_bias, v_ffn_conv_b, v_final_norm_g,
          v_b_v_norm_g, v_ffn_conv_w]
    shapes = [w.shape for w in ws]
    d_p, m_p, v_p = _adamw_flat("adamw_small", _pack(ws), _pack(g_all), _pack(ms), _pack(vs))
    res = {}
    for n, g, d, nm, nv in zip(names, g_all, _unpack(d_p, shapes), _unpack(m_p, shapes), _unpack(v_p, shapes)):
        res[n] = [g, d, nm, nv]
    for st in last_chips:
        chip_end_update(st, [d_p] + list(adamw_tokens))
    for tag, name in (("kv", "w_mem_kv"), ("ain", "a_w_in"), ("aout", "a_w_out"), ("bin", "b_w_in"),
                      ("bout", "b_w_out"), ("up", "ffn_w_up"), ("down", "ffn_w_down")):
        res[name] = list(chain[tag])

    order = ["mix_norm_g", "ffn_norm_g", "mem_norm_g", "w_mem_kv", "a_w_in", "a_w_out", "b_w_in", "b_v_norm_g",
             "b_w_s", "b_s_bias", "b_w_out", "ffn_w_up", "ffn_conv_w", "ffn_conv_b", "ffn_w_down", "final_norm_g"]
    return (loss, grad_x, *[res[n][0] for n in order], *[res[n][1] for n in order],
            *[res[n][2] for n in order], *[res[n][3] for n in order])
```

```python
import functools

import numpy as np
import jax
import jax.numpy as jnp
from jax import lax
from jax.experimental import pallas as pl
from jax.experimental.pallas import tpu as pltpu

F32 = jnp.float32
BF16 = jnp.bfloat16
MESH = pl.DeviceIdType.MESH
N_DEV = 8

EPS = 1e-6
NEG = -1e30
HEAD = 128
HPG = 4
GW = HPG * HEAD
A_PATTERNS = ((128, 1), (512, 4), (2048, 16))
A_HEADS = HPG * len(A_PATTERNS)
QBLK = 128
B_GROUPS = 12
B_W = B_GROUPS * HEAD
SLOPES = (2.0 ** (-8.0 * (np.arange(A_HEADS) + 1) / A_HEADS)).astype(np.float32)
SCALE = HEAD ** -0.5

ADAM_LR = 0.001
ADAM_B1 = 0.9
ADAM_B2 = 0.999
ADAM_EPS = 1e-08
ADAM_WD = 0.01
ADAM_STEP = 10

V7X_VMEM_LIMIT = 50 * 1024 * 1024

NN = (((1,), (0,)), ((), ()))
NT = (((1,), (1,)), ((), ()))
TN = (((0,), (0,)), ((), ()))


def _cp(*sem):
    return pltpu.CompilerParams(dimension_semantics=sem, vmem_limit_bytes=V7X_VMEM_LIMIT)


def _dot(a, b, dims=NN):
    return lax.dot_general(a.astype(BF16), b.astype(BF16), dims, preferred_element_type=F32)


def _tile(n, pref):
    t = min(n, pref)
    assert n % t == 0, (n, pref)
    return t


def _row_tile(rows, cols, mib=1):
    best = None
    for t in range(16, rows + 1, 16):
        if rows % t == 0 and t * cols * 4 <= (mib << 20):
            best = t
    if best is None:
        best = rows
    return best


_DEP = pl.BlockSpec((8, 128), lambda *_: (0, 0))


def _matmul(name, dims, grid, a, a_spec, b, b_spec, out_shape, o_spec, tile, res=None, res_spec=None, deps=(),
            nsub=1):
    nk = grid[2]
    has_res = res is not None

    def body(*refs):
        a_ref, b_ref = refs[0], refs[1]
        r_ref = refs[2] if has_res else None
        o_ref, acc_ref = refs[-2], refs[-1]
        if nsub == 1:
            part = _dot(a_ref[...], b_ref[...], dims)
        else:
            w = a_ref.shape[-1] // nsub
            part = _dot(a_ref[:, :w], b_ref[0], dims)
            for q in range(1, nsub):
                part = part + _dot(a_ref[:, q * w:(q + 1) * w], b_ref[q], dims)

        def finish(val):
            if has_res:
                val = val + r_ref[...]
            o_ref[...] = val.astype(o_ref.dtype)

        if nk == 1:
            finish(part)
        else:
            k = pl.program_id(2)

            @pl.when(k == 0)
            def _():
                acc_ref[...] = part

            @pl.when(k > 0)
            def _():
                acc_ref[...] += part

            @pl.when(k == nk - 1)
            def _():
                finish(acc_ref[...])

    ins = [a, b] + ([res] if has_res else []) + list(deps)
    specs = [a_spec, b_spec] + ([res_spec] if has_res else []) + [_DEP] * len(deps)
    return pl.pallas_call(
        body, name=name, grid=grid, in_specs=specs, out_specs=o_spec, out_shape=out_shape,
        scratch_shapes=[pltpu.VMEM(tile if nk > 1 else (8, 128), F32)],
        compiler_params=_cp("parallel", "parallel", "arbitrary"))(*ins)


def _mm_full(name, a, w, res=None, tm=1024, tn=512, tk=2048, deps=()):
    M, K = a.shape
    N = w.shape[1]
    tm, tn, tk = _tile(M, tm), _tile(N, tn), _tile(K, tk)
    return _matmul(
        name, NN, (N // tn, M // tm, K // tk),
        a, pl.BlockSpec((tm, tk), lambda j, i, k: (i, k)),
        w, pl.BlockSpec((tk, tn), lambda j, i, k: (k, j)),
        jax.ShapeDtypeStruct((M, N), F32), pl.BlockSpec((tm, tn), lambda j, i, k: (i, j)), (tm, tn),
        res, pl.BlockSpec((tm, tn), lambda j, i, k: (i, j)), deps=deps)


def _mm_gcols(name, a, wg, res=None, split_out=False, tm=1024, deps=(), out_dtype=F32):
    M, K = a.shape
    G, _, Nl = wg.shape
    tm = _tile(M, tm)
    hg = G // 2
    if split_out:
        shape = jax.ShapeDtypeStruct((2, M, hg * Nl), out_dtype)
        o_spec = pl.BlockSpec((None, tm, Nl), lambda g, i, k: (g // hg, i, g % hg))
    else:
        shape = jax.ShapeDtypeStruct((M, G * Nl), out_dtype)
        o_spec = pl.BlockSpec((tm, Nl), lambda g, i, k: (i, g))
    return _matmul(
        name, NN, (G, M // tm, 1),
        a, pl.BlockSpec((tm, K), lambda g, i, k: (i, 0)),
        wg, pl.BlockSpec((None, K, Nl), lambda g, i, k: (g, 0, 0)),
        shape, o_spec, (tm, Nl),
        res, pl.BlockSpec((tm, Nl), lambda g, i, k: (i, g)), deps=deps)


def _mm_dx_full(name, dy, w, tm=512, tko=512, tc=2048, deps=()):
    M, N = dy.shape
    K = w.shape[0]
    tm, tko, tc = _tile(M, tm), _tile(K, tko), _tile(N, tc)
    return _matmul(
        name, NT, (K // tko, M // tm, N // tc),
        dy, pl.BlockSpec((tm, tc), lambda j, i, k: (i, k)),
        w, pl.BlockSpec((tko, tc), lambda j, i, k: (j, k)),
        jax.ShapeDtypeStruct((M, K), F32), pl.BlockSpec((tm, tko), lambda j, i, k: (i, j)), (tm, tko), deps=deps)


def _mm_dx_gcols(name, dy, wg, split_in=False, tm=1024, tko=1024, deps=(), nsub=1):
    G, K, Nl = wg.shape
    M = dy.shape[-2]
    tm, tko = _tile(M, tm), _tile(K, tko)
    hs = G // 2 // nsub if split_in else None
    if split_in:
        dy_spec = pl.BlockSpec((None, tm, nsub * Nl), lambda j, i, g: (g // hs, i, g % hs))
    else:
        dy_spec = pl.BlockSpec((tm, nsub * Nl), lambda j, i, g: (i, g))
    w_block = (None, tko, Nl) if nsub == 1 else (nsub, tko, Nl)
    return _matmul(
        name, NT, (K // tko, M // tm, G // nsub),
        dy, dy_spec,
        wg, pl.BlockSpec(w_block, lambda j, i, g: (g, j, 0)),
        jax.ShapeDtypeStruct((M, K), F32), pl.BlockSpec((tm, tko), lambda j, i, g: (i, j)), (tm, tko), deps=deps,
        nsub=nsub)


def _lhs_of_dw(a_t, ts, tko, index):
    if a_t:
        return NN, pl.BlockSpec((tko, ts), lambda *ids: index(*ids))
    return TN, pl.BlockSpec((ts, tko), lambda *ids: index(*ids)[::-1])


def _mm_dw(name, a, dy, tko=512, tn=1024, ts=2048, deps=(), a_t=False):
    K1, S = a.shape if a_t else a.shape[::-1]
    N = dy.shape[1]
    tko, tn, ts = _tile(K1, tko), _tile(N, tn), _tile(S, ts)
    dims, a_spec = _lhs_of_dw(a_t, ts, tko, lambda i, j, k: (j, k))
    return _matmul(
        name, dims, (N // tn, K1 // tko, S // ts),
        a, a_spec,
        dy, pl.BlockSpec((ts, tn), lambda i, j, k: (k, i)),
        jax.ShapeDtypeStruct((K1, N), BF16), pl.BlockSpec((tko, tn), lambda i, j, k: (j, i)), (tko, tn), deps=deps)


def _mm_dw_gcols(name, a, dy, G, split_in=False, tko=1024, ts=2048, deps=(), a_t=False):
    K1, S = a.shape if a_t else a.shape[::-1]
    Nl = (dy.shape[-1] * (2 if split_in else 1)) // G
    tko, ts = _tile(K1, tko), _tile(S, ts)
    hg = G // 2
    dims, a_spec = _lhs_of_dw(a_t, ts, tko, lambda g, j, k: (j, k))
    if split_in:
        dy_spec = pl.BlockSpec((None, ts, Nl), lambda g, j, k: (g // hg, k, g % hg))
    else:
        dy_spec = pl.BlockSpec((ts, Nl), lambda g, j, k: (k, g))
    return _matmul(
        name, dims, (G, K1 // tko, S // ts),
        a, a_spec,
        dy, dy_spec,
        jax.ShapeDtypeStruct((G, K1, Nl), BF16), pl.BlockSpec((None, tko, Nl), lambda g, j, k: (g, j, 0)),
        (tko, Nl), deps=deps)


def _rms_fwd(name, x, g, tr=256, deps=(), with_t=False):
    S, D = x.shape
    tr = _tile(S, tr)

    def body(x_ref, g_ref, *rest):
        xf = x_ref[...]
        r = lax.rsqrt(jnp.mean(xf * xf, axis=-1, keepdims=True) + EPS)
        y = xf * r * g_ref[...]
        if with_t:
            rest[-2][...] = y.astype(BF16)
            rest[-1][...] = y.T.astype(BF16)
        else:
            rest[-1][...] = y.astype(BF16)

    row = pl.BlockSpec((tr, D), lambda i: (i, 0))
    out_specs, out_shape = row, jax.ShapeDtypeStruct((S, D), BF16)
    if with_t:
        out_specs = [row, pl.BlockSpec((D, tr), lambda i: (0, i))]
        out_shape = [out_shape, jax.ShapeDtypeStruct((D, S), BF16)]
    return pl.pallas_call(
        body, name=name, grid=(S // tr,),
        in_specs=[row, pl.BlockSpec((1, D), lambda i: (0, 0))] + [_DEP] * len(deps),
        out_specs=out_specs, out_shape=out_shape, compiler_params=_cp("parallel"))(x, g.reshape(1, D), *deps)


def _rms_bwd(name, dh, x, g, dres=None, tr=256):
    S, D = x.shape
    tr = _tile(S, tr)
    has_res = dres is not None

    def body(*refs):
        dh_ref, x_ref, g_ref = refs[:3]
        dres_ref = refs[3] if has_res else None
        dx_ref, dg_ref, dx16_ref = refs[-3:]
        xf = x_ref[...]
        r = lax.rsqrt(jnp.mean(xf * xf, axis=-1, keepdims=True) + EPS)
        xh = xf * r
        dhv = dh_ref[...]
        dxh = dhv * g_ref[...]
        dx = r * (dxh - xh * jnp.mean(dxh * xh, axis=-1, keepdims=True))
        if has_res:
            dx = dx + dres_ref[...]
        dx_ref[...] = dx
        dx16_ref[...] = dx.astype(BF16)
        part = jnp.sum(dhv * xh, axis=0, keepdims=True)
        i = pl.program_id(0)

        @pl.when(i == 0)
        def _():
            dg_ref[...] = part

        @pl.when(i > 0)
        def _():
            dg_ref[...] += part

    row = pl.BlockSpec((tr, D), lambda i: (i, 0))
    vec = pl.BlockSpec((1, D), lambda i: (0, 0))
    ins = [dh, x, g.reshape(1, D)] + ([dres] if has_res else [])
    return pl.pallas_call(
        body, name=name, grid=(S // tr,),
        in_specs=[row, row, vec] + ([row] if has_res else []),
        out_specs=[row, vec, row],
        out_shape=[jax.ShapeDtypeStruct((S, D), F32), jax.ShapeDtypeStruct((1, D), F32),
                   jax.ShapeDtypeStruct((S, D), BF16)],
        compiler_params=_cp("arbitrary"))(*ins)


def _final(name, x, tgt, g, tr=256):
    S, D = x.shape
    tr = _tile(S, tr)

    def body(x_ref, t_ref, g_ref, dx_ref, dg_ref, loss_ref, dx16_ref):
        xf = x_ref[...]
        gv = g_ref[...]
        r = lax.rsqrt(jnp.mean(xf * xf, axis=-1, keepdims=True) + EPS)
        xh = xf * r
        err = xh * gv - t_ref[...]
        sq = jnp.sum(jnp.sum(err * err, axis=1, keepdims=True), axis=0, keepdims=True)
        dy = err * (1.0 / D)
        dxh = dy * gv
        dx = r * (dxh - xh * jnp.mean(dxh * xh, axis=-1, keepdims=True))
        dx_ref[...] = dx
        dx16_ref[...] = dx.astype(BF16)
        part = jnp.sum(dy * xh, axis=0, keepdims=True)
        lpart = jnp.broadcast_to(sq, (8, 128))
        i = pl.program_id(0)

        @pl.when(i == 0)
        def _():
            dg_ref[...] = part
            loss_ref[...] = lpart

        @pl.when(i > 0)
        def _():
            dg_ref[...] += part
            loss_ref[...] += lpart

    row = pl.BlockSpec((tr, D), lambda i: (i, 0))
    vec = pl.BlockSpec((1, D), lambda i: (0, 0))
    return pl.pallas_call(
        body, name=name, grid=(S // tr,), in_specs=[row, row, vec],
        out_specs=[row, vec, pl.BlockSpec((8, 128), lambda i: (0, 0)), row],
        out_shape=[jax.ShapeDtypeStruct((S, D), F32), jax.ShapeDtypeStruct((1, D), F32),
                   jax.ShapeDtypeStruct((8, 128), F32), jax.ShapeDtypeStruct((S, D), BF16)],
        compiler_params=_cp("arbitrary"))(x, tgt, g.reshape(1, D))


def _band_specs(nb, col_of):
    prev = pl.BlockSpec((QBLK, GW), lambda r, b: (jnp.maximum(b - 1, 0), col_of(r)))
    cur = pl.BlockSpec((QBLK, GW), lambda r, b: (b, col_of(r)))
    nxt = pl.BlockSpec((QBLK, GW), lambda r, b: (jnp.minimum(b + 1, nb - 1), col_of(r)))
    return [prev, cur, nxt]


HALF = QBLK // 2
WIN = 2 * QBLK


def _cat3(refs, sl):
    prev, cur, nxt = refs
    return jnp.concatenate([prev[HALF:, sl], cur[:, sl], nxt[:HALF, sl]], axis=0)


def _group_view(proj, g):
    _, dil = A_PATTERNS[g]
    S, C = proj.shape
    ng = len(A_PATTERNS)
    if dil == 1:
        return proj, lambda which, r: which * ng + g
    cols = [proj[:, (which * ng + g) * GW:(which * ng + g + 1) * GW] for which in range(3)]
    return jnp.concatenate(cols, axis=1).reshape(S // dil, dil * 3 * GW), lambda which, r: r * 3 + which


def _attn_fwd(name, proj, g):
    window, dil = A_PATTERNS[g]
    n_side = (window // 2) // dil
    S, C = proj.shape
    L = S // dil
    nb = L // QBLK
    pv, col = _group_view(proj, g)

    assert n_side == HALF

    def body(q_ref, kp, kc, kn, vp, vc, vn, o_ref, lse_ref):
        b = pl.program_id(1)
        jq = b * QBLK + lax.broadcasted_iota(jnp.int32, (QBLK, WIN), 0)
        jk = b * QBLK - HALF + lax.broadcasted_iota(jnp.int32, (QBLK, WIN), 1)
        rel = jnp.abs(jk - jq)
        mask = (rel <= n_side) & (jk >= 0) & (jk < L)
        dist = rel.astype(F32) * float(dil)
        for hh in range(HPG):
            sl = slice(hh * HEAD, (hh + 1) * HEAD)
            k = _cat3((kp, kc, kn), sl)
            v = _cat3((vp, vc, vn), sl)
            s = _dot(q_ref[:, sl], k, NT) * SCALE - float(SLOPES[g * HPG + hh]) * dist
            s = jnp.where(mask, s, NEG)
            m = jnp.max(s, axis=1, keepdims=True)
            p = jnp.exp(s - m)
            l = jnp.sum(p, axis=1, keepdims=True)
            o_ref[:, sl] = _dot(p, v) / l
            lse_ref[:, sl] = jnp.broadcast_to(m + jnp.log(l), (QBLK, HEAD))

    q_spec = pl.BlockSpec((QBLK, GW), lambda r, b: (b, col(0, r)))
    k_specs = _band_specs(nb, lambda r: col(1, r))
    v_specs = _band_specs(nb, lambda r: col(2, r))
    o_spec = pl.BlockSpec((QBLK, GW), lambda r, b: (b, r))
    shape = jax.ShapeDtypeStruct((L, dil * GW), F32)
    o, lse = pl.pallas_call(
        body, name=name, grid=(dil, nb), in_specs=[q_spec] + k_specs + v_specs,
        out_specs=[o_spec, o_spec], out_shape=[shape, shape],
        compiler_params=_cp("parallel", "parallel"))(pv, pv, pv, pv, pv, pv, pv)
    return o.reshape(S, GW), lse.reshape(S, GW)


def _attn_merge(name, outs, lses, tr=256):
    S = outs[0].shape[0]
    tr = _tile(S, tr)
    ng = len(outs)

    def body(*refs):
        o_refs, l_refs = refs[:ng], refs[ng:2 * ng]
        tok_ref, lse_ref = refs[-2], refs[-1]
        ls = [r[...] for r in l_refs]
        m = functools.reduce(jnp.maximum, ls)
        es = [jnp.exp(l - m) for l in ls]
        tot = functools.reduce(lambda a, b: a + b, es)
        acc = None
        for e, o_ref in zip(es, o_refs):
            term = (e / tot) * o_ref[...]
            acc = term if acc is None else acc + term
        tok_ref[...] = acc
        lse_ref[...] = m + jnp.log(tot)

    row = pl.BlockSpec((tr, GW), lambda i: (i, 0))
    shape = jax.ShapeDtypeStruct((S, GW), F32)
    return pl.pallas_call(
        body, name=name, grid=(S // tr,), in_specs=[row] * (2 * ng), out_specs=[row, row],
        out_shape=[shape, shape], compiler_params=_cp("parallel"))(*outs, *lses)


def _attn_bwd(name, proj, g, dtok_src, dtok_blk, tok, lse):
    window, dil = A_PATTERNS[g]
    n_side = (window // 2) // dil
    S, C = proj.shape
    L = S // dil
    nb = L // QBLK
    pv, col = _group_view(proj, g)
    assert n_side == HALF
    if dil == 1:
        dcb, dv_ = dtok_src.shape[1] // GW, dtok_src
    else:
        dcb, dv_ = 1, dtok_src[:, dtok_blk * GW:(dtok_blk + 1) * GW].reshape(L, dil * GW)
        dtok_blk = 0
    ov = tok.reshape(L, dil * GW)
    lv = lse.reshape(L, dil * GW)

    def body(qp, qc, qn, kp, kc, kn, vp, vc, vn, dop, doc, don, op, oc, on, lp, lc, ln,
             dq_ref, dk_ref, dv_ref):
        b = pl.program_id(1)
        jq = b * QBLK + lax.broadcasted_iota(jnp.int32, (QBLK, WIN), 0)
        jk = b * QBLK - HALF + lax.broadcasted_iota(jnp.int32, (QBLK, WIN), 1)
        rel = jnp.abs(jk - jq)
        mask = (rel <= n_side) & (jk >= 0) & (jk < L)
        dist = rel.astype(F32) * float(dil)
        jq3 = b * QBLK - HALF + lax.broadcasted_iota(jnp.int32, (WIN, QBLK), 0)
        jk1 = b * QBLK + lax.broadcasted_iota(jnp.int32, (WIN, QBLK), 1)
        rel3 = jnp.abs(jk1 - jq3)
        mask3 = (rel3 <= n_side) & (jq3 >= 0) & (jq3 < L)
        dist3 = rel3.astype(F32) * float(dil)
        for hh in range(HPG):
            sl = slice(hh * HEAD, (hh + 1) * HEAD)
            one = slice(hh * HEAD, hh * HEAD + 1)
            slope = float(SLOPES[g * HPG + hh])
            q = qc[:, sl]
            do = doc[:, sl]
            k3 = _cat3((kp, kc, kn), sl)
            v3 = _cat3((vp, vc, vn), sl)
            delta = jnp.sum(do * oc[:, sl], axis=1, keepdims=True)
            s = _dot(q, k3, NT) * SCALE - slope * dist
            p = jnp.where(mask, jnp.exp(s - lc[:, one]), 0.0)
            ds = p * (_dot(do, v3, NT) - delta)
            dq_ref[:, sl] = (_dot(ds, k3) * SCALE).astype(dq_ref.dtype)

            q3 = _cat3((qp, qc, qn), sl)
            do3 = _cat3((dop, doc, don), sl)
            o3 = _cat3((op, oc, on), sl)
            lse3 = _cat3((lp, lc, ln), sl)[:, :1]
            delta3 = jnp.sum(do3 * o3, axis=1, keepdims=True)
            k = kc[:, sl]
            v = vc[:, sl]
            s3 = _dot(q3, k, NT) * SCALE - slope * dist3
            p3 = jnp.where(mask3, jnp.exp(s3 - lse3), 0.0)
            ds3 = p3 * (_dot(do3, v, NT) - delta3)
            dv_ref[:, sl] = _dot(p3, do3, TN).astype(dv_ref.dtype)
            dk_ref[:, sl] = (_dot(ds3, q3, TN) * SCALE).astype(dk_ref.dtype)

    specs = (_band_specs(nb, lambda r: col(0, r)) + _band_specs(nb, lambda r: col(1, r))
             + _band_specs(nb, lambda r: col(2, r))
             + _band_specs(nb, lambda r: r * dcb + dtok_blk)
             + _band_specs(nb, lambda r: r) + _band_specs(nb, lambda r: r))
    o_spec = pl.BlockSpec((QBLK, GW), lambda r, b: (b, r))
    shape = jax.ShapeDtypeStruct((L, dil * GW), BF16)
    outs = pl.pallas_call(
        body, name=name, grid=(dil, nb), in_specs=specs, out_specs=[o_spec] * 3, out_shape=[shape] * 3,
        compiler_params=_cp("parallel", "parallel"))(*([pv] * 9 + [dv_] * 3 + [ov] * 3 + [lv] * 3))
    return [o.reshape(S, GW) for o in outs]


def _mem_fwd(name, proj, q_blk, kv, tq=256):
    S = proj.shape[0]
    M = kv.shape[0]
    tq = _tile(S, tq)

    def body(q_ref, kv_ref, o_ref):
        for hh in range(HPG):
            sl = slice(hh * HEAD, (hh + 1) * HEAD)
            k = kv_ref[:, sl]
            v = kv_ref[:, GW + hh * HEAD:GW + (hh + 1) * HEAD]
            s = _dot(q_ref[:, sl], k, NT) * SCALE
            m = jnp.max(s, axis=1, keepdims=True)
            p = jnp.exp(s - m)
            p = p / jnp.sum(p, axis=1, keepdims=True)
            o_ref[:, sl] = _dot(p, v)

    return pl.pallas_call(
        body, name=name, grid=(S // tq,),
        in_specs=[pl.BlockSpec((tq, GW), lambda i: (i, q_blk)), pl.BlockSpec((M, 2 * GW), lambda i: (0, 0))],
        out_specs=pl.BlockSpec((tq, GW), lambda i: (i, 0)),
        out_shape=jax.ShapeDtypeStruct((S, GW), F32), compiler_params=_cp("parallel"))(proj, kv)


def _mem_bwd(name, proj, q_blk, kv, dcat, do_blk, tq=256, deps=()):
    S = proj.shape[0]
    M = kv.shape[0]
    tq = _tile(S, tq)

    def body(q_ref, kv_ref, do_ref, *rest):
        dq_ref, dkv_ref = rest[-2:]
        i = pl.program_id(0)
        for hh in range(HPG):
            sl = slice(hh * HEAD, (hh + 1) * HEAD)
            vsl = slice(GW + hh * HEAD, GW + (hh + 1) * HEAD)
            q = q_ref[:, sl]
            do = do_ref[:, sl]
            k = kv_ref[:, sl]
            v = kv_ref[:, vsl]
            s = _dot(q, k, NT) * SCALE
            m = jnp.max(s, axis=1, keepdims=True)
            p = jnp.exp(s - m)
            p = p / jnp.sum(p, axis=1, keepdims=True)
            dp = _dot(do, v, NT)
            ds = p * (dp - jnp.sum(dp * p, axis=1, keepdims=True))
            dq_ref[:, sl] = (_dot(ds, k) * SCALE).astype(dq_ref.dtype)
            dk = _dot(ds, q, TN) * SCALE
            dvv = _dot(p, do, TN)

            @pl.when(i == 0)
            def _():
                dkv_ref[:, sl] = dk
                dkv_ref[:, vsl] = dvv

            @pl.when(i > 0)
            def _():
                dkv_ref[:, sl] += dk
                dkv_ref[:, vsl] += dvv

    return pl.pallas_call(
        body, name=name, grid=(S // tq,),
        in_specs=[pl.BlockSpec((tq, GW), lambda i: (i, q_blk)), pl.BlockSpec((M, 2 * GW), lambda i: (0, 0)),
                  pl.BlockSpec((tq, GW), lambda i: (i, do_blk))] + [_DEP] * len(deps),
        out_specs=[pl.BlockSpec((tq, GW), lambda i: (i, 0)), pl.BlockSpec((M, 2 * GW), lambda i: (0, 0))],
        out_shape=[jax.ShapeDtypeStruct((S, GW), BF16), jax.ShapeDtypeStruct((M, 2 * GW), F32)],
        compiler_params=_cp("arbitrary"))(proj, kv, dcat, *deps)


_RSQRT2 = float(1.0 / np.sqrt(2.0))
_RSQRT2PI = float(1.0 / np.sqrt(2.0 * np.pi))


def _gelu(x):
    return 0.5 * x * (1.0 + lax.erf(x * _RSQRT2))


def _gelu_and_grad(x):
    cdf = 0.5 * (1.0 + lax.erf(x * _RSQRT2))
    return x * cdf, cdf + x * jnp.exp(-0.5 * x * x) * _RSQRT2PI


def _sgu_fwd(name, proj, gv, w_s, bias_t):
    S = proj.shape[0]
    nch = S // HEAD

    def body(u_ref, v_ref, gv_ref, ws_ref, b_ref, o_ref):
        v = _gelu(v_ref[...])
        r = lax.rsqrt(jnp.mean(v * v, axis=-1, keepdims=True) + EPS)
        vn = v * r * gv_ref[...]
        for gg in range(B_GROUPS):
            sl = slice(gg * HEAD, (gg + 1) * HEAD)
            mixed = _dot(ws_ref[gg], vn[:, sl]) + b_ref[:, gg:gg + 1]
            o_ref[:, sl] = _gelu(u_ref[:, sl]) * mixed

    return pl.pallas_call(
        body, name=name, grid=(nch,),
        in_specs=[pl.BlockSpec((HEAD, B_W), lambda c: (c, 0)), pl.BlockSpec((HEAD, B_W), lambda c: (c, 1)),
                  pl.BlockSpec((1, B_W), lambda c: (0, 0)),
                  pl.BlockSpec((B_GROUPS, HEAD, HEAD), lambda c: (0, 0, 0)),
                  pl.BlockSpec((HEAD, B_GROUPS), lambda c: (0, 0))],
        out_specs=pl.BlockSpec((HEAD, B_W), lambda c: (c, 0)),
        out_shape=jax.ShapeDtypeStruct((S, B_W), F32),
        compiler_params=_cp("parallel"))(proj, proj, gv.reshape(1, B_W), w_s, bias_t)


def _sgu_bwd(name, proj, gv, w_s, bias_t, dcat):
    S = proj.shape[0]
    nch = S // HEAD

    def body(u_ref, v_ref, gv_ref, ws_ref, b_ref, dt_ref, du_ref, dvp_ref, dgv_ref, dws_ref, db_ref, dvn_ref):
        c = pl.program_id(0)
        vpre = v_ref[...]
        v, v_slope = _gelu_and_grad(vpre)
        r = lax.rsqrt(jnp.mean(v * v, axis=-1, keepdims=True) + EPS)
        vh = v * r
        gvv = gv_ref[...]
        vn = vh * gvv
        for gg in range(B_GROUPS):
            sl = slice(gg * HEAD, (gg + 1) * HEAD)
            upre = u_ref[:, sl]
            dt = dt_ref[:, sl]
            vng = vn[:, sl]
            mixed = _dot(ws_ref[gg], vng) + b_ref[:, gg:gg + 1]
            u, u_slope = _gelu_and_grad(upre)
            du_ref[:, sl] = (dt * mixed * u_slope).astype(du_ref.dtype)
            dmix = dt * u
            dvn_ref[:, sl] = _dot(ws_ref[gg], dmix, TN)
            dws = _dot(dmix, vng, NT)
            dbs = jnp.sum(dmix, axis=1, keepdims=True)

            @pl.when(c == 0)
            def _():
                dws_ref[gg] = dws
                db_ref[:, gg:gg + 1] = dbs

            @pl.when(c > 0)
            def _():
                dws_ref[gg] += dws
                db_ref[:, gg:gg + 1] += dbs

        dvn = dvn_ref[...]
        dgp = jnp.sum(dvn * vh, axis=0, keepdims=True)
        dvh = dvn * gvv
        dv = r * (dvh - vh * jnp.mean(dvh * vh, axis=-1, keepdims=True))
        dvp_ref[...] = (dv * v_slope).astype(dvp_ref.dtype)

        @pl.when(c == 0)
        def _():
            dgv_ref[...] = dgp

        @pl.when(c > 0)
        def _():
            dgv_ref[...] += dgp

    blk = lambda j: pl.BlockSpec((HEAD, B_W), lambda c: (c, j))
    vec = pl.BlockSpec((1, B_W), lambda c: (0, 0))
    ws_spec = pl.BlockSpec((B_GROUPS, HEAD, HEAD), lambda c: (0, 0, 0))
    b_spec = pl.BlockSpec((HEAD, B_GROUPS), lambda c: (0, 0))
    du, dvp, dgv, dws, db = pl.pallas_call(
        body, name=name, grid=(nch,),
        in_specs=[blk(0), blk(1), vec, ws_spec, b_spec, blk(0)],
        out_specs=[blk(0), blk(0), vec, ws_spec, b_spec],
        out_shape=[jax.ShapeDtypeStruct((S, B_W), BF16), jax.ShapeDtypeStruct((S, B_W), BF16),
                   jax.ShapeDtypeStruct((1, B_W), F32), jax.ShapeDtypeStruct((B_GROUPS, HEAD, HEAD), F32),
                   jax.ShapeDtypeStruct((HEAD, B_GROUPS), F32)],
        scratch_shapes=[pltpu.VMEM((HEAD, B_W), F32)],
        compiler_params=_cp("arbitrary"))(proj, proj, gv.reshape(1, B_W), w_s, bias_t, dcat)
    return du, dvp, dgv, dws, db


def _shift_down(a, row):
    return jnp.where(row == 0, 0.0, pltpu.roll(a, 1, 0))


def _shift_up(a, row):
    n = a.shape[0]
    return jnp.where(row == n - 1, 0.0, pltpu.roll(a, n - 1, 0))


def _conv(a, w, b, row):
    return _shift_down(a, row) * w[0:1] + a * w[1:2] + _shift_up(a, row) * w[2:3] + b


def _conv_fwd(name, a3, cw, cb, tc=256):
    _, S, FF = a3.shape
    tc = _tile(FF, tc)

    def body(a_ref, w_ref, b_ref, o_ref):
        row = lax.broadcasted_iota(jnp.int32, (S, tc), 0)
        cg = _conv(a_ref[0], w_ref[0], b_ref[0], row)
        cv = _conv(a_ref[1], w_ref[1], b_ref[1], row)
        o_ref[...] = (_gelu(cg) * cv).astype(o_ref.dtype)

    return pl.pallas_call(
        body, name=name, grid=(FF // tc,),
        in_specs=[pl.BlockSpec((2, S, tc), lambda j: (0, 0, j)), pl.BlockSpec((2, 3, tc), lambda j: (0, 0, j)),
                  pl.BlockSpec((2, 1, tc), lambda j: (0, 0, j))],
        out_specs=pl.BlockSpec((S, tc), lambda j: (0, j)),
        out_shape=jax.ShapeDtypeStruct((S, FF), BF16), compiler_params=_cp("parallel"))(a3, cw, cb)


def _conv_bwd(name, a3, cw, cb, dact, tc=128):
    _, S, FF = a3.shape
    tc = _tile(FF, tc)

    def body(a_ref, w_ref, b_ref, d_ref, da_ref, dw_ref, db_ref):
        row = lax.broadcasted_iota(jnp.int32, (S, tc), 0)
        ag, av = a_ref[0], a_ref[1]
        wg, wv = w_ref[0], w_ref[1]
        cg = _conv(ag, wg, b_ref[0], row)
        cv = _conv(av, wv, b_ref[1], row)
        d = d_ref[...]
        gate, gate_slope = _gelu_and_grad(cg)
        dcs = (d * cv * gate_slope, d * gate)
        for h, (dc, a, w) in enumerate(zip(dcs, (ag, av), (wg, wv))):
            da = _shift_up(dc, row) * w[0:1] + dc * w[1:2] + _shift_down(dc, row) * w[2:3]
            da_ref[h] = da.astype(da_ref.dtype)
            dw_ref[h, 0:1, :] = jnp.sum(dc * _shift_down(a, row), axis=0, keepdims=True)
            dw_ref[h, 1:2, :] = jnp.sum(dc * a, axis=0, keepdims=True)
            dw_ref[h, 2:3, :] = jnp.sum(dc * _shift_up(a, row), axis=0, keepdims=True)
            db_ref[h] = jnp.sum(dc, axis=0, keepdims=True)

    a_spec = pl.BlockSpec((2, S, tc), lambda j: (0, 0, j))
    w_spec = pl.BlockSpec((2, 3, tc), lambda j: (0, 0, j))
    b_spec = pl.BlockSpec((2, 1, tc), lambda j: (0, 0, j))
    return pl.pallas_call(
        body, name=name, grid=(FF // tc,),
        in_specs=[a_spec, w_spec, b_spec, pl.BlockSpec((S, tc), lambda j: (0, j))],
        out_specs=[a_spec, w_spec, b_spec],
        out_shape=[jax.ShapeDtypeStruct((2, S, FF), BF16), jax.ShapeDtypeStruct((2, 3, FF), F32),
                   jax.ShapeDtypeStruct((2, 1, FF), F32)],
        compiler_params=_cp("parallel"))(a3, cw, cb, dact)


_HBM = pl.BlockSpec(memory_space=pltpu.HBM)


def _position():
    return lax.axis_index("x"), lax.axis_index("y"), lax.axis_index("c")


_SEM =pl.BlockSpec(memory_space=pltpu.SEMAPHORE)
_EFFECT = pltpu.SideEffectType.DATAFLOW_SIDE_EFFECTING
_FLIPS = ((1, 0), (0, 1), (1, 1))


def _split_start(name, bufs, ncopy, plan, after=()):
    n = len(bufs)
    after = list(after)

    def body(*refs):
        ins = refs[:n]
        send_sems, recv_sems, token = refs[n + len(after)], refs[n + len(after) + 1], refs[-1]
        for i, (src, dst, to) in enumerate(plan(ins)):
            pltpu.make_async_remote_copy(src_ref=src, dst_ref=dst, send_sem=send_sems.at[i], recv_sem=recv_sems.at[i],
                                         device_id=to, device_id_type=MESH).start()
        token[...] = jnp.zeros_like(token)

    outs = pl.pallas_call(
        body, name=name,
        out_shape=(pltpu.SemaphoreType.DMA((ncopy,)), pltpu.SemaphoreType.DMA((ncopy,)),
                   *[pltpu.HBM(b.shape, b.dtype) for b in bufs], jax.ShapeDtypeStruct((8, 128), F32)),
        in_specs=[_HBM] * n + [pl.BlockSpec(memory_space=pl.ANY)] * len(after),
        out_specs=(_SEM, _SEM, *([_HBM] * n), pl.BlockSpec(memory_space=pltpu.VMEM)),
        input_output_aliases={i: 2 + i for i in range(n)},
        compiler_params=pltpu.CompilerParams(has_side_effects=_EFFECT),
    )(*[pltpu.with_memory_space_constraint(b, pltpu.HBM) for b in bufs], *after)
    return outs[0], outs[1], list(outs[2:2 + n]), outs[-1]


def _split_wait(name, bufs, send_sems, recv_sems, plan, after):
    n = len(bufs)
    after = list(after)

    def body(*refs):
        ins = refs[:n]
        ssem, rsem = refs[n], refs[n + 1]
        for i, (src, dst, to) in enumerate(plan(ins)):
            cp = pltpu.make_async_remote_copy(src_ref=src, dst_ref=dst, send_sem=ssem.at[i], recv_sem=rsem.at[i],
                                              device_id=to, device_id_type=MESH)
            cp.wait_send()
            cp.wait_recv()

    outs = pl.pallas_call(
        body, name=name, out_shape=tuple(pltpu.HBM(b.shape, b.dtype) for b in bufs),
        in_specs=[_HBM] * n + [_SEM, _SEM] + [pl.BlockSpec(memory_space=pl.ANY)] * len(after),
        out_specs=tuple([_HBM] * n), input_output_aliases={i: i for i in range(n)},
        compiler_params=pltpu.CompilerParams(has_side_effects=_EFFECT),
    )(*bufs, send_sems, recv_sems, *after)
    return list(outs)


def _gather_plan(refs):
    px, py, pc = _position()
    me = 4 * px + 2 * py + pc
    targets = [(px, py, 1 - pc), (1 - px, py, pc), (px, 1 - py, pc), (1 - px, 1 - py, pc)]
    return [(r.at[me], r.at[me], to) for r in refs for to in targets]


def _forward_plan(refs):
    px, py, pc = _position()
    out = []
    for r in refs:
        for fx, fy in _FLIPS:
            slot = 4 * (1 - px if fx else px) + 2 * (1 - py if fy else py) + pc
            out.append((r.at[slot], r.at[slot], (px, py, 1 - pc)))
    return out


def _pair_plan(n):
    def plan(refs):
        px, py, pc = _position()
        return [(refs[w].at[2 * k + (1 - pc)], refs[n + w].at[k], (px, py, 1 - pc)) for w in range(n) for k in range(4)]
    return plan


def _chip_plan(n):
    def plan(refs):
        px, py, pc = _position()
        out = []
        for w in range(n):
            for j, (fx, fy) in enumerate(_FLIPS):
                qx = 1 - px if fx else px
                qy = 1 - py if fy else py
                out.append((refs[w].at[2 * qx + qy], refs[n + w].at[j], (qx, qy, pc)))
        return out
    return plan


def _broadcast_plan(refs):
    px, py, pc = _position()
    me = 4 * px + 2 * py + pc
    flips = [(fx, fy, fc) for fx in (0, 1) for fy in (0, 1) for fc in (0, 1)][1:]
    targets = [(1 - px if fx else px, 1 - py if fy else py, 1 - pc if fc else pc) for fx, fy, fc in flips]
    return [(r.at[me], r.at[me], to) for r in refs for to in targets]


def _cast_place(name, dev, w, layer, dtype=BF16):
    nl, R, C = w.shape
    tr = _row_tile(R, C, 4)

    def body(dev_ref, w_ref, o_ref):
        o_ref[...] = w_ref[...].astype(o_ref.dtype)

    return pl.pallas_call(
        body, name=name,
        grid_spec=pltpu.PrefetchScalarGridSpec(
            num_scalar_prefetch=1, grid=(R // tr,),
            in_specs=[pl.BlockSpec((None, tr, C), lambda i, d: (layer, i, 0))],
            out_specs=pl.BlockSpec((None, tr, C), lambda i, d: (d[0], i, 0))),
        out_shape=jax.ShapeDtypeStruct((N_DEV, R, C), dtype), compiler_params=_cp("parallel"))(dev, w)


def _pair_sum(name, core, dw, recv):
    _, R, C = dw.shape
    tr = _row_tile(R, C, 4)
    dw4 = dw.reshape(4, 2, R, C)

    def body(core_ref, a_ref, b_ref, o_ref):
        o_ref[...] = (a_ref[...].astype(F32) + b_ref[...].astype(F32)).astype(o_ref.dtype)

    return pl.pallas_call(
        body, name=name,
        grid_spec=pltpu.PrefetchScalarGridSpec(
            num_scalar_prefetch=1, grid=(4, R // tr),
            in_specs=[pl.BlockSpec((None, None, tr, C), lambda k, i, c_ref: (k, c_ref[0], i, 0)),
                      pl.BlockSpec((None, tr, C), lambda k, i, c_ref: (k, i, 0))],
            out_specs=pl.BlockSpec((None, tr, C), lambda k, i, c_ref: (k, i, 0))),
        out_shape=jax.ShapeDtypeStruct((4, R, C), BF16),
        compiler_params=_cp("parallel", "parallel"))(core, dw4, recv)


def _adamw_math(w, g, m, v):
    m = ADAM_B1 * m + (1.0 - ADAM_B1) * g
    v = ADAM_B2 * v + (1.0 - ADAM_B2) * (g * g)
    m_hat = m / (1.0 - ADAM_B1 ** ADAM_STEP)
    v_hat = v / (1.0 - ADAM_B2 ** ADAM_STEP)
    delta = -ADAM_LR * (m_hat / (jnp.sqrt(v_hat) + ADAM_EPS) + ADAM_WD * w)
    return delta, m, v


def _adamw_shard(name, chip, layer, w, m, v, p, recv, prev, deps=()):
    nl, R, C = w.shape
    tr = _row_tile(R, C, 2)
    n_prev = 0 if prev is None else 4

    def body(chip_ref, w_ref, m_ref, v_ref, p_ref, r_ref, *rest):
        g_ref, d_ref, nm_ref, nv_ref, tok_ref = rest[-5:]
        tok_ref[...] = jnp.zeros_like(tok_ref)
        g = p_ref[...].astype(F32)
        for j in range(3):
            g = g + r_ref[j].astype(F32)
        delta, nm, nv = _adamw_math(w_ref[...], g, m_ref[...], v_ref[...])
        g_ref[...] = g
        d_ref[...] = delta
        nm_ref[...] = nm
        nv_ref[...] = nv

    lay = pl.BlockSpec((None, tr, C), lambda i, c_ref: (layer, i, 0))
    in_specs = [lay, lay, lay,
                pl.BlockSpec((None, tr, C), lambda i, c_ref: (c_ref[0], i, 0)),
                pl.BlockSpec((3, tr, C), lambda i, c_ref: (0, i, 0))]
    in_specs += [pl.BlockSpec(memory_space=pl.ANY)] * n_prev + [_DEP] * len(deps)
    shape = jax.ShapeDtypeStruct((nl, R, C), F32)
    ins = [chip, w, m, v, p, recv] + ([] if prev is None else list(prev)) + list(deps)
    return pl.pallas_call(
        body, name=name,
        grid_spec=pltpu.PrefetchScalarGridSpec(
            num_scalar_prefetch=1, grid=(R // tr,), in_specs=in_specs, out_specs=[lay] * 4 + [_DEP]),
        out_shape=[shape] * 4 + [jax.ShapeDtypeStruct((8, 128), F32)],
        input_output_aliases={6 + j: j for j in range(n_prev)},
        compiler_params=_cp("arbitrary"))(*ins)


def _sum_slots(name, parts, tr=512):
    n, R, C = parts.shape
    tr = _tile(R, tr)

    def body(p_ref, o_ref):
        acc = p_ref[0]
        for j in range(1, n):
            acc = acc + p_ref[j]
        o_ref[...] = acc

    return pl.pallas_call(
        body, name=name, grid=(R // tr,),
        in_specs=[pl.BlockSpec((n, tr, C), lambda i: (0, i, 0))],
        out_specs=pl.BlockSpec((tr, C), lambda i: (i, 0)),
        out_shape=jax.ShapeDtypeStruct((R, C), F32), compiler_params=_cp("parallel"))(parts)


def _adamw_flat(name, w, g, m, v, tr=512):
    R, C = w.shape
    tr = _tile(R, tr)

    def body(w_ref, g_ref, m_ref, v_ref, d_ref, nm_ref, nv_ref):
        delta, nm, nv = _adamw_math(w_ref[...], g_ref[...], m_ref[...], v_ref[...])
        d_ref[...] = delta
        nm_ref[...] = nm
        nv_ref[...] = nv

    row = pl.BlockSpec((tr, C), lambda i: (i, 0))
    shape = jax.ShapeDtypeStruct((R, C), F32)
    return pl.pallas_call(
        body, name=name, grid=(R // tr,), in_specs=[row] * 4, out_specs=[row] * 3, out_shape=[shape] * 3,
        compiler_params=_cp("parallel"))(w, g, m, v)


_PACK_ROWS = 512


def _pack(arrs):
    flat = jnp.concatenate([a.reshape(-1) for a in arrs])
    unit = _PACK_ROWS * 128
    pad = (-flat.shape[0]) % unit
    return jnp.pad(flat, (0, pad)).reshape(-1, 128)


def _unpack(packed, shapes):
    flat = packed.reshape(-1)
    outs, off = [], 0
    for s in shapes:
        n = int(np.prod(s))
        outs.append(flat[off:off + n].reshape(s))
        off += n
    return outs


def kernel(x, mem, mix_norm_g, ffn_norm_g, mem_norm_g, w_mem_kv, a_w_in, a_w_out, b_w_in, b_v_norm_g, b_w_s, b_s_bias, b_w_out, ffn_w_up, ffn_conv_w, ffn_conv_b, ffn_w_down, final_norm_g, loss_target, m_mix_norm_g, m_ffn_norm_g, m_mem_norm_g, m_w_mem_kv, m_a_w_in, m_a_w_out, m_b_w_in, m_b_v_norm_g, m_b_w_s, m_b_s_bias, m_b_w_out, m_ffn_w_up, m_ffn_conv_w, m_ffn_conv_b, m_ffn_w_down, m_final_norm_g, v_mix_norm_g, v_ffn_norm_g, v_mem_norm_g, v_w_mem_kv, v_a_w_in, v_a_w_out, v_b_w_in, v_b_v_norm_g, v_b_w_s, v_b_s_bias, v_b_w_out, v_ffn_w_up, v_ffn_conv_w, v_ffn_conv_b, v_ffn_w_down, v_final_norm_g):
    px, py, pc = _position()
    dev = 4 * px + 2 * py + pc
    core = jnp.reshape(pc, (1,)).astype(jnp.int32)
    chip = jnp.reshape(2 * px + py, (1,)).astype(jnp.int32)

    x0 = x[0]
    mem0 = mem[0]
    tgt = loss_target[0]
    S, D = x0.shape
    depth = mix_norm_g.shape[0]
    FF = ffn_w_down.shape[1] * N_DEV
    a_in = a_w_in.shape[2] * N_DEV
    b_in = b_w_in.shape[2] * N_DEV
    a_q_blk = (a_in - GW) // GW
    b_q_blk = (b_in - GW) // GW

    stacks = {"kv": (w_mem_kv, m_w_mem_kv, v_w_mem_kv), "ain": (a_w_in, m_a_w_in, v_a_w_in),
              "aout": (a_w_out, m_a_w_out, v_a_w_out), "bin": (b_w_in, m_b_w_in, v_b_w_in),
              "bout": (b_w_out, m_b_w_out, v_b_w_out), "up": (ffn_w_up, m_ffn_w_up, v_ffn_w_up),
              "down": (ffn_w_down, m_ffn_w_down, v_ffn_w_down)}
    dev1 = jnp.reshape(dev, (1,)).astype(jnp.int32)

    def groups_of(i):
        j = i // 2
        mix = [("kv", i), ("ain", j), ("aout", j)] if i % 2 == 0 else [("kv", i), ("bin", j), ("bout", j)]
        return mix, [("up", i), ("down", i)]

    gather_groups = [(f"{half}{i}", members) for i in range(depth) for half, members in zip("mf", groups_of(i))]
    gather_ahead = 2
    in_flight = {}

    def gather_start(k, after):
        gname, members = gather_groups[k]
        lands = [_cast_place(f"place_{t}{l}", dev1, stacks[t][0], l) for t, l in members]
        ssem, rsem, lands, tok = _split_start(f"ag_start_{gname}", lands, 4 * len(lands), _gather_plan, after)
        in_flight[k] = (lands, ssem, rsem)
        return tok

    small_land = _cast_place("place_small_w", dev1, _pack([ffn_conv_w, b_v_norm_g])[None], 0, F32)
    small_ssem, small_rsem, small_lands, small_tok = _split_start("smallw_start", [small_land], N_DEV - 1,
                                                                  _broadcast_plan)
    start_tokens = [small_tok, gather_start(0, [small_tok])]
    passing = {}

    def gather_arrive(k, after):
        if k >= len(gather_groups):
            return []
        gname, members = gather_groups[k]
        lands, ssem, rsem = in_flight.pop(k)
        lands = _split_wait(f"ag_wait_{gname}", lands, ssem, rsem, _gather_plan, after)
        toks = []
        for q in (range(1, 1 + gather_ahead) if k == 0 else [k + gather_ahead]):
            if q < len(gather_groups):
                toks.append(gather_start(q, [lands[0]] + toks))
        ssem, rsem, lands, tok = _split_start(f"ag_pass_{gname}", lands, 3 * len(lands), _forward_plan, toks)
        passing[k] = (lands, ssem, rsem)
        return toks + [tok]

    def gather_ready(k, after):
        gname, members = gather_groups[k]
        lands, ssem, rsem = passing.pop(k)
        lands = _split_wait(f"ag_ready_{gname}", lands, ssem, rsem, _forward_plan, after)
        out = {}
        for (t, l), land in zip(members, lands):
            if t == "kv":
                out["kv"] = land.reshape(D, 2 * GW)
            elif t in ("ain", "aout", "up"):
                out[{"ain": "in", "aout": "out", "up": "up"}[t]] = land
            elif t == "bin":
                out["in"] = jnp.transpose(land, (1, 0, 2)).reshape(D, b_in)
            elif t == "bout":
                out["out"] = land.reshape(B_W + GW, D)
            else:
                out["down"] = land.reshape(FF, D)
        return out

    def small_weights(after):
        (small_all,) = _split_wait("smallw_wait", small_lands, small_ssem, small_rsem, _broadcast_plan, after)
        cw_parts, gv_parts = [], []
        for d in range(N_DEV):
            cw_d, gv_d = _unpack(small_all[d], [ffn_conv_w.shape, b_v_norm_g.shape])
            cw_parts.append(cw_d)
            gv_parts.append(gv_d)
        return jnp.concatenate(cw_parts, axis=-1), jnp.concatenate(gv_parts, axis=-1)

    def conv_params(i):
        cw = conv_w_full[i].reshape(3, 2, FF).transpose(1, 0, 2)
        cb = ffn_conv_b[i].reshape(2, 1, FF)
        return cw, cb

    saved = []
    W = []
    xc = x0
    toks = start_tokens + gather_arrive(0, [x0])
    for i in range(depth):
        j = i // 2
        lw = gather_ready(2 * i, [xc])
        sv = {"x0": xc}
        h1, h1t = _rms_fwd(f"mixnorm{i}", xc, mix_norm_g[i], deps=toks, with_t=True)
        memn = _rms_fwd(f"memnorm{i}", mem0, mem_norm_g[i])
        if i % 2 == 0:
            proj = _mm_gcols(f"ain{i}", h1, lw["in"], out_dtype=BF16)
        else:
            proj = _mm_full(f"bin{i}", h1, lw["in"])
        kv = _mm_full(f"kvproj{i}", memn, lw["kv"])
        if i % 2 == 0:
            outs, lses = [], []
            for g in range(len(A_PATTERNS)):
                o, l = _attn_fwd(f"attn{i}_{g}", proj, g)
                outs.append(o)
                lses.append(l)
            tok, lse = _attn_merge(f"merge{i}", outs, lses)
            sv.update(tok=tok, lse=lse)
        else:
            tok = _sgu_fwd(f"sgu{i}", proj, gv_full[j], b_w_s[j], b_s_bias[j].T)
        toks = gather_arrive(2 * i + 1, [tok]) if i > 0 else []
        mo = _mem_fwd(f"memattn{i}", proj, a_q_blk if i % 2 == 0 else b_q_blk, kv)
        cat = jnp.concatenate([tok.astype(BF16), mo.astype(BF16)], axis=1)
        if i % 2 == 0:
            x1 = _mm_gcols(f"aout{i}", cat, lw["out"], res=xc, deps=toks)
        else:
            x1 = _mm_full(f"bout{i}", cat, lw["out"], res=xc, deps=toks)
        toks = gather_arrive(1, [x1]) if i == 0 else []
        lw.update(gather_ready(2 * i + 1, [x1]))
        W.append(lw)
        if i == 0:
            conv_w_full, gv_full = small_weights([x1])
        h2, h2t = _rms_fwd(f"ffnnorm{i}", x1, ffn_norm_g[i], deps=toks, with_t=True)
        cw, cb = conv_params(i)
        a3 = _mm_gcols(f"up{i}", h2, lw["up"], split_out=True)
        toks = gather_arrive(2 * i + 2, [a3])
        act = _conv_fwd(f"conv{i}", a3, cw, cb)
        x2 = _mm_full(f"down{i}", act, lw["down"], res=x1, tn=1024, tk=FF // 4, deps=toks)
        toks = []
        sv.update(h1t=h1t, memn=memn, kv=kv, proj=proj, cat=cat, x1=x1, h2t=h2t, a3=a3, act=act)
        saved.append(sv)
        xc = x2

    dx, dg_final, sq, dx16 = _final("final", xc, tgt, final_norm_g)
    loss_local = sq[0, 0] * (0.5 / D)

    chain = {}
    adamw_tokens = []

    def pair_begin(gname, members, dws):
        n = len(dws)
        recvs = [lax.empty((4,) + dw.shape[1:], dw.dtype) for dw in dws]
        ssem, rsem, bufs, tok = _split_start(f"rs_pair_start_{gname}", dws + recvs, 4 * n, _pair_plan(n))
        return dict(name=gname, members=members, n=n, bufs=bufs, sems=(ssem, rsem)), tok

    def pair_end_chip_begin(st, after):
        n, gname = st["n"], st["name"]
        bufs = _split_wait(f"rs_pair_wait_{gname}", st["bufs"], *st["sems"], _pair_plan(n), after)
        ps = [_pair_sum(f"rs_sum_{t}{l}", core, bufs[w], bufs[n + w]) for w, (t, l) in enumerate(st["members"])]
        recvs = [lax.empty((3,) + p.shape[1:], BF16) for p in ps]
        ssem, rsem, bufs, tok = _split_start(f"rs_chip_start_{gname}", ps + recvs, 3 * n, _chip_plan(n))
        return dict(name=gname, members=st["members"], n=n, bufs=bufs, sems=(ssem, rsem)), tok

    def chip_end_update(st, after, deps=()):
        n = st["n"]
        bufs = _split_wait(f"rs_chip_wait_{st['name']}", st["bufs"], *st["sems"], _chip_plan(n), after)
        for w, (t, l) in enumerate(st["members"]):
            wst, mst, vst = stacks[t]
            *chain[t], tok = _adamw_shard(f"adamw_{t}{l}", chip, l, wst, mst, vst, bufs[w], bufs[n + w], chain.get(t),
                                          deps)
            adamw_tokens.append(tok)

    pipe = {"pair": [], "chip": [], "deps": []}

    def take_deps():
        deps, pipe["deps"] = pipe["deps"], []
        return deps

    def submit(gname, members, dws):
        st, tok = pair_begin(gname, members, dws)
        pipe["pair"].append(st)
        pipe["deps"].append(tok)

    def advance(after):
        arrived, pipe["chip"] = pipe["chip"], []
        toks = []
        for st in pipe["pair"]:
            new, tok = pair_end_chip_begin(st, [after])
            pipe["chip"].append(new)
            toks.append(tok)
        pipe["pair"] = []
        done = len(adamw_tokens)
        for st in arrived:
            chip_end_update(st, [after], toks)
        pipe["deps"] += toks + adamw_tokens[done:]

    def small_start(tag, arrs, after):
        land = _cast_place(f"place_small_{tag}", dev1, _pack(arrs)[None], 0, F32)
        ssem, rsem, lands, tok = _split_start(f"small_start_{tag}", [land], N_DEV - 1, _broadcast_plan, after)
        return (lands, ssem, rsem), tok

    def small_end(tag, state, shapes, after):
        lands, ssem, rsem = state
        lands = _split_wait(f"small_wait_{tag}", lands, ssem, rsem, _broadcast_plan, after)
        return _unpack(_sum_slots(f"small_sum_{tag}", lands[0]), shapes)

    def late_small():
        return [dg_mix[0], dg_ffn[0], dg_mem[0], d_conv_b[0][None], d_conv_w[0][None]]

    assert depth >= 2
    big = {k: [None] * n for k, n in (("kv", depth), ("ain", depth // 2 + depth % 2), ("aout", depth // 2 + depth % 2),
                                      ("bin", depth // 2), ("bout", depth // 2), ("up", depth), ("down", depth))}
    dg_mix, dg_ffn, dg_mem = [None] * depth, [None] * depth, [None] * depth
    d_conv_w, d_conv_b = [None] * depth, [None] * depth
    d_gv, d_ws, d_sb = [None] * (depth // 2), [None] * (depth // 2), [None] * (depth // 2)
    for i in reversed(range(depth)):
        j = i // 2
        lw, sv = W[i], saved[i]
        cw, cb = conv_params(i)
        mix_members, ffn_members = groups_of(i)
        if i == 0:
            early_arrays = [loss_local.reshape(1), dg_final.reshape(D), jnp.concatenate(dg_mix[1:]),
                            jnp.concatenate(dg_ffn[1:]), jnp.concatenate(dg_mem[1:]), jnp.stack(d_ws), jnp.stack(d_sb),
                            jnp.stack(d_gv), jnp.stack(d_conv_b[1:]), jnp.stack(d_conv_w[1:])]
            early_state, tok = small_start("early", early_arrays, [dx])
            pipe["deps"].append(tok)
        deps = take_deps()
        dact = _mm_dx_full(f"ddown{i}", dx16, lw["down"], tm=1024, tko=FF // 4, tc=D, deps=deps)
        big["down"][i] = _mm_dw(f"wdown{i}", sv["act"], dx16, deps=deps).reshape(N_DEV, FF // N_DEV, D)
        da3, dcw, dcb = _conv_bwd(f"dconv{i}", sv["a3"], cw, cb, dact)
        d_conv_w[i] = dcw.transpose(1, 0, 2).reshape(3, 2 * FF)
        d_conv_b[i] = dcb.reshape(2 * FF)
        advance(da3)
        deps = take_deps()
        dh2 = _mm_dx_gcols(f"dup{i}", da3, lw["up"], split_in=True, deps=deps, nsub=2)
        big["up"][i] = _mm_dw_gcols(f"wup{i}", sv["h2t"], da3, N_DEV, split_in=True, deps=deps, a_t=True)
        dx1, dg_ffn[i], dx1_16 = _rms_bwd(f"dffnnorm{i}", dh2, sv["x1"], ffn_norm_g[i], dx)
        submit(f"f{i}", ffn_members, [big["up"][i], big["down"][i]])
        deps = take_deps()
        if i % 2 == 0:
            dcat = _mm_dx_gcols(f"daout{i}", dx1_16, lw["out"], deps=deps, nsub=N_DEV)
            big["aout"][j] = _mm_dw_gcols(f"waout{i}", sv["cat"], dx1_16, N_DEV, deps=deps)
            advance(dcat)
            dqm, dkv = _mem_bwd(f"dmemattn{i}", sv["proj"], a_q_blk, sv["kv"], dcat, 1, deps=take_deps())
            parts = [None] * 9
            for g in range(len(A_PATTERNS)):
                dq, dk, dv = _attn_bwd(f"dattn{i}_{g}", sv["proj"], g, dcat, 0, sv["tok"], sv["lse"])
                parts[g], parts[3 + g], parts[6 + g] = dq, dk, dv
            dproj = jnp.concatenate(parts + [dqm], axis=1)
            deps = []
            dh1 = _mm_dx_gcols(f"dain{i}", dproj, lw["in"], nsub=4)
        else:
            dcat = _mm_dx_full(f"dbout{i}", dx1_16, lw["out"], tm=1024, deps=deps)
            big["bout"][j] = _mm_dw(f"wbout{i}", sv["cat"], dx1_16, deps=deps).reshape(
                N_DEV, (B_W + GW) // N_DEV, D)
            advance(dcat)
            dqm, dkv = _mem_bwd(f"dmemattn{i}", sv["proj"], b_q_blk, sv["kv"], dcat, B_W // GW, deps=take_deps())
            bias_t = b_s_bias[j].T
            du, dvp, dgv, dws, dbt = _sgu_bwd(f"dsgu{i}", sv["proj"], gv_full[j], b_w_s[j], bias_t, dcat)
            d_gv[j], d_ws[j], d_sb[j] = dgv.reshape(B_W), dws, dbt.T
            dproj = jnp.concatenate([du, dvp, dqm], axis=1)
            deps = []
            dh1 = _mm_dx_full(f"dbin{i}", dproj, lw["in"], tc=b_in // 2)
        dmemn = _mm_dx_full(f"dkvproj{i}", dkv, lw["kv"], tko=1024)
        _, dg_mem[i], _ = _rms_bwd(f"dmemnorm{i}", dmemn, mem0, mem_norm_g[i])
        dx, dg_mix[i], dx16 = _rms_bwd(f"dmixnorm{i}", dh1, sv["x0"], mix_norm_g[i], dx1)
        if i == 0:
            late_state, late_tok = small_start("late", late_small(), [dx])
            deps = deps + [late_tok]
        if i % 2 == 0:
            big["ain"][j] = _mm_dw_gcols(f"wain{i}", sv["h1t"], dproj, N_DEV, deps=deps, a_t=True)
        else:
            dwin = _mm_dw(f"wbin{i}", sv["h1t"], dproj, tko=1024, tn=512, deps=deps, a_t=True)
            big["bin"][j] = dwin.reshape(D, N_DEV, b_in // N_DEV).transpose(1, 0, 2)
        big["kv"][i] = _mm_dw(f"wkv{i}", sv["memn"], dkv, tko=1024, deps=deps).reshape(N_DEV, D // N_DEV, 2 * GW)
        submit(f"m{i}", mix_members, [big[t][l] for t, l in mix_members])
    grad_x = dx[None]

    last_chips, toks = [], []
    for st in pipe["pair"]:
        new, tok = pair_end_chip_begin(st, [dx])
        last_chips.append(new)
        toks.append(tok)
    g_early = small_end("early", early_state, [a.shape for a in early_arrays], [dx])
    for st in pipe["chip"]:
        chip_end_update(st, [g_early[1]], toks)
    g_late = small_end("late", late_state, [a.shape for a in late_small()], list(adamw_tokens))

    loss = g_early[0][0]
    layer0 = dict(zip(("mix", "ffn", "mem", "conv_b", "conv_w"), g_late))
    rest = dict(zip(("final", "mix", "ffn", "mem", "w_s", "s_bias", "gv", "conv_b", "conv_w"), g_early[1:]))
    g_cw_full = jnp.concatenate([layer0["conv_w"], rest["conv_w"]])
    g_gv = lax.dynamic_slice_in_dim(rest["gv"], dev * b_v_norm_g.shape[1], b_v_norm_g.shape[1], axis=1)
    g_cw = lax.dynamic_slice_in_dim(g_cw_full, dev * ffn_conv_w.shape[2], ffn_conv_w.shape[2], axis=2)
    g_all = [jnp.concatenate([layer0["mix"], rest["mix"]]), jnp.concatenate([layer0["ffn"], rest["ffn"]]),
             jnp.concatenate([layer0["mem"], rest["mem"]]), rest["w_s"], rest["s_bias"],
             jnp.concatenate([layer0["conv_b"], rest["conv_b"]]), rest["final"], g_gv, g_cw]
    names = ["mix_norm_g", "ffn_norm_g", "mem_norm_g", "b_w_s", "b_s_bias", "ffn_conv_b", "final_norm_g",
             "b_v_norm_g", "ffn_conv_w"]
    ws = [mix_norm_g, ffn_norm_g, mem_norm_g, b_w_s, b_s_bias, ffn_conv_b, final_norm_g, b_v_norm_g, ffn_conv_w]
    ms = [m_mix_norm_g, m_ffn_norm_g, m_mem_norm_g, m_b_w_s, m_b_s_bias, m_ffn_conv_b, m_final_norm_g,
          m_b_v_norm_g, m_ffn_conv_w]
    vs = [v_mix_norm_g, v_ffn_norm_g, v_mem_norm_g, v_b_w_s, v_b_s_bias, v_ffn_conv_b, v_final_norm_g,
          v_b_v_norm_g, v_ffn_conv_w]
    shapes = [w.shape for w in ws]
    d_p, m_p, v_p = _adamw_flat("adamw_small", _pack(ws), _pack(g_all), _pack(ms), _pack(vs))
    res = {}
    for n, g, d, nm, nv in zip(names, g_all, _unpack(d_p, shapes), _unpack(m_p, shapes), _unpack(v_p, shapes)):
        res[n] = [g, d, nm, nv]
    for st in last_chips:
        chip_end_update(st, [d_p] + list(adamw_tokens))
    for tag, name in (("kv", "w_mem_kv"), ("ain", "a_w_in"), ("aout", "a_w_out"), ("bin", "b_w_in"),
                      ("bout", "b_w_out"), ("up", "ffn_w_up"), ("down", "ffn_w_down")):
        res[name] = list(chain[tag])

    order = ["mix_norm_g", "ffn_norm_g", "mem_norm_g", "w_mem_kv", "a_w_in", "a_w_out", "b_w_in", "b_v_norm_g",
             "b_w_s", "b_s_bias", "b_w_out", "ffn_w_up", "ffn_conv_w", "ffn_conv_b", "ffn_w_down", "final_norm_g"]
    return (loss, grad_x, *[res[n][0] for n in order], *[res[n][1] for n in order],
            *[res[n][2] for n in order], *[res[n][3] for n in order])
```

```python
import functools

import numpy as np
import jax
import jax.numpy as jnp
from jax import lax
from jax.experimental import pallas as pl
from jax.experimental.pallas import tpu as pltpu

F32 = jnp.float32
BF16 = jnp.bfloat16
MESH = pl.DeviceIdType.MESH
N_DEV = 8

EPS = 1e-6
NEG = -1e30
HEAD = 128
HPG = 4
GW = HPG * HEAD
A_PATTERNS = ((128, 1), (512, 4), (2048, 16))
A_HEADS = HPG * len(A_PATTERNS)
QBLK = 128
B_GROUPS = 12
B_W = B_GROUPS * HEAD
SLOPES = (2.0 ** (-8.0 * (np.arange(A_HEADS) + 1) / A_HEADS)).astype(np.float32)
SCALE = HEAD ** -0.5

ADAM_LR = 0.001
ADAM_B1 = 0.9
ADAM_B2 = 0.999
ADAM_EPS = 1e-08
ADAM_WD = 0.01
ADAM_STEP = 10

V7X_VMEM_LIMIT = 50 * 1024 * 1024

NN = (((1,), (0,)), ((), ()))
NT = (((1,), (1,)), ((), ()))
TN = (((0,), (0,)), ((), ()))


def _cp(*sem):
    return pltpu.CompilerParams(dimension_semantics=sem, vmem_limit_bytes=V7X_VMEM_LIMIT)


def _dot(a, b, dims=NN):
    return lax.dot_general(a.astype(BF16), b.astype(BF16), dims, preferred_element_type=F32)


def _tile(n, pref):
    t = min(n, pref)
    assert n % t == 0, (n, pref)
    return t


def _row_tile(rows, cols, mib=1):
    best = None
    for t in range(16, rows + 1, 16):
        if rows % t == 0 and t * cols * 4 <= (mib << 20):
            best = t
    if best is None:
        best = rows
    return best


_DEP = pl.BlockSpec((8, 128), lambda *_: (0, 0))


def _matmul(name, dims, grid, a, a_spec, b, b_spec, out_shape, o_spec, tile, res=None, res_spec=None, deps=(),
            nsub=1):
    nk = grid[2]
    has_res = res is not None

    def body(*refs):
        a_ref, b_ref = refs[0], refs[1]
        r_ref = refs[2] if has_res else None
        o_ref, acc_ref = refs[-2], refs[-1]
        if nsub == 1:
            part = _dot(a_ref[...], b_ref[...], dims)
        else:
            w = a_ref.shape[-1] // nsub
            part = _dot(a_ref[:, :w], b_ref[0], dims)
            for q in range(1, nsub):
                part = part + _dot(a_ref[:, q * w:(q + 1) * w], b_ref[q], dims)

        def finish(val):
            if has_res:
                val = val + r_ref[...]
            o_ref[...] = val.astype(o_ref.dtype)

        if nk == 1:
            finish(part)
        else:
            k = pl.program_id(2)

            @pl.when(k == 0)
            def _():
                acc_ref[...] = part

            @pl.when(k > 0)
            def _():
                acc_ref[...] += part

            @pl.when(k == nk - 1)
            def _():
                finish(acc_ref[...])

    ins = [a, b] + ([res] if has_res else []) + list(deps)
    specs = [a_spec, b_spec] + ([res_spec] if has_res else []) + [_DEP] * len(deps)
    return pl.pallas_call(
        body, name=name, grid=grid, in_specs=specs, out_specs=o_spec, out_shape=out_shape,
        scratch_shapes=[pltpu.VMEM(tile if nk > 1 else (8, 128), F32)],
        compiler_params=_cp("parallel", "parallel", "arbitrary"))(*ins)


def _mm_full(name, a, w, res=None, tm=1024, tn=512, tk=2048, deps=()):
    M, K = a.shape
    N = w.shape[1]
    tm, tn, tk = _tile(M, tm), _tile(N, tn), _tile(K, tk)
    return _matmul(
        name, NN, (N // tn, M // tm, K // tk),
        a, pl.BlockSpec((tm, tk), lambda j, i, k: (i, k)),
        w, pl.BlockSpec((tk, tn), lambda j, i, k: (k, j)),
        jax.ShapeDtypeStruct((M, N), F32), pl.BlockSpec((tm, tn), lambda j, i, k: (i, j)), (tm, tn),
        res, pl.BlockSpec((tm, tn), lambda j, i, k: (i, j)), deps=deps)


def _mm_gcols(name, a, wg, res=None, split_out=False, tm=1024, deps=(), out_dtype=F32):
    M, K = a.shape
    G, _, Nl = wg.shape
    tm = _tile(M, tm)
    hg = G // 2
    if split_out:
        shape = jax.ShapeDtypeStruct((2, M, hg * Nl), out_dtype)
        o_spec = pl.BlockSpec((None, tm, Nl), lambda g, i, k: (g // hg, i, g % hg))
    else:
        shape = jax.ShapeDtypeStruct((M, G * Nl), out_dtype)
        o_spec = pl.BlockSpec((tm, Nl), lambda g, i, k: (i, g))
    return _matmul(
        name, NN, (G, M // tm, 1),
        a, pl.BlockSpec((tm, K), lambda g, i, k: (i, 0)),
        wg, pl.BlockSpec((None, K, Nl), lambda g, i, k: (g, 0, 0)),
        shape, o_spec, (tm, Nl),
        res, pl.BlockSpec((tm, Nl), lambda g, i, k: (i, g)), deps=deps)


def _mm_dx_full(name, dy, w, tm=512, tko=512, tc=2048, deps=()):
    M, N = dy.shape
    K = w.shape[0]
    tm, tko, tc = _tile(M, tm), _tile(K, tko), _tile(N, tc)
    return _matmul(
        name, NT, (K // tko, M // tm, N // tc),
        dy, pl.BlockSpec((tm, tc), lambda j, i, k: (i, k)),
        w, pl.BlockSpec((tko, tc), lambda j, i, k: (j, k)),
        jax.ShapeDtypeStruct((M, K), F32), pl.BlockSpec((tm, tko), lambda j, i, k: (i, j)), (tm, tko), deps=deps)


def _mm_dx_gcols(name, dy, wg, split_in=False, tm=1024, tko=1024, deps=(), nsub=1):
    G, K, Nl = wg.shape
    M = dy.shape[-2]
    tm, tko = _tile(M, tm), _tile(K, tko)
    hs = G // 2 // nsub if split_in else None
    if split_in:
        dy_spec = pl.BlockSpec((None, tm, nsub * Nl), lambda j, i, g: (g // hs, i, g % hs))
    else:
        dy_spec = pl.BlockSpec((tm, nsub * Nl), lambda j, i, g: (i, g))
    w_block = (None, tko, Nl) if nsub == 1 else (nsub, tko, Nl)
    return _matmul(
        name, NT, (K // tko, M // tm, G // nsub),
        dy, dy_spec,
        wg, pl.BlockSpec(w_block, lambda j, i, g: (g, j, 0)),
        jax.ShapeDtypeStruct((M, K), F32), pl.BlockSpec((tm, tko), lambda j, i, g: (i, j)), (tm, tko), deps=deps,
        nsub=nsub)


def _lhs_of_dw(a_t, ts, tko, index):
    if a_t:
        return NN, pl.BlockSpec((tko, ts), lambda *ids: index(*ids))
    return TN, pl.BlockSpec((ts, tko), lambda *ids: index(*ids)[::-1])


def _mm_dw(name, a, dy, tko=512, tn=1024, ts=2048, deps=(), a_t=False):
    K1, S = a.shape if a_t else a.shape[::-1]
    N = dy.shape[1]
    tko, tn, ts = _tile(K1, tko), _tile(N, tn), _tile(S, ts)
    dims, a_spec = _lhs_of_dw(a_t, ts, tko, lambda i, j, k: (j, k))
    return _matmul(
        name, dims, (N // tn, K1 // tko, S // ts),
        a, a_spec,
        dy, pl.BlockSpec((ts, tn), lambda i, j, k: (k, i)),
        jax.ShapeDtypeStruct((K1, N), BF16), pl.BlockSpec((tko, tn), lambda i, j, k: (j, i)), (tko, tn), deps=deps)


def _mm_dw_gcols(name, a, dy, G, split_in=False, tko=1024, ts=2048, deps=(), a_t=False):
    K1, S = a.shape if a_t else a.shape[::-1]
    Nl = (dy.shape[-1] * (2 if split_in else 1)) // G
    tko, ts = _tile(K1, tko), _tile(S, ts)
    hg = G // 2
    dims, a_spec = _lhs_of_dw(a_t, ts, tko, lambda g, j, k: (j, k))
    if split_in:
        dy_spec = pl.BlockSpec((None, ts, Nl), lambda g, j, k: (g // hg, k, g % hg))
    else:
        dy_spec = pl.BlockSpec((ts, Nl), lambda g, j, k: (k, g))
    return _matmul(
        name, dims, (G, K1 // tko, S // ts),
        a, a_spec,
        dy, dy_spec,
        jax.ShapeDtypeStruct((G, K1, Nl), BF16), pl.BlockSpec((None, tko, Nl), lambda g, j, k: (g, j, 0)),
        (tko, Nl), deps=deps)


def _rms_fwd(name, x, g, tr=256, deps=(), with_t=False):
    S, D = x.shape
    tr = _tile(S, tr)

    def body(x_ref, g_ref, *rest):
        xf = x_ref[...]
        r = lax.rsqrt(jnp.mean(xf * xf, axis=-1, keepdims=True) + EPS)
        y = xf * r * g_ref[...]
        if with_t:
            rest[-2][...] = y.astype(BF16)
            rest[-1][...] = y.T.astype(BF16)
        else:
            rest[-1][...] = y.astype(BF16)

    row = pl.BlockSpec((tr, D), lambda i: (i, 0))
    out_specs, out_shape = row, jax.ShapeDtypeStruct((S, D), BF16)
    if with_t:
        out_specs = [row, pl.BlockSpec((D, tr), lambda i: (0, i))]
        out_shape = [out_shape, jax.ShapeDtypeStruct((D, S), BF16)]
    return pl.pallas_call(
        body, name=name, grid=(S // tr,),
        in_specs=[row, pl.BlockSpec((1, D), lambda i: (0, 0))] + [_DEP] * len(deps),
        out_specs=out_specs, out_shape=out_shape, compiler_params=_cp("parallel"))(x, g.reshape(1, D), *deps)


def _rms_bwd(name, dh, x, g, dres=None, tr=256):
    S, D = x.shape
    tr = _tile(S, tr)
    has_res = dres is not None

    def body(*refs):
        dh_ref, x_ref, g_ref = refs[:3]
        dres_ref = refs[3] if has_res else None
        dx_ref, dg_ref, dx16_ref = refs[-3:]
        xf = x_ref[...]
        r = lax.rsqrt(jnp.mean(xf * xf, axis=-1, keepdims=True) + EPS)
        xh = xf * r
        dhv = dh_ref[...]
        dxh = dhv * g_ref[...]
        dx = r * (dxh - xh * jnp.mean(dxh * xh, axis=-1, keepdims=True))
        if has_res:
            dx = dx + dres_ref[...]
        dx_ref[...] = dx
        dx16_ref[...] = dx.astype(BF16)
        part = jnp.sum(dhv * xh, axis=0, keepdims=True)
        i = pl.program_id(0)

        @pl.when(i == 0)
        def _():
            dg_ref[...] = part

        @pl.when(i > 0)
        def _():
            dg_ref[...] += part

    row = pl.BlockSpec((tr, D), lambda i: (i, 0))
    vec = pl.BlockSpec((1, D), lambda i: (0, 0))
    ins = [dh, x, g.reshape(1, D)] + ([dres] if has_res else [])
    return pl.pallas_call(
        body, name=name, grid=(S // tr,),
        in_specs=[row, row, vec] + ([row] if has_res else []),
        out_specs=[row, vec, row],
        out_shape=[jax.ShapeDtypeStruct((S, D), F32), jax.ShapeDtypeStruct((1, D), F32),
                   jax.ShapeDtypeStruct((S, D), BF16)],
        compiler_params=_cp("arbitrary"))(*ins)


def _final(name, x, tgt, g, tr=256):
    S, D = x.shape
    tr = _tile(S, tr)

    def body(x_ref, t_ref, g_ref, dx_ref, dg_ref, loss_ref, dx16_ref):
        xf = x_ref[...]
        gv = g_ref[...]
        r = lax.rsqrt(jnp.mean(xf * xf, axis=-1, keepdims=True) + EPS)
        xh = xf * r
        err = xh * gv - t_ref[...]
        sq = jnp.sum(jnp.sum(err * err, axis=1, keepdims=True), axis=0, keepdims=True)
        dy = err * (1.0 / D)
        dxh = dy * gv
        dx = r * (dxh - xh * jnp.mean(dxh * xh, axis=-1, keepdims=True))
        dx_ref[...] = dx
        dx16_ref[...] = dx.astype(BF16)
        part = jnp.sum(dy * xh, axis=0, keepdims=True)
        lpart = jnp.broadcast_to(sq, (8, 128))
        i = pl.program_id(0)

        @pl.when(i == 0)
        def _():
            dg_ref[...] = part
            loss_ref[...] = lpart

        @pl.when(i > 0)
        def _():
            dg_ref[...] += part
            loss_ref[...] += lpart

    row = pl.BlockSpec((tr, D), lambda i: (i, 0))
    vec = pl.BlockSpec((1, D), lambda i: (0, 0))
    return pl.pallas_call(
        body, name=name, grid=(S // tr,), in_specs=[row, row, vec],
        out_specs=[row, vec, pl.BlockSpec((8, 128), lambda i: (0, 0)), row],
        out_shape=[jax.ShapeDtypeStruct((S, D), F32), jax.ShapeDtypeStruct((1, D), F32),
                   jax.ShapeDtypeStruct((8, 128), F32), jax.ShapeDtypeStruct((S, D), BF16)],
        compiler_params=_cp("arbitrary"))(x, tgt, g.reshape(1, D))


def _band_specs(nb, col_of):
    prev = pl.BlockSpec((QBLK, GW), lambda r, b: (jnp.maximum(b - 1, 0), col_of(r)))
    cur = pl.BlockSpec((QBLK, GW), lambda r, b: (b, col_of(r)))
    nxt = pl.BlockSpec((QBLK, GW), lambda r, b: (jnp.minimum(b + 1, nb - 1), col_of(r)))
    return [prev, cur, nxt]


HALF = QBLK // 2
WIN = 2 * QBLK


def _cat3(refs, sl):
    prev, cur, nxt = refs
    return jnp.concatenate([prev[HALF:, sl], cur[:, sl], nxt[:HALF, sl]], axis=0)


def _group_view(proj, g):
    _, dil = A_PATTERNS[g]
    S, C = proj.shape
    ng = len(A_PATTERNS)
    if dil == 1:
        return proj, lambda which, r: which * ng + g
    cols = [proj[:, (which * ng + g) * GW:(which * ng + g + 1) * GW] for which in range(3)]
    return jnp.concatenate(cols, axis=1).reshape(S // dil, dil * 3 * GW), lambda which, r: r * 3 + which


def _attn_fwd(name, proj, g):
    window, dil = A_PATTERNS[g]
    n_side = (window // 2) // dil
    S, C = proj.shape
    L = S // dil
    nb = L // QBLK
    pv, col = _group_view(proj, g)

    assert n_side == HALF

    def body(q_ref, kp, kc, kn, vp, vc, vn, o_ref, lse_ref):
        b = pl.program_id(1)
        jq = b * QBLK + lax.broadcasted_iota(jnp.int32, (QBLK, WIN), 0)
        jk = b * QBLK - HALF + lax.broadcasted_iota(jnp.int32, (QBLK, WIN), 1)
        rel = jnp.abs(jk - jq)
        mask = (rel <= n_side) & (jk >= 0) & (jk < L)
        dist = rel.astype(F32) * float(dil)
        for hh in range(HPG):
            sl = slice(hh * HEAD, (hh + 1) * HEAD)
            k = _cat3((kp, kc, kn), sl)
            v = _cat3((vp, vc, vn), sl)
            s = _dot(q_ref[:, sl], k, NT) * SCALE - float(SLOPES[g * HPG + hh]) * dist
            s = jnp.where(mask, s, NEG)
            m = jnp.max(s, axis=1, keepdims=True)
            p = jnp.exp(s - m)
            l = jnp.sum(p, axis=1, keepdims=True)
            o_ref[:, sl] = _dot(p, v) / l
            lse_ref[:, sl] = jnp.broadcast_to(m + jnp.log(l), (QBLK, HEAD))

    q_spec = pl.BlockSpec((QBLK, GW), lambda r, b: (b, col(0, r)))
    k_specs = _band_specs(nb, lambda r: col(1, r))
    v_specs = _band_specs(nb, lambda r: col(2, r))
    o_spec = pl.BlockSpec((QBLK, GW), lambda r, b: (b, r))
    shape = jax.ShapeDtypeStruct((L, dil * GW), F32)
    o, lse = pl.pallas_call(
        body, name=name, grid=(dil, nb), in_specs=[q_spec] + k_specs + v_specs,
        out_specs=[o_spec, o_spec], out_shape=[shape, shape],
        compiler_params=_cp("parallel", "parallel"))(pv, pv, pv, pv, pv, pv, pv)
    return o.reshape(S, GW), lse.reshape(S, GW)


def _attn_merge(name, outs, lses, tr=256):
    S = outs[0].shape[0]
    tr = _tile(S, tr)
    ng = len(outs)

    def body(*refs):
        o_refs, l_refs = refs[:ng], refs[ng:2 * ng]
        tok_ref, lse_ref = refs[-2], refs[-1]
        ls = [r[...] for r in l_refs]
        m = functools.reduce(jnp.maximum, ls)
        es = [jnp.exp(l - m) for l in ls]
        tot = functools.reduce(lambda a, b: a + b, es)
        acc = None
        for e, o_ref in zip(es, o_refs):
            term = (e / tot) * o_ref[...]
            acc = term if acc is None else acc + term
        tok_ref[...] = acc
        lse_ref[...] = m + jnp.log(tot)

    row = pl.BlockSpec((tr, GW), lambda i: (i, 0))
    shape = jax.ShapeDtypeStruct((S, GW), F32)
    return pl.pallas_call(
        body, name=name, grid=(S // tr,), in_specs=[row] * (2 * ng), out_specs=[row, row],
        out_shape=[shape, shape], compiler_params=_cp("parallel"))(*outs, *lses)


def _attn_bwd(name, proj, g, dtok_src, dtok_blk, tok, lse):
    window, dil = A_PATTERNS[g]
    n_side = (window // 2) // dil
    S, C = proj.shape
    L = S // dil
    nb = L // QBLK
    pv, col = _group_view(proj, g)
    assert n_side == HALF
    if dil == 1:
        dcb, dv_ = dtok_src.shape[1] // GW, dtok_src
    else:
        dcb, dv_ = 1, dtok_src[:, dtok_blk * GW:(dtok_blk + 1) * GW].reshape(L, dil * GW)
        dtok_blk = 0
    ov = tok.reshape(L, dil * GW)
    lv = lse.reshape(L, dil * GW)

    def body(qp, qc, qn, kp, kc, kn, vp, vc, vn, dop, doc, don, op, oc, on, lp, lc, ln,
             dq_ref, dk_ref, dv_ref):
        b = pl.program_id(1)
        jq = b * QBLK + lax.broadcasted_iota(jnp.int32, (QBLK, WIN), 0)
        jk = b * QBLK - HALF + lax.broadcasted_iota(jnp.int32, (QBLK, WIN), 1)
        rel = jnp.abs(jk - jq)
        mask = (rel <= n_side) & (jk >= 0) & (jk < L)
        dist = rel.astype(F32) * float(dil)
        jq3 = b * QBLK - HALF + lax.broadcasted_iota(jnp.int32, (WIN, QBLK), 0)
        jk1 = b * QBLK + lax.broadcasted_iota(jnp.int32, (WIN, QBLK), 1)
        rel3 = jnp.abs(jk1 - jq3)
        mask3 = (rel3 <= n_side) & (jq3 >= 0) & (jq3 < L)
        dist3 = rel3.astype(F32) * float(dil)
        for hh in range(HPG):
            sl = slice(hh * HEAD, (hh + 1) * HEAD)
            one = slice(hh * HEAD, hh * HEAD + 1)
            slope = float(SLOPES[g * HPG + hh])
            q = qc[:, sl]
            do = doc[:, sl]
            k3 = _cat3((kp, kc, kn), sl)
            v3 = _cat3((vp, vc, vn), sl)
            delta = jnp.sum(do * oc[:, sl], axis=1, keepdims=True)
            s = _dot(q, k3, NT) * SCALE - slope * dist
            p = jnp.where(mask, jnp.exp(s - lc[:, one]), 0.0)
            ds = p * (_dot(do, v3, NT) - delta)
            dq_ref[:, sl] = (_dot(ds, k3) * SCALE).astype(dq_ref.dtype)

            q3 = _cat3((qp, qc, qn), sl)
            do3 = _cat3((dop, doc, don), sl)
            o3 = _cat3((op, oc, on), sl)
            lse3 = _cat3((lp, lc, ln), sl)[:, :1]
            delta3 = jnp.sum(do3 * o3, axis=1, keepdims=True)
            k = kc[:, sl]
            v = vc[:, sl]
            s3 = _dot(q3, k, NT) * SCALE - slope * dist3
            p3 = jnp.where(mask3, jnp.exp(s3 - lse3), 0.0)
            ds3 = p3 * (_dot(do3, v, NT) - delta3)
            dv_ref[:, sl] = _dot(p3, do3, TN).astype(dv_ref.dtype)
            dk_ref[:, sl] = (_dot(ds3, q3, TN) * SCALE).astype(dk_ref.dtype)

    specs = (_band_specs(nb, lambda r: col(0, r)) + _band_specs(nb, lambda r: col(1, r))
             + _band_specs(nb, lambda r: col(2, r))
             + _band_specs(nb, lambda r: r * dcb + dtok_blk)
             + _band_specs(nb, lambda r: r) + _band_specs(nb, lambda r: r))
    o_spec = pl.BlockSpec((QBLK, GW), lambda r, b: (b, r))
    shape = jax.ShapeDtypeStruct((L, dil * GW), BF16)
    outs = pl.pallas_call(
        body, name=name, grid=(dil, nb), in_specs=specs, out_specs=[o_spec] * 3, out_shape=[shape] * 3,
        compiler_params=_cp("parallel", "parallel"))(*([pv] * 9 + [dv_] * 3 + [ov] * 3 + [lv] * 3))
    return [o.reshape(S, GW) for o in outs]


def _mem_fwd(name, proj, q_blk, kv, tq=256):
    S = proj.shape[0]
    M = kv.shape[0]
    tq = _tile(S, tq)

    def body(q_ref, kv_ref, o_ref):
        for hh in range(HPG):
            sl = slice(hh * HEAD, (hh + 1) * HEAD)
            k = kv_ref[:, sl]
            v = kv_ref[:, GW + hh * HEAD:GW + (hh + 1) * HEAD]
            s = _dot(q_ref[:, sl], k, NT) * SCALE
            m = jnp.max(s, axis=1, keepdims=True)
            p = jnp.exp(s - m)
            p = p / jnp.sum(p, axis=1, keepdims=True)
            o_ref[:, sl] = _dot(p, v)

    return pl.pallas_call(
        body, name=name, grid=(S // tq,),
        in_specs=[pl.BlockSpec((tq, GW), lambda i: (i, q_blk)), pl.BlockSpec((M, 2 * GW), lambda i: (0, 0))],
        out_specs=pl.BlockSpec((tq, GW), lambda i: (i, 0)),
        out_shape=jax.ShapeDtypeStruct((S, GW), F32), compiler_params=_cp("parallel"))(proj, kv)


def _mem_bwd(name, proj, q_blk, kv, dcat, do_blk, tq=256, deps=()):
    S = proj.shape[0]
    M = kv.shape[0]
    tq = _tile(S, tq)

    def body(q_ref, kv_ref, do_ref, *rest):
        dq_ref, dkv_ref = rest[-2:]
        i = pl.program_id(0)
        for hh in range(HPG):
            sl = slice(hh * HEAD, (hh + 1) * HEAD)
            vsl = slice(GW + hh * HEAD, GW + (hh + 1) * HEAD)
            q = q_ref[:, sl]
            do = do_ref[:, sl]
            k = kv_ref[:, sl]
            v = kv_ref[:, vsl]
            s = _dot(q, k, NT) * SCALE
            m = jnp.max(s, axis=1, keepdims=True)
            p = jnp.exp(s - m)
            p = p / jnp.sum(p, axis=1, keepdims=True)
            dp = _dot(do, v, NT)
            ds = p * (dp - jnp.sum(dp * p, axis=1, keepdims=True))
            dq_ref[:, sl] = (_dot(ds, k) * SCALE).astype(dq_ref.dtype)
            dk = _dot(ds, q, TN) * SCALE
            dvv = _dot(p, do, TN)

            @pl.when(i == 0)
            def _():
                dkv_ref[:, sl] = dk
                dkv_ref[:, vsl] = dvv

            @pl.when(i > 0)
            def _():
                dkv_ref[:, sl] += dk
                dkv_ref[:, vsl] += dvv

    return pl.pallas_call(
        body, name=name, grid=(S // tq,),
        in_specs=[pl.BlockSpec((tq, GW), lambda i: (i, q_blk)), pl.BlockSpec((M, 2 * GW), lambda i: (0, 0)),
                  pl.BlockSpec((tq, GW), lambda i: (i, do_blk))] + [_DEP] * len(deps),
        out_specs=[pl.BlockSpec((tq, GW), lambda i: (i, 0)), pl.BlockSpec((M, 2 * GW), lambda i: (0, 0))],
        out_shape=[jax.ShapeDtypeStruct((S, GW), BF16), jax.ShapeDtypeStruct((M, 2 * GW), F32)],
        compiler_params=_cp("arbitrary"))(proj, kv, dcat, *deps)


_RSQRT2 = float(1.0 / np.sqrt(2.0))
_RSQRT2PI = float(1.0 / np.sqrt(2.0 * np.pi))


def _gelu(x):
    return 0.5 * x * (1.0 + lax.erf(x * _RSQRT2))


def _gelu_and_grad(x):
    cdf = 0.5 * (1.0 + lax.erf(x * _RSQRT2))
    return x * cdf, cdf + x * jnp.exp(-0.5 * x * x) * _RSQRT2PI


def _sgu_fwd(name, proj, gv, w_s, bias_t):
    S = proj.shape[0]
    nch = S // HEAD

    def body(u_ref, v_ref, gv_ref, ws_ref, b_ref, o_ref):
        v = _gelu(v_ref[...])
        r = lax.rsqrt(jnp.mean(v * v, axis=-1, keepdims=True) + EPS)
        vn = v * r * gv_ref[...]
        for gg in range(B_GROUPS):
            sl = slice(gg * HEAD, (gg + 1) * HEAD)
            mixed = _dot(ws_ref[gg], vn[:, sl]) + b_ref[:, gg:gg + 1]
            o_ref[:, sl] = _gelu(u_ref[:, sl]) * mixed

    return pl.pallas_call(
        body, name=name, grid=(nch,),
        in_specs=[pl.BlockSpec((HEAD, B_W), lambda c: (c, 0)), pl.BlockSpec((HEAD, B_W), lambda c: (c, 1)),
                  pl.BlockSpec((1, B_W), lambda c: (0, 0)),
                  pl.BlockSpec((B_GROUPS, HEAD, HEAD), lambda c: (0, 0, 0)),
                  pl.BlockSpec((HEAD, B_GROUPS), lambda c: (0, 0))],
        out_specs=pl.BlockSpec((HEAD, B_W), lambda c: (c, 0)),
        out_shape=jax.ShapeDtypeStruct((S, B_W), F32),
        compiler_params=_cp("parallel"))(proj, proj, gv.reshape(1, B_W), w_s, bias_t)


def _sgu_bwd(name, proj, gv, w_s, bias_t, dcat):
    S = proj.shape[0]
    nch = S // HEAD

    def body(u_ref, v_ref, gv_ref, ws_ref, b_ref, dt_ref, du_ref, dvp_ref, dgv_ref, dws_ref, db_ref, dvn_ref):
        c = pl.program_id(0)
        vpre = v_ref[...]
        v, v_slope = _gelu_and_grad(vpre)
        r = lax.rsqrt(jnp.mean(v * v, axis=-1, keepdims=True) + EPS)
        vh = v * r
        gvv = gv_ref[...]
        vn = vh * gvv
        for gg in range(B_GROUPS):
            sl = slice(gg * HEAD, (gg + 1) * HEAD)
            upre = u_ref[:, sl]
            dt = dt_ref[:, sl]
            vng = vn[:, sl]
            mixed = _dot(ws_ref[gg], vng) + b_ref[:, gg:gg + 1]
            u, u_slope = _gelu_and_grad(upre)
            du_ref[:, sl] = (dt * mixed * u_slope).astype(du_ref.dtype)
            dmix = dt * u
            dvn_ref[:, sl] = _dot(ws_ref[gg], dmix, TN)
            dws = _dot(dmix, vng, NT)
            dbs = jnp.sum(dmix, axis=1, keepdims=True)

            @pl.when(c == 0)
            def _():
                dws_ref[gg] = dws
                db_ref[:, gg:gg + 1] = dbs

            @pl.when(c > 0)
            def _():
                dws_ref[gg] += dws
                db_ref[:, gg:gg + 1] += dbs

        dvn = dvn_ref[...]
        dgp = jnp.sum(dvn * vh, axis=0, keepdims=True)
        dvh = dvn * gvv
        dv = r * (dvh - vh * jnp.mean(dvh * vh, axis=-1, keepdims=True))
        dvp_ref[...] = (dv * v_slope).astype(dvp_ref.dtype)

        @pl.when(c == 0)
        def _():
            dgv_ref[...] = dgp

        @pl.when(c > 0)
        def _():
            dgv_ref[...] += dgp

    blk = lambda j: pl.BlockSpec((HEAD, B_W), lambda c: (c, j))
    vec = pl.BlockSpec((1, B_W), lambda c: (0, 0))
    ws_spec = pl.BlockSpec((B_GROUPS, HEAD, HEAD), lambda c: (0, 0, 0))
    b_spec = pl.BlockSpec((HEAD, B_GROUPS), lambda c: (0, 0))
    du, dvp, dgv, dws, db = pl.pallas_call(
        body, name=name, grid=(nch,),
        in_specs=[blk(0), blk(1), vec, ws_spec, b_spec, blk(0)],
        out_specs=[blk(0), blk(0), vec, ws_spec, b_spec],
        out_shape=[jax.ShapeDtypeStruct((S, B_W), BF16), jax.ShapeDtypeStruct((S, B_W), BF16),
                   jax.ShapeDtypeStruct((1, B_W), F32), jax.ShapeDtypeStruct((B_GROUPS, HEAD, HEAD), F32),
                   jax.ShapeDtypeStruct((HEAD, B_GROUPS), F32)],
        scratch_shapes=[pltpu.VMEM((HEAD, B_W), F32)],
        compiler_params=_cp("arbitrary"))(proj, proj, gv.reshape(1, B_W), w_s, bias_t, dcat)
    return du, dvp, dgv, dws, db


def _shift_down(a, row):
    return jnp.where(row == 0, 0.0, pltpu.roll(a, 1, 0))


def _shift_up(a, row):
    n = a.shape[0]
    return jnp.where(row == n - 1, 0.0, pltpu.roll(a, n - 1, 0))


def _conv(a, w, b, row):
    return _shift_down(a, row) * w[0:1] + a * w[1:2] + _shift_up(a, row) * w[2:3] + b


def _conv_fwd(name, a3, cw, cb, tc=256):
    _, S, FF = a3.shape
    tc = _tile(FF, tc)

    def body(a_ref, w_ref, b_ref, o_ref):
        row = lax.broadcasted_iota(jnp.int32, (S, tc), 0)
        cg = _conv(a_ref[0], w_ref[0], b_ref[0], row)
        cv = _conv(a_ref[1], w_ref[1], b_ref[1], row)
        o_ref[...] = (_gelu(cg) * cv).astype(o_ref.dtype)

    return pl.pallas_call(
        body, name=name, grid=(FF // tc,),
        in_specs=[pl.BlockSpec((2, S, tc), lambda j: (0, 0, j)), pl.BlockSpec((2, 3, tc), lambda j: (0, 0, j)),
                  pl.BlockSpec((2, 1, tc), lambda j: (0, 0, j))],
        out_specs=pl.BlockSpec((S, tc), lambda j: (0, j)),
        out_shape=jax.ShapeDtypeStruct((S, FF), BF16), compiler_params=_cp("parallel"))(a3, cw, cb)


def _conv_bwd(name, a3, cw, cb, dact, tc=128):
    _, S, FF = a3.shape
    tc = _tile(FF, tc)

    def body(a_ref, w_ref, b_ref, d_ref, da_ref, dw_ref, db_ref):
        row = lax.broadcasted_iota(jnp.int32, (S, tc), 0)
        ag, av = a_ref[0], a_ref[1]
        wg, wv = w_ref[0], w_ref[1]
        cg = _conv(ag, wg, b_ref[0], row)
        cv = _conv(av, wv, b_ref[1], row)
        d = d_ref[...]
        gate, gate_slope = _gelu_and_grad(cg)
        dcs = (d * cv * gate_slope, d * gate)
        for h, (dc, a, w) in enumerate(zip(dcs, (ag, av), (wg, wv))):
            da = _shift_up(dc, row) * w[0:1] + dc * w[1:2] + _shift_down(dc, row) * w[2:3]
            da_ref[h] = da.astype(da_ref.dtype)
            dw_ref[h, 0:1, :] = jnp.sum(dc * _shift_down(a, row), axis=0, keepdims=True)
            dw_ref[h, 1:2, :] = jnp.sum(dc * a, axis=0, keepdims=True)
            dw_ref[h, 2:3, :] = jnp.sum(dc * _shift_up(a, row), axis=0, keepdims=True)
            db_ref[h] = jnp.sum(dc, axis=0, keepdims=True)

    a_spec = pl.BlockSpec((2, S, tc), lambda j: (0, 0, j))
    w_spec = pl.BlockSpec((2, 3, tc), lambda j: (0, 0, j))
    b_spec = pl.BlockSpec((2, 1, tc), lambda j: (0, 0, j))
    return pl.pallas_call(
        body, name=name, grid=(FF // tc,),
        in_specs=[a_spec, w_spec, b_spec, pl.BlockSpec((S, tc), lambda j: (0, j))],
        out_specs=[a_spec, w_spec, b_spec],
        out_shape=[jax.ShapeDtypeStruct((2, S, FF), BF16), jax.ShapeDtypeStruct((2, 3, FF), F32),
                   jax.ShapeDtypeStruct((2, 1, FF), F32)],
        compiler_params=_cp("parallel"))(a3, cw, cb, dact)


_HBM = pl.BlockSpec(memory_space=pltpu.HBM)


def _position():
    return lax.axis_index("x"), lax.axis_index("y"), lax.axis_index("c")


_SEM =pl.BlockSpec(memory_space=pltpu.SEMAPHORE)
_EFFECT = pltpu.SideEffectType.DATAFLOW_SIDE_EFFECTING
_FLIPS = ((1, 0), (0, 1), (1, 1))


def _split_start(name, bufs, ncopy, plan, after=()):
    n = len(bufs)
    after = list(after)

    def body(*refs):
        ins = refs[:n]
        send_sems, recv_sems, token = refs[n + len(after)], refs[n + len(after) + 1], refs[-1]
        for i, (src, dst, to) in enumerate(plan(ins)):
            pltpu.make_async_remote_copy(src_ref=src, dst_ref=dst, send_sem=send_sems.at[i], recv_sem=recv_sems.at[i],
                                         device_id=to, device_id_type=MESH).start()
        token[...] = jnp.zeros_like(token)

    outs = pl.pallas_call(
        body, name=name,
        out_shape=(pltpu.SemaphoreType.DMA((ncopy,)), pltpu.SemaphoreType.DMA((ncopy,)),
                   *[pltpu.HBM(b.shape, b.dtype) for b in bufs], jax.ShapeDtypeStruct((8, 128), F32)),
        in_specs=[_HBM] * n + [pl.BlockSpec(memory_space=pl.ANY)] * len(after),
        out_specs=(_SEM, _SEM, *([_HBM] * n), pl.BlockSpec(memory_space=pltpu.VMEM)),
        input_output_aliases={i: 2 + i for i in range(n)},
        compiler_params=pltpu.CompilerParams(has_side_effects=_EFFECT),
    )(*[pltpu.with_memory_space_constraint(b, pltpu.HBM) for b in bufs], *after)
    return outs[0], outs[1], list(outs[2:2 + n]), outs[-1]


def _split_wait(name, bufs, send_sems, recv_sems, plan, after):
    n = len(bufs)
    after = list(after)

    def body(*refs):
        ins = refs[:n]
        ssem, rsem = refs[n], refs[n + 1]
        for i, (src, dst, to) in enumerate(plan(ins)):
            cp = pltpu.make_async_remote_copy(src_ref=src, dst_ref=dst, send_sem=ssem.at[i], recv_sem=rsem.at[i],
                                              device_id=to, device_id_type=MESH)
            cp.wait_send()
            cp.wait_recv()

    outs = pl.pallas_call(
        body, name=name, out_shape=tuple(pltpu.HBM(b.shape, b.dtype) for b in bufs),
        in_specs=[_HBM] * n + [_SEM, _SEM] + [pl.BlockSpec(memory_space=pl.ANY)] * len(after),
        out_specs=tuple([_HBM] * n), input_output_aliases={i: i for i in range(n)},
        compiler_params=pltpu.CompilerParams(has_side_effects=_EFFECT),
    )(*bufs, send_sems, recv_sems, *after)
    return list(outs)


def _gather_plan(refs):
    px, py, pc = _position()
    me = 4 * px + 2 * py + pc
    targets = [(px, py, 1 - pc), (1 - px, py, pc), (px, 1 - py, pc), (1 - px, 1 - py, pc)]
    return [(r.at[me], r.at[me], to) for r in refs for to in targets]


def _forward_plan(refs):
    px, py, pc = _position()
    out = []
    for r in refs:
        for fx, fy in _FLIPS:
            slot = 4 * (1 - px if fx else px) + 2 * (1 - py if fy else py) + pc
            out.append((r.at[slot], r.at[slot], (px, py, 1 - pc)))
    return out


def _pair_plan(n):
    def plan(refs):
        px, py, pc = _position()
        return [(refs[w].at[2 * k + (1 - pc)], refs[n + w].at[k], (px, py, 1 - pc)) for w in range(n) for k in range(4)]
    return plan


def _chip_plan(n):
    def plan(refs):
        px, py, pc = _position()
        out = []
        for w in range(n):
            for j, (fx, fy) in enumerate(_FLIPS):
                qx = 1 - px if fx else px
                qy = 1 - py if fy else py
                out.append((refs[w].at[2 * qx + qy], refs[n + w].at[j], (qx, qy, pc)))
        return out
    return plan


def _broadcast_plan(refs):
    px, py, pc = _position()
    me = 4 * px + 2 * py + pc
    flips = [(fx, fy, fc) for fx in (0, 1) for fy in (0, 1) for fc in (0, 1)][1:]
    targets = [(1 - px if fx else px, 1 - py if fy else py, 1 - pc if fc else pc) for fx, fy, fc in flips]
    return [(r.at[me], r.at[me], to) for r in refs for to in targets]


def _cast_place(name, dev, w, layer, dtype=BF16):
    nl, R, C = w.shape
    tr = _row_tile(R, C, 4)

    def body(dev_ref, w_ref, o_ref):
        o_ref[...] = w_ref[...].astype(o_ref.dtype)

    return pl.pallas_call(
        body, name=name,
        grid_spec=pltpu.PrefetchScalarGridSpec(
            num_scalar_prefetch=1, grid=(R // tr,),
            in_specs=[pl.BlockSpec((None, tr, C), lambda i, d: (layer, i, 0))],
            out_specs=pl.BlockSpec((None, tr, C), lambda i, d: (d[0], i, 0))),
        out_shape=jax.ShapeDtypeStruct((N_DEV, R, C), dtype), compiler_params=_cp("parallel"))(dev, w)


def _pair_sum(name, core, dw, recv):
    _, R, C = dw.shape
    tr = _row_tile(R, C, 4)
    dw4 = dw.reshape(4, 2, R, C)

    def body(core_ref, a_ref, b_ref, o_ref):
        o_ref[...] = (a_ref[...].astype(F32) + b_ref[...].astype(F32)).astype(o_ref.dtype)

    return pl.pallas_call(
        body, name=name,
        grid_spec=pltpu.PrefetchScalarGridSpec(
            num_scalar_prefetch=1, grid=(4, R // tr),
            in_specs=[pl.BlockSpec((None, None, tr, C), lambda k, i, c_ref: (k, c_ref[0], i, 0)),
                      pl.BlockSpec((None, tr, C), lambda k, i, c_ref: (k, i, 0))],
            out_specs=pl.BlockSpec((None, tr, C), lambda k, i, c_ref: (k, i, 0))),
        out_shape=jax.ShapeDtypeStruct((4, R, C), BF16),
        compiler_params=_cp("parallel", "parallel"))(core, dw4, recv)


def _adamw_math(w, g, m, v):
    m = ADAM_B1 * m + (1.0 - ADAM_B1) * g
    v = ADAM_B2 * v + (1.0 - ADAM_B2) * (g * g)
    m_hat = m / (1.0 - ADAM_B1 ** ADAM_STEP)
    v_hat = v / (1.0 - ADAM_B2 ** ADAM_STEP)
    delta = -ADAM_LR * (m_hat / (jnp.sqrt(v_hat) + ADAM_EPS) + ADAM_WD * w)
    return delta, m, v


def _adamw_shard(name, chip, layer, w, m, v, p, recv, prev, deps=()):
    nl, R, C = w.shape
    tr = _row_tile(R, C, 2)
    n_prev = 0 if prev is None else 4

    def body(chip_ref, w_ref, m_ref, v_ref, p_ref, r_ref, *rest):
        g_ref, d_ref, nm_ref, nv_ref, tok_ref = rest[-5:]
        tok_ref[...] = jnp.zeros_like(tok_ref)
        g = p_ref[...].astype(F32)
        for j in range(3):
            g = g + r_ref[j].astype(F32)
        delta, nm, nv = _adamw_math(w_ref[...], g, m_ref[...], v_ref[...])
        g_ref[...] = g
        d_ref[...] = delta
        nm_ref[...] = nm
        nv_ref[...] = nv

    lay = pl.BlockSpec((None, tr, C), lambda i, c_ref: (layer, i, 0))
    in_specs = [lay, lay, lay,
                pl.BlockSpec((None, tr, C), lambda i, c_ref: (c_ref[0], i, 0)),
                pl.BlockSpec((3, tr, C), lambda i, c_ref: (0, i, 0))]
    in_specs += [pl.BlockSpec(memory_space=pl.ANY)] * n_prev + [_DEP] * len(deps)
    shape = jax.ShapeDtypeStruct((nl, R, C), F32)
    ins = [chip, w, m, v, p, recv] + ([] if prev is None else list(prev)) + list(deps)
    return pl.pallas_call(
        body, name=name,
        grid_spec=pltpu.PrefetchScalarGridSpec(
            num_scalar_prefetch=1, grid=(R // tr,), in_specs=in_specs, out_specs=[lay] * 4 + [_DEP]),
        out_shape=[shape] * 4 + [jax.ShapeDtypeStruct((8, 128), F32)],
        input_output_aliases={6 + j: j for j in range(n_prev)},
        compiler_params=_cp("arbitrary"))(*ins)


def _sum_slots(name, parts, tr=512):
    n, R, C = parts.shape
    tr = _tile(R, tr)

    def body(p_ref, o_ref):
        acc = p_ref[0]
        for j in range(1, n):
            acc = acc + p_ref[j]
        o_ref[...] = acc

    return pl.pallas_call(
        body, name=name, grid=(R // tr,),
        in_specs=[pl.BlockSpec((n, tr, C), lambda i: (0, i, 0))],
        out_specs=pl.BlockSpec((tr, C), lambda i: (i, 0)),
        out_shape=jax.ShapeDtypeStruct((R, C), F32), compiler_params=_cp("parallel"))(parts)


def _adamw_flat(name, w, g, m, v, tr=512):
    R, C = w.shape
    tr = _tile(R, tr)

    def body(w_ref, g_ref, m_ref, v_ref, d_ref, nm_ref, nv_ref):
        delta, nm, nv = _adamw_math(w_ref[...], g_ref[...], m_ref[...], v_ref[...])
        d_ref[...] = delta
        nm_ref[...] = nm
        nv_ref[...] = nv

    row = pl.BlockSpec((tr, C), lambda i: (i, 0))
    shape = jax.ShapeDtypeStruct((R, C), F32)
    return pl.pallas_call(
        body, name=name, grid=(R // tr,), in_specs=[row] * 4, out_specs=[row] * 3, out_shape=[shape] * 3,
        compiler_params=_cp("parallel"))(w, g, m, v)


_PACK_ROWS = 512


def _pack(arrs):
    flat = jnp.concatenate([a.reshape(-1) for a in arrs])
    unit = _PACK_ROWS * 128
    pad = (-flat.shape[0]) % unit
    return jnp.pad(flat, (0, pad)).reshape(-1, 128)


def _unpack(packed, shapes):
    flat = packed.reshape(-1)
    outs, off = [], 0
    for s in shapes:
        n = int(np.prod(s))
        outs.append(flat[off:off + n].reshape(s))
        off += n
    return outs


def kernel(x, mem, mix_norm_g, ffn_norm_g, mem_norm_g, w_mem_kv, a_w_in, a_w_out, b_w_in, b_v_norm_g, b_w_s, b_s_bias, b_w_out, ffn_w_up, ffn_conv_w, ffn_conv_b, ffn_w_down, final_norm_g, loss_target, m_mix_norm_g, m_ffn_norm_g, m_mem_norm_g, m_w_mem_kv, m_a_w_in, m_a_w_out, m_b_w_in, m_b_v_norm_g, m_b_w_s, m_b_s_bias, m_b_w_out, m_ffn_w_up, m_ffn_conv_w, m_ffn_conv_b, m_ffn_w_down, m_final_norm_g, v_mix_norm_g, v_ffn_norm_g, v_mem_norm_g, v_w_mem_kv, v_a_w_in, v_a_w_out, v_b_w_in, v_b_v_norm_g, v_b_w_s, v_b_s_bias, v_b_w_out, v_ffn_w_up, v_ffn_conv_w, v_ffn_conv_b, v_ffn_w_down, v_final_norm_g):
    px, py, pc = _position()
    dev = 4 * px + 2 * py + pc
    core = jnp.reshape(pc, (1,)).astype(jnp.int32)
    chip = jnp.reshape(2 * px + py, (1,)).astype(jnp.int32)

    x0 = x[0]
    mem0 = mem[0]
    tgt = loss_target[0]
    S, D = x0.shape
    depth = mix_norm_g.shape[0]
    FF = ffn_w_down.shape[1] * N_DEV
    a_in = a_w_in.shape[2] * N_DEV
    b_in = b_w_in.shape[2] * N_DEV
    a_q_blk = (a_in - GW) // GW
    b_q_blk = (b_in - GW) // GW

    stacks = {"kv": (w_mem_kv, m_w_mem_kv, v_w_mem_kv), "ain": (a_w_in, m_a_w_in, v_a_w_in),
              "aout": (a_w_out, m_a_w_out, v_a_w_out), "bin": (b_w_in, m_b_w_in, v_b_w_in),
              "bout": (b_w_out, m_b_w_out, v_b_w_out), "up": (ffn_w_up, m_ffn_w_up, v_ffn_w_up),
              "down": (ffn_w_down, m_ffn_w_down, v_ffn_w_down)}
    dev1 = jnp.reshape(dev, (1,)).astype(jnp.int32)

    def groups_of(i):
        j = i // 2
        mix = [("kv", i), ("ain", j), ("aout", j)] if i % 2 == 0 else [("kv", i), ("bin", j), ("bout", j)]
        return mix, [("up", i), ("down", i)]

    gather_groups = [(f"{half}{i}", members) for i in range(depth) for half, members in zip("mf", groups_of(i))]
    gather_ahead = 2
    in_flight = {}

    def gather_start(k, after):
        gname, members = gather_groups[k]
        lands = [_cast_place(f"place_{t}{l}", dev1, stacks[t][0], l) for t, l in members]
        ssem, rsem, lands, tok = _split_start(f"ag_start_{gname}", lands, 4 * len(lands), _gather_plan, after)
        in_flight[k] = (lands, ssem, rsem)
        return tok

    small_land = _cast_place("place_small_w", dev1, _pack([ffn_conv_w, b_v_norm_g])[None], 0, F32)
    small_ssem, small_rsem, small_lands, small_tok = _split_start("smallw_start", [small_land], N_DEV - 1,
                                                                  _broadcast_plan)
    start_tokens = [small_tok, gather_start(0, [small_tok])]
    passing = {}

    def gather_arrive(k, after):
        if k >= len(gather_groups):
            return []
        gname, members = gather_groups[k]
        lands, ssem, rsem = in_flight.pop(k)
        lands = _split_wait(f"ag_wait_{gname}", lands, ssem, rsem, _gather_plan, after)
        toks = []
        for q in (range(1, 1 + gather_ahead) if k == 0 else [k + gather_ahead]):
            if q < len(gather_groups):
                toks.append(gather_start(q, [lands[0]] + toks))
        ssem, rsem, lands, tok = _split_start(f"ag_pass_{gname}", lands, 3 * len(lands), _forward_plan, toks)
        passing[k] = (lands, ssem, rsem)
        return toks + [tok]

    def gather_ready(k, after):
        gname, members = gather_groups[k]
        lands, ssem, rsem = passing.pop(k)
        lands = _split_wait(f"ag_ready_{gname}", lands, ssem, rsem, _forward_plan, after)
        out = {}
        for (t, l), land in zip(members, lands):
            if t == "kv":
                out["kv"] = land.reshape(D, 2 * GW)
            elif t in ("ain", "aout", "up"):
                out[{"ain": "in", "aout": "out", "up": "up"}[t]] = land
            elif t == "bin":
                out["in"] = jnp.transpose(land, (1, 0, 2)).reshape(D, b_in)
            elif t == "bout":
                out["out"] = land.reshape(B_W + GW, D)
            else:
                out["down"] = land.reshape(FF, D)
        return out

    def small_weights(after):
        (small_all,) = _split_wait("smallw_wait", small_lands, small_ssem, small_rsem, _broadcast_plan, after)
        cw_parts, gv_parts = [], []
        for d in range(N_DEV):
            cw_d, gv_d = _unpack(small_all[d], [ffn_conv_w.shape, b_v_norm_g.shape])
            cw_parts.append(cw_d)
            gv_parts.append(gv_d)
        return jnp.concatenate(cw_parts, axis=-1), jnp.concatenate(gv_parts, axis=-1)

    def conv_params(i):
        cw = conv_w_full[i].reshape(3, 2, FF).transpose(1, 0, 2)
        cb = ffn_conv_b[i].reshape(2, 1, FF)
        return cw, cb

    saved = []
    W = []
    xc = x0
    toks = start_tokens + gather_arrive(0, [x0])
    for i in range(depth):
        j = i // 2
        lw = gather_ready(2 * i, [xc])
        sv = {"x0": xc}
        h1, h1t = _rms_fwd(f"mixnorm{i}", xc, mix_norm_g[i], deps=toks, with_t=True)
        memn = _rms_fwd(f"memnorm{i}", mem0, mem_norm_g[i])
        if i % 2 == 0:
            proj = _mm_gcols(f"ain{i}", h1, lw["in"], out_dtype=BF16)
        else:
            proj = _mm_full(f"bin{i}", h1, lw["in"])
        kv = _mm_full(f"kvproj{i}", memn, lw["kv"])
        if i % 2 == 0:
            outs, lses = [], []
            for g in range(len(A_PATTERNS)):
                o, l = _attn_fwd(f"attn{i}_{g}", proj, g)
                outs.append(o)
                lses.append(l)
            tok, lse = _attn_merge(f"merge{i}", outs, lses)
            sv.update(tok=tok, lse=lse)
        else:
            tok = _sgu_fwd(f"sgu{i}", proj, gv_full[j], b_w_s[j], b_s_bias[j].T)
        toks = gather_arrive(2 * i + 1, [tok]) if i > 0 else []
        mo = _mem_fwd(f"memattn{i}", proj, a_q_blk if i % 2 == 0 else b_q_blk, kv)
        cat = jnp.concatenate([tok.astype(BF16), mo.astype(BF16)], axis=1)
        if i % 2 == 0:
            x1 = _mm_gcols(f"aout{i}", cat, lw["out"], res=xc, deps=toks)
        else:
            x1 = _mm_full(f"bout{i}", cat, lw["out"], res=xc, deps=toks)
        toks = gather_arrive(1, [x1]) if i == 0 else []
        lw.update(gather_ready(2 * i + 1, [x1]))
        W.append(lw)
        if i == 0:
            conv_w_full, gv_full = small_weights([x1])
        h2, h2t = _rms_fwd(f"ffnnorm{i}", x1, ffn_norm_g[i], deps=toks, with_t=True)
        cw, cb = conv_params(i)
        a3 = _mm_gcols(f"up{i}", h2, lw["up"], split_out=True)
        toks = gather_arrive(2 * i + 2, [a3])
        act = _conv_fwd(f"conv{i}", a3, cw, cb)
        x2 = _mm_full(f"down{i}", act, lw["down"], res=x1, tn=512, tk=FF // 2, deps=toks)
        toks = []
        sv.update(h1t=h1t, memn=memn, kv=kv, proj=proj, cat=cat, x1=x1, h2t=h2t, a3=a3, act=act)
        saved.append(sv)
        xc = x2

    dx, dg_final, sq, dx16 = _final("final", xc, tgt, final_norm_g)
    loss_local = sq[0, 0] * (0.5 / D)

    chain = {}
    adamw_tokens = []

    def pair_begin(gname, members, dws):
        n = len(dws)
        recvs = [lax.empty((4,) + dw.shape[1:], dw.dtype) for dw in dws]
        ssem, rsem, bufs, tok = _split_start(f"rs_pair_start_{gname}", dws + recvs, 4 * n, _pair_plan(n))
        return dict(name=gname, members=members, n=n, bufs=bufs, sems=(ssem, rsem)), tok

    def pair_end_chip_begin(st, after):
        n, gname = st["n"], st["name"]
        bufs = _split_wait(f"rs_pair_wait_{gname}", st["bufs"], *st["sems"], _pair_plan(n), after)
        ps = [_pair_sum(f"rs_sum_{t}{l}", core, bufs[w], bufs[n + w]) for w, (t, l) in enumerate(st["members"])]
        recvs = [lax.empty((3,) + p.shape[1:], BF16) for p in ps]
        ssem, rsem, bufs, tok = _split_start(f"rs_chip_start_{gname}", ps + recvs, 3 * n, _chip_plan(n))
        return dict(name=gname, members=st["members"], n=n, bufs=bufs, sems=(ssem, rsem)), tok

    def chip_end_update(st, after, deps=()):
        n = st["n"]
        bufs = _split_wait(f"rs_chip_wait_{st['name']}", st["bufs"], *st["sems"], _chip_plan(n), after)
        for w, (t, l) in enumerate(st["members"]):
            wst, mst, vst = stacks[t]
            *chain[t], tok = _adamw_shard(f"adamw_{t}{l}", chip, l, wst, mst, vst, bufs[w], bufs[n + w], chain.get(t),
                                          deps)
            adamw_tokens.append(tok)

    pipe = {"pair": [], "chip": [], "deps": []}

    def take_deps():
        deps, pipe["deps"] = pipe["deps"], []
        return deps

    def submit(gname, members, dws):
        st, tok = pair_begin(gname, members, dws)
        pipe["pair"].append(st)
        pipe["deps"].append(tok)

    def advance(after):
        arrived, pipe["chip"] = pipe["chip"], []
        toks = []
        for st in pipe["pair"]:
            new, tok = pair_end_chip_begin(st, [after])
            pipe["chip"].append(new)
            toks.append(tok)
        pipe["pair"] = []
        done = len(adamw_tokens)
        for st in arrived:
            chip_end_update(st, [after], toks)
        pipe["deps"] += toks + adamw_tokens[done:]

    def small_start(tag, arrs, after):
        land = _cast_place(f"place_small_{tag}", dev1, _pack(arrs)[None], 0, F32)
        ssem, rsem, lands, tok = _split_start(f"small_start_{tag}", [land], N_DEV - 1, _broadcast_plan, after)
        return (lands, ssem, rsem), tok

    def small_end(tag, state, shapes, after):
        lands, ssem, rsem = state
        lands = _split_wait(f"small_wait_{tag}", lands, ssem, rsem, _broadcast_plan, after)
        return _unpack(_sum_slots(f"small_sum_{tag}", lands[0]), shapes)

    def late_small():
        return [dg_mix[0], dg_ffn[0], dg_mem[0], d_conv_b[0][None], d_conv_w[0][None]]

    assert depth >= 2
    big = {k: [None] * n for k, n in (("kv", depth), ("ain", depth // 2 + depth % 2), ("aout", depth // 2 + depth % 2),
                                      ("bin", depth // 2), ("bout", depth // 2), ("up", depth), ("down", depth))}
    dg_mix, dg_ffn, dg_mem = [None] * depth, [None] * depth, [None] * depth
    d_conv_w, d_conv_b = [None] * depth, [None] * depth
    d_gv, d_ws, d_sb = [None] * (depth // 2), [None] * (depth // 2), [None] * (depth // 2)
    for i in reversed(range(depth)):
        j = i // 2
        lw, sv = W[i], saved[i]
        cw, cb = conv_params(i)
        mix_members, ffn_members = groups_of(i)
        if i == 0:
            early_arrays = [loss_local.reshape(1), dg_final.reshape(D), jnp.concatenate(dg_mix[1:]),
                            jnp.concatenate(dg_ffn[1:]), jnp.concatenate(dg_mem[1:]), jnp.stack(d_ws), jnp.stack(d_sb),
                            jnp.stack(d_gv), jnp.stack(d_conv_b[1:]), jnp.stack(d_conv_w[1:])]
            early_state, tok = small_start("early", early_arrays, [dx])
            pipe["deps"].append(tok)
        deps = take_deps()
        dact = _mm_dx_full(f"ddown{i}", dx16, lw["down"], tm=1024, tko=FF // 4, tc=D, deps=deps)
        big["down"][i] = _mm_dw(f"wdown{i}", sv["act"], dx16, deps=deps).reshape(N_DEV, FF // N_DEV, D)
        da3, dcw, dcb = _conv_bwd(f"dconv{i}", sv["a3"], cw, cb, dact)
        d_conv_w[i] = dcw.transpose(1, 0, 2).reshape(3, 2 * FF)
        d_conv_b[i] = dcb.reshape(2 * FF)
        advance(da3)
        deps = take_deps()
        dh2 = _mm_dx_gcols(f"dup{i}", da3, lw["up"], split_in=True, deps=deps, nsub=2)
        big["up"][i] = _mm_dw_gcols(f"wup{i}", sv["h2t"], da3, N_DEV, split_in=True, deps=deps, a_t=True)
        dx1, dg_ffn[i], dx1_16 = _rms_bwd(f"dffnnorm{i}", dh2, sv["x1"], ffn_norm_g[i], dx)
        submit(f"f{i}", ffn_members, [big["up"][i], big["down"][i]])
        deps = take_deps()
        if i % 2 == 0:
            dcat = _mm_dx_gcols(f"daout{i}", dx1_16, lw["out"], deps=deps, nsub=N_DEV)
            big["aout"][j] = _mm_dw_gcols(f"waout{i}", sv["cat"], dx1_16, N_DEV, deps=deps)
            advance(dcat)
            dqm, dkv = _mem_bwd(f"dmemattn{i}", sv["proj"], a_q_blk, sv["kv"], dcat, 1, deps=take_deps())
            parts = [None] * 9
            for g in range(len(A_PATTERNS)):
                dq, dk, dv = _attn_bwd(f"dattn{i}_{g}", sv["proj"], g, dcat, 0, sv["tok"], sv["lse"])
                parts[g], parts[3 + g], parts[6 + g] = dq, dk, dv
            dproj = jnp.concatenate(parts + [dqm], axis=1)
            deps = []
            dh1 = _mm_dx_gcols(f"dain{i}", dproj, lw["in"], nsub=4)
        else:
            dcat = _mm_dx_full(f"dbout{i}", dx1_16, lw["out"], tm=1024, tko=1024, deps=deps)
            big["bout"][j] = _mm_dw(f"wbout{i}", sv["cat"], dx1_16, deps=deps).reshape(
                N_DEV, (B_W + GW) // N_DEV, D)
            advance(dcat)
            dqm, dkv = _mem_bwd(f"dmemattn{i}", sv["proj"], b_q_blk, sv["kv"], dcat, B_W // GW, deps=take_deps())
            bias_t = b_s_bias[j].T
            du, dvp, dgv, dws, dbt = _sgu_bwd(f"dsgu{i}", sv["proj"], gv_full[j], b_w_s[j], bias_t, dcat)
            d_gv[j], d_ws[j], d_sb[j] = dgv.reshape(B_W), dws, dbt.T
            dproj = jnp.concatenate([du, dvp, dqm], axis=1)
            deps = []
            dh1 = _mm_dx_full(f"dbin{i}", dproj, lw["in"], tm=1024, tko=1024, tc=b_in)
        dmemn = _mm_dx_full(f"dkvproj{i}", dkv, lw["kv"], tko=1024)
        _, dg_mem[i], _ = _rms_bwd(f"dmemnorm{i}", dmemn, mem0, mem_norm_g[i])
        dx, dg_mix[i], dx16 = _rms_bwd(f"dmixnorm{i}", dh1, sv["x0"], mix_norm_g[i], dx1)
        if i == 0:
            late_state, late_tok = small_start("late", late_small(), [dx])
            deps = deps + [late_tok]
        if i % 2 == 0:
            big["ain"][j] = _mm_dw_gcols(f"wain{i}", sv["h1t"], dproj, N_DEV, deps=deps, a_t=True)
        else:
            dwin = _mm_dw(f"wbin{i}", sv["h1t"], dproj, tko=1024, tn=512, deps=deps, a_t=True)
            big["bin"][j] = dwin.reshape(D, N_DEV, b_in // N_DEV).transpose(1, 0, 2)
        big["kv"][i] = _mm_dw(f"wkv{i}", sv["memn"], dkv, tko=1024, deps=deps).reshape(N_DEV, D // N_DEV, 2 * GW)
        submit(f"m{i}", mix_members, [big[t][l] for t, l in mix_members])
    grad_x = dx[None]

    last_chips, toks = [], []
    for st in pipe["pair"]:
        new, tok = pair_end_chip_begin(st, [dx])
        last_chips.append(new)
        toks.append(tok)
    g_early = small_end("early", early_state, [a.shape for a in early_arrays], [dx])
    for st in pipe["chip"]:
        chip_end_update(st, [g_early[1]], toks)
    g_late = small_end("late", late_state, [a.shape for a in late_small()], list(adamw_tokens))

    loss = g_early[0][0]
    layer0 = dict(zip(("mix", "ffn", "mem", "conv_b", "conv_w"), g_late))
    rest = dict(zip(("final", "mix", "ffn", "mem", "w_s", "s_bias", "gv", "conv_b", "conv_w"), g_early[1:]))
    g_cw_full = jnp.concatenate([layer0["conv_w"], rest["conv_w"]])
    g_gv = lax.dynamic_slice_in_dim(rest["gv"], dev * b_v_norm_g.shape[1], b_v_norm_g.shape[1], axis=1)
    g_cw = lax.dynamic_slice_in_dim(g_cw_full, dev * ffn_conv_w.shape[2], ffn_conv_w.shape[2], axis=2)
    g_all = [jnp.concatenate([layer0["mix"], rest["mix"]]), jnp.concatenate([layer0["ffn"], rest["ffn"]]),
             jnp.concatenate([layer0["mem"], rest["mem"]]), rest["w_s"], rest["s_bias"],
             jnp.concatenate([layer0["conv_b"], rest["conv_b"]]), rest["final"], g_gv, g_cw]
    names = ["mix_norm_g", "ffn_norm_g", "mem_norm_g", "b_w_s", "b_s_bias", "ffn_conv_b", "final_norm_g",
             "b_v_norm_g", "ffn_conv_w"]
    ws = [mix_norm_g, ffn_norm_g, mem_norm_g, b_w_s, b_s_bias, ffn_conv_b, final_norm_g, b_v_norm_g, ffn_conv_w]
    ms = [m_mix_norm_g, m_ffn_norm_g, m_mem_norm_g, m_b_w_s, m_b_s_bias, m_ffn_conv_b, m_final_norm_g,
          m_b_v_norm_g, m_ffn_conv_w]
    vs = [v_mix_norm_g, v_ffn_norm_g, v_mem_norm_g, v_b_w_s, v_b_s_bias, v_ffn_conv_b, v_final_norm_g,
          v_b_v_norm_g, v_ffn_conv_w]
    shapes = [w.shape for w in ws]
    d_p, m_p, v_p = _adamw_flat("adamw_small", _pack(ws), _pack(g_all), _pack(ms), _pack(vs))
    res = {}
    for n, g, d, nm, nv in zip(names, g_all, _unpack(d_p, shapes), _unpack(m_p, shapes), _unpack(v_p, shapes)):
        res[n] = [g, d, nm, nv]
    for st in last_chips:
        chip_end_update(st, [d_p] + list(adamw_tokens))
    for tag, name in (("kv", "w_mem_kv"), ("ain", "a_w_in"), ("aout", "a_w_out"), ("bin", "b_w_in"),
                      ("bout", "b_w_out"), ("up", "ffn_w_up"), ("down", "ffn_w_down")):
        res[name] = list(chain[tag])

    order = ["mix_norm_g", "ffn_norm_g", "mem_norm_g", "w_mem_kv", "a_w_in", "a_w_out", "b_w_in", "b_v_norm_g",
             "b_w_s", "b_s_bias", "b_w_out", "ffn_w_up", "ffn_conv_w", "ffn_conv_b", "ffn_w_down", "final_norm_g"]
    return (loss, grad_x, *[res[n][0] for n in order], *[res[n][1] for n in order],
            *[res[n][2] for n in order], *[res[n][3] for n in order])
```

```python
import functools

import numpy as np
import jax
import jax.numpy as jnp
from jax import lax
from jax.experimental import pallas as pl
from jax.experimental.pallas import tpu as pltpu

F32 = jnp.float32
BF16 = jnp.bfloat16
MESH = pl.DeviceIdType.MESH
N_DEV = 8

EPS = 1e-6
NEG = -1e30
HEAD = 128
HPG = 4
GW = HPG * HEAD
A_PATTERNS = ((128, 1), (512, 4), (2048, 16))
A_HEADS = HPG * len(A_PATTERNS)
QBLK = 128
B_GROUPS = 12
B_W = B_GROUPS * HEAD
SLOPES = (2.0 ** (-8.0 * (np.arange(A_HEADS) + 1) / A_HEADS)).astype(np.float32)
SCALE = HEAD ** -0.5

ADAM_LR = 0.001
ADAM_B1 = 0.9
ADAM_B2 = 0.999
ADAM_EPS = 1e-08
ADAM_WD = 0.01
ADAM_STEP = 10

V7X_VMEM_LIMIT = 50 * 1024 * 1024

NN = (((1,), (0,)), ((), ()))
NT = (((1,), (1,)), ((), ()))
TN = (((0,), (0,)), ((), ()))


def _cp(*sem):
    return pltpu.CompilerParams(dimension_semantics=sem, vmem_limit_bytes=V7X_VMEM_LIMIT)


def _dot(a, b, dims=NN):
    return lax.dot_general(a.astype(BF16), b.astype(BF16), dims, preferred_element_type=F32)


def _tile(n, pref):
    t = min(n, pref)
    assert n % t == 0, (n, pref)
    return t


def _row_tile(rows, cols, mib=1):
    best = None
    for t in range(16, rows + 1, 16):
        if rows % t == 0 and t * cols * 4 <= (mib << 20):
            best = t
    if best is None:
        best = rows
    return best


_DEP = pl.BlockSpec((8, 128), lambda *_: (0, 0))


def _matmul(name, dims, grid, a, a_spec, b, b_spec, out_shape, o_spec, tile, res=None, res_spec=None, deps=(),
            nsub=1):
    nk = grid[2]
    has_res = res is not None

    def body(*refs):
        a_ref, b_ref = refs[0], refs[1]
        r_ref = refs[2] if has_res else None
        o_ref, acc_ref = refs[-2], refs[-1]
        if nsub == 1:
            part = _dot(a_ref[...], b_ref[...], dims)
        else:
            w = a_ref.shape[-1] // nsub
            part = _dot(a_ref[:, :w], b_ref[0], dims)
            for q in range(1, nsub):
                part = part + _dot(a_ref[:, q * w:(q + 1) * w], b_ref[q], dims)

        def finish(val):
            if has_res:
                val = val + r_ref[...]
            o_ref[...] = val.astype(o_ref.dtype)

        if nk == 1:
            finish(part)
        else:
            k = pl.program_id(2)

            @pl.when(k == 0)
            def _():
                acc_ref[...] = part

            @pl.when(k > 0)
            def _():
                acc_ref[...] += part

            @pl.when(k == nk - 1)
            def _():
                finish(acc_ref[...])

    ins = [a, b] + ([res] if has_res else []) + list(deps)
    specs = [a_spec, b_spec] + ([res_spec] if has_res else []) + [_DEP] * len(deps)
    return pl.pallas_call(
        body, name=name, grid=grid, in_specs=specs, out_specs=o_spec, out_shape=out_shape,
        scratch_shapes=[pltpu.VMEM(tile if nk > 1 else (8, 128), F32)],
        compiler_params=_cp("parallel", "parallel", "arbitrary"))(*ins)


def _mm_full(name, a, w, res=None, tm=1024, tn=512, tk=2048, deps=()):
    M, K = a.shape
    N = w.shape[1]
    tm, tn, tk = _tile(M, tm), _tile(N, tn), _tile(K, tk)
    return _matmul(
        name, NN, (N // tn, M // tm, K // tk),
        a, pl.BlockSpec((tm, tk), lambda j, i, k: (i, k)),
        w, pl.BlockSpec((tk, tn), lambda j, i, k: (k, j)),
        jax.ShapeDtypeStruct((M, N), F32), pl.BlockSpec((tm, tn), lambda j, i, k: (i, j)), (tm, tn),
        res, pl.BlockSpec((tm, tn), lambda j, i, k: (i, j)), deps=deps)


def _mm_gcols(name, a, wg, res=None, split_out=False, tm=1024, deps=(), out_dtype=F32):
    M, K = a.shape
    G, _, Nl = wg.shape
    tm = _tile(M, tm)
    hg = G // 2
    if split_out:
        shape = jax.ShapeDtypeStruct((2, M, hg * Nl), out_dtype)
        o_spec = pl.BlockSpec((None, tm, Nl), lambda g, i, k: (g // hg, i, g % hg))
    else:
        shape = jax.ShapeDtypeStruct((M, G * Nl), out_dtype)
        o_spec = pl.BlockSpec((tm, Nl), lambda g, i, k: (i, g))
    return _matmul(
        name, NN, (G, M // tm, 1),
        a, pl.BlockSpec((tm, K), lambda g, i, k: (i, 0)),
        wg, pl.BlockSpec((None, K, Nl), lambda g, i, k: (g, 0, 0)),
        shape, o_spec, (tm, Nl),
        res, pl.BlockSpec((tm, Nl), lambda g, i, k: (i, g)), deps=deps)


def _mm_dx_full(name, dy, w, tm=512, tko=512, tc=2048, deps=()):
    M, N = dy.shape
    K = w.shape[0]
    tm, tko, tc = _tile(M, tm), _tile(K, tko), _tile(N, tc)
    return _matmul(
        name, NT, (K // tko, M // tm, N // tc),
        dy, pl.BlockSpec((tm, tc), lambda j, i, k: (i, k)),
        w, pl.BlockSpec((tko, tc), lambda j, i, k: (j, k)),
        jax.ShapeDtypeStruct((M, K), F32), pl.BlockSpec((tm, tko), lambda j, i, k: (i, j)), (tm, tko), deps=deps)


def _mm_dx_gcols(name, dy, wg, split_in=False, tm=1024, tko=1024, deps=(), nsub=1):
    G, K, Nl = wg.shape
    M = dy.shape[-2]
    tm, tko = _tile(M, tm), _tile(K, tko)
    hs = G // 2 // nsub if split_in else None
    if split_in:
        dy_spec = pl.BlockSpec((None, tm, nsub * Nl), lambda j, i, g: (g // hs, i, g % hs))
    else:
        dy_spec = pl.BlockSpec((tm, nsub * Nl), lambda j, i, g: (i, g))
    w_block = (None, tko, Nl) if nsub == 1 else (nsub, tko, Nl)
    return _matmul(
        name, NT, (K // tko, M // tm, G // nsub),
        dy, dy_spec,
        wg, pl.BlockSpec(w_block, lambda j, i, g: (g, j, 0)),
        jax.ShapeDtypeStruct((M, K), F32), pl.BlockSpec((tm, tko), lambda j, i, g: (i, j)), (tm, tko), deps=deps,
        nsub=nsub)


def _lhs_of_dw(a_t, ts, tko, index):
    if a_t:
        return NN, pl.BlockSpec((tko, ts), lambda *ids: index(*ids))
    return TN, pl.BlockSpec((ts, tko), lambda *ids: index(*ids)[::-1])


def _mm_dw(name, a, dy, tko=512, tn=1024, ts=2048, deps=(), a_t=False):
    K1, S = a.shape if a_t else a.shape[::-1]
    N = dy.shape[1]
    tko, tn, ts = _tile(K1, tko), _tile(N, tn), _tile(S, ts)
    dims, a_spec = _lhs_of_dw(a_t, ts, tko, lambda i, j, k: (j, k))
    return _matmul(
        name, dims, (N // tn, K1 // tko, S // ts),
        a, a_spec,
        dy, pl.BlockSpec((ts, tn), lambda i, j, k: (k, i)),
        jax.ShapeDtypeStruct((K1, N), BF16), pl.BlockSpec((tko, tn), lambda i, j, k: (j, i)), (tko, tn), deps=deps)


def _mm_dw_gcols(name, a, dy, G, split_in=False, tko=1024, ts=2048, deps=(), a_t=False):
    K1, S = a.shape if a_t else a.shape[::-1]
    Nl = (dy.shape[-1] * (2 if split_in else 1)) // G
    tko, ts = _tile(K1, tko), _tile(S, ts)
    hg = G // 2
    dims, a_spec = _lhs_of_dw(a_t, ts, tko, lambda g, j, k: (j, k))
    if split_in:
        dy_spec = pl.BlockSpec((None, ts, Nl), lambda g, j, k: (g // hg, k, g % hg))
    else:
        dy_spec = pl.BlockSpec((ts, Nl), lambda g, j, k: (k, g))
    return _matmul(
        name, dims, (G, K1 // tko, S // ts),
        a, a_spec,
        dy, dy_spec,
        jax.ShapeDtypeStruct((G, K1, Nl), BF16), pl.BlockSpec((None, tko, Nl), lambda g, j, k: (g, j, 0)),
        (tko, Nl), deps=deps)


def _rms_fwd(name, x, g, tr=256, deps=(), with_t=False):
    S, D = x.shape
    tr = _tile(S, tr)

    def body(x_ref, g_ref, *rest):
        xf = x_ref[...]
        r = lax.rsqrt(jnp.mean(xf * xf, axis=-1, keepdims=True) + EPS)
        y = xf * r * g_ref[...]
        if with_t:
            rest[-2][...] = y.astype(BF16)
            rest[-1][...] = y.T.astype(BF16)
        else:
            rest[-1][...] = y.astype(BF16)

    row = pl.BlockSpec((tr, D), lambda i: (i, 0))
    out_specs, out_shape = row, jax.ShapeDtypeStruct((S, D), BF16)
    if with_t:
        out_specs = [row, pl.BlockSpec((D, tr), lambda i: (0, i))]
        out_shape = [out_shape, jax.ShapeDtypeStruct((D, S), BF16)]
    return pl.pallas_call(
        body, name=name, grid=(S // tr,),
        in_specs=[row, pl.BlockSpec((1, D), lambda i: (0, 0))] + [_DEP] * len(deps),
        out_specs=out_specs, out_shape=out_shape, compiler_params=_cp("parallel"))(x, g.reshape(1, D), *deps)


def _rms_bwd(name, dh, x, g, dres=None, tr=256):
    S, D = x.shape
    tr = _tile(S, tr)
    has_res = dres is not None

    def body(*refs):
        dh_ref, x_ref, g_ref = refs[:3]
        dres_ref = refs[3] if has_res else None
        dx_ref, dg_ref, dx16_ref = refs[-3:]
        xf = x_ref[...]
        r = lax.rsqrt(jnp.mean(xf * xf, axis=-1, keepdims=True) + EPS)
        xh = xf * r
        dhv = dh_ref[...]
        dxh = dhv * g_ref[...]
        dx = r * (dxh - xh * jnp.mean(dxh * xh, axis=-1, keepdims=True))
        if has_res:
            dx = dx + dres_ref[...]
        dx_ref[...] = dx
        dx16_ref[...] = dx.astype(BF16)
        part = jnp.sum(dhv * xh, axis=0, keepdims=True)
        i = pl.program_id(0)

        @pl.when(i == 0)
        def _():
            dg_ref[...] = part

        @pl.when(i > 0)
        def _():
            dg_ref[...] += part

    row = pl.BlockSpec((tr, D), lambda i: (i, 0))
    vec = pl.BlockSpec((1, D), lambda i: (0, 0))
    ins = [dh, x, g.reshape(1, D)] + ([dres] if has_res else [])
    return pl.pallas_call(
        body, name=name, grid=(S // tr,),
        in_specs=[row, row, vec] + ([row] if has_res else []),
        out_specs=[row, vec, row],
        out_shape=[jax.ShapeDtypeStruct((S, D), F32), jax.ShapeDtypeStruct((1, D), F32),
                   jax.ShapeDtypeStruct((S, D), BF16)],
        compiler_params=_cp("arbitrary"))(*ins)


def _final(name, x, tgt, g, tr=256):
    S, D = x.shape
    tr = _tile(S, tr)

    def body(x_ref, t_ref, g_ref, dx_ref, dg_ref, loss_ref, dx16_ref):
        xf = x_ref[...]
        gv = g_ref[...]
        r = lax.rsqrt(jnp.mean(xf * xf, axis=-1, keepdims=True) + EPS)
        xh = xf * r
        err = xh * gv - t_ref[...]
        sq = jnp.sum(jnp.sum(err * err, axis=1, keepdims=True), axis=0, keepdims=True)
        dy = err * (1.0 / D)
        dxh = dy * gv
        dx = r * (dxh - xh * jnp.mean(dxh * xh, axis=-1, keepdims=True))
        dx_ref[...] = dx
        dx16_ref[...] = dx.astype(BF16)
        part = jnp.sum(dy * xh, axis=0, keepdims=True)
        lpart = jnp.broadcast_to(sq, (8, 128))
        i = pl.program_id(0)

        @pl.when(i == 0)
        def _():
            dg_ref[...] = part
            loss_ref[...] = lpart

        @pl.when(i > 0)
        def _():
            dg_ref[...] += part
            loss_ref[...] += lpart

    row = pl.BlockSpec((tr, D), lambda i: (i, 0))
    vec = pl.BlockSpec((1, D), lambda i: (0, 0))
    return pl.pallas_call(
        body, name=name, grid=(S // tr,), in_specs=[row, row, vec],
        out_specs=[row, vec, pl.BlockSpec((8, 128), lambda i: (0, 0)), row],
        out_shape=[jax.ShapeDtypeStruct((S, D), F32), jax.ShapeDtypeStruct((1, D), F32),
                   jax.ShapeDtypeStruct((8, 128), F32), jax.ShapeDtypeStruct((S, D), BF16)],
        compiler_params=_cp("arbitrary"))(x, tgt, g.reshape(1, D))


def _band_specs(nb, col_of):
    prev = pl.BlockSpec((QBLK, GW), lambda r, b: (jnp.maximum(b - 1, 0), col_of(r)))
    cur = pl.BlockSpec((QBLK, GW), lambda r, b: (b, col_of(r)))
    nxt = pl.BlockSpec((QBLK, GW), lambda r, b: (jnp.minimum(b + 1, nb - 1), col_of(r)))
    return [prev, cur, nxt]


HALF = QBLK // 2
WIN = 2 * QBLK


def _cat3(refs, sl):
    prev, cur, nxt = refs
    return jnp.concatenate([prev[HALF:, sl], cur[:, sl], nxt[:HALF, sl]], axis=0)


def _group_view(proj, g):
    _, dil = A_PATTERNS[g]
    S, C = proj.shape
    ng = len(A_PATTERNS)
    if dil == 1:
        return proj, lambda which, r: which * ng + g
    cols = [proj[:, (which * ng + g) * GW:(which * ng + g + 1) * GW] for which in range(3)]
    return jnp.concatenate(cols, axis=1).reshape(S // dil, dil * 3 * GW), lambda which, r: r * 3 + which


def _attn_fwd(name, proj, g):
    window, dil = A_PATTERNS[g]
    n_side = (window // 2) // dil
    S, C = proj.shape
    L = S // dil
    nb = L // QBLK
    pv, col = _group_view(proj, g)

    assert n_side == HALF

    def body(q_ref, kp, kc, kn, vp, vc, vn, o_ref, lse_ref):
        b = pl.program_id(1)
        jq = b * QBLK + lax.broadcasted_iota(jnp.int32, (QBLK, WIN), 0)
        jk = b * QBLK - HALF + lax.broadcasted_iota(jnp.int32, (QBLK, WIN), 1)
        rel = jnp.abs(jk - jq)
        mask = (rel <= n_side) & (jk >= 0) & (jk < L)
        dist = rel.astype(F32) * float(dil)
        for hh in range(HPG):
            sl = slice(hh * HEAD, (hh + 1) * HEAD)
            k = _cat3((kp, kc, kn), sl)
            v = _cat3((vp, vc, vn), sl)
            s = _dot(q_ref[:, sl], k, NT) * SCALE - float(SLOPES[g * HPG + hh]) * dist
            s = jnp.where(mask, s, NEG)
            m = jnp.max(s, axis=1, keepdims=True)
            p = jnp.exp(s - m)
            l = jnp.sum(p, axis=1, keepdims=True)
            o_ref[:, sl] = _dot(p, v) / l
            lse_ref[:, sl] = jnp.broadcast_to(m + jnp.log(l), (QBLK, HEAD))

    q_spec = pl.BlockSpec((QBLK, GW), lambda r, b: (b, col(0, r)))
    k_specs = _band_specs(nb, lambda r: col(1, r))
    v_specs = _band_specs(nb, lambda r: col(2, r))
    o_spec = pl.BlockSpec((QBLK, GW), lambda r, b: (b, r))
    shape = jax.ShapeDtypeStruct((L, dil * GW), F32)
    o, lse = pl.pallas_call(
        body, name=name, grid=(dil, nb), in_specs=[q_spec] + k_specs + v_specs,
        out_specs=[o_spec, o_spec], out_shape=[shape, shape],
        compiler_params=_cp("parallel", "parallel"))(pv, pv, pv, pv, pv, pv, pv)
    return o.reshape(S, GW), lse.reshape(S, GW)


def _attn_merge(name, outs, lses, tr=256):
    S = outs[0].shape[0]
    tr = _tile(S, tr)
    ng = len(outs)

    def body(*refs):
        o_refs, l_refs = refs[:ng], refs[ng:2 * ng]
        tok_ref, lse_ref = refs[-2], refs[-1]
        ls = [r[...] for r in l_refs]
        m = functools.reduce(jnp.maximum, ls)
        es = [jnp.exp(l - m) for l in ls]
        tot = functools.reduce(lambda a, b: a + b, es)
        acc = None
        for e, o_ref in zip(es, o_refs):
            term = (e / tot) * o_ref[...]
            acc = term if acc is None else acc + term
        tok_ref[...] = acc
        lse_ref[...] = m + jnp.log(tot)

    row = pl.BlockSpec((tr, GW), lambda i: (i, 0))
    shape = jax.ShapeDtypeStruct((S, GW), F32)
    return pl.pallas_call(
        body, name=name, grid=(S // tr,), in_specs=[row] * (2 * ng), out_specs=[row, row],
        out_shape=[shape, shape], compiler_params=_cp("parallel"))(*outs, *lses)


def _attn_bwd(name, proj, g, dtok_src, dtok_blk, tok, lse):
    window, dil = A_PATTERNS[g]
    n_side = (window // 2) // dil
    S, C = proj.shape
    L = S // dil
    nb = L // QBLK
    pv, col = _group_view(proj, g)
    assert n_side == HALF
    if dil == 1:
        dcb, dv_ = dtok_src.shape[1] // GW, dtok_src
    else:
        dcb, dv_ = 1, dtok_src[:, dtok_blk * GW:(dtok_blk + 1) * GW].reshape(L, dil * GW)
        dtok_blk = 0
    ov = tok.reshape(L, dil * GW)
    lv = lse.reshape(L, dil * GW)

    def body(qp, qc, qn, kp, kc, kn, vp, vc, vn, dop, doc, don, op, oc, on, lp, lc, ln,
             dq_ref, dk_ref, dv_ref):
        b = pl.program_id(1)
        jq = b * QBLK + lax.broadcasted_iota(jnp.int32, (QBLK, WIN), 0)
        jk = b * QBLK - HALF + lax.broadcasted_iota(jnp.int32, (QBLK, WIN), 1)
        rel = jnp.abs(jk - jq)
        mask = (rel <= n_side) & (jk >= 0) & (jk < L)
        dist = rel.astype(F32) * float(dil)
        jq3 = b * QBLK - HALF + lax.broadcasted_iota(jnp.int32, (WIN, QBLK), 0)
        jk1 = b * QBLK + lax.broadcasted_iota(jnp.int32, (WIN, QBLK), 1)
        rel3 = jnp.abs(jk1 - jq3)
        mask3 = (rel3 <= n_side) & (jq3 >= 0) & (jq3 < L)
        dist3 = rel3.astype(F32) * float(dil)
        for hh in range(HPG):
            sl = slice(hh * HEAD, (hh + 1) * HEAD)
            one = slice(hh * HEAD, hh * HEAD + 1)
            slope = float(SLOPES[g * HPG + hh])
            q = qc[:, sl]
            do = doc[:, sl]
            k3 = _cat3((kp, kc, kn), sl)
            v3 = _cat3((vp, vc, vn), sl)
            delta = jnp.sum(do * oc[:, sl], axis=1, keepdims=True)
            s = _dot(q, k3, NT) * SCALE - slope * dist
            p = jnp.where(mask, jnp.exp(s - lc[:, one]), 0.0)
            ds = p * (_dot(do, v3, NT) - delta)
            dq_ref[:, sl] = (_dot(ds, k3) * SCALE).astype(dq_ref.dtype)

            q3 = _cat3((qp, qc, qn), sl)
            do3 = _cat3((dop, doc, don), sl)
            o3 = _cat3((op, oc, on), sl)
            lse3 = _cat3((lp, lc, ln), sl)[:, :1]
            delta3 = jnp.sum(do3 * o3, axis=1, keepdims=True)
            k = kc[:, sl]
            v = vc[:, sl]
            s3 = _dot(q3, k, NT) * SCALE - slope * dist3
            p3 = jnp.where(mask3, jnp.exp(s3 - lse3), 0.0)
            ds3 = p3 * (_dot(do3, v, NT) - delta3)
            dv_ref[:, sl] = _dot(p3, do3, TN).astype(dv_ref.dtype)
            dk_ref[:, sl] = (_dot(ds3, q3, TN) * SCALE).astype(dk_ref.dtype)

    specs = (_band_specs(nb, lambda r: col(0, r)) + _band_specs(nb, lambda r: col(1, r))
             + _band_specs(nb, lambda r: col(2, r))
             + _band_specs(nb, lambda r: r * dcb + dtok_blk)
             + _band_specs(nb, lambda r: r) + _band_specs(nb, lambda r: r))
    o_spec = pl.BlockSpec((QBLK, GW), lambda r, b: (b, r))
    shape = jax.ShapeDtypeStruct((L, dil * GW), BF16)
    outs = pl.pallas_call(
        body, name=name, grid=(dil, nb), in_specs=specs, out_specs=[o_spec] * 3, out_shape=[shape] * 3,
        compiler_params=_cp("parallel", "parallel"))(*([pv] * 9 + [dv_] * 3 + [ov] * 3 + [lv] * 3))
    return [o.reshape(S, GW) for o in outs]


def _mem_fwd(name, proj, q_blk, kv, tq=256):
    S = proj.shape[0]
    M = kv.shape[0]
    tq = _tile(S, tq)

    def body(q_ref, kv_ref, o_ref):
        for hh in range(HPG):
            sl = slice(hh * HEAD, (hh + 1) * HEAD)
            k = kv_ref[:, sl]
            v = kv_ref[:, GW + hh * HEAD:GW + (hh + 1) * HEAD]
            s = _dot(q_ref[:, sl], k, NT) * SCALE
            m = jnp.max(s, axis=1, keepdims=True)
            p = jnp.exp(s - m)
            p = p / jnp.sum(p, axis=1, keepdims=True)
            o_ref[:, sl] = _dot(p, v)

    return pl.pallas_call(
        body, name=name, grid=(S // tq,),
        in_specs=[pl.BlockSpec((tq, GW), lambda i: (i, q_blk)), pl.BlockSpec((M, 2 * GW), lambda i: (0, 0))],
        out_specs=pl.BlockSpec((tq, GW), lambda i: (i, 0)),
        out_shape=jax.ShapeDtypeStruct((S, GW), F32), compiler_params=_cp("parallel"))(proj, kv)


def _mem_bwd(name, proj, q_blk, kv, dcat, do_blk, tq=256, deps=()):
    S = proj.shape[0]
    M = kv.shape[0]
    tq = _tile(S, tq)

    def body(q_ref, kv_ref, do_ref, *rest):
        dq_ref, dkv_ref = rest[-2:]
        i = pl.program_id(0)
        for hh in range(HPG):
            sl = slice(hh * HEAD, (hh + 1) * HEAD)
            vsl = slice(GW + hh * HEAD, GW + (hh + 1) * HEAD)
            q = q_ref[:, sl]
            do = do_ref[:, sl]
            k = kv_ref[:, sl]
            v = kv_ref[:, vsl]
            s = _dot(q, k, NT) * SCALE
            m = jnp.max(s, axis=1, keepdims=True)
            p = jnp.exp(s - m)
            p = p / jnp.sum(p, axis=1, keepdims=True)
            dp = _dot(do, v, NT)
            ds = p * (dp - jnp.sum(dp * p, axis=1, keepdims=True))
            dq_ref[:, sl] = (_dot(ds, k) * SCALE).astype(dq_ref.dtype)
            dk = _dot(ds, q, TN) * SCALE
            dvv = _dot(p, do, TN)

            @pl.when(i == 0)
            def _():
                dkv_ref[:, sl] = dk
                dkv_ref[:, vsl] = dvv

            @pl.when(i > 0)
            def _():
                dkv_ref[:, sl] += dk
                dkv_ref[:, vsl] += dvv

    return pl.pallas_call(
        body, name=name, grid=(S // tq,),
        in_specs=[pl.BlockSpec((tq, GW), lambda i: (i, q_blk)), pl.BlockSpec((M, 2 * GW), lambda i: (0, 0)),
                  pl.BlockSpec((tq, GW), lambda i: (i, do_blk))] + [_DEP] * len(deps),
        out_specs=[pl.BlockSpec((tq, GW), lambda i: (i, 0)), pl.BlockSpec((M, 2 * GW), lambda i: (0, 0))],
        out_shape=[jax.ShapeDtypeStruct((S, GW), BF16), jax.ShapeDtypeStruct((M, 2 * GW), F32)],
        compiler_params=_cp("arbitrary"))(proj, kv, dcat, *deps)


_RSQRT2 = float(1.0 / np.sqrt(2.0))
_RSQRT2PI = float(1.0 / np.sqrt(2.0 * np.pi))


def _gelu(x):
    return 0.5 * x * (1.0 + lax.erf(x * _RSQRT2))


def _gelu_and_grad(x):
    cdf = 0.5 * (1.0 + lax.erf(x * _RSQRT2))
    return x * cdf, cdf + x * jnp.exp(-0.5 * x * x) * _RSQRT2PI


def _sgu_fwd(name, proj, gv, w_s, bias_t):
    S = proj.shape[0]
    nch = S // HEAD

    def body(u_ref, v_ref, gv_ref, ws_ref, b_ref, o_ref):
        v = _gelu(v_ref[...])
        r = lax.rsqrt(jnp.mean(v * v, axis=-1, keepdims=True) + EPS)
        vn = v * r * gv_ref[...]
        for gg in range(B_GROUPS):
            sl = slice(gg * HEAD, (gg + 1) * HEAD)
            mixed = _dot(ws_ref[gg], vn[:, sl]) + b_ref[:, gg:gg + 1]
            o_ref[:, sl] = _gelu(u_ref[:, sl]) * mixed

    return pl.pallas_call(
        body, name=name, grid=(nch,),
        in_specs=[pl.BlockSpec((HEAD, B_W), lambda c: (c, 0)), pl.BlockSpec((HEAD, B_W), lambda c: (c, 1)),
                  pl.BlockSpec((1, B_W), lambda c: (0, 0)),
                  pl.BlockSpec((B_GROUPS, HEAD, HEAD), lambda c: (0, 0, 0)),
                  pl.BlockSpec((HEAD, B_GROUPS), lambda c: (0, 0))],
        out_specs=pl.BlockSpec((HEAD, B_W), lambda c: (c, 0)),
        out_shape=jax.ShapeDtypeStruct((S, B_W), F32),
        compiler_params=_cp("parallel"))(proj, proj, gv.reshape(1, B_W), w_s, bias_t)


def _sgu_bwd(name, proj, gv, w_s, bias_t, dcat):
    S = proj.shape[0]
    nch = S // HEAD

    def body(u_ref, v_ref, gv_ref, ws_ref, b_ref, dt_ref, du_ref, dvp_ref, dgv_ref, dws_ref, db_ref, dvn_ref):
        c = pl.program_id(0)
        vpre = v_ref[...]
        v, v_slope = _gelu_and_grad(vpre)
        r = lax.rsqrt(jnp.mean(v * v, axis=-1, keepdims=True) + EPS)
        vh = v * r
        gvv = gv_ref[...]
        vn = vh * gvv
        for gg in range(B_GROUPS):
            sl = slice(gg * HEAD, (gg + 1) * HEAD)
            upre = u_ref[:, sl]
            dt = dt_ref[:, sl]
            vng = vn[:, sl]
            mixed = _dot(ws_ref[gg], vng) + b_ref[:, gg:gg + 1]
            u, u_slope = _gelu_and_grad(upre)
            du_ref[:, sl] = (dt * mixed * u_slope).astype(du_ref.dtype)
            dmix = dt * u
            dvn_ref[:, sl] = _dot(ws_ref[gg], dmix, TN)
            dws = _dot(dmix, vng, NT)
            dbs = jnp.sum(dmix, axis=1, keepdims=True)

            @pl.when(c == 0)
            def _():
                dws_ref[gg] = dws
                db_ref[:, gg:gg + 1] = dbs

            @pl.when(c > 0)
            def _():
                dws_ref[gg] += dws
                db_ref[:, gg:gg + 1] += dbs

        dvn = dvn_ref[...]
        dgp = jnp.sum(dvn * vh, axis=0, keepdims=True)
        dvh = dvn * gvv
        dv = r * (dvh - vh * jnp.mean(dvh * vh, axis=-1, keepdims=True))
        dvp_ref[...] = (dv * v_slope).astype(dvp_ref.dtype)

        @pl.when(c == 0)
        def _():
            dgv_ref[...] = dgp

        @pl.when(c > 0)
        def _():
            dgv_ref[...] += dgp

    blk = lambda j: pl.BlockSpec((HEAD, B_W), lambda c: (c, j))
    vec = pl.BlockSpec((1, B_W), lambda c: (0, 0))
    ws_spec = pl.BlockSpec((B_GROUPS, HEAD, HEAD), lambda c: (0, 0, 0))
    b_spec = pl.BlockSpec((HEAD, B_GROUPS), lambda c: (0, 0))
    du, dvp, dgv, dws, db = pl.pallas_call(
        body, name=name, grid=(nch,),
        in_specs=[blk(0), blk(1), vec, ws_spec, b_spec, blk(0)],
        out_specs=[blk(0), blk(0), vec, ws_spec, b_spec],
        out_shape=[jax.ShapeDtypeStruct((S, B_W), BF16), jax.ShapeDtypeStruct((S, B_W), BF16),
                   jax.ShapeDtypeStruct((1, B_W), F32), jax.ShapeDtypeStruct((B_GROUPS, HEAD, HEAD), F32),
                   jax.ShapeDtypeStruct((HEAD, B_GROUPS), F32)],
        scratch_shapes=[pltpu.VMEM((HEAD, B_W), F32)],
        compiler_params=_cp("arbitrary"))(proj, proj, gv.reshape(1, B_W), w_s, bias_t, dcat)
    return du, dvp, dgv, dws, db


def _shift_down(a, row):
    return jnp.where(row == 0, 0.0, pltpu.roll(a, 1, 0))


def _shift_up(a, row):
    n = a.shape[0]
    return jnp.where(row == n - 1, 0.0, pltpu.roll(a, n - 1, 0))


def _conv(a, w, b, row):
    return _shift_down(a, row) * w[0:1] + a * w[1:2] + _shift_up(a, row) * w[2:3] + b


def _conv_fwd(name, a3, cw, cb, tc=256):
    _, S, FF = a3.shape
    tc = _tile(FF, tc)

    def body(a_ref, w_ref, b_ref, o_ref):
        row = lax.broadcasted_iota(jnp.int32, (S, tc), 0)
        cg = _conv(a_ref[0], w_ref[0], b_ref[0], row)
        cv = _conv(a_ref[1], w_ref[1], b_ref[1], row)
        o_ref[...] = (_gelu(cg) * cv).astype(o_ref.dtype)

    return pl.pallas_call(
        body, name=name, grid=(FF // tc,),
        in_specs=[pl.BlockSpec((2, S, tc), lambda j: (0, 0, j)), pl.BlockSpec((2, 3, tc), lambda j: (0, 0, j)),
                  pl.BlockSpec((2, 1, tc), lambda j: (0, 0, j))],
        out_specs=pl.BlockSpec((S, tc), lambda j: (0, j)),
        out_shape=jax.ShapeDtypeStruct((S, FF), BF16), compiler_params=_cp("parallel"))(a3, cw, cb)


def _conv_bwd(name, a3, cw, cb, dact, tc=128):
    _, S, FF = a3.shape
    tc = _tile(FF, tc)

    def body(a_ref, w_ref, b_ref, d_ref, da_ref, dw_ref, db_ref):
        row = lax.broadcasted_iota(jnp.int32, (S, tc), 0)
        ag, av = a_ref[0], a_ref[1]
        wg, wv = w_ref[0], w_ref[1]
        cg = _conv(ag, wg, b_ref[0], row)
        cv = _conv(av, wv, b_ref[1], row)
        d = d_ref[...]
        gate, gate_slope = _gelu_and_grad(cg)
        dcs = (d * cv * gate_slope, d * gate)
        for h, (dc, a, w) in enumerate(zip(dcs, (ag, av), (wg, wv))):
            da = _shift_up(dc, row) * w[0:1] + dc * w[1:2] + _shift_down(dc, row) * w[2:3]
            da_ref[h] = da.astype(da_ref.dtype)
            dw_ref[h, 0:1, :] = jnp.sum(dc * _shift_down(a, row), axis=0, keepdims=True)
            dw_ref[h, 1:2, :] = jnp.sum(dc * a, axis=0, keepdims=True)
            dw_ref[h, 2:3, :] = jnp.sum(dc * _shift_up(a, row), axis=0, keepdims=True)
            db_ref[h] = jnp.sum(dc, axis=0, keepdims=True)

    a_spec = pl.BlockSpec((2, S, tc), lambda j: (0, 0, j))
    w_spec = pl.BlockSpec((2, 3, tc), lambda j: (0, 0, j))
    b_spec = pl.BlockSpec((2, 1, tc), lambda j: (0, 0, j))
    return pl.pallas_call(
        body, name=name, grid=(FF // tc,),
        in_specs=[a_spec, w_spec, b_spec, pl.BlockSpec((S, tc), lambda j: (0, j))],
        out_specs=[a_spec, w_spec, b_spec],
        out_shape=[jax.ShapeDtypeStruct((2, S, FF), BF16), jax.ShapeDtypeStruct((2, 3, FF), F32),
                   jax.ShapeDtypeStruct((2, 1, FF), F32)],
        compiler_params=_cp("parallel"))(a3, cw, cb, dact)


_HBM = pl.BlockSpec(memory_space=pltpu.HBM)


def _position():
    return lax.axis_index("x"), lax.axis_index("y"), lax.axis_index("c")


_SEM =pl.BlockSpec(memory_space=pltpu.SEMAPHORE)
_EFFECT = pltpu.SideEffectType.DATAFLOW_SIDE_EFFECTING
_FLIPS = ((1, 0), (0, 1), (1, 1))


def _split_start(name, bufs, ncopy, plan, after=()):
    n = len(bufs)
    after = list(after)

    def body(*refs):
        ins = refs[:n]
        send_sems, recv_sems, token = refs[n + len(after)], refs[n + len(after) + 1], refs[-1]
        for i, (src, dst, to) in enumerate(plan(ins)):
            pltpu.make_async_remote_copy(src_ref=src, dst_ref=dst, send_sem=send_sems.at[i], recv_sem=recv_sems.at[i],
                                         device_id=to, device_id_type=MESH).start()
        token[...] = jnp.zeros_like(token)

    outs = pl.pallas_call(
        body, name=name,
        out_shape=(pltpu.SemaphoreType.DMA((ncopy,)), pltpu.SemaphoreType.DMA((ncopy,)),
                   *[pltpu.HBM(b.shape, b.dtype) for b in bufs], jax.ShapeDtypeStruct((8, 128), F32)),
        in_specs=[_HBM] * n + [pl.BlockSpec(memory_space=pl.ANY)] * len(after),
        out_specs=(_SEM, _SEM, *([_HBM] * n), pl.BlockSpec(memory_space=pltpu.VMEM)),
        input_output_aliases={i: 2 + i for i in range(n)},
        compiler_params=pltpu.CompilerParams(has_side_effects=_EFFECT),
    )(*[pltpu.with_memory_space_constraint(b, pltpu.HBM) for b in bufs], *after)
    return outs[0], outs[1], list(outs[2:2 + n]), outs[-1]


def _split_wait(name, bufs, send_sems, recv_sems, plan, after):
    n = len(bufs)
    after = list(after)

    def body(*refs):
        ins = refs[:n]
        ssem, rsem = refs[n], refs[n + 1]
        for i, (src, dst, to) in enumerate(plan(ins)):
            cp = pltpu.make_async_remote_copy(src_ref=src, dst_ref=dst, send_sem=ssem.at[i], recv_sem=rsem.at[i],
                                              device_id=to, device_id_type=MESH)
            cp.wait_send()
            cp.wait_recv()

    outs = pl.pallas_call(
        body, name=name, out_shape=tuple(pltpu.HBM(b.shape, b.dtype) for b in bufs),
        in_specs=[_HBM] * n + [_SEM, _SEM] + [pl.BlockSpec(memory_space=pl.ANY)] * len(after),
        out_specs=tuple([_HBM] * n), input_output_aliases={i: i for i in range(n)},
        compiler_params=pltpu.CompilerParams(has_side_effects=_EFFECT),
    )(*bufs, send_sems, recv_sems, *after)
    return list(outs)


def _gather_plan(refs):
    px, py, pc = _position()
    me = 4 * px + 2 * py + pc
    targets = [(px, py, 1 - pc), (1 - px, py, pc), (px, 1 - py, pc), (1 - px, 1 - py, pc)]
    return [(r.at[me], r.at[me], to) for r in refs for to in targets]


def _forward_plan(refs):
    px, py, pc = _position()
    out = []
    for r in refs:
        for fx, fy in _FLIPS:
            slot = 4 * (1 - px if fx else px) + 2 * (1 - py if fy else py) + pc
            out.append((r.at[slot], r.at[slot], (px, py, 1 - pc)))
    return out


def _pair_plan(n):
    def plan(refs):
        px, py, pc = _position()
        return [(refs[w].at[2 * k + (1 - pc)], refs[n + w].at[k], (px, py, 1 - pc)) for w in range(n) for k in range(4)]
    return plan


def _chip_plan(n):
    def plan(refs):
        px, py, pc = _position()
        out = []
        for w in range(n):
            for j, (fx, fy) in enumerate(_FLIPS):
                qx = 1 - px if fx else px
                qy = 1 - py if fy else py
                out.append((refs[w].at[2 * qx + qy], refs[n + w].at[j], (qx, qy, pc)))
        return out
    return plan


def _broadcast_plan(refs):
    px, py, pc = _position()
    me = 4 * px + 2 * py + pc
    flips = [(fx, fy, fc) for fx in (0, 1) for fy in (0, 1) for fc in (0, 1)][1:]
    targets = [(1 - px if fx else px, 1 - py if fy else py, 1 - pc if fc else pc) for fx, fy, fc in flips]
    return [(r.at[me], r.at[me], to) for r in refs for to in targets]


def _cast_place(name, dev, w, layer, dtype=BF16):
    nl, R, C = w.shape
    tr = _row_tile(R, C, 4)

    def body(dev_ref, w_ref, o_ref):
        o_ref[...] = w_ref[...].astype(o_ref.dtype)

    return pl.pallas_call(
        body, name=name,
        grid_spec=pltpu.PrefetchScalarGridSpec(
            num_scalar_prefetch=1, grid=(R // tr,),
            in_specs=[pl.BlockSpec((None, tr, C), lambda i, d: (layer, i, 0))],
            out_specs=pl.BlockSpec((None, tr, C), lambda i, d: (d[0], i, 0))),
        out_shape=jax.ShapeDtypeStruct((N_DEV, R, C), dtype), compiler_params=_cp("parallel"))(dev, w)


def _pair_sum(name, core, dw, recv):
    _, R, C = dw.shape
    tr = _row_tile(R, C, 4)
    dw4 = dw.reshape(4, 2, R, C)

    def body(core_ref, a_ref, b_ref, o_ref):
        o_ref[...] = (a_ref[...].astype(F32) + b_ref[...].astype(F32)).astype(o_ref.dtype)

    return pl.pallas_call(
        body, name=name,
        grid_spec=pltpu.PrefetchScalarGridSpec(
            num_scalar_prefetch=1, grid=(4, R // tr),
            in_specs=[pl.BlockSpec((None, None, tr, C), lambda k, i, c_ref: (k, c_ref[0], i, 0)),
                      pl.BlockSpec((None, tr, C), lambda k, i, c_ref: (k, i, 0))],
            out_specs=pl.BlockSpec((None, tr, C), lambda k, i, c_ref: (k, i, 0))),
        out_shape=jax.ShapeDtypeStruct((4, R, C), BF16),
        compiler_params=_cp("parallel", "parallel"))(core, dw4, recv)


def _adamw_math(w, g, m, v):
    m = ADAM_B1 * m + (1.0 - ADAM_B1) * g
    v = ADAM_B2 * v + (1.0 - ADAM_B2) * (g * g)
    m_hat = m / (1.0 - ADAM_B1 ** ADAM_STEP)
    v_hat = v / (1.0 - ADAM_B2 ** ADAM_STEP)
    delta = -ADAM_LR * (m_hat / (jnp.sqrt(v_hat) + ADAM_EPS) + ADAM_WD * w)
    return delta, m, v


def _adamw_shard(name, chip, layer, w, m, v, p, recv, prev, deps=()):
    nl, R, C = w.shape
    tr = _row_tile(R, C, 2)
    n_prev = 0 if prev is None else 4

    def body(chip_ref, w_ref, m_ref, v_ref, p_ref, r_ref, *rest):
        g_ref, d_ref, nm_ref, nv_ref, tok_ref = rest[-5:]
        tok_ref[...] = jnp.zeros_like(tok_ref)
        g = p_ref[...].astype(F32)
        for j in range(3):
            g = g + r_ref[j].astype(F32)
        delta, nm, nv = _adamw_math(w_ref[...], g, m_ref[...], v_ref[...])
        g_ref[...] = g
        d_ref[...] = delta
        nm_ref[...] = nm
        nv_ref[...] = nv

    lay = pl.BlockSpec((None, tr, C), lambda i, c_ref: (layer, i, 0))
    in_specs = [lay, lay, lay,
                pl.BlockSpec((None, tr, C), lambda i, c_ref: (c_ref[0], i, 0)),
                pl.BlockSpec((3, tr, C), lambda i, c_ref: (0, i, 0))]
    in_specs += [pl.BlockSpec(memory_space=pl.ANY)] * n_prev + [_DEP] * len(deps)
    shape = jax.ShapeDtypeStruct((nl, R, C), F32)
    ins = [chip, w, m, v, p, recv] + ([] if prev is None else list(prev)) + list(deps)
    return pl.pallas_call(
        body, name=name,
        grid_spec=pltpu.PrefetchScalarGridSpec(
            num_scalar_prefetch=1, grid=(R // tr,), in_specs=in_specs, out_specs=[lay] * 4 + [_DEP]),
        out_shape=[shape] * 4 + [jax.ShapeDtypeStruct((8, 128), F32)],
        input_output_aliases={6 + j: j for j in range(n_prev)},
        compiler_params=_cp("arbitrary"))(*ins)


def _sum_slots(name, parts, tr=512):
    n, R, C = parts.shape
    tr = _tile(R, tr)

    def body(p_ref, o_ref):
        acc = p_ref[0]
        for j in range(1, n):
            acc = acc + p_ref[j]
        o_ref[...] = acc

    return pl.pallas_call(
        body, name=name, grid=(R // tr,),
        in_specs=[pl.BlockSpec((n, tr, C), lambda i: (0, i, 0))],
        out_specs=pl.BlockSpec((tr, C), lambda i: (i, 0)),
        out_shape=jax.ShapeDtypeStruct((R, C), F32), compiler_params=_cp("parallel"))(parts)


def _adamw_flat(name, w, g, m, v, tr=512):
    R, C = w.shape
    tr = _tile(R, tr)

    def body(w_ref, g_ref, m_ref, v_ref, d_ref, nm_ref, nv_ref):
        delta, nm, nv = _adamw_math(w_ref[...], g_ref[...], m_ref[...], v_ref[...])
        d_ref[...] = delta
        nm_ref[...] = nm
        nv_ref[...] = nv

    row = pl.BlockSpec((tr, C), lambda i: (i, 0))
    shape = jax.ShapeDtypeStruct((R, C), F32)
    return pl.pallas_call(
        body, name=name, grid=(R // tr,), in_specs=[row] * 4, out_specs=[row] * 3, out_shape=[shape] * 3,
        compiler_params=_cp("parallel"))(w, g, m, v)


_PACK_ROWS = 512


def _pack(arrs):
    flat = jnp.concatenate([a.reshape(-1) for a in arrs])
    unit = _PACK_ROWS * 128
    pad = (-flat.shape[0]) % unit
    return jnp.pad(flat, (0, pad)).reshape(-1, 128)


def _unpack(packed, shapes):
    flat = packed.reshape(-1)
    outs, off = [], 0
    for s in shapes:
        n = int(np.prod(s))
        outs.append(flat[off:off + n].reshape(s))
        off += n
    return outs


def kernel(x, mem, mix_norm_g, ffn_norm_g, mem_norm_g, w_mem_kv, a_w_in, a_w_out, b_w_in, b_v_norm_g, b_w_s, b_s_bias, b_w_out, ffn_w_up, ffn_conv_w, ffn_conv_b, ffn_w_down, final_norm_g, loss_target, m_mix_norm_g, m_ffn_norm_g, m_mem_norm_g, m_w_mem_kv, m_a_w_in, m_a_w_out, m_b_w_in, m_b_v_norm_g, m_b_w_s, m_b_s_bias, m_b_w_out, m_ffn_w_up, m_ffn_conv_w, m_ffn_conv_b, m_ffn_w_down, m_final_norm_g, v_mix_norm_g, v_ffn_norm_g, v_mem_norm_g, v_w_mem_kv, v_a_w_in, v_a_w_out, v_b_w_in, v_b_v_norm_g, v_b_w_s, v_b_s_bias, v_b_w_out, v_ffn_w_up, v_ffn_conv_w, v_ffn_conv_b, v_ffn_w_down, v_final_norm_g):
    px, py, pc = _position()
    dev = 4 * px + 2 * py + pc
    core = jnp.reshape(pc, (1,)).astype(jnp.int32)
    chip = jnp.reshape(2 * px + py, (1,)).astype(jnp.int32)

    x0 = x[0]
    mem0 = mem[0]
    tgt = loss_target[0]
    S, D = x0.shape
    depth = mix_norm_g.shape[0]
    FF = ffn_w_down.shape[1] * N_DEV
    a_in = a_w_in.shape[2] * N_DEV
    b_in = b_w_in.shape[2] * N_DEV
    a_q_blk = (a_in - GW) // GW
    b_q_blk = (b_in - GW) // GW

    stacks = {"kv": (w_mem_kv, m_w_mem_kv, v_w_mem_kv), "ain": (a_w_in, m_a_w_in, v_a_w_in),
              "aout": (a_w_out, m_a_w_out, v_a_w_out), "bin": (b_w_in, m_b_w_in, v_b_w_in),
              "bout": (b_w_out, m_b_w_out, v_b_w_out), "up": (ffn_w_up, m_ffn_w_up, v_ffn_w_up),
              "down": (ffn_w_down, m_ffn_w_down, v_ffn_w_down)}
    dev1 = jnp.reshape(dev, (1,)).astype(jnp.int32)

    def groups_of(i):
        j = i // 2
        mix = [("kv", i), ("ain", j), ("aout", j)] if i % 2 == 0 else [("kv", i), ("bin", j), ("bout", j)]
        return mix, [("up", i), ("down", i)]

    gather_groups = [(f"{half}{i}", members) for i in range(depth) for half, members in zip("mf", groups_of(i))]
    gather_ahead = 2
    in_flight = {}

    def gather_start(k, after):
        gname, members = gather_groups[k]
        lands = [_cast_place(f"place_{t}{l}", dev1, stacks[t][0], l) for t, l in members]
        ssem, rsem, lands, tok = _split_start(f"ag_start_{gname}", lands, 4 * len(lands), _gather_plan, after)
        in_flight[k] = (lands, ssem, rsem)
        return tok

    small_land = _cast_place("place_small_w", dev1, _pack([ffn_conv_w, b_v_norm_g])[None], 0, F32)
    small_ssem, small_rsem, small_lands, small_tok = _split_start("smallw_start", [small_land], N_DEV - 1,
                                                                  _broadcast_plan)
    start_tokens = [small_tok, gather_start(0, [small_tok])]
    passing = {}

    def gather_arrive(k, after):
        if k >= len(gather_groups):
            return []
        gname, members = gather_groups[k]
        lands, ssem, rsem = in_flight.pop(k)
        lands = _split_wait(f"ag_wait_{gname}", lands, ssem, rsem, _gather_plan, after)
        toks = []
        for q in {0: [1], 1: [2], 2: [3, 4]}.get(k, [k + gather_ahead]):
            if q < len(gather_groups):
                toks.append(gather_start(q, [lands[0]] + toks))
        ssem, rsem, lands, tok = _split_start(f"ag_pass_{gname}", lands, 3 * len(lands), _forward_plan, toks)
        passing[k] = (lands, ssem, rsem)
        return toks + [tok]

    def gather_ready(k, after):
        gname, members = gather_groups[k]
        lands, ssem, rsem = passing.pop(k)
        lands = _split_wait(f"ag_ready_{gname}", lands, ssem, rsem, _forward_plan, after)
        out = {}
        for (t, l), land in zip(members, lands):
            if t == "kv":
                out["kv"] = land.reshape(D, 2 * GW)
            elif t in ("ain", "aout", "up"):
                out[{"ain": "in", "aout": "out", "up": "up"}[t]] = land
            elif t == "bin":
                out["in"] = jnp.transpose(land, (1, 0, 2)).reshape(D, b_in)
            elif t == "bout":
                out["out"] = land.reshape(B_W + GW, D)
            else:
                out["down"] = land.reshape(FF, D)
        return out

    def small_weights(after):
        (small_all,) = _split_wait("smallw_wait", small_lands, small_ssem, small_rsem, _broadcast_plan, after)
        cw_parts, gv_parts = [], []
        for d in range(N_DEV):
            cw_d, gv_d = _unpack(small_all[d], [ffn_conv_w.shape, b_v_norm_g.shape])
            cw_parts.append(cw_d)
            gv_parts.append(gv_d)
        return jnp.concatenate(cw_parts, axis=-1), jnp.concatenate(gv_parts, axis=-1)

    def conv_params(i):
        cw = conv_w_full[i].reshape(3, 2, FF).transpose(1, 0, 2)
        cb = ffn_conv_b[i].reshape(2, 1, FF)
        return cw, cb

    saved = []
    W = []
    xc = x0
    toks = start_tokens + gather_arrive(0, [x0])
    for i in range(depth):
        j = i // 2
        lw = gather_ready(2 * i, [xc])
        sv = {"x0": xc}
        h1, h1t = _rms_fwd(f"mixnorm{i}", xc, mix_norm_g[i], deps=toks, with_t=True)
        memn = _rms_fwd(f"memnorm{i}", mem0, mem_norm_g[i])
        if i % 2 == 0:
            proj = _mm_gcols(f"ain{i}", h1, lw["in"], out_dtype=BF16)
        else:
            proj = _mm_full(f"bin{i}", h1, lw["in"])
        kv = _mm_full(f"kvproj{i}", memn, lw["kv"])
        if i % 2 == 0:
            outs, lses = [], []
            for g in range(len(A_PATTERNS)):
                o, l = _attn_fwd(f"attn{i}_{g}", proj, g)
                outs.append(o)
                lses.append(l)
            tok, lse = _attn_merge(f"merge{i}", outs, lses)
            sv.update(tok=tok, lse=lse)
        else:
            tok = _sgu_fwd(f"sgu{i}", proj, gv_full[j], b_w_s[j], b_s_bias[j].T)
        toks = gather_arrive(2 * i + 1, [tok]) if i > 0 else []
        mo = _mem_fwd(f"memattn{i}", proj, a_q_blk if i % 2 == 0 else b_q_blk, kv)
        cat = jnp.concatenate([tok.astype(BF16), mo.astype(BF16)], axis=1)
        if i % 2 == 0:
            x1 = _mm_gcols(f"aout{i}", cat, lw["out"], res=xc, deps=toks)
        else:
            x1 = _mm_full(f"bout{i}", cat, lw["out"], res=xc, deps=toks)
        toks = gather_arrive(1, [x1]) if i == 0 else []
        lw.update(gather_ready(2 * i + 1, [x1]))
        W.append(lw)
        if i == 0:
            conv_w_full, gv_full = small_weights([x1])
        h2, h2t = _rms_fwd(f"ffnnorm{i}", x1, ffn_norm_g[i], deps=toks, with_t=True)
        cw, cb = conv_params(i)
        a3 = _mm_gcols(f"up{i}", h2, lw["up"], split_out=True)
        toks = gather_arrive(2 * i + 2, [a3])
        act = _conv_fwd(f"conv{i}", a3, cw, cb)
        x2 = _mm_full(f"down{i}", act, lw["down"], res=x1, tn=512, tk=FF // 2, deps=toks)
        toks = []
        sv.update(h1t=h1t, memn=memn, kv=kv, proj=proj, cat=cat, x1=x1, h2t=h2t, a3=a3, act=act)
        saved.append(sv)
        xc = x2

    dx, dg_final, sq, dx16 = _final("final", xc, tgt, final_norm_g)
    loss_local = sq[0, 0] * (0.5 / D)

    chain = {}
    adamw_tokens = []

    def pair_begin(gname, members, dws):
        n = len(dws)
        recvs = [lax.empty((4,) + dw.shape[1:], dw.dtype) for dw in dws]
        ssem, rsem, bufs, tok = _split_start(f"rs_pair_start_{gname}", dws + recvs, 4 * n, _pair_plan(n))
        return dict(name=gname, members=members, n=n, bufs=bufs, sems=(ssem, rsem)), tok

    def pair_end_chip_begin(st, after):
        n, gname = st["n"], st["name"]
        bufs = _split_wait(f"rs_pair_wait_{gname}", st["bufs"], *st["sems"], _pair_plan(n), after)
        ps = [_pair_sum(f"rs_sum_{t}{l}", core, bufs[w], bufs[n + w]) for w, (t, l) in enumerate(st["members"])]
        recvs = [lax.empty((3,) + p.shape[1:], BF16) for p in ps]
        ssem, rsem, bufs, tok = _split_start(f"rs_chip_start_{gname}", ps + recvs, 3 * n, _chip_plan(n))
        return dict(name=gname, members=st["members"], n=n, bufs=bufs, sems=(ssem, rsem)), tok

    def chip_end_update(st, after, deps=()):
        n = st["n"]
        bufs = _split_wait(f"rs_chip_wait_{st['name']}", st["bufs"], *st["sems"], _chip_plan(n), after)
        for w, (t, l) in enumerate(st["members"]):
            wst, mst, vst = stacks[t]
            *chain[t], tok = _adamw_shard(f"adamw_{t}{l}", chip, l, wst, mst, vst, bufs[w], bufs[n + w], chain.get(t),
                                          deps)
            adamw_tokens.append(tok)

    pipe = {"pair": [], "chip": [], "deps": []}

    def take_deps():
        deps, pipe["deps"] = pipe["deps"], []
        return deps

    def submit(gname, members, dws):
        st, tok = pair_begin(gname, members, dws)
        pipe["pair"].append(st)
        pipe["deps"].append(tok)

    def advance(after):
        arrived, pipe["chip"] = pipe["chip"], []
        toks = []
        for st in pipe["pair"]:
            new, tok = pair_end_chip_begin(st, [after])
            pipe["chip"].append(new)
            toks.append(tok)
        pipe["pair"] = []
        done = len(adamw_tokens)
        for st in arrived:
            chip_end_update(st, [after], toks)
        pipe["deps"] += toks + adamw_tokens[done:]

    def small_start(tag, arrs, after):
        land = _cast_place(f"place_small_{tag}", dev1, _pack(arrs)[None], 0, F32)
        ssem, rsem, lands, tok = _split_start(f"small_start_{tag}", [land], N_DEV - 1, _broadcast_plan, after)
        return (lands, ssem, rsem), tok

    def small_end(tag, state, shapes, after):
        lands, ssem, rsem = state
        lands = _split_wait(f"small_wait_{tag}", lands, ssem, rsem, _broadcast_plan, after)
        return _unpack(_sum_slots(f"small_sum_{tag}", lands[0]), shapes)

    def late_small():
        return [dg_mix[0], dg_ffn[0], dg_mem[0], d_conv_b[0][None], d_conv_w[0][None]]

    assert depth >= 2
    big = {k: [None] * n for k, n in (("kv", depth), ("ain", depth // 2 + depth % 2), ("aout", depth // 2 + depth % 2),
                                      ("bin", depth // 2), ("bout", depth // 2), ("up", depth), ("down", depth))}
    dg_mix, dg_ffn, dg_mem = [None] * depth, [None] * depth, [None] * depth
    d_conv_w, d_conv_b = [None] * depth, [None] * depth
    d_gv, d_ws, d_sb = [None] * (depth // 2), [None] * (depth // 2), [None] * (depth // 2)
    for i in reversed(range(depth)):
        j = i // 2
        lw, sv = W[i], saved[i]
        cw, cb = conv_params(i)
        mix_members, ffn_members = groups_of(i)
        if i == 0:
            early_arrays = [loss_local.reshape(1), dg_final.reshape(D), jnp.concatenate(dg_mix[1:]),
                            jnp.concatenate(dg_ffn[1:]), jnp.concatenate(dg_mem[1:]), jnp.stack(d_ws), jnp.stack(d_sb),
                            jnp.stack(d_gv), jnp.stack(d_conv_b[1:]), jnp.stack(d_conv_w[1:])]
            early_state, tok = small_start("early", early_arrays, [dx])
            pipe["deps"].append(tok)
        deps = take_deps()
        dact = _mm_dx_full(f"ddown{i}", dx16, lw["down"], tm=1024, tko=FF // 4, tc=D, deps=deps)
        big["down"][i] = _mm_dw(f"wdown{i}", sv["act"], dx16, deps=deps).reshape(N_DEV, FF // N_DEV, D)
        da3, dcw, dcb = _conv_bwd(f"dconv{i}", sv["a3"], cw, cb, dact)
        d_conv_w[i] = dcw.transpose(1, 0, 2).reshape(3, 2 * FF)
        d_conv_b[i] = dcb.reshape(2 * FF)
        advance(da3)
        deps = take_deps()
        dh2 = _mm_dx_gcols(f"dup{i}", da3, lw["up"], split_in=True, deps=deps, tm=512, nsub=N_DEV // 2)
        big["up"][i] = _mm_dw_gcols(f"wup{i}", sv["h2t"], da3, N_DEV, split_in=True, deps=deps, a_t=True)
        dx1, dg_ffn[i], dx1_16 = _rms_bwd(f"dffnnorm{i}", dh2, sv["x1"], ffn_norm_g[i], dx)
        submit(f"f{i}", ffn_members, [big["up"][i], big["down"][i]])
        deps = take_deps()
        if i % 2 == 0:
            dcat = _mm_dx_gcols(f"daout{i}", dx1_16, lw["out"], deps=deps, nsub=N_DEV)
            big["aout"][j] = _mm_dw_gcols(f"waout{i}", sv["cat"], dx1_16, N_DEV, deps=deps)
            advance(dcat)
            dqm, dkv = _mem_bwd(f"dmemattn{i}", sv["proj"], a_q_blk, sv["kv"], dcat, 1, deps=take_deps())
            parts = [None] * 9
            for g in range(len(A_PATTERNS)):
                dq, dk, dv = _attn_bwd(f"dattn{i}_{g}", sv["proj"], g, dcat, 0, sv["tok"], sv["lse"])
                parts[g], parts[3 + g], parts[6 + g] = dq, dk, dv
            dproj = jnp.concatenate(parts + [dqm], axis=1)
            deps = []
            dh1 = _mm_dx_gcols(f"dain{i}", dproj, lw["in"], nsub=4)
        else:
            dcat = _mm_dx_full(f"dbout{i}", dx1_16, lw["out"], tm=1024, tko=1024, deps=deps)
            big["bout"][j] = _mm_dw(f"wbout{i}", sv["cat"], dx1_16, deps=deps).reshape(
                N_DEV, (B_W + GW) // N_DEV, D)
            advance(dcat)
            dqm, dkv = _mem_bwd(f"dmemattn{i}", sv["proj"], b_q_blk, sv["kv"], dcat, B_W // GW, deps=take_deps())
            bias_t = b_s_bias[j].T
            du, dvp, dgv, dws, dbt = _sgu_bwd(f"dsgu{i}", sv["proj"], gv_full[j], b_w_s[j], bias_t, dcat)
            d_gv[j], d_ws[j], d_sb[j] = dgv.reshape(B_W), dws, dbt.T
            dproj = jnp.concatenate([du, dvp, dqm], axis=1)
            deps = []
            dh1 = _mm_dx_full(f"dbin{i}", dproj, lw["in"], tm=1024, tko=1024, tc=b_in)
        dmemn = _mm_dx_full(f"dkvproj{i}", dkv, lw["kv"], tko=1024)
        _, dg_mem[i], _ = _rms_bwd(f"dmemnorm{i}", dmemn, mem0, mem_norm_g[i])
        dx, dg_mix[i], dx16 = _rms_bwd(f"dmixnorm{i}", dh1, sv["x0"], mix_norm_g[i], dx1)
        if i == 0:
            late_state, late_tok = small_start("late", late_small(), [dx])
            deps = deps + [late_tok]
        if i % 2 == 0:
            big["ain"][j] = _mm_dw_gcols(f"wain{i}", sv["h1t"], dproj, N_DEV, deps=deps, a_t=True)
        else:
            dwin = _mm_dw(f"wbin{i}", sv["h1t"], dproj, tko=1024, tn=512, deps=deps, a_t=True)
            big["bin"][j] = dwin.reshape(D, N_DEV, b_in // N_DEV).transpose(1, 0, 2)
        big["kv"][i] = _mm_dw(f"wkv{i}", sv["memn"], dkv, tko=1024, deps=deps).reshape(N_DEV, D // N_DEV, 2 * GW)
        submit(f"m{i}", mix_members, [big[t][l] for t, l in mix_members])
    grad_x = dx[None]

    last_chips, toks = [], []
    for st in pipe["pair"]:
        new, tok = pair_end_chip_begin(st, [dx])
        last_chips.append(new)
        toks.append(tok)
    g_early = small_end("early", early_state, [a.shape for a in early_arrays], [dx])
    for st in pipe["chip"]:
        chip_end_update(st, [g_early[1]], toks)
    g_late = small_end("late", late_state, [a.shape for a in late_small()], list(adamw_tokens))

    loss = g_early[0][0]
    layer0 = dict(zip(("mix", "ffn", "mem", "conv_b", "conv_w"), g_late))
    rest = dict(zip(("final", "mix", "ffn", "mem", "w_s", "s_bias", "gv", "conv_b", "conv_w"), g_early[1:]))
    g_cw_full = jnp.concatenate([layer0["conv_w"], rest["conv_w"]])
    g_gv = lax.dynamic_slice_in_dim(rest["gv"], dev * b_v_norm_g.shape[1], b_v_norm_g.shape[1], axis=1)
    g_cw = lax.dynamic_slice_in_dim(g_cw_full, dev * ffn_conv_w.shape[2], ffn_conv_w.shape[2], axis=2)
    g_all = [jnp.concatenate([layer0["mix"], rest["mix"]]), jnp.concatenate([layer0["ffn"], rest["ffn"]]),
             jnp.concatenate([layer0["mem"], rest["mem"]]), rest["w_s"], rest["s_bias"],
             jnp.concatenate([layer0["conv_b"], rest["conv_b"]]), rest["final"], g_gv, g_cw]
    names = ["mix_norm_g", "ffn_norm_g", "mem_norm_g", "b_w_s", "b_s_bias", "ffn_conv_b", "final_norm_g",
             "b_v_norm_g", "ffn_conv_w"]
    ws = [mix_norm_g, ffn_norm_g, mem_norm_g, b_w_s, b_s_bias, ffn_conv_b, final_norm_g, b_v_norm_g, ffn_conv_w]
    ms = [m_mix_norm_g, m_ffn_norm_g, m_mem_norm_g, m_b_w_s, m_b_s_bias, m_ffn_conv_b, m_final_norm_g,
          m_b_v_norm_g, m_ffn_conv_w]
    vs = [v_mix_norm_g, v_ffn_norm_g, v_mem_norm_g, v_b_w_s, v_b_s_bias, v_ffn_conv_b, v_final_norm_g,
          v_b_v_norm_g, v_ffn_conv_w]
    shapes = [w.shape for w in ws]
    d_p, m_p, v_p = _adamw_flat("adamw_small", _pack(ws), _pack(g_all), _pack(ms), _pack(vs))
    res = {}
    for n, g, d, nm, nv in zip(names, g_all, _unpack(d_p, shapes), _unpack(m_p, shapes), _unpack(v_p, shapes)):
        res[n] = [g, d, nm, nv]
    for st in last_chips:
        chip_end_update(st, [d_p] + list(adamw_tokens))
    for tag, name in (("kv", "w_mem_kv"), ("ain", "a_w_in"), ("aout", "a_w_out"), ("bin", "b_w_in"),
                      ("bout", "b_w_out"), ("up", "ffn_w_up"), ("down", "ffn_w_down")):
        res[name] = list(chain[tag])

    order = ["mix_norm_g", "ffn_norm_g", "mem_norm_g", "w_mem_kv", "a_w_in", "a_w_out", "b_w_in", "b_v_norm_g",
             "b_w_s", "b_s_bias", "b_w_out", "ffn_w_up", "ffn_conv_w", "ffn_conv_b", "ffn_w_down", "final_norm_g"]
    return (loss, grad_x, *[res[n][0] for n in order], *[res[n][1] for n in order],
            *[res[n][2] for n in order], *[res[n][3] for n in order])
```

```python
import functools

import numpy as np
import jax
import jax.numpy as jnp
from jax import lax
from jax.experimental import pallas as pl
from jax.experimental.pallas import tpu as pltpu

F32 = jnp.float32
BF16 = jnp.bfloat16
MESH = pl.DeviceIdType.MESH
N_DEV = 8

EPS = 1e-6
NEG = -1e30
HEAD = 128
HPG = 4
GW = HPG * HEAD
A_PATTERNS = ((128, 1), (512, 4), (2048, 16))
A_HEADS = HPG * len(A_PATTERNS)
QBLK = 128
B_GROUPS = 12
B_W = B_GROUPS * HEAD
SLOPES = (2.0 ** (-8.0 * (np.arange(A_HEADS) + 1) / A_HEADS)).astype(np.float32)
SCALE = HEAD ** -0.5

ADAM_LR = 0.001
ADAM_B1 = 0.9
ADAM_B2 = 0.999
ADAM_EPS = 1e-08
ADAM_WD = 0.01
ADAM_STEP = 10

V7X_VMEM_LIMIT = 50 * 1024 * 1024

NN = (((1,), (0,)), ((), ()))
NT = (((1,), (1,)), ((), ()))
TN = (((0,), (0,)), ((), ()))


def _cp(*sem):
    return pltpu.CompilerParams(dimension_semantics=sem, vmem_limit_bytes=V7X_VMEM_LIMIT)


def _dot(a, b, dims=NN):
    return lax.dot_general(a.astype(BF16), b.astype(BF16), dims, preferred_element_type=F32)


def _tile(n, pref):
    t = min(n, pref)
    assert n % t == 0, (n, pref)
    return t


def _row_tile(rows, cols, mib=1):
    best = None
    for t in range(16, rows + 1, 16):
        if rows % t == 0 and t * cols * 4 <= (mib << 20):
            best = t
    if best is None:
        best = rows
    return best


_DEP = pl.BlockSpec((8, 128), lambda *_: (0, 0))


def _matmul(name, dims, grid, a, a_spec, b, b_spec, out_shape, o_spec, tile, res=None, res_spec=None, deps=(),
            nsub=1):
    nk = grid[2]
    has_res = res is not None

    def body(*refs):
        a_ref, b_ref = refs[0], refs[1]
        r_ref = refs[2] if has_res else None
        o_ref, acc_ref = refs[-2], refs[-1]
        if nsub == 1:
            part = _dot(a_ref[...], b_ref[...], dims)
        else:
            w = a_ref.shape[-1] // nsub
            part = _dot(a_ref[:, :w], b_ref[0], dims)
            for q in range(1, nsub):
                part = part + _dot(a_ref[:, q * w:(q + 1) * w], b_ref[q], dims)

        def finish(val):
            if has_res:
                val = val + r_ref[...]
            o_ref[...] = val.astype(o_ref.dtype)

        if nk == 1:
            finish(part)
        else:
            k = pl.program_id(2)

            @pl.when(k == 0)
            def _():
                acc_ref[...] = part

            @pl.when(k > 0)
            def _():
                acc_ref[...] += part

            @pl.when(k == nk - 1)
            def _():
                finish(acc_ref[...])

    ins = [a, b] + ([res] if has_res else []) + list(deps)
    specs = [a_spec, b_spec] + ([res_spec] if has_res else []) + [_DEP] * len(deps)
    return pl.pallas_call(
        body, name=name, grid=grid, in_specs=specs, out_specs=o_spec, out_shape=out_shape,
        scratch_shapes=[pltpu.VMEM(tile if nk > 1 else (8, 128), F32)],
        compiler_params=_cp("parallel", "parallel", "arbitrary"))(*ins)


def _mm_full(name, a, w, res=None, tm=1024, tn=512, tk=2048, deps=()):
    M, K = a.shape
    N = w.shape[1]
    tm, tn, tk = _tile(M, tm), _tile(N, tn), _tile(K, tk)
    return _matmul(
        name, NN, (N // tn, M // tm, K // tk),
        a, pl.BlockSpec((tm, tk), lambda j, i, k: (i, k)),
        w, pl.BlockSpec((tk, tn), lambda j, i, k: (k, j)),
        jax.ShapeDtypeStruct((M, N), F32), pl.BlockSpec((tm, tn), lambda j, i, k: (i, j)), (tm, tn),
        res, pl.BlockSpec((tm, tn), lambda j, i, k: (i, j)), deps=deps)


def _mm_gcols(name, a, wg, res=None, split_out=False, tm=1024, deps=(), out_dtype=F32):
    M, K = a.shape
    G, _, Nl = wg.shape
    tm = _tile(M, tm)
    hg = G // 2
    if split_out:
        shape = jax.ShapeDtypeStruct((2, M, hg * Nl), out_dtype)
        o_spec = pl.BlockSpec((None, tm, Nl), lambda g, i, k: (g // hg, i, g % hg))
    else:
        shape = jax.ShapeDtypeStruct((M, G * Nl), out_dtype)
        o_spec = pl.BlockSpec((tm, Nl), lambda g, i, k: (i, g))
    return _matmul(
        name, NN, (G, M // tm, 1),
        a, pl.BlockSpec((tm, K), lambda g, i, k: (i, 0)),
        wg, pl.BlockSpec((None, K, Nl), lambda g, i, k: (g, 0, 0)),
        shape, o_spec, (tm, Nl),
        res, pl.BlockSpec((tm, Nl), lambda g, i, k: (i, g)), deps=deps)


def _mm_dx_full(name, dy, w, tm=512, tko=512, tc=2048, deps=()):
    M, N = dy.shape
    K = w.shape[0]
    tm, tko, tc = _tile(M, tm), _tile(K, tko), _tile(N, tc)
    return _matmul(
        name, NT, (K // tko, M // tm, N // tc),
        dy, pl.BlockSpec((tm, tc), lambda j, i, k: (i, k)),
        w, pl.BlockSpec((tko, tc), lambda j, i, k: (j, k)),
        jax.ShapeDtypeStruct((M, K), F32), pl.BlockSpec((tm, tko), lambda j, i, k: (i, j)), (tm, tko), deps=deps)


def _mm_dx_gcols(name, dy, wg, split_in=False, tm=1024, tko=1024, deps=(), nsub=1):
    G, K, Nl = wg.shape
    M = dy.shape[-2]
    tm, tko = _tile(M, tm), _tile(K, tko)
    hs = G // 2 // nsub if split_in else None
    if split_in:
        dy_spec = pl.BlockSpec((None, tm, nsub * Nl), lambda j, i, g: (g // hs, i, g % hs))
    else:
        dy_spec = pl.BlockSpec((tm, nsub * Nl), lambda j, i, g: (i, g))
    w_block = (None, tko, Nl) if nsub == 1 else (nsub, tko, Nl)
    return _matmul(
        name, NT, (K // tko, M // tm, G // nsub),
        dy, dy_spec,
        wg, pl.BlockSpec(w_block, lambda j, i, g: (g, j, 0)),
        jax.ShapeDtypeStruct((M, K), F32), pl.BlockSpec((tm, tko), lambda j, i, g: (i, j)), (tm, tko), deps=deps,
        nsub=nsub)


def _lhs_of_dw(a_t, ts, tko, index):
    if a_t:
        return NN, pl.BlockSpec((tko, ts), lambda *ids: index(*ids))
    return TN, pl.BlockSpec((ts, tko), lambda *ids: index(*ids)[::-1])


def _mm_dw(name, a, dy, tko=512, tn=1024, ts=2048, deps=(), a_t=False):
    K1, S = a.shape if a_t else a.shape[::-1]
    N = dy.shape[1]
    tko, tn, ts = _tile(K1, tko), _tile(N, tn), _tile(S, ts)
    dims, a_spec = _lhs_of_dw(a_t, ts, tko, lambda i, j, k: (j, k))
    return _matmul(
        name, dims, (N // tn, K1 // tko, S // ts),
        a, a_spec,
        dy, pl.BlockSpec((ts, tn), lambda i, j, k: (k, i)),
        jax.ShapeDtypeStruct((K1, N), BF16), pl.BlockSpec((tko, tn), lambda i, j, k: (j, i)), (tko, tn), deps=deps)


def _mm_dw_gcols(name, a, dy, G, split_in=False, tko=1024, ts=2048, deps=(), a_t=False):
    K1, S = a.shape if a_t else a.shape[::-1]
    Nl = (dy.shape[-1] * (2 if split_in else 1)) // G
    tko, ts = _tile(K1, tko), _tile(S, ts)
    hg = G // 2
    dims, a_spec = _lhs_of_dw(a_t, ts, tko, lambda g, j, k: (j, k))
    if split_in:
        dy_spec = pl.BlockSpec((None, ts, Nl), lambda g, j, k: (g // hg, k, g % hg))
    else:
        dy_spec = pl.BlockSpec((ts, Nl), lambda g, j, k: (k, g))
    return _matmul(
        name, dims, (G, K1 // tko, S // ts),
        a, a_spec,
        dy, dy_spec,
        jax.ShapeDtypeStruct((G, K1, Nl), BF16), pl.BlockSpec((None, tko, Nl), lambda g, j, k: (g, j, 0)),
        (tko, Nl), deps=deps)


def _rms_fwd(name, x, g, tr=256, deps=(), with_t=False):
    S, D = x.shape
    tr = _tile(S, tr)

    def body(x_ref, g_ref, *rest):
        xf = x_ref[...]
        r = lax.rsqrt(jnp.mean(xf * xf, axis=-1, keepdims=True) + EPS)
        y = xf * r * g_ref[...]
        if with_t:
            rest[-2][...] = y.astype(BF16)
            rest[-1][...] = y.T.astype(BF16)
        else:
            rest[-1][...] = y.astype(BF16)

    row = pl.BlockSpec((tr, D), lambda i: (i, 0))
    out_specs, out_shape = row, jax.ShapeDtypeStruct((S, D), BF16)
    if with_t:
        out_specs = [row, pl.BlockSpec((D, tr), lambda i: (0, i))]
        out_shape = [out_shape, jax.ShapeDtypeStruct((D, S), BF16)]
    return pl.pallas_call(
        body, name=name, grid=(S // tr,),
        in_specs=[row, pl.BlockSpec((1, D), lambda i: (0, 0))] + [_DEP] * len(deps),
        out_specs=out_specs, out_shape=out_shape, compiler_params=_cp("parallel"))(x, g.reshape(1, D), *deps)


def _rms_bwd(name, dh, x, g, dres=None, tr=256):
    S, D = x.shape
    tr = _tile(S, tr)
    has_res = dres is not None

    def body(*refs):
        dh_ref, x_ref, g_ref = refs[:3]
        dres_ref = refs[3] if has_res else None
        dx_ref, dg_ref, dx16_ref = refs[-3:]
        xf = x_ref[...]
        r = lax.rsqrt(jnp.mean(xf * xf, axis=-1, keepdims=True) + EPS)
        xh = xf * r
        dhv = dh_ref[...]
        dxh = dhv * g_ref[...]
        dx = r * (dxh - xh * jnp.mean(dxh * xh, axis=-1, keepdims=True))
        if has_res:
            dx = dx + dres_ref[...]
        dx_ref[...] = dx
        dx16_ref[...] = dx.astype(BF16)
        part = jnp.sum(dhv * xh, axis=0, keepdims=True)
        i = pl.program_id(0)

        @pl.when(i == 0)
        def _():
            dg_ref[...] = part

        @pl.when(i > 0)
        def _():
            dg_ref[...] += part

    row = pl.BlockSpec((tr, D), lambda i: (i, 0))
    vec = pl.BlockSpec((1, D), lambda i: (0, 0))
    ins = [dh, x, g.reshape(1, D)] + ([dres] if has_res else [])
    return pl.pallas_call(
        body, name=name, grid=(S // tr,),
        in_specs=[row, row, vec] + ([row] if has_res else []),
        out_specs=[row, vec, row],
        out_shape=[jax.ShapeDtypeStruct((S, D), F32), jax.ShapeDtypeStruct((1, D), F32),
                   jax.ShapeDtypeStruct((S, D), BF16)],
        compiler_params=_cp("arbitrary"))(*ins)


def _final(name, x, tgt, g, tr=256):
    S, D = x.shape
    tr = _tile(S, tr)

    def body(x_ref, t_ref, g_ref, dx_ref, dg_ref, loss_ref, dx16_ref):
        xf = x_ref[...]
        gv = g_ref[...]
        r = lax.rsqrt(jnp.mean(xf * xf, axis=-1, keepdims=True) + EPS)
        xh = xf * r
        err = xh * gv - t_ref[...]
        sq = jnp.sum(jnp.sum(err * err, axis=1, keepdims=True), axis=0, keepdims=True)
        dy = err * (1.0 / D)
        dxh = dy * gv
        dx = r * (dxh - xh * jnp.mean(dxh * xh, axis=-1, keepdims=True))
        dx_ref[...] = dx
        dx16_ref[...] = dx.astype(BF16)
        part = jnp.sum(dy * xh, axis=0, keepdims=True)
        lpart = jnp.broadcast_to(sq, (8, 128))
        i = pl.program_id(0)

        @pl.when(i == 0)
        def _():
            dg_ref[...] = part
            loss_ref[...] = lpart

        @pl.when(i > 0)
        def _():
            dg_ref[...] += part
            loss_ref[...] += lpart

    row = pl.BlockSpec((tr, D), lambda i: (i, 0))
    vec = pl.BlockSpec((1, D), lambda i: (0, 0))
    return pl.pallas_call(
        body, name=name, grid=(S // tr,), in_specs=[row, row, vec],
        out_specs=[row, vec, pl.BlockSpec((8, 128), lambda i: (0, 0)), row],
        out_shape=[jax.ShapeDtypeStruct((S, D), F32), jax.ShapeDtypeStruct((1, D), F32),
                   jax.ShapeDtypeStruct((8, 128), F32), jax.ShapeDtypeStruct((S, D), BF16)],
        compiler_params=_cp("arbitrary"))(x, tgt, g.reshape(1, D))


def _band_specs(nb, col_of):
    prev = pl.BlockSpec((QBLK, GW), lambda r, b: (jnp.maximum(b - 1, 0), col_of(r)))
    cur = pl.BlockSpec((QBLK, GW), lambda r, b: (b, col_of(r)))
    nxt = pl.BlockSpec((QBLK, GW), lambda r, b: (jnp.minimum(b + 1, nb - 1), col_of(r)))
    return [prev, cur, nxt]


HALF = QBLK // 2
WIN = 2 * QBLK


def _cat3(refs, sl):
    prev, cur, nxt = refs
    return jnp.concatenate([prev[HALF:, sl], cur[:, sl], nxt[:HALF, sl]], axis=0)


def _group_view(proj, g):
    _, dil = A_PATTERNS[g]
    S, C = proj.shape
    ng = len(A_PATTERNS)
    if dil == 1:
        return proj, lambda which, r: which * ng + g
    cols = [proj[:, (which * ng + g) * GW:(which * ng + g + 1) * GW] for which in range(3)]
    return jnp.concatenate(cols, axis=1).reshape(S // dil, dil * 3 * GW), lambda which, r: r * 3 + which


def _attn_fwd(name, proj, g):
    window, dil = A_PATTERNS[g]
    n_side = (window // 2) // dil
    S, C = proj.shape
    L = S // dil
    nb = L // QBLK
    pv, col = _group_view(proj, g)

    assert n_side == HALF

    def body(q_ref, kp, kc, kn, vp, vc, vn, o_ref, lse_ref):
        b = pl.program_id(1)
        jq = b * QBLK + lax.broadcasted_iota(jnp.int32, (QBLK, WIN), 0)
        jk = b * QBLK - HALF + lax.broadcasted_iota(jnp.int32, (QBLK, WIN), 1)
        rel = jnp.abs(jk - jq)
        mask = (rel <= n_side) & (jk >= 0) & (jk < L)
        dist = rel.astype(F32) * float(dil)
        for hh in range(HPG):
            sl = slice(hh * HEAD, (hh + 1) * HEAD)
            k = _cat3((kp, kc, kn), sl)
            v = _cat3((vp, vc, vn), sl)
            s = _dot(q_ref[:, sl], k, NT) * SCALE - float(SLOPES[g * HPG + hh]) * dist
            s = jnp.where(mask, s, NEG)
            m = jnp.max(s, axis=1, keepdims=True)
            p = jnp.exp(s - m)
            l = jnp.sum(p, axis=1, keepdims=True)
            o_ref[:, sl] = _dot(p, v) / l
            lse_ref[:, sl] = jnp.broadcast_to(m + jnp.log(l), (QBLK, HEAD))

    q_spec = pl.BlockSpec((QBLK, GW), lambda r, b: (b, col(0, r)))
    k_specs = _band_specs(nb, lambda r: col(1, r))
    v_specs = _band_specs(nb, lambda r: col(2, r))
    o_spec = pl.BlockSpec((QBLK, GW), lambda r, b: (b, r))
    shape = jax.ShapeDtypeStruct((L, dil * GW), F32)
    o, lse = pl.pallas_call(
        body, name=name, grid=(dil, nb), in_specs=[q_spec] + k_specs + v_specs,
        out_specs=[o_spec, o_spec], out_shape=[shape, shape],
        compiler_params=_cp("parallel", "parallel"))(pv, pv, pv, pv, pv, pv, pv)
    return o.reshape(S, GW), lse.reshape(S, GW)


def _attn_merge(name, outs, lses, tr=256):
    S = outs[0].shape[0]
    tr = _tile(S, tr)
    ng = len(outs)

    def body(*refs):
        o_refs, l_refs = refs[:ng], refs[ng:2 * ng]
        tok_ref, lse_ref, tok16_ref = refs[-3:]
        ls = [r[...] for r in l_refs]
        m = functools.reduce(jnp.maximum, ls)
        es = [jnp.exp(l - m) for l in ls]
        tot = functools.reduce(lambda a, b: a + b, es)
        acc = None
        for e, o_ref in zip(es, o_refs):
            term = (e / tot) * o_ref[...]
            acc = term if acc is None else acc + term
        tok_ref[...] = acc
        tok16_ref[...] = acc.astype(BF16)
        lse_ref[...] = m + jnp.log(tot)

    row = pl.BlockSpec((tr, GW), lambda i: (i, 0))
    shape = jax.ShapeDtypeStruct((S, GW), F32)
    return pl.pallas_call(
        body, name=name, grid=(S // tr,), in_specs=[row] * (2 * ng), out_specs=[row, row, row],
        out_shape=[shape, shape, jax.ShapeDtypeStruct((S, GW), BF16)],
        compiler_params=_cp("parallel"))(*outs, *lses)


def _attn_bwd(name, proj, g, dtok_src, dtok_blk, tok, lse):
    window, dil = A_PATTERNS[g]
    n_side = (window // 2) // dil
    S, C = proj.shape
    L = S // dil
    nb = L // QBLK
    pv, col = _group_view(proj, g)
    assert n_side == HALF
    if dil == 1:
        dcb, dv_ = dtok_src.shape[1] // GW, dtok_src
    else:
        dcb, dv_ = 1, dtok_src[:, dtok_blk * GW:(dtok_blk + 1) * GW].reshape(L, dil * GW)
        dtok_blk = 0
    ov = tok.reshape(L, dil * GW)
    lv = lse.reshape(L, dil * GW)

    def body(qp, qc, qn, kp, kc, kn, vp, vc, vn, dop, doc, don, op, oc, on, lp, lc, ln,
             dq_ref, dk_ref, dv_ref):
        b = pl.program_id(1)
        jq = b * QBLK + lax.broadcasted_iota(jnp.int32, (QBLK, WIN), 0)
        jk = b * QBLK - HALF + lax.broadcasted_iota(jnp.int32, (QBLK, WIN), 1)
        rel = jnp.abs(jk - jq)
        mask = (rel <= n_side) & (jk >= 0) & (jk < L)
        dist = rel.astype(F32) * float(dil)
        jq3 = b * QBLK - HALF + lax.broadcasted_iota(jnp.int32, (WIN, QBLK), 0)
        jk1 = b * QBLK + lax.broadcasted_iota(jnp.int32, (WIN, QBLK), 1)
        rel3 = jnp.abs(jk1 - jq3)
        mask3 = (rel3 <= n_side) & (jq3 >= 0) & (jq3 < L)
        dist3 = rel3.astype(F32) * float(dil)
        for hh in range(HPG):
            sl = slice(hh * HEAD, (hh + 1) * HEAD)
            one = slice(hh * HEAD, hh * HEAD + 1)
            slope = float(SLOPES[g * HPG + hh])
            q = qc[:, sl]
            do = doc[:, sl]
            k3 = _cat3((kp, kc, kn), sl)
            v3 = _cat3((vp, vc, vn), sl)
            delta = jnp.sum(do * oc[:, sl], axis=1, keepdims=True)
            s = _dot(q, k3, NT) * SCALE - slope * dist
            p = jnp.where(mask, jnp.exp(s - lc[:, one]), 0.0)
            ds = p * (_dot(do, v3, NT) - delta)
            dq_ref[:, sl] = (_dot(ds, k3) * SCALE).astype(dq_ref.dtype)

            q3 = _cat3((qp, qc, qn), sl)
            do3 = _cat3((dop, doc, don), sl)
            o3 = _cat3((op, oc, on), sl)
            lse3 = _cat3((lp, lc, ln), sl)[:, :1]
            delta3 = jnp.sum(do3 * o3, axis=1, keepdims=True)
            k = kc[:, sl]
            v = vc[:, sl]
            s3 = _dot(q3, k, NT) * SCALE - slope * dist3
            p3 = jnp.where(mask3, jnp.exp(s3 - lse3), 0.0)
            ds3 = p3 * (_dot(do3, v, NT) - delta3)
            dv_ref[:, sl] = _dot(p3, do3, TN).astype(dv_ref.dtype)
            dk_ref[:, sl] = (_dot(ds3, q3, TN) * SCALE).astype(dk_ref.dtype)

    specs = (_band_specs(nb, lambda r: col(0, r)) + _band_specs(nb, lambda r: col(1, r))
             + _band_specs(nb, lambda r: col(2, r))
             + _band_specs(nb, lambda r: r * dcb + dtok_blk)
             + _band_specs(nb, lambda r: r) + _band_specs(nb, lambda r: r))
    o_spec = pl.BlockSpec((QBLK, GW), lambda r, b: (b, r))
    shape = jax.ShapeDtypeStruct((L, dil * GW), BF16)
    outs = pl.pallas_call(
        body, name=name, grid=(dil, nb), in_specs=specs, out_specs=[o_spec] * 3, out_shape=[shape] * 3,
        compiler_params=_cp("parallel", "parallel"))(*([pv] * 9 + [dv_] * 3 + [ov] * 3 + [lv] * 3))
    return [o.reshape(S, GW) for o in outs]


def _mem_fwd(name, proj, q_blk, kv, tq=256):
    S = proj.shape[0]
    M = kv.shape[0]
    tq = _tile(S, tq)

    def body(q_ref, kv_ref, o_ref):
        for hh in range(HPG):
            sl = slice(hh * HEAD, (hh + 1) * HEAD)
            k = kv_ref[:, sl]
            v = kv_ref[:, GW + hh * HEAD:GW + (hh + 1) * HEAD]
            s = _dot(q_ref[:, sl], k, NT) * SCALE
            m = jnp.max(s, axis=1, keepdims=True)
            p = jnp.exp(s - m)
            p = p / jnp.sum(p, axis=1, keepdims=True)
            o_ref[:, sl] = _dot(p, v).astype(o_ref.dtype)

    return pl.pallas_call(
        body, name=name, grid=(S // tq,),
        in_specs=[pl.BlockSpec((tq, GW), lambda i: (i, q_blk)), pl.BlockSpec((M, 2 * GW), lambda i: (0, 0))],
        out_specs=pl.BlockSpec((tq, GW), lambda i: (i, 0)),
        out_shape=jax.ShapeDtypeStruct((S, GW), BF16), compiler_params=_cp("parallel"))(proj, kv)


def _mem_bwd(name, proj, q_blk, kv, dcat, do_blk, tq=256, deps=()):
    S = proj.shape[0]
    M = kv.shape[0]
    tq = _tile(S, tq)

    def body(q_ref, kv_ref, do_ref, *rest):
        dq_ref, dkv_ref = rest[-2:]
        i = pl.program_id(0)
        for hh in range(HPG):
            sl = slice(hh * HEAD, (hh + 1) * HEAD)
            vsl = slice(GW + hh * HEAD, GW + (hh + 1) * HEAD)
            q = q_ref[:, sl]
            do = do_ref[:, sl]
            k = kv_ref[:, sl]
            v = kv_ref[:, vsl]
            s = _dot(q, k, NT) * SCALE
            m = jnp.max(s, axis=1, keepdims=True)
            p = jnp.exp(s - m)
            p = p / jnp.sum(p, axis=1, keepdims=True)
            dp = _dot(do, v, NT)
            ds = p * (dp - jnp.sum(dp * p, axis=1, keepdims=True))
            dq_ref[:, sl] = (_dot(ds, k) * SCALE).astype(dq_ref.dtype)
            dk = _dot(ds, q, TN) * SCALE
            dvv = _dot(p, do, TN)

            @pl.when(i == 0)
            def _():
                dkv_ref[:, sl] = dk
                dkv_ref[:, vsl] = dvv

            @pl.when(i > 0)
            def _():
                dkv_ref[:, sl] += dk
                dkv_ref[:, vsl] += dvv

    return pl.pallas_call(
        body, name=name, grid=(S // tq,),
        in_specs=[pl.BlockSpec((tq, GW), lambda i: (i, q_blk)), pl.BlockSpec((M, 2 * GW), lambda i: (0, 0)),
                  pl.BlockSpec((tq, GW), lambda i: (i, do_blk))] + [_DEP] * len(deps),
        out_specs=[pl.BlockSpec((tq, GW), lambda i: (i, 0)), pl.BlockSpec((M, 2 * GW), lambda i: (0, 0))],
        out_shape=[jax.ShapeDtypeStruct((S, GW), BF16), jax.ShapeDtypeStruct((M, 2 * GW), F32)],
        compiler_params=_cp("arbitrary"))(proj, kv, dcat, *deps)


_RSQRT2 = float(1.0 / np.sqrt(2.0))
_RSQRT2PI = float(1.0 / np.sqrt(2.0 * np.pi))


def _gelu(x):
    return 0.5 * x * (1.0 + lax.erf(x * _RSQRT2))


def _gelu_and_grad(x):
    cdf = 0.5 * (1.0 + lax.erf(x * _RSQRT2))
    return x * cdf, cdf + x * jnp.exp(-0.5 * x * x) * _RSQRT2PI


def _sgu_fwd(name, proj, gv, w_s, bias_t):
    S = proj.shape[0]
    nch = S // HEAD

    def body(u_ref, v_ref, gv_ref, ws_ref, b_ref, o_ref):
        v = _gelu(v_ref[...])
        r = lax.rsqrt(jnp.mean(v * v, axis=-1, keepdims=True) + EPS)
        vn = v * r * gv_ref[...]
        for gg in range(B_GROUPS):
            sl = slice(gg * HEAD, (gg + 1) * HEAD)
            mixed = _dot(ws_ref[gg], vn[:, sl]) + b_ref[:, gg:gg + 1]
            o_ref[:, sl] = (_gelu(u_ref[:, sl]) * mixed).astype(o_ref.dtype)

    return pl.pallas_call(
        body, name=name, grid=(nch,),
        in_specs=[pl.BlockSpec((HEAD, B_W), lambda c: (c, 0)), pl.BlockSpec((HEAD, B_W), lambda c: (c, 1)),
                  pl.BlockSpec((1, B_W), lambda c: (0, 0)),
                  pl.BlockSpec((B_GROUPS, HEAD, HEAD), lambda c: (0, 0, 0)),
                  pl.BlockSpec((HEAD, B_GROUPS), lambda c: (0, 0))],
        out_specs=pl.BlockSpec((HEAD, B_W), lambda c: (c, 0)),
        out_shape=jax.ShapeDtypeStruct((S, B_W), BF16),
        compiler_params=_cp("parallel"))(proj, proj, gv.reshape(1, B_W), w_s, bias_t)


def _sgu_bwd(name, proj, gv, w_s, bias_t, dcat):
    S = proj.shape[0]
    nch = S // HEAD

    def body(u_ref, v_ref, gv_ref, ws_ref, b_ref, dt_ref, du_ref, dvp_ref, dgv_ref, dws_ref, db_ref, dvn_ref):
        c = pl.program_id(0)
        vpre = v_ref[...]
        v, v_slope = _gelu_and_grad(vpre)
        r = lax.rsqrt(jnp.mean(v * v, axis=-1, keepdims=True) + EPS)
        vh = v * r
        gvv = gv_ref[...]
        vn = vh * gvv
        for gg in range(B_GROUPS):
            sl = slice(gg * HEAD, (gg + 1) * HEAD)
            upre = u_ref[:, sl]
            dt = dt_ref[:, sl]
            vng = vn[:, sl]
            mixed = _dot(ws_ref[gg], vng) + b_ref[:, gg:gg + 1]
            u, u_slope = _gelu_and_grad(upre)
            du_ref[:, sl] = (dt * mixed * u_slope).astype(du_ref.dtype)
            dmix = dt * u
            dvn_ref[:, sl] = _dot(ws_ref[gg], dmix, TN)
            dws = _dot(dmix, vng, NT)
            dbs = jnp.sum(dmix, axis=1, keepdims=True)

            @pl.when(c == 0)
            def _():
                dws_ref[gg] = dws
                db_ref[:, gg:gg + 1] = dbs

            @pl.when(c > 0)
            def _():
                dws_ref[gg] += dws
                db_ref[:, gg:gg + 1] += dbs

        dvn = dvn_ref[...]
        dgp = jnp.sum(dvn * vh, axis=0, keepdims=True)
        dvh = dvn * gvv
        dv = r * (dvh - vh * jnp.mean(dvh * vh, axis=-1, keepdims=True))
        dvp_ref[...] = (dv * v_slope).astype(dvp_ref.dtype)

        @pl.when(c == 0)
        def _():
            dgv_ref[...] = dgp

        @pl.when(c > 0)
        def _():
            dgv_ref[...] += dgp

    blk = lambda j: pl.BlockSpec((HEAD, B_W), lambda c: (c, j))
    vec = pl.BlockSpec((1, B_W), lambda c: (0, 0))
    ws_spec = pl.BlockSpec((B_GROUPS, HEAD, HEAD), lambda c: (0, 0, 0))
    b_spec = pl.BlockSpec((HEAD, B_GROUPS), lambda c: (0, 0))
    du, dvp, dgv, dws, db = pl.pallas_call(
        body, name=name, grid=(nch,),
        in_specs=[blk(0), blk(1), vec, ws_spec, b_spec, blk(0)],
        out_specs=[blk(0), blk(0), vec, ws_spec, b_spec],
        out_shape=[jax.ShapeDtypeStruct((S, B_W), BF16), jax.ShapeDtypeStruct((S, B_W), BF16),
                   jax.ShapeDtypeStruct((1, B_W), F32), jax.ShapeDtypeStruct((B_GROUPS, HEAD, HEAD), F32),
                   jax.ShapeDtypeStruct((HEAD, B_GROUPS), F32)],
        scratch_shapes=[pltpu.VMEM((HEAD, B_W), F32)],
        compiler_params=_cp("arbitrary"))(proj, proj, gv.reshape(1, B_W), w_s, bias_t, dcat)
    return du, dvp, dgv, dws, db


def _shift_down(a, row):
    return jnp.where(row == 0, 0.0, pltpu.roll(a, 1, 0))


def _shift_up(a, row):
    n = a.shape[0]
    return jnp.where(row == n - 1, 0.0, pltpu.roll(a, n - 1, 0))


def _conv(a, w, b, row):
    return _shift_down(a, row) * w[0:1] + a * w[1:2] + _shift_up(a, row) * w[2:3] + b


def _conv_fwd(name, a3, cw, cb, tc=256):
    _, S, FF = a3.shape
    tc = _tile(FF, tc)

    def body(a_ref, w_ref, b_ref, o_ref):
        row = lax.broadcasted_iota(jnp.int32, (S, tc), 0)
        cg = _conv(a_ref[0], w_ref[0], b_ref[0], row)
        cv = _conv(a_ref[1], w_ref[1], b_ref[1], row)
        o_ref[...] = (_gelu(cg) * cv).astype(o_ref.dtype)

    return pl.pallas_call(
        body, name=name, grid=(FF // tc,),
        in_specs=[pl.BlockSpec((2, S, tc), lambda j: (0, 0, j)), pl.BlockSpec((2, 3, tc), lambda j: (0, 0, j)),
                  pl.BlockSpec((2, 1, tc), lambda j: (0, 0, j))],
        out_specs=pl.BlockSpec((S, tc), lambda j: (0, j)),
        out_shape=jax.ShapeDtypeStruct((S, FF), BF16), compiler_params=_cp("parallel"))(a3, cw, cb)


def _conv_bwd(name, a3, cw, cb, dact, tc=128):
    _, S, FF = a3.shape
    tc = _tile(FF, tc)

    def body(a_ref, w_ref, b_ref, d_ref, da_ref, dw_ref, db_ref):
        row = lax.broadcasted_iota(jnp.int32, (S, tc), 0)
        ag, av = a_ref[0], a_ref[1]
        wg, wv = w_ref[0], w_ref[1]
        cg = _conv(ag, wg, b_ref[0], row)
        cv = _conv(av, wv, b_ref[1], row)
        d = d_ref[...]
        gate, gate_slope = _gelu_and_grad(cg)
        dcs = (d * cv * gate_slope, d * gate)
        for h, (dc, a, w) in enumerate(zip(dcs, (ag, av), (wg, wv))):
            da = _shift_up(dc, row) * w[0:1] + dc * w[1:2] + _shift_down(dc, row) * w[2:3]
            da_ref[h] = da.astype(da_ref.dtype)
            dw_ref[h, 0:1, :] = jnp.sum(dc * _shift_down(a, row), axis=0, keepdims=True)
            dw_ref[h, 1:2, :] = jnp.sum(dc * a, axis=0, keepdims=True)
            dw_ref[h, 2:3, :] = jnp.sum(dc * _shift_up(a, row), axis=0, keepdims=True)
            db_ref[h] = jnp.sum(dc, axis=0, keepdims=True)

    a_spec = pl.BlockSpec((2, S, tc), lambda j: (0, 0, j))
    w_spec = pl.BlockSpec((2, 3, tc), lambda j: (0, 0, j))
    b_spec = pl.BlockSpec((2, 1, tc), lambda j: (0, 0, j))
    return pl.pallas_call(
        body, name=name, grid=(FF // tc,),
        in_specs=[a_spec, w_spec, b_spec, pl.BlockSpec((S, tc), lambda j: (0, j))],
        out_specs=[a_spec, w_spec, b_spec],
        out_shape=[jax.ShapeDtypeStruct((2, S, FF), BF16), jax.ShapeDtypeStruct((2, 3, FF), F32),
                   jax.ShapeDtypeStruct((2, 1, FF), F32)],
        compiler_params=_cp("parallel"))(a3, cw, cb, dact)


_HBM = pl.BlockSpec(memory_space=pltpu.HBM)


def _position():
    return lax.axis_index("x"), lax.axis_index("y"), lax.axis_index("c")


_SEM =pl.BlockSpec(memory_space=pltpu.SEMAPHORE)
_EFFECT = pltpu.SideEffectType.DATAFLOW_SIDE_EFFECTING
_FLIPS = ((1, 0), (0, 1), (1, 1))


def _split_start(name, bufs, ncopy, plan, after=()):
    n = len(bufs)
    after = list(after)

    def body(*refs):
        ins = refs[:n]
        send_sems, recv_sems, token = refs[n + len(after)], refs[n + len(after) + 1], refs[-1]
        for i, (src, dst, to) in enumerate(plan(ins)):
            pltpu.make_async_remote_copy(src_ref=src, dst_ref=dst, send_sem=send_sems.at[i], recv_sem=recv_sems.at[i],
                                         device_id=to, device_id_type=MESH).start()
        token[...] = jnp.zeros_like(token)

    outs = pl.pallas_call(
        body, name=name,
        out_shape=(pltpu.SemaphoreType.DMA((ncopy,)), pltpu.SemaphoreType.DMA((ncopy,)),
                   *[pltpu.HBM(b.shape, b.dtype) for b in bufs], jax.ShapeDtypeStruct((8, 128), F32)),
        in_specs=[_HBM] * n + [pl.BlockSpec(memory_space=pl.ANY)] * len(after),
        out_specs=(_SEM, _SEM, *([_HBM] * n), pl.BlockSpec(memory_space=pltpu.VMEM)),
        input_output_aliases={i: 2 + i for i in range(n)},
        compiler_params=pltpu.CompilerParams(has_side_effects=_EFFECT),
    )(*[pltpu.with_memory_space_constraint(b, pltpu.HBM) for b in bufs], *after)
    return outs[0], outs[1], list(outs[2:2 + n]), outs[-1]


def _split_wait(name, bufs, send_sems, recv_sems, plan, after):
    n = len(bufs)
    after = list(after)

    def body(*refs):
        ins = refs[:n]
        ssem, rsem = refs[n], refs[n + 1]
        for i, (src, dst, to) in enumerate(plan(ins)):
            cp = pltpu.make_async_remote_copy(src_ref=src, dst_ref=dst, send_sem=ssem.at[i], recv_sem=rsem.at[i],
                                              device_id=to, device_id_type=MESH)
            cp.wait_send()
            cp.wait_recv()

    outs = pl.pallas_call(
        body, name=name, out_shape=tuple(pltpu.HBM(b.shape, b.dtype) for b in bufs),
        in_specs=[_HBM] * n + [_SEM, _SEM] + [pl.BlockSpec(memory_space=pl.ANY)] * len(after),
        out_specs=tuple([_HBM] * n), input_output_aliases={i: i for i in range(n)},
        compiler_params=pltpu.CompilerParams(has_side_effects=_EFFECT),
    )(*bufs, send_sems, recv_sems, *after)
    return list(outs)


def _gather_plan(refs):
    px, py, pc = _position()
    me = 4 * px + 2 * py + pc
    targets = [(px, py, 1 - pc), (1 - px, py, pc), (px, 1 - py, pc), (1 - px, 1 - py, pc)]
    return [(r.at[me], r.at[me], to) for r in refs for to in targets]


def _forward_plan(refs):
    px, py, pc = _position()
    out = []
    for r in refs:
        for fx, fy in _FLIPS:
            slot = 4 * (1 - px if fx else px) + 2 * (1 - py if fy else py) + pc
            out.append((r.at[slot], r.at[slot], (px, py, 1 - pc)))
    return out


def _pair_plan(n):
    def plan(refs):
        px, py, pc = _position()
        return [(refs[w].at[2 * k + (1 - pc)], refs[n + w].at[k], (px, py, 1 - pc)) for w in range(n) for k in range(4)]
    return plan


def _chip_plan(n):
    def plan(refs):
        px, py, pc = _position()
        out = []
        for w in range(n):
            for j, (fx, fy) in enumerate(_FLIPS):
                qx = 1 - px if fx else px
                qy = 1 - py if fy else py
                out.append((refs[w].at[2 * qx + qy], refs[n + w].at[j], (qx, qy, pc)))
        return out
    return plan


def _broadcast_plan(refs):
    px, py, pc = _position()
    me = 4 * px + 2 * py + pc
    flips = [(fx, fy, fc) for fx in (0, 1) for fy in (0, 1) for fc in (0, 1)][1:]
    targets = [(1 - px if fx else px, 1 - py if fy else py, 1 - pc if fc else pc) for fx, fy, fc in flips]
    return [(r.at[me], r.at[me], to) for r in refs for to in targets]


def _cast_place(name, dev, w, layer, dtype=BF16, deps=()):
    nl, R, C = w.shape
    tr = _row_tile(R, C, 4)

    def body(dev_ref, w_ref, *rest):
        o_ref = rest[-1]
        o_ref[...] = w_ref[...].astype(o_ref.dtype)

    return pl.pallas_call(
        body, name=name,
        grid_spec=pltpu.PrefetchScalarGridSpec(
            num_scalar_prefetch=1, grid=(R // tr,),
            in_specs=[pl.BlockSpec((None, tr, C), lambda i, d: (layer, i, 0))] + [_DEP] * len(deps),
            out_specs=pl.BlockSpec((None, tr, C), lambda i, d: (d[0], i, 0))),
        out_shape=jax.ShapeDtypeStruct((N_DEV, R, C), dtype), compiler_params=_cp("parallel"))(dev, w, *deps)


def _pair_sum(name, core, dw, recv):
    _, R, C = dw.shape
    tr = _row_tile(R, C, 4)
    dw4 = dw.reshape(4, 2, R, C)

    def body(core_ref, a_ref, b_ref, o_ref):
        o_ref[...] = (a_ref[...].astype(F32) + b_ref[...].astype(F32)).astype(o_ref.dtype)

    return pl.pallas_call(
        body, name=name,
        grid_spec=pltpu.PrefetchScalarGridSpec(
            num_scalar_prefetch=1, grid=(4, R // tr),
            in_specs=[pl.BlockSpec((None, None, tr, C), lambda k, i, c_ref: (k, c_ref[0], i, 0)),
                      pl.BlockSpec((None, tr, C), lambda k, i, c_ref: (k, i, 0))],
            out_specs=pl.BlockSpec((None, tr, C), lambda k, i, c_ref: (k, i, 0))),
        out_shape=jax.ShapeDtypeStruct((4, R, C), BF16),
        compiler_params=_cp("parallel", "parallel"))(core, dw4, recv)


def _adamw_math(w, g, m, v):
    m = ADAM_B1 * m + (1.0 - ADAM_B1) * g
    v = ADAM_B2 * v + (1.0 - ADAM_B2) * (g * g)
    m_hat = m / (1.0 - ADAM_B1 ** ADAM_STEP)
    v_hat = v / (1.0 - ADAM_B2 ** ADAM_STEP)
    delta = -ADAM_LR * (m_hat / (jnp.sqrt(v_hat) + ADAM_EPS) + ADAM_WD * w)
    return delta, m, v


def _adamw_shard(name, chip, layer, w, m, v, p, recv, prev, deps=()):
    nl, R, C = w.shape
    tr = _row_tile(R, C, 2)
    n_prev = 0 if prev is None else 4

    def body(chip_ref, w_ref, m_ref, v_ref, p_ref, r_ref, *rest):
        g_ref, d_ref, nm_ref, nv_ref, tok_ref = rest[-5:]
        tok_ref[...] = jnp.zeros_like(tok_ref)
        g = p_ref[...].astype(F32)
        for j in range(3):
            g = g + r_ref[j].astype(F32)
        delta, nm, nv = _adamw_math(w_ref[...], g, m_ref[...], v_ref[...])
        g_ref[...] = g
        d_ref[...] = delta
        nm_ref[...] = nm
        nv_ref[...] = nv

    lay = pl.BlockSpec((None, tr, C), lambda i, c_ref: (layer, i, 0))
    in_specs = [lay, lay, lay,
                pl.BlockSpec((None, tr, C), lambda i, c_ref: (c_ref[0], i, 0)),
                pl.BlockSpec((3, tr, C), lambda i, c_ref: (0, i, 0))]
    in_specs += [pl.BlockSpec(memory_space=pl.ANY)] * n_prev + [_DEP] * len(deps)
    shape = jax.ShapeDtypeStruct((nl, R, C), F32)
    ins = [chip, w, m, v, p, recv] + ([] if prev is None else list(prev)) + list(deps)
    return pl.pallas_call(
        body, name=name,
        grid_spec=pltpu.PrefetchScalarGridSpec(
            num_scalar_prefetch=1, grid=(R // tr,), in_specs=in_specs, out_specs=[lay] * 4 + [_DEP]),
        out_shape=[shape] * 4 + [jax.ShapeDtypeStruct((8, 128), F32)],
        input_output_aliases={6 + j: j for j in range(n_prev)},
        compiler_params=_cp("arbitrary"))(*ins)


def _sum_slots(name, parts, tr=512):
    n, R, C = parts.shape
    tr = _tile(R, tr)

    def body(p_ref, o_ref):
        acc = p_ref[0]
        for j in range(1, n):
            acc = acc + p_ref[j]
        o_ref[...] = acc

    return pl.pallas_call(
        body, name=name, grid=(R // tr,),
        in_specs=[pl.BlockSpec((n, tr, C), lambda i: (0, i, 0))],
        out_specs=pl.BlockSpec((tr, C), lambda i: (i, 0)),
        out_shape=jax.ShapeDtypeStruct((R, C), F32), compiler_params=_cp("parallel"))(parts)


def _adamw_flat(name, w, g, m, v, tr=512):
    R, C = w.shape
    tr = _tile(R, tr)

    def body(w_ref, g_ref, m_ref, v_ref, d_ref, nm_ref, nv_ref):
        delta, nm, nv = _adamw_math(w_ref[...], g_ref[...], m_ref[...], v_ref[...])
        d_ref[...] = delta
        nm_ref[...] = nm
        nv_ref[...] = nv

    row = pl.BlockSpec((tr, C), lambda i: (i, 0))
    shape = jax.ShapeDtypeStruct((R, C), F32)
    return pl.pallas_call(
        body, name=name, grid=(R // tr,), in_specs=[row] * 4, out_specs=[row] * 3, out_shape=[shape] * 3,
        compiler_params=_cp("parallel"))(w, g, m, v)


_PACK_ROWS = 512


def _pack(arrs):
    flat = jnp.concatenate([a.reshape(-1) for a in arrs])
    unit = _PACK_ROWS * 128
    pad = (-flat.shape[0]) % unit
    return jnp.pad(flat, (0, pad)).reshape(-1, 128)


def _unpack(packed, shapes):
    flat = packed.reshape(-1)
    outs, off = [], 0
    for s in shapes:
        n = int(np.prod(s))
        outs.append(flat[off:off + n].reshape(s))
        off += n
    return outs


def kernel(x, mem, mix_norm_g, ffn_norm_g, mem_norm_g, w_mem_kv, a_w_in, a_w_out, b_w_in, b_v_norm_g, b_w_s, b_s_bias, b_w_out, ffn_w_up, ffn_conv_w, ffn_conv_b, ffn_w_down, final_norm_g, loss_target, m_mix_norm_g, m_ffn_norm_g, m_mem_norm_g, m_w_mem_kv, m_a_w_in, m_a_w_out, m_b_w_in, m_b_v_norm_g, m_b_w_s, m_b_s_bias, m_b_w_out, m_ffn_w_up, m_ffn_conv_w, m_ffn_conv_b, m_ffn_w_down, m_final_norm_g, v_mix_norm_g, v_ffn_norm_g, v_mem_norm_g, v_w_mem_kv, v_a_w_in, v_a_w_out, v_b_w_in, v_b_v_norm_g, v_b_w_s, v_b_s_bias, v_b_w_out, v_ffn_w_up, v_ffn_conv_w, v_ffn_conv_b, v_ffn_w_down, v_final_norm_g):
    px, py, pc = _position()
    dev = 4 * px + 2 * py + pc
    core = jnp.reshape(pc, (1,)).astype(jnp.int32)
    chip = jnp.reshape(2 * px + py, (1,)).astype(jnp.int32)

    x0 = x[0]
    mem0 = mem[0]
    tgt = loss_target[0]
    S, D = x0.shape
    depth = mix_norm_g.shape[0]
    FF = ffn_w_down.shape[1] * N_DEV
    a_in = a_w_in.shape[2] * N_DEV
    b_in = b_w_in.shape[2] * N_DEV
    a_q_blk = (a_in - GW) // GW
    b_q_blk = (b_in - GW) // GW

    stacks = {"kv": (w_mem_kv, m_w_mem_kv, v_w_mem_kv), "ain": (a_w_in, m_a_w_in, v_a_w_in),
              "aout": (a_w_out, m_a_w_out, v_a_w_out), "bin": (b_w_in, m_b_w_in, v_b_w_in),
              "bout": (b_w_out, m_b_w_out, v_b_w_out), "up": (ffn_w_up, m_ffn_w_up, v_ffn_w_up),
              "down": (ffn_w_down, m_ffn_w_down, v_ffn_w_down)}
    dev1 = jnp.reshape(dev, (1,)).astype(jnp.int32)

    def groups_of(i):
        j = i // 2
        mix = [("kv", i), ("ain", j), ("aout", j)] if i % 2 == 0 else [("kv", i), ("bin", j), ("bout", j)]
        return mix, [("up", i), ("down", i)]

    gather_groups = [(f"{half}{i}", members) for i in range(depth) for half, members in zip("mf", groups_of(i))]
    gather_ahead = 2
    in_flight = {}

    def gather_start(k, after):
        gname, members = gather_groups[k]
        deps = [start_tokens[-1]] if k else []
        lands = [_cast_place(f"place_{t}{l}", dev1, stacks[t][0], l, deps=deps) for t, l in members]
        ssem, rsem, lands, tok = _split_start(f"ag_start_{gname}", lands, 4 * len(lands), _gather_plan, after)
        in_flight[k] = (lands, ssem, rsem)
        return tok

    small_land = _cast_place("place_small_w", dev1, _pack([ffn_conv_w, b_v_norm_g])[None], 0, F32)
    small_ssem, small_rsem, small_lands, small_tok = _split_start("smallw_start", [small_land], N_DEV - 1,
                                                                  _broadcast_plan)
    start_tokens = [small_tok, gather_start(0, [small_tok])]
    passing = {}

    def gather_arrive(k, after):
        if k >= len(gather_groups):
            return []
        gname, members = gather_groups[k]
        lands, ssem, rsem = in_flight.pop(k)
        lands = _split_wait(f"ag_wait_{gname}", lands, ssem, rsem, _gather_plan, after)
        toks = []
        for q in (range(1, 1 + gather_ahead) if k == 0 else [k + gather_ahead]):
            if q < len(gather_groups):
                toks.append(gather_start(q, [lands[0]] + toks))
        ssem, rsem, lands, tok = _split_start(f"ag_pass_{gname}", lands, 3 * len(lands), _forward_plan, toks)
        passing[k] = (lands, ssem, rsem)
        return toks + [tok]

    def gather_ready(k, after):
        gname, members = gather_groups[k]
        lands, ssem, rsem = passing.pop(k)
        lands = _split_wait(f"ag_ready_{gname}", lands, ssem, rsem, _forward_plan, after)
        out = {}
        for (t, l), land in zip(members, lands):
            if t == "kv":
                out["kv"] = land.reshape(D, 2 * GW)
            elif t in ("ain", "aout", "up"):
                out[{"ain": "in", "aout": "out", "up": "up"}[t]] = land
            elif t == "bin":
                out["in"] = jnp.transpose(land, (1, 0, 2)).reshape(D, b_in)
            elif t == "bout":
                out["out"] = land.reshape(B_W + GW, D)
            else:
                out["down"] = land.reshape(FF, D)
        return out

    def small_weights(after):
        (small_all,) = _split_wait("smallw_wait", small_lands, small_ssem, small_rsem, _broadcast_plan, after)
        cw_parts, gv_parts = [], []
        for d in range(N_DEV):
            cw_d, gv_d = _unpack(small_all[d], [ffn_conv_w.shape, b_v_norm_g.shape])
            cw_parts.append(cw_d)
            gv_parts.append(gv_d)
        return jnp.concatenate(cw_parts, axis=-1), jnp.concatenate(gv_parts, axis=-1)

    def conv_params(i):
        cw = conv_w_full[i].reshape(3, 2, FF).transpose(1, 0, 2)
        cb = ffn_conv_b[i].reshape(2, 1, FF)
        return cw, cb

    saved = []
    W = []
    xc = x0
    toks = start_tokens + gather_arrive(0, [x0])
    for i in range(depth):
        j = i // 2
        lw = gather_ready(2 * i, [xc])
        sv = {"x0": xc}
        h1, h1t = _rms_fwd(f"mixnorm{i}", xc, mix_norm_g[i], deps=toks, with_t=True)
        memn = _rms_fwd(f"memnorm{i}", mem0, mem_norm_g[i])
        if i % 2 == 0:
            proj = _mm_gcols(f"ain{i}", h1, lw["in"], out_dtype=BF16)
        else:
            proj = _mm_full(f"bin{i}", h1, lw["in"])
        kv = _mm_full(f"kvproj{i}", memn, lw["kv"])
        if i % 2 == 0:
            outs, lses = [], []
            for g in range(len(A_PATTERNS)):
                o, l = _attn_fwd(f"attn{i}_{g}", proj, g)
                outs.append(o)
                lses.append(l)
            tok, lse, tok16 = _attn_merge(f"merge{i}", outs, lses)
            sv.update(tok=tok, lse=lse)
        else:
            tok16 = _sgu_fwd(f"sgu{i}", proj, gv_full[j], b_w_s[j], b_s_bias[j].T)
        toks = gather_arrive(2 * i + 1, [tok16]) if i > 0 else []
        mo = _mem_fwd(f"memattn{i}", proj, a_q_blk if i % 2 == 0 else b_q_blk, kv)
        cat = jnp.concatenate([tok16, mo], axis=1)
        if i % 2 == 0:
            x1 = _mm_gcols(f"aout{i}", cat, lw["out"], res=xc, deps=toks)
        else:
            x1 = _mm_full(f"bout{i}", cat, lw["out"], res=xc, deps=toks)
        toks = gather_arrive(1, [x1]) if i == 0 else []
        lw.update(gather_ready(2 * i + 1, [x1]))
        W.append(lw)
        if i == 0:
            conv_w_full, gv_full = small_weights([x1])
        h2, h2t = _rms_fwd(f"ffnnorm{i}", x1, ffn_norm_g[i], deps=toks, with_t=True)
        cw, cb = conv_params(i)
        a3 = _mm_gcols(f"up{i}", h2, lw["up"], split_out=True)
        toks = gather_arrive(2 * i + 2, [a3])
        act = _conv_fwd(f"conv{i}", a3, cw, cb)
        x2 = _mm_full(f"down{i}", act, lw["down"], res=x1, tn=512, tk=FF // 2, deps=toks)
        toks = []
        sv.update(h1t=h1t, memn=memn, kv=kv, proj=proj, cat=cat, x1=x1, h2t=h2t, a3=a3, act=act)
        saved.append(sv)
        xc = x2

    dx, dg_final, sq, dx16 = _final("final", xc, tgt, final_norm_g)
    loss_local = sq[0, 0] * (0.5 / D)

    chain = {}
    adamw_tokens = []

    def pair_begin(gname, members, dws):
        n = len(dws)
        recvs = [lax.empty((4,) + dw.shape[1:], dw.dtype) for dw in dws]
        ssem, rsem, bufs, tok = _split_start(f"rs_pair_start_{gname}", dws + recvs, 4 * n, _pair_plan(n))
        return dict(name=gname, members=members, n=n, bufs=bufs, sems=(ssem, rsem)), tok

    def pair_end_chip_begin(st, after):
        n, gname = st["n"], st["name"]
        bufs = _split_wait(f"rs_pair_wait_{gname}", st["bufs"], *st["sems"], _pair_plan(n), after)
        ps = [_pair_sum(f"rs_sum_{t}{l}", core, bufs[w], bufs[n + w]) for w, (t, l) in enumerate(st["members"])]
        recvs = [lax.empty((3,) + p.shape[1:], BF16) for p in ps]
        ssem, rsem, bufs, tok = _split_start(f"rs_chip_start_{gname}", ps + recvs, 3 * n, _chip_plan(n))
        return dict(name=gname, members=st["members"], n=n, bufs=bufs, sems=(ssem, rsem)), tok

    def chip_end_update(st, after, deps=()):
        n = st["n"]
        bufs = _split_wait(f"rs_chip_wait_{st['name']}", st["bufs"], *st["sems"], _chip_plan(n), after)
        for w, (t, l) in enumerate(st["members"]):
            wst, mst, vst = stacks[t]
            *chain[t], tok = _adamw_shard(f"adamw_{t}{l}", chip, l, wst, mst, vst, bufs[w], bufs[n + w], chain.get(t),
                                          deps)
            adamw_tokens.append(tok)

    pipe = {"pair": [], "chip": [], "deps": []}

    def take_deps():
        deps, pipe["deps"] = pipe["deps"], []
        return deps

    def submit(gname, members, dws):
        st, tok = pair_begin(gname, members, dws)
        pipe["pair"].append(st)
        pipe["deps"].append(tok)

    def advance(after):
        arrived, pipe["chip"] = pipe["chip"], []
        toks = []
        for st in pipe["pair"]:
            new, tok = pair_end_chip_begin(st, [after])
            pipe["chip"].append(new)
            toks.append(tok)
        pipe["pair"] = []
        done = len(adamw_tokens)
        for st in arrived:
            chip_end_update(st, [after], toks)
        pipe["deps"] += toks + adamw_tokens[done:]

    def small_start(tag, arrs, after):
        land = _cast_place(f"place_small_{tag}", dev1, _pack(arrs)[None], 0, F32)
        ssem, rsem, lands, tok = _split_start(f"small_start_{tag}", [land], N_DEV - 1, _broadcast_plan, after)
        return (lands, ssem, rsem), tok

    def small_end(tag, state, shapes, after):
        lands, ssem, rsem = state
        lands = _split_wait(f"small_wait_{tag}", lands, ssem, rsem, _broadcast_plan, after)
        return _unpack(_sum_slots(f"small_sum_{tag}", lands[0]), shapes)

    def late_small():
        return [dg_mix[0], dg_ffn[0], dg_mem[0], d_conv_b[0][None], d_conv_w[0][None]]

    assert depth >= 2
    big = {k: [None] * n for k, n in (("kv", depth), ("ain", depth // 2 + depth % 2), ("aout", depth // 2 + depth % 2),
                                      ("bin", depth // 2), ("bout", depth // 2), ("up", depth), ("down", depth))}
    dg_mix, dg_ffn, dg_mem = [None] * depth, [None] * depth, [None] * depth
    d_conv_w, d_conv_b = [None] * depth, [None] * depth
    d_gv, d_ws, d_sb = [None] * (depth // 2), [None] * (depth // 2), [None] * (depth // 2)
    for i in reversed(range(depth)):
        j = i // 2
        lw, sv = W[i], saved[i]
        cw, cb = conv_params(i)
        mix_members, ffn_members = groups_of(i)
        if i == 0:
            early_arrays = [loss_local.reshape(1), dg_final.reshape(D), jnp.concatenate(dg_mix[1:]),
                            jnp.concatenate(dg_ffn[1:]), jnp.concatenate(dg_mem[1:]), jnp.stack(d_ws), jnp.stack(d_sb),
                            jnp.stack(d_gv), jnp.stack(d_conv_b[1:]), jnp.stack(d_conv_w[1:])]
            early_state, tok = small_start("early", early_arrays, [dx])
            pipe["deps"].append(tok)
        deps = take_deps()
        dact = _mm_dx_full(f"ddown{i}", dx16, lw["down"], tm=1024, tko=FF // 4, tc=D, deps=deps)
        big["down"][i] = _mm_dw(f"wdown{i}", sv["act"], dx16, deps=deps).reshape(N_DEV, FF // N_DEV, D)
        da3, dcw, dcb = _conv_bwd(f"dconv{i}", sv["a3"], cw, cb, dact)
        d_conv_w[i] = dcw.transpose(1, 0, 2).reshape(3, 2 * FF)
        d_conv_b[i] = dcb.reshape(2 * FF)
        advance(da3)
        deps = take_deps()
        dh2 = _mm_dx_gcols(f"dup{i}", da3, lw["up"], split_in=True, deps=deps, nsub=2)
        big["up"][i] = _mm_dw_gcols(f"wup{i}", sv["h2t"], da3, N_DEV, split_in=True, deps=deps, a_t=True)
        dx1, dg_ffn[i], dx1_16 = _rms_bwd(f"dffnnorm{i}", dh2, sv["x1"], ffn_norm_g[i], dx)
        submit(f"f{i}", ffn_members, [big["up"][i], big["down"][i]])
        deps = take_deps()
        if i % 2 == 0:
            dcat = _mm_dx_gcols(f"daout{i}", dx1_16, lw["out"], deps=deps, nsub=N_DEV)
            big["aout"][j] = _mm_dw_gcols(f"waout{i}", sv["cat"], dx1_16, N_DEV, deps=deps)
            advance(dcat)
            dqm, dkv = _mem_bwd(f"dmemattn{i}", sv["proj"], a_q_blk, sv["kv"], dcat, 1, deps=take_deps())
            parts = [None] * 9
            for g in range(len(A_PATTERNS)):
                dq, dk, dv = _attn_bwd(f"dattn{i}_{g}", sv["proj"], g, dcat, 0, sv["tok"], sv["lse"])
                parts[g], parts[3 + g], parts[6 + g] = dq, dk, dv
            dproj = jnp.concatenate(parts + [dqm], axis=1)
            deps = []
            dh1 = _mm_dx_gcols(f"dain{i}", dproj, lw["in"], nsub=4)
        else:
            dcat = _mm_dx_full(f"dbout{i}", dx1_16, lw["out"], tm=1024, tko=1024, deps=deps)
            big["bout"][j] = _mm_dw(f"wbout{i}", sv["cat"], dx1_16, deps=deps).reshape(
                N_DEV, (B_W + GW) // N_DEV, D)
            advance(dcat)
            dqm, dkv = _mem_bwd(f"dmemattn{i}", sv["proj"], b_q_blk, sv["kv"], dcat, B_W // GW, deps=take_deps())
            bias_t = b_s_bias[j].T
            du, dvp, dgv, dws, dbt = _sgu_bwd(f"dsgu{i}", sv["proj"], gv_full[j], b_w_s[j], bias_t, dcat)
            d_gv[j], d_ws[j], d_sb[j] = dgv.reshape(B_W), dws, dbt.T
            dproj = jnp.concatenate([du, dvp, dqm], axis=1)
            deps = []
            dh1 = _mm_dx_full(f"dbin{i}", dproj, lw["in"], tm=1024, tko=1024, tc=b_in)
        dmemn = _mm_dx_full(f"dkvproj{i}", dkv, lw["kv"], tko=1024)
        _, dg_mem[i], _ = _rms_bwd(f"dmemnorm{i}", dmemn, mem0, mem_norm_g[i])
        dx, dg_mix[i], dx16 = _rms_bwd(f"dmixnorm{i}", dh1, sv["x0"], mix_norm_g[i], dx1)
        if i == 0:
            late_state, late_tok = small_start("late", late_small(), [dx])
            deps = deps + [late_tok]
        if i % 2 == 0:
            big["ain"][j] = _mm_dw_gcols(f"wain{i}", sv["h1t"], dproj, N_DEV, deps=deps, a_t=True)
        else:
            dwin = _mm_dw(f"wbin{i}", sv["h1t"], dproj, tko=1024, tn=512, deps=deps, a_t=True)
            big["bin"][j] = dwin.reshape(D, N_DEV, b_in // N_DEV).transpose(1, 0, 2)
        big["kv"][i] = _mm_dw(f"wkv{i}", sv["memn"], dkv, tko=1024, deps=deps).reshape(N_DEV, D // N_DEV, 2 * GW)
        submit(f"m{i}", mix_members, [big[t][l] for t, l in mix_members])
    grad_x = dx[None]

    last_chips, toks = [], []
    for st in pipe["pair"]:
        new, tok = pair_end_chip_begin(st, [dx])
        last_chips.append(new)
        toks.append(tok)
    g_early = small_end("early", early_state, [a.shape for a in early_arrays], [dx])
    for st in pipe["chip"]:
        chip_end_update(st, [g_early[1]], toks)
    g_late = small_end("late", late_state, [a.shape for a in late_small()], list(adamw_tokens))

    loss = g_early[0][0]
    layer0 = dict(zip(("mix", "ffn", "mem", "conv_b", "conv_w"), g_late))
    rest = dict(zip(("final", "mix", "ffn", "mem", "w_s", "s_bias", "gv", "conv_b", "conv_w"), g_early[1:]))
    g_cw_full = jnp.concatenate([layer0["conv_w"], rest["conv_w"]])
    g_gv = lax.dynamic_slice_in_dim(rest["gv"], dev * b_v_norm_g.shape[1], b_v_norm_g.shape[1], axis=1)
    g_cw = lax.dynamic_slice_in_dim(g_cw_full, dev * ffn_conv_w.shape[2], ffn_conv_w.shape[2], axis=2)
    g_all = [jnp.concatenate([layer0["mix"], rest["mix"]]), jnp.concatenate([layer0["ffn"], rest["ffn"]]),
             jnp.concatenate([layer0["mem"], rest["mem"]]), rest["w_s"], rest["s_bias"],
             jnp.concatenate([layer0["conv_b"], rest["conv_b"]]), rest["final"], g_gv, g_cw]
    names = ["mix_norm_g", "ffn_norm_g", "mem_norm_g", "b_w_s", "b_s_bias", "ffn_conv_b", "final_norm_g",
             "b_v_norm_g", "ffn_conv_w"]
    ws = [mix_norm_g, ffn_norm_g, mem_norm_g, b_w_s, b_s_bias, ffn_conv_b, final_norm_g, b_v_norm_g, ffn_conv_w]
    ms = [m_mix_norm_g, m_ffn_norm_g, m_mem_norm_g, m_b_w_s, m_b_s_bias, m_ffn_conv_b, m_final_norm_g,
          m_b_v_norm_g, m_ffn_conv_w]
    vs = [v_mix_norm_g, v_ffn_norm_g, v_mem_norm_g, v_b_w_s, v_b_s_bias, v_ffn_conv_b, v_final_norm_g,
          v_b_v_norm_g, v_ffn_conv_w]
    shapes = [w.shape for w in ws]
    d_p, m_p, v_p = _adamw_flat("adamw_small", _pack(ws), _pack(g_all), _pack(ms), _pack(vs))
    res = {}
    for n, g, d, nm, nv in zip(names, g_all, _unpack(d_p, shapes), _unpack(m_p, shapes), _unpack(v_p, shapes)):
        res[n] = [g, d, nm, nv]
    for st in last_chips:
        chip_end_update(st, [d_p] + list(adamw_tokens))
    for tag, name in (("kv", "w_mem_kv"), ("ain", "a_w_in"), ("aout", "a_w_out"), ("bin", "b_w_in"),
                      ("bout", "b_w_out"), ("up", "ffn_w_up"), ("down", "ffn_w_down")):
        res[name] = list(chain[tag])

    order = ["mix_norm_g", "ffn_norm_g", "mem_norm_g", "w_mem_kv", "a_w_in", "a_w_out", "b_w_in", "b_v_norm_g",
             "b_w_s", "b_s_bias", "b_w_out", "ffn_w_up", "ffn_conv_w", "ffn_conv_b", "ffn_w_down", "final_norm_g"]
    return (loss, grad_x, *[res[n][0] for n in order], *[res[n][1] for n in order],
            *[res[n][2] for n in order], *[res[n][3] for n in order])
```

```python
import functools

import numpy as np
import jax
import jax.numpy as jnp
from jax import lax
from jax.experimental import pallas as pl
from jax.experimental.pallas import tpu as pltpu

F32 = jnp.float32
BF16 = jnp.bfloat16
MESH = pl.DeviceIdType.MESH
N_DEV = 8

EPS = 1e-6
NEG = -1e30
HEAD = 128
HPG = 4
GW = HPG * HEAD
A_PATTERNS = ((128, 1), (512, 4), (2048, 16))
A_HEADS = HPG * len(A_PATTERNS)
QBLK = 128
B_GROUPS = 12
B_W = B_GROUPS * HEAD
SLOPES = (2.0 ** (-8.0 * (np.arange(A_HEADS) + 1) / A_HEADS)).astype(np.float32)
SCALE = HEAD ** -0.5

ADAM_LR = 0.001
ADAM_B1 = 0.9
ADAM_B2 = 0.999
ADAM_EPS = 1e-08
ADAM_WD = 0.01
ADAM_STEP = 10

V7X_VMEM_LIMIT = 50 * 1024 * 1024

NN = (((1,), (0,)), ((), ()))
NT = (((1,), (1,)), ((), ()))
TN = (((0,), (0,)), ((), ()))


def _cp(*sem):
    return pltpu.CompilerParams(dimension_semantics=sem, vmem_limit_bytes=V7X_VMEM_LIMIT)


def _dot(a, b, dims=NN):
    return lax.dot_general(a.astype(BF16), b.astype(BF16), dims, preferred_element_type=F32)


def _tile(n, pref):
    t = min(n, pref)
    assert n % t == 0, (n, pref)
    return t


def _row_tile(rows, cols, mib=1):
    best = None
    for t in range(16, rows + 1, 16):
        if rows % t == 0 and t * cols * 4 <= (mib << 20):
            best = t
    if best is None:
        best = rows
    return best


_DEP = pl.BlockSpec((8, 128), lambda *_: (0, 0))


def _matmul(name, dims, grid, a, a_spec, b, b_spec, out_shape, o_spec, tile, res=None, res_spec=None, deps=(),
            nsub=1):
    nk = grid[2]
    has_res = res is not None

    def body(*refs):
        a_ref, b_ref = refs[0], refs[1]
        r_ref = refs[2] if has_res else None
        o_ref, acc_ref = refs[-2], refs[-1]
        if nsub == 1:
            part = _dot(a_ref[...], b_ref[...], dims)
        else:
            w = a_ref.shape[-1] // nsub
            part = _dot(a_ref[:, :w], b_ref[0], dims)
            for q in range(1, nsub):
                part = part + _dot(a_ref[:, q * w:(q + 1) * w], b_ref[q], dims)

        def finish(val):
            if has_res:
                val = val + r_ref[...]
            o_ref[...] = val.astype(o_ref.dtype)

        if nk == 1:
            finish(part)
        else:
            k = pl.program_id(2)

            @pl.when(k == 0)
            def _():
                acc_ref[...] = part

            @pl.when(k > 0)
            def _():
                acc_ref[...] += part

            @pl.when(k == nk - 1)
            def _():
                finish(acc_ref[...])

    ins = [a, b] + ([res] if has_res else []) + list(deps)
    specs = [a_spec, b_spec] + ([res_spec] if has_res else []) + [_DEP] * len(deps)
    return pl.pallas_call(
        body, name=name, grid=grid, in_specs=specs, out_specs=o_spec, out_shape=out_shape,
        scratch_shapes=[pltpu.VMEM(tile if nk > 1 else (8, 128), F32)],
        compiler_params=_cp("parallel", "parallel", "arbitrary"))(*ins)


def _mm_full(name, a, w, res=None, tm=1024, tn=512, tk=2048, deps=()):
    M, K = a.shape
    N = w.shape[1]
    tm, tn, tk = _tile(M, tm), _tile(N, tn), _tile(K, tk)
    return _matmul(
        name, NN, (N // tn, M // tm, K // tk),
        a, pl.BlockSpec((tm, tk), lambda j, i, k: (i, k)),
        w, pl.BlockSpec((tk, tn), lambda j, i, k: (k, j)),
        jax.ShapeDtypeStruct((M, N), F32), pl.BlockSpec((tm, tn), lambda j, i, k: (i, j)), (tm, tn),
        res, pl.BlockSpec((tm, tn), lambda j, i, k: (i, j)), deps=deps)


def _mm_gcols(name, a, wg, res=None, split_out=False, tm=1024, deps=(), out_dtype=F32):
    M, K = a.shape
    G, _, Nl = wg.shape
    tm = _tile(M, tm)
    hg = G // 2
    if split_out:
        shape = jax.ShapeDtypeStruct((2, M, hg * Nl), out_dtype)
        o_spec = pl.BlockSpec((None, tm, Nl), lambda g, i, k: (g // hg, i, g % hg))
    else:
        shape = jax.ShapeDtypeStruct((M, G * Nl), out_dtype)
        o_spec = pl.BlockSpec((tm, Nl), lambda g, i, k: (i, g))
    return _matmul(
        name, NN, (G, M // tm, 1),
        a, pl.BlockSpec((tm, K), lambda g, i, k: (i, 0)),
        wg, pl.BlockSpec((None, K, Nl), lambda g, i, k: (g, 0, 0)),
        shape, o_spec, (tm, Nl),
        res, pl.BlockSpec((tm, Nl), lambda g, i, k: (i, g)), deps=deps)


def _mm_dx_full(name, dy, w, tm=512, tko=512, tc=2048, deps=()):
    M, N = dy.shape
    K = w.shape[0]
    tm, tko, tc = _tile(M, tm), _tile(K, tko), _tile(N, tc)
    return _matmul(
        name, NT, (K // tko, M // tm, N // tc),
        dy, pl.BlockSpec((tm, tc), lambda j, i, k: (i, k)),
        w, pl.BlockSpec((tko, tc), lambda j, i, k: (j, k)),
        jax.ShapeDtypeStruct((M, K), F32), pl.BlockSpec((tm, tko), lambda j, i, k: (i, j)), (tm, tko), deps=deps)


def _mm_dx_gcols(name, dy, wg, split_in=False, tm=1024, tko=1024, deps=(), nsub=1):
    G, K, Nl = wg.shape
    M = dy.shape[-2]
    tm, tko = _tile(M, tm), _tile(K, tko)
    hs = G // 2 // nsub if split_in else None
    if split_in:
        dy_spec = pl.BlockSpec((None, tm, nsub * Nl), lambda j, i, g: (g // hs, i, g % hs))
    else:
        dy_spec = pl.BlockSpec((tm, nsub * Nl), lambda j, i, g: (i, g))
    w_block = (None, tko, Nl) if nsub == 1 else (nsub, tko, Nl)
    return _matmul(
        name, NT, (K // tko, M // tm, G // nsub),
        dy, dy_spec,
        wg, pl.BlockSpec(w_block, lambda j, i, g: (g, j, 0)),
        jax.ShapeDtypeStruct((M, K), F32), pl.BlockSpec((tm, tko), lambda j, i, g: (i, j)), (tm, tko), deps=deps,
        nsub=nsub)


def _lhs_of_dw(a_t, ts, tko, index):
    if a_t:
        return NN, pl.BlockSpec((tko, ts), lambda *ids: index(*ids))
    return TN, pl.BlockSpec((ts, tko), lambda *ids: index(*ids)[::-1])


def _mm_dw(name, a, dy, tko=512, tn=1024, ts=2048, deps=(), a_t=False):
    K1, S = a.shape if a_t else a.shape[::-1]
    N = dy.shape[1]
    tko, tn, ts = _tile(K1, tko), _tile(N, tn), _tile(S, ts)
    dims, a_spec = _lhs_of_dw(a_t, ts, tko, lambda i, j, k: (j, k))
    return _matmul(
        name, dims, (N // tn, K1 // tko, S // ts),
        a, a_spec,
        dy, pl.BlockSpec((ts, tn), lambda i, j, k: (k, i)),
        jax.ShapeDtypeStruct((K1, N), BF16), pl.BlockSpec((tko, tn), lambda i, j, k: (j, i)), (tko, tn), deps=deps)


def _mm_dw_gcols(name, a, dy, G, split_in=False, tko=1024, ts=2048, deps=(), a_t=False):
    K1, S = a.shape if a_t else a.shape[::-1]
    Nl = (dy.shape[-1] * (2 if split_in else 1)) // G
    tko, ts = _tile(K1, tko), _tile(S, ts)
    hg = G // 2
    dims, a_spec = _lhs_of_dw(a_t, ts, tko, lambda g, j, k: (j, k))
    if split_in:
        dy_spec = pl.BlockSpec((None, ts, Nl), lambda g, j, k: (g // hg, k, g % hg))
    else:
        dy_spec = pl.BlockSpec((ts, Nl), lambda g, j, k: (k, g))
    return _matmul(
        name, dims, (G, K1 // tko, S // ts),
        a, a_spec,
        dy, dy_spec,
        jax.ShapeDtypeStruct((G, K1, Nl), BF16), pl.BlockSpec((None, tko, Nl), lambda g, j, k: (g, j, 0)),
        (tko, Nl), deps=deps)


def _rms_fwd(name, x, g, tr=256, deps=(), with_t=False):
    S, D = x.shape
    tr = _tile(S, tr)

    def body(x_ref, g_ref, *rest):
        xf = x_ref[...]
        r = lax.rsqrt(jnp.mean(xf * xf, axis=-1, keepdims=True) + EPS)
        y = xf * r * g_ref[...]
        if with_t:
            rest[-2][...] = y.astype(BF16)
            rest[-1][...] = y.T.astype(BF16)
        else:
            rest[-1][...] = y.astype(BF16)

    row = pl.BlockSpec((tr, D), lambda i: (i, 0))
    out_specs, out_shape = row, jax.ShapeDtypeStruct((S, D), BF16)
    if with_t:
        out_specs = [row, pl.BlockSpec((D, tr), lambda i: (0, i))]
        out_shape = [out_shape, jax.ShapeDtypeStruct((D, S), BF16)]
    return pl.pallas_call(
        body, name=name, grid=(S // tr,),
        in_specs=[row, pl.BlockSpec((1, D), lambda i: (0, 0))] + [_DEP] * len(deps),
        out_specs=out_specs, out_shape=out_shape, compiler_params=_cp("parallel"))(x, g.reshape(1, D), *deps)


def _rms_bwd(name, dh, x, g, dres=None, tr=256):
    S, D = x.shape
    tr = _tile(S, tr)
    has_res = dres is not None

    def body(*refs):
        dh_ref, x_ref, g_ref = refs[:3]
        dres_ref = refs[3] if has_res else None
        dx_ref, dg_ref, dx16_ref = refs[-3:]
        xf = x_ref[...]
        r = lax.rsqrt(jnp.mean(xf * xf, axis=-1, keepdims=True) + EPS)
        xh = xf * r
        dhv = dh_ref[...]
        dxh = dhv * g_ref[...]
        dx = r * (dxh - xh * jnp.mean(dxh * xh, axis=-1, keepdims=True))
        if has_res:
            dx = dx + dres_ref[...]
        dx_ref[...] = dx
        dx16_ref[...] = dx.astype(BF16)
        part = jnp.sum(dhv * xh, axis=0, keepdims=True)
        i = pl.program_id(0)

        @pl.when(i == 0)
        def _():
            dg_ref[...] = part

        @pl.when(i > 0)
        def _():
            dg_ref[...] += part

    row = pl.BlockSpec((tr, D), lambda i: (i, 0))
    vec = pl.BlockSpec((1, D), lambda i: (0, 0))
    ins = [dh, x, g.reshape(1, D)] + ([dres] if has_res else [])
    return pl.pallas_call(
        body, name=name, grid=(S // tr,),
        in_specs=[row, row, vec] + ([row] if has_res else []),
        out_specs=[row, vec, row],
        out_shape=[jax.ShapeDtypeStruct((S, D), F32), jax.ShapeDtypeStruct((1, D), F32),
                   jax.ShapeDtypeStruct((S, D), BF16)],
        compiler_params=_cp("arbitrary"))(*ins)


def _final(name, x, tgt, g, tr=256):
    S, D = x.shape
    tr = _tile(S, tr)

    def body(x_ref, t_ref, g_ref, dx_ref, dg_ref, loss_ref, dx16_ref):
        xf = x_ref[...]
        gv = g_ref[...]
        r = lax.rsqrt(jnp.mean(xf * xf, axis=-1, keepdims=True) + EPS)
        xh = xf * r
        err = xh * gv - t_ref[...]
        sq = jnp.sum(jnp.sum(err * err, axis=1, keepdims=True), axis=0, keepdims=True)
        dy = err * (1.0 / D)
        dxh = dy * gv
        dx = r * (dxh - xh * jnp.mean(dxh * xh, axis=-1, keepdims=True))
        dx_ref[...] = dx
        dx16_ref[...] = dx.astype(BF16)
        part = jnp.sum(dy * xh, axis=0, keepdims=True)
        lpart = jnp.broadcast_to(sq, (8, 128))
        i = pl.program_id(0)

        @pl.when(i == 0)
        def _():
            dg_ref[...] = part
            loss_ref[...] = lpart

        @pl.when(i > 0)
        def _():
            dg_ref[...] += part
            loss_ref[...] += lpart

    row = pl.BlockSpec((tr, D), lambda i: (i, 0))
    vec = pl.BlockSpec((1, D), lambda i: (0, 0))
    return pl.pallas_call(
        body, name=name, grid=(S // tr,), in_specs=[row, row, vec],
        out_specs=[row, vec, pl.BlockSpec((8, 128), lambda i: (0, 0)), row],
        out_shape=[jax.ShapeDtypeStruct((S, D), F32), jax.ShapeDtypeStruct((1, D), F32),
                   jax.ShapeDtypeStruct((8, 128), F32), jax.ShapeDtypeStruct((S, D), BF16)],
        compiler_params=_cp("arbitrary"))(x, tgt, g.reshape(1, D))


def _band_specs(nb, col_of):
    prev = pl.BlockSpec((QBLK, GW), lambda r, b: (jnp.maximum(b - 1, 0), col_of(r)))
    cur = pl.BlockSpec((QBLK, GW), lambda r, b: (b, col_of(r)))
    nxt = pl.BlockSpec((QBLK, GW), lambda r, b: (jnp.minimum(b + 1, nb - 1), col_of(r)))
    return [prev, cur, nxt]


HALF = QBLK // 2
WIN = 2 * QBLK


def _cat3(refs, sl):
    prev, cur, nxt = refs
    return jnp.concatenate([prev[HALF:, sl], cur[:, sl], nxt[:HALF, sl]], axis=0)


def _group_view(proj, g):
    _, dil = A_PATTERNS[g]
    S, C = proj.shape
    ng = len(A_PATTERNS)
    if dil == 1:
        return proj, lambda which, r: which * ng + g
    cols = [proj[:, (which * ng + g) * GW:(which * ng + g + 1) * GW] for which in range(3)]
    return jnp.concatenate(cols, axis=1).reshape(S // dil, dil * 3 * GW), lambda which, r: r * 3 + which


def _attn_fwd(name, proj, g):
    window, dil = A_PATTERNS[g]
    n_side = (window // 2) // dil
    S, C = proj.shape
    L = S // dil
    nb = L // QBLK
    pv, col = _group_view(proj, g)

    assert n_side == HALF

    def body(q_ref, kp, kc, kn, vp, vc, vn, o_ref, lse_ref):
        b = pl.program_id(1)
        jq = b * QBLK + lax.broadcasted_iota(jnp.int32, (QBLK, WIN), 0)
        jk = b * QBLK - HALF + lax.broadcasted_iota(jnp.int32, (QBLK, WIN), 1)
        rel = jnp.abs(jk - jq)
        mask = (rel <= n_side) & (jk >= 0) & (jk < L)
        dist = rel.astype(F32) * float(dil)
        for hh in range(HPG):
            sl = slice(hh * HEAD, (hh + 1) * HEAD)
            k = _cat3((kp, kc, kn), sl)
            v = _cat3((vp, vc, vn), sl)
            s = _dot(q_ref[:, sl], k, NT) * SCALE - float(SLOPES[g * HPG + hh]) * dist
            s = jnp.where(mask, s, NEG)
            m = jnp.max(s, axis=1, keepdims=True)
            p = jnp.exp(s - m)
            l = jnp.sum(p, axis=1, keepdims=True)
            o_ref[:, sl] = _dot(p, v) / l
            lse_ref[:, sl] = jnp.broadcast_to(m + jnp.log(l), (QBLK, HEAD))

    q_spec = pl.BlockSpec((QBLK, GW), lambda r, b: (b, col(0, r)))
    k_specs = _band_specs(nb, lambda r: col(1, r))
    v_specs = _band_specs(nb, lambda r: col(2, r))
    o_spec = pl.BlockSpec((QBLK, GW), lambda r, b: (b, r))
    shape = jax.ShapeDtypeStruct((L, dil * GW), F32)
    o, lse = pl.pallas_call(
        body, name=name, grid=(dil, nb), in_specs=[q_spec] + k_specs + v_specs,
        out_specs=[o_spec, o_spec], out_shape=[shape, shape],
        compiler_params=_cp("parallel", "parallel"))(pv, pv, pv, pv, pv, pv, pv)
    return o.reshape(S, GW), lse.reshape(S, GW)


def _attn_merge(name, outs, lses, tr=256):
    S = outs[0].shape[0]
    tr = _tile(S, tr)
    ng = len(outs)

    def body(*refs):
        o_refs, l_refs = refs[:ng], refs[ng:2 * ng]
        tok_ref, lse_ref, tok16_ref = refs[-3:]
        ls = [r[...] for r in l_refs]
        m = functools.reduce(jnp.maximum, ls)
        es = [jnp.exp(l - m) for l in ls]
        tot = functools.reduce(lambda a, b: a + b, es)
        acc = None
        for e, o_ref in zip(es, o_refs):
            term = (e / tot) * o_ref[...]
            acc = term if acc is None else acc + term
        tok_ref[...] = acc
        tok16_ref[...] = acc.astype(BF16)
        lse_ref[...] = m + jnp.log(tot)

    row = pl.BlockSpec((tr, GW), lambda i: (i, 0))
    shape = jax.ShapeDtypeStruct((S, GW), F32)
    return pl.pallas_call(
        body, name=name, grid=(S // tr,), in_specs=[row] * (2 * ng), out_specs=[row, row, row],
        out_shape=[shape, shape, jax.ShapeDtypeStruct((S, GW), BF16)],
        compiler_params=_cp("parallel"))(*outs, *lses)


def _attn_bwd(name, proj, g, dtok_src, dtok_blk, tok, lse):
    window, dil = A_PATTERNS[g]
    n_side = (window // 2) // dil
    S, C = proj.shape
    L = S // dil
    nb = L // QBLK
    pv, col = _group_view(proj, g)
    assert n_side == HALF
    if dil == 1:
        dcb, dv_ = dtok_src.shape[1] // GW, dtok_src
    else:
        dcb, dv_ = 1, dtok_src[:, dtok_blk * GW:(dtok_blk + 1) * GW].reshape(L, dil * GW)
        dtok_blk = 0
    ov = tok.reshape(L, dil * GW)
    lv = lse.reshape(L, dil * GW)

    def body(qp, qc, qn, kp, kc, kn, vp, vc, vn, dop, doc, don, op, oc, on, lp, lc, ln,
             dq_ref, dk_ref, dv_ref):
        b = pl.program_id(1)
        jq = b * QBLK + lax.broadcasted_iota(jnp.int32, (QBLK, WIN), 0)
        jk = b * QBLK - HALF + lax.broadcasted_iota(jnp.int32, (QBLK, WIN), 1)
        rel = jnp.abs(jk - jq)
        mask = (rel <= n_side) & (jk >= 0) & (jk < L)
        dist = rel.astype(F32) * float(dil)
        jq3 = b * QBLK - HALF + lax.broadcasted_iota(jnp.int32, (WIN, QBLK), 0)
        jk1 = b * QBLK + lax.broadcasted_iota(jnp.int32, (WIN, QBLK), 1)
        rel3 = jnp.abs(jk1 - jq3)
        mask3 = (rel3 <= n_side) & (jq3 >= 0) & (jq3 < L)
        dist3 = rel3.astype(F32) * float(dil)
        for hh in range(HPG):
            sl = slice(hh * HEAD, (hh + 1) * HEAD)
            one = slice(hh * HEAD, hh * HEAD + 1)
            slope = float(SLOPES[g * HPG + hh])
            q = qc[:, sl]
            do = doc[:, sl]
            k3 = _cat3((kp, kc, kn), sl)
            v3 = _cat3((vp, vc, vn), sl)
            delta = jnp.sum(do * oc[:, sl], axis=1, keepdims=True)
            s = _dot(q, k3, NT) * SCALE - slope * dist
            p = jnp.where(mask, jnp.exp(s - lc[:, one]), 0.0)
            ds = p * (_dot(do, v3, NT) - delta)
            dq_ref[:, sl] = (_dot(ds, k3) * SCALE).astype(dq_ref.dtype)

            q3 = _cat3((qp, qc, qn), sl)
            do3 = _cat3((dop, doc, don), sl)
            o3 = _cat3((op, oc, on), sl)
            lse3 = _cat3((lp, lc, ln), sl)[:, :1]
            delta3 = jnp.sum(do3 * o3, axis=1, keepdims=True)
            k = kc[:, sl]
            v = vc[:, sl]
            s3 = _dot(q3, k, NT) * SCALE - slope * dist3
            p3 = jnp.where(mask3, jnp.exp(s3 - lse3), 0.0)
            ds3 = p3 * (_dot(do3, v, NT) - delta3)
            dv_ref[:, sl] = _dot(p3, do3, TN).astype(dv_ref.dtype)
            dk_ref[:, sl] = (_dot(ds3, q3, TN) * SCALE).astype(dk_ref.dtype)

    specs = (_band_specs(nb, lambda r: col(0, r)) + _band_specs(nb, lambda r: col(1, r))
             + _band_specs(nb, lambda r: col(2, r))
             + _band_specs(nb, lambda r: r * dcb + dtok_blk)
             + _band_specs(nb, lambda r: r) + _band_specs(nb, lambda r: r))
    o_spec = pl.BlockSpec((QBLK, GW), lambda r, b: (b, r))
    shape = jax.ShapeDtypeStruct((L, dil * GW), BF16)
    outs = pl.pallas_call(
        body, name=name, grid=(dil, nb), in_specs=specs, out_specs=[o_spec] * 3, out_shape=[shape] * 3,
        compiler_params=_cp("parallel", "parallel"))(*([pv] * 9 + [dv_] * 3 + [ov] * 3 + [lv] * 3))
    return [o.reshape(S, GW) for o in outs]


def _mem_fwd(name, proj, q_blk, kv, tq=256):
    S = proj.shape[0]
    M = kv.shape[0]
    tq = _tile(S, tq)

    def body(q_ref, kv_ref, o_ref):
        for hh in range(HPG):
            sl = slice(hh * HEAD, (hh + 1) * HEAD)
            k = kv_ref[:, sl]
            v = kv_ref[:, GW + hh * HEAD:GW + (hh + 1) * HEAD]
            s = _dot(q_ref[:, sl], k, NT) * SCALE
            m = jnp.max(s, axis=1, keepdims=True)
            p = jnp.exp(s - m)
            p = p / jnp.sum(p, axis=1, keepdims=True)
            o_ref[:, sl] = _dot(p, v).astype(o_ref.dtype)

    return pl.pallas_call(
        body, name=name, grid=(S // tq,),
        in_specs=[pl.BlockSpec((tq, GW), lambda i: (i, q_blk)), pl.BlockSpec((M, 2 * GW), lambda i: (0, 0))],
        out_specs=pl.BlockSpec((tq, GW), lambda i: (i, 0)),
        out_shape=jax.ShapeDtypeStruct((S, GW), BF16), compiler_params=_cp("parallel"))(proj, kv)


def _mem_bwd(name, proj, q_blk, kv, dcat, do_blk, tq=256, deps=()):
    S = proj.shape[0]
    M = kv.shape[0]
    tq = _tile(S, tq)

    def body(q_ref, kv_ref, do_ref, *rest):
        dq_ref, dkv_ref = rest[-2:]
        i = pl.program_id(0)
        for hh in range(HPG):
            sl = slice(hh * HEAD, (hh + 1) * HEAD)
            vsl = slice(GW + hh * HEAD, GW + (hh + 1) * HEAD)
            q = q_ref[:, sl]
            do = do_ref[:, sl]
            k = kv_ref[:, sl]
            v = kv_ref[:, vsl]
            s = _dot(q, k, NT) * SCALE
            m = jnp.max(s, axis=1, keepdims=True)
            p = jnp.exp(s - m)
            p = p / jnp.sum(p, axis=1, keepdims=True)
            dp = _dot(do, v, NT)
            ds = p * (dp - jnp.sum(dp * p, axis=1, keepdims=True))
            dq_ref[:, sl] = (_dot(ds, k) * SCALE).astype(dq_ref.dtype)
            dk = _dot(ds, q, TN) * SCALE
            dvv = _dot(p, do, TN)

            @pl.when(i == 0)
            def _():
                dkv_ref[:, sl] = dk
                dkv_ref[:, vsl] = dvv

            @pl.when(i > 0)
            def _():
                dkv_ref[:, sl] += dk
                dkv_ref[:, vsl] += dvv

    return pl.pallas_call(
        body, name=name, grid=(S // tq,),
        in_specs=[pl.BlockSpec((tq, GW), lambda i: (i, q_blk)), pl.BlockSpec((M, 2 * GW), lambda i: (0, 0)),
                  pl.BlockSpec((tq, GW), lambda i: (i, do_blk))] + [_DEP] * len(deps),
        out_specs=[pl.BlockSpec((tq, GW), lambda i: (i, 0)), pl.BlockSpec((M, 2 * GW), lambda i: (0, 0))],
        out_shape=[jax.ShapeDtypeStruct((S, GW), BF16), jax.ShapeDtypeStruct((M, 2 * GW), F32)],
        compiler_params=_cp("arbitrary"))(proj, kv, dcat, *deps)


_RSQRT2 = float(1.0 / np.sqrt(2.0))
_RSQRT2PI = float(1.0 / np.sqrt(2.0 * np.pi))


def _gelu(x):
    return 0.5 * x * (1.0 + lax.erf(x * _RSQRT2))


def _gelu_and_grad(x):
    cdf = 0.5 * (1.0 + lax.erf(x * _RSQRT2))
    return x * cdf, cdf + x * jnp.exp(-0.5 * x * x) * _RSQRT2PI


def _sgu_fwd(name, proj, gv, w_s, bias_t):
    S = proj.shape[0]
    nch = S // HEAD

    def body(u_ref, v_ref, gv_ref, ws_ref, b_ref, o_ref):
        v = _gelu(v_ref[...])
        r = lax.rsqrt(jnp.mean(v * v, axis=-1, keepdims=True) + EPS)
        vn = v * r * gv_ref[...]
        for gg in range(B_GROUPS):
            sl = slice(gg * HEAD, (gg + 1) * HEAD)
            mixed = _dot(ws_ref[gg], vn[:, sl]) + b_ref[:, gg:gg + 1]
            o_ref[:, sl] = (_gelu(u_ref[:, sl]) * mixed).astype(o_ref.dtype)

    return pl.pallas_call(
        body, name=name, grid=(nch,),
        in_specs=[pl.BlockSpec((HEAD, B_W), lambda c: (c, 0)), pl.BlockSpec((HEAD, B_W), lambda c: (c, 1)),
                  pl.BlockSpec((1, B_W), lambda c: (0, 0)),
                  pl.BlockSpec((B_GROUPS, HEAD, HEAD), lambda c: (0, 0, 0)),
                  pl.BlockSpec((HEAD, B_GROUPS), lambda c: (0, 0))],
        out_specs=pl.BlockSpec((HEAD, B_W), lambda c: (c, 0)),
        out_shape=jax.ShapeDtypeStruct((S, B_W), BF16),
        compiler_params=_cp("parallel"))(proj, proj, gv.reshape(1, B_W), w_s, bias_t)


def _sgu_bwd(name, proj, gv, w_s, bias_t, dcat):
    S = proj.shape[0]
    nch = S // HEAD

    def body(u_ref, v_ref, gv_ref, ws_ref, b_ref, dt_ref, du_ref, dvp_ref, dgv_ref, dws_ref, db_ref, dvn_ref):
        c = pl.program_id(0)
        vpre = v_ref[...]
        v, v_slope = _gelu_and_grad(vpre)
        r = lax.rsqrt(jnp.mean(v * v, axis=-1, keepdims=True) + EPS)
        vh = v * r
        gvv = gv_ref[...]
        vn = vh * gvv
        for gg in range(B_GROUPS):
            sl = slice(gg * HEAD, (gg + 1) * HEAD)
            upre = u_ref[:, sl]
            dt = dt_ref[:, sl]
            vng = vn[:, sl]
            mixed = _dot(ws_ref[gg], vng) + b_ref[:, gg:gg + 1]
            u, u_slope = _gelu_and_grad(upre)
            du_ref[:, sl] = (dt * mixed * u_slope).astype(du_ref.dtype)
            dmix = dt * u
            dvn_ref[:, sl] = _dot(ws_ref[gg], dmix, TN)
            dws = _dot(dmix, vng, NT)
            dbs = jnp.sum(dmix, axis=1, keepdims=True)

            @pl.when(c == 0)
            def _():
                dws_ref[gg] = dws
                db_ref[:, gg:gg + 1] = dbs

            @pl.when(c > 0)
            def _():
                dws_ref[gg] += dws
                db_ref[:, gg:gg + 1] += dbs

        dvn = dvn_ref[...]
        dgp = jnp.sum(dvn * vh, axis=0, keepdims=True)
        dvh = dvn * gvv
        dv = r * (dvh - vh * jnp.mean(dvh * vh, axis=-1, keepdims=True))
        dvp_ref[...] = (dv * v_slope).astype(dvp_ref.dtype)

        @pl.when(c == 0)
        def _():
            dgv_ref[...] = dgp

        @pl.when(c > 0)
        def _():
            dgv_ref[...] += dgp

    blk = lambda j: pl.BlockSpec((HEAD, B_W), lambda c: (c, j))
    vec = pl.BlockSpec((1, B_W), lambda c: (0, 0))
    ws_spec = pl.BlockSpec((B_GROUPS, HEAD, HEAD), lambda c: (0, 0, 0))
    b_spec = pl.BlockSpec((HEAD, B_GROUPS), lambda c: (0, 0))
    du, dvp, dgv, dws, db = pl.pallas_call(
        body, name=name, grid=(nch,),
        in_specs=[blk(0), blk(1), vec, ws_spec, b_spec, blk(0)],
        out_specs=[blk(0), blk(0), vec, ws_spec, b_spec],
        out_shape=[jax.ShapeDtypeStruct((S, B_W), BF16), jax.ShapeDtypeStruct((S, B_W), BF16),
                   jax.ShapeDtypeStruct((1, B_W), F32), jax.ShapeDtypeStruct((B_GROUPS, HEAD, HEAD), F32),
                   jax.ShapeDtypeStruct((HEAD, B_GROUPS), F32)],
        scratch_shapes=[pltpu.VMEM((HEAD, B_W), F32)],
        compiler_params=_cp("arbitrary"))(proj, proj, gv.reshape(1, B_W), w_s, bias_t, dcat)
    return du, dvp, dgv, dws, db


def _shift_down(a, row):
    return jnp.where(row == 0, 0.0, pltpu.roll(a, 1, 0))


def _shift_up(a, row):
    n = a.shape[0]
    return jnp.where(row == n - 1, 0.0, pltpu.roll(a, n - 1, 0))


def _conv(a, w, b, row):
    return _shift_down(a, row) * w[0:1] + a * w[1:2] + _shift_up(a, row) * w[2:3] + b


def _conv_fwd(name, a3, cw, cb, tc=256):
    _, S, FF = a3.shape
    tc = _tile(FF, tc)

    def body(a_ref, w_ref, b_ref, o_ref, c_ref):
        row = lax.broadcasted_iota(jnp.int32, (S, tc), 0)
        cg = _conv(a_ref[0], w_ref[0], b_ref[0], row)
        cv = _conv(a_ref[1], w_ref[1], b_ref[1], row)
        c_ref[0] = cg
        c_ref[1] = cv
        o_ref[...] = (_gelu(cg) * cv).astype(o_ref.dtype)

    a_spec = pl.BlockSpec((2, S, tc), lambda j: (0, 0, j))
    return pl.pallas_call(
        body, name=name, grid=(FF // tc,),
        in_specs=[a_spec, pl.BlockSpec((2, 3, tc), lambda j: (0, 0, j)), pl.BlockSpec((2, 1, tc), lambda j: (0, 0, j))],
        out_specs=[pl.BlockSpec((S, tc), lambda j: (0, j)), a_spec],
        out_shape=[jax.ShapeDtypeStruct((S, FF), BF16), jax.ShapeDtypeStruct((2, S, FF), F32)],
        compiler_params=_cp("parallel"))(a3, cw, cb)


def _conv_bwd(name, a3, c3, cw, dact, tc=128):
    _, S, FF = a3.shape
    tc = _tile(FF, tc)

    def body(a_ref, c_ref, w_ref, d_ref, da_ref, dw_ref, db_ref):
        row = lax.broadcasted_iota(jnp.int32, (S, tc), 0)
        ag, av = a_ref[0], a_ref[1]
        wg, wv = w_ref[0], w_ref[1]
        cg, cv = c_ref[0], c_ref[1]
        d = d_ref[...]
        gate, gate_slope = _gelu_and_grad(cg)
        dcs = (d * cv * gate_slope, d * gate)
        for h, (dc, a, w) in enumerate(zip(dcs, (ag, av), (wg, wv))):
            da = _shift_up(dc, row) * w[0:1] + dc * w[1:2] + _shift_down(dc, row) * w[2:3]
            da_ref[h] = da.astype(da_ref.dtype)
            dw_ref[h, 0:1, :] = jnp.sum(dc * _shift_down(a, row), axis=0, keepdims=True)
            dw_ref[h, 1:2, :] = jnp.sum(dc * a, axis=0, keepdims=True)
            dw_ref[h, 2:3, :] = jnp.sum(dc * _shift_up(a, row), axis=0, keepdims=True)
            db_ref[h] = jnp.sum(dc, axis=0, keepdims=True)

    a_spec = pl.BlockSpec((2, S, tc), lambda j: (0, 0, j))
    w_spec = pl.BlockSpec((2, 3, tc), lambda j: (0, 0, j))
    b_spec = pl.BlockSpec((2, 1, tc), lambda j: (0, 0, j))
    return pl.pallas_call(
        body, name=name, grid=(FF // tc,),
        in_specs=[a_spec, a_spec, w_spec, pl.BlockSpec((S, tc), lambda j: (0, j))],
        out_specs=[a_spec, w_spec, b_spec],
        out_shape=[jax.ShapeDtypeStruct((2, S, FF), BF16), jax.ShapeDtypeStruct((2, 3, FF), F32),
                   jax.ShapeDtypeStruct((2, 1, FF), F32)],
        compiler_params=_cp("parallel"))(a3, c3, cw, dact)


_HBM = pl.BlockSpec(memory_space=pltpu.HBM)


def _position():
    return lax.axis_index("x"), lax.axis_index("y"), lax.axis_index("c")


_SEM =pl.BlockSpec(memory_space=pltpu.SEMAPHORE)
_EFFECT = pltpu.SideEffectType.DATAFLOW_SIDE_EFFECTING
_FLIPS = ((1, 0), (0, 1), (1, 1))


def _split_start(name, bufs, ncopy, plan, after=()):
    n = len(bufs)
    after = list(after)

    def body(*refs):
        ins = refs[:n]
        send_sems, recv_sems, token = refs[n + len(after)], refs[n + len(after) + 1], refs[-1]
        for i, (src, dst, to) in enumerate(plan(ins)):
            pltpu.make_async_remote_copy(src_ref=src, dst_ref=dst, send_sem=send_sems.at[i], recv_sem=recv_sems.at[i],
                                         device_id=to, device_id_type=MESH).start()
        token[...] = jnp.zeros_like(token)

    outs = pl.pallas_call(
        body, name=name,
        out_shape=(pltpu.SemaphoreType.DMA((ncopy,)), pltpu.SemaphoreType.DMA((ncopy,)),
                   *[pltpu.HBM(b.shape, b.dtype) for b in bufs], jax.ShapeDtypeStruct((8, 128), F32)),
        in_specs=[_HBM] * n + [pl.BlockSpec(memory_space=pl.ANY)] * len(after),
        out_specs=(_SEM, _SEM, *([_HBM] * n), pl.BlockSpec(memory_space=pltpu.VMEM)),
        input_output_aliases={i: 2 + i for i in range(n)},
        compiler_params=pltpu.CompilerParams(has_side_effects=_EFFECT),
    )(*[pltpu.with_memory_space_constraint(b, pltpu.HBM) for b in bufs], *after)
    return outs[0], outs[1], list(outs[2:2 + n]), outs[-1]


def _split_wait(name, bufs, send_sems, recv_sems, plan, after):
    n = len(bufs)
    after = list(after)

    def body(*refs):
        ins = refs[:n]
        ssem, rsem = refs[n], refs[n + 1]
        for i, (src, dst, to) in enumerate(plan(ins)):
            cp = pltpu.make_async_remote_copy(src_ref=src, dst_ref=dst, send_sem=ssem.at[i], recv_sem=rsem.at[i],
                                              device_id=to, device_id_type=MESH)
            cp.wait_send()
            cp.wait_recv()

    outs = pl.pallas_call(
        body, name=name, out_shape=tuple(pltpu.HBM(b.shape, b.dtype) for b in bufs),
        in_specs=[_HBM] * n + [_SEM, _SEM] + [pl.BlockSpec(memory_space=pl.ANY)] * len(after),
        out_specs=tuple([_HBM] * n), input_output_aliases={i: i for i in range(n)},
        compiler_params=pltpu.CompilerParams(has_side_effects=_EFFECT),
    )(*bufs, send_sems, recv_sems, *after)
    return list(outs)


def _gather_plan(refs):
    px, py, pc = _position()
    me = 4 * px + 2 * py + pc
    targets = [(px, py, 1 - pc), (1 - px, py, pc), (px, 1 - py, pc), (1 - px, 1 - py, pc)]
    return [(r.at[me], r.at[me], to) for r in refs for to in targets]


def _forward_plan(refs):
    px, py, pc = _position()
    out = []
    for r in refs:
        for fx, fy in _FLIPS:
            slot = 4 * (1 - px if fx else px) + 2 * (1 - py if fy else py) + pc
            out.append((r.at[slot], r.at[slot], (px, py, 1 - pc)))
    return out


def _pair_plan(n):
    def plan(refs):
        px, py, pc = _position()
        return [(refs[w].at[2 * k + (1 - pc)], refs[n + w].at[k], (px, py, 1 - pc)) for w in range(n) for k in range(4)]
    return plan


def _chip_plan(n):
    def plan(refs):
        px, py, pc = _position()
        out = []
        for w in range(n):
            for j, (fx, fy) in enumerate(_FLIPS):
                qx = 1 - px if fx else px
                qy = 1 - py if fy else py
                out.append((refs[w].at[2 * qx + qy], refs[n + w].at[j], (qx, qy, pc)))
        return out
    return plan


def _broadcast_plan(refs):
    px, py, pc = _position()
    me = 4 * px + 2 * py + pc
    flips = [(fx, fy, fc) for fx in (0, 1) for fy in (0, 1) for fc in (0, 1)][1:]
    targets = [(1 - px if fx else px, 1 - py if fy else py, 1 - pc if fc else pc) for fx, fy, fc in flips]
    return [(r.at[me], r.at[me], to) for r in refs for to in targets]


def _cast_place(name, dev, w, layer, dtype=BF16, deps=()):
    nl, R, C = w.shape
    tr = _row_tile(R, C, 4)

    def body(dev_ref, w_ref, *rest):
        o_ref = rest[-1]
        o_ref[...] = w_ref[...].astype(o_ref.dtype)

    return pl.pallas_call(
        body, name=name,
        grid_spec=pltpu.PrefetchScalarGridSpec(
            num_scalar_prefetch=1, grid=(R // tr,),
            in_specs=[pl.BlockSpec((None, tr, C), lambda i, d: (layer, i, 0))] + [_DEP] * len(deps),
            out_specs=pl.BlockSpec((None, tr, C), lambda i, d: (d[0], i, 0))),
        out_shape=jax.ShapeDtypeStruct((N_DEV, R, C), dtype), compiler_params=_cp("parallel"))(dev, w, *deps)


def _pair_sum(name, core, dw, recv):
    _, R, C = dw.shape
    tr = _row_tile(R, C, 4)
    dw4 = dw.reshape(4, 2, R, C)

    def body(core_ref, a_ref, b_ref, o_ref):
        o_ref[...] = (a_ref[...].astype(F32) + b_ref[...].astype(F32)).astype(o_ref.dtype)

    return pl.pallas_call(
        body, name=name,
        grid_spec=pltpu.PrefetchScalarGridSpec(
            num_scalar_prefetch=1, grid=(4, R // tr),
            in_specs=[pl.BlockSpec((None, None, tr, C), lambda k, i, c_ref: (k, c_ref[0], i, 0)),
                      pl.BlockSpec((None, tr, C), lambda k, i, c_ref: (k, i, 0))],
            out_specs=pl.BlockSpec((None, tr, C), lambda k, i, c_ref: (k, i, 0))),
        out_shape=jax.ShapeDtypeStruct((4, R, C), BF16),
        compiler_params=_cp("parallel", "parallel"))(core, dw4, recv)


def _adamw_math(w, g, m, v):
    m = ADAM_B1 * m + (1.0 - ADAM_B1) * g
    v = ADAM_B2 * v + (1.0 - ADAM_B2) * (g * g)
    m_hat = m / (1.0 - ADAM_B1 ** ADAM_STEP)
    v_hat = v / (1.0 - ADAM_B2 ** ADAM_STEP)
    delta = -ADAM_LR * (m_hat / (jnp.sqrt(v_hat) + ADAM_EPS) + ADAM_WD * w)
    return delta, m, v


def _adamw_shard(name, chip, layer, w, m, v, p, recv, prev, deps=()):
    nl, R, C = w.shape
    tr = _row_tile(R, C, 2)
    n_prev = 0 if prev is None else 4

    def body(chip_ref, w_ref, m_ref, v_ref, p_ref, r_ref, *rest):
        g_ref, d_ref, nm_ref, nv_ref, tok_ref = rest[-5:]
        tok_ref[...] = jnp.zeros_like(tok_ref)
        g = p_ref[...].astype(F32)
        for j in range(3):
            g = g + r_ref[j].astype(F32)
        delta, nm, nv = _adamw_math(w_ref[...], g, m_ref[...], v_ref[...])
        g_ref[...] = g
        d_ref[...] = delta
        nm_ref[...] = nm
        nv_ref[...] = nv

    lay = pl.BlockSpec((None, tr, C), lambda i, c_ref: (layer, i, 0))
    in_specs = [lay, lay, lay,
                pl.BlockSpec((None, tr, C), lambda i, c_ref: (c_ref[0], i, 0)),
                pl.BlockSpec((3, tr, C), lambda i, c_ref: (0, i, 0))]
    in_specs += [pl.BlockSpec(memory_space=pl.ANY)] * n_prev + [_DEP] * len(deps)
    shape = jax.ShapeDtypeStruct((nl, R, C), F32)
    ins = [chip, w, m, v, p, recv] + ([] if prev is None else list(prev)) + list(deps)
    return pl.pallas_call(
        body, name=name,
        grid_spec=pltpu.PrefetchScalarGridSpec(
            num_scalar_prefetch=1, grid=(R // tr,), in_specs=in_specs, out_specs=[lay] * 4 + [_DEP]),
        out_shape=[shape] * 4 + [jax.ShapeDtypeStruct((8, 128), F32)],
        input_output_aliases={6 + j: j for j in range(n_prev)},
        compiler_params=_cp("arbitrary"))(*ins)


def _sum_slots(name, parts, tr=512):
    n, R, C = parts.shape
    tr = _tile(R, tr)

    def body(p_ref, o_ref):
        acc = p_ref[0]
        for j in range(1, n):
            acc = acc + p_ref[j]
        o_ref[...] = acc

    return pl.pallas_call(
        body, name=name, grid=(R // tr,),
        in_specs=[pl.BlockSpec((n, tr, C), lambda i: (0, i, 0))],
        out_specs=pl.BlockSpec((tr, C), lambda i: (i, 0)),
        out_shape=jax.ShapeDtypeStruct((R, C), F32), compiler_params=_cp("parallel"))(parts)


def _adamw_flat(name, w, g, m, v, tr=512):
    R, C = w.shape
    tr = _tile(R, tr)

    def body(w_ref, g_ref, m_ref, v_ref, d_ref, nm_ref, nv_ref):
        delta, nm, nv = _adamw_math(w_ref[...], g_ref[...], m_ref[...], v_ref[...])
        d_ref[...] = delta
        nm_ref[...] = nm
        nv_ref[...] = nv

    row = pl.BlockSpec((tr, C), lambda i: (i, 0))
    shape = jax.ShapeDtypeStruct((R, C), F32)
    return pl.pallas_call(
        body, name=name, grid=(R // tr,), in_specs=[row] * 4, out_specs=[row] * 3, out_shape=[shape] * 3,
        compiler_params=_cp("parallel"))(w, g, m, v)


_PACK_ROWS = 512


def _pack(arrs):
    flat = jnp.concatenate([a.reshape(-1) for a in arrs])
    unit = _PACK_ROWS * 128
    pad = (-flat.shape[0]) % unit
    return jnp.pad(flat, (0, pad)).reshape(-1, 128)


def _unpack(packed, shapes):
    flat = packed.reshape(-1)
    outs, off = [], 0
    for s in shapes:
        n = int(np.prod(s))
        outs.append(flat[off:off + n].reshape(s))
        off += n
    return outs


def kernel(x, mem, mix_norm_g, ffn_norm_g, mem_norm_g, w_mem_kv, a_w_in, a_w_out, b_w_in, b_v_norm_g, b_w_s, b_s_bias, b_w_out, ffn_w_up, ffn_conv_w, ffn_conv_b, ffn_w_down, final_norm_g, loss_target, m_mix_norm_g, m_ffn_norm_g, m_mem_norm_g, m_w_mem_kv, m_a_w_in, m_a_w_out, m_b_w_in, m_b_v_norm_g, m_b_w_s, m_b_s_bias, m_b_w_out, m_ffn_w_up, m_ffn_conv_w, m_ffn_conv_b, m_ffn_w_down, m_final_norm_g, v_mix_norm_g, v_ffn_norm_g, v_mem_norm_g, v_w_mem_kv, v_a_w_in, v_a_w_out, v_b_w_in, v_b_v_norm_g, v_b_w_s, v_b_s_bias, v_b_w_out, v_ffn_w_up, v_ffn_conv_w, v_ffn_conv_b, v_ffn_w_down, v_final_norm_g):
    px, py, pc = _position()
    dev = 4 * px + 2 * py + pc
    core = jnp.reshape(pc, (1,)).astype(jnp.int32)
    chip = jnp.reshape(2 * px + py, (1,)).astype(jnp.int32)

    x0 = x[0]
    mem0 = mem[0]
    tgt = loss_target[0]
    S, D = x0.shape
    depth = mix_norm_g.shape[0]
    FF = ffn_w_down.shape[1] * N_DEV
    a_in = a_w_in.shape[2] * N_DEV
    b_in = b_w_in.shape[2] * N_DEV
    a_q_blk = (a_in - GW) // GW
    b_q_blk = (b_in - GW) // GW

    stacks = {"kv": (w_mem_kv, m_w_mem_kv, v_w_mem_kv), "ain": (a_w_in, m_a_w_in, v_a_w_in),
              "aout": (a_w_out, m_a_w_out, v_a_w_out), "bin": (b_w_in, m_b_w_in, v_b_w_in),
              "bout": (b_w_out, m_b_w_out, v_b_w_out), "up": (ffn_w_up, m_ffn_w_up, v_ffn_w_up),
              "down": (ffn_w_down, m_ffn_w_down, v_ffn_w_down)}
    dev1 = jnp.reshape(dev, (1,)).astype(jnp.int32)

    def groups_of(i):
        j = i // 2
        mix = [("kv", i), ("ain", j), ("aout", j)] if i % 2 == 0 else [("kv", i), ("bin", j), ("bout", j)]
        return mix, [("up", i), ("down", i)]

    gather_groups = [(f"{half}{i}", members) for i in range(depth) for half, members in zip("mf", groups_of(i))]
    gather_ahead = 2
    in_flight = {}

    def gather_start(k, after):
        gname, members = gather_groups[k]
        deps = [start_tokens[-1]] if k else []
        lands = [_cast_place(f"place_{t}{l}", dev1, stacks[t][0], l, deps=deps) for t, l in members]
        ssem, rsem, lands, tok = _split_start(f"ag_start_{gname}", lands, 4 * len(lands), _gather_plan, after)
        in_flight[k] = (lands, ssem, rsem)
        return tok

    small_land = _cast_place("place_small_w", dev1, _pack([ffn_conv_w, b_v_norm_g])[None], 0, F32)
    small_ssem, small_rsem, small_lands, small_tok = _split_start("smallw_start", [small_land], N_DEV - 1,
                                                                  _broadcast_plan)
    start_tokens = [small_tok, gather_start(0, [small_tok])]
    passing = {}

    def gather_arrive(k, after):
        if k >= len(gather_groups):
            return []
        gname, members = gather_groups[k]
        lands, ssem, rsem = in_flight.pop(k)
        lands = _split_wait(f"ag_wait_{gname}", lands, ssem, rsem, _gather_plan, after)
        toks = []
        for q in (range(1, 1 + gather_ahead) if k == 0 else [k + gather_ahead]):
            if q < len(gather_groups):
                toks.append(gather_start(q, [lands[0]] + toks))
        ssem, rsem, lands, tok = _split_start(f"ag_pass_{gname}", lands, 3 * len(lands), _forward_plan, toks)
        passing[k] = (lands, ssem, rsem)
        return toks + [tok]

    def gather_ready(k, after):
        gname, members = gather_groups[k]
        lands, ssem, rsem = passing.pop(k)
        lands = _split_wait(f"ag_ready_{gname}", lands, ssem, rsem, _forward_plan, after)
        out = {}
        for (t, l), land in zip(members, lands):
            if t == "kv":
                out["kv"] = land.reshape(D, 2 * GW)
            elif t in ("ain", "aout", "up"):
                out[{"ain": "in", "aout": "out", "up": "up"}[t]] = land
            elif t == "bin":
                out["in"] = jnp.transpose(land, (1, 0, 2)).reshape(D, b_in)
            elif t == "bout":
                out["out"] = land.reshape(B_W + GW, D)
            else:
                out["down"] = land.reshape(FF, D)
        return out

    def small_weights(after):
        (small_all,) = _split_wait("smallw_wait", small_lands, small_ssem, small_rsem, _broadcast_plan, after)
        cw_parts, gv_parts = [], []
        for d in range(N_DEV):
            cw_d, gv_d = _unpack(small_all[d], [ffn_conv_w.shape, b_v_norm_g.shape])
            cw_parts.append(cw_d)
            gv_parts.append(gv_d)
        return jnp.concatenate(cw_parts, axis=-1), jnp.concatenate(gv_parts, axis=-1)

    def conv_params(i):
        cw = conv_w_full[i].reshape(3, 2, FF).transpose(1, 0, 2)
        cb = ffn_conv_b[i].reshape(2, 1, FF)
        return cw, cb

    saved = []
    W = []
    xc = x0
    toks = start_tokens + gather_arrive(0, [x0])
    for i in range(depth):
        j = i // 2
        lw = gather_ready(2 * i, [xc])
        sv = {"x0": xc}
        h1, h1t = _rms_fwd(f"mixnorm{i}", xc, mix_norm_g[i], deps=toks, with_t=True)
        memn = _rms_fwd(f"memnorm{i}", mem0, mem_norm_g[i])
        if i % 2 == 0:
            proj = _mm_gcols(f"ain{i}", h1, lw["in"], out_dtype=BF16)
        else:
            proj = _mm_full(f"bin{i}", h1, lw["in"])
        kv = _mm_full(f"kvproj{i}", memn, lw["kv"])
        if i % 2 == 0:
            outs, lses = [], []
            for g in range(len(A_PATTERNS)):
                o, l = _attn_fwd(f"attn{i}_{g}", proj, g)
                outs.append(o)
                lses.append(l)
            tok, lse, tok16 = _attn_merge(f"merge{i}", outs, lses)
            sv.update(tok=tok, lse=lse)
        else:
            tok16 = _sgu_fwd(f"sgu{i}", proj, gv_full[j], b_w_s[j], b_s_bias[j].T)
        toks = gather_arrive(2 * i + 1, [tok16]) if i > 0 else []
        mo = _mem_fwd(f"memattn{i}", proj, a_q_blk if i % 2 == 0 else b_q_blk, kv)
        cat = jnp.concatenate([tok16, mo], axis=1)
        if i % 2 == 0:
            x1 = _mm_gcols(f"aout{i}", cat, lw["out"], res=xc, deps=toks)
        else:
            x1 = _mm_full(f"bout{i}", cat, lw["out"], res=xc, deps=toks)
        toks = gather_arrive(1, [x1]) if i == 0 else []
        lw.update(gather_ready(2 * i + 1, [x1]))
        W.append(lw)
        if i == 0:
            conv_w_full, gv_full = small_weights([x1])
        h2, h2t = _rms_fwd(f"ffnnorm{i}", x1, ffn_norm_g[i], deps=toks, with_t=True)
        cw, cb = conv_params(i)
        a3 = _mm_gcols(f"up{i}", h2, lw["up"], split_out=True)
        toks = gather_arrive(2 * i + 2, [a3])
        act, c3 = _conv_fwd(f"conv{i}", a3, cw, cb)
        x2 = _mm_full(f"down{i}", act, lw["down"], res=x1, tn=512, tk=FF // 2, deps=toks)
        toks = []
        sv.update(h1t=h1t, memn=memn, kv=kv, proj=proj, cat=cat, x1=x1, h2t=h2t, a3=a3, c3=c3, act=act)
        saved.append(sv)
        xc = x2

    dx, dg_final, sq, dx16 = _final("final", xc, tgt, final_norm_g)
    loss_local = sq[0, 0] * (0.5 / D)

    chain = {}
    adamw_tokens = []

    def pair_begin(gname, members, dws):
        n = len(dws)
        recvs = [lax.empty((4,) + dw.shape[1:], dw.dtype) for dw in dws]
        ssem, rsem, bufs, tok = _split_start(f"rs_pair_start_{gname}", dws + recvs, 4 * n, _pair_plan(n))
        return dict(name=gname, members=members, n=n, bufs=bufs, sems=(ssem, rsem)), tok

    def pair_end_chip_begin(st, after):
        n, gname = st["n"], st["name"]
        bufs = _split_wait(f"rs_pair_wait_{gname}", st["bufs"], *st["sems"], _pair_plan(n), after)
        ps = [_pair_sum(f"rs_sum_{t}{l}", core, bufs[w], bufs[n + w]) for w, (t, l) in enumerate(st["members"])]
        recvs = [lax.empty((3,) + p.shape[1:], BF16) for p in ps]
        ssem, rsem, bufs, tok = _split_start(f"rs_chip_start_{gname}", ps + recvs, 3 * n, _chip_plan(n))
        return dict(name=gname, members=st["members"], n=n, bufs=bufs, sems=(ssem, rsem)), tok

    def chip_end_update(st, after, deps=()):
        n = st["n"]
        bufs = _split_wait(f"rs_chip_wait_{st['name']}", st["bufs"], *st["sems"], _chip_plan(n), after)
        for w, (t, l) in enumerate(st["members"]):
            wst, mst, vst = stacks[t]
            *chain[t], tok = _adamw_shard(f"adamw_{t}{l}", chip, l, wst, mst, vst, bufs[w], bufs[n + w], chain.get(t),
                                          deps)
            adamw_tokens.append(tok)

    pipe = {"pair": [], "chip": [], "deps": []}

    def take_deps():
        deps, pipe["deps"] = pipe["deps"], []
        return deps

    def submit(gname, members, dws):
        st, tok = pair_begin(gname, members, dws)
        pipe["pair"].append(st)
        pipe["deps"].append(tok)

    def advance(after):
        arrived, pipe["chip"] = pipe["chip"], []
        toks = []
        for st in pipe["pair"]:
            new, tok = pair_end_chip_begin(st, [after])
            pipe["chip"].append(new)
            toks.append(tok)
        pipe["pair"] = []
        done = len(adamw_tokens)
        for st in arrived:
            chip_end_update(st, [after], toks)
        pipe["deps"] += toks + adamw_tokens[done:]

    def small_start(tag, arrs, after):
        land = _cast_place(f"place_small_{tag}", dev1, _pack(arrs)[None], 0, F32)
        ssem, rsem, lands, tok = _split_start(f"small_start_{tag}", [land], N_DEV - 1, _broadcast_plan, after)
        return (lands, ssem, rsem), tok

    def small_end(tag, state, shapes, after):
        lands, ssem, rsem = state
        lands = _split_wait(f"small_wait_{tag}", lands, ssem, rsem, _broadcast_plan, after)
        return _unpack(_sum_slots(f"small_sum_{tag}", lands[0]), shapes)

    def late_small():
        return [dg_mix[0], dg_ffn[0], dg_mem[0], d_conv_b[0][None], d_conv_w[0][None]]

    assert depth >= 2
    big = {k: [None] * n for k, n in (("kv", depth), ("ain", depth // 2 + depth % 2), ("aout", depth // 2 + depth % 2),
                                      ("bin", depth // 2), ("bout", depth // 2), ("up", depth), ("down", depth))}
    dg_mix, dg_ffn, dg_mem = [None] * depth, [None] * depth, [None] * depth
    d_conv_w, d_conv_b = [None] * depth, [None] * depth
    d_gv, d_ws, d_sb = [None] * (depth // 2), [None] * (depth // 2), [None] * (depth // 2)
    for i in reversed(range(depth)):
        j = i // 2
        lw, sv = W[i], saved[i]
        cw, cb = conv_params(i)
        mix_members, ffn_members = groups_of(i)
        if i == 0:
            early_arrays = [loss_local.reshape(1), dg_final.reshape(D), jnp.concatenate(dg_mix[1:]),
                            jnp.concatenate(dg_ffn[1:]), jnp.concatenate(dg_mem[1:]), jnp.stack(d_ws), jnp.stack(d_sb),
                            jnp.stack(d_gv), jnp.stack(d_conv_b[1:]), jnp.stack(d_conv_w[1:])]
            early_state, tok = small_start("early", early_arrays, [dx])
            pipe["deps"].append(tok)
        deps = take_deps()
        dact = _mm_dx_full(f"ddown{i}", dx16, lw["down"], tm=1024, tko=FF // 4, tc=D, deps=deps)
        big["down"][i] = _mm_dw(f"wdown{i}", sv["act"], dx16, deps=deps).reshape(N_DEV, FF // N_DEV, D)
        da3, dcw, dcb = _conv_bwd(f"dconv{i}", sv["a3"], sv["c3"], cw, dact)
        d_conv_w[i] = dcw.transpose(1, 0, 2).reshape(3, 2 * FF)
        d_conv_b[i] = dcb.reshape(2 * FF)
        advance(da3)
        deps = take_deps()
        dh2 = _mm_dx_gcols(f"dup{i}", da3, lw["up"], split_in=True, deps=deps, nsub=2)
        big["up"][i] = _mm_dw_gcols(f"wup{i}", sv["h2t"], da3, N_DEV, split_in=True, deps=deps, a_t=True)
        dx1, dg_ffn[i], dx1_16 = _rms_bwd(f"dffnnorm{i}", dh2, sv["x1"], ffn_norm_g[i], dx)
        submit(f"f{i}", ffn_members, [big["up"][i], big["down"][i]])
        deps = take_deps()
        if i % 2 == 0:
            dcat = _mm_dx_gcols(f"daout{i}", dx1_16, lw["out"], deps=deps, nsub=N_DEV)
            big["aout"][j] = _mm_dw_gcols(f"waout{i}", sv["cat"], dx1_16, N_DEV, deps=deps)
            advance(dcat)
            dqm, dkv = _mem_bwd(f"dmemattn{i}", sv["proj"], a_q_blk, sv["kv"], dcat, 1, deps=take_deps())
            parts = [None] * 9
            for g in range(len(A_PATTERNS)):
                dq, dk, dv = _attn_bwd(f"dattn{i}_{g}", sv["proj"], g, dcat, 0, sv["tok"], sv["lse"])
                parts[g], parts[3 + g], parts[6 + g] = dq, dk, dv
            dproj = jnp.concatenate(parts + [dqm], axis=1)
            deps = []
            dh1 = _mm_dx_gcols(f"dain{i}", dproj, lw["in"], nsub=4)
        else:
            dcat = _mm_dx_full(f"dbout{i}", dx1_16, lw["out"], tm=1024, tko=1024, deps=deps)
            big["bout"][j] = _mm_dw(f"wbout{i}", sv["cat"], dx1_16, deps=deps).reshape(
                N_DEV, (B_W + GW) // N_DEV, D)
            advance(dcat)
            dqm, dkv = _mem_bwd(f"dmemattn{i}", sv["proj"], b_q_blk, sv["kv"], dcat, B_W // GW, deps=take_deps())
            bias_t = b_s_bias[j].T
            du, dvp, dgv, dws, dbt = _sgu_bwd(f"dsgu{i}", sv["proj"], gv_full[j], b_w_s[j], bias_t, dcat)
            d_gv[j], d_ws[j], d_sb[j] = dgv.reshape(B_W), dws, dbt.T
            dproj = jnp.concatenate([du, dvp, dqm], axis=1)
            deps = []
            dh1 = _mm_dx_full(f"dbin{i}", dproj, lw["in"], tm=1024, tko=1024, tc=b_in)
        dmemn = _mm_dx_full(f"dkvproj{i}", dkv, lw["kv"], tko=1024)
        _, dg_mem[i], _ = _rms_bwd(f"dmemnorm{i}", dmemn, mem0, mem_norm_g[i])
        dx, dg_mix[i], dx16 = _rms_bwd(f"dmixnorm{i}", dh1, sv["x0"], mix_norm_g[i], dx1)
        if i == 0:
            late_state, late_tok = small_start("late", late_small(), [dx])
            deps = deps + [late_tok]
        if i % 2 == 0:
            big["ain"][j] = _mm_dw_gcols(f"wain{i}", sv["h1t"], dproj, N_DEV, deps=deps, a_t=True)
        else:
            dwin = _mm_dw(f"wbin{i}", sv["h1t"], dproj, tko=1024, tn=512, deps=deps, a_t=True)
            big["bin"][j] = dwin.reshape(D, N_DEV, b_in // N_DEV).transpose(1, 0, 2)
        big["kv"][i] = _mm_dw(f"wkv{i}", sv["memn"], dkv, tko=1024, deps=deps).reshape(N_DEV, D // N_DEV, 2 * GW)
        submit(f"m{i}", mix_members, [big[t][l] for t, l in mix_members])
    grad_x = dx[None]

    last_chips, toks = [], []
    for st in pipe["pair"]:
        new, tok = pair_end_chip_begin(st, [dx])
        last_chips.append(new)
        toks.append(tok)
    g_early = small_end("early", early_state, [a.shape for a in early_arrays], [dx])
    for st in pipe["chip"]:
        chip_end_update(st, [g_early[1]], toks)
    g_late = small_end("late", late_state, [a.shape for a in late_small()], list(adamw_tokens))

    loss = g_early[0][0]
    layer0 = dict(zip(("mix", "ffn", "mem", "conv_b", "conv_w"), g_late))
    rest = dict(zip(("final", "mix", "ffn", "mem", "w_s", "s_bias", "gv", "conv_b", "conv_w"), g_early[1:]))
    g_cw_full = jnp.concatenate([layer0["conv_w"], rest["conv_w"]])
    g_gv = lax.dynamic_slice_in_dim(rest["gv"], dev * b_v_norm_g.shape[1], b_v_norm_g.shape[1], axis=1)
    g_cw = lax.dynamic_slice_in_dim(g_cw_full, dev * ffn_conv_w.shape[2], ffn_conv_w.shape[2], axis=2)
    g_all = [jnp.concatenate([layer0["mix"], rest["mix"]]), jnp.concatenate([layer0["ffn"], rest["ffn"]]),
             jnp.concatenate([layer0["mem"], rest["mem"]]), rest["w_s"], rest["s_bias"],
             jnp.concatenate([layer0["conv_b"], rest["conv_b"]]), rest["final"], g_gv, g_cw]
    names = ["mix_norm_g", "ffn_norm_g", "mem_norm_g", "b_w_s", "b_s_bias", "ffn_conv_b", "final_norm_g",
             "b_v_norm_g", "ffn_conv_w"]
    ws = [mix_norm_g, ffn_norm_g, mem_norm_g, b_w_s, b_s_bias, ffn_conv_b, final_norm_g, b_v_norm_g, ffn_conv_w]
    ms = [m_mix_norm_g, m_ffn_norm_g, m_mem_norm_g, m_b_w_s, m_b_s_bias, m_ffn_conv_b, m_final_norm_g,
          m_b_v_norm_g, m_ffn_conv_w]
    vs = [v_mix_norm_g, v_ffn_norm_g, v_mem_norm_g, v_b_w_s, v_b_s_bias, v_ffn_conv_b, v_final_norm_g,
          v_b_v_norm_g, v_ffn_conv_w]
    shapes = [w.shape for w in ws]
    d_p, m_p, v_p = _adamw_flat("adamw_small", _pack(ws), _pack(g_all), _pack(ms), _pack(vs))
    res = {}
    for n, g, d, nm, nv in zip(names, g_all, _unpack(d_p, shapes), _unpack(m_p, shapes), _unpack(v_p, shapes)):
        res[n] = [g, d, nm, nv]
    for st in last_chips:
        chip_end_update(st, [d_p] + list(adamw_tokens))
    for tag, name in (("kv", "w_mem_kv"), ("ain", "a_w_in"), ("aout", "a_w_out"), ("bin", "b_w_in"),
                      ("bout", "b_w_out"), ("up", "ffn_w_up"), ("down", "ffn_w_down")):
        res[name] = list(chain[tag])

    order = ["mix_norm_g", "ffn_norm_g", "mem_norm_g", "w_mem_kv", "a_w_in", "a_w_out", "b_w_in", "b_v_norm_g",
             "b_w_s", "b_s_bias", "b_w_out", "ffn_w_up", "ffn_conv_w", "ffn_conv_b", "ffn_w_down", "final_norm_g"]
    return (loss, grad_x, *[res[n][0] for n in order], *[res[n][1] for n in order],
            *[res[n][2] for n in order], *[res[n][3] for n in order])
```
